```python
import jax, jax.numpy as jnp
from jax import lax
import numpy as np

D_MODEL = 1024
BATCH = 8
SEQ = 16384
DEPTH = 1

PLE_DIM = 256
POOL_GROUPS = 4
POOL_WIDTH = D_MODEL // 2
POOL_GROUP_DIM = POOL_WIDTH // POOL_GROUPS
POOL_WINDOWS = (2, 4, 8, 16)
MAX_WINDOW = 16
LRU_WIDTH = D_MODEL
LRU_HEADS = 8
LRU_HEAD_DIM = LRU_WIDTH // LRU_HEADS
CONV_WIDTH = 4
LRU_C = 8.0
N_BRANCHES = 2
D_FF = ((8 * D_MODEL // 3 + 255) // 256) * 256
RMS_EPS = 1e-6
IN_COLS = POOL_WIDTH + 2 * LRU_WIDTH + N_BRANCHES * D_MODEL

kernel_name = "hybrid_pool_rglru_gated_block"


def rms_norm(x, g):
    xf = x.astype(jnp.float32)
    return xf * lax.rsqrt(jnp.mean(xf * xf, axis=-1, keepdims=True) + RMS_EPS) * g.astype(jnp.float32)


def causal_multiscale_pool(z, w_grp, scale):
    B, S, _ = z.shape
    zf = z.astype(jnp.float32)
    csum = jnp.cumsum(zf, axis=1)
    csum_pad = jnp.pad(csum, ((0, 0), (MAX_WINDOW, 0), (0, 0)))
    pos = jnp.arange(S)
    outs = []
    for g, w in enumerate(POOL_WINDOWS):
        lo, hi = g * POOL_GROUP_DIM, (g + 1) * POOL_GROUP_DIM
        cur = csum[:, :, lo:hi]
        prev = csum_pad[:, MAX_WINDOW - w:MAX_WINDOW - w + S, lo:hi]
        count = jnp.minimum(pos + 1, w).astype(jnp.float32)[None, :, None]
        outs.append((cur - prev) / count - zf[:, :, lo:hi])
    pooled = jnp.stack(outs, axis=2)
    mixed = jnp.einsum('bsgc,gcd->bsgd', pooled, w_grp.astype(jnp.float32))
    return mixed.reshape(B, S, POOL_WIDTH) * scale.astype(jnp.float32)


def _linear_recurrence_combine(earlier, later):
    a1, b1 = earlier
    a2, b2 = later
    return a1 * a2, a2 * b1 + b2


def rglru_branch(z_x, z_g, conv_w, conv_b, w_rg, b_rg, w_ig, b_ig, lam):
    B, S, _ = z_x.shape
    f32 = jnp.float32
    xc = lax.conv_general_dilated(
        z_x.astype(f32), conv_w.astype(f32)[:, None, :], window_strides=(1,),
        padding=[(CONV_WIDTH - 1, 0)], dimension_numbers=('NWC', 'WIO', 'NWC'),
        feature_group_count=LRU_WIDTH) + conv_b.astype(f32)
    xh = xc.reshape(B, S, LRU_HEADS, LRU_HEAD_DIM)
    r = jax.nn.sigmoid(jnp.einsum('bshi,hij->bshj', xh, w_rg.astype(f32)) + b_rg.astype(f32)).reshape(B, S, LRU_WIDTH)
    ig = jax.nn.sigmoid(jnp.einsum('bshi,hij->bshj', xh, w_ig.astype(f32)) + b_ig.astype(f32)).reshape(B, S, LRU_WIDTH)
    log_a = -LRU_C * r * jax.nn.softplus(-lam.astype(f32))
    a = jnp.exp(log_a)
    mult = jnp.sqrt(jnp.maximum(1.0 - jnp.exp(2.0 * log_a), 0.0))
    mult = jnp.where((jnp.arange(S) == 0)[None, :, None], 1.0, mult)
    b = mult * ig * xc
    _, h = lax.associative_scan(_linear_recurrence_combine, (a, b), axis=1)
    return h * jax.nn.gelu(z_g.astype(f32))


def _fwd_setup_inputs(seed: int = 0) -> dict:
    key = jax.random.key(seed)
    ks = jax.random.split(key, 32)
    f32 = jnp.float32
    nrm = lambda k, shape, fan_in: jax.random.normal(k, shape, f32) * (fan_in ** -0.5)
    gain = lambda k, shape: 1.0 + 0.02 * jax.random.normal(k, shape, f32)
    small = lambda k, shape: 0.02 * jax.random.normal(k, shape, f32)
    u = jax.random.uniform(ks[14], (DEPTH, LRU_WIDTH), f32, 0.9, 0.999)
    s = u ** (1.0 / LRU_C)
    lru_lambda = jnp.log(s) - jnp.log1p(-s)
    return {
        "x": jax.random.normal(ks[0], (BATCH, SEQ, D_MODEL), f32),
        "p": jax.random.normal(ks[1], (DEPTH, BATCH, SEQ, PLE_DIM), f32),
        "norm1_g": gain(ks[2], (DEPTH, D_MODEL)),
        "w_in": nrm(ks[3], (DEPTH, D_MODEL, IN_COLS), D_MODEL),
        "b_gate": small(ks[4], (DEPTH, N_BRANCHES, D_MODEL)),
        "pool_w": nrm(ks[5], (DEPTH, POOL_GROUPS, POOL_GROUP_DIM, POOL_GROUP_DIM), POOL_GROUP_DIM),
        "pool_scale": 1.0 + 0.1 * jax.random.normal(ks[6], (DEPTH, POOL_WIDTH), f32),
        "pool_proj": nrm(ks[7], (DEPTH, POOL_WIDTH, D_MODEL), POOL_WIDTH),
        "conv_w": nrm(ks[8], (DEPTH, CONV_WIDTH, LRU_WIDTH), CONV_WIDTH),
        "conv_b": small(ks[9], (DEPTH, LRU_WIDTH)),
        "w_rg": nrm(ks[10], (DEPTH, LRU_HEADS, LRU_HEAD_DIM, LRU_HEAD_DIM), LRU_HEAD_DIM),
        "b_rg": small(ks[11], (DEPTH, LRU_HEADS, LRU_HEAD_DIM)),
        "w_ig": nrm(ks[12], (DEPTH, LRU_HEADS, LRU_HEAD_DIM, LRU_HEAD_DIM), LRU_HEAD_DIM),
        "b_ig": small(ks[13], (DEPTH, LRU_HEADS, LRU_HEAD_DIM)),
        "lru_lambda": lru_lambda,
        "lru_proj": nrm(ks[15], (DEPTH, LRU_WIDTH, D_MODEL), LRU_WIDTH),
        "w_out": nrm(ks[16], (DEPTH, D_MODEL, D_MODEL), D_MODEL),
        "norm2_g": gain(ks[17], (DEPTH, D_MODEL)),
        "w_ffn_in": nrm(ks[18], (DEPTH, D_MODEL, 2 * D_FF), D_MODEL),
        "w_ffn_out": nrm(ks[19], (DEPTH, D_FF, D_MODEL), D_FF),
        "ple_norm_g": gain(ks[20], (DEPTH, D_MODEL)),
        "w_ple_gate": nrm(ks[21], (DEPTH, D_MODEL, D_MODEL), D_MODEL),
        "w_ple_proj": nrm(ks[22], (DEPTH, PLE_DIM, D_MODEL), PLE_DIM),
        "final_g": gain(ks[23], (D_MODEL,)),
    }


def _fwd_reference(x, p, norm1_g, w_in, b_gate, pool_w, pool_scale, pool_proj, conv_w, conv_b,
              w_rg, b_rg, w_ig, b_ig, lru_lambda, lru_proj, w_out, norm2_g, w_ffn_in,
              w_ffn_out, ple_norm_g, w_ple_gate, w_ple_proj, final_g):
    B, S, _ = x.shape
    f32 = jnp.float32
    h = x.astype(f32)
    for i in range(DEPTH):
        u = rms_norm(h, norm1_g[i])
        z = u @ w_in[i].astype(f32)
        z_pool, z_lru, z_gelu, z_gate = jnp.split(
            z, [POOL_WIDTH, POOL_WIDTH + LRU_WIDTH, POOL_WIDTH + 2 * LRU_WIDTH], axis=-1)
        y_pool = causal_multiscale_pool(z_pool, pool_w[i], pool_scale[i]) @ pool_proj[i].astype(f32)
        y_lru = rglru_branch(z_lru, z_gelu, conv_w[i], conv_b[i], w_rg[i], b_rg[i],
                             w_ig[i], b_ig[i], lru_lambda[i]) @ lru_proj[i].astype(f32)
        gates = jax.nn.sigmoid(z_gate.reshape(B, S, N_BRANCHES, D_MODEL) + b_gate[i].astype(f32))
        merged = gates[:, :, 0, :] * y_pool + gates[:, :, 1, :] * y_lru
        h = h + merged @ w_out[i].astype(f32)
        v = rms_norm(h, norm2_g[i])
        g_ff, u_ff = jnp.split(v @ w_ffn_in[i].astype(f32), 2, axis=-1)
        h = h + (jax.nn.silu(g_ff) * u_ff) @ w_ffn_out[i].astype(f32)
        e = p[i].astype(f32) @ w_ple_proj[i].astype(f32)
        ple_gate = jax.nn.sigmoid(rms_norm(h, ple_norm_g[i]) @ w_ple_gate[i].astype(f32))
        h = h + ple_gate * e
    return rms_norm(h, final_g).astype(x.dtype)


import jax as _jax
import jax.numpy as _jnp

TWIN_FORMAT = 'train_step'
FWD_PARAMS = ['x', 'p', 'norm1_g', 'w_in', 'b_gate', 'pool_w', 'pool_scale', 'pool_proj', 'conv_w', 'conv_b', 'w_rg', 'b_rg', 'w_ig', 'b_ig', 'lru_lambda', 'lru_proj', 'w_out', 'norm2_g', 'w_ffn_in', 'w_ffn_out', 'ple_norm_g', 'w_ple_gate', 'w_ple_proj', 'final_g']
TWIN_WEIGHTS = ['norm1_g', 'w_in', 'b_gate', 'pool_w', 'pool_scale', 'pool_proj', 'conv_w', 'conv_b', 'w_rg', 'b_rg', 'w_ig', 'b_ig', 'lru_lambda', 'lru_proj', 'w_out', 'norm2_g', 'w_ffn_in', 'w_ffn_out', 'ple_norm_g', 'w_ple_gate', 'w_ple_proj', 'final_g']
TWIN_DIFF_INPUT = 'x'
TWIN_INPUTS = ['x', 'p', 'norm1_g', 'w_in', 'b_gate', 'pool_w', 'pool_scale', 'pool_proj', 'conv_w', 'conv_b', 'w_rg', 'b_rg', 'w_ig', 'b_ig', 'lru_lambda', 'lru_proj', 'w_out', 'norm2_g', 'w_ffn_in', 'w_ffn_out', 'ple_norm_g', 'w_ple_gate', 'w_ple_proj', 'final_g', 'loss_target', 'm_norm1_g', 'm_w_in', 'm_b_gate', 'm_pool_w', 'm_pool_scale', 'm_pool_proj', 'm_conv_w', 'm_conv_b', 'm_w_rg', 'm_b_rg', 'm_w_ig', 'm_b_ig', 'm_lru_lambda', 'm_lru_proj', 'm_w_out', 'm_norm2_g', 'm_w_ffn_in', 'm_w_ffn_out', 'm_ple_norm_g', 'm_w_ple_gate', 'm_w_ple_proj', 'm_final_g', 'v_norm1_g', 'v_w_in', 'v_b_gate', 'v_pool_w', 'v_pool_scale', 'v_pool_proj', 'v_conv_w', 'v_conv_b', 'v_w_rg', 'v_b_rg', 'v_w_ig', 'v_b_ig', 'v_lru_lambda', 'v_lru_proj', 'v_w_out', 'v_norm2_g', 'v_w_ffn_in', 'v_w_ffn_out', 'v_ple_norm_g', 'v_w_ple_gate', 'v_w_ple_proj', 'v_final_g']
TWIN_OUTPUTS = ['loss', 'grad_x', 'grad_norm1_g', 'grad_w_in', 'grad_b_gate', 'grad_pool_w', 'grad_pool_scale', 'grad_pool_proj', 'grad_conv_w', 'grad_conv_b', 'grad_w_rg', 'grad_b_rg', 'grad_w_ig', 'grad_b_ig', 'grad_lru_lambda', 'grad_lru_proj', 'grad_w_out', 'grad_norm2_g', 'grad_w_ffn_in', 'grad_w_ffn_out', 'grad_ple_norm_g', 'grad_w_ple_gate', 'grad_w_ple_proj', 'grad_final_g', 'delta_norm1_g', 'delta_w_in', 'delta_b_gate', 'delta_pool_w', 'delta_pool_scale', 'delta_pool_proj', 'delta_conv_w', 'delta_conv_b', 'delta_w_rg', 'delta_b_rg', 'delta_w_ig', 'delta_b_ig', 'delta_lru_lambda', 'delta_lru_proj', 'delta_w_out', 'delta_norm2_g', 'delta_w_ffn_in', 'delta_w_ffn_out', 'delta_ple_norm_g', 'delta_w_ple_gate', 'delta_w_ple_proj', 'delta_final_g', 'new_m_norm1_g', 'new_m_w_in', 'new_m_b_gate', 'new_m_pool_w', 'new_m_pool_scale', 'new_m_pool_proj', 'new_m_conv_w', 'new_m_conv_b', 'new_m_w_rg', 'new_m_b_rg', 'new_m_w_ig', 'new_m_b_ig', 'new_m_lru_lambda', 'new_m_lru_proj', 'new_m_w_out', 'new_m_norm2_g', 'new_m_w_ffn_in', 'new_m_w_ffn_out', 'new_m_ple_norm_g', 'new_m_w_ple_gate', 'new_m_w_ple_proj', 'new_m_final_g', 'new_v_norm1_g', 'new_v_w_in', 'new_v_b_gate', 'new_v_pool_w', 'new_v_pool_scale', 'new_v_pool_proj', 'new_v_conv_w', 'new_v_conv_b', 'new_v_w_rg', 'new_v_b_rg', 'new_v_w_ig', 'new_v_b_ig', 'new_v_lru_lambda', 'new_v_lru_proj', 'new_v_w_out', 'new_v_norm2_g', 'new_v_w_ffn_in', 'new_v_w_ffn_out', 'new_v_ple_norm_g', 'new_v_w_ple_gate', 'new_v_w_ple_proj', 'new_v_final_g']
TWIN_LEAF_KINDS = {'loss': 'loss', 'grad_x': 'grad_x', 'grad_norm1_g': 'grad_w', 'grad_w_in': 'grad_w', 'grad_b_gate': 'grad_w', 'grad_pool_w': 'grad_w', 'grad_pool_scale': 'grad_w', 'grad_pool_proj': 'grad_w', 'grad_conv_w': 'grad_w', 'grad_conv_b': 'grad_w', 'grad_w_rg': 'grad_w', 'grad_b_rg': 'grad_w', 'grad_w_ig': 'grad_w', 'grad_b_ig': 'grad_w', 'grad_lru_lambda': 'grad_w', 'grad_lru_proj': 'grad_w', 'grad_w_out': 'grad_w', 'grad_norm2_g': 'grad_w', 'grad_w_ffn_in': 'grad_w', 'grad_w_ffn_out': 'grad_w', 'grad_ple_norm_g': 'grad_w', 'grad_w_ple_gate': 'grad_w', 'grad_w_ple_proj': 'grad_w', 'grad_final_g': 'grad_w', 'delta_norm1_g': 'delta_w', 'delta_w_in': 'delta_w', 'delta_b_gate': 'delta_w', 'delta_pool_w': 'delta_w', 'delta_pool_scale': 'delta_w', 'delta_pool_proj': 'delta_w', 'delta_conv_w': 'delta_w', 'delta_conv_b': 'delta_w', 'delta_w_rg': 'delta_w', 'delta_b_rg': 'delta_w', 'delta_w_ig': 'delta_w', 'delta_b_ig': 'delta_w', 'delta_lru_lambda': 'delta_w', 'delta_lru_proj': 'delta_w', 'delta_w_out': 'delta_w', 'delta_norm2_g': 'delta_w', 'delta_w_ffn_in': 'delta_w', 'delta_w_ffn_out': 'delta_w', 'delta_ple_norm_g': 'delta_w', 'delta_w_ple_gate': 'delta_w', 'delta_w_ple_proj': 'delta_w', 'delta_final_g': 'delta_w', 'new_m_norm1_g': 'new_m', 'new_m_w_in': 'new_m', 'new_m_b_gate': 'new_m', 'new_m_pool_w': 'new_m', 'new_m_pool_scale': 'new_m', 'new_m_pool_proj': 'new_m', 'new_m_conv_w': 'new_m', 'new_m_conv_b': 'new_m', 'new_m_w_rg': 'new_m', 'new_m_b_rg': 'new_m', 'new_m_w_ig': 'new_m', 'new_m_b_ig': 'new_m', 'new_m_lru_lambda': 'new_m', 'new_m_lru_proj': 'new_m', 'new_m_w_out': 'new_m', 'new_m_norm2_g': 'new_m', 'new_m_w_ffn_in': 'new_m', 'new_m_w_ffn_out': 'new_m', 'new_m_ple_norm_g': 'new_m', 'new_m_w_ple_gate': 'new_m', 'new_m_w_ple_proj': 'new_m', 'new_m_final_g': 'new_m', 'new_v_norm1_g': 'new_v', 'new_v_w_in': 'new_v', 'new_v_b_gate': 'new_v', 'new_v_pool_w': 'new_v', 'new_v_pool_scale': 'new_v', 'new_v_pool_proj': 'new_v', 'new_v_conv_w': 'new_v', 'new_v_conv_b': 'new_v', 'new_v_w_rg': 'new_v', 'new_v_b_rg': 'new_v', 'new_v_w_ig': 'new_v', 'new_v_b_ig': 'new_v', 'new_v_lru_lambda': 'new_v', 'new_v_lru_proj': 'new_v', 'new_v_w_out': 'new_v', 'new_v_norm2_g': 'new_v', 'new_v_w_ffn_in': 'new_v', 'new_v_w_ffn_out': 'new_v', 'new_v_ple_norm_g': 'new_v', 'new_v_w_ple_gate': 'new_v', 'new_v_w_ple_proj': 'new_v', 'new_v_final_g': 'new_v'}


def _forward(args):
    return _fwd_reference(*[args[k] for k in FWD_PARAMS])


def _output_shape():
    def fwd():
        inp = _fwd_setup_inputs(0)
        return _fwd_reference(*[inp[k] for k in FWD_PARAMS])
    out = _jax.eval_shape(fwd)
    return out.shape, out.dtype

N_MICROBATCH = 1
ADAM_LR = 0.001
ADAM_B1 = 0.9
ADAM_B2 = 0.999
ADAM_EPS = 1e-08
ADAM_WD = 0.01
ADAM_STEP = 10
PER_EXAMPLE_BATCH_AXIS = {'x': 0, 'p': 1, 'loss_target': 0}
SHARED_INPUTS = []
_WEIGHT_DTYPES = {'norm1_g': _jnp.float32, 'w_in': _jnp.float32, 'b_gate': _jnp.float32, 'pool_w': _jnp.float32, 'pool_scale': _jnp.float32, 'pool_proj': _jnp.float32, 'conv_w': _jnp.float32, 'conv_b': _jnp.float32, 'w_rg': _jnp.float32, 'b_rg': _jnp.float32, 'w_ig': _jnp.float32, 'b_ig': _jnp.float32, 'lru_lambda': _jnp.float32, 'lru_proj': _jnp.float32, 'w_out': _jnp.float32, 'norm2_g': _jnp.float32, 'w_ffn_in': _jnp.float32, 'w_ffn_out': _jnp.float32, 'ple_norm_g': _jnp.float32, 'w_ple_gate': _jnp.float32, 'w_ple_proj': _jnp.float32, 'final_g': _jnp.float32}
MOMENT_SCALE = {'norm1_g': 1.933302e-01, 'w_in': 9.558736e-02, 'b_gate': 4.624966e-02, 'pool_w': 2.200445e-01, 'pool_scale': 2.236977e-01, 'pool_proj': 1.565513e-01, 'conv_w': 8.885540e-02, 'conv_b': 1.178972e+00, 'w_rg': 2.879545e-02, 'b_rg': 2.412126e-02, 'w_ig': 5.120330e-02, 'b_ig': 2.620382e-02, 'lru_lambda': 4.678283e-02, 'lru_proj': 8.221489e-02, 'w_out': 1.760408e-01, 'norm2_g': 2.280639e-01, 'w_ffn_in': 9.496929e-02, 'w_ffn_out': 1.550463e-01, 'ple_norm_g': 6.294267e-02, 'w_ple_gate': 5.414539e-02, 'w_ple_proj': 1.387237e-01, 'final_g': 1.277976e+02}


def _to_microbatches(a, axis):
    t = _jnp.moveaxis(a, axis, 0)
    t = t.reshape((N_MICROBATCH, t.shape[0] // N_MICROBATCH) + t.shape[1:])
    return _jnp.moveaxis(t, 1, axis + 1)


def setup_inputs(seed: int = 0) -> dict:
    inp = _fwd_setup_inputs(seed)
    key = _jax.random.fold_in(_jax.random.key(seed), 7919)
    shape, _ = _output_shape()
    out = dict(inp)
    out["loss_target"] = _jax.random.normal(_jax.random.fold_in(key, 0), shape, _jnp.float32)
    for i, name in enumerate(TWIN_WEIGHTS):
        w = inp[name].astype(_jnp.float32)
        if MOMENT_SCALE is None:
            s = _jnp.sqrt(_jnp.mean(_jnp.square(w)) + 1e-30)
        else:
            s = MOMENT_SCALE[name]
        km, kv = _jax.random.split(_jax.random.fold_in(key, i + 1))
        out[name] = w
        out["m_" + name] = s * _jax.random.normal(km, w.shape, _jnp.float32)
        out["v_" + name] = (s * s) * _jax.random.uniform(kv, w.shape, _jnp.float32, 0.5, 1.5)
    if N_MICROBATCH > 1:
        for name, axis in PER_EXAMPLE_BATCH_AXIS.items():
            out[name] = _to_microbatches(out[name], axis)
    return {'x': out['x'], 'p': out['p'], 'norm1_g': out['norm1_g'], 'w_in': out['w_in'], 'b_gate': out['b_gate'], 'pool_w': out['pool_w'], 'pool_scale': out['pool_scale'], 'pool_proj': out['pool_proj'], 'conv_w': out['conv_w'], 'conv_b': out['conv_b'], 'w_rg': out['w_rg'], 'b_rg': out['b_rg'], 'w_ig': out['w_ig'], 'b_ig': out['b_ig'], 'lru_lambda': out['lru_lambda'], 'lru_proj': out['lru_proj'], 'w_out': out['w_out'], 'norm2_g': out['norm2_g'], 'w_ffn_in': out['w_ffn_in'], 'w_ffn_out': out['w_ffn_out'], 'ple_norm_g': out['ple_norm_g'], 'w_ple_gate': out['w_ple_gate'], 'w_ple_proj': out['w_ple_proj'], 'final_g': out['final_g'], 'loss_target': out['loss_target'], 'm_norm1_g': out['m_norm1_g'], 'm_w_in': out['m_w_in'], 'm_b_gate': out['m_b_gate'], 'm_pool_w': out['m_pool_w'], 'm_pool_scale': out['m_pool_scale'], 'm_pool_proj': out['m_pool_proj'], 'm_conv_w': out['m_conv_w'], 'm_conv_b': out['m_conv_b'], 'm_w_rg': out['m_w_rg'], 'm_b_rg': out['m_b_rg'], 'm_w_ig': out['m_w_ig'], 'm_b_ig': out['m_b_ig'], 'm_lru_lambda': out['m_lru_lambda'], 'm_lru_proj': out['m_lru_proj'], 'm_w_out': out['m_w_out'], 'm_norm2_g': out['m_norm2_g'], 'm_w_ffn_in': out['m_w_ffn_in'], 'm_w_ffn_out': out['m_w_ffn_out'], 'm_ple_norm_g': out['m_ple_norm_g'], 'm_w_ple_gate': out['m_w_ple_gate'], 'm_w_ple_proj': out['m_w_ple_proj'], 'm_final_g': out['m_final_g'], 'v_norm1_g': out['v_norm1_g'], 'v_w_in': out['v_w_in'], 'v_b_gate': out['v_b_gate'], 'v_pool_w': out['v_pool_w'], 'v_pool_scale': out['v_pool_scale'], 'v_pool_proj': out['v_pool_proj'], 'v_conv_w': out['v_conv_w'], 'v_conv_b': out['v_conv_b'], 'v_w_rg': out['v_w_rg'], 'v_b_rg': out['v_b_rg'], 'v_w_ig': out['v_w_ig'], 'v_b_ig': out['v_b_ig'], 'v_lru_lambda': out['v_lru_lambda'], 'v_lru_proj': out['v_lru_proj'], 'v_w_out': out['v_w_out'], 'v_norm2_g': out['v_norm2_g'], 'v_w_ffn_in': out['v_w_ffn_in'], 'v_w_ffn_out': out['v_w_ffn_out'], 'v_ple_norm_g': out['v_ple_norm_g'], 'v_w_ple_gate': out['v_w_ple_gate'], 'v_w_ple_proj': out['v_w_ple_proj'], 'v_final_g': out['v_final_g']}


def _loss(weights, diff, rest, loss_target):
    with _jax.named_scope("forward"):
        args = {**rest, TWIN_DIFF_INPUT: diff, **{k: w.astype(_WEIGHT_DTYPES[k]) for k, w in weights.items()}}
        y = _forward(args)
    with _jax.named_scope("loss_head"):
        err = _jnp.square(y.astype(_jnp.float32) - loss_target)
        return 0.5 * _jnp.sum(_jnp.mean(err, axis=-1)) if err.ndim else 0.5 * err


def _adamw(w, g, m, v):
    m = ADAM_B1 * m + (1.0 - ADAM_B1) * g
    v = ADAM_B2 * v + (1.0 - ADAM_B2) * _jnp.square(g)
    m_hat = m / (1.0 - ADAM_B1 ** ADAM_STEP)
    v_hat = v / (1.0 - ADAM_B2 ** ADAM_STEP)
    delta = -ADAM_LR * (m_hat / (_jnp.sqrt(v_hat) + ADAM_EPS) + ADAM_WD * w)
    return delta, m, v


def reference(x, p, norm1_g, w_in, b_gate, pool_w, pool_scale, pool_proj, conv_w, conv_b, w_rg, b_rg, w_ig, b_ig, lru_lambda, lru_proj, w_out, norm2_g, w_ffn_in, w_ffn_out, ple_norm_g, w_ple_gate, w_ple_proj, final_g, loss_target, m_norm1_g, m_w_in, m_b_gate, m_pool_w, m_pool_scale, m_pool_proj, m_conv_w, m_conv_b, m_w_rg, m_b_rg, m_w_ig, m_b_ig, m_lru_lambda, m_lru_proj, m_w_out, m_norm2_g, m_w_ffn_in, m_w_ffn_out, m_ple_norm_g, m_w_ple_gate, m_w_ple_proj, m_final_g, v_norm1_g, v_w_in, v_b_gate, v_pool_w, v_pool_scale, v_pool_proj, v_conv_w, v_conv_b, v_w_rg, v_b_rg, v_w_ig, v_b_ig, v_lru_lambda, v_lru_proj, v_w_out, v_norm2_g, v_w_ffn_in, v_w_ffn_out, v_ple_norm_g, v_w_ple_gate, v_w_ple_proj, v_final_g):
    given = dict(x=x, p=p, norm1_g=norm1_g, w_in=w_in, b_gate=b_gate, pool_w=pool_w, pool_scale=pool_scale, pool_proj=pool_proj, conv_w=conv_w, conv_b=conv_b, w_rg=w_rg, b_rg=b_rg, w_ig=w_ig, b_ig=b_ig, lru_lambda=lru_lambda, lru_proj=lru_proj, w_out=w_out, norm2_g=norm2_g, w_ffn_in=w_ffn_in, w_ffn_out=w_ffn_out, ple_norm_g=ple_norm_g, w_ple_gate=w_ple_gate, w_ple_proj=w_ple_proj, final_g=final_g, loss_target=loss_target, m_norm1_g=m_norm1_g, m_w_in=m_w_in, m_b_gate=m_b_gate, m_pool_w=m_pool_w, m_pool_scale=m_pool_scale, m_pool_proj=m_pool_proj, m_conv_w=m_conv_w, m_conv_b=m_conv_b, m_w_rg=m_w_rg, m_b_rg=m_b_rg, m_w_ig=m_w_ig, m_b_ig=m_b_ig, m_lru_lambda=m_lru_lambda, m_lru_proj=m_lru_proj, m_w_out=m_w_out, m_norm2_g=m_norm2_g, m_w_ffn_in=m_w_ffn_in, m_w_ffn_out=m_w_ffn_out, m_ple_norm_g=m_ple_norm_g, m_w_ple_gate=m_w_ple_gate, m_w_ple_proj=m_w_ple_proj, m_final_g=m_final_g, v_norm1_g=v_norm1_g, v_w_in=v_w_in, v_b_gate=v_b_gate, v_pool_w=v_pool_w, v_pool_scale=v_pool_scale, v_pool_proj=v_pool_proj, v_conv_w=v_conv_w, v_conv_b=v_conv_b, v_w_rg=v_w_rg, v_b_rg=v_b_rg, v_w_ig=v_w_ig, v_b_ig=v_b_ig, v_lru_lambda=v_lru_lambda, v_lru_proj=v_lru_proj, v_w_out=v_w_out, v_norm2_g=v_norm2_g, v_w_ffn_in=v_w_ffn_in, v_w_ffn_out=v_w_ffn_out, v_ple_norm_g=v_ple_norm_g, v_w_ple_gate=v_w_ple_gate, v_w_ple_proj=v_w_ple_proj, v_final_g=v_final_g)
    weights = {n: given[n] for n in TWIN_WEIGHTS}
    shared = {n: given[n] for n in SHARED_INPUTS}
    per_example = {n: given[n] for n in ['x', 'p']}
    grad_fn = _jax.value_and_grad(_loss, argnums=(0, 1))

    def one_microbatch(ex, loss_target):
        ex = dict(ex)
        diff = ex.pop(TWIN_DIFF_INPUT)
        return grad_fn(weights, diff, {**shared, **ex}, loss_target)

    if N_MICROBATCH == 1:
        loss, (grad_w, grad_x) = one_microbatch(per_example, given["loss_target"])
    else:
        def body(carry, xs):
            loss_sum, grad_sum = carry
            l_k, (gw_k, gx_k) = one_microbatch(xs[0], xs[1])
            with _jax.named_scope("update"):
                return (loss_sum + l_k, _jax.tree.map(_jnp.add, grad_sum, gw_k)), gx_k

        init = (_jnp.zeros((), _jnp.float32), _jax.tree.map(_jnp.zeros_like, weights))
        (loss, grad_w), grad_x = _jax.lax.scan(body, init, (per_example, given["loss_target"]))
    with _jax.named_scope("update"):
        delta_w, new_m, new_v = {}, {}, {}
        for n in TWIN_WEIGHTS:
            delta_w[n], new_m[n], new_v[n] = _adamw(weights[n], grad_w[n], given["m_" + n], given["v_" + n])
    return (loss, grad_x, *[grad_w[n] for n in TWIN_WEIGHTS], *[delta_w[n] for n in TWIN_WEIGHTS],
            *[new_m[n] for n in TWIN_WEIGHTS], *[new_v[n] for n in TWIN_WEIGHTS])
```

```python
import functools
import math

import jax
import jax.numpy as jnp
from jax import lax
from jax.experimental import pallas as pl
from jax.experimental.pallas import tpu as pltpu

F32 = jnp.float32
D = 1024
NIN = 4608
PW = 512
FF = 2816
PLE = 256
HEADS, HD = 8, 128
GROUPS, GD = 4, 128
WINDOWS = (2, 4, 8, 16)
HALO = 16
CONV = 4
EPS = 1e-6
LRU_C = 8.0
NDEV = 8
MESH = pl.DeviceIdType.MESH

ADAM_LR, ADAM_B1, ADAM_B2, ADAM_EPS, ADAM_WD, ADAM_STEP = 0.001, 0.9, 0.999, 1e-08, 0.01, 10

_MXU = jnp.bfloat16
TM = 512
TM_SEQ = 256
VMEM_LIMIT = 56 * 1024 * 1024

W_SLABS = (("win", 576), ("wffn", 704), ("pproj", 128), ("ple", 128), ("lru", 128), ("wout", 128),
           ("wffo", 352), ("wpg", 128), ("f32s", 16))
W_OFF = {}
_o = 0
for _n, _r in W_SLABS:
    W_OFF[_n] = (_o, _r)
    _o += _r
RW = _o
SMALL_ROWS = 48
G_SLABS = (("win", 576), ("wffn", 704), ("pproj", 128), ("ple", 128), ("lru", 128), ("wout", 128),
           ("wffo", 352), ("wpg", 128), ("small", SMALL_ROWS))
G_OFF = {}
_o = 0
for _n, _r in G_SLABS:
    G_OFF[_n] = (_o, _r)
    _o += _r
RG = _o


def _cparams(n_axes=1, vmem=VMEM_LIMIT):
    return pltpu.CompilerParams(dimension_semantics=("arbitrary",) * n_axes, vmem_limit_bytes=vmem)


def _my_pos():
    return lax.axis_index("x"), lax.axis_index("y"), lax.axis_index("c")


def _nt(a, b):
    return lax.dot_general(a, b, (((1,), (1,)), ((), ())), preferred_element_type=F32)


def _nn(a, b):
    return lax.dot_general(a, b, (((1,), (0,)), ((), ())), preferred_element_type=F32)


def _tn(a, b):
    return lax.dot_general(a, b, (((0,), (0,)), ((), ())), preferred_element_type=F32)


def _sigmoid(x):
    return 1.0 / (1.0 + jnp.exp(-x))


_GELU_K = math.sqrt(2.0 / math.pi)


def _gelu_and_grad(x):
    x2 = x * x
    inner = _GELU_K * (x + 0.044715 * x2 * x)
    t = jnp.tanh(inner)
    g = 0.5 * x * (1.0 + t)
    dg = 0.5 * (1.0 + t) + 0.5 * x * (1.0 - t * t) * _GELU_K * (1.0 + 3.0 * 0.044715 * x2)
    return g, dg


def _softplus_neg(lam):
    x = -lam
    t = jnp.exp(-jnp.abs(x))
    u = 1.0 + t
    l1p = jnp.where(u == 1.0, t, jnp.log(u) * t / (u - 1.0))
    return jnp.maximum(x, 0.0) + l1p, _sigmoid(x)


def _start_slab_loads(g_ref, name, dst_ref, sems, base, width=D):
    off, rows = W_OFF[name]
    copies = []
    for k in range(NDEV):
        if width == D:
            src = g_ref.at[k, pl.ds(off, rows), :]
        else:
            src = g_ref.at[k, pl.ds(off, rows), pl.ds(0, width)]
        cp = pltpu.make_async_copy(src, dst_ref.at[pl.ds(k * rows, rows), :], sems.at[base + k])
        cp.start()
        copies.append(cp)
    return copies


def _load_weights(g_ref, items, sems):
    copies = []
    for n, (name, dst, width) in enumerate(items):
        copies += _start_slab_loads(g_ref, name, dst, sems, n * NDEV, width)
    for cp in copies:
        cp.wait()


def _all_gather_weights(own):
    rows, cols = own.shape

    def body(own_ref, out_ref, send_sems, recv_sems, local_sem):
        x, y, c = _my_pos()
        me, sibling = (x, y, c), (x, y, 1 - c)
        chips = [(1 - x, y), (x, 1 - y), (1 - x, 1 - y)]

        def slab(px, py, pc):
            return out_ref.at[4 * px + 2 * py + pc]

        def copy(k, block, to, src=None):
            return pltpu.make_async_remote_copy(
                src_ref=slab(*block) if src is None else src, dst_ref=slab(*block),
                send_sem=send_sems.at[k], recv_sem=recv_sems.at[k], device_id=to, device_id_type=MESH)

        mine = pltpu.make_async_copy(own_ref, slab(*me), local_sem)
        mine.start()
        first = [copy(0, me, sibling, src=own_ref)]
        first += [copy(1 + j, me, (*chip, c), src=own_ref) for j, chip in enumerate(chips)]
        for cp in first:
            cp.start()
        passed = [copy(4 + j, (*chip, c), sibling) for j, chip in enumerate(chips)]
        for j, chip in enumerate(chips):
            copy(1 + j, (*chip, c), me).wait_recv()
            passed[j].start()
        copy(0, sibling, me).wait_recv()
        for j, chip in enumerate(chips):
            copy(4 + j, (*chip, 1 - c), me).wait_recv()
        for cp in first + passed:
            cp.wait_send()
        mine.wait()

    return pl.pallas_call(
        body, name="ag_weights",
        out_shape=jax.ShapeDtypeStruct((NDEV, rows, cols), own.dtype),
        in_specs=[pl.BlockSpec(memory_space=pl.ANY)],
        out_specs=pl.BlockSpec(memory_space=pl.ANY),
        scratch_shapes=[pltpu.SemaphoreType.DMA((7,)), pltpu.SemaphoreType.DMA((7,)), pltpu.SemaphoreType.DMA],
    )(own)


def _rs_d2d(parts):
    n = len(parts)

    def body(*refs):
        part_refs = refs[:n]
        own_ref, land_ref, send_sems, recv_sems, local_sems = refs[n:]
        x, y, c = _my_pos()
        sibling = (x, y, 1 - c)
        local, remote = [], []
        for j in range(4):
            xj, yj = j // 2, j % 2
            for w, (name, rows) in enumerate(G_SLABS):
                off = G_OFF[name][0]
                k = j * n + w

                def dst_of(ref):
                    return ref.at[j, pl.ds(off, rows), :]

                s_mine = pl.multiple_of((4 * xj + 2 * yj + c) * rows, 8)
                s_sib = pl.multiple_of((4 * xj + 2 * yj + 1 - c) * rows, 8)
                lc = pltpu.make_async_copy(part_refs[w].at[pl.ds(s_mine, rows), :], dst_of(own_ref), local_sems.at[k])
                rc = pltpu.make_async_remote_copy(
                    src_ref=part_refs[w].at[pl.ds(s_sib, rows), :], dst_ref=dst_of(land_ref),
                    send_sem=send_sems.at[k], recv_sem=recv_sems.at[k], device_id=sibling, device_id_type=MESH)
                lc.start()
                rc.start()
                local.append(lc)
                remote.append(rc)
        for rc in remote:
            rc.wait_recv()
        for rc in remote:
            rc.wait_send()
        for lc in local:
            lc.wait()

    shape = jax.ShapeDtypeStruct((4, RG, D), F32)
    return pl.pallas_call(
        body, name="rs_d2d",
        out_shape=(shape, shape),
        in_specs=[pl.BlockSpec(memory_space=pl.ANY)] * n,
        out_specs=(pl.BlockSpec(memory_space=pl.ANY), pl.BlockSpec(memory_space=pl.ANY)),
        scratch_shapes=[pltpu.SemaphoreType.DMA((4 * n,)), pltpu.SemaphoreType.DMA((4 * n,)),
                        pltpu.SemaphoreType.DMA((4 * n,))],
    )(*parts)


def _rs_ici(t):
    def body(t_ref, land_ref, send_sems, recv_sems):
        x, y, c = _my_pos()
        targets = [(1 - x, y), (x, 1 - y), (1 - x, 1 - y)]
        copies = []
        for r, (xt, yt) in enumerate(targets):
            cp = pltpu.make_async_remote_copy(
                src_ref=t_ref.at[2 * xt + yt], dst_ref=land_ref.at[r],
                send_sem=send_sems.at[r], recv_sem=recv_sems.at[r], device_id=(xt, yt, c), device_id_type=MESH)
            cp.start()
            copies.append(cp)
        for cp in copies:
            cp.wait_recv()
        for cp in copies:
            cp.wait_send()

    return pl.pallas_call(
        body, name="rs_ici",
        out_shape=jax.ShapeDtypeStruct((3, RG, D), F32),
        in_specs=[pl.BlockSpec(memory_space=pl.ANY)],
        out_specs=pl.BlockSpec(memory_space=pl.ANY),
        scratch_shapes=[pltpu.SemaphoreType.DMA((3,)), pltpu.SemaphoreType.DMA((3,))],
    )(t)


def _all_gather_small(piece):
    rows = piece.shape[0]

    def body(p_ref, out_ref, send_sems, recv_sems, local_sem):
        x, y, c = _my_pos()
        me = 4 * x + 2 * y + c
        mine = pltpu.make_async_copy(p_ref, out_ref.at[pl.ds(pl.multiple_of(me * rows, 8), rows), :], local_sem)
        mine.start()
        sends = []
        peers = []
        for r in range(1, NDEV):
            px = 1 - x if (r >> 2) & 1 else x
            py = 1 - y if (r >> 1) & 1 else y
            pc = 1 - c if r & 1 else c
            peers.append((px, py, pc))
            cp = pltpu.make_async_remote_copy(
                src_ref=p_ref, dst_ref=out_ref.at[pl.ds(pl.multiple_of(me * rows, 8), rows), :],
                send_sem=send_sems.at[r - 1], recv_sem=recv_sems.at[r - 1], device_id=(px, py, pc),
                device_id_type=MESH)
            cp.start()
            sends.append(cp)
        for r, (px, py, pc) in enumerate(peers):
            them = 4 * px + 2 * py + pc
            pltpu.make_async_remote_copy(
                src_ref=p_ref, dst_ref=out_ref.at[pl.ds(pl.multiple_of(them * rows, 8), rows), :],
                send_sem=send_sems.at[r], recv_sem=recv_sems.at[r], device_id=(px, py, pc),
                device_id_type=MESH).wait_recv()
        for cp in sends:
            cp.wait_send()
        mine.wait()

    return pl.pallas_call(
        body, name="ag_small",
        out_shape=jax.ShapeDtypeStruct((NDEV * rows, piece.shape[1]), piece.dtype),
        in_specs=[pl.BlockSpec(memory_space=pl.ANY)],
        out_specs=pl.BlockSpec(memory_space=pl.ANY),
        scratch_shapes=[pltpu.SemaphoreType.DMA((7,)), pltpu.SemaphoreType.DMA((7,)), pltpu.SemaphoreType.DMA],
    )(piece)


def _row_block(rows, target=512):
    b = min(rows, target)
    while rows % b or b % 8:
        b -= 8
    return b


def _sum_arrays(arrs, name):
    rows, cols = arrs[0].shape
    br = _row_block(rows, 464)
    n = len(arrs)

    def body(*refs):
        acc = refs[0][...]
        for r in refs[1:n]:
            acc = acc + r[...]
        refs[n][...] = acc

    spec = pl.BlockSpec((br, cols), lambda i: (i, 0))
    return pl.pallas_call(
        body, name=name, grid=(rows // br,),
        out_shape=jax.ShapeDtypeStruct((rows, cols), F32),
        in_specs=[spec] * n, out_specs=spec, compiler_params=_cparams(1),
    )(*arrs)


def _adamw(w, g, m, v, name):
    rows, cols = w.shape
    br = _row_block(rows, 256)

    def body(w_ref, g_ref, m_ref, v_ref, d_ref, nm_ref, nv_ref):
        g_ = g_ref[...]
        m_ = ADAM_B1 * m_ref[...] + (1.0 - ADAM_B1) * g_
        v_ = ADAM_B2 * v_ref[...] + (1.0 - ADAM_B2) * (g_ * g_)
        m_hat = m_ / (1.0 - ADAM_B1 ** ADAM_STEP)
        v_hat = v_ / (1.0 - ADAM_B2 ** ADAM_STEP)
        d_ref[...] = -ADAM_LR * (m_hat / (jnp.sqrt(v_hat) + ADAM_EPS) + ADAM_WD * w_ref[...])
        nm_ref[...] = m_
        nv_ref[...] = v_

    spec = pl.BlockSpec((br, cols), lambda i: (i, 0))
    shape = jax.ShapeDtypeStruct((rows, cols), F32)
    return pl.pallas_call(
        body, name=name, grid=(rows // br,), out_shape=(shape, shape, shape),
        in_specs=[spec] * 4, out_specs=(spec, spec, spec), compiler_params=_cparams(1),
    )(w, g, m, v)


def _grad_matmul(lhs, rhs, name, block_rows):
    s, r = lhs.shape
    k = rhs.shape[1]
    tm = min(TM, s)

    def body(l_ref, r_ref, o_ref):
        @pl.when(pl.program_id(1) == 0)
        def _():
            o_ref[...] = jnp.zeros_like(o_ref)

        o_ref[:, pl.ds(0, k)] += _tn(l_ref[...].astype(_MXU), r_ref[...].astype(_MXU))

    return pl.pallas_call(
        body, name=name, grid=(r // block_rows, s // tm),
        out_shape=jax.ShapeDtypeStruct((r, D), F32),
        in_specs=[pl.BlockSpec((tm, block_rows), lambda b, i: (i, b)), pl.BlockSpec((tm, k), lambda b, i: (i, 0))],
        out_specs=pl.BlockSpec((block_rows, D), lambda b, i: (b, 0)),
        compiler_params=_cparams(2),
    )(lhs, rhs)


def _inproj_fwd(x, g1, gw):
    s = x.shape[0]
    tm = min(TM, s)
    nchunk = 4
    cw = NIN // nchunk

    def body(x_ref, g1_ref, gw_ref, u_ref, z_ref, w_vmem, sems):
        @pl.when(pl.program_id(0) == 0)
        def _():
            _load_weights(gw_ref, [("win", w_vmem, D)], sems)

        xv = x_ref[...]
        inv = lax.rsqrt(jnp.mean(xv * xv, axis=-1, keepdims=True) + EPS)
        u = (xv * inv * g1_ref[...]).astype(_MXU)
        u_ref[...] = u
        for ch in range(nchunk):
            z_ref[:, pl.ds(ch * cw, cw)] = _nt(u, w_vmem[pl.ds(ch * cw, cw), :])

    return pl.pallas_call(
        body, name="inproj_fwd", grid=(s // tm,),
        out_shape=(jax.ShapeDtypeStruct((s, D), _MXU), jax.ShapeDtypeStruct((s, NIN), F32)),
        in_specs=[pl.BlockSpec((tm, D), lambda i: (i, 0)), pl.BlockSpec((1, D), lambda i: (0, 0)),
                  pl.BlockSpec(memory_space=pl.ANY)],
        out_specs=(pl.BlockSpec((tm, D), lambda i: (i, 0)), pl.BlockSpec((tm, NIN), lambda i: (i, 0))),
        scratch_shapes=[pltpu.VMEM((NIN, D), _MXU), pltpu.SemaphoreType.DMA((NDEV,))],
        compiler_params=_cparams(1),
    )(x, g1, gw)


def _pool_tile(pbuf, t0, tm, pw_ref, scale_ref):
    t = t0 + lax.broadcasted_iota(jnp.int32, (tm, GD), 0)
    pooled, mixed_pre = [], []
    for g, w in enumerate(WINDOWS):
        cs = pl.ds(g * GD, GD)
        cur = pbuf[pl.ds(HALO, tm), cs]
        acc = cur
        for d in range(1, w):
            acc = acc + pbuf[pl.ds(HALO - d, tm), cs]
        cnt = jnp.minimum(t + 1, w).astype(F32)
        pg = acc / cnt - cur
        pooled.append(pg)
        mixed_pre.append(_nn(pg.astype(_MXU), pw_ref[g]))
    return pooled, mixed_pre


def _lru_gates_head(hh, lbuf, t0, tm, cw_ref, cb_ref, wrg_ref, brg_ref, wig_ref, big_ref, sp):
    cs = pl.ds(hh * HD, HD)
    xc = cb_ref[:, cs] + cw_ref[pl.ds(CONV - 1, 1), cs] * lbuf[pl.ds(HALO, tm), cs]
    for k in range(CONV - 1):
        xc = xc + cw_ref[pl.ds(k, 1), cs] * lbuf[pl.ds(HALO - (CONV - 1) + k, tm), cs]
    xcm = xc.astype(_MXU)
    r = _sigmoid(_nn(xcm, wrg_ref[hh]) + brg_ref[pl.ds(hh, 1), :])
    ig = _sigmoid(_nn(xcm, wig_ref[hh]) + big_ref[pl.ds(hh, 1), :])
    a = jnp.exp(-LRU_C * r * sp[:, hh * HD:(hh + 1) * HD])
    one_m = 1.0 - a * a
    t = t0 + lax.broadcasted_iota(jnp.int32, (tm, HD), 0)
    mult = jnp.where(t == 0, 1.0, jnp.sqrt(jnp.maximum(one_m, 0.0)))
    return xc, r, ig, a, one_m, mult


def _scan_forward(a_ref, b_ref, h_ref, carry, tm):
    c = a_ref.shape[1]
    row = lax.broadcasted_iota(jnp.int32, (8, c), 0)

    def step(i, carry):
        r0 = pl.multiple_of(i * 8, 8)
        av = a_ref[pl.ds(r0, 8), :]
        bv = b_ref[pl.ds(r0, 8), :]
        for d in (1, 2, 4):
            keep = row >= d
            bv = bv + av * jnp.where(keep, pltpu.roll(bv, d, 0), 0.0)
            av = av * jnp.where(keep, pltpu.roll(av, d, 0), 1.0)
        hv = bv + av * carry
        h_ref[pl.ds(r0, 8), :] = hv
        return jnp.broadcast_to(hv[7:8, :], (8, c))

    return lax.fori_loop(0, tm // 8, step, carry)


def _scan_backward(an_ref, g_ref, dh_ref, carry, tm):
    c = g_ref.shape[1]
    row = lax.broadcasted_iota(jnp.int32, (8, c), 0)
    n = tm // 8

    def step(i, carry):
        r0 = pl.multiple_of((n - 1 - i) * 8, 8)
        av = an_ref[pl.ds(r0, 8), :]
        gv = g_ref[pl.ds(r0, 8), :]
        for d in (1, 2, 4):
            keep = row < 8 - d
            gv = gv + av * jnp.where(keep, pltpu.roll(gv, 8 - d, 0), 0.0)
            av = av * jnp.where(keep, pltpu.roll(av, 8 - d, 0), 1.0)
        hv = gv + av * carry
        dh_ref[pl.ds(r0, 8), :] = hv
        return jnp.broadcast_to(hv[0:1, :], (8, c))

    return lax.fori_loop(0, n, step, carry)


def _mixer_fwd(z, x, gw, small):
    s = x.shape[0]
    tm = min(TM_SEQ, s)
    (pool_w, pool_scale, conv_w, conv_b, w_rg, b_rg, w_ig, b_ig, lam, b_gate) = small

    def body(z_ref, x_ref, gw_ref, pw_ref, ps_ref, cw_ref, cb_ref, wrg_ref, brg_ref, wig_ref, big_ref, lam_ref,
             bg_ref, h_ref, yl_ref, mg_ref, yp_ref, yr_ref, h1_ref,
             pprojT, lru_w, wout_w, pbuf, lbuf, a_s, b_s, hcar, sems):
        i = pl.program_id(0)
        t0 = i * tm

        @pl.when(i == 0)
        def _():
            _load_weights(gw_ref, [("pproj", pprojT, PW), ("lru", lru_w, D), ("wout", wout_w, D)], sems)
            pbuf[pl.ds(0, HALO), :] = jnp.zeros((HALO, PW), F32)
            lbuf[pl.ds(0, HALO), :] = jnp.zeros((HALO, D), F32)
            hcar[...] = jnp.zeros_like(hcar)

        pbuf[pl.ds(HALO, tm), :] = z_ref[:, pl.ds(0, PW)]
        _, mixed_pre = _pool_tile(pbuf, t0, tm, pw_ref, ps_ref)
        mixed = jnp.concatenate(mixed_pre, axis=1) * ps_ref[...]
        y_pool = _nt(mixed.astype(_MXU), pprojT[...])
        pbuf[pl.ds(0, HALO), :] = pbuf[pl.ds(tm, HALO), :]

        lbuf[pl.ds(HALO, tm), :] = z_ref[:, pl.ds(PW, D)]
        sp, _ = _softplus_neg(lam_ref[...])
        for hh in range(HEADS):
            xc, r, ig, a, one_m, mult = _lru_gates_head(hh, lbuf, t0, tm, cw_ref, cb_ref, wrg_ref, brg_ref,
                                                        wig_ref, big_ref, sp)
            a_s[:, pl.ds(hh * HD, HD)] = a
            b_s[:, pl.ds(hh * HD, HD)] = mult * ig * xc
        lbuf[pl.ds(0, HALO), :] = lbuf[pl.ds(tm, HALO), :]
        hcar[...] = _scan_forward(a_s, b_s, h_ref, hcar[...], tm)
        gel, _ = _gelu_and_grad(z_ref[:, pl.ds(PW + D, D)])
        yl = (h_ref[...] * gel).astype(_MXU)
        yl_ref[...] = yl
        y_lru = _nn(yl, lru_w[...])

        g0 = _sigmoid(z_ref[:, pl.ds(PW + 2 * D, D)] + bg_ref[pl.ds(0, 1), :])
        g1 = _sigmoid(z_ref[:, pl.ds(PW + 3 * D, D)] + bg_ref[pl.ds(1, 1), :])
        merged = (g0 * y_pool + g1 * y_lru).astype(_MXU)
        mg_ref[...] = merged
        yp_ref[...] = y_pool.astype(_MXU)
        yr_ref[...] = y_lru.astype(_MXU)
        h1_ref[...] = x_ref[...] + _nn(merged, wout_w[...])

    tok = lambda w, dt: jax.ShapeDtypeStruct((s, w), dt)
    tspec = lambda w: pl.BlockSpec((tm, w), lambda i: (i, 0))
    full = lambda a: pl.BlockSpec(a.shape, lambda i: (0,) * a.ndim)
    return pl.pallas_call(
        body, name="mixer_fwd", grid=(s // tm,),
        out_shape=(tok(D, F32), tok(D, _MXU), tok(D, _MXU), tok(D, _MXU), tok(D, _MXU), tok(D, F32)),
        in_specs=[tspec(NIN), tspec(D), pl.BlockSpec(memory_space=pl.ANY)] + [full(a) for a in small],
        out_specs=(tspec(D),) * 6,
        scratch_shapes=[pltpu.VMEM((D, PW), _MXU), pltpu.VMEM((D, D), _MXU), pltpu.VMEM((D, D), _MXU),
                        pltpu.VMEM((tm + HALO, PW), F32), pltpu.VMEM((tm + HALO, D), F32),
                        pltpu.VMEM((tm, D), F32), pltpu.VMEM((tm, D), F32), pltpu.VMEM((8, D), F32),
                        pltpu.SemaphoreType.DMA((3 * NDEV,))],
        compiler_params=_cparams(1),
    )(z, x, gw, *small)


def _ffn_fwd(h1, g2, gw):
    s = h1.shape[0]
    tm = min(TM, s)
    half = FF // 2

    def body(h1_ref, g2_ref, gw_ref, v_ref, gf_ref, uf_ref, h2_ref, wffnT, wffo, sems):
        @pl.when(pl.program_id(0) == 0)
        def _():
            _load_weights(gw_ref, [("wffn", wffnT, D), ("wffo", wffo, D)], sems)

        hv = h1_ref[...]
        inv = lax.rsqrt(jnp.mean(hv * hv, axis=-1, keepdims=True) + EPS)
        v = (hv * inv * g2_ref[...]).astype(_MXU)
        v_ref[...] = v
        acc = hv
        for ch in range(2):
            cs = pl.ds(ch * half, half)
            gf = _nt(v, wffnT[pl.ds(ch * half, half), :]).astype(_MXU)
            uf = _nt(v, wffnT[pl.ds(FF + ch * half, half), :]).astype(_MXU)
            gf_ref[:, cs] = gf
            uf_ref[:, cs] = uf
            gf32 = gf.astype(F32)
            act = (gf32 * _sigmoid(gf32) * uf.astype(F32)).astype(_MXU)
            acc = acc + _nn(act, wffo[pl.ds(ch * half, half), :])
        h2_ref[...] = acc

    tspec = lambda w: pl.BlockSpec((tm, w), lambda i: (i, 0))
    return pl.pallas_call(
        body, name="ffn_fwd", grid=(s // tm,),
        out_shape=(jax.ShapeDtypeStruct((s, D), _MXU), jax.ShapeDtypeStruct((s, FF), _MXU),
                   jax.ShapeDtypeStruct((s, FF), _MXU), jax.ShapeDtypeStruct((s, D), F32)),
        in_specs=[tspec(D), pl.BlockSpec((1, D), lambda i: (0, 0)), pl.BlockSpec(memory_space=pl.ANY)],
        out_specs=(tspec(D), tspec(FF), tspec(FF), tspec(D)),
        scratch_shapes=[pltpu.VMEM((2 * FF, D), _MXU), pltpu.VMEM((FF, D), _MXU), pltpu.SemaphoreType.DMA((2 * NDEV,))],
        compiler_params=_cparams(1),
    )(h1, g2, gw)


def _rms_bwd(dy, xn, inv, g):
    dg = jnp.sum(dy * xn, axis=0, keepdims=True)
    dxn = dy * g
    dx = inv * (dxn - xn * jnp.mean(dxn * xn, axis=-1, keepdims=True))
    return dx, dg


def _ple_loss_fwd_bwd(h2, p, target, g3, gfin, gw):
    s = h2.shape[0]
    tm = min(TM, s)

    def body(h2_ref, p_ref, t_ref, g3_ref, gf_ref, gw_ref,
             dh2_ref, n3_ref, dpg_ref, de_ref, loss_ref, dg3_ref, dgf_ref, wpg, pleT, sems):
        i = pl.program_id(0)

        @pl.when(i == 0)
        def _():
            _load_weights(gw_ref, [("wpg", wpg, D), ("ple", pleT, PLE)], sems)
            loss_ref[...] = jnp.zeros_like(loss_ref)
            dg3_ref[...] = jnp.zeros_like(dg3_ref)
            dgf_ref[...] = jnp.zeros_like(dgf_ref)

        hv = h2_ref[...]
        inv3 = lax.rsqrt(jnp.mean(hv * hv, axis=-1, keepdims=True) + EPS)
        xn3 = hv * inv3
        n3 = (xn3 * g3_ref[...]).astype(_MXU)
        n3_ref[...] = n3
        pg = _sigmoid(_nn(n3, wpg[...]))
        e = _nt(p_ref[...].astype(_MXU), pleT[...])
        h3 = hv + pg * e
        invf = lax.rsqrt(jnp.mean(h3 * h3, axis=-1, keepdims=True) + EPS)
        xf = h3 * invf
        diff = xf * gf_ref[...] - t_ref[...]
        loss_ref[...] += jnp.sum(diff * diff) * (0.5 / D)
        dh3, dgf = _rms_bwd(diff * (1.0 / D), xf, invf, gf_ref[...])
        dgf_ref[...] += dgf
        de_ref[...] = (dh3 * pg).astype(_MXU)
        dpg = (dh3 * e * pg * (1.0 - pg)).astype(_MXU)
        dpg_ref[...] = dpg
        dn3 = _nt(dpg, wpg[...])
        dx3, dg3 = _rms_bwd(dn3, xn3, inv3, g3_ref[...])
        dg3_ref[...] += dg3
        dh2_ref[...] = dh3 + dx3

    tspec = lambda w: pl.BlockSpec((tm, w), lambda i: (i, 0))
    vec = pl.BlockSpec((1, D), lambda i: (0, 0))
    tok = lambda w, dt: jax.ShapeDtypeStruct((s, w), dt)
    return pl.pallas_call(
        body, name="ple_loss", grid=(s // tm,),
        out_shape=(tok(D, F32), tok(D, _MXU), tok(D, _MXU), tok(D, _MXU), jax.ShapeDtypeStruct((8, 128), F32),
                   jax.ShapeDtypeStruct((1, D), F32), jax.ShapeDtypeStruct((1, D), F32)),
        in_specs=[tspec(D), tspec(PLE), tspec(D), vec, vec, pl.BlockSpec(memory_space=pl.ANY)],
        out_specs=(tspec(D), tspec(D), tspec(D), tspec(D), pl.BlockSpec((8, 128), lambda i: (0, 0)), vec, vec),
        scratch_shapes=[pltpu.VMEM((D, D), _MXU), pltpu.VMEM((D, PLE), _MXU), pltpu.SemaphoreType.DMA((2 * NDEV,))],
        compiler_params=_cparams(1),
    )(h2, p, target, g3, gfin, gw)


def _ffn_bwd_hidden(dh2, gf, uf, gw):
    s = dh2.shape[0]
    tm = min(TM, s)
    half = FF // 2

    def body(dh2_ref, gf_ref, uf_ref, gw_ref, dff_ref, act_ref, wffo, sems):
        @pl.when(pl.program_id(0) == 0)
        def _():
            _load_weights(gw_ref, [("wffo", wffo, D)], sems)

        dm = dh2_ref[...].astype(_MXU)
        for ch in range(2):
            cs = pl.ds(ch * half, half)
            dact = _nt(dm, wffo[pl.ds(ch * half, half), :])
            gfv = gf_ref[:, cs].astype(F32)
            ufv = uf_ref[:, cs].astype(F32)
            sg = _sigmoid(gfv)
            silu = gfv * sg
            act_ref[:, cs] = (silu * ufv).astype(_MXU)
            dff_ref[:, pl.ds(ch * half, half)] = (dact * ufv * (sg * (1.0 + gfv * (1.0 - sg)))).astype(_MXU)
            dff_ref[:, pl.ds(FF + ch * half, half)] = (dact * silu).astype(_MXU)

    tspec = lambda w: pl.BlockSpec((tm, w), lambda i: (i, 0))
    return pl.pallas_call(
        body, name="ffn_bwd_hidden", grid=(s // tm,),
        out_shape=(jax.ShapeDtypeStruct((s, 2 * FF), _MXU), jax.ShapeDtypeStruct((s, FF), _MXU)),
        in_specs=[tspec(D), tspec(FF), tspec(FF), pl.BlockSpec(memory_space=pl.ANY)],
        out_specs=(tspec(2 * FF), tspec(FF)),
        scratch_shapes=[pltpu.VMEM((FF, D), _MXU), pltpu.SemaphoreType.DMA((NDEV,))],
        compiler_params=_cparams(1),
    )(dh2, gf, uf, gw)


def _proj_norm_bwd(dy, x, dres, g, gw, slab, width, name):
    s = x.shape[0]
    tm = min(TM, s)

    def body(dy_ref, x_ref, dr_ref, g_ref, gw_ref, dx_ref, dg_ref, wT, sems):
        @pl.when(pl.program_id(0) == 0)
        def _():
            _load_weights(gw_ref, [(slab, wT, D)], sems)
            dg_ref[...] = jnp.zeros_like(dg_ref)

        dv = _nn(dy_ref[...], wT[...])
        xv = x_ref[...]
        inv = lax.rsqrt(jnp.mean(xv * xv, axis=-1, keepdims=True) + EPS)
        dx, dg = _rms_bwd(dv, xv * inv, inv, g_ref[...])
        dg_ref[...] += dg
        dx_ref[...] = dr_ref[...] + dx

    tspec = lambda w: pl.BlockSpec((tm, w), lambda i: (i, 0))
    vec = pl.BlockSpec((1, D), lambda i: (0, 0))
    return pl.pallas_call(
        body, name=name, grid=(s // tm,),
        out_shape=(jax.ShapeDtypeStruct((s, D), F32), jax.ShapeDtypeStruct((1, D), F32)),
        in_specs=[tspec(width), tspec(D), tspec(D), vec, pl.BlockSpec(memory_space=pl.ANY)],
        out_specs=(tspec(D), vec),
        scratch_shapes=[pltpu.VMEM((width, D), _MXU), pltpu.SemaphoreType.DMA((NDEV,))],
        compiler_params=_cparams(1),
    )(dy, x, dres, g, gw)


def _mixer_bwd(dh1, z, h, y_pool, y_lru, gw, small):
    s = dh1.shape[0]
    tm = min(TM_SEQ, s)
    nt = s // tm
    (pool_w, pool_scale, conv_w, conv_b, w_rg, b_rg, w_ig, b_ig, lam, b_gate) = small

    def body(dh1_ref, z_ref, zp_ref, h_ref, hp_ref, yp_ref, yr_ref, gw_ref,
             pw_ref, ps_ref, cw_ref, cb_ref, wrg_ref, brg_ref, wig_ref, big_ref, lam_ref, bg_ref,
             dz_ref, dyr_ref, dyp_ref, mx_ref,
             gbg_ref, glam_ref, gbrg_ref, gbig_ref, gcb_ref, gcw_ref, gps_ref, gpw_ref, gwrg_ref, gwig_ref,
             pprojT, lru_w, wout_w, pbuf, lbuf, hbuf, qbuf, xbuf, abuf, an_s, g_s, dh_s, r_s, ig_s, xc_s, dcar, sems):
        step = pl.program_id(0)
        i = nt - 1 - step
        t0 = i * tm

        @pl.when(step == 0)
        def _():
            _load_weights(gw_ref, [("pproj", pprojT, PW), ("lru", lru_w, D), ("wout", wout_w, D)], sems)
            for ref in (gbg_ref, glam_ref, gbrg_ref, gbig_ref, gcb_ref, gcw_ref, gps_ref, gpw_ref, gwrg_ref, gwig_ref):
                ref[...] = jnp.zeros_like(ref)
            qbuf[pl.ds(tm, HALO), :] = jnp.zeros((HALO, PW), F32)
            xbuf[pl.ds(tm, 8), :] = jnp.zeros((8, D), F32)
            dcar[...] = jnp.zeros_like(dcar)
            abuf[pl.ds(8 + tm, 8), :] = jnp.ones((8, D), F32)

        first = i == 0
        zprev = jnp.where(first, 0.0, zp_ref[...])
        hprev = jnp.where(first, 0.0, hp_ref[...])

        d_merged = _nt(dh1_ref[...].astype(_MXU), wout_w[...])

        g0 = _sigmoid(z_ref[:, pl.ds(PW + 2 * D, D)] + bg_ref[pl.ds(0, 1), :])
        g1 = _sigmoid(z_ref[:, pl.ds(PW + 3 * D, D)] + bg_ref[pl.ds(1, 1), :])
        dz0 = d_merged * yp_ref[...].astype(F32) * g0 * (1.0 - g0)
        dz1 = d_merged * yr_ref[...].astype(F32) * g1 * (1.0 - g1)
        dz_ref[:, pl.ds(PW + 2 * D, D)] = dz0.astype(_MXU)
        dz_ref[:, pl.ds(PW + 3 * D, D)] = dz1.astype(_MXU)
        gbg_ref[pl.ds(0, 1), :] += jnp.sum(dz0, axis=0, keepdims=True)
        gbg_ref[pl.ds(1, 1), :] += jnp.sum(dz1, axis=0, keepdims=True)
        d_ypool = (d_merged * g0).astype(_MXU)
        d_ylru = (d_merged * g1).astype(_MXU)
        dyp_ref[...] = d_ypool
        dyr_ref[...] = d_ylru

        d_yl = _nt(d_ylru, lru_w[...])
        gel, dgel = _gelu_and_grad(z_ref[:, pl.ds(PW + D, D)])
        dz_ref[:, pl.ds(PW + D, D)] = (d_yl * h_ref[...] * dgel).astype(_MXU)
        g_s[...] = d_yl * gel
        lbuf[pl.ds(0, HALO), :] = zprev[:, PW:PW + D]
        lbuf[pl.ds(HALO, tm), :] = z_ref[:, pl.ds(PW, D)]
        hbuf[pl.ds(0, 8), :] = hprev
        hbuf[pl.ds(8, tm), :] = h_ref[...]
        sp, sneg = _softplus_neg(lam_ref[...])
        for hh in range(HEADS):
            cs = pl.ds(hh * HD, HD)
            xc, r, ig, a, one_m, mult = _lru_gates_head(hh, lbuf, t0, tm, cw_ref, cb_ref, wrg_ref, brg_ref,
                                                        wig_ref, big_ref, sp)
            abuf[pl.ds(8, tm), cs] = a
            r_s[:, cs] = r
            ig_s[:, cs] = ig
            xc_s[:, cs] = xc
        an_s[...] = abuf[pl.ds(9, tm), :]
        _scan_backward(an_s, g_s, dh_s, dcar[...], tm)
        dcar[...] = jnp.broadcast_to((abuf[pl.ds(8, 8), :] * dh_s[pl.ds(0, 8), :])[0:1, :], (8, D))
        for hh in range(HEADS):
            cs = pl.ds(hh * HD, HD)
            a = abuf[pl.ds(8, tm), cs]
            r = r_s[:, cs]
            ig = ig_s[:, cs]
            xc = xc_s[:, cs]
            dh = dh_s[:, cs]
            t = t0 + lax.broadcasted_iota(jnp.int32, (tm, HD), 0)
            one_m = 1.0 - a * a
            live = jnp.logical_and(t != 0, one_m > 0.0)
            mult = jnp.where(t == 0, 1.0, jnp.sqrt(jnp.maximum(one_m, 0.0)))
            d_mult = dh * ig * xc
            d_loga = dh * hbuf[pl.ds(7, tm), cs] * a + jnp.where(live, -d_mult * a * a / mult, 0.0)
            glam_ref[:, cs] += jnp.sum(d_loga * (LRU_C * r) * sneg[:, hh * HD:(hh + 1) * HD], axis=0, keepdims=True)
            d_rpre = d_loga * (-LRU_C * sp[:, hh * HD:(hh + 1) * HD]) * r * (1.0 - r)
            d_igpre = dh * mult * xc * ig * (1.0 - ig)
            gbrg_ref[pl.ds(hh, 1), :] += jnp.sum(d_rpre, axis=0, keepdims=True)
            gbig_ref[pl.ds(hh, 1), :] += jnp.sum(d_igpre, axis=0, keepdims=True)
            drm = d_rpre.astype(_MXU)
            dim = d_igpre.astype(_MXU)
            xcm = xc.astype(_MXU)
            gwrg_ref[hh] += _tn(xcm, drm)
            gwig_ref[hh] += _tn(xcm, dim)
            d_xc = dh * mult * ig + _nt(drm, wrg_ref[hh]) + _nt(dim, wig_ref[hh])
            gcb_ref[:, cs] += jnp.sum(d_xc, axis=0, keepdims=True)
            for k in range(CONV):
                gcw_ref[pl.ds(k, 1), cs] += jnp.sum(d_xc * lbuf[pl.ds(HALO - (CONV - 1) + k, tm), cs], axis=0,
                                                    keepdims=True)
            xbuf[pl.ds(0, tm), cs] = d_xc
        dzl = cw_ref[pl.ds(CONV - 1, 1), :] * xbuf[pl.ds(0, tm), :]
        for k in range(CONV - 1):
            dzl = dzl + cw_ref[pl.ds(k, 1), :] * xbuf[pl.ds(CONV - 1 - k, tm), :]
        dz_ref[:, pl.ds(PW, D)] = dzl.astype(_MXU)
        xbuf[pl.ds(tm, 8), :] = xbuf[pl.ds(0, 8), :]

        d_mixed = _nn(d_ypool, pprojT[...])
        pbuf[pl.ds(0, HALO), :] = zprev[:, 0:PW]
        pbuf[pl.ds(HALO, tm), :] = z_ref[:, pl.ds(0, PW)]
        pooled, mixed_pre = _pool_tile(pbuf, t0, tm, pw_ref, ps_ref)
        mp = jnp.concatenate(mixed_pre, axis=1)
        mx_ref[...] = (mp * ps_ref[...]).astype(_MXU)
        gps_ref[...] += jnp.sum(d_mixed * mp, axis=0, keepdims=True)
        d_mp = (d_mixed * ps_ref[...]).astype(_MXU)
        t = t0 + lax.broadcasted_iota(jnp.int32, (tm, GD), 0)
        d_pooled = []
        for g, w in enumerate(WINDOWS):
            dmg = d_mp[:, g * GD:(g + 1) * GD]
            gpw_ref[g] += _tn(pooled[g].astype(_MXU), dmg)
            dp = _nt(dmg, pw_ref[g])
            d_pooled.append(dp)
            qbuf[pl.ds(0, tm), pl.ds(g * GD, GD)] = dp / jnp.minimum(t + 1, w).astype(F32)
        for g, w in enumerate(WINDOWS):
            cs = pl.ds(g * GD, GD)
            acc = qbuf[pl.ds(0, tm), cs]
            for d in range(1, w):
                acc = acc + qbuf[pl.ds(d, tm), cs]
            dz_ref[:, cs] = (acc - d_pooled[g]).astype(_MXU)
        qbuf[pl.ds(tm, HALO), :] = qbuf[pl.ds(0, HALO), :]

    rev = lambda w: pl.BlockSpec((tm, w), lambda g: (nt - 1 - g, 0))
    prev = lambda rows, w: pl.BlockSpec((rows, w), lambda g: (jnp.maximum((nt - 1 - g) * (tm // rows) - 1, 0), 0))
    full = lambda a: pl.BlockSpec(a.shape, lambda g: (0,) * a.ndim)
    tok = lambda w, dt: jax.ShapeDtypeStruct((s, w), dt)
    acc_shapes = [(2, D), (1, D), (HEADS, HD), (HEADS, HD), (1, D), (CONV, D), (1, PW), (GROUPS, GD, GD),
                  (HEADS, HD, HD), (HEADS, HD, HD)]
    acc_specs = tuple(pl.BlockSpec(sh, lambda g, n=len(sh): (0,) * n) for sh in acc_shapes)
    return pl.pallas_call(
        body, name="mixer_bwd", grid=(nt,),
        out_shape=(tok(NIN, _MXU), tok(D, _MXU), tok(D, _MXU), tok(PW, _MXU))
        + tuple(jax.ShapeDtypeStruct(sh, F32) for sh in acc_shapes),
        in_specs=[rev(D), rev(NIN), prev(HALO, NIN), rev(D), prev(8, D), rev(D), rev(D),
                  pl.BlockSpec(memory_space=pl.ANY)] + [full(a) for a in small],
        out_specs=(rev(NIN), rev(D), rev(D), rev(PW)) + acc_specs,
        scratch_shapes=[pltpu.VMEM((D, PW), _MXU), pltpu.VMEM((D, D), _MXU), pltpu.VMEM((D, D), _MXU),
                        pltpu.VMEM((tm + HALO, PW), F32), pltpu.VMEM((tm + HALO, D), F32),
                        pltpu.VMEM((tm + 8, D), F32), pltpu.VMEM((tm + HALO, PW), F32), pltpu.VMEM((tm + 8, D), F32),
                        pltpu.VMEM((tm + 16, D), F32), pltpu.VMEM((tm, D), F32), pltpu.VMEM((tm, D), F32),
                        pltpu.VMEM((tm, D), F32), pltpu.VMEM((tm, D), F32), pltpu.VMEM((tm, D), F32),
                        pltpu.VMEM((tm, D), F32), pltpu.VMEM((8, D), F32), pltpu.SemaphoreType.DMA((3 * NDEV,))],
        compiler_params=_cparams(1),
    )(dh1, z, z, h, h, y_pool, y_lru, gw, *small)


def _split3(a):
    hi = a.astype(jnp.bfloat16).astype(F32)
    mid = (a - hi).astype(jnp.bfloat16).astype(F32)
    lo = (a - hi - mid).astype(jnp.bfloat16).astype(F32)
    return jnp.stack([hi, mid, lo])


def _small_pack(parts):
    flat = jnp.concatenate([a.reshape(-1) for a in parts])
    return jnp.pad(flat, (0, NDEV * SMALL_ROWS * D - flat.shape[0])).reshape(NDEV * SMALL_ROWS, D)


def _small_unpack(packed, shapes):
    flat = packed.reshape(-1)
    out, o = [], 0
    for sh in shapes:
        n = math.prod(sh)
        out.append(flat[o:o + n].reshape(sh))
        o += n
    return out


def kernel(x, p, norm1_g, w_in, b_gate, pool_w, pool_scale, pool_proj, conv_w, conv_b, w_rg, b_rg, w_ig, b_ig, lru_lambda, lru_proj, w_out, norm2_g, w_ffn_in, w_ffn_out, ple_norm_g, w_ple_gate, w_ple_proj, final_g, loss_target, m_norm1_g, m_w_in, m_b_gate, m_pool_w, m_pool_scale, m_pool_proj, m_conv_w, m_conv_b, m_w_rg, m_b_rg, m_w_ig, m_b_ig, m_lru_lambda, m_lru_proj, m_w_out, m_norm2_g, m_w_ffn_in, m_w_ffn_out, m_ple_norm_g, m_w_ple_gate, m_w_ple_proj, m_final_g, v_norm1_g, v_w_in, v_b_gate, v_pool_w, v_pool_scale, v_pool_proj, v_conv_w, v_conv_b, v_w_rg, v_b_rg, v_w_ig, v_b_ig, v_lru_lambda, v_lru_proj, v_w_out, v_norm2_g, v_w_ffn_in, v_w_ffn_out, v_ple_norm_g, v_w_ple_gate, v_w_ple_proj, v_final_g):
    axes = ("x", "y", "c")
    me = 4 * lax.axis_index("x") + 2 * lax.axis_index("y") + lax.axis_index("c")
    x2 = x[0]
    p2 = p[0, 0]
    tgt = loss_target[0]

    n_small = (CONV + 2) * 128
    small_terms = _split3(jnp.concatenate([conv_w[0].reshape(-1), b_gate[0].reshape(-1)]))
    small_rows = jnp.pad(small_terms, ((0, 16 - 3), (0, D - n_small)))
    own = jnp.concatenate([
        w_in[0].T.astype(_MXU), w_ffn_in[0].T.astype(_MXU),
        jnp.pad(pool_proj[0].T, ((0, 0), (0, D - PW))).astype(_MXU),
        jnp.pad(w_ple_proj[0].T, ((0, 0), (0, D - PLE))).astype(_MXU),
        lru_proj[0].astype(_MXU), w_out[0].astype(_MXU), w_ffn_out[0].astype(_MXU), w_ple_gate[0].astype(_MXU),
        small_rows.astype(_MXU),
    ], axis=0)
    gw = _all_gather_weights(own)
    off = W_OFF["f32s"][0]
    st = gw[:, off:off + 3, :n_small].astype(F32)
    sf = st[:, 0] + st[:, 1] + st[:, 2]
    conv_w_full = sf[:, :CONV * 128].reshape(NDEV, CONV, 128).transpose(1, 0, 2).reshape(CONV, D)
    b_gate_full = sf[:, CONV * 128:].reshape(NDEV, 2, 128).transpose(1, 0, 2).reshape(2, D)

    small = (pool_w[0].astype(_MXU), pool_scale, conv_w_full, conv_b, w_rg[0].astype(_MXU), b_rg[0],
             w_ig[0].astype(_MXU), b_ig[0], lru_lambda, b_gate_full)

    u, z = _inproj_fwd(x2, norm1_g, gw)
    h, yl, merged, y_pool, y_lru, h1 = _mixer_fwd(z, x2, gw, small)
    v, gf, uf, h2 = _ffn_fwd(h1, norm2_g, gw)

    dh2, n3, dpg, de, loss_blk, g_ple_norm, g_final = _ple_loss_fwd_bwd(h2, p2, tgt, ple_norm_g, final_g.reshape(1, D), gw)
    loss = lax.psum(loss_blk[0, 0], axes)
    dff, act = _ffn_bwd_hidden(dh2, gf, uf, gw)
    dh1, g_norm2 = _proj_norm_bwd(dff, h1, dh2, norm2_g, gw, "wffn", 2 * FF, "ffn_bwd_in")
    (dz, d_ylru, d_ypool, mixed, g_bgate, g_lam, g_brg, g_big, g_convb, g_convw, g_pscale, g_poolw, g_wrg,
     g_wig) = _mixer_bwd(dh1, z, h, y_pool, y_lru, gw, small)
    grad_x, g_norm1 = _proj_norm_bwd(dz, x2, dh1, norm1_g, gw, "win", NIN, "inproj_bwd")

    parts = [
        _grad_matmul(dz, u, "grad_w_in", NIN // 4),
        _grad_matmul(dff, v, "grad_w_ffn_in", 2 * FF // 4),
        _grad_matmul(d_ypool, mixed, "grad_pool_proj", D),
        _grad_matmul(de, p2, "grad_w_ple_proj", D),
        _grad_matmul(yl, d_ylru, "grad_lru_proj", D),
        _grad_matmul(merged, dh1, "grad_w_out", D),
        _grad_matmul(act, dh2, "grad_w_ffn_out", FF // 2),
        _grad_matmul(n3, dpg, "grad_w_ple_gate", D),
    ]
    small_shapes = [(1, D), (GROUPS, GD, GD), (1, PW), (1, D), (HEADS, HD, HD), (HEADS, HD), (HEADS, HD, HD),
                    (HEADS, HD), (1, D), (1, D), (1, D), (1, D), (2, D), (CONV, D)]
    parts.append(_small_pack([g_norm1, g_poolw, g_pscale, g_convb, g_wrg, g_brg, g_wig, g_big, g_lam, g_norm2,
                              g_ple_norm, g_final, g_bgate, g_convw]))

    own_a, land_a = _rs_d2d(parts)
    t_sum = _sum_arrays([own_a.reshape(4 * RG, D), land_a.reshape(4 * RG, D)], "rs_sum_d2d").reshape(4, RG, D)
    land_b = _rs_ici(t_sum)
    mine = lax.dynamic_index_in_dim(t_sum, 2 * lax.axis_index("x") + lax.axis_index("y"), 0, keepdims=False)
    red = _sum_arrays([mine, land_b[0], land_b[1], land_b[2]], "rs_sum_ici")

    def slab(name):
        o, r = G_OFF[name]
        return red[o:o + r]

    g_w_in = slab("win").T
    g_w_ffn_in = slab("wffn").T
    g_pool_proj = slab("pproj")[:, :PW].T
    g_w_ple_proj = slab("ple")[:, :PLE].T
    g_lru_proj, g_w_out, g_w_ffn_out, g_w_ple_gate = slab("lru"), slab("wout"), slab("wffo"), slab("wpg")
    small_red = _all_gather_small(slab("small"))
    (gs_norm1, gs_poolw, gs_pscale, gs_convb, gs_wrg, gs_brg, gs_wig, gs_big, gs_lam, gs_norm2, gs_ple_norm,
     gs_final, gs_bgate, gs_convw) = _small_unpack(small_red, small_shapes)
    g_b_gate = lax.dynamic_slice_in_dim(gs_bgate, me * 128, 128, axis=1)
    g_conv_w = lax.dynamic_slice_in_dim(gs_convw, me * 128, 128, axis=1)

    grads = {
        "norm1_g": gs_norm1, "w_in": g_w_in[None], "b_gate": g_b_gate[None], "pool_w": gs_poolw[None],
        "pool_scale": gs_pscale, "pool_proj": g_pool_proj[None], "conv_w": g_conv_w[None], "conv_b": gs_convb,
        "w_rg": gs_wrg[None], "b_rg": gs_brg[None], "w_ig": gs_wig[None], "b_ig": gs_big[None], "lru_lambda": gs_lam,
        "lru_proj": g_lru_proj[None], "w_out": g_w_out[None], "norm2_g": gs_norm2, "w_ffn_in": g_w_ffn_in[None],
        "w_ffn_out": g_w_ffn_out[None], "ple_norm_g": gs_ple_norm, "w_ple_gate": g_w_ple_gate[None],
        "w_ple_proj": g_w_ple_proj[None], "final_g": gs_final.reshape(D),
    }
    weights = dict(norm1_g=norm1_g, w_in=w_in, b_gate=b_gate, pool_w=pool_w, pool_scale=pool_scale, pool_proj=pool_proj,
                   conv_w=conv_w, conv_b=conv_b, w_rg=w_rg, b_rg=b_rg, w_ig=w_ig, b_ig=b_ig, lru_lambda=lru_lambda,
                   lru_proj=lru_proj, w_out=w_out, norm2_g=norm2_g, w_ffn_in=w_ffn_in, w_ffn_out=w_ffn_out,
                   ple_norm_g=ple_norm_g, w_ple_gate=w_ple_gate, w_ple_proj=w_ple_proj, final_g=final_g)
    moments_m = dict(norm1_g=m_norm1_g, w_in=m_w_in, b_gate=m_b_gate, pool_w=m_pool_w, pool_scale=m_pool_scale,
                     pool_proj=m_pool_proj, conv_w=m_conv_w, conv_b=m_conv_b, w_rg=m_w_rg, b_rg=m_b_rg, w_ig=m_w_ig,
                     b_ig=m_b_ig, lru_lambda=m_lru_lambda, lru_proj=m_lru_proj, w_out=m_w_out, norm2_g=m_norm2_g,
                     w_ffn_in=m_w_ffn_in, w_ffn_out=m_w_ffn_out, ple_norm_g=m_ple_norm_g, w_ple_gate=m_w_ple_gate,
                     w_ple_proj=m_w_ple_proj, final_g=m_final_g)
    moments_v = dict(norm1_g=v_norm1_g, w_in=v_w_in, b_gate=v_b_gate, pool_w=v_pool_w, pool_scale=v_pool_scale,
                     pool_proj=v_pool_proj, conv_w=v_conv_w, conv_b=v_conv_b, w_rg=v_w_rg, b_rg=v_b_rg, w_ig=v_w_ig,
                     b_ig=v_b_ig, lru_lambda=v_lru_lambda, lru_proj=v_lru_proj, w_out=v_w_out, norm2_g=v_norm2_g,
                     w_ffn_in=v_w_ffn_in, w_ffn_out=v_w_ffn_out, ple_norm_g=v_ple_norm_g, w_ple_gate=v_w_ple_gate,
                     w_ple_proj=v_w_ple_proj, final_g=v_final_g)
    names = list(weights)
    big = ("w_in", "w_ffn_in", "w_ffn_out", "lru_proj", "w_out", "w_ple_gate", "pool_proj", "w_ple_proj")
    delta, new_m, new_v = {}, {}, {}
    for n in big:
        sh = weights[n].shape
        as2d = lambda a: a.reshape(sh[-2], sh[-1])
        d_, m_, v_ = _adamw(as2d(weights[n]), as2d(grads[n]), as2d(moments_m[n]), as2d(moments_v[n]), "adamw_" + n)
        delta[n], new_m[n], new_v[n] = d_.reshape(sh), m_.reshape(sh), v_.reshape(sh)
    rest = [n for n in names if n not in big]
    rest_shapes = [weights[n].shape for n in rest]
    packed = [_small_pack([src[n] for n in rest]) for src in (weights, grads, moments_m, moments_v)]
    d_, m_, v_ = _adamw(*packed, "adamw_small")
    for n, a, b_, c_ in zip(rest, _small_unpack(d_, rest_shapes), _small_unpack(m_, rest_shapes),
                            _small_unpack(v_, rest_shapes)):
        delta[n], new_m[n], new_v[n] = a, b_, c_

    return (loss, grad_x[None], *[grads[n] for n in names], *[delta[n] for n in names],
            *[new_m[n] for n in names], *[new_v[n] for n in names])
```

```python
import functools
import math

import jax
import jax.numpy as jnp
from jax import lax
from jax.experimental import pallas as pl
from jax.experimental.pallas import tpu as pltpu

F32 = jnp.float32
D = 1024
NIN = 4608
PW = 512
FF = 2816
PLE = 256
HEADS, HD = 8, 128
GROUPS, GD = 4, 128
WINDOWS = (2, 4, 8, 16)
HALO = 16
CONV = 4
EPS = 1e-6
LRU_C = 8.0
NDEV = 8
MESH = pl.DeviceIdType.MESH

ADAM_LR, ADAM_B1, ADAM_B2, ADAM_EPS, ADAM_WD, ADAM_STEP = 0.001, 0.9, 0.999, 1e-08, 0.01, 10

_MXU = jnp.bfloat16
TM = 512
TM_SEQ = 256
VMEM_LIMIT = 56 * 1024 * 1024

W_SLABS = (("win", 576), ("wffn", 704), ("pproj", 128), ("ple", 128), ("lru", 128), ("wout", 128),
           ("wffo", 352), ("wpg", 128), ("f32s", 16))
W_OFF = {}
_o = 0
for _n, _r in W_SLABS:
    W_OFF[_n] = (_o, _r)
    _o += _r
RW = _o
SMALL_ROWS = 48
G_SLABS = (("win", 576), ("wffn", 704), ("pproj", 128), ("ple", 128), ("lru", 128), ("wout", 128),
           ("wffo", 352), ("wpg", 128), ("small", SMALL_ROWS))
G_OFF = {}
_o = 0
for _n, _r in G_SLABS:
    G_OFF[_n] = (_o, _r)
    _o += _r
RG = _o


def _cparams(n_axes=1, vmem=VMEM_LIMIT):
    return pltpu.CompilerParams(dimension_semantics=("arbitrary",) * n_axes, vmem_limit_bytes=vmem)


def _my_pos():
    return lax.axis_index("x"), lax.axis_index("y"), lax.axis_index("c")


def _nt(a, b):
    return lax.dot_general(a, b, (((1,), (1,)), ((), ())), preferred_element_type=F32)


def _nn(a, b):
    return lax.dot_general(a, b, (((1,), (0,)), ((), ())), preferred_element_type=F32)


def _tn(a, b):
    return lax.dot_general(a, b, (((0,), (0,)), ((), ())), preferred_element_type=F32)


def _sigmoid(x):
    return 1.0 / (1.0 + jnp.exp(-x))


_GELU_K = math.sqrt(2.0 / math.pi)


def _gelu_and_grad(x):
    x2 = x * x
    inner = _GELU_K * (x + 0.044715 * x2 * x)
    t = jnp.tanh(inner)
    g = 0.5 * x * (1.0 + t)
    dg = 0.5 * (1.0 + t) + 0.5 * x * (1.0 - t * t) * _GELU_K * (1.0 + 3.0 * 0.044715 * x2)
    return g, dg


def _softplus_neg(lam):
    x = -lam
    t = jnp.exp(-jnp.abs(x))
    u = 1.0 + t
    l1p = jnp.where(u == 1.0, t, jnp.log(u) * t / (u - 1.0))
    return jnp.maximum(x, 0.0) + l1p, _sigmoid(x)


def _start_slab_loads(g_ref, name, dst_ref, sems, base, width=D):
    off, rows = W_OFF[name]
    copies = []
    for k in range(NDEV):
        if width == D:
            src = g_ref.at[k, pl.ds(off, rows), :]
        else:
            src = g_ref.at[k, pl.ds(off, rows), pl.ds(0, width)]
        cp = pltpu.make_async_copy(src, dst_ref.at[pl.ds(k * rows, rows), :], sems.at[base + k])
        cp.start()
        copies.append(cp)
    return copies


def _load_weights(g_ref, items, sems):
    copies = []
    for n, (name, dst, width) in enumerate(items):
        copies += _start_slab_loads(g_ref, name, dst, sems, n * NDEV, width)
    for cp in copies:
        cp.wait()


def _all_gather_weights(own):
    rows, cols = own.shape

    def body(own_ref, out_ref, stage, send_sems, recv_sems, local_sem):
        x, y, c = _my_pos()
        me, sibling = (x, y, c), (x, y, 1 - c)
        chips = [(1 - x, y), (x, 1 - y), (1 - x, 1 - y)]

        def slab(px, py, pc):
            return out_ref.at[4 * px + 2 * py + pc]

        def copy(k, block, to, src=None):
            return pltpu.make_async_remote_copy(
                src_ref=slab(*block) if src is None else src, dst_ref=slab(*block),
                send_sem=send_sems.at[k], recv_sem=recv_sems.at[k], device_id=to, device_id_type=MESH)

        pltpu.sync_copy(own_ref, stage)
        mine = pltpu.make_async_copy(stage, slab(*me), local_sem)
        mine.start()
        first = [copy(0, me, sibling, src=stage)]
        first += [copy(1 + j, me, (*chip, c), src=stage) for j, chip in enumerate(chips)]
        for cp in first:
            cp.start()
        passed = [copy(4 + j, (*chip, c), sibling) for j, chip in enumerate(chips)]
        for j, chip in enumerate(chips):
            copy(1 + j, (*chip, c), me).wait_recv()
            passed[j].start()
        copy(0, sibling, me).wait_recv()
        for j, chip in enumerate(chips):
            copy(4 + j, (*chip, 1 - c), me).wait_recv()
        for cp in first + passed:
            cp.wait_send()
        mine.wait()

    return pl.pallas_call(
        body, name="ag_weights",
        out_shape=jax.ShapeDtypeStruct((NDEV, rows, cols), own.dtype),
        in_specs=[pl.BlockSpec(memory_space=pl.ANY)],
        out_specs=pl.BlockSpec(memory_space=pl.ANY),
        scratch_shapes=[pltpu.VMEM((rows, cols), own.dtype), pltpu.SemaphoreType.DMA((7,)),
                        pltpu.SemaphoreType.DMA((7,)), pltpu.SemaphoreType.DMA],
        compiler_params=pltpu.CompilerParams(vmem_limit_bytes=VMEM_LIMIT),
    )(own)


def _rs_d2d(parts):
    n = len(parts)

    def body(*refs):
        part_refs = refs[:n]
        land_ref, send_sems, recv_sems = refs[n:]
        x, y, c = _my_pos()
        sibling = (x, y, 1 - c)
        remote = []
        for j in range(4):
            xj, yj = j // 2, j % 2
            for w, (name, rows) in enumerate(G_SLABS):
                off = G_OFF[name][0]
                k = j * n + w
                s_sib = pl.multiple_of((4 * xj + 2 * yj + 1 - c) * rows, 8)
                rc = pltpu.make_async_remote_copy(
                    src_ref=part_refs[w].at[pl.ds(s_sib, rows), :], dst_ref=land_ref.at[j, pl.ds(off, rows), :],
                    send_sem=send_sems.at[k], recv_sem=recv_sems.at[k], device_id=sibling, device_id_type=MESH)
                rc.start()
                remote.append(rc)
        for rc in remote:
            rc.wait_recv()
        for rc in remote:
            rc.wait_send()

    land = pl.pallas_call(
        body, name="rs_d2d",
        out_shape=jax.ShapeDtypeStruct((4, RG, D), F32),
        in_specs=[pl.BlockSpec(memory_space=pl.ANY)] * n,
        out_specs=pl.BlockSpec(memory_space=pl.ANY),
        scratch_shapes=[pltpu.SemaphoreType.DMA((4 * n,)), pltpu.SemaphoreType.DMA((4 * n,))],
    )(*parts)
    c = lax.axis_index("c")
    own = jnp.concatenate(
        [lax.dynamic_index_in_dim(p.reshape(4, 2, rows, D), c, 1, keepdims=False) for p, (_, rows) in zip(parts, G_SLABS)],
        axis=1)
    return own, land


def _rs_ici(t_narrow, t_full):
    big = RG - SMALL_ROWS

    def body(tn_ref, tf_ref, land_n, land_f, send_sems, recv_sems):
        x, y, c = _my_pos()
        targets = [(1 - x, y), (x, 1 - y), (1 - x, 1 - y)]
        copies = []
        for r, (xt, yt) in enumerate(targets):
            j = 2 * xt + yt
            for k, (src, dst) in enumerate(((tn_ref.at[j, pl.ds(0, big), :], land_n.at[r]),
                                           (tf_ref.at[j, pl.ds(big, SMALL_ROWS), :], land_f.at[r]))):
                cp = pltpu.make_async_remote_copy(
                    src_ref=src, dst_ref=dst, send_sem=send_sems.at[2 * r + k], recv_sem=recv_sems.at[2 * r + k],
                    device_id=(xt, yt, c), device_id_type=MESH)
                cp.start()
                copies.append(cp)
        for cp in copies:
            cp.wait_recv()
        for cp in copies:
            cp.wait_send()

    return pl.pallas_call(
        body, name="rs_ici",
        out_shape=(jax.ShapeDtypeStruct((3, big, D), t_narrow.dtype), jax.ShapeDtypeStruct((3, SMALL_ROWS, D), F32)),
        in_specs=[pl.BlockSpec(memory_space=pl.ANY)] * 2,
        out_specs=(pl.BlockSpec(memory_space=pl.ANY),) * 2,
        scratch_shapes=[pltpu.SemaphoreType.DMA((6,)), pltpu.SemaphoreType.DMA((6,))],
    )(t_narrow, t_full)


def _all_gather_small(piece):
    rows = piece.shape[0]

    def body(p_ref, out_ref, send_sems, recv_sems, local_sem):
        x, y, c = _my_pos()
        me = 4 * x + 2 * y + c
        mine = pltpu.make_async_copy(p_ref, out_ref.at[pl.ds(pl.multiple_of(me * rows, 8), rows), :], local_sem)
        mine.start()
        sends = []
        peers = []
        for r in range(1, NDEV):
            px = 1 - x if (r >> 2) & 1 else x
            py = 1 - y if (r >> 1) & 1 else y
            pc = 1 - c if r & 1 else c
            peers.append((px, py, pc))
            cp = pltpu.make_async_remote_copy(
                src_ref=p_ref, dst_ref=out_ref.at[pl.ds(pl.multiple_of(me * rows, 8), rows), :],
                send_sem=send_sems.at[r - 1], recv_sem=recv_sems.at[r - 1], device_id=(px, py, pc),
                device_id_type=MESH)
            cp.start()
            sends.append(cp)
        for r, (px, py, pc) in enumerate(peers):
            them = 4 * px + 2 * py + pc
            pltpu.make_async_remote_copy(
                src_ref=p_ref, dst_ref=out_ref.at[pl.ds(pl.multiple_of(them * rows, 8), rows), :],
                send_sem=send_sems.at[r], recv_sem=recv_sems.at[r], device_id=(px, py, pc),
                device_id_type=MESH).wait_recv()
        for cp in sends:
            cp.wait_send()
        mine.wait()

    return pl.pallas_call(
        body, name="ag_small",
        out_shape=jax.ShapeDtypeStruct((NDEV * rows, piece.shape[1]), piece.dtype),
        in_specs=[pl.BlockSpec(memory_space=pltpu.VMEM)],
        out_specs=pl.BlockSpec(memory_space=pl.ANY),
        scratch_shapes=[pltpu.SemaphoreType.DMA((7,)), pltpu.SemaphoreType.DMA((7,)), pltpu.SemaphoreType.DMA],
    )(piece)


def _row_block(rows, target=512, mult=8):
    b = min(rows, target) // mult * mult
    while rows % b:
        b -= mult
    return b


def _sum_arrays(arrs, name, narrow=None, target=464):
    rows, cols = arrs[0].shape
    br = _row_block(rows, target, 16)
    n = len(arrs)

    def body(*refs):
        acc = refs[0][...].astype(F32)
        for r in refs[1:n]:
            acc = acc + r[...].astype(F32)
        refs[n][...] = acc
        if narrow is not None:
            refs[n + 1][...] = acc.astype(narrow)

    spec = pl.BlockSpec((br, cols), lambda i: (i, 0))
    shape = jax.ShapeDtypeStruct((rows, cols), F32)
    if narrow is None:
        out_shape, out_specs = shape, spec
    else:
        out_shape, out_specs = (shape, jax.ShapeDtypeStruct((rows, cols), narrow)), (spec, spec)
    return pl.pallas_call(
        body, name=name, grid=(rows // br,), out_shape=out_shape,
        in_specs=[spec] * n, out_specs=out_specs, compiler_params=_cparams(1),
    )(*arrs)


def _adamw(w, g, m, v, name):
    rows, cols = w.shape
    br = _row_block(rows, 256)

    def body(w_ref, g_ref, m_ref, v_ref, d_ref, nm_ref, nv_ref):
        g_ = g_ref[...]
        m_ = ADAM_B1 * m_ref[...] + (1.0 - ADAM_B1) * g_
        v_ = ADAM_B2 * v_ref[...] + (1.0 - ADAM_B2) * (g_ * g_)
        m_hat = m_ / (1.0 - ADAM_B1 ** ADAM_STEP)
        v_hat = v_ / (1.0 - ADAM_B2 ** ADAM_STEP)
        d_ref[...] = -ADAM_LR * (m_hat / (jnp.sqrt(v_hat) + ADAM_EPS) + ADAM_WD * w_ref[...])
        nm_ref[...] = m_
        nv_ref[...] = v_

    spec = pl.BlockSpec((br, cols), lambda i: (i, 0))
    shape = jax.ShapeDtypeStruct((rows, cols), F32)
    return pl.pallas_call(
        body, name=name, grid=(rows // br,), out_shape=(shape, shape, shape),
        in_specs=[spec] * 4, out_specs=(spec, spec, spec), compiler_params=_cparams(1),
    )(w, g, m, v)


def _grad_matmul(lhs, rhs, name, block_rows):
    s, r = lhs.shape
    k = rhs.shape[1]
    tm = min(TM, s)

    def body(l_ref, r_ref, o_ref):
        @pl.when(pl.program_id(1) == 0)
        def _():
            o_ref[...] = jnp.zeros_like(o_ref)

        o_ref[:, pl.ds(0, k)] += _tn(l_ref[...].astype(_MXU), r_ref[...].astype(_MXU))

    return pl.pallas_call(
        body, name=name, grid=(r // block_rows, s // tm),
        out_shape=jax.ShapeDtypeStruct((r, D), F32),
        in_specs=[pl.BlockSpec((tm, block_rows), lambda b, i: (i, b)), pl.BlockSpec((tm, k), lambda b, i: (i, 0))],
        out_specs=pl.BlockSpec((block_rows, D), lambda b, i: (b, 0)),
        compiler_params=_cparams(2),
    )(lhs, rhs)


def _inproj_fwd(x, g1, gw):
    s = x.shape[0]
    tm = min(TM, s)
    nchunk = 4
    cw = NIN // nchunk

    def body(x_ref, g1_ref, gw_ref, u_ref, z_ref, w_vmem, sems):
        @pl.when(pl.program_id(0) == 0)
        def _():
            _load_weights(gw_ref, [("win", w_vmem, D)], sems)

        xv = x_ref[...]
        inv = lax.rsqrt(jnp.mean(xv * xv, axis=-1, keepdims=True) + EPS)
        u = (xv * inv * g1_ref[...]).astype(_MXU)
        u_ref[...] = u
        for ch in range(nchunk):
            z_ref[:, pl.ds(ch * cw, cw)] = _nt(u, w_vmem[pl.ds(ch * cw, cw), :])

    return pl.pallas_call(
        body, name="inproj_fwd", grid=(s // tm,),
        out_shape=(jax.ShapeDtypeStruct((s, D), _MXU), jax.ShapeDtypeStruct((s, NIN), F32)),
        in_specs=[pl.BlockSpec((tm, D), lambda i: (i, 0)), pl.BlockSpec((1, D), lambda i: (0, 0)),
                  pl.BlockSpec(memory_space=pl.ANY)],
        out_specs=(pl.BlockSpec((tm, D), lambda i: (i, 0)), pl.BlockSpec((tm, NIN), lambda i: (i, 0))),
        scratch_shapes=[pltpu.VMEM((NIN, D), _MXU), pltpu.SemaphoreType.DMA((NDEV,))],
        compiler_params=_cparams(1),
    )(x, g1, gw)


def _pool_tile(pbuf, t0, tm, pw_ref, scale_ref):
    t = t0 + lax.broadcasted_iota(jnp.int32, (tm, GD), 0)
    pooled, mixed_pre = [], []
    for g, w in enumerate(WINDOWS):
        cs = pl.ds(g * GD, GD)
        cur = pbuf[pl.ds(HALO, tm), cs]
        acc = cur
        for d in range(1, w):
            acc = acc + pbuf[pl.ds(HALO - d, tm), cs]
        cnt = jnp.minimum(t + 1, w).astype(F32)
        pg = acc / cnt - cur
        pooled.append(pg)
        mixed_pre.append(_nn(pg.astype(_MXU), pw_ref[g]))
    return pooled, mixed_pre


def _lru_gates_head(hh, lbuf, t0, tm, cw_ref, cb_ref, wrg_ref, brg_ref, wig_ref, big_ref, sp):
    cs = pl.ds(hh * HD, HD)
    xc = cb_ref[:, cs] + cw_ref[pl.ds(CONV - 1, 1), cs] * lbuf[pl.ds(HALO, tm), cs]
    for k in range(CONV - 1):
        xc = xc + cw_ref[pl.ds(k, 1), cs] * lbuf[pl.ds(HALO - (CONV - 1) + k, tm), cs]
    xcm = xc.astype(_MXU)
    r = _sigmoid(_nn(xcm, wrg_ref[hh]) + brg_ref[pl.ds(hh, 1), :])
    ig = _sigmoid(_nn(xcm, wig_ref[hh]) + big_ref[pl.ds(hh, 1), :])
    a = jnp.exp(-LRU_C * r * sp[:, hh * HD:(hh + 1) * HD])
    one_m = 1.0 - a * a
    t = t0 + lax.broadcasted_iota(jnp.int32, (tm, HD), 0)
    mult = jnp.where(t == 0, 1.0, jnp.sqrt(jnp.maximum(one_m, 0.0)))
    return xc, r, ig, a, one_m, mult


def _scan_forward(a_ref, b_ref, h_ref, carry, tm):
    c = a_ref.shape[1]
    row = lax.broadcasted_iota(jnp.int32, (8, c), 0)

    def step(i, carry):
        r0 = pl.multiple_of(i * 8, 8)
        av = a_ref[pl.ds(r0, 8), :]
        bv = b_ref[pl.ds(r0, 8), :]
        for d in (1, 2, 4):
            keep = row >= d
            bv = bv + av * jnp.where(keep, pltpu.roll(bv, d, 0), 0.0)
            av = av * jnp.where(keep, pltpu.roll(av, d, 0), 1.0)
        hv = bv + av * carry
        h_ref[pl.ds(r0, 8), :] = hv
        return jnp.broadcast_to(hv[7:8, :], (8, c))

    return lax.fori_loop(0, tm // 8, step, carry)


def _scan_backward(an_ref, g_ref, dh_ref, carry, tm):
    c = g_ref.shape[1]
    row = lax.broadcasted_iota(jnp.int32, (8, c), 0)
    n = tm // 8

    def step(i, carry):
        r0 = pl.multiple_of((n - 1 - i) * 8, 8)
        av = an_ref[pl.ds(r0, 8), :]
        gv = g_ref[pl.ds(r0, 8), :]
        for d in (1, 2, 4):
            keep = row < 8 - d
            gv = gv + av * jnp.where(keep, pltpu.roll(gv, 8 - d, 0), 0.0)
            av = av * jnp.where(keep, pltpu.roll(av, 8 - d, 0), 1.0)
        hv = gv + av * carry
        dh_ref[pl.ds(r0, 8), :] = hv
        return jnp.broadcast_to(hv[0:1, :], (8, c))

    return lax.fori_loop(0, n, step, carry)


def _mixer_fwd(z, x, gw, small):
    s = x.shape[0]
    tm = min(TM_SEQ, s)
    (pool_w, pool_scale, conv_w, conv_b, w_rg, b_rg, w_ig, b_ig, lam, b_gate) = small

    def body(z_ref, x_ref, gw_ref, pw_ref, ps_ref, cw_ref, cb_ref, wrg_ref, brg_ref, wig_ref, big_ref, lam_ref,
             bg_ref, h_ref, yl_ref, mg_ref, yp_ref, yr_ref, h1_ref,
             pprojT, lru_w, wout_w, pbuf, lbuf, a_s, b_s, hcar, sems):
        i = pl.program_id(0)
        t0 = i * tm

        @pl.when(i == 0)
        def _():
            _load_weights(gw_ref, [("pproj", pprojT, PW), ("lru", lru_w, D), ("wout", wout_w, D)], sems)
            pbuf[pl.ds(0, HALO), :] = jnp.zeros((HALO, PW), F32)
            lbuf[pl.ds(0, HALO), :] = jnp.zeros((HALO, D), F32)
            hcar[...] = jnp.zeros_like(hcar)

        pbuf[pl.ds(HALO, tm), :] = z_ref[:, pl.ds(0, PW)]
        _, mixed_pre = _pool_tile(pbuf, t0, tm, pw_ref, ps_ref)
        mixed = jnp.concatenate(mixed_pre, axis=1) * ps_ref[...]
        y_pool = _nt(mixed.astype(_MXU), pprojT[...])
        pbuf[pl.ds(0, HALO), :] = pbuf[pl.ds(tm, HALO), :]

        lbuf[pl.ds(HALO, tm), :] = z_ref[:, pl.ds(PW, D)]
        sp, _ = _softplus_neg(lam_ref[...])
        for hh in range(HEADS):
            xc, r, ig, a, one_m, mult = _lru_gates_head(hh, lbuf, t0, tm, cw_ref, cb_ref, wrg_ref, brg_ref,
                                                        wig_ref, big_ref, sp)
            a_s[:, pl.ds(hh * HD, HD)] = a
            b_s[:, pl.ds(hh * HD, HD)] = mult * ig * xc
        lbuf[pl.ds(0, HALO), :] = lbuf[pl.ds(tm, HALO), :]
        hcar[...] = _scan_forward(a_s, b_s, h_ref, hcar[...], tm)
        gel, _ = _gelu_and_grad(z_ref[:, pl.ds(PW + D, D)])
        yl = (h_ref[...] * gel).astype(_MXU)
        yl_ref[...] = yl
        y_lru = _nn(yl, lru_w[...])

        g0 = _sigmoid(z_ref[:, pl.ds(PW + 2 * D, D)] + bg_ref[pl.ds(0, 1), :])
        g1 = _sigmoid(z_ref[:, pl.ds(PW + 3 * D, D)] + bg_ref[pl.ds(1, 1), :])
        merged = (g0 * y_pool + g1 * y_lru).astype(_MXU)
        mg_ref[...] = merged
        yp_ref[...] = y_pool.astype(_MXU)
        yr_ref[...] = y_lru.astype(_MXU)
        h1_ref[...] = x_ref[...] + _nn(merged, wout_w[...])

    tok = lambda w, dt: jax.ShapeDtypeStruct((s, w), dt)
    tspec = lambda w: pl.BlockSpec((tm, w), lambda i: (i, 0))
    full = lambda a: pl.BlockSpec(a.shape, lambda i: (0,) * a.ndim)
    return pl.pallas_call(
        body, name="mixer_fwd", grid=(s // tm,),
        out_shape=(tok(D, F32), tok(D, _MXU), tok(D, _MXU), tok(D, _MXU), tok(D, _MXU), tok(D, F32)),
        in_specs=[tspec(NIN), tspec(D), pl.BlockSpec(memory_space=pl.ANY)] + [full(a) for a in small],
        out_specs=(tspec(D),) * 6,
        scratch_shapes=[pltpu.VMEM((D, PW), _MXU), pltpu.VMEM((D, D), _MXU), pltpu.VMEM((D, D), _MXU),
                        pltpu.VMEM((tm + HALO, PW), F32), pltpu.VMEM((tm + HALO, D), F32),
                        pltpu.VMEM((tm, D), F32), pltpu.VMEM((tm, D), F32), pltpu.VMEM((8, D), F32),
                        pltpu.SemaphoreType.DMA((3 * NDEV,))],
        compiler_params=_cparams(1),
    )(z, x, gw, *small)


def _ffn_fwd(h1, g2, gw):
    s = h1.shape[0]
    tm = min(TM, s)
    half = FF // 2

    def body(h1_ref, g2_ref, gw_ref, v_ref, gf_ref, uf_ref, h2_ref, wffnT, wffo, sems):
        @pl.when(pl.program_id(0) == 0)
        def _():
            _load_weights(gw_ref, [("wffn", wffnT, D), ("wffo", wffo, D)], sems)

        hv = h1_ref[...]
        inv = lax.rsqrt(jnp.mean(hv * hv, axis=-1, keepdims=True) + EPS)
        v = (hv * inv * g2_ref[...]).astype(_MXU)
        v_ref[...] = v
        acc = hv
        for ch in range(2):
            cs = pl.ds(ch * half, half)
            gf = _nt(v, wffnT[pl.ds(ch * half, half), :]).astype(_MXU)
            uf = _nt(v, wffnT[pl.ds(FF + ch * half, half), :]).astype(_MXU)
            gf_ref[:, cs] = gf
            uf_ref[:, cs] = uf
            gf32 = gf.astype(F32)
            act = (gf32 * _sigmoid(gf32) * uf.astype(F32)).astype(_MXU)
            acc = acc + _nn(act, wffo[pl.ds(ch * half, half), :])
        h2_ref[...] = acc

    tspec = lambda w: pl.BlockSpec((tm, w), lambda i: (i, 0))
    return pl.pallas_call(
        body, name="ffn_fwd", grid=(s // tm,),
        out_shape=(jax.ShapeDtypeStruct((s, D), _MXU), jax.ShapeDtypeStruct((s, FF), _MXU),
                   jax.ShapeDtypeStruct((s, FF), _MXU), jax.ShapeDtypeStruct((s, D), F32)),
        in_specs=[tspec(D), pl.BlockSpec((1, D), lambda i: (0, 0)), pl.BlockSpec(memory_space=pl.ANY)],
        out_specs=(tspec(D), tspec(FF), tspec(FF), tspec(D)),
        scratch_shapes=[pltpu.VMEM((2 * FF, D), _MXU), pltpu.VMEM((FF, D), _MXU), pltpu.SemaphoreType.DMA((2 * NDEV,))],
        compiler_params=_cparams(1),
    )(h1, g2, gw)


def _rms_bwd(dy, xn, inv, g):
    dg = jnp.sum(dy * xn, axis=0, keepdims=True)
    dxn = dy * g
    dx = inv * (dxn - xn * jnp.mean(dxn * xn, axis=-1, keepdims=True))
    return dx, dg


def _ple_loss_fwd_bwd(h2, p, target, g3, gfin, gw):
    s = h2.shape[0]
    tm = min(TM, s)

    def body(h2_ref, p_ref, t_ref, g3_ref, gf_ref, gw_ref,
             dh2_ref, n3_ref, dpg_ref, de_ref, loss_ref, dg3_ref, dgf_ref, wpg, pleT, sems):
        i = pl.program_id(0)

        @pl.when(i == 0)
        def _():
            _load_weights(gw_ref, [("wpg", wpg, D), ("ple", pleT, PLE)], sems)
            loss_ref[...] = jnp.zeros_like(loss_ref)
            dg3_ref[...] = jnp.zeros_like(dg3_ref)
            dgf_ref[...] = jnp.zeros_like(dgf_ref)

        hv = h2_ref[...]
        inv3 = lax.rsqrt(jnp.mean(hv * hv, axis=-1, keepdims=True) + EPS)
        xn3 = hv * inv3
        n3 = (xn3 * g3_ref[...]).astype(_MXU)
        n3_ref[...] = n3
        pg = _sigmoid(_nn(n3, wpg[...]))
        e = _nt(p_ref[...].astype(_MXU), pleT[...])
        h3 = hv + pg * e
        invf = lax.rsqrt(jnp.mean(h3 * h3, axis=-1, keepdims=True) + EPS)
        xf = h3 * invf
        diff = xf * gf_ref[...] - t_ref[...]
        loss_ref[...] += jnp.sum(diff * diff) * (0.5 / D)
        dh3, dgf = _rms_bwd(diff * (1.0 / D), xf, invf, gf_ref[...])
        dgf_ref[...] += dgf
        de_ref[...] = (dh3 * pg).astype(_MXU)
        dpg = (dh3 * e * pg * (1.0 - pg)).astype(_MXU)
        dpg_ref[...] = dpg
        dn3 = _nt(dpg, wpg[...])
        dx3, dg3 = _rms_bwd(dn3, xn3, inv3, g3_ref[...])
        dg3_ref[...] += dg3
        dh2_ref[...] = dh3 + dx3

    tspec = lambda w: pl.BlockSpec((tm, w), lambda i: (i, 0))
    vec = pl.BlockSpec((1, D), lambda i: (0, 0))
    tok = lambda w, dt: jax.ShapeDtypeStruct((s, w), dt)
    return pl.pallas_call(
        body, name="ple_loss", grid=(s // tm,),
        out_shape=(tok(D, F32), tok(D, _MXU), tok(D, _MXU), tok(D, _MXU), jax.ShapeDtypeStruct((8, 128), F32),
                   jax.ShapeDtypeStruct((1, D), F32), jax.ShapeDtypeStruct((1, D), F32)),
        in_specs=[tspec(D), tspec(PLE), tspec(D), vec, vec, pl.BlockSpec(memory_space=pl.ANY)],
        out_specs=(tspec(D), tspec(D), tspec(D), tspec(D), pl.BlockSpec((8, 128), lambda i: (0, 0)), vec, vec),
        scratch_shapes=[pltpu.VMEM((D, D), _MXU), pltpu.VMEM((D, PLE), _MXU), pltpu.SemaphoreType.DMA((2 * NDEV,))],
        compiler_params=_cparams(1),
    )(h2, p, target, g3, gfin, gw)


def _ffn_bwd_hidden(dh2, gf, uf, gw):
    s = dh2.shape[0]
    tm = min(TM, s)
    half = FF // 2

    def body(dh2_ref, gf_ref, uf_ref, gw_ref, dff_ref, act_ref, wffo, sems):
        @pl.when(pl.program_id(0) == 0)
        def _():
            _load_weights(gw_ref, [("wffo", wffo, D)], sems)

        dm = dh2_ref[...].astype(_MXU)
        for ch in range(2):
            cs = pl.ds(ch * half, half)
            dact = _nt(dm, wffo[pl.ds(ch * half, half), :])
            gfv = gf_ref[:, cs].astype(F32)
            ufv = uf_ref[:, cs].astype(F32)
            sg = _sigmoid(gfv)
            silu = gfv * sg
            act_ref[:, cs] = (silu * ufv).astype(_MXU)
            dff_ref[:, pl.ds(ch * half, half)] = (dact * ufv * (sg * (1.0 + gfv * (1.0 - sg)))).astype(_MXU)
            dff_ref[:, pl.ds(FF + ch * half, half)] = (dact * silu).astype(_MXU)

    tspec = lambda w: pl.BlockSpec((tm, w), lambda i: (i, 0))
    return pl.pallas_call(
        body, name="ffn_bwd_hidden", grid=(s // tm,),
        out_shape=(jax.ShapeDtypeStruct((s, 2 * FF), _MXU), jax.ShapeDtypeStruct((s, FF), _MXU)),
        in_specs=[tspec(D), tspec(FF), tspec(FF), pl.BlockSpec(memory_space=pl.ANY)],
        out_specs=(tspec(2 * FF), tspec(FF)),
        scratch_shapes=[pltpu.VMEM((FF, D), _MXU), pltpu.SemaphoreType.DMA((NDEV,))],
        compiler_params=_cparams(1),
    )(dh2, gf, uf, gw)


def _proj_norm_bwd(dy, x, dres, g, gw, slab, width, name):
    s = x.shape[0]
    tm = min(TM, s)

    def body(dy_ref, x_ref, dr_ref, g_ref, gw_ref, dx_ref, dg_ref, wT, sems):
        @pl.when(pl.program_id(0) == 0)
        def _():
            _load_weights(gw_ref, [(slab, wT, D)], sems)
            dg_ref[...] = jnp.zeros_like(dg_ref)

        dv = _nn(dy_ref[...], wT[...])
        xv = x_ref[...]
        inv = lax.rsqrt(jnp.mean(xv * xv, axis=-1, keepdims=True) + EPS)
        dx, dg = _rms_bwd(dv, xv * inv, inv, g_ref[...])
        dg_ref[...] += dg
        dx_ref[...] = dr_ref[...] + dx

    tspec = lambda w: pl.BlockSpec((tm, w), lambda i: (i, 0))
    vec = pl.BlockSpec((1, D), lambda i: (0, 0))
    return pl.pallas_call(
        body, name=name, grid=(s // tm,),
        out_shape=(jax.ShapeDtypeStruct((s, D), F32), jax.ShapeDtypeStruct((1, D), F32)),
        in_specs=[tspec(width), tspec(D), tspec(D), vec, pl.BlockSpec(memory_space=pl.ANY)],
        out_specs=(tspec(D), vec),
        scratch_shapes=[pltpu.VMEM((width, D), _MXU), pltpu.SemaphoreType.DMA((NDEV,))],
        compiler_params=_cparams(1),
    )(dy, x, dres, g, gw)


def _mixer_bwd(dh1, z, h, y_pool, y_lru, gw, small):
    s = dh1.shape[0]
    tm = min(TM_SEQ, s)
    nt = s // tm
    (pool_w, pool_scale, conv_w, conv_b, w_rg, b_rg, w_ig, b_ig, lam, b_gate) = small

    def body(dh1_ref, z_ref, zp_ref, h_ref, hp_ref, yp_ref, yr_ref, gw_ref,
             pw_ref, ps_ref, cw_ref, cb_ref, wrg_ref, brg_ref, wig_ref, big_ref, lam_ref, bg_ref,
             dz_ref, dyr_ref, dyp_ref, mx_ref,
             gbg_ref, glam_ref, gbrg_ref, gbig_ref, gcb_ref, gcw_ref, gps_ref, gpw_ref, gwrg_ref, gwig_ref,
             pprojT, lru_w, wout_w, pbuf, lbuf, hbuf, qbuf, xbuf, abuf, an_s, g_s, dh_s, r_s, ig_s, xc_s, dcar, sems):
        step = pl.program_id(0)
        i = nt - 1 - step
        t0 = i * tm

        @pl.when(step == 0)
        def _():
            _load_weights(gw_ref, [("pproj", pprojT, PW), ("lru", lru_w, D), ("wout", wout_w, D)], sems)
            for ref in (gbg_ref, glam_ref, gbrg_ref, gbig_ref, gcb_ref, gcw_ref, gps_ref, gpw_ref, gwrg_ref, gwig_ref):
                ref[...] = jnp.zeros_like(ref)
            qbuf[pl.ds(tm, HALO), :] = jnp.zeros((HALO, PW), F32)
            xbuf[pl.ds(tm, 8), :] = jnp.zeros((8, D), F32)
            dcar[...] = jnp.zeros_like(dcar)
            abuf[pl.ds(8 + tm, 8), :] = jnp.ones((8, D), F32)

        first = i == 0
        zprev = jnp.where(first, 0.0, zp_ref[...])
        hprev = jnp.where(first, 0.0, hp_ref[...])

        d_merged = _nt(dh1_ref[...].astype(_MXU), wout_w[...])

        g0 = _sigmoid(z_ref[:, pl.ds(PW + 2 * D, D)] + bg_ref[pl.ds(0, 1), :])
        g1 = _sigmoid(z_ref[:, pl.ds(PW + 3 * D, D)] + bg_ref[pl.ds(1, 1), :])
        dz0 = d_merged * yp_ref[...].astype(F32) * g0 * (1.0 - g0)
        dz1 = d_merged * yr_ref[...].astype(F32) * g1 * (1.0 - g1)
        dz_ref[:, pl.ds(PW + 2 * D, D)] = dz0.astype(_MXU)
        dz_ref[:, pl.ds(PW + 3 * D, D)] = dz1.astype(_MXU)
        gbg_ref[pl.ds(0, 1), :] += jnp.sum(dz0, axis=0, keepdims=True)
        gbg_ref[pl.ds(1, 1), :] += jnp.sum(dz1, axis=0, keepdims=True)
        d_ypool = (d_merged * g0).astype(_MXU)
        d_ylru = (d_merged * g1).astype(_MXU)
        dyp_ref[...] = d_ypool
        dyr_ref[...] = d_ylru

        d_yl = _nt(d_ylru, lru_w[...])
        gel, dgel = _gelu_and_grad(z_ref[:, pl.ds(PW + D, D)])
        dz_ref[:, pl.ds(PW + D, D)] = (d_yl * h_ref[...] * dgel).astype(_MXU)
        g_s[...] = d_yl * gel
        lbuf[pl.ds(0, HALO), :] = zprev[:, PW:PW + D]
        lbuf[pl.ds(HALO, tm), :] = z_ref[:, pl.ds(PW, D)]
        hbuf[pl.ds(0, 8), :] = hprev
        hbuf[pl.ds(8, tm), :] = h_ref[...]
        sp, sneg = _softplus_neg(lam_ref[...])
        for hh in range(HEADS):
            cs = pl.ds(hh * HD, HD)
            xc, r, ig, a, one_m, mult = _lru_gates_head(hh, lbuf, t0, tm, cw_ref, cb_ref, wrg_ref, brg_ref,
                                                        wig_ref, big_ref, sp)
            abuf[pl.ds(8, tm), cs] = a
            r_s[:, cs] = r
            ig_s[:, cs] = ig
            xc_s[:, cs] = xc
        an_s[...] = abuf[pl.ds(9, tm), :]
        _scan_backward(an_s, g_s, dh_s, dcar[...], tm)
        dcar[...] = jnp.broadcast_to((abuf[pl.ds(8, 8), :] * dh_s[pl.ds(0, 8), :])[0:1, :], (8, D))
        for hh in range(HEADS):
            cs = pl.ds(hh * HD, HD)
            a = abuf[pl.ds(8, tm), cs]
            r = r_s[:, cs]
            ig = ig_s[:, cs]
            xc = xc_s[:, cs]
            dh = dh_s[:, cs]
            t = t0 + lax.broadcasted_iota(jnp.int32, (tm, HD), 0)
            one_m = 1.0 - a * a
            live = jnp.logical_and(t != 0, one_m > 0.0)
            mult = jnp.where(t == 0, 1.0, jnp.sqrt(jnp.maximum(one_m, 0.0)))
            d_mult = dh * ig * xc
            d_loga = dh * hbuf[pl.ds(7, tm), cs] * a + jnp.where(live, -d_mult * a * a / mult, 0.0)
            glam_ref[:, cs] += jnp.sum(d_loga * (LRU_C * r) * sneg[:, hh * HD:(hh + 1) * HD], axis=0, keepdims=True)
            d_rpre = d_loga * (-LRU_C * sp[:, hh * HD:(hh + 1) * HD]) * r * (1.0 - r)
            d_igpre = dh * mult * xc * ig * (1.0 - ig)
            gbrg_ref[pl.ds(hh, 1), :] += jnp.sum(d_rpre, axis=0, keepdims=True)
            gbig_ref[pl.ds(hh, 1), :] += jnp.sum(d_igpre, axis=0, keepdims=True)
            drm = d_rpre.astype(_MXU)
            dim = d_igpre.astype(_MXU)
            xcm = xc.astype(_MXU)
            gwrg_ref[hh] += _tn(xcm, drm)
            gwig_ref[hh] += _tn(xcm, dim)
            d_xc = dh * mult * ig + _nt(drm, wrg_ref[hh]) + _nt(dim, wig_ref[hh])
            gcb_ref[:, cs] += jnp.sum(d_xc, axis=0, keepdims=True)
            for k in range(CONV):
                gcw_ref[pl.ds(k, 1), cs] += jnp.sum(d_xc * lbuf[pl.ds(HALO - (CONV - 1) + k, tm), cs], axis=0,
                                                    keepdims=True)
            xbuf[pl.ds(0, tm), cs] = d_xc
        dzl = cw_ref[pl.ds(CONV - 1, 1), :] * xbuf[pl.ds(0, tm), :]
        for k in range(CONV - 1):
            dzl = dzl + cw_ref[pl.ds(k, 1), :] * xbuf[pl.ds(CONV - 1 - k, tm), :]
        dz_ref[:, pl.ds(PW, D)] = dzl.astype(_MXU)
        xbuf[pl.ds(tm, 8), :] = xbuf[pl.ds(0, 8), :]

        d_mixed = _nn(d_ypool, pprojT[...])
        pbuf[pl.ds(0, HALO), :] = zprev[:, 0:PW]
        pbuf[pl.ds(HALO, tm), :] = z_ref[:, pl.ds(0, PW)]
        pooled, mixed_pre = _pool_tile(pbuf, t0, tm, pw_ref, ps_ref)
        mp = jnp.concatenate(mixed_pre, axis=1)
        mx_ref[...] = (mp * ps_ref[...]).astype(_MXU)
        gps_ref[...] += jnp.sum(d_mixed * mp, axis=0, keepdims=True)
        d_mp = (d_mixed * ps_ref[...]).astype(_MXU)
        t = t0 + lax.broadcasted_iota(jnp.int32, (tm, GD), 0)
        d_pooled = []
        for g, w in enumerate(WINDOWS):
            dmg = d_mp[:, g * GD:(g + 1) * GD]
            gpw_ref[g] += _tn(pooled[g].astype(_MXU), dmg)
            dp = _nt(dmg, pw_ref[g])
            d_pooled.append(dp)
            qbuf[pl.ds(0, tm), pl.ds(g * GD, GD)] = dp / jnp.minimum(t + 1, w).astype(F32)
        for g, w in enumerate(WINDOWS):
            cs = pl.ds(g * GD, GD)
            acc = qbuf[pl.ds(0, tm), cs]
            for d in range(1, w):
                acc = acc + qbuf[pl.ds(d, tm), cs]
            dz_ref[:, cs] = (acc - d_pooled[g]).astype(_MXU)
        qbuf[pl.ds(tm, HALO), :] = qbuf[pl.ds(0, HALO), :]

    rev = lambda w: pl.BlockSpec((tm, w), lambda g: (nt - 1 - g, 0))
    prev = lambda rows, w: pl.BlockSpec((rows, w), lambda g: (jnp.maximum((nt - 1 - g) * (tm // rows) - 1, 0), 0))
    full = lambda a: pl.BlockSpec(a.shape, lambda g: (0,) * a.ndim)
    tok = lambda w, dt: jax.ShapeDtypeStruct((s, w), dt)
    acc_shapes = [(2, D), (1, D), (HEADS, HD), (HEADS, HD), (1, D), (CONV, D), (1, PW), (GROUPS, GD, GD),
                  (HEADS, HD, HD), (HEADS, HD, HD)]
    acc_specs = tuple(pl.BlockSpec(sh, lambda g, n=len(sh): (0,) * n) for sh in acc_shapes)
    return pl.pallas_call(
        body, name="mixer_bwd", grid=(nt,),
        out_shape=(tok(NIN, _MXU), tok(D, _MXU), tok(D, _MXU), tok(PW, _MXU))
        + tuple(jax.ShapeDtypeStruct(sh, F32) for sh in acc_shapes),
        in_specs=[rev(D), rev(NIN), prev(HALO, NIN), rev(D), prev(8, D), rev(D), rev(D),
                  pl.BlockSpec(memory_space=pl.ANY)] + [full(a) for a in small],
        out_specs=(rev(NIN), rev(D), rev(D), rev(PW)) + acc_specs,
        scratch_shapes=[pltpu.VMEM((D, PW), _MXU), pltpu.VMEM((D, D), _MXU), pltpu.VMEM((D, D), _MXU),
                        pltpu.VMEM((tm + HALO, PW), F32), pltpu.VMEM((tm + HALO, D), F32),
                        pltpu.VMEM((tm + 8, D), F32), pltpu.VMEM((tm + HALO, PW), F32), pltpu.VMEM((tm + 8, D), F32),
                        pltpu.VMEM((tm + 16, D), F32), pltpu.VMEM((tm, D), F32), pltpu.VMEM((tm, D), F32),
                        pltpu.VMEM((tm, D), F32), pltpu.VMEM((tm, D), F32), pltpu.VMEM((tm, D), F32),
                        pltpu.VMEM((tm, D), F32), pltpu.VMEM((8, D), F32), pltpu.SemaphoreType.DMA((3 * NDEV,))],
        compiler_params=_cparams(1),
    )(dh1, z, z, h, h, y_pool, y_lru, gw, *small)


def _split3(a):
    hi = a.astype(jnp.bfloat16).astype(F32)
    mid = (a - hi).astype(jnp.bfloat16).astype(F32)
    lo = (a - hi - mid).astype(jnp.bfloat16).astype(F32)
    return jnp.stack([hi, mid, lo])


def _small_pack(parts):
    flat = jnp.concatenate([a.reshape(-1) for a in parts])
    return jnp.pad(flat, (0, NDEV * SMALL_ROWS * D - flat.shape[0])).reshape(NDEV * SMALL_ROWS, D)


def _small_unpack(packed, shapes):
    flat = packed.reshape(-1)
    out, o = [], 0
    for sh in shapes:
        n = math.prod(sh)
        out.append(flat[o:o + n].reshape(sh))
        o += n
    return out


def kernel(x, p, norm1_g, w_in, b_gate, pool_w, pool_scale, pool_proj, conv_w, conv_b, w_rg, b_rg, w_ig, b_ig, lru_lambda, lru_proj, w_out, norm2_g, w_ffn_in, w_ffn_out, ple_norm_g, w_ple_gate, w_ple_proj, final_g, loss_target, m_norm1_g, m_w_in, m_b_gate, m_pool_w, m_pool_scale, m_pool_proj, m_conv_w, m_conv_b, m_w_rg, m_b_rg, m_w_ig, m_b_ig, m_lru_lambda, m_lru_proj, m_w_out, m_norm2_g, m_w_ffn_in, m_w_ffn_out, m_ple_norm_g, m_w_ple_gate, m_w_ple_proj, m_final_g, v_norm1_g, v_w_in, v_b_gate, v_pool_w, v_pool_scale, v_pool_proj, v_conv_w, v_conv_b, v_w_rg, v_b_rg, v_w_ig, v_b_ig, v_lru_lambda, v_lru_proj, v_w_out, v_norm2_g, v_w_ffn_in, v_w_ffn_out, v_ple_norm_g, v_w_ple_gate, v_w_ple_proj, v_final_g):
    axes = ("x", "y", "c")
    me = 4 * lax.axis_index("x") + 2 * lax.axis_index("y") + lax.axis_index("c")
    x2 = x[0]
    p2 = p[0, 0]
    tgt = loss_target[0]

    n_small = (CONV + 2) * 128
    small_terms = _split3(jnp.concatenate([conv_w[0].reshape(-1), b_gate[0].reshape(-1)]))
    small_rows = jnp.pad(small_terms, ((0, 16 - 3), (0, D - n_small)))
    own = jnp.concatenate([
        w_in[0].T.astype(_MXU), w_ffn_in[0].T.astype(_MXU),
        jnp.pad(pool_proj[0].T, ((0, 0), (0, D - PW))).astype(_MXU),
        jnp.pad(w_ple_proj[0].T, ((0, 0), (0, D - PLE))).astype(_MXU),
        lru_proj[0].astype(_MXU), w_out[0].astype(_MXU), w_ffn_out[0].astype(_MXU), w_ple_gate[0].astype(_MXU),
        small_rows.astype(_MXU),
    ], axis=0)
    gw = _all_gather_weights(own)
    off = W_OFF["f32s"][0]
    st = gw[:, off:off + 3, :n_small].astype(F32)
    sf = st[:, 0] + st[:, 1] + st[:, 2]
    conv_w_full = sf[:, :CONV * 128].reshape(NDEV, CONV, 128).transpose(1, 0, 2).reshape(CONV, D)
    b_gate_full = sf[:, CONV * 128:].reshape(NDEV, 2, 128).transpose(1, 0, 2).reshape(2, D)

    small = (pool_w[0].astype(_MXU), pool_scale, conv_w_full, conv_b, w_rg[0].astype(_MXU), b_rg[0],
             w_ig[0].astype(_MXU), b_ig[0], lru_lambda, b_gate_full)

    u, z = _inproj_fwd(x2, norm1_g, gw)
    h, yl, merged, y_pool, y_lru, h1 = _mixer_fwd(z, x2, gw, small)
    v, gf, uf, h2 = _ffn_fwd(h1, norm2_g, gw)

    dh2, n3, dpg, de, loss_blk, g_ple_norm, g_final = _ple_loss_fwd_bwd(h2, p2, tgt, ple_norm_g, final_g.reshape(1, D), gw)
    loss = lax.psum(loss_blk[0, 0], axes)
    dff, act = _ffn_bwd_hidden(dh2, gf, uf, gw)
    dh1, g_norm2 = _proj_norm_bwd(dff, h1, dh2, norm2_g, gw, "wffn", 2 * FF, "ffn_bwd_in")
    (dz, d_ylru, d_ypool, mixed, g_bgate, g_lam, g_brg, g_big, g_convb, g_convw, g_pscale, g_poolw, g_wrg,
     g_wig) = _mixer_bwd(dh1, z, h, y_pool, y_lru, gw, small)
    grad_x, g_norm1 = _proj_norm_bwd(dz, x2, dh1, norm1_g, gw, "win", NIN, "inproj_bwd")

    parts = [
        _grad_matmul(dz, u, "grad_w_in", NIN // 4),
        _grad_matmul(dff, v, "grad_w_ffn_in", 2 * FF // 4),
        _grad_matmul(d_ypool, mixed, "grad_pool_proj", D),
        _grad_matmul(de, p2, "grad_w_ple_proj", D),
        _grad_matmul(yl, d_ylru, "grad_lru_proj", D),
        _grad_matmul(merged, dh1, "grad_w_out", D),
        _grad_matmul(act, dh2, "grad_w_ffn_out", FF // 2),
        _grad_matmul(n3, dpg, "grad_w_ple_gate", D),
    ]
    small_shapes = [(1, D), (GROUPS, GD, GD), (1, PW), (1, D), (HEADS, HD, HD), (HEADS, HD), (HEADS, HD, HD),
                    (HEADS, HD), (1, D), (1, D), (1, D), (1, D), (2, D), (CONV, D)]
    parts.append(_small_pack([g_norm1, g_poolw, g_pscale, g_convb, g_wrg, g_brg, g_wig, g_big, g_lam, g_norm2,
                              g_ple_norm, g_final, g_bgate, g_convw]))

    own_a, land_a = _rs_d2d(parts)
    t_sum, t_ici = _sum_arrays([own_a.reshape(4 * RG, D), land_a.reshape(4 * RG, D)], "rs_sum_d2d", narrow=_MXU)
    t_sum = t_sum.reshape(4, RG, D)
    land_n, land_f = _rs_ici(t_ici.reshape(4, RG, D), t_sum)
    mine = lax.dynamic_index_in_dim(t_sum, 2 * lax.axis_index("x") + lax.axis_index("y"), 0, keepdims=False)
    n_big = RG - SMALL_ROWS
    red = _sum_arrays([mine[:n_big], land_n[0], land_n[1], land_n[2]], "rs_sum_ici", target=n_big // 2)
    red_small = _sum_arrays([mine[n_big:], land_f[0], land_f[1], land_f[2]], "rs_sum_ici_small")

    def slab(name):
        o, r = G_OFF[name]
        return red_small if name == "small" else red[o:o + r]

    g_w_in = slab("win").T
    g_w_ffn_in = slab("wffn").T
    g_pool_proj = slab("pproj")[:, :PW].T
    g_w_ple_proj = slab("ple")[:, :PLE].T
    g_lru_proj, g_w_out, g_w_ffn_out, g_w_ple_gate = slab("lru"), slab("wout"), slab("wffo"), slab("wpg")
    small_red = _all_gather_small(slab("small"))
    (gs_norm1, gs_poolw, gs_pscale, gs_convb, gs_wrg, gs_brg, gs_wig, gs_big, gs_lam, gs_norm2, gs_ple_norm,
     gs_final, gs_bgate, gs_convw) = _small_unpack(small_red, small_shapes)
    g_b_gate = lax.dynamic_slice_in_dim(gs_bgate, me * 128, 128, axis=1)
    g_conv_w = lax.dynamic_slice_in_dim(gs_convw, me * 128, 128, axis=1)

    grads = {
        "norm1_g": gs_norm1, "w_in": g_w_in[None], "b_gate": g_b_gate[None], "pool_w": gs_poolw[None],
        "pool_scale": gs_pscale, "pool_proj": g_pool_proj[None], "conv_w": g_conv_w[None], "conv_b": gs_convb,
        "w_rg": gs_wrg[None], "b_rg": gs_brg[None], "w_ig": gs_wig[None], "b_ig": gs_big[None], "lru_lambda": gs_lam,
        "lru_proj": g_lru_proj[None], "w_out": g_w_out[None], "norm2_g": gs_norm2, "w_ffn_in": g_w_ffn_in[None],
        "w_ffn_out": g_w_ffn_out[None], "ple_norm_g": gs_ple_norm, "w_ple_gate": g_w_ple_gate[None],
        "w_ple_proj": g_w_ple_proj[None], "final_g": gs_final.reshape(D),
    }
    weights = dict(norm1_g=norm1_g, w_in=w_in, b_gate=b_gate, pool_w=pool_w, pool_scale=pool_scale, pool_proj=pool_proj,
                   conv_w=conv_w, conv_b=conv_b, w_rg=w_rg, b_rg=b_rg, w_ig=w_ig, b_ig=b_ig, lru_lambda=lru_lambda,
                   lru_proj=lru_proj, w_out=w_out, norm2_g=norm2_g, w_ffn_in=w_ffn_in, w_ffn_out=w_ffn_out,
                   ple_norm_g=ple_norm_g, w_ple_gate=w_ple_gate, w_ple_proj=w_ple_proj, final_g=final_g)
    moments_m = dict(norm1_g=m_norm1_g, w_in=m_w_in, b_gate=m_b_gate, pool_w=m_pool_w, pool_scale=m_pool_scale,
                     pool_proj=m_pool_proj, conv_w=m_conv_w, conv_b=m_conv_b, w_rg=m_w_rg, b_rg=m_b_rg, w_ig=m_w_ig,
                     b_ig=m_b_ig, lru_lambda=m_lru_lambda, lru_proj=m_lru_proj, w_out=m_w_out, norm2_g=m_norm2_g,
                     w_ffn_in=m_w_ffn_in, w_ffn_out=m_w_ffn_out, ple_norm_g=m_ple_norm_g, w_ple_gate=m_w_ple_gate,
                     w_ple_proj=m_w_ple_proj, final_g=m_final_g)
    moments_v = dict(norm1_g=v_norm1_g, w_in=v_w_in, b_gate=v_b_gate, pool_w=v_pool_w, pool_scale=v_pool_scale,
                     pool_proj=v_pool_proj, conv_w=v_conv_w, conv_b=v_conv_b, w_rg=v_w_rg, b_rg=v_b_rg, w_ig=v_w_ig,
                     b_ig=v_b_ig, lru_lambda=v_lru_lambda, lru_proj=v_lru_proj, w_out=v_w_out, norm2_g=v_norm2_g,
                     w_ffn_in=v_w_ffn_in, w_ffn_out=v_w_ffn_out, ple_norm_g=v_ple_norm_g, w_ple_gate=v_w_ple_gate,
                     w_ple_proj=v_w_ple_proj, final_g=v_final_g)
    names = list(weights)
    big = ("w_in", "w_ffn_in", "w_ffn_out", "lru_proj", "w_out", "w_ple_gate", "pool_proj", "w_ple_proj")
    delta, new_m, new_v = {}, {}, {}
    for n in big:
        sh = weights[n].shape
        as2d = lambda a: a.reshape(sh[-2], sh[-1])
        d_, m_, v_ = _adamw(as2d(weights[n]), as2d(grads[n]), as2d(moments_m[n]), as2d(moments_v[n]), "adamw_" + n)
        delta[n], new_m[n], new_v[n] = d_.reshape(sh), m_.reshape(sh), v_.reshape(sh)
    rest = [n for n in names if n not in big]
    rest_shapes = [weights[n].shape for n in rest]
    packed = [_small_pack([src[n] for n in rest]) for src in (weights, grads, moments_m, moments_v)]
    d_, m_, v_ = _adamw(*packed, "adamw_small")
    for n, a, b_, c_ in zip(rest, _small_unpack(d_, rest_shapes), _small_unpack(m_, rest_shapes),
                            _small_unpack(v_, rest_shapes)):
        delta[n], new_m[n], new_v[n] = a, b_, c_

    return (loss, grad_x[None], *[grads[n] for n in names], *[delta[n] for n in names],
            *[new_m[n] for n in names], *[new_v[n] for n in names])
```

```python
import functools
import math

import jax
import jax.numpy as jnp
from jax import lax
from jax.experimental import pallas as pl
from jax.experimental.pallas import tpu as pltpu

F32 = jnp.float32
D = 1024
NIN = 4608
PW = 512
FF = 2816
PLE = 256
HEADS, HD = 8, 128
GROUPS, GD = 4, 128
WINDOWS = (2, 4, 8, 16)
HALO = 16
CONV = 4
EPS = 1e-6
LRU_C = 8.0
NDEV = 8
MESH = pl.DeviceIdType.MESH

ADAM_LR, ADAM_B1, ADAM_B2, ADAM_EPS, ADAM_WD, ADAM_STEP = 0.001, 0.9, 0.999, 1e-08, 0.01, 10

_MXU = jnp.bfloat16
TM = 512
TM_SEQ = 256
VMEM_LIMIT = 56 * 1024 * 1024

W_SLABS = (("win", 576), ("wffn", 704), ("pproj", 128), ("ple", 128), ("lru", 128), ("wout", 128),
           ("wffo", 352), ("wpg", 128), ("f32s", 16))
W_OFF = {}
_o = 0
for _n, _r in W_SLABS:
    W_OFF[_n] = (_o, _r)
    _o += _r
RW = _o
SMALL_ROWS = 48
G_SLABS = (("win", 576), ("wffn", 704), ("pproj", 128), ("ple", 128), ("lru", 128), ("wout", 128),
           ("wffo", 352), ("wpg", 128), ("small", SMALL_ROWS))
G_OFF = {}
_o = 0
for _n, _r in G_SLABS:
    G_OFF[_n] = (_o, _r)
    _o += _r
RG = _o


def _cparams(n_axes=1, vmem=VMEM_LIMIT):
    return pltpu.CompilerParams(dimension_semantics=("arbitrary",) * n_axes, vmem_limit_bytes=vmem)


def _my_pos():
    return lax.axis_index("x"), lax.axis_index("y"), lax.axis_index("c")


def _nt(a, b):
    return lax.dot_general(a, b, (((1,), (1,)), ((), ())), preferred_element_type=F32)


def _nn(a, b):
    return lax.dot_general(a, b, (((1,), (0,)), ((), ())), preferred_element_type=F32)


def _tn(a, b):
    return lax.dot_general(a, b, (((0,), (0,)), ((), ())), preferred_element_type=F32)


def _sigmoid(x):
    return 0.5 * jnp.tanh(0.5 * x) + 0.5


_GELU_K = math.sqrt(2.0 / math.pi)


def _gelu_and_grad(x):
    x2 = x * x
    inner = _GELU_K * (x + 0.044715 * x2 * x)
    t = jnp.tanh(inner)
    g = 0.5 * x * (1.0 + t)
    dg = 0.5 * (1.0 + t) + 0.5 * x * (1.0 - t * t) * _GELU_K * (1.0 + 3.0 * 0.044715 * x2)
    return g, dg


def _softplus_neg(lam):
    x = -lam
    t = jnp.exp(-jnp.abs(x))
    u = 1.0 + t
    l1p = jnp.where(u == 1.0, t, jnp.log(u) * t / (u - 1.0))
    return jnp.maximum(x, 0.0) + l1p, _sigmoid(x)


def _start_slab_loads(g_ref, name, dst_ref, sems, base, width=D):
    off, rows = W_OFF[name]
    copies = []
    for k in range(NDEV):
        if width == D:
            src = g_ref.at[k, pl.ds(off, rows), :]
        else:
            src = g_ref.at[k, pl.ds(off, rows), pl.ds(0, width)]
        cp = pltpu.make_async_copy(src, dst_ref.at[pl.ds(k * rows, rows), :], sems.at[base + k])
        cp.start()
        copies.append(cp)
    return copies


def _load_weights(g_ref, items, sems):
    copies = []
    for n, (name, dst, width) in enumerate(items):
        copies += _start_slab_loads(g_ref, name, dst, sems, n * NDEV, width)
    for cp in copies:
        cp.wait()


def _all_gather_weights(own):
    rows, cols = own.shape

    def body(own_ref, out_ref, stage, send_sems, recv_sems, local_sem):
        x, y, c = _my_pos()
        me, sibling = (x, y, c), (x, y, 1 - c)
        chips = [(1 - x, y), (x, 1 - y), (1 - x, 1 - y)]

        def slab(px, py, pc):
            return out_ref.at[4 * px + 2 * py + pc]

        def copy(k, block, to, src=None):
            return pltpu.make_async_remote_copy(
                src_ref=slab(*block) if src is None else src, dst_ref=slab(*block),
                send_sem=send_sems.at[k], recv_sem=recv_sems.at[k], device_id=to, device_id_type=MESH)

        pltpu.sync_copy(own_ref, stage)
        mine = pltpu.make_async_copy(stage, slab(*me), local_sem)
        mine.start()
        first = [copy(0, me, sibling, src=stage)]
        first += [copy(1 + j, me, (*chip, c), src=stage) for j, chip in enumerate(chips)]
        for cp in first:
            cp.start()
        passed = [copy(4 + j, (*chip, c), sibling) for j, chip in enumerate(chips)]
        for j, chip in enumerate(chips):
            copy(1 + j, (*chip, c), me).wait_recv()
            passed[j].start()
        copy(0, sibling, me).wait_recv()
        for j, chip in enumerate(chips):
            copy(4 + j, (*chip, 1 - c), me).wait_recv()
        for cp in first + passed:
            cp.wait_send()
        mine.wait()

    return pl.pallas_call(
        body, name="ag_weights",
        out_shape=jax.ShapeDtypeStruct((NDEV, rows, cols), own.dtype),
        in_specs=[pl.BlockSpec(memory_space=pl.ANY)],
        out_specs=pl.BlockSpec(memory_space=pl.ANY),
        scratch_shapes=[pltpu.VMEM((rows, cols), own.dtype), pltpu.SemaphoreType.DMA((7,)),
                        pltpu.SemaphoreType.DMA((7,)), pltpu.SemaphoreType.DMA],
        compiler_params=pltpu.CompilerParams(vmem_limit_bytes=VMEM_LIMIT),
    )(own)


def _rs_d2d(parts):
    n = len(parts)

    def body(*refs):
        part_refs = refs[:n]
        land_ref, send_sems, recv_sems = refs[n:]
        x, y, c = _my_pos()
        sibling = (x, y, 1 - c)
        remote = []
        for j in range(4):
            xj, yj = j // 2, j % 2
            for w, (name, rows) in enumerate(G_SLABS):
                off = G_OFF[name][0]
                k = j * n + w
                s_sib = pl.multiple_of((4 * xj + 2 * yj + 1 - c) * rows, 8)
                rc = pltpu.make_async_remote_copy(
                    src_ref=part_refs[w].at[pl.ds(s_sib, rows), :], dst_ref=land_ref.at[j, pl.ds(off, rows), :],
                    send_sem=send_sems.at[k], recv_sem=recv_sems.at[k], device_id=sibling, device_id_type=MESH)
                rc.start()
                remote.append(rc)
        for rc in remote:
            rc.wait_recv()
        for rc in remote:
            rc.wait_send()

    land = pl.pallas_call(
        body, name="rs_d2d",
        out_shape=jax.ShapeDtypeStruct((4, RG, D), F32),
        in_specs=[pl.BlockSpec(memory_space=pl.ANY)] * n,
        out_specs=pl.BlockSpec(memory_space=pl.ANY),
        scratch_shapes=[pltpu.SemaphoreType.DMA((4 * n,)), pltpu.SemaphoreType.DMA((4 * n,))],
    )(*parts)
    c = lax.axis_index("c")
    own = jnp.concatenate(
        [lax.dynamic_index_in_dim(p.reshape(4, 2, rows, D), c, 1, keepdims=False) for p, (_, rows) in zip(parts, G_SLABS)],
        axis=1)
    return own, land


def _rs_ici(t_narrow, t_full):
    big = RG - SMALL_ROWS

    def body(tn_ref, tf_ref, land_n, land_f, send_sems, recv_sems):
        x, y, c = _my_pos()
        targets = [(1 - x, y), (x, 1 - y), (1 - x, 1 - y)]
        copies = []
        for r, (xt, yt) in enumerate(targets):
            j = 2 * xt + yt
            for k, (src, dst) in enumerate(((tn_ref.at[j, pl.ds(0, big), :], land_n.at[r]),
                                           (tf_ref.at[j, pl.ds(big, SMALL_ROWS), :], land_f.at[r]))):
                cp = pltpu.make_async_remote_copy(
                    src_ref=src, dst_ref=dst, send_sem=send_sems.at[2 * r + k], recv_sem=recv_sems.at[2 * r + k],
                    device_id=(xt, yt, c), device_id_type=MESH)
                cp.start()
                copies.append(cp)
        for cp in copies:
            cp.wait_recv()
        for cp in copies:
            cp.wait_send()

    return pl.pallas_call(
        body, name="rs_ici",
        out_shape=(jax.ShapeDtypeStruct((3, big, D), t_narrow.dtype), jax.ShapeDtypeStruct((3, SMALL_ROWS, D), F32)),
        in_specs=[pl.BlockSpec(memory_space=pl.ANY)] * 2,
        out_specs=(pl.BlockSpec(memory_space=pl.ANY),) * 2,
        scratch_shapes=[pltpu.SemaphoreType.DMA((6,)), pltpu.SemaphoreType.DMA((6,))],
    )(t_narrow, t_full)


def _all_gather_small(piece):
    rows = piece.shape[0]

    def body(p_ref, out_ref, send_sems, recv_sems, local_sem):
        x, y, c = _my_pos()
        me = 4 * x + 2 * y + c
        mine = pltpu.make_async_copy(p_ref, out_ref.at[pl.ds(pl.multiple_of(me * rows, 8), rows), :], local_sem)
        mine.start()
        sends = []
        peers = []
        for r in range(1, NDEV):
            px = 1 - x if (r >> 2) & 1 else x
            py = 1 - y if (r >> 1) & 1 else y
            pc = 1 - c if r & 1 else c
            peers.append((px, py, pc))
            cp = pltpu.make_async_remote_copy(
                src_ref=p_ref, dst_ref=out_ref.at[pl.ds(pl.multiple_of(me * rows, 8), rows), :],
                send_sem=send_sems.at[r - 1], recv_sem=recv_sems.at[r - 1], device_id=(px, py, pc),
                device_id_type=MESH)
            cp.start()
            sends.append(cp)
        for r, (px, py, pc) in enumerate(peers):
            them = 4 * px + 2 * py + pc
            pltpu.make_async_remote_copy(
                src_ref=p_ref, dst_ref=out_ref.at[pl.ds(pl.multiple_of(them * rows, 8), rows), :],
                send_sem=send_sems.at[r], recv_sem=recv_sems.at[r], device_id=(px, py, pc),
                device_id_type=MESH).wait_recv()
        for cp in sends:
            cp.wait_send()
        mine.wait()

    return pl.pallas_call(
        body, name="ag_small",
        out_shape=jax.ShapeDtypeStruct((NDEV * rows, piece.shape[1]), piece.dtype),
        in_specs=[pl.BlockSpec(memory_space=pltpu.VMEM)],
        out_specs=pl.BlockSpec(memory_space=pl.ANY),
        scratch_shapes=[pltpu.SemaphoreType.DMA((7,)), pltpu.SemaphoreType.DMA((7,)), pltpu.SemaphoreType.DMA],
    )(piece)


def _row_block(rows, target=512, mult=8):
    b = min(rows, target) // mult * mult
    while rows % b:
        b -= mult
    return b


def _sum_arrays(arrs, name, narrow=None, target=464):
    rows, cols = arrs[0].shape
    br = _row_block(rows, target, 16)
    n = len(arrs)

    def body(*refs):
        acc = refs[0][...].astype(F32)
        for r in refs[1:n]:
            acc = acc + r[...].astype(F32)
        refs[n][...] = acc
        if narrow is not None:
            refs[n + 1][...] = acc.astype(narrow)

    spec = pl.BlockSpec((br, cols), lambda i: (i, 0))
    shape = jax.ShapeDtypeStruct((rows, cols), F32)
    if narrow is None:
        out_shape, out_specs = shape, spec
    else:
        out_shape, out_specs = (shape, jax.ShapeDtypeStruct((rows, cols), narrow)), (spec, spec)
    return pl.pallas_call(
        body, name=name, grid=(rows // br,), out_shape=out_shape,
        in_specs=[spec] * n, out_specs=out_specs, compiler_params=_cparams(1),
    )(*arrs)


def _adamw(w, g, m, v, name):
    rows, cols = w.shape
    br = _row_block(rows, 256)

    def body(w_ref, g_ref, m_ref, v_ref, d_ref, nm_ref, nv_ref):
        g_ = g_ref[...]
        m_ = ADAM_B1 * m_ref[...] + (1.0 - ADAM_B1) * g_
        v_ = ADAM_B2 * v_ref[...] + (1.0 - ADAM_B2) * (g_ * g_)
        m_hat = m_ / (1.0 - ADAM_B1 ** ADAM_STEP)
        v_hat = v_ / (1.0 - ADAM_B2 ** ADAM_STEP)
        d_ref[...] = -ADAM_LR * (m_hat / (jnp.sqrt(v_hat) + ADAM_EPS) + ADAM_WD * w_ref[...])
        nm_ref[...] = m_
        nv_ref[...] = v_

    spec = pl.BlockSpec((br, cols), lambda i: (i, 0))
    shape = jax.ShapeDtypeStruct((rows, cols), F32)
    return pl.pallas_call(
        body, name=name, grid=(rows // br,), out_shape=(shape, shape, shape),
        in_specs=[spec] * 4, out_specs=(spec, spec, spec), compiler_params=_cparams(1),
    )(w, g, m, v)


def _grad_matmul(lhs, rhs, name, block_rows):
    s, r = lhs.shape
    k = rhs.shape[1]
    tm = min(TM, s)

    def body(l_ref, r_ref, o_ref):
        @pl.when(pl.program_id(1) == 0)
        def _():
            o_ref[...] = jnp.zeros_like(o_ref)

        o_ref[:, pl.ds(0, k)] += _tn(l_ref[...].astype(_MXU), r_ref[...].astype(_MXU))

    return pl.pallas_call(
        body, name=name, grid=(r // block_rows, s // tm),
        out_shape=jax.ShapeDtypeStruct((r, D), F32),
        in_specs=[pl.BlockSpec((tm, block_rows), lambda b, i: (i, b)), pl.BlockSpec((tm, k), lambda b, i: (i, 0))],
        out_specs=pl.BlockSpec((block_rows, D), lambda b, i: (b, 0)),
        compiler_params=_cparams(2),
    )(lhs, rhs)


def _inproj_fwd(x, g1, gw):
    s = x.shape[0]
    tm = min(TM, s)
    nchunk = 4
    cw = NIN // nchunk

    def body(x_ref, g1_ref, gw_ref, u_ref, z_ref, w_vmem, sems):
        @pl.when(pl.program_id(0) == 0)
        def _():
            _load_weights(gw_ref, [("win", w_vmem, D)], sems)

        xv = x_ref[...]
        inv = lax.rsqrt(jnp.mean(xv * xv, axis=-1, keepdims=True) + EPS)
        u = (xv * inv * g1_ref[...]).astype(_MXU)
        u_ref[...] = u
        for ch in range(nchunk):
            z_ref[:, pl.ds(ch * cw, cw)] = _nt(u, w_vmem[pl.ds(ch * cw, cw), :])

    return pl.pallas_call(
        body, name="inproj_fwd", grid=(s // tm,),
        out_shape=(jax.ShapeDtypeStruct((s, D), _MXU), jax.ShapeDtypeStruct((s, NIN), F32)),
        in_specs=[pl.BlockSpec((tm, D), lambda i: (i, 0)), pl.BlockSpec((1, D), lambda i: (0, 0)),
                  pl.BlockSpec(memory_space=pl.ANY)],
        out_specs=(pl.BlockSpec((tm, D), lambda i: (i, 0)), pl.BlockSpec((tm, NIN), lambda i: (i, 0))),
        scratch_shapes=[pltpu.VMEM((NIN, D), _MXU), pltpu.SemaphoreType.DMA((NDEV,))],
        compiler_params=_cparams(1),
    )(x, g1, gw)


def _pool_tile(pbuf, t0, tm, pw_ref, scale_ref):
    t = t0 + lax.broadcasted_iota(jnp.int32, (tm, GD), 0)
    pooled, mixed_pre = [], []
    for g, w in enumerate(WINDOWS):
        cs = pl.ds(g * GD, GD)
        cur = pbuf[pl.ds(HALO, tm), cs]
        acc = cur
        for d in range(1, w):
            acc = acc + pbuf[pl.ds(HALO - d, tm), cs]
        cnt = jnp.minimum(t + 1, w).astype(F32)
        pg = acc / cnt - cur
        pooled.append(pg)
        mixed_pre.append(_nn(pg.astype(_MXU), pw_ref[g]))
    return pooled, mixed_pre


def _lru_gates_head(hh, lbuf, start, tm, cw_ref, cb_ref, wrg_ref, brg_ref, wig_ref, big_ref, sp):
    cs = pl.ds(hh * HD, HD)
    xc = cb_ref[:, cs] + cw_ref[pl.ds(CONV - 1, 1), cs] * lbuf[pl.ds(HALO, tm), cs]
    for k in range(CONV - 1):
        xc = xc + cw_ref[pl.ds(k, 1), cs] * lbuf[pl.ds(HALO - (CONV - 1) + k, tm), cs]
    xcm = xc.astype(_MXU)
    r = _sigmoid(_nn(xcm, wrg_ref[hh]) + brg_ref[pl.ds(hh, 1), :])
    ig = _sigmoid(_nn(xcm, wig_ref[hh]) + big_ref[pl.ds(hh, 1), :])
    a = jnp.exp(-LRU_C * r * sp[:, hh * HD:(hh + 1) * HD])
    one_m = 1.0 - a * a
    live = jnp.logical_and(one_m > 0.0, jnp.logical_not(start))
    inv_mult = lax.rsqrt(jnp.where(live, one_m, 1.0))
    mult = jnp.where(live, one_m * inv_mult, jnp.where(start, 1.0, 0.0))
    return xc, r, ig, a, live, inv_mult, mult


def _seg_layout(tm):
    seg = tm // 8
    return seg, seg + 8


def _to_segments(dst_ref, hh, val, tm):
    seg, pitch = _seg_layout(tm)
    for s in range(8):
        dst_ref[hh, pl.ds(s * pitch, seg), :] = val[s * seg:(s + 1) * seg, :]


def _from_segments(src_ref, hh, tm):
    seg, pitch = _seg_layout(tm)
    return jnp.concatenate([src_ref[hh, pl.ds(s * pitch, seg), :] for s in range(8)], axis=0)


def _segment_scan(a_ref, b_ref, out_ref, hk, pk, carry_ref, tm, reverse):
    seg, pitch = _seg_layout(tm)
    row = lax.broadcasted_iota(jnp.int32, (8, HD), 0)
    order = range(seg - 1, -1, -1) if reverse else range(seg)
    for hh in range(HEADS):
        cs = pl.ds(hh * HD, HD)
        if reverse:
            a0 = a_ref[hh, pl.ds(0, 8, stride=pitch), :]
            a_wrap = jnp.where(row <= 6, pltpu.roll(a0, 7, 0), 1.0)
        hv = jnp.zeros((8, HD), F32)
        pv = jnp.ones((8, HD), F32)
        for k in order:
            if not reverse:
                av = a_ref[hh, pl.ds(k, 8, stride=pitch), :]
            elif k + 1 < seg:
                av = a_ref[hh, pl.ds(k + 1, 8, stride=pitch), :]
            else:
                av = a_wrap
            hv = av * hv + b_ref[hh, pl.ds(k, 8, stride=pitch), :]
            pv = av * pv
            hk[hh, pl.ds(8 * k, 8), :] = hv
            pk[hh, pl.ds(8 * k, 8), :] = pv
        for d in (1, 2, 4):
            if reverse:
                keep, sh = row < 8 - d, 8 - d
            else:
                keep, sh = row >= d, d
            hv = hv + pv * jnp.where(keep, pltpu.roll(hv, sh, 0), 0.0)
            pv = pv * jnp.where(keep, pltpu.roll(pv, sh, 0), 1.0)
        cin = carry_ref[:, cs]
        ends = hv + pv * cin
        if reverse:
            enter = jnp.where(row <= 6, pltpu.roll(ends, 7, 0), cin)
            carry_ref[:, cs] = jnp.broadcast_to((a0 * ends)[0:1, :], (8, HD))
        else:
            enter = jnp.where(row >= 1, pltpu.roll(ends, 1, 0), cin)
            carry_ref[:, cs] = jnp.broadcast_to(ends[7:8, :], (8, HD))
        for k in range(seg):
            out_ref[hh, pl.ds(k, 8, stride=pitch), :] = hk[hh, pl.ds(8 * k, 8), :] + pk[hh, pl.ds(8 * k, 8), :] * enter


def _mixer_fwd(z, x, gw, small):
    s = x.shape[0]
    tm = min(TM_SEQ, s)
    (pool_w, pool_scale, conv_w, conv_b, w_rg, b_rg, w_ig, b_ig, lam, b_gate) = small

    def body(z_ref, x_ref, gw_ref, pw_ref, ps_ref, cw_ref, cb_ref, wrg_ref, brg_ref, wig_ref, big_ref, lam_ref,
             bg_ref, h_ref, yl_ref, mg_ref, yp_ref, yr_ref, h1_ref,
             pprojT, lru_w, wout_w, pbuf, lbuf, a_s, b_s, h_s, hk, pk, hcar, sems):
        i = pl.program_id(0)
        t0 = i * tm

        @pl.when(i == 0)
        def _():
            _load_weights(gw_ref, [("pproj", pprojT, PW), ("lru", lru_w, D), ("wout", wout_w, D)], sems)
            pbuf[pl.ds(0, HALO), :] = jnp.zeros((HALO, PW), F32)
            lbuf[pl.ds(0, HALO), :] = jnp.zeros((HALO, D), F32)
            hcar[...] = jnp.zeros_like(hcar)

        pbuf[pl.ds(HALO, tm), :] = z_ref[:, pl.ds(0, PW)]
        _, mixed_pre = _pool_tile(pbuf, t0, tm, pw_ref, ps_ref)
        mixed = jnp.concatenate(mixed_pre, axis=1) * ps_ref[...]
        y_pool = _nt(mixed.astype(_MXU), pprojT[...])
        pbuf[pl.ds(0, HALO), :] = pbuf[pl.ds(tm, HALO), :]

        lbuf[pl.ds(HALO, tm), :] = z_ref[:, pl.ds(PW, D)]
        sp, _ = _softplus_neg(lam_ref[...])
        start = (t0 + lax.broadcasted_iota(jnp.int32, (tm, HD), 0)) == 0
        for hh in range(HEADS):
            xc, r, ig, a, _, _, mult = _lru_gates_head(hh, lbuf, start, tm, cw_ref, cb_ref, wrg_ref, brg_ref,
                                                       wig_ref, big_ref, sp)
            _to_segments(a_s, hh, a, tm)
            _to_segments(b_s, hh, mult * ig * xc, tm)
        lbuf[pl.ds(0, HALO), :] = lbuf[pl.ds(tm, HALO), :]
        _segment_scan(a_s, b_s, h_s, hk, pk, hcar, tm, reverse=False)
        for hh in range(HEADS):
            h_ref[:, pl.ds(hh * HD, HD)] = _from_segments(h_s, hh, tm)
        gel, _ = _gelu_and_grad(z_ref[:, pl.ds(PW + D, D)])
        yl = (h_ref[...] * gel).astype(_MXU)
        yl_ref[...] = yl
        y_lru = _nn(yl, lru_w[...])

        g0 = _sigmoid(z_ref[:, pl.ds(PW + 2 * D, D)] + bg_ref[pl.ds(0, 1), :])
        g1 = _sigmoid(z_ref[:, pl.ds(PW + 3 * D, D)] + bg_ref[pl.ds(1, 1), :])
        merged = (g0 * y_pool + g1 * y_lru).astype(_MXU)
        mg_ref[...] = merged
        yp_ref[...] = y_pool.astype(_MXU)
        yr_ref[...] = y_lru.astype(_MXU)
        h1_ref[...] = x_ref[...] + _nn(merged, wout_w[...])

    tok = lambda w, dt: jax.ShapeDtypeStruct((s, w), dt)
    tspec = lambda w: pl.BlockSpec((tm, w), lambda i: (i, 0))
    full = lambda a: pl.BlockSpec(a.shape, lambda i: (0,) * a.ndim)
    seg_buf = pltpu.VMEM((HEADS, 8 * _seg_layout(tm)[1], HD), F32)
    return pl.pallas_call(
        body, name="mixer_fwd", grid=(s // tm,),
        out_shape=(tok(D, F32), tok(D, _MXU), tok(D, _MXU), tok(D, _MXU), tok(D, _MXU), tok(D, F32)),
        in_specs=[tspec(NIN), tspec(D), pl.BlockSpec(memory_space=pl.ANY)] + [full(a) for a in small],
        out_specs=(tspec(D),) * 6,
        scratch_shapes=[pltpu.VMEM((D, PW), _MXU), pltpu.VMEM((D, D), _MXU), pltpu.VMEM((D, D), _MXU),
                        pltpu.VMEM((tm + HALO, PW), F32), pltpu.VMEM((tm + HALO, D), F32),
                        seg_buf, seg_buf, seg_buf, pltpu.VMEM((HEADS, tm, HD), F32), pltpu.VMEM((HEADS, tm, HD), F32),
                        pltpu.VMEM((8, D), F32), pltpu.SemaphoreType.DMA((3 * NDEV,))],
        compiler_params=_cparams(1),
    )(z, x, gw, *small)


def _ffn_fwd(h1, g2, gw):
    s = h1.shape[0]
    tm = min(TM, s)
    half = FF // 2

    def body(h1_ref, g2_ref, gw_ref, v_ref, gf_ref, uf_ref, h2_ref, wffnT, wffo, sems):
        @pl.when(pl.program_id(0) == 0)
        def _():
            _load_weights(gw_ref, [("wffn", wffnT, D), ("wffo", wffo, D)], sems)

        hv = h1_ref[...]
        inv = lax.rsqrt(jnp.mean(hv * hv, axis=-1, keepdims=True) + EPS)
        v = (hv * inv * g2_ref[...]).astype(_MXU)
        v_ref[...] = v
        acc = hv
        for ch in range(2):
            cs = pl.ds(ch * half, half)
            gf = _nt(v, wffnT[pl.ds(ch * half, half), :]).astype(_MXU)
            uf = _nt(v, wffnT[pl.ds(FF + ch * half, half), :]).astype(_MXU)
            gf_ref[:, cs] = gf
            uf_ref[:, cs] = uf
            gf32 = gf.astype(F32)
            act = (gf32 * _sigmoid(gf32) * uf.astype(F32)).astype(_MXU)
            acc = acc + _nn(act, wffo[pl.ds(ch * half, half), :])
        h2_ref[...] = acc

    tspec = lambda w: pl.BlockSpec((tm, w), lambda i: (i, 0))
    return pl.pallas_call(
        body, name="ffn_fwd", grid=(s // tm,),
        out_shape=(jax.ShapeDtypeStruct((s, D), _MXU), jax.ShapeDtypeStruct((s, FF), _MXU),
                   jax.ShapeDtypeStruct((s, FF), _MXU), jax.ShapeDtypeStruct((s, D), F32)),
        in_specs=[tspec(D), pl.BlockSpec((1, D), lambda i: (0, 0)), pl.BlockSpec(memory_space=pl.ANY)],
        out_specs=(tspec(D), tspec(FF), tspec(FF), tspec(D)),
        scratch_shapes=[pltpu.VMEM((2 * FF, D), _MXU), pltpu.VMEM((FF, D), _MXU), pltpu.SemaphoreType.DMA((2 * NDEV,))],
        compiler_params=_cparams(1),
    )(h1, g2, gw)


def _rms_bwd(dy, xn, inv, g):
    dg = jnp.sum(dy * xn, axis=0, keepdims=True)
    dxn = dy * g
    dx = inv * (dxn - xn * jnp.mean(dxn * xn, axis=-1, keepdims=True))
    return dx, dg


def _ple_loss_fwd_bwd(h2, p, target, g3, gfin, gw):
    s = h2.shape[0]
    tm = min(TM, s)

    def body(h2_ref, p_ref, t_ref, g3_ref, gf_ref, gw_ref,
             dh2_ref, n3_ref, dpg_ref, de_ref, loss_ref, dg3_ref, dgf_ref, wpg, pleT, sems):
        i = pl.program_id(0)

        @pl.when(i == 0)
        def _():
            _load_weights(gw_ref, [("wpg", wpg, D), ("ple", pleT, PLE)], sems)
            loss_ref[...] = jnp.zeros_like(loss_ref)
            dg3_ref[...] = jnp.zeros_like(dg3_ref)
            dgf_ref[...] = jnp.zeros_like(dgf_ref)

        hv = h2_ref[...]
        inv3 = lax.rsqrt(jnp.mean(hv * hv, axis=-1, keepdims=True) + EPS)
        xn3 = hv * inv3
        n3 = (xn3 * g3_ref[...]).astype(_MXU)
        n3_ref[...] = n3
        pg = _sigmoid(_nn(n3, wpg[...]))
        e = _nt(p_ref[...].astype(_MXU), pleT[...])
        h3 = hv + pg * e
        invf = lax.rsqrt(jnp.mean(h3 * h3, axis=-1, keepdims=True) + EPS)
        xf = h3 * invf
        diff = xf * gf_ref[...] - t_ref[...]
        loss_ref[...] += jnp.sum(diff * diff) * (0.5 / D)
        dh3, dgf = _rms_bwd(diff * (1.0 / D), xf, invf, gf_ref[...])
        dgf_ref[...] += dgf
        de_ref[...] = (dh3 * pg).astype(_MXU)
        dpg = (dh3 * e * pg * (1.0 - pg)).astype(_MXU)
        dpg_ref[...] = dpg
        dn3 = _nt(dpg, wpg[...])
        dx3, dg3 = _rms_bwd(dn3, xn3, inv3, g3_ref[...])
        dg3_ref[...] += dg3
        dh2_ref[...] = dh3 + dx3

    tspec = lambda w: pl.BlockSpec((tm, w), lambda i: (i, 0))
    vec = pl.BlockSpec((1, D), lambda i: (0, 0))
    tok = lambda w, dt: jax.ShapeDtypeStruct((s, w), dt)
    return pl.pallas_call(
        body, name="ple_loss", grid=(s // tm,),
        out_shape=(tok(D, F32), tok(D, _MXU), tok(D, _MXU), tok(D, _MXU), jax.ShapeDtypeStruct((8, 128), F32),
                   jax.ShapeDtypeStruct((1, D), F32), jax.ShapeDtypeStruct((1, D), F32)),
        in_specs=[tspec(D), tspec(PLE), tspec(D), vec, vec, pl.BlockSpec(memory_space=pl.ANY)],
        out_specs=(tspec(D), tspec(D), tspec(D), tspec(D), pl.BlockSpec((8, 128), lambda i: (0, 0)), vec, vec),
        scratch_shapes=[pltpu.VMEM((D, D), _MXU), pltpu.VMEM((D, PLE), _MXU), pltpu.SemaphoreType.DMA((2 * NDEV,))],
        compiler_params=_cparams(1),
    )(h2, p, target, g3, gfin, gw)


def _ffn_bwd_hidden(dh2, gf, uf, gw):
    s = dh2.shape[0]
    tm = min(TM, s)
    half = FF // 2

    def body(dh2_ref, gf_ref, uf_ref, gw_ref, dff_ref, act_ref, wffo, sems):
        @pl.when(pl.program_id(0) == 0)
        def _():
            _load_weights(gw_ref, [("wffo", wffo, D)], sems)

        dm = dh2_ref[...].astype(_MXU)
        for ch in range(2):
            cs = pl.ds(ch * half, half)
            dact = _nt(dm, wffo[pl.ds(ch * half, half), :])
            gfv = gf_ref[:, cs].astype(F32)
            ufv = uf_ref[:, cs].astype(F32)
            sg = _sigmoid(gfv)
            silu = gfv * sg
            act_ref[:, cs] = (silu * ufv).astype(_MXU)
            dff_ref[:, pl.ds(ch * half, half)] = (dact * ufv * (sg * (1.0 + gfv * (1.0 - sg)))).astype(_MXU)
            dff_ref[:, pl.ds(FF + ch * half, half)] = (dact * silu).astype(_MXU)

    tspec = lambda w: pl.BlockSpec((tm, w), lambda i: (i, 0))
    return pl.pallas_call(
        body, name="ffn_bwd_hidden", grid=(s // tm,),
        out_shape=(jax.ShapeDtypeStruct((s, 2 * FF), _MXU), jax.ShapeDtypeStruct((s, FF), _MXU)),
        in_specs=[tspec(D), tspec(FF), tspec(FF), pl.BlockSpec(memory_space=pl.ANY)],
        out_specs=(tspec(2 * FF), tspec(FF)),
        scratch_shapes=[pltpu.VMEM((FF, D), _MXU), pltpu.SemaphoreType.DMA((NDEV,))],
        compiler_params=_cparams(1),
    )(dh2, gf, uf, gw)


def _proj_norm_bwd(dy, x, dres, g, gw, slab, width, name):
    s = x.shape[0]
    tm = min(TM, s)

    def body(dy_ref, x_ref, dr_ref, g_ref, gw_ref, dx_ref, dg_ref, wT, sems):
        @pl.when(pl.program_id(0) == 0)
        def _():
            _load_weights(gw_ref, [(slab, wT, D)], sems)
            dg_ref[...] = jnp.zeros_like(dg_ref)

        dv = _nn(dy_ref[...], wT[...])
        xv = x_ref[...]
        inv = lax.rsqrt(jnp.mean(xv * xv, axis=-1, keepdims=True) + EPS)
        dx, dg = _rms_bwd(dv, xv * inv, inv, g_ref[...])
        dg_ref[...] += dg
        dx_ref[...] = dr_ref[...] + dx

    tspec = lambda w: pl.BlockSpec((tm, w), lambda i: (i, 0))
    vec = pl.BlockSpec((1, D), lambda i: (0, 0))
    return pl.pallas_call(
        body, name=name, grid=(s // tm,),
        out_shape=(jax.ShapeDtypeStruct((s, D), F32), jax.ShapeDtypeStruct((1, D), F32)),
        in_specs=[tspec(width), tspec(D), tspec(D), vec, pl.BlockSpec(memory_space=pl.ANY)],
        out_specs=(tspec(D), vec),
        scratch_shapes=[pltpu.VMEM((width, D), _MXU), pltpu.SemaphoreType.DMA((NDEV,))],
        compiler_params=_cparams(1),
    )(dy, x, dres, g, gw)


def _mixer_bwd(dh1, z, h, y_pool, y_lru, gw, small):
    s = dh1.shape[0]
    tm = min(TM_SEQ, s)
    nt = s // tm
    (pool_w, pool_scale, conv_w, conv_b, w_rg, b_rg, w_ig, b_ig, lam, b_gate) = small

    def body(dh1_ref, z_ref, zp_ref, h_ref, hp_ref, yp_ref, yr_ref, gw_ref,
             pw_ref, ps_ref, cw_ref, cb_ref, wrg_ref, brg_ref, wig_ref, big_ref, lam_ref, bg_ref,
             dz_ref, dyr_ref, dyp_ref, mx_ref,
             gbg_ref, glam_ref, gbrg_ref, gbig_ref, gcb_ref, gcw_ref, gps_ref, gpw_ref, gwrg_ref, gwig_ref,
             pprojT, lru_w, wout_w, pbuf, lbuf, hbuf, qbuf, xbuf, a_s, g_s, dh_s, hk, pk, r_s, ig_s, xc_s, mu_s, f_s,
             dcar, sems):
        step = pl.program_id(0)
        i = nt - 1 - step
        t0 = i * tm

        @pl.when(step == 0)
        def _():
            _load_weights(gw_ref, [("pproj", pprojT, PW), ("lru", lru_w, D), ("wout", wout_w, D)], sems)
            for ref in (gbg_ref, glam_ref, gbrg_ref, gbig_ref, gcb_ref, gcw_ref, gps_ref, gpw_ref, gwrg_ref, gwig_ref):
                ref[...] = jnp.zeros_like(ref)
            qbuf[pl.ds(tm, HALO), :] = jnp.zeros((HALO, PW), F32)
            xbuf[pl.ds(tm, 8), :] = jnp.zeros((8, D), F32)
            dcar[...] = jnp.zeros_like(dcar)

        first = i == 0
        zprev = jnp.where(first, 0.0, zp_ref[...])
        hprev = jnp.where(first, 0.0, hp_ref[...])

        d_merged = _nt(dh1_ref[...].astype(_MXU), wout_w[...])

        g0 = _sigmoid(z_ref[:, pl.ds(PW + 2 * D, D)] + bg_ref[pl.ds(0, 1), :])
        g1 = _sigmoid(z_ref[:, pl.ds(PW + 3 * D, D)] + bg_ref[pl.ds(1, 1), :])
        dz0 = d_merged * yp_ref[...].astype(F32) * g0 * (1.0 - g0)
        dz1 = d_merged * yr_ref[...].astype(F32) * g1 * (1.0 - g1)
        dz_ref[:, pl.ds(PW + 2 * D, D)] = dz0.astype(_MXU)
        dz_ref[:, pl.ds(PW + 3 * D, D)] = dz1.astype(_MXU)
        gbg_ref[pl.ds(0, 1), :] += jnp.sum(dz0, axis=0, keepdims=True)
        gbg_ref[pl.ds(1, 1), :] += jnp.sum(dz1, axis=0, keepdims=True)
        d_ypool = (d_merged * g0).astype(_MXU)
        d_ylru = (d_merged * g1).astype(_MXU)
        dyp_ref[...] = d_ypool
        dyr_ref[...] = d_ylru

        d_yl = _nt(d_ylru, lru_w[...])
        gel, dgel = _gelu_and_grad(z_ref[:, pl.ds(PW + D, D)])
        dz_ref[:, pl.ds(PW + D, D)] = (d_yl * h_ref[...] * dgel).astype(_MXU)
        g_full = d_yl * gel
        lbuf[pl.ds(0, HALO), :] = zprev[:, PW:PW + D]
        lbuf[pl.ds(HALO, tm), :] = z_ref[:, pl.ds(PW, D)]
        hbuf[pl.ds(0, 8), :] = hprev
        hbuf[pl.ds(8, tm), :] = h_ref[...]
        sp, sneg = _softplus_neg(lam_ref[...])
        start = (t0 + lax.broadcasted_iota(jnp.int32, (tm, HD), 0)) == 0
        for hh in range(HEADS):
            cs = pl.ds(hh * HD, HD)
            xc, r, ig, a, live, inv_mult, mult = _lru_gates_head(hh, lbuf, start, tm, cw_ref, cb_ref, wrg_ref,
                                                                 brg_ref, wig_ref, big_ref, sp)
            _to_segments(a_s, hh, a, tm)
            _to_segments(g_s, hh, g_full[:, hh * HD:(hh + 1) * HD], tm)
            r_s[:, cs] = r
            ig_s[:, cs] = ig
            xc_s[:, cs] = xc
            mu_s[:, cs] = mult
            f_s[:, cs] = jnp.where(live, -(a * a) * inv_mult, 0.0)
        _segment_scan(a_s, g_s, dh_s, hk, pk, dcar, tm, reverse=True)
        for hh in range(HEADS):
            cs = pl.ds(hh * HD, HD)
            a = _from_segments(a_s, hh, tm)
            r = r_s[:, cs]
            ig = ig_s[:, cs]
            xc = xc_s[:, cs]
            mult = mu_s[:, cs]
            dh = _from_segments(dh_s, hh, tm)
            d_mult = dh * ig * xc
            d_loga = dh * hbuf[pl.ds(7, tm), cs] * a + d_mult * f_s[:, cs]
            glam_ref[:, cs] += jnp.sum(d_loga * (LRU_C * r) * sneg[:, hh * HD:(hh + 1) * HD], axis=0, keepdims=True)
            d_rpre = d_loga * (-LRU_C * sp[:, hh * HD:(hh + 1) * HD]) * r * (1.0 - r)
            d_igpre = dh * mult * xc * ig * (1.0 - ig)
            gbrg_ref[pl.ds(hh, 1), :] += jnp.sum(d_rpre, axis=0, keepdims=True)
            gbig_ref[pl.ds(hh, 1), :] += jnp.sum(d_igpre, axis=0, keepdims=True)
            drm = d_rpre.astype(_MXU)
            dim = d_igpre.astype(_MXU)
            xcm = xc.astype(_MXU)
            gwrg_ref[hh] += _tn(xcm, drm)
            gwig_ref[hh] += _tn(xcm, dim)
            d_xc = dh * mult * ig + _nt(drm, wrg_ref[hh]) + _nt(dim, wig_ref[hh])
            gcb_ref[:, cs] += jnp.sum(d_xc, axis=0, keepdims=True)
            for k in range(CONV):
                gcw_ref[pl.ds(k, 1), cs] += jnp.sum(d_xc * lbuf[pl.ds(HALO - (CONV - 1) + k, tm), cs], axis=0,
                                                    keepdims=True)
            xbuf[pl.ds(0, tm), cs] = d_xc
        dzl = cw_ref[pl.ds(CONV - 1, 1), :] * xbuf[pl.ds(0, tm), :]
        for k in range(CONV - 1):
            dzl = dzl + cw_ref[pl.ds(k, 1), :] * xbuf[pl.ds(CONV - 1 - k, tm), :]
        dz_ref[:, pl.ds(PW, D)] = dzl.astype(_MXU)
        xbuf[pl.ds(tm, 8), :] = xbuf[pl.ds(0, 8), :]

        d_mixed = _nn(d_ypool, pprojT[...])
        pbuf[pl.ds(0, HALO), :] = zprev[:, 0:PW]
        pbuf[pl.ds(HALO, tm), :] = z_ref[:, pl.ds(0, PW)]
        pooled, mixed_pre = _pool_tile(pbuf, t0, tm, pw_ref, ps_ref)
        mp = jnp.concatenate(mixed_pre, axis=1)
        mx_ref[...] = (mp * ps_ref[...]).astype(_MXU)
        gps_ref[...] += jnp.sum(d_mixed * mp, axis=0, keepdims=True)
        d_mp = (d_mixed * ps_ref[...]).astype(_MXU)
        t = t0 + lax.broadcasted_iota(jnp.int32, (tm, GD), 0)
        d_pooled = []
        for g, w in enumerate(WINDOWS):
            dmg = d_mp[:, g * GD:(g + 1) * GD]
            gpw_ref[g] += _tn(pooled[g].astype(_MXU), dmg)
            dp = _nt(dmg, pw_ref[g])
            d_pooled.append(dp)
            qbuf[pl.ds(0, tm), pl.ds(g * GD, GD)] = dp / jnp.minimum(t + 1, w).astype(F32)
        for g, w in enumerate(WINDOWS):
            cs = pl.ds(g * GD, GD)
            acc = qbuf[pl.ds(0, tm), cs]
            for d in range(1, w):
                acc = acc + qbuf[pl.ds(d, tm), cs]
            dz_ref[:, cs] = (acc - d_pooled[g]).astype(_MXU)
        qbuf[pl.ds(tm, HALO), :] = qbuf[pl.ds(0, HALO), :]

    rev = lambda w: pl.BlockSpec((tm, w), lambda g: (nt - 1 - g, 0))
    prev = lambda rows, w: pl.BlockSpec((rows, w), lambda g: (jnp.maximum((nt - 1 - g) * (tm // rows) - 1, 0), 0))
    full = lambda a: pl.BlockSpec(a.shape, lambda g: (0,) * a.ndim)
    tok = lambda w, dt: jax.ShapeDtypeStruct((s, w), dt)
    acc_shapes = [(2, D), (1, D), (HEADS, HD), (HEADS, HD), (1, D), (CONV, D), (1, PW), (GROUPS, GD, GD),
                  (HEADS, HD, HD), (HEADS, HD, HD)]
    acc_specs = tuple(pl.BlockSpec(sh, lambda g, n=len(sh): (0,) * n) for sh in acc_shapes)
    seg_buf = pltpu.VMEM((HEADS, 8 * _seg_layout(tm)[1], HD), F32)
    tile_buf = pltpu.VMEM((tm, D), F32)
    return pl.pallas_call(
        body, name="mixer_bwd", grid=(nt,),
        out_shape=(tok(NIN, _MXU), tok(D, _MXU), tok(D, _MXU), tok(PW, _MXU))
        + tuple(jax.ShapeDtypeStruct(sh, F32) for sh in acc_shapes),
        in_specs=[rev(D), rev(NIN), prev(HALO, NIN), rev(D), prev(8, D), rev(D), rev(D),
                  pl.BlockSpec(memory_space=pl.ANY)] + [full(a) for a in small],
        out_specs=(rev(NIN), rev(D), rev(D), rev(PW)) + acc_specs,
        scratch_shapes=[pltpu.VMEM((D, PW), _MXU), pltpu.VMEM((D, D), _MXU), pltpu.VMEM((D, D), _MXU),
                        pltpu.VMEM((tm + HALO, PW), F32), pltpu.VMEM((tm + HALO, D), F32),
                        pltpu.VMEM((tm + 8, D), F32), pltpu.VMEM((tm + HALO, PW), F32), pltpu.VMEM((tm + 8, D), F32),
                        seg_buf, seg_buf, seg_buf, pltpu.VMEM((HEADS, tm, HD), F32), pltpu.VMEM((HEADS, tm, HD), F32),
                        tile_buf, tile_buf, tile_buf, tile_buf, tile_buf,
                        pltpu.VMEM((8, D), F32), pltpu.SemaphoreType.DMA((3 * NDEV,))],
        compiler_params=_cparams(1),
    )(dh1, z, z, h, h, y_pool, y_lru, gw, *small)


def _split3(a):
    hi = a.astype(jnp.bfloat16).astype(F32)
    mid = (a - hi).astype(jnp.bfloat16).astype(F32)
    lo = (a - hi - mid).astype(jnp.bfloat16).astype(F32)
    return jnp.stack([hi, mid, lo])


def _small_pack(parts):
    flat = jnp.concatenate([a.reshape(-1) for a in parts])
    return jnp.pad(flat, (0, NDEV * SMALL_ROWS * D - flat.shape[0])).reshape(NDEV * SMALL_ROWS, D)


def _small_unpack(packed, shapes):
    flat = packed.reshape(-1)
    out, o = [], 0
    for sh in shapes:
        n = math.prod(sh)
        out.append(flat[o:o + n].reshape(sh))
        o += n
    return out


def kernel(x, p, norm1_g, w_in, b_gate, pool_w, pool_scale, pool_proj, conv_w, conv_b, w_rg, b_rg, w_ig, b_ig, lru_lambda, lru_proj, w_out, norm2_g, w_ffn_in, w_ffn_out, ple_norm_g, w_ple_gate, w_ple_proj, final_g, loss_target, m_norm1_g, m_w_in, m_b_gate, m_pool_w, m_pool_scale, m_pool_proj, m_conv_w, m_conv_b, m_w_rg, m_b_rg, m_w_ig, m_b_ig, m_lru_lambda, m_lru_proj, m_w_out, m_norm2_g, m_w_ffn_in, m_w_ffn_out, m_ple_norm_g, m_w_ple_gate, m_w_ple_proj, m_final_g, v_norm1_g, v_w_in, v_b_gate, v_pool_w, v_pool_scale, v_pool_proj, v_conv_w, v_conv_b, v_w_rg, v_b_rg, v_w_ig, v_b_ig, v_lru_lambda, v_lru_proj, v_w_out, v_norm2_g, v_w_ffn_in, v_w_ffn_out, v_ple_norm_g, v_w_ple_gate, v_w_ple_proj, v_final_g):
    axes = ("x", "y", "c")
    me = 4 * lax.axis_index("x") + 2 * lax.axis_index("y") + lax.axis_index("c")
    x2 = x[0]
    p2 = p[0, 0]
    tgt = loss_target[0]

    n_small = (CONV + 2) * 128
    small_terms = _split3(jnp.concatenate([conv_w[0].reshape(-1), b_gate[0].reshape(-1)]))
    small_rows = jnp.pad(small_terms, ((0, 16 - 3), (0, D - n_small)))
    own = jnp.concatenate([
        w_in[0].T.astype(_MXU), w_ffn_in[0].T.astype(_MXU),
        jnp.pad(pool_proj[0].T, ((0, 0), (0, D - PW))).astype(_MXU),
        jnp.pad(w_ple_proj[0].T, ((0, 0), (0, D - PLE))).astype(_MXU),
        lru_proj[0].astype(_MXU), w_out[0].astype(_MXU), w_ffn_out[0].astype(_MXU), w_ple_gate[0].astype(_MXU),
        small_rows.astype(_MXU),
    ], axis=0)
    gw = _all_gather_weights(own)
    off = W_OFF["f32s"][0]
    st = gw[:, off:off + 3, :n_small].astype(F32)
    sf = st[:, 0] + st[:, 1] + st[:, 2]
    conv_w_full = sf[:, :CONV * 128].reshape(NDEV, CONV, 128).transpose(1, 0, 2).reshape(CONV, D)
    b_gate_full = sf[:, CONV * 128:].reshape(NDEV, 2, 128).transpose(1, 0, 2).reshape(2, D)

    small = (pool_w[0].astype(_MXU), pool_scale, conv_w_full, conv_b, w_rg[0].astype(_MXU), b_rg[0],
             w_ig[0].astype(_MXU), b_ig[0], lru_lambda, b_gate_full)

    u, z = _inproj_fwd(x2, norm1_g, gw)
    h, yl, merged, y_pool, y_lru, h1 = _mixer_fwd(z, x2, gw, small)
    v, gf, uf, h2 = _ffn_fwd(h1, norm2_g, gw)

    dh2, n3, dpg, de, loss_blk, g_ple_norm, g_final = _ple_loss_fwd_bwd(h2, p2, tgt, ple_norm_g, final_g.reshape(1, D), gw)
    loss = lax.psum(loss_blk[0, 0], axes)
    dff, act = _ffn_bwd_hidden(dh2, gf, uf, gw)
    dh1, g_norm2 = _proj_norm_bwd(dff, h1, dh2, norm2_g, gw, "wffn", 2 * FF, "ffn_bwd_in")
    (dz, d_ylru, d_ypool, mixed, g_bgate, g_lam, g_brg, g_big, g_convb, g_convw, g_pscale, g_poolw, g_wrg,
     g_wig) = _mixer_bwd(dh1, z, h, y_pool, y_lru, gw, small)
    grad_x, g_norm1 = _proj_norm_bwd(dz, x2, dh1, norm1_g, gw, "win", NIN, "inproj_bwd")

    parts = [
        _grad_matmul(dz, u, "grad_w_in", NIN // 4),
        _grad_matmul(dff, v, "grad_w_ffn_in", 2 * FF // 4),
        _grad_matmul(d_ypool, mixed, "grad_pool_proj", D),
        _grad_matmul(de, p2, "grad_w_ple_proj", D),
        _grad_matmul(yl, d_ylru, "grad_lru_proj", D),
        _grad_matmul(merged, dh1, "grad_w_out", D),
        _grad_matmul(act, dh2, "grad_w_ffn_out", FF // 2),
        _grad_matmul(n3, dpg, "grad_w_ple_gate", D),
    ]
    small_shapes = [(1, D), (GROUPS, GD, GD), (1, PW), (1, D), (HEADS, HD, HD), (HEADS, HD), (HEADS, HD, HD),
                    (HEADS, HD), (1, D), (1, D), (1, D), (1, D), (2, D), (CONV, D)]
    parts.append(_small_pack([g_norm1, g_poolw, g_pscale, g_convb, g_wrg, g_brg, g_wig, g_big, g_lam, g_norm2,
                              g_ple_norm, g_final, g_bgate, g_convw]))

    own_a, land_a = _rs_d2d(parts)
    t_sum, t_ici = _sum_arrays([own_a.reshape(4 * RG, D), land_a.reshape(4 * RG, D)], "rs_sum_d2d", narrow=_MXU)
    t_sum = t_sum.reshape(4, RG, D)
    land_n, land_f = _rs_ici(t_ici.reshape(4, RG, D), t_sum)
    mine = lax.dynamic_index_in_dim(t_sum, 2 * lax.axis_index("x") + lax.axis_index("y"), 0, keepdims=False)
    n_big = RG - SMALL_ROWS
    red = _sum_arrays([mine[:n_big], land_n[0], land_n[1], land_n[2]], "rs_sum_ici", target=n_big // 2)
    red_small = _sum_arrays([mine[n_big:], land_f[0], land_f[1], land_f[2]], "rs_sum_ici_small")

    def slab(name):
        o, r = G_OFF[name]
        return red_small if name == "small" else red[o:o + r]

    g_w_in = slab("win").T
    g_w_ffn_in = slab("wffn").T
    g_pool_proj = slab("pproj")[:, :PW].T
    g_w_ple_proj = slab("ple")[:, :PLE].T
    g_lru_proj, g_w_out, g_w_ffn_out, g_w_ple_gate = slab("lru"), slab("wout"), slab("wffo"), slab("wpg")
    small_red = _all_gather_small(slab("small"))
    (gs_norm1, gs_poolw, gs_pscale, gs_convb, gs_wrg, gs_brg, gs_wig, gs_big, gs_lam, gs_norm2, gs_ple_norm,
     gs_final, gs_bgate, gs_convw) = _small_unpack(small_red, small_shapes)
    g_b_gate = lax.dynamic_slice_in_dim(gs_bgate, me * 128, 128, axis=1)
    g_conv_w = lax.dynamic_slice_in_dim(gs_convw, me * 128, 128, axis=1)

    grads = {
        "norm1_g": gs_norm1, "w_in": g_w_in[None], "b_gate": g_b_gate[None], "pool_w": gs_poolw[None],
        "pool_scale": gs_pscale, "pool_proj": g_pool_proj[None], "conv_w": g_conv_w[None], "conv_b": gs_convb,
        "w_rg": gs_wrg[None], "b_rg": gs_brg[None], "w_ig": gs_wig[None], "b_ig": gs_big[None], "lru_lambda": gs_lam,
        "lru_proj": g_lru_proj[None], "w_out": g_w_out[None], "norm2_g": gs_norm2, "w_ffn_in": g_w_ffn_in[None],
        "w_ffn_out": g_w_ffn_out[None], "ple_norm_g": gs_ple_norm, "w_ple_gate": g_w_ple_gate[None],
        "w_ple_proj": g_w_ple_proj[None], "final_g": gs_final.reshape(D),
    }
    weights = dict(norm1_g=norm1_g, w_in=w_in, b_gate=b_gate, pool_w=pool_w, pool_scale=pool_scale, pool_proj=pool_proj,
                   conv_w=conv_w, conv_b=conv_b, w_rg=w_rg, b_rg=b_rg, w_ig=w_ig, b_ig=b_ig, lru_lambda=lru_lambda,
                   lru_proj=lru_proj, w_out=w_out, norm2_g=norm2_g, w_ffn_in=w_ffn_in, w_ffn_out=w_ffn_out,
                   ple_norm_g=ple_norm_g, w_ple_gate=w_ple_gate, w_ple_proj=w_ple_proj, final_g=final_g)
    moments_m = dict(norm1_g=m_norm1_g, w_in=m_w_in, b_gate=m_b_gate, pool_w=m_pool_w, pool_scale=m_pool_scale,
                     pool_proj=m_pool_proj, conv_w=m_conv_w, conv_b=m_conv_b, w_rg=m_w_rg, b_rg=m_b_rg, w_ig=m_w_ig,
                     b_ig=m_b_ig, lru_lambda=m_lru_lambda, lru_proj=m_lru_proj, w_out=m_w_out, norm2_g=m_norm2_g,
                     w_ffn_in=m_w_ffn_in, w_ffn_out=m_w_ffn_out, ple_norm_g=m_ple_norm_g, w_ple_gate=m_w_ple_gate,
                     w_ple_proj=m_w_ple_proj, final_g=m_final_g)
    moments_v = dict(norm1_g=v_norm1_g, w_in=v_w_in, b_gate=v_b_gate, pool_w=v_pool_w, pool_scale=v_pool_scale,
                     pool_proj=v_pool_proj, conv_w=v_conv_w, conv_b=v_conv_b, w_rg=v_w_rg, b_rg=v_b_rg, w_ig=v_w_ig,
                     b_ig=v_b_ig, lru_lambda=v_lru_lambda, lru_proj=v_lru_proj, w_out=v_w_out, norm2_g=v_norm2_g,
                     w_ffn_in=v_w_ffn_in, w_ffn_out=v_w_ffn_out, ple_norm_g=v_ple_norm_g, w_ple_gate=v_w_ple_gate,
                     w_ple_proj=v_w_ple_proj, final_g=v_final_g)
    names = list(weights)
    big = ("w_in", "w_ffn_in", "w_ffn_out", "lru_proj", "w_out", "w_ple_gate", "pool_proj", "w_ple_proj")
    delta, new_m, new_v = {}, {}, {}
    for n in big:
        sh = weights[n].shape
        as2d = lambda a: a.reshape(sh[-2], sh[-1])
        d_, m_, v_ = _adamw(as2d(weights[n]), as2d(grads[n]), as2d(moments_m[n]), as2d(moments_v[n]), "adamw_" + n)
        delta[n], new_m[n], new_v[n] = d_.reshape(sh), m_.reshape(sh), v_.reshape(sh)
    rest = [n for n in names if n not in big]
    rest_shapes = [weights[n].shape for n in rest]
    packed = [_small_pack([src[n] for n in rest]) for src in (weights, grads, moments_m, moments_v)]
    d_, m_, v_ = _adamw(*packed, "adamw_small")
    for n, a, b_, c_ in zip(rest, _small_unpack(d_, rest_shapes), _small_unpack(m_, rest_shapes),
                            _small_unpack(v_, rest_shapes)):
        delta[n], new_m[n], new_v[n] = a, b_, c_

    return (loss, grad_x[None], *[grads[n] for n in names], *[delta[n] for n in names],
            *[new_m[n] for n in names], *[new_v[n] for n in names])
```

```python
import functools
import math

import jax
import jax.numpy as jnp
from jax import lax
from jax.experimental import pallas as pl
from jax.experimental.pallas import tpu as pltpu

F32 = jnp.float32
D = 1024
NIN = 4608
PW = 512
FF = 2816
PLE = 256
HEADS, HD = 8, 128
GROUPS, GD = 4, 128
WINDOWS = (2, 4, 8, 16)
HALO = 16
CONV = 4
EPS = 1e-6
LRU_C = 8.0
NDEV = 8
MESH = pl.DeviceIdType.MESH

ADAM_LR, ADAM_B1, ADAM_B2, ADAM_EPS, ADAM_WD, ADAM_STEP = 0.001, 0.9, 0.999, 1e-08, 0.01, 10

_MXU = jnp.bfloat16
TM = 512
TM_SEQ = 256
VMEM_LIMIT = 56 * 1024 * 1024

W_SLABS = (("win", 576), ("wffn", 704), ("pproj", 128), ("ple", 128), ("lru", 128), ("wout", 128),
           ("wffo", 352), ("wpg", 128), ("f32s", 16))
W_OFF = {}
_o = 0
for _n, _r in W_SLABS:
    W_OFF[_n] = (_o, _r)
    _o += _r
RW = _o
SMALL_ROWS = 48


def _cparams(n_axes=1, vmem=VMEM_LIMIT):
    return pltpu.CompilerParams(dimension_semantics=("arbitrary",) * n_axes, vmem_limit_bytes=vmem)


def _my_pos():
    return lax.axis_index("x"), lax.axis_index("y"), lax.axis_index("c")


def _nt(a, b):
    return lax.dot_general(a, b, (((1,), (1,)), ((), ())), preferred_element_type=F32)


def _nn(a, b):
    return lax.dot_general(a, b, (((1,), (0,)), ((), ())), preferred_element_type=F32)


def _tn(a, b):
    return lax.dot_general(a, b, (((0,), (0,)), ((), ())), preferred_element_type=F32)


def _sigmoid(x):
    return 0.5 * jnp.tanh(0.5 * x) + 0.5


_GELU_K = math.sqrt(2.0 / math.pi)


def _gelu_and_grad(x):
    x2 = x * x
    inner = _GELU_K * (x + 0.044715 * x2 * x)
    t = jnp.tanh(inner)
    g = 0.5 * x * (1.0 + t)
    dg = 0.5 * (1.0 + t) + 0.5 * x * (1.0 - t * t) * _GELU_K * (1.0 + 3.0 * 0.044715 * x2)
    return g, dg


def _softplus_neg(lam):
    x = -lam
    t = jnp.exp(-jnp.abs(x))
    u = 1.0 + t
    l1p = jnp.where(u == 1.0, t, jnp.log(u) * t / (u - 1.0))
    return jnp.maximum(x, 0.0) + l1p, _sigmoid(x)


def _start_slab_loads(g_ref, name, dst_ref, sems, base, width=D):
    off, rows = W_OFF[name]
    copies = []
    for k in range(NDEV):
        if width == D:
            src = g_ref.at[k, pl.ds(off, rows), :]
        else:
            src = g_ref.at[k, pl.ds(off, rows), pl.ds(0, width)]
        cp = pltpu.make_async_copy(src, dst_ref.at[pl.ds(k * rows, rows), :], sems.at[base + k])
        cp.start()
        copies.append(cp)
    return copies


def _load_weights(g_ref, items, sems):
    copies = []
    for n, (name, dst, width) in enumerate(items):
        copies += _start_slab_loads(g_ref, name, dst, sems, n * NDEV, width)
    for cp in copies:
        cp.wait()


def _all_gather_weights(own):
    rows, cols = own.shape

    def body(own_ref, out_ref, stage, send_sems, recv_sems, local_sem):
        x, y, c = _my_pos()
        me, sibling = (x, y, c), (x, y, 1 - c)
        chips = [(1 - x, y), (x, 1 - y), (1 - x, 1 - y)]

        def slab(px, py, pc):
            return out_ref.at[4 * px + 2 * py + pc]

        def copy(k, block, to, src=None):
            return pltpu.make_async_remote_copy(
                src_ref=slab(*block) if src is None else src, dst_ref=slab(*block),
                send_sem=send_sems.at[k], recv_sem=recv_sems.at[k], device_id=to, device_id_type=MESH)

        pltpu.sync_copy(own_ref, stage)
        mine = pltpu.make_async_copy(stage, slab(*me), local_sem)
        mine.start()
        first = [copy(0, me, sibling, src=stage)]
        first += [copy(1 + j, me, (*chip, c), src=stage) for j, chip in enumerate(chips)]
        for cp in first:
            cp.start()
        passed = [copy(4 + j, (*chip, c), sibling) for j, chip in enumerate(chips)]
        for j, chip in enumerate(chips):
            copy(1 + j, (*chip, c), me).wait_recv()
            passed[j].start()
        copy(0, sibling, me).wait_recv()
        for j, chip in enumerate(chips):
            copy(4 + j, (*chip, 1 - c), me).wait_recv()
        for cp in first + passed:
            cp.wait_send()
        mine.wait()

    return pl.pallas_call(
        body, name="ag_weights",
        out_shape=jax.ShapeDtypeStruct((NDEV, rows, cols), own.dtype),
        in_specs=[pl.BlockSpec(memory_space=pl.ANY)],
        out_specs=pl.BlockSpec(memory_space=pl.ANY),
        scratch_shapes=[pltpu.VMEM((rows, cols), own.dtype), pltpu.SemaphoreType.DMA((7,)),
                        pltpu.SemaphoreType.DMA((7,)), pltpu.SemaphoreType.DMA],
        compiler_params=pltpu.CompilerParams(vmem_limit_bytes=VMEM_LIMIT),
    )(own)


def _all_gather_small(piece):
    rows = piece.shape[0]

    def body(p_ref, out_ref, send_sems, recv_sems, local_sem):
        x, y, c = _my_pos()
        me = 4 * x + 2 * y + c
        mine = pltpu.make_async_copy(p_ref, out_ref.at[pl.ds(pl.multiple_of(me * rows, 8), rows), :], local_sem)
        mine.start()
        sends = []
        peers = []
        for r in range(1, NDEV):
            px = 1 - x if (r >> 2) & 1 else x
            py = 1 - y if (r >> 1) & 1 else y
            pc = 1 - c if r & 1 else c
            peers.append((px, py, pc))
            cp = pltpu.make_async_remote_copy(
                src_ref=p_ref, dst_ref=out_ref.at[pl.ds(pl.multiple_of(me * rows, 8), rows), :],
                send_sem=send_sems.at[r - 1], recv_sem=recv_sems.at[r - 1], device_id=(px, py, pc),
                device_id_type=MESH)
            cp.start()
            sends.append(cp)
        for r, (px, py, pc) in enumerate(peers):
            them = 4 * px + 2 * py + pc
            pltpu.make_async_remote_copy(
                src_ref=p_ref, dst_ref=out_ref.at[pl.ds(pl.multiple_of(them * rows, 8), rows), :],
                send_sem=send_sems.at[r], recv_sem=recv_sems.at[r], device_id=(px, py, pc),
                device_id_type=MESH).wait_recv()
        for cp in sends:
            cp.wait_send()
        mine.wait()

    return pl.pallas_call(
        body, name="ag_small",
        out_shape=jax.ShapeDtypeStruct((NDEV * rows, piece.shape[1]), piece.dtype),
        in_specs=[pl.BlockSpec(memory_space=pltpu.VMEM)],
        out_specs=pl.BlockSpec(memory_space=pl.ANY),
        scratch_shapes=[pltpu.SemaphoreType.DMA((7,)), pltpu.SemaphoreType.DMA((7,)), pltpu.SemaphoreType.DMA],
    )(piece)


def _row_block(rows, target=512, mult=8):
    b = min(rows, target) // mult * mult
    while rows % b:
        b -= mult
    return b


def _sum_arrays(arrs, name, narrow=None, target=464):
    rows, cols = arrs[0].shape
    br = _row_block(rows, target, 16)
    n = len(arrs)

    def body(*refs):
        acc = refs[0][...].astype(F32)
        for r in refs[1:n]:
            acc = acc + r[...].astype(F32)
        refs[n][...] = acc
        if narrow is not None:
            refs[n + 1][...] = acc.astype(narrow)

    spec = pl.BlockSpec((br, cols), lambda i: (i, 0))
    shape = jax.ShapeDtypeStruct((rows, cols), F32)
    if narrow is None:
        out_shape, out_specs = shape, spec
    else:
        out_shape, out_specs = (shape, jax.ShapeDtypeStruct((rows, cols), narrow)), (spec, spec)
    return pl.pallas_call(
        body, name=name, grid=(rows // br,), out_shape=out_shape,
        in_specs=[spec] * n, out_specs=out_specs, compiler_params=_cparams(1),
    )(*arrs)


def _adamw(w, g, m, v, name):
    rows, cols = w.shape
    br = _row_block(rows, 256)

    def body(w_ref, g_ref, m_ref, v_ref, d_ref, nm_ref, nv_ref):
        g_ = g_ref[...]
        m_ = ADAM_B1 * m_ref[...] + (1.0 - ADAM_B1) * g_
        v_ = ADAM_B2 * v_ref[...] + (1.0 - ADAM_B2) * (g_ * g_)
        m_hat = m_ / (1.0 - ADAM_B1 ** ADAM_STEP)
        v_hat = v_ / (1.0 - ADAM_B2 ** ADAM_STEP)
        d_ref[...] = -ADAM_LR * (m_hat / (jnp.sqrt(v_hat) + ADAM_EPS) + ADAM_WD * w_ref[...])
        nm_ref[...] = m_
        nv_ref[...] = v_

    spec = pl.BlockSpec((br, cols), lambda i: (i, 0))
    shape = jax.ShapeDtypeStruct((rows, cols), F32)
    return pl.pallas_call(
        body, name=name, grid=(rows // br,), out_shape=(shape, shape, shape),
        in_specs=[spec] * 4, out_specs=(spec, spec, spec), compiler_params=_cparams(1),
    )(w, g, m, v)


_CHIP_FLIPS = (2, 3, 1, 0)


def _grad_matmul_rs(lhs, rhs, name, rows, chips_per_block=1, extra=None, narrow=None):
    s, r8 = lhs.shape
    k = rhs.shape[1]
    tm = min(TM, s)
    nt = s // tm
    cpb = chips_per_block
    nblk = 4 // cpb
    er = 0 if extra is None else extra.shape[0] // NDEV
    srows = rows + er
    brows = 2 * cpb * srows
    wire = F32 if narrow is None else narrow
    mid = nt // 2

    def flip_of(p):
        return jnp.where(p == 0, 2, jnp.where(p == 1, 3, jnp.where(p == 2, 1, 0)))

    def block_col(b):
        x, y, _ = _my_pos()
        mine = 2 * x + y
        if cpb == 1:
            return mine ^ flip_of(b)
        return jnp.where(b == 0, (mine >> 1) ^ 1, mine >> 1)

    def body(*refs):
        if extra is None:
            l_ref, r_ref, town_ref, lici_ref, ld2d_ref, acc, stage, send_buf, dsend, drecv, isend, irecv, lsem = refs
            x_ref = None
        else:
            (l_ref, r_ref, x_ref, town_ref, lici_ref, ld2d_ref, acc, stage, send_buf, dsend, drecv, isend, irecv,
             lsem) = refs
        b = pl.program_id(0)
        i = pl.program_id(1)
        x, y, c = _my_pos()
        mine = 2 * x + y
        sibling = (x, y, 1 - c)

        def chip_at(p):
            return mine ^ _CHIP_FLIPS[p]

        def slab_rows(p, parity):
            within = 0 if cpb == 1 else (chip_at(p) & 1) * 2
            return pl.ds(pl.multiple_of((within + parity) * srows, 8), srows)

        def push(p, slot):
            return pltpu.make_async_remote_copy(
                src_ref=acc.at[slot, slab_rows(p, 1 - c), :], dst_ref=ld2d_ref.at[p],
                send_sem=dsend.at[p], recv_sem=drecv.at[p], device_id=sibling, device_id_type=MESH)

        def ici(p):
            ch = chip_at(p)
            return pltpu.make_async_remote_copy(
                src_ref=send_buf.at[p % 2], dst_ref=lici_ref.at[p],
                send_sem=isend.at[p], recv_sem=irecv.at[p], device_id=(ch >> 1, ch & 1, c), device_id_type=MESH)

        def combine(p, slot):
            push(p, slot).wait_recv()
            pltpu.sync_copy(ld2d_ref.at[p], stage)
            total = acc[slot, slab_rows(p, c), :] + stage[...]
            if p == 3:
                stage[...] = total
                pltpu.sync_copy(stage, town_ref)
            else:
                if p == 2:
                    ici(0).wait_send()
                send_buf[p % 2] = total.astype(wire)
                ici(p).start()

        for bb in range(nblk):
            slot = bb % 2
            positions = list(range(bb * cpb, (bb + 1) * cpb))

            @pl.when(jnp.logical_and(b == bb, i == 0))
            def _(bb=bb, slot=slot, positions=positions):
                if bb >= 2:
                    for p in range((bb - 2) * cpb, (bb - 1) * cpb):
                        push(p, slot).wait_send()
                acc[slot] = jnp.zeros((brows, D), F32)
                if extra is not None:
                    for p in positions:
                        for parity in range(2):
                            src = x_ref.at[pl.ds(pl.multiple_of((2 * chip_at(p) + parity) * er, 8), er), :]
                            within = 0 if cpb == 1 else (chip_at(p) & 1) * 2
                            dst = acc.at[slot, pl.ds(pl.multiple_of((within + parity) * srows + rows, 8), er), :]
                            pltpu.sync_copy(src, dst)

            if bb >= 1:
                @pl.when(jnp.logical_and(b == bb, i == mid))
                def _(bb=bb):
                    for p in range((bb - 1) * cpb, bb * cpb):
                        combine(p, (bb - 1) % 2)

        res = _tn(l_ref[...].astype(_MXU), r_ref[...].astype(_MXU))
        slot_now = b % 2
        for q in range(2 * cpb):
            acc[slot_now, pl.ds(q * srows, rows), pl.ds(0, k)] += res[q * rows:(q + 1) * rows, :]

        for bb in range(nblk):
            slot = bb % 2
            positions = list(range(bb * cpb, (bb + 1) * cpb))

            @pl.when(jnp.logical_and(b == bb, i == nt - 1))
            def _(bb=bb, slot=slot, positions=positions):
                for p in positions:
                    push(p, slot).start()
                if bb == nblk - 1:
                    for p in positions:
                        combine(p, slot)
                    for p in range(max(0, (nblk - 2) * cpb), 4):
                        push(p, slot).wait_send()
                    for p in range(1, 3):
                        ici(p).wait_send()
                    for p in range(3):
                        ici(p).wait_recv()

    in_specs = [pl.BlockSpec((tm, 2 * cpb * rows), lambda b, i: (i, block_col(b))),
                pl.BlockSpec((tm, k), lambda b, i: (i, 0))]
    args = [lhs, rhs]
    if extra is not None:
        in_specs.append(pl.BlockSpec(memory_space=pl.ANY))
        args.append(extra)
    any_spec = pl.BlockSpec(memory_space=pl.ANY)
    return pl.pallas_call(
        body, name=name, grid=(nblk, nt),
        out_shape=(jax.ShapeDtypeStruct((srows, D), F32), jax.ShapeDtypeStruct((3, srows, D), wire),
                   jax.ShapeDtypeStruct((4, srows, D), F32)),
        in_specs=in_specs, out_specs=(any_spec, any_spec, any_spec),
        scratch_shapes=[pltpu.VMEM((2, brows, D), F32), pltpu.VMEM((srows, D), F32), pltpu.VMEM((2, srows, D), wire),
                        pltpu.SemaphoreType.DMA((4,)), pltpu.SemaphoreType.DMA((4,)), pltpu.SemaphoreType.DMA((3,)),
                        pltpu.SemaphoreType.DMA((3,)), pltpu.SemaphoreType.DMA],
        compiler_params=_cparams(2),
    )(*args)


def _inproj_fwd(x, g1, gw):
    s = x.shape[0]
    tm = min(TM, s)
    nchunk = 4
    cw = NIN // nchunk

    def body(x_ref, g1_ref, gw_ref, u_ref, z_ref, w_vmem, sems):
        @pl.when(pl.program_id(0) == 0)
        def _():
            _load_weights(gw_ref, [("win", w_vmem, D)], sems)

        xv = x_ref[...]
        inv = lax.rsqrt(jnp.mean(xv * xv, axis=-1, keepdims=True) + EPS)
        u = (xv * inv * g1_ref[...]).astype(_MXU)
        u_ref[...] = u
        for ch in range(nchunk):
            z_ref[:, pl.ds(ch * cw, cw)] = _nt(u, w_vmem[pl.ds(ch * cw, cw), :])

    return pl.pallas_call(
        body, name="inproj_fwd", grid=(s // tm,),
        out_shape=(jax.ShapeDtypeStruct((s, D), _MXU), jax.ShapeDtypeStruct((s, NIN), F32)),
        in_specs=[pl.BlockSpec((tm, D), lambda i: (i, 0)), pl.BlockSpec((1, D), lambda i: (0, 0)),
                  pl.BlockSpec(memory_space=pl.ANY)],
        out_specs=(pl.BlockSpec((tm, D), lambda i: (i, 0)), pl.BlockSpec((tm, NIN), lambda i: (i, 0))),
        scratch_shapes=[pltpu.VMEM((NIN, D), _MXU), pltpu.SemaphoreType.DMA((NDEV,))],
        compiler_params=_cparams(1),
    )(x, g1, gw)


def _pool_tile(pbuf, t0, tm, pw_ref, scale_ref):
    t = t0 + lax.broadcasted_iota(jnp.int32, (tm, GD), 0)
    pooled, mixed_pre = [], []
    for g, w in enumerate(WINDOWS):
        cs = pl.ds(g * GD, GD)
        cur = pbuf[pl.ds(HALO, tm), cs]
        acc = cur
        for d in range(1, w):
            acc = acc + pbuf[pl.ds(HALO - d, tm), cs]
        cnt = jnp.minimum(t + 1, w).astype(F32)
        pg = acc / cnt - cur
        pooled.append(pg)
        mixed_pre.append(_nn(pg.astype(_MXU), pw_ref[g]))
    return pooled, mixed_pre


def _lru_gates_head(hh, lbuf, start, tm, cw_ref, cb_ref, wrg_ref, brg_ref, wig_ref, big_ref, sp):
    cs = pl.ds(hh * HD, HD)
    xc = cb_ref[:, cs] + cw_ref[pl.ds(CONV - 1, 1), cs] * lbuf[pl.ds(HALO, tm), cs]
    for k in range(CONV - 1):
        xc = xc + cw_ref[pl.ds(k, 1), cs] * lbuf[pl.ds(HALO - (CONV - 1) + k, tm), cs]
    xcm = xc.astype(_MXU)
    r = _sigmoid(_nn(xcm, wrg_ref[hh]) + brg_ref[pl.ds(hh, 1), :])
    ig = _sigmoid(_nn(xcm, wig_ref[hh]) + big_ref[pl.ds(hh, 1), :])
    a = jnp.exp(-LRU_C * r * sp[:, hh * HD:(hh + 1) * HD])
    one_m = 1.0 - a * a
    live = jnp.logical_and(one_m > 0.0, jnp.logical_not(start))
    inv_mult = lax.rsqrt(jnp.where(live, one_m, 1.0))
    mult = jnp.where(live, one_m * inv_mult, jnp.where(start, 1.0, 0.0))
    return xc, r, ig, a, live, inv_mult, mult


def _seg_layout(tm):
    seg = tm // 8
    return seg, seg + 8


def _to_segments(dst_ref, hh, val, tm):
    seg, pitch = _seg_layout(tm)
    for s in range(8):
        dst_ref[hh, pl.ds(s * pitch, seg), :] = val[s * seg:(s + 1) * seg, :]


def _from_segments(src_ref, hh, tm):
    seg, pitch = _seg_layout(tm)
    return jnp.concatenate([src_ref[hh, pl.ds(s * pitch, seg), :] for s in range(8)], axis=0)


def _segment_scan(a_ref, b_ref, out_ref, hk, pk, carry_ref, tm, reverse):
    seg, pitch = _seg_layout(tm)
    row = lax.broadcasted_iota(jnp.int32, (8, HD), 0)
    order = range(seg - 1, -1, -1) if reverse else range(seg)
    for hh in range(HEADS):
        cs = pl.ds(hh * HD, HD)
        if reverse:
            a0 = a_ref[hh, pl.ds(0, 8, stride=pitch), :]
            a_wrap = jnp.where(row <= 6, pltpu.roll(a0, 7, 0), 1.0)
        hv = jnp.zeros((8, HD), F32)
        pv = jnp.ones((8, HD), F32)
        for k in order:
            if not reverse:
                av = a_ref[hh, pl.ds(k, 8, stride=pitch), :]
            elif k + 1 < seg:
                av = a_ref[hh, pl.ds(k + 1, 8, stride=pitch), :]
            else:
                av = a_wrap
            hv = av * hv + b_ref[hh, pl.ds(k, 8, stride=pitch), :]
            pv = av * pv
            hk[hh, pl.ds(8 * k, 8), :] = hv
            pk[hh, pl.ds(8 * k, 8), :] = pv
        for d in (1, 2, 4):
            if reverse:
                keep, sh = row < 8 - d, 8 - d
            else:
                keep, sh = row >= d, d
            hv = hv + pv * jnp.where(keep, pltpu.roll(hv, sh, 0), 0.0)
            pv = pv * jnp.where(keep, pltpu.roll(pv, sh, 0), 1.0)
        cin = carry_ref[:, cs]
        ends = hv + pv * cin
        if reverse:
            enter = jnp.where(row <= 6, pltpu.roll(ends, 7, 0), cin)
            carry_ref[:, cs] = jnp.broadcast_to((a0 * ends)[0:1, :], (8, HD))
        else:
            enter = jnp.where(row >= 1, pltpu.roll(ends, 1, 0), cin)
            carry_ref[:, cs] = jnp.broadcast_to(ends[7:8, :], (8, HD))
        for k in range(seg):
            out_ref[hh, pl.ds(k, 8, stride=pitch), :] = hk[hh, pl.ds(8 * k, 8), :] + pk[hh, pl.ds(8 * k, 8), :] * enter


def _mixer_fwd(z, x, gw, small):
    s = x.shape[0]
    tm = min(TM_SEQ, s)
    (pool_w, pool_scale, conv_w, conv_b, w_rg, b_rg, w_ig, b_ig, lam, b_gate) = small

    def body(z_ref, x_ref, gw_ref, pw_ref, ps_ref, cw_ref, cb_ref, wrg_ref, brg_ref, wig_ref, big_ref, lam_ref,
             bg_ref, h_ref, yl_ref, mg_ref, yp_ref, yr_ref, h1_ref,
             pprojT, lru_w, wout_w, pbuf, lbuf, a_s, b_s, h_s, hk, pk, hcar, sems):
        i = pl.program_id(0)
        t0 = i * tm

        @pl.when(i == 0)
        def _():
            _load_weights(gw_ref, [("pproj", pprojT, PW), ("lru", lru_w, D), ("wout", wout_w, D)], sems)
            pbuf[pl.ds(0, HALO), :] = jnp.zeros((HALO, PW), F32)
            lbuf[pl.ds(0, HALO), :] = jnp.zeros((HALO, D), F32)
            hcar[...] = jnp.zeros_like(hcar)

        pbuf[pl.ds(HALO, tm), :] = z_ref[:, pl.ds(0, PW)]
        _, mixed_pre = _pool_tile(pbuf, t0, tm, pw_ref, ps_ref)
        mixed = jnp.concatenate(mixed_pre, axis=1) * ps_ref[...]
        y_pool = _nt(mixed.astype(_MXU), pprojT[...])
        pbuf[pl.ds(0, HALO), :] = pbuf[pl.ds(tm, HALO), :]

        lbuf[pl.ds(HALO, tm), :] = z_ref[:, pl.ds(PW, D)]
        sp, _ = _softplus_neg(lam_ref[...])
        start = (t0 + lax.broadcasted_iota(jnp.int32, (tm, HD), 0)) == 0
        for hh in range(HEADS):
            xc, r, ig, a, _, _, mult = _lru_gates_head(hh, lbuf, start, tm, cw_ref, cb_ref, wrg_ref, brg_ref,
                                                       wig_ref, big_ref, sp)
            _to_segments(a_s, hh, a, tm)
            _to_segments(b_s, hh, mult * ig * xc, tm)
        lbuf[pl.ds(0, HALO), :] = lbuf[pl.ds(tm, HALO), :]
        _segment_scan(a_s, b_s, h_s, hk, pk, hcar, tm, reverse=False)
        for hh in range(HEADS):
            h_ref[:, pl.ds(hh * HD, HD)] = _from_segments(h_s, hh, tm)
        gel, _ = _gelu_and_grad(z_ref[:, pl.ds(PW + D, D)])
        yl = (h_ref[...] * gel).astype(_MXU)
        yl_ref[...] = yl
        y_lru = _nn(yl, lru_w[...])

        g0 = _sigmoid(z_ref[:, pl.ds(PW + 2 * D, D)] + bg_ref[pl.ds(0, 1), :])
        g1 = _sigmoid(z_ref[:, pl.ds(PW + 3 * D, D)] + bg_ref[pl.ds(1, 1), :])
        merged = (g0 * y_pool + g1 * y_lru).astype(_MXU)
        mg_ref[...] = merged
        yp_ref[...] = y_pool.astype(_MXU)
        yr_ref[...] = y_lru.astype(_MXU)
        h1_ref[...] = x_ref[...] + _nn(merged, wout_w[...])

    tok = lambda w, dt: jax.ShapeDtypeStruct((s, w), dt)
    tspec = lambda w: pl.BlockSpec((tm, w), lambda i: (i, 0))
    full = lambda a: pl.BlockSpec(a.shape, lambda i: (0,) * a.ndim)
    seg_buf = pltpu.VMEM((HEADS, 8 * _seg_layout(tm)[1], HD), F32)
    return pl.pallas_call(
        body, name="mixer_fwd", grid=(s // tm,),
        out_shape=(tok(D, F32), tok(D, _MXU), tok(D, _MXU), tok(D, _MXU), tok(D, _MXU), tok(D, F32)),
        in_specs=[tspec(NIN), tspec(D), pl.BlockSpec(memory_space=pl.ANY)] + [full(a) for a in small],
        out_specs=(tspec(D),) * 6,
        scratch_shapes=[pltpu.VMEM((D, PW), _MXU), pltpu.VMEM((D, D), _MXU), pltpu.VMEM((D, D), _MXU),
                        pltpu.VMEM((tm + HALO, PW), F32), pltpu.VMEM((tm + HALO, D), F32),
                        seg_buf, seg_buf, seg_buf, pltpu.VMEM((HEADS, tm, HD), F32), pltpu.VMEM((HEADS, tm, HD), F32),
                        pltpu.VMEM((8, D), F32), pltpu.SemaphoreType.DMA((3 * NDEV,))],
        compiler_params=_cparams(1),
    )(z, x, gw, *small)


def _ffn_fwd(h1, g2, gw):
    s = h1.shape[0]
    tm = min(TM, s)
    half = FF // 2

    def body(h1_ref, g2_ref, gw_ref, v_ref, gf_ref, uf_ref, h2_ref, wffnT, wffo, sems):
        @pl.when(pl.program_id(0) == 0)
        def _():
            _load_weights(gw_ref, [("wffn", wffnT, D), ("wffo", wffo, D)], sems)

        hv = h1_ref[...]
        inv = lax.rsqrt(jnp.mean(hv * hv, axis=-1, keepdims=True) + EPS)
        v = (hv * inv * g2_ref[...]).astype(_MXU)
        v_ref[...] = v
        acc = hv
        for ch in range(2):
            cs = pl.ds(ch * half, half)
            gf = _nt(v, wffnT[pl.ds(ch * half, half), :]).astype(_MXU)
            uf = _nt(v, wffnT[pl.ds(FF + ch * half, half), :]).astype(_MXU)
            gf_ref[:, cs] = gf
            uf_ref[:, cs] = uf
            gf32 = gf.astype(F32)
            act = (gf32 * _sigmoid(gf32) * uf.astype(F32)).astype(_MXU)
            acc = acc + _nn(act, wffo[pl.ds(ch * half, half), :])
        h2_ref[...] = acc

    tspec = lambda w: pl.BlockSpec((tm, w), lambda i: (i, 0))
    return pl.pallas_call(
        body, name="ffn_fwd", grid=(s // tm,),
        out_shape=(jax.ShapeDtypeStruct((s, D), _MXU), jax.ShapeDtypeStruct((s, FF), _MXU),
                   jax.ShapeDtypeStruct((s, FF), _MXU), jax.ShapeDtypeStruct((s, D), F32)),
        in_specs=[tspec(D), pl.BlockSpec((1, D), lambda i: (0, 0)), pl.BlockSpec(memory_space=pl.ANY)],
        out_specs=(tspec(D), tspec(FF), tspec(FF), tspec(D)),
        scratch_shapes=[pltpu.VMEM((2 * FF, D), _MXU), pltpu.VMEM((FF, D), _MXU), pltpu.SemaphoreType.DMA((2 * NDEV,))],
        compiler_params=_cparams(1),
    )(h1, g2, gw)


def _rms_bwd(dy, xn, inv, g):
    dg = jnp.sum(dy * xn, axis=0, keepdims=True)
    dxn = dy * g
    dx = inv * (dxn - xn * jnp.mean(dxn * xn, axis=-1, keepdims=True))
    return dx, dg


def _ple_loss_fwd_bwd(h2, p, target, g3, gfin, gw):
    s = h2.shape[0]
    tm = min(TM, s)

    def body(h2_ref, p_ref, t_ref, g3_ref, gf_ref, gw_ref,
             dh2_ref, n3_ref, dpg_ref, de_ref, loss_ref, dg3_ref, dgf_ref, wpg, pleT, sems):
        i = pl.program_id(0)

        @pl.when(i == 0)
        def _():
            _load_weights(gw_ref, [("wpg", wpg, D), ("ple", pleT, PLE)], sems)
            loss_ref[...] = jnp.zeros_like(loss_ref)
            dg3_ref[...] = jnp.zeros_like(dg3_ref)
            dgf_ref[...] = jnp.zeros_like(dgf_ref)

        hv = h2_ref[...]
        inv3 = lax.rsqrt(jnp.mean(hv * hv, axis=-1, keepdims=True) + EPS)
        xn3 = hv * inv3
        n3 = (xn3 * g3_ref[...]).astype(_MXU)
        n3_ref[...] = n3
        pg = _sigmoid(_nn(n3, wpg[...]))
        e = _nt(p_ref[...].astype(_MXU), pleT[...])
        h3 = hv + pg * e
        invf = lax.rsqrt(jnp.mean(h3 * h3, axis=-1, keepdims=True) + EPS)
        xf = h3 * invf
        diff = xf * gf_ref[...] - t_ref[...]
        loss_ref[...] += jnp.sum(diff * diff) * (0.5 / D)
        dh3, dgf = _rms_bwd(diff * (1.0 / D), xf, invf, gf_ref[...])
        dgf_ref[...] += dgf
        de_ref[...] = (dh3 * pg).astype(_MXU)
        dpg = (dh3 * e * pg * (1.0 - pg)).astype(_MXU)
        dpg_ref[...] = dpg
        dn3 = _nt(dpg, wpg[...])
        dx3, dg3 = _rms_bwd(dn3, xn3, inv3, g3_ref[...])
        dg3_ref[...] += dg3
        dh2_ref[...] = dh3 + dx3

    tspec = lambda w: pl.BlockSpec((tm, w), lambda i: (i, 0))
    vec = pl.BlockSpec((1, D), lambda i: (0, 0))
    tok = lambda w, dt: jax.ShapeDtypeStruct((s, w), dt)
    return pl.pallas_call(
        body, name="ple_loss", grid=(s // tm,),
        out_shape=(tok(D, F32), tok(D, _MXU), tok(D, _MXU), tok(D, _MXU), jax.ShapeDtypeStruct((8, 128), F32),
                   jax.ShapeDtypeStruct((1, D), F32), jax.ShapeDtypeStruct((1, D), F32)),
        in_specs=[tspec(D), tspec(PLE), tspec(D), vec, vec, pl.BlockSpec(memory_space=pl.ANY)],
        out_specs=(tspec(D), tspec(D), tspec(D), tspec(D), pl.BlockSpec((8, 128), lambda i: (0, 0)), vec, vec),
        scratch_shapes=[pltpu.VMEM((D, D), _MXU), pltpu.VMEM((D, PLE), _MXU), pltpu.SemaphoreType.DMA((2 * NDEV,))],
        compiler_params=_cparams(1),
    )(h2, p, target, g3, gfin, gw)


def _ffn_bwd_hidden(dh2, gf, uf, gw):
    s = dh2.shape[0]
    tm = min(TM, s)
    half = FF // 2

    def body(dh2_ref, gf_ref, uf_ref, gw_ref, dff_ref, act_ref, wffo, sems):
        @pl.when(pl.program_id(0) == 0)
        def _():
            _load_weights(gw_ref, [("wffo", wffo, D)], sems)

        dm = dh2_ref[...].astype(_MXU)
        for ch in range(2):
            cs = pl.ds(ch * half, half)
            dact = _nt(dm, wffo[pl.ds(ch * half, half), :])
            gfv = gf_ref[:, cs].astype(F32)
            ufv = uf_ref[:, cs].astype(F32)
            sg = _sigmoid(gfv)
            silu = gfv * sg
            act_ref[:, cs] = (silu * ufv).astype(_MXU)
            dff_ref[:, pl.ds(ch * half, half)] = (dact * ufv * (sg * (1.0 + gfv * (1.0 - sg)))).astype(_MXU)
            dff_ref[:, pl.ds(FF + ch * half, half)] = (dact * silu).astype(_MXU)

    tspec = lambda w: pl.BlockSpec((tm, w), lambda i: (i, 0))
    return pl.pallas_call(
        body, name="ffn_bwd_hidden", grid=(s // tm,),
        out_shape=(jax.ShapeDtypeStruct((s, 2 * FF), _MXU), jax.ShapeDtypeStruct((s, FF), _MXU)),
        in_specs=[tspec(D), tspec(FF), tspec(FF), pl.BlockSpec(memory_space=pl.ANY)],
        out_specs=(tspec(2 * FF), tspec(FF)),
        scratch_shapes=[pltpu.VMEM((FF, D), _MXU), pltpu.SemaphoreType.DMA((NDEV,))],
        compiler_params=_cparams(1),
    )(dh2, gf, uf, gw)


def _proj_norm_bwd(dy, x, dres, g, gw, slab, width, name):
    s = x.shape[0]
    tm = min(TM, s)

    def body(dy_ref, x_ref, dr_ref, g_ref, gw_ref, dx_ref, dg_ref, wT, sems):
        @pl.when(pl.program_id(0) == 0)
        def _():
            _load_weights(gw_ref, [(slab, wT, D)], sems)
            dg_ref[...] = jnp.zeros_like(dg_ref)

        dv = _nn(dy_ref[...], wT[...])
        xv = x_ref[...]
        inv = lax.rsqrt(jnp.mean(xv * xv, axis=-1, keepdims=True) + EPS)
        dx, dg = _rms_bwd(dv, xv * inv, inv, g_ref[...])
        dg_ref[...] += dg
        dx_ref[...] = dr_ref[...] + dx

    tspec = lambda w: pl.BlockSpec((tm, w), lambda i: (i, 0))
    vec = pl.BlockSpec((1, D), lambda i: (0, 0))
    return pl.pallas_call(
        body, name=name, grid=(s // tm,),
        out_shape=(jax.ShapeDtypeStruct((s, D), F32), jax.ShapeDtypeStruct((1, D), F32)),
        in_specs=[tspec(width), tspec(D), tspec(D), vec, pl.BlockSpec(memory_space=pl.ANY)],
        out_specs=(tspec(D), vec),
        scratch_shapes=[pltpu.VMEM((width, D), _MXU), pltpu.SemaphoreType.DMA((NDEV,))],
        compiler_params=_cparams(1),
    )(dy, x, dres, g, gw)


def _mixer_bwd(dh1, z, h, y_pool, y_lru, gw, small):
    s = dh1.shape[0]
    tm = min(TM_SEQ, s)
    nt = s // tm
    (pool_w, pool_scale, conv_w, conv_b, w_rg, b_rg, w_ig, b_ig, lam, b_gate) = small

    def body(dh1_ref, z_ref, zp_ref, h_ref, hp_ref, yp_ref, yr_ref, gw_ref,
             pw_ref, ps_ref, cw_ref, cb_ref, wrg_ref, brg_ref, wig_ref, big_ref, lam_ref, bg_ref,
             dz_ref, dyr_ref, dyp_ref, mx_ref,
             gbg_ref, glam_ref, gbrg_ref, gbig_ref, gcb_ref, gcw_ref, gps_ref, gpw_ref, gwrg_ref, gwig_ref,
             pprojT, lru_w, wout_w, pbuf, lbuf, hbuf, qbuf, xbuf, a_s, g_s, dh_s, hk, pk, r_s, ig_s, xc_s, mu_s, f_s,
             dcar, sems):
        step = pl.program_id(0)
        i = nt - 1 - step
        t0 = i * tm

        @pl.when(step == 0)
        def _():
            _load_weights(gw_ref, [("pproj", pprojT, PW), ("lru", lru_w, D), ("wout", wout_w, D)], sems)
            for ref in (gbg_ref, glam_ref, gbrg_ref, gbig_ref, gcb_ref, gcw_ref, gps_ref, gpw_ref, gwrg_ref, gwig_ref):
                ref[...] = jnp.zeros_like(ref)
            qbuf[pl.ds(tm, HALO), :] = jnp.zeros((HALO, PW), F32)
            xbuf[pl.ds(tm, 8), :] = jnp.zeros((8, D), F32)
            dcar[...] = jnp.zeros_like(dcar)

        first = i == 0
        zprev = jnp.where(first, 0.0, zp_ref[...])
        hprev = jnp.where(first, 0.0, hp_ref[...])

        d_merged = _nt(dh1_ref[...].astype(_MXU), wout_w[...])

        g0 = _sigmoid(z_ref[:, pl.ds(PW + 2 * D, D)] + bg_ref[pl.ds(0, 1), :])
        g1 = _sigmoid(z_ref[:, pl.ds(PW + 3 * D, D)] + bg_ref[pl.ds(1, 1), :])
        dz0 = d_merged * yp_ref[...].astype(F32) * g0 * (1.0 - g0)
        dz1 = d_merged * yr_ref[...].astype(F32) * g1 * (1.0 - g1)
        dz_ref[:, pl.ds(PW + 2 * D, D)] = dz0.astype(_MXU)
        dz_ref[:, pl.ds(PW + 3 * D, D)] = dz1.astype(_MXU)
        gbg_ref[pl.ds(0, 1), :] += jnp.sum(dz0, axis=0, keepdims=True)
        gbg_ref[pl.ds(1, 1), :] += jnp.sum(dz1, axis=0, keepdims=True)
        d_ypool = (d_merged * g0).astype(_MXU)
        d_ylru = (d_merged * g1).astype(_MXU)
        dyp_ref[...] = d_ypool
        dyr_ref[...] = d_ylru

        d_yl = _nt(d_ylru, lru_w[...])
        gel, dgel = _gelu_and_grad(z_ref[:, pl.ds(PW + D, D)])
        dz_ref[:, pl.ds(PW + D, D)] = (d_yl * h_ref[...] * dgel).astype(_MXU)
        g_full = d_yl * gel
        lbuf[pl.ds(0, HALO), :] = zprev[:, PW:PW + D]
        lbuf[pl.ds(HALO, tm), :] = z_ref[:, pl.ds(PW, D)]
        hbuf[pl.ds(0, 8), :] = hprev
        hbuf[pl.ds(8, tm), :] = h_ref[...]
        sp, sneg = _softplus_neg(lam_ref[...])
        start = (t0 + lax.broadcasted_iota(jnp.int32, (tm, HD), 0)) == 0
        for hh in range(HEADS):
            cs = pl.ds(hh * HD, HD)
            xc, r, ig, a, live, inv_mult, mult = _lru_gates_head(hh, lbuf, start, tm, cw_ref, cb_ref, wrg_ref,
                                                                 brg_ref, wig_ref, big_ref, sp)
            _to_segments(a_s, hh, a, tm)
            _to_segments(g_s, hh, g_full[:, hh * HD:(hh + 1) * HD], tm)
            r_s[:, cs] = r
            ig_s[:, cs] = ig
            xc_s[:, cs] = xc
            mu_s[:, cs] = mult
            f_s[:, cs] = jnp.where(live, -(a * a) * inv_mult, 0.0)
        _segment_scan(a_s, g_s, dh_s, hk, pk, dcar, tm, reverse=True)
        for hh in range(HEADS):
            cs = pl.ds(hh * HD, HD)
            a = _from_segments(a_s, hh, tm)
            r = r_s[:, cs]
            ig = ig_s[:, cs]
            xc = xc_s[:, cs]
            mult = mu_s[:, cs]
            dh = _from_segments(dh_s, hh, tm)
            d_mult = dh * ig * xc
            d_loga = dh * hbuf[pl.ds(7, tm), cs] * a + d_mult * f_s[:, cs]
            glam_ref[:, cs] += jnp.sum(d_loga * (LRU_C * r) * sneg[:, hh * HD:(hh + 1) * HD], axis=0, keepdims=True)
            d_rpre = d_loga * (-LRU_C * sp[:, hh * HD:(hh + 1) * HD]) * r * (1.0 - r)
            d_igpre = dh * mult * xc * ig * (1.0 - ig)
            gbrg_ref[pl.ds(hh, 1), :] += jnp.sum(d_rpre, axis=0, keepdims=True)
            gbig_ref[pl.ds(hh, 1), :] += jnp.sum(d_igpre, axis=0, keepdims=True)
            drm = d_rpre.astype(_MXU)
            dim = d_igpre.astype(_MXU)
            xcm = xc.astype(_MXU)
            gwrg_ref[hh] += _tn(xcm, drm)
            gwig_ref[hh] += _tn(xcm, dim)
            d_xc = dh * mult * ig + _nt(drm, wrg_ref[hh]) + _nt(dim, wig_ref[hh])
            gcb_ref[:, cs] += jnp.sum(d_xc, axis=0, keepdims=True)
            for k in range(CONV):
                gcw_ref[pl.ds(k, 1), cs] += jnp.sum(d_xc * lbuf[pl.ds(HALO - (CONV - 1) + k, tm), cs], axis=0,
                                                    keepdims=True)
            xbuf[pl.ds(0, tm), cs] = d_xc
        dzl = cw_ref[pl.ds(CONV - 1, 1), :] * xbuf[pl.ds(0, tm), :]
        for k in range(CONV - 1):
            dzl = dzl + cw_ref[pl.ds(k, 1), :] * xbuf[pl.ds(CONV - 1 - k, tm), :]
        dz_ref[:, pl.ds(PW, D)] = dzl.astype(_MXU)
        xbuf[pl.ds(tm, 8), :] = xbuf[pl.ds(0, 8), :]

        d_mixed = _nn(d_ypool, pprojT[...])
        pbuf[pl.ds(0, HALO), :] = zprev[:, 0:PW]
        pbuf[pl.ds(HALO, tm), :] = z_ref[:, pl.ds(0, PW)]
        pooled, mixed_pre = _pool_tile(pbuf, t0, tm, pw_ref, ps_ref)
        mp = jnp.concatenate(mixed_pre, axis=1)
        mx_ref[...] = (mp * ps_ref[...]).astype(_MXU)
        gps_ref[...] += jnp.sum(d_mixed * mp, axis=0, keepdims=True)
        d_mp = (d_mixed * ps_ref[...]).astype(_MXU)
        t = t0 + lax.broadcasted_iota(jnp.int32, (tm, GD), 0)
        d_pooled = []
        for g, w in enumerate(WINDOWS):
            dmg = d_mp[:, g * GD:(g + 1) * GD]
            gpw_ref[g] += _tn(pooled[g].astype(_MXU), dmg)
            dp = _nt(dmg, pw_ref[g])
            d_pooled.append(dp)
            qbuf[pl.ds(0, tm), pl.ds(g * GD, GD)] = dp / jnp.minimum(t + 1, w).astype(F32)
        for g, w in enumerate(WINDOWS):
            cs = pl.ds(g * GD, GD)
            acc = qbuf[pl.ds(0, tm), cs]
            for d in range(1, w):
                acc = acc + qbuf[pl.ds(d, tm), cs]
            dz_ref[:, cs] = (acc - d_pooled[g]).astype(_MXU)
        qbuf[pl.ds(tm, HALO), :] = qbuf[pl.ds(0, HALO), :]

    rev = lambda w: pl.BlockSpec((tm, w), lambda g: (nt - 1 - g, 0))
    prev = lambda rows, w: pl.BlockSpec((rows, w), lambda g: (jnp.maximum((nt - 1 - g) * (tm // rows) - 1, 0), 0))
    full = lambda a: pl.BlockSpec(a.shape, lambda g: (0,) * a.ndim)
    tok = lambda w, dt: jax.ShapeDtypeStruct((s, w), dt)
    acc_shapes = [(2, D), (1, D), (HEADS, HD), (HEADS, HD), (1, D), (CONV, D), (1, PW), (GROUPS, GD, GD),
                  (HEADS, HD, HD), (HEADS, HD, HD)]
    acc_specs = tuple(pl.BlockSpec(sh, lambda g, n=len(sh): (0,) * n) for sh in acc_shapes)
    seg_buf = pltpu.VMEM((HEADS, 8 * _seg_layout(tm)[1], HD), F32)
    tile_buf = pltpu.VMEM((tm, D), F32)
    return pl.pallas_call(
        body, name="mixer_bwd", grid=(nt,),
        out_shape=(tok(NIN, _MXU), tok(D, _MXU), tok(D, _MXU), tok(PW, _MXU))
        + tuple(jax.ShapeDtypeStruct(sh, F32) for sh in acc_shapes),
        in_specs=[rev(D), rev(NIN), prev(HALO, NIN), rev(D), prev(8, D), rev(D), rev(D),
                  pl.BlockSpec(memory_space=pl.ANY)] + [full(a) for a in small],
        out_specs=(rev(NIN), rev(D), rev(D), rev(PW)) + acc_specs,
        scratch_shapes=[pltpu.VMEM((D, PW), _MXU), pltpu.VMEM((D, D), _MXU), pltpu.VMEM((D, D), _MXU),
                        pltpu.VMEM((tm + HALO, PW), F32), pltpu.VMEM((tm + HALO, D), F32),
                        pltpu.VMEM((tm + 8, D), F32), pltpu.VMEM((tm + HALO, PW), F32), pltpu.VMEM((tm + 8, D), F32),
                        seg_buf, seg_buf, seg_buf, pltpu.VMEM((HEADS, tm, HD), F32), pltpu.VMEM((HEADS, tm, HD), F32),
                        tile_buf, tile_buf, tile_buf, tile_buf, tile_buf,
                        pltpu.VMEM((8, D), F32), pltpu.SemaphoreType.DMA((3 * NDEV,))],
        compiler_params=_cparams(1),
    )(dh1, z, z, h, h, y_pool, y_lru, gw, *small)


def _split3(a):
    hi = a.astype(jnp.bfloat16).astype(F32)
    mid = (a - hi).astype(jnp.bfloat16).astype(F32)
    lo = (a - hi - mid).astype(jnp.bfloat16).astype(F32)
    return jnp.stack([hi, mid, lo])


def _small_pack(parts):
    flat = jnp.concatenate([a.reshape(-1) for a in parts])
    return jnp.pad(flat, (0, NDEV * SMALL_ROWS * D - flat.shape[0])).reshape(NDEV * SMALL_ROWS, D)


def _small_unpack(packed, shapes):
    flat = packed.reshape(-1)
    out, o = [], 0
    for sh in shapes:
        n = math.prod(sh)
        out.append(flat[o:o + n].reshape(sh))
        o += n
    return out


def kernel(x, p, norm1_g, w_in, b_gate, pool_w, pool_scale, pool_proj, conv_w, conv_b, w_rg, b_rg, w_ig, b_ig, lru_lambda, lru_proj, w_out, norm2_g, w_ffn_in, w_ffn_out, ple_norm_g, w_ple_gate, w_ple_proj, final_g, loss_target, m_norm1_g, m_w_in, m_b_gate, m_pool_w, m_pool_scale, m_pool_proj, m_conv_w, m_conv_b, m_w_rg, m_b_rg, m_w_ig, m_b_ig, m_lru_lambda, m_lru_proj, m_w_out, m_norm2_g, m_w_ffn_in, m_w_ffn_out, m_ple_norm_g, m_w_ple_gate, m_w_ple_proj, m_final_g, v_norm1_g, v_w_in, v_b_gate, v_pool_w, v_pool_scale, v_pool_proj, v_conv_w, v_conv_b, v_w_rg, v_b_rg, v_w_ig, v_b_ig, v_lru_lambda, v_lru_proj, v_w_out, v_norm2_g, v_w_ffn_in, v_w_ffn_out, v_ple_norm_g, v_w_ple_gate, v_w_ple_proj, v_final_g):
    axes = ("x", "y", "c")
    me = 4 * lax.axis_index("x") + 2 * lax.axis_index("y") + lax.axis_index("c")
    x2 = x[0]
    p2 = p[0, 0]
    tgt = loss_target[0]

    n_small = (CONV + 2) * 128
    small_terms = _split3(jnp.concatenate([conv_w[0].reshape(-1), b_gate[0].reshape(-1)]))
    small_rows = jnp.pad(small_terms, ((0, 16 - 3), (0, D - n_small)))
    own = jnp.concatenate([
        w_in[0].T.astype(_MXU), w_ffn_in[0].T.astype(_MXU),
        jnp.pad(pool_proj[0].T, ((0, 0), (0, D - PW))).astype(_MXU),
        jnp.pad(w_ple_proj[0].T, ((0, 0), (0, D - PLE))).astype(_MXU),
        lru_proj[0].astype(_MXU), w_out[0].astype(_MXU), w_ffn_out[0].astype(_MXU), w_ple_gate[0].astype(_MXU),
        small_rows.astype(_MXU),
    ], axis=0)
    gw = _all_gather_weights(own)
    off = W_OFF["f32s"][0]
    st = gw[:, off:off + 3, :n_small].astype(F32)
    sf = st[:, 0] + st[:, 1] + st[:, 2]
    conv_w_full = sf[:, :CONV * 128].reshape(NDEV, CONV, 128).transpose(1, 0, 2).reshape(CONV, D)
    b_gate_full = sf[:, CONV * 128:].reshape(NDEV, 2, 128).transpose(1, 0, 2).reshape(2, D)

    small = (pool_w[0].astype(_MXU), pool_scale, conv_w_full, conv_b, w_rg[0].astype(_MXU), b_rg[0],
             w_ig[0].astype(_MXU), b_ig[0], lru_lambda, b_gate_full)

    u, z = _inproj_fwd(x2, norm1_g, gw)
    h, yl, merged, y_pool, y_lru, h1 = _mixer_fwd(z, x2, gw, small)
    v, gf, uf, h2 = _ffn_fwd(h1, norm2_g, gw)

    dh2, n3, dpg, de, loss_blk, g_ple_norm, g_final = _ple_loss_fwd_bwd(h2, p2, tgt, ple_norm_g, final_g.reshape(1, D), gw)
    loss = lax.psum(loss_blk[0, 0], axes)
    dff, act = _ffn_bwd_hidden(dh2, gf, uf, gw)
    dh1, g_norm2 = _proj_norm_bwd(dff, h1, dh2, norm2_g, gw, "wffn", 2 * FF, "ffn_bwd_in")
    (dz, d_ylru, d_ypool, mixed, g_bgate, g_lam, g_brg, g_big, g_convb, g_convw, g_pscale, g_poolw, g_wrg,
     g_wig) = _mixer_bwd(dh1, z, h, y_pool, y_lru, gw, small)
    grad_x, g_norm1 = _proj_norm_bwd(dz, x2, dh1, norm1_g, gw, "win", NIN, "inproj_bwd")

    small_shapes = [(1, D), (GROUPS, GD, GD), (1, PW), (1, D), (HEADS, HD, HD), (HEADS, HD), (HEADS, HD, HD),
                    (HEADS, HD), (1, D), (1, D), (1, D), (1, D), (2, D), (CONV, D)]
    small_part = _small_pack([g_norm1, g_poolw, g_pscale, g_convb, g_wrg, g_brg, g_wig, g_big, g_lam, g_norm2,
                              g_ple_norm, g_final, g_bgate, g_convw])
    rs = {
        "wffn": _grad_matmul_rs(dff, v, "grad_w_ffn_in", 704, narrow=_MXU),
        "wffo": _grad_matmul_rs(act, dh2, "grad_w_ffn_out", 352, chips_per_block=2, narrow=_MXU),
        "wpg": _grad_matmul_rs(n3, dpg, "grad_w_ple_gate", 128, narrow=_MXU),
        "lru": _grad_matmul_rs(yl, d_ylru, "grad_lru_proj", 128, narrow=_MXU),
        "wout": _grad_matmul_rs(merged, dh1, "grad_w_out", 128, narrow=_MXU),
        "pproj": _grad_matmul_rs(d_ypool, mixed, "grad_pool_proj", 128, narrow=_MXU),
        "win": _grad_matmul_rs(dz, u, "grad_w_in", 576, narrow=_MXU),
        "ple": _grad_matmul_rs(de, p2, "grad_w_ple_proj", 128, extra=small_part),
    }

    def slab(name):
        t_own, landed, _ = rs[name]
        return _sum_arrays([t_own, landed[0], landed[1], landed[2]], "rs_sum_" + name)

    g_w_in = slab("win").T
    g_w_ffn_in = slab("wffn").T
    g_pool_proj = slab("pproj")[:, :PW].T
    ple_and_small = slab("ple")
    g_w_ple_proj = ple_and_small[:128, :PLE].T
    g_lru_proj, g_w_out, g_w_ffn_out, g_w_ple_gate = slab("lru"), slab("wout"), slab("wffo"), slab("wpg")
    small_red = _all_gather_small(ple_and_small[128:])
    (gs_norm1, gs_poolw, gs_pscale, gs_convb, gs_wrg, gs_brg, gs_wig, gs_big, gs_lam, gs_norm2, gs_ple_norm,
     gs_final, gs_bgate, gs_convw) = _small_unpack(small_red, small_shapes)
    g_b_gate = lax.dynamic_slice_in_dim(gs_bgate, me * 128, 128, axis=1)
    g_conv_w = lax.dynamic_slice_in_dim(gs_convw, me * 128, 128, axis=1)

    grads = {
        "norm1_g": gs_norm1, "w_in": g_w_in[None], "b_gate": g_b_gate[None], "pool_w": gs_poolw[None],
        "pool_scale": gs_pscale, "pool_proj": g_pool_proj[None], "conv_w": g_conv_w[None], "conv_b": gs_convb,
        "w_rg": gs_wrg[None], "b_rg": gs_brg[None], "w_ig": gs_wig[None], "b_ig": gs_big[None], "lru_lambda": gs_lam,
        "lru_proj": g_lru_proj[None], "w_out": g_w_out[None], "norm2_g": gs_norm2, "w_ffn_in": g_w_ffn_in[None],
        "w_ffn_out": g_w_ffn_out[None], "ple_norm_g": gs_ple_norm, "w_ple_gate": g_w_ple_gate[None],
        "w_ple_proj": g_w_ple_proj[None], "final_g": gs_final.reshape(D),
    }
    weights = dict(norm1_g=norm1_g, w_in=w_in, b_gate=b_gate, pool_w=pool_w, pool_scale=pool_scale, pool_proj=pool_proj,
                   conv_w=conv_w, conv_b=conv_b, w_rg=w_rg, b_rg=b_rg, w_ig=w_ig, b_ig=b_ig, lru_lambda=lru_lambda,
                   lru_proj=lru_proj, w_out=w_out, norm2_g=norm2_g, w_ffn_in=w_ffn_in, w_ffn_out=w_ffn_out,
                   ple_norm_g=ple_norm_g, w_ple_gate=w_ple_gate, w_ple_proj=w_ple_proj, final_g=final_g)
    moments_m = dict(norm1_g=m_norm1_g, w_in=m_w_in, b_gate=m_b_gate, pool_w=m_pool_w, pool_scale=m_pool_scale,
                     pool_proj=m_pool_proj, conv_w=m_conv_w, conv_b=m_conv_b, w_rg=m_w_rg, b_rg=m_b_rg, w_ig=m_w_ig,
                     b_ig=m_b_ig, lru_lambda=m_lru_lambda, lru_proj=m_lru_proj, w_out=m_w_out, norm2_g=m_norm2_g,
                     w_ffn_in=m_w_ffn_in, w_ffn_out=m_w_ffn_out, ple_norm_g=m_ple_norm_g, w_ple_gate=m_w_ple_gate,
                     w_ple_proj=m_w_ple_proj, final_g=m_final_g)
    moments_v = dict(norm1_g=v_norm1_g, w_in=v_w_in, b_gate=v_b_gate, pool_w=v_pool_w, pool_scale=v_pool_scale,
                     pool_proj=v_pool_proj, conv_w=v_conv_w, conv_b=v_conv_b, w_rg=v_w_rg, b_rg=v_b_rg, w_ig=v_w_ig,
                     b_ig=v_b_ig, lru_lambda=v_lru_lambda, lru_proj=v_lru_proj, w_out=v_w_out, norm2_g=v_norm2_g,
                     w_ffn_in=v_w_ffn_in, w_ffn_out=v_w_ffn_out, ple_norm_g=v_ple_norm_g, w_ple_gate=v_w_ple_gate,
                     w_ple_proj=v_w_ple_proj, final_g=v_final_g)
    names = list(weights)
    big = ("w_in", "w_ffn_in", "w_ffn_out", "lru_proj", "w_out", "w_ple_gate", "pool_proj", "w_ple_proj")
    delta, new_m, new_v = {}, {}, {}
    for n in big:
        sh = weights[n].shape
        as2d = lambda a: a.reshape(sh[-2], sh[-1])
        d_, m_, v_ = _adamw(as2d(weights[n]), as2d(grads[n]), as2d(moments_m[n]), as2d(moments_v[n]), "adamw_" + n)
        delta[n], new_m[n], new_v[n] = d_.reshape(sh), m_.reshape(sh), v_.reshape(sh)
    rest = [n for n in names if n not in big]
    rest_shapes = [weights[n].shape for n in rest]
    packed = [_small_pack([src[n] for n in rest]) for src in (weights, grads, moments_m, moments_v)]
    d_, m_, v_ = _adamw(*packed, "adamw_small")
    for n, a, b_, c_ in zip(rest, _small_unpack(d_, rest_shapes), _small_unpack(m_, rest_shapes),
                            _small_unpack(v_, rest_shapes)):
        delta[n], new_m[n], new_v[n] = a, b_, c_

    return (loss, grad_x[None], *[grads[n] for n in names], *[delta[n] for n in names],
            *[new_m[n] for n in names], *[new_v[n] for n in names])
```

```python
import functools
import math

import jax
import jax.numpy as jnp
from jax import lax
from jax.experimental import pallas as pl
from jax.experimental.pallas import tpu as pltpu

F32 = jnp.float32
D = 1024
NIN = 4608
PW = 512
FF = 2816
PLE = 256
HEADS, HD = 8, 128
GROUPS, GD = 4, 128
WINDOWS = (2, 4, 8, 16)
HALO = 16
CONV = 4
EPS = 1e-6
LRU_C = 8.0
NDEV = 8
MESH = pl.DeviceIdType.MESH

ADAM_LR, ADAM_B1, ADAM_B2, ADAM_EPS, ADAM_WD, ADAM_STEP = 0.001, 0.9, 0.999, 1e-08, 0.01, 10

_MXU = jnp.bfloat16
TM = 512
TM_SEQ = 256
VMEM_LIMIT = 56 * 1024 * 1024

W_SLABS = (("win", 576), ("wffn", 704), ("pproj", 128), ("ple", 128), ("lru", 128), ("wout", 128),
           ("wffo", 352), ("wpg", 128), ("f32s", 16))
W_OFF = {}
_o = 0
for _n, _r in W_SLABS:
    W_OFF[_n] = (_o, _r)
    _o += _r
RW = _o
SMALL_ROWS = 48


def _cparams(n_axes=1, vmem=VMEM_LIMIT):
    return pltpu.CompilerParams(dimension_semantics=("arbitrary",) * n_axes, vmem_limit_bytes=vmem)


def _my_pos():
    return lax.axis_index("x"), lax.axis_index("y"), lax.axis_index("c")


def _nt(a, b):
    return lax.dot_general(a, b, (((1,), (1,)), ((), ())), preferred_element_type=F32)


def _nn(a, b):
    return lax.dot_general(a, b, (((1,), (0,)), ((), ())), preferred_element_type=F32)


def _tn(a, b):
    return lax.dot_general(a, b, (((0,), (0,)), ((), ())), preferred_element_type=F32)


def _sigmoid(x):
    return 0.5 * jnp.tanh(0.5 * x) + 0.5


_GELU_K = math.sqrt(2.0 / math.pi)


def _gelu_and_grad(x):
    x2 = x * x
    inner = _GELU_K * (x + 0.044715 * x2 * x)
    t = jnp.tanh(inner)
    g = 0.5 * x * (1.0 + t)
    dg = 0.5 * (1.0 + t) + 0.5 * x * (1.0 - t * t) * _GELU_K * (1.0 + 3.0 * 0.044715 * x2)
    return g, dg


def _softplus_neg(lam):
    x = -lam
    t = jnp.exp(-jnp.abs(x))
    u = 1.0 + t
    l1p = jnp.where(u == 1.0, t, jnp.log(u) * t / (u - 1.0))
    return jnp.maximum(x, 0.0) + l1p, _sigmoid(x)


def _start_slab_loads(g_ref, name, dst_ref, sems, base, width=D):
    off, rows = W_OFF[name]
    copies = []
    for k in range(NDEV):
        if width == D:
            src = g_ref.at[k, pl.ds(off, rows), :]
        else:
            src = g_ref.at[k, pl.ds(off, rows), pl.ds(0, width)]
        cp = pltpu.make_async_copy(src, dst_ref.at[pl.ds(k * rows, rows), :], sems.at[base + k])
        cp.start()
        copies.append(cp)
    return copies


def _load_weights(g_ref, items, sems):
    copies = []
    for n, (name, dst, width) in enumerate(items):
        copies += _start_slab_loads(g_ref, name, dst, sems, n * NDEV, width)
    for cp in copies:
        cp.wait()


def _all_gather_weights(own):
    rows, cols = own.shape

    def body(own_ref, out_ref, stage, send_sems, recv_sems, local_sem):
        x, y, c = _my_pos()
        me, sibling = (x, y, c), (x, y, 1 - c)
        chips = [(1 - x, y), (x, 1 - y), (1 - x, 1 - y)]

        def slab(px, py, pc):
            return out_ref.at[4 * px + 2 * py + pc]

        def copy(k, block, to, src=None):
            return pltpu.make_async_remote_copy(
                src_ref=slab(*block) if src is None else src, dst_ref=slab(*block),
                send_sem=send_sems.at[k], recv_sem=recv_sems.at[k], device_id=to, device_id_type=MESH)

        pltpu.sync_copy(own_ref, stage)
        mine = pltpu.make_async_copy(stage, slab(*me), local_sem)
        mine.start()
        first = [copy(0, me, sibling, src=stage)]
        first += [copy(1 + j, me, (*chip, c), src=stage) for j, chip in enumerate(chips)]
        for cp in first:
            cp.start()
        passed = [copy(4 + j, (*chip, c), sibling) for j, chip in enumerate(chips)]
        for j, chip in enumerate(chips):
            copy(1 + j, (*chip, c), me).wait_recv()
            passed[j].start()
        copy(0, sibling, me).wait_recv()
        for j, chip in enumerate(chips):
            copy(4 + j, (*chip, 1 - c), me).wait_recv()
        for cp in first + passed:
            cp.wait_send()
        mine.wait()

    return pl.pallas_call(
        body, name="ag_weights",
        out_shape=jax.ShapeDtypeStruct((NDEV, rows, cols), own.dtype),
        in_specs=[pl.BlockSpec(memory_space=pl.ANY)],
        out_specs=pl.BlockSpec(memory_space=pl.ANY),
        scratch_shapes=[pltpu.VMEM((rows, cols), own.dtype), pltpu.SemaphoreType.DMA((7,)),
                        pltpu.SemaphoreType.DMA((7,)), pltpu.SemaphoreType.DMA],
        compiler_params=pltpu.CompilerParams(vmem_limit_bytes=VMEM_LIMIT),
    )(own)


def _all_gather_small(piece):
    rows = piece.shape[0]

    def body(p_ref, out_ref, send_sems, recv_sems, local_sem):
        x, y, c = _my_pos()
        me = 4 * x + 2 * y + c
        mine = pltpu.make_async_copy(p_ref, out_ref.at[pl.ds(pl.multiple_of(me * rows, 8), rows), :], local_sem)
        mine.start()
        sends = []
        peers = []
        for r in range(1, NDEV):
            px = 1 - x if (r >> 2) & 1 else x
            py = 1 - y if (r >> 1) & 1 else y
            pc = 1 - c if r & 1 else c
            peers.append((px, py, pc))
            cp = pltpu.make_async_remote_copy(
                src_ref=p_ref, dst_ref=out_ref.at[pl.ds(pl.multiple_of(me * rows, 8), rows), :],
                send_sem=send_sems.at[r - 1], recv_sem=recv_sems.at[r - 1], device_id=(px, py, pc),
                device_id_type=MESH)
            cp.start()
            sends.append(cp)
        for r, (px, py, pc) in enumerate(peers):
            them = 4 * px + 2 * py + pc
            pltpu.make_async_remote_copy(
                src_ref=p_ref, dst_ref=out_ref.at[pl.ds(pl.multiple_of(them * rows, 8), rows), :],
                send_sem=send_sems.at[r], recv_sem=recv_sems.at[r], device_id=(px, py, pc),
                device_id_type=MESH).wait_recv()
        for cp in sends:
            cp.wait_send()
        mine.wait()

    return pl.pallas_call(
        body, name="ag_small",
        out_shape=jax.ShapeDtypeStruct((NDEV * rows, piece.shape[1]), piece.dtype),
        in_specs=[pl.BlockSpec(memory_space=pltpu.VMEM)],
        out_specs=pl.BlockSpec(memory_space=pl.ANY),
        scratch_shapes=[pltpu.SemaphoreType.DMA((7,)), pltpu.SemaphoreType.DMA((7,)), pltpu.SemaphoreType.DMA],
    )(piece)


def _row_block(rows, target=512, mult=8):
    b = min(rows, target) // mult * mult
    while rows % b:
        b -= mult
    return b


def _sum_arrays(arrs, name, narrow=None, target=464):
    rows, cols = arrs[0].shape
    br = _row_block(rows, target, 16)
    n = len(arrs)

    def body(*refs):
        acc = refs[0][...].astype(F32)
        for r in refs[1:n]:
            acc = acc + r[...].astype(F32)
        refs[n][...] = acc
        if narrow is not None:
            refs[n + 1][...] = acc.astype(narrow)

    spec = pl.BlockSpec((br, cols), lambda i: (i, 0))
    shape = jax.ShapeDtypeStruct((rows, cols), F32)
    if narrow is None:
        out_shape, out_specs = shape, spec
    else:
        out_shape, out_specs = (shape, jax.ShapeDtypeStruct((rows, cols), narrow)), (spec, spec)
    return pl.pallas_call(
        body, name=name, grid=(rows // br,), out_shape=out_shape,
        in_specs=[spec] * n, out_specs=out_specs, compiler_params=_cparams(1),
    )(*arrs)


def _adamw(w, g, m, v, name):
    rows, cols = w.shape
    br = _row_block(rows, 256)

    def body(w_ref, g_ref, m_ref, v_ref, d_ref, nm_ref, nv_ref):
        g_ = g_ref[...]
        m_ = ADAM_B1 * m_ref[...] + (1.0 - ADAM_B1) * g_
        v_ = ADAM_B2 * v_ref[...] + (1.0 - ADAM_B2) * (g_ * g_)
        m_hat = m_ / (1.0 - ADAM_B1 ** ADAM_STEP)
        v_hat = v_ / (1.0 - ADAM_B2 ** ADAM_STEP)
        d_ref[...] = -ADAM_LR * (m_hat / (jnp.sqrt(v_hat) + ADAM_EPS) + ADAM_WD * w_ref[...])
        nm_ref[...] = m_
        nv_ref[...] = v_

    spec = pl.BlockSpec((br, cols), lambda i: (i, 0))
    shape = jax.ShapeDtypeStruct((rows, cols), F32)
    return pl.pallas_call(
        body, name=name, grid=(rows // br,), out_shape=(shape, shape, shape),
        in_specs=[spec] * 4, out_specs=(spec, spec, spec), compiler_params=_cparams(1),
    )(w, g, m, v)


_CHIP_FLIPS = (2, 3, 1, 0)


def _grad_matmul(lhs, rhs, name):
    s, r = lhs.shape
    k = rhs.shape[1]
    tm = min(TM, s)

    def body(l_ref, r_ref, o_ref):
        @pl.when(pl.program_id(0) == 0)
        def _():
            o_ref[...] = jnp.zeros_like(o_ref)

        o_ref[:, pl.ds(0, k)] += _tn(l_ref[...].astype(_MXU), r_ref[...].astype(_MXU))

    return pl.pallas_call(
        body, name=name, grid=(s // tm,),
        out_shape=jax.ShapeDtypeStruct((r, D), F32),
        in_specs=[pl.BlockSpec((tm, r), lambda i: (i, 0)), pl.BlockSpec((tm, k), lambda i: (i, 0))],
        out_specs=pl.BlockSpec((r, D), lambda i: (0, 0)),
        compiler_params=_cparams(1),
    )(lhs, rhs)


def _grad_matmul_rs(lhs, rhs, name, rows, chips_per_block=1, extras=(), narrow=None):
    s, r8 = lhs.shape
    k = rhs.shape[1]
    tm = min(TM, s)
    nt = s // tm
    cpb = chips_per_block
    nblk = 4 // cpb
    nx = len(extras)
    ers = [e.shape[0] // NDEV for e in extras]
    er = sum(ers)
    srows = rows + er
    brows = 2 * cpb * srows
    wire = F32 if narrow is None else narrow
    mid = nt // 2

    def flip_of(p):
        return jnp.where(p == 0, 2, jnp.where(p == 1, 3, jnp.where(p == 2, 1, 0)))

    def block_col(b):
        x, y, _ = _my_pos()
        mine = 2 * x + y
        if cpb == 1:
            return mine ^ flip_of(b)
        return jnp.where(b == 0, (mine >> 1) ^ 1, mine >> 1)

    def body(*refs):
        l_ref, r_ref = refs[:2]
        x_refs = refs[2:2 + nx]
        town_ref, lici_ref, ld2d_ref, acc, stage, send_buf, dsend, drecv, isend, irecv, xsem = refs[2 + nx:]
        b = pl.program_id(0)
        i = pl.program_id(1)
        x, y, c = _my_pos()
        mine = 2 * x + y
        sibling = (x, y, 1 - c)

        def chip_at(p):
            return mine ^ _CHIP_FLIPS[p]

        def slab_rows(p, parity):
            within = 0 if cpb == 1 else (chip_at(p) & 1) * 2
            return pl.ds(pl.multiple_of((within + parity) * srows, 8), srows)

        def push(p, slot):
            return pltpu.make_async_remote_copy(
                src_ref=acc.at[slot, slab_rows(p, 1 - c), :], dst_ref=ld2d_ref.at[p],
                send_sem=dsend.at[p], recv_sem=drecv.at[p], device_id=sibling, device_id_type=MESH)

        def ici(p):
            ch = chip_at(p)
            return pltpu.make_async_remote_copy(
                src_ref=send_buf.at[p % 2], dst_ref=lici_ref.at[p],
                send_sem=isend.at[p], recv_sem=irecv.at[p], device_id=(ch >> 1, ch & 1, c), device_id_type=MESH)

        def extra_loads(p, slot):
            copies = []
            within = 0 if cpb == 1 else (chip_at(p) & 1) * 2
            for parity in range(2):
                off = rows
                for n, (x_ref, e) in enumerate(zip(x_refs, ers)):
                    src = x_ref.at[pl.ds(pl.multiple_of((2 * chip_at(p) + parity) * e, 8), e), :]
                    dst = acc.at[slot, pl.ds(pl.multiple_of((within + parity) * srows + off, 8), e), :]
                    copies.append(pltpu.make_async_copy(src, dst, xsem.at[(p * 2 + parity) * nx + n]))
                    off += e
            return copies

        def combine(p, slot):
            push(p, slot).wait_recv()
            pltpu.sync_copy(ld2d_ref.at[p], stage)
            total = acc[slot, slab_rows(p, c), :] + stage[...]
            if p == 3:
                stage[...] = total
                pltpu.sync_copy(stage, town_ref)
            else:
                if p == 2:
                    ici(0).wait_send()
                send_buf[p % 2] = total.astype(wire)
                ici(p).start()

        for bb in range(nblk):
            slot = bb % 2
            positions = list(range(bb * cpb, (bb + 1) * cpb))

            @pl.when(jnp.logical_and(b == bb, i == 0))
            def _(bb=bb, slot=slot, positions=positions):
                if bb >= 2:
                    for p in range((bb - 2) * cpb, (bb - 1) * cpb):
                        push(p, slot).wait_send()
                for q in range(2 * cpb):
                    acc[slot, pl.ds(q * srows, rows), :] = jnp.zeros((rows, D), F32)
                for p in positions:
                    for cp in extra_loads(p, slot):
                        cp.start()

            if bb >= 1:
                @pl.when(jnp.logical_and(b == bb, i == mid))
                def _(bb=bb):
                    for p in range((bb - 1) * cpb, bb * cpb):
                        combine(p, (bb - 1) % 2)

        res = _tn(l_ref[...].astype(_MXU), r_ref[...].astype(_MXU))
        slot_now = b % 2
        for q in range(2 * cpb):
            acc[slot_now, pl.ds(q * srows, rows), pl.ds(0, k)] += res[q * rows:(q + 1) * rows, :]

        for bb in range(nblk):
            slot = bb % 2
            positions = list(range(bb * cpb, (bb + 1) * cpb))

            @pl.when(jnp.logical_and(b == bb, i == nt - 1))
            def _(bb=bb, slot=slot, positions=positions):
                for p in positions:
                    for cp in extra_loads(p, slot):
                        cp.wait()
                for p in positions:
                    push(p, slot).start()
                if bb == nblk - 1:
                    for p in positions:
                        combine(p, slot)
                    for p in range(max(0, (nblk - 2) * cpb), 4):
                        push(p, slot).wait_send()
                    for p in range(1, 3):
                        ici(p).wait_send()
                    for p in range(3):
                        ici(p).wait_recv()

    in_specs = [pl.BlockSpec((tm, 2 * cpb * rows), lambda b, i: (i, block_col(b))),
                pl.BlockSpec((tm, k), lambda b, i: (i, 0))]
    any_spec = pl.BlockSpec(memory_space=pl.ANY)
    in_specs += [any_spec] * nx
    args = [lhs, rhs, *extras]
    return pl.pallas_call(
        body, name=name, grid=(nblk, nt),
        out_shape=(jax.ShapeDtypeStruct((srows, D), F32), jax.ShapeDtypeStruct((3, srows, D), wire),
                   jax.ShapeDtypeStruct((4, srows, D), F32)),
        in_specs=in_specs, out_specs=(any_spec, any_spec, any_spec),
        scratch_shapes=[pltpu.VMEM((2, brows, D), F32), pltpu.VMEM((srows, D), F32), pltpu.VMEM((2, srows, D), wire),
                        pltpu.SemaphoreType.DMA((4,)), pltpu.SemaphoreType.DMA((4,)), pltpu.SemaphoreType.DMA((3,)),
                        pltpu.SemaphoreType.DMA((3,)), pltpu.SemaphoreType.DMA((max(1, 8 * nx),))],
        compiler_params=_cparams(2),
    )(*args)


def _inproj_fwd(x, g1, gw):
    s = x.shape[0]
    tm = min(TM, s)
    nchunk = 4
    cw = NIN // nchunk

    def body(x_ref, g1_ref, gw_ref, u_ref, z_ref, w_vmem, sems):
        @pl.when(pl.program_id(0) == 0)
        def _():
            _load_weights(gw_ref, [("win", w_vmem, D)], sems)

        xv = x_ref[...]
        inv = lax.rsqrt(jnp.mean(xv * xv, axis=-1, keepdims=True) + EPS)
        u = (xv * inv * g1_ref[...]).astype(_MXU)
        u_ref[...] = u
        for ch in range(nchunk):
            z_ref[:, pl.ds(ch * cw, cw)] = _nt(u, w_vmem[pl.ds(ch * cw, cw), :])

    return pl.pallas_call(
        body, name="inproj_fwd", grid=(s // tm,),
        out_shape=(jax.ShapeDtypeStruct((s, D), _MXU), jax.ShapeDtypeStruct((s, NIN), F32)),
        in_specs=[pl.BlockSpec((tm, D), lambda i: (i, 0)), pl.BlockSpec((1, D), lambda i: (0, 0)),
                  pl.BlockSpec(memory_space=pl.ANY)],
        out_specs=(pl.BlockSpec((tm, D), lambda i: (i, 0)), pl.BlockSpec((tm, NIN), lambda i: (i, 0))),
        scratch_shapes=[pltpu.VMEM((NIN, D), _MXU), pltpu.SemaphoreType.DMA((NDEV,))],
        compiler_params=_cparams(1),
    )(x, g1, gw)


def _pool_tile(pbuf, t0, tm, pw_ref, scale_ref):
    t = t0 + lax.broadcasted_iota(jnp.int32, (tm, GD), 0)
    pooled, mixed_pre = [], []
    for g, w in enumerate(WINDOWS):
        cs = pl.ds(g * GD, GD)
        cur = pbuf[pl.ds(HALO, tm), cs]
        acc = cur
        for d in range(1, w):
            acc = acc + pbuf[pl.ds(HALO - d, tm), cs]
        cnt = jnp.minimum(t + 1, w).astype(F32)
        pg = acc / cnt - cur
        pooled.append(pg)
        mixed_pre.append(_nn(pg.astype(_MXU), pw_ref[g]))
    return pooled, mixed_pre


def _lru_gates_head(hh, lbuf, start, tm, cw_ref, cb_ref, wrg_ref, brg_ref, wig_ref, big_ref, sp):
    cs = pl.ds(hh * HD, HD)
    xc = cb_ref[:, cs] + cw_ref[pl.ds(CONV - 1, 1), cs] * lbuf[pl.ds(HALO, tm), cs]
    for k in range(CONV - 1):
        xc = xc + cw_ref[pl.ds(k, 1), cs] * lbuf[pl.ds(HALO - (CONV - 1) + k, tm), cs]
    xcm = xc.astype(_MXU)
    r = _sigmoid(_nn(xcm, wrg_ref[hh]) + brg_ref[pl.ds(hh, 1), :])
    ig = _sigmoid(_nn(xcm, wig_ref[hh]) + big_ref[pl.ds(hh, 1), :])
    a = jnp.exp(-LRU_C * r * sp[:, hh * HD:(hh + 1) * HD])
    one_m = 1.0 - a * a
    live = jnp.logical_and(one_m > 0.0, jnp.logical_not(start))
    inv_mult = lax.rsqrt(jnp.where(live, one_m, 1.0))
    mult = jnp.where(live, one_m * inv_mult, jnp.where(start, 1.0, 0.0))
    return xc, r, ig, a, live, inv_mult, mult


def _seg_layout(tm):
    seg = tm // 8
    return seg, seg + 8


def _to_segments(dst_ref, hh, val, tm):
    seg, pitch = _seg_layout(tm)
    for s in range(8):
        dst_ref[hh, pl.ds(s * pitch, seg), :] = val[s * seg:(s + 1) * seg, :]


def _from_segments(src_ref, hh, tm):
    seg, pitch = _seg_layout(tm)
    return jnp.concatenate([src_ref[hh, pl.ds(s * pitch, seg), :] for s in range(8)], axis=0)


def _segment_scan(a_ref, b_ref, out_ref, hk, pk, carry_ref, tm, reverse):
    seg, pitch = _seg_layout(tm)
    row = lax.broadcasted_iota(jnp.int32, (8, HD), 0)
    order = range(seg - 1, -1, -1) if reverse else range(seg)
    for hh in range(HEADS):
        cs = pl.ds(hh * HD, HD)
        if reverse:
            a0 = a_ref[hh, pl.ds(0, 8, stride=pitch), :]
            a_wrap = jnp.where(row <= 6, pltpu.roll(a0, 7, 0), 1.0)
        hv = jnp.zeros((8, HD), F32)
        pv = jnp.ones((8, HD), F32)
        for k in order:
            if not reverse:
                av = a_ref[hh, pl.ds(k, 8, stride=pitch), :]
            elif k + 1 < seg:
                av = a_ref[hh, pl.ds(k + 1, 8, stride=pitch), :]
            else:
                av = a_wrap
            hv = av * hv + b_ref[hh, pl.ds(k, 8, stride=pitch), :]
            pv = av * pv
            hk[hh, pl.ds(8 * k, 8), :] = hv
            pk[hh, pl.ds(8 * k, 8), :] = pv
        for d in (1, 2, 4):
            if reverse:
                keep, sh = row < 8 - d, 8 - d
            else:
                keep, sh = row >= d, d
            hv = hv + pv * jnp.where(keep, pltpu.roll(hv, sh, 0), 0.0)
            pv = pv * jnp.where(keep, pltpu.roll(pv, sh, 0), 1.0)
        cin = carry_ref[:, cs]
        ends = hv + pv * cin
        if reverse:
            enter = jnp.where(row <= 6, pltpu.roll(ends, 7, 0), cin)
            carry_ref[:, cs] = jnp.broadcast_to((a0 * ends)[0:1, :], (8, HD))
        else:
            enter = jnp.where(row >= 1, pltpu.roll(ends, 1, 0), cin)
            carry_ref[:, cs] = jnp.broadcast_to(ends[7:8, :], (8, HD))
        for k in range(seg):
            out_ref[hh, pl.ds(k, 8, stride=pitch), :] = hk[hh, pl.ds(8 * k, 8), :] + pk[hh, pl.ds(8 * k, 8), :] * enter


def _mixer_fwd(z, x, gw, small):
    s = x.shape[0]
    tm = min(TM_SEQ, s)
    (pool_w, pool_scale, conv_w, conv_b, w_rg, b_rg, w_ig, b_ig, lam, b_gate) = small

    def body(z_ref, x_ref, gw_ref, pw_ref, ps_ref, cw_ref, cb_ref, wrg_ref, brg_ref, wig_ref, big_ref, lam_ref,
             bg_ref, h_ref, yl_ref, mg_ref, yp_ref, yr_ref, h1_ref,
             pprojT, lru_w, wout_w, pbuf, lbuf, a_s, b_s, h_s, hk, pk, hcar, sems):
        i = pl.program_id(0)
        t0 = i * tm

        @pl.when(i == 0)
        def _():
            _load_weights(gw_ref, [("pproj", pprojT, PW), ("lru", lru_w, D), ("wout", wout_w, D)], sems)
            pbuf[pl.ds(0, HALO), :] = jnp.zeros((HALO, PW), F32)
            lbuf[pl.ds(0, HALO), :] = jnp.zeros((HALO, D), F32)
            hcar[...] = jnp.zeros_like(hcar)

        pbuf[pl.ds(HALO, tm), :] = z_ref[:, pl.ds(0, PW)]
        _, mixed_pre = _pool_tile(pbuf, t0, tm, pw_ref, ps_ref)
        mixed = jnp.concatenate(mixed_pre, axis=1) * ps_ref[...]
        y_pool = _nt(mixed.astype(_MXU), pprojT[...])
        pbuf[pl.ds(0, HALO), :] = pbuf[pl.ds(tm, HALO), :]

        lbuf[pl.ds(HALO, tm), :] = z_ref[:, pl.ds(PW, D)]
        sp, _ = _softplus_neg(lam_ref[...])
        start = (t0 + lax.broadcasted_iota(jnp.int32, (tm, HD), 0)) == 0
        for hh in range(HEADS):
            xc, r, ig, a, _, _, mult = _lru_gates_head(hh, lbuf, start, tm, cw_ref, cb_ref, wrg_ref, brg_ref,
                                                       wig_ref, big_ref, sp)
            _to_segments(a_s, hh, a, tm)
            _to_segments(b_s, hh, mult * ig * xc, tm)
        lbuf[pl.ds(0, HALO), :] = lbuf[pl.ds(tm, HALO), :]
        _segment_scan(a_s, b_s, h_s, hk, pk, hcar, tm, reverse=False)
        for hh in range(HEADS):
            h_ref[:, pl.ds(hh * HD, HD)] = _from_segments(h_s, hh, tm)
        gel, _ = _gelu_and_grad(z_ref[:, pl.ds(PW + D, D)])
        yl = (h_ref[...] * gel).astype(_MXU)
        yl_ref[...] = yl
        y_lru = _nn(yl, lru_w[...])

        g0 = _sigmoid(z_ref[:, pl.ds(PW + 2 * D, D)] + bg_ref[pl.ds(0, 1), :])
        g1 = _sigmoid(z_ref[:, pl.ds(PW + 3 * D, D)] + bg_ref[pl.ds(1, 1), :])
        merged = (g0 * y_pool + g1 * y_lru).astype(_MXU)
        mg_ref[...] = merged
        yp_ref[...] = y_pool.astype(_MXU)
        yr_ref[...] = y_lru.astype(_MXU)
        h1_ref[...] = x_ref[...] + _nn(merged, wout_w[...])

    tok = lambda w, dt: jax.ShapeDtypeStruct((s, w), dt)
    tspec = lambda w: pl.BlockSpec((tm, w), lambda i: (i, 0))
    full = lambda a: pl.BlockSpec(a.shape, lambda i: (0,) * a.ndim)
    seg_buf = pltpu.VMEM((HEADS, 8 * _seg_layout(tm)[1], HD), F32)
    return pl.pallas_call(
        body, name="mixer_fwd", grid=(s // tm,),
        out_shape=(tok(D, F32), tok(D, _MXU), tok(D, _MXU), tok(D, _MXU), tok(D, _MXU), tok(D, F32)),
        in_specs=[tspec(NIN), tspec(D), pl.BlockSpec(memory_space=pl.ANY)] + [full(a) for a in small],
        out_specs=(tspec(D),) * 6,
        scratch_shapes=[pltpu.VMEM((D, PW), _MXU), pltpu.VMEM((D, D), _MXU), pltpu.VMEM((D, D), _MXU),
                        pltpu.VMEM((tm + HALO, PW), F32), pltpu.VMEM((tm + HALO, D), F32),
                        seg_buf, seg_buf, seg_buf, pltpu.VMEM((HEADS, tm, HD), F32), pltpu.VMEM((HEADS, tm, HD), F32),
                        pltpu.VMEM((8, D), F32), pltpu.SemaphoreType.DMA((3 * NDEV,))],
        compiler_params=_cparams(1),
    )(z, x, gw, *small)


def _ffn_fwd(h1, g2, gw):
    s = h1.shape[0]
    tm = min(TM, s)
    half = FF // 2

    def body(h1_ref, g2_ref, gw_ref, v_ref, gf_ref, uf_ref, h2_ref, wffnT, wffo, sems):
        @pl.when(pl.program_id(0) == 0)
        def _():
            _load_weights(gw_ref, [("wffn", wffnT, D), ("wffo", wffo, D)], sems)

        hv = h1_ref[...]
        inv = lax.rsqrt(jnp.mean(hv * hv, axis=-1, keepdims=True) + EPS)
        v = (hv * inv * g2_ref[...]).astype(_MXU)
        v_ref[...] = v
        acc = hv
        for ch in range(2):
            cs = pl.ds(ch * half, half)
            gf = _nt(v, wffnT[pl.ds(ch * half, half), :]).astype(_MXU)
            uf = _nt(v, wffnT[pl.ds(FF + ch * half, half), :]).astype(_MXU)
            gf_ref[:, cs] = gf
            uf_ref[:, cs] = uf
            gf32 = gf.astype(F32)
            act = (gf32 * _sigmoid(gf32) * uf.astype(F32)).astype(_MXU)
            acc = acc + _nn(act, wffo[pl.ds(ch * half, half), :])
        h2_ref[...] = acc

    tspec = lambda w: pl.BlockSpec((tm, w), lambda i: (i, 0))
    return pl.pallas_call(
        body, name="ffn_fwd", grid=(s // tm,),
        out_shape=(jax.ShapeDtypeStruct((s, D), _MXU), jax.ShapeDtypeStruct((s, FF), _MXU),
                   jax.ShapeDtypeStruct((s, FF), _MXU), jax.ShapeDtypeStruct((s, D), F32)),
        in_specs=[tspec(D), pl.BlockSpec((1, D), lambda i: (0, 0)), pl.BlockSpec(memory_space=pl.ANY)],
        out_specs=(tspec(D), tspec(FF), tspec(FF), tspec(D)),
        scratch_shapes=[pltpu.VMEM((2 * FF, D), _MXU), pltpu.VMEM((FF, D), _MXU), pltpu.SemaphoreType.DMA((2 * NDEV,))],
        compiler_params=_cparams(1),
    )(h1, g2, gw)


def _rms_bwd(dy, xn, inv, g):
    dg = jnp.sum(dy * xn, axis=0, keepdims=True)
    dxn = dy * g
    dx = inv * (dxn - xn * jnp.mean(dxn * xn, axis=-1, keepdims=True))
    return dx, dg


def _ple_loss_fwd_bwd(h2, p, target, g3, gfin, gw):
    s = h2.shape[0]
    tm = min(TM, s)

    def body(h2_ref, p_ref, t_ref, g3_ref, gf_ref, gw_ref,
             dh2_ref, n3_ref, dpg_ref, de_ref, loss_ref, dg3_ref, dgf_ref, wpg, pleT, sems):
        i = pl.program_id(0)

        @pl.when(i == 0)
        def _():
            _load_weights(gw_ref, [("wpg", wpg, D), ("ple", pleT, PLE)], sems)
            loss_ref[...] = jnp.zeros_like(loss_ref)
            dg3_ref[...] = jnp.zeros_like(dg3_ref)
            dgf_ref[...] = jnp.zeros_like(dgf_ref)

        hv = h2_ref[...]
        inv3 = lax.rsqrt(jnp.mean(hv * hv, axis=-1, keepdims=True) + EPS)
        xn3 = hv * inv3
        n3 = (xn3 * g3_ref[...]).astype(_MXU)
        n3_ref[...] = n3
        pg = _sigmoid(_nn(n3, wpg[...]))
        e = _nt(p_ref[...].astype(_MXU), pleT[...])
        h3 = hv + pg * e
        invf = lax.rsqrt(jnp.mean(h3 * h3, axis=-1, keepdims=True) + EPS)
        xf = h3 * invf
        diff = xf * gf_ref[...] - t_ref[...]
        loss_ref[...] += jnp.sum(diff * diff) * (0.5 / D)
        dh3, dgf = _rms_bwd(diff * (1.0 / D), xf, invf, gf_ref[...])
        dgf_ref[...] += dgf
        de_ref[...] = (dh3 * pg).astype(_MXU)
        dpg = (dh3 * e * pg * (1.0 - pg)).astype(_MXU)
        dpg_ref[...] = dpg
        dn3 = _nt(dpg, wpg[...])
        dx3, dg3 = _rms_bwd(dn3, xn3, inv3, g3_ref[...])
        dg3_ref[...] += dg3
        dh2_ref[...] = dh3 + dx3

    tspec = lambda w: pl.BlockSpec((tm, w), lambda i: (i, 0))
    vec = pl.BlockSpec((1, D), lambda i: (0, 0))
    tok = lambda w, dt: jax.ShapeDtypeStruct((s, w), dt)
    return pl.pallas_call(
        body, name="ple_loss", grid=(s // tm,),
        out_shape=(tok(D, F32), tok(D, _MXU), tok(D, _MXU), tok(D, _MXU), jax.ShapeDtypeStruct((8, 128), F32),
                   jax.ShapeDtypeStruct((1, D), F32), jax.ShapeDtypeStruct((1, D), F32)),
        in_specs=[tspec(D), tspec(PLE), tspec(D), vec, vec, pl.BlockSpec(memory_space=pl.ANY)],
        out_specs=(tspec(D), tspec(D), tspec(D), tspec(D), pl.BlockSpec((8, 128), lambda i: (0, 0)), vec, vec),
        scratch_shapes=[pltpu.VMEM((D, D), _MXU), pltpu.VMEM((D, PLE), _MXU), pltpu.SemaphoreType.DMA((2 * NDEV,))],
        compiler_params=_cparams(1),
    )(h2, p, target, g3, gfin, gw)


def _ffn_bwd_hidden(dh2, gf, uf, gw):
    s = dh2.shape[0]
    tm = min(TM, s)
    half = FF // 2

    def body(dh2_ref, gf_ref, uf_ref, gw_ref, dff_ref, act_ref, wffo, sems):
        @pl.when(pl.program_id(0) == 0)
        def _():
            _load_weights(gw_ref, [("wffo", wffo, D)], sems)

        dm = dh2_ref[...].astype(_MXU)
        for ch in range(2):
            cs = pl.ds(ch * half, half)
            dact = _nt(dm, wffo[pl.ds(ch * half, half), :])
            gfv = gf_ref[:, cs].astype(F32)
            ufv = uf_ref[:, cs].astype(F32)
            sg = _sigmoid(gfv)
            silu = gfv * sg
            act_ref[:, cs] = (silu * ufv).astype(_MXU)
            dff_ref[:, pl.ds(ch * half, half)] = (dact * ufv * (sg * (1.0 + gfv * (1.0 - sg)))).astype(_MXU)
            dff_ref[:, pl.ds(FF + ch * half, half)] = (dact * silu).astype(_MXU)

    tspec = lambda w: pl.BlockSpec((tm, w), lambda i: (i, 0))
    return pl.pallas_call(
        body, name="ffn_bwd_hidden", grid=(s // tm,),
        out_shape=(jax.ShapeDtypeStruct((s, 2 * FF), _MXU), jax.ShapeDtypeStruct((s, FF), _MXU)),
        in_specs=[tspec(D), tspec(FF), tspec(FF), pl.BlockSpec(memory_space=pl.ANY)],
        out_specs=(tspec(2 * FF), tspec(FF)),
        scratch_shapes=[pltpu.VMEM((FF, D), _MXU), pltpu.SemaphoreType.DMA((NDEV,))],
        compiler_params=_cparams(1),
    )(dh2, gf, uf, gw)


def _proj_norm_bwd(dy, x, dres, g, gw, slab, width, name):
    s = x.shape[0]
    tm = min(TM, s)

    def body(dy_ref, x_ref, dr_ref, g_ref, gw_ref, dx_ref, dg_ref, wT, sems):
        @pl.when(pl.program_id(0) == 0)
        def _():
            _load_weights(gw_ref, [(slab, wT, D)], sems)
            dg_ref[...] = jnp.zeros_like(dg_ref)

        dv = _nn(dy_ref[...], wT[...])
        xv = x_ref[...]
        inv = lax.rsqrt(jnp.mean(xv * xv, axis=-1, keepdims=True) + EPS)
        dx, dg = _rms_bwd(dv, xv * inv, inv, g_ref[...])
        dg_ref[...] += dg
        dx_ref[...] = dr_ref[...] + dx

    tspec = lambda w: pl.BlockSpec((tm, w), lambda i: (i, 0))
    vec = pl.BlockSpec((1, D), lambda i: (0, 0))
    return pl.pallas_call(
        body, name=name, grid=(s // tm,),
        out_shape=(jax.ShapeDtypeStruct((s, D), F32), jax.ShapeDtypeStruct((1, D), F32)),
        in_specs=[tspec(width), tspec(D), tspec(D), vec, pl.BlockSpec(memory_space=pl.ANY)],
        out_specs=(tspec(D), vec),
        scratch_shapes=[pltpu.VMEM((width, D), _MXU), pltpu.SemaphoreType.DMA((NDEV,))],
        compiler_params=_cparams(1),
    )(dy, x, dres, g, gw)


def _mixer_bwd(dh1, z, h, y_pool, y_lru, gw, small):
    s = dh1.shape[0]
    tm = min(TM_SEQ, s)
    nt = s // tm
    (pool_w, pool_scale, conv_w, conv_b, w_rg, b_rg, w_ig, b_ig, lam, b_gate) = small

    def body(dh1_ref, z_ref, zp_ref, h_ref, hp_ref, yp_ref, yr_ref, gw_ref,
             pw_ref, ps_ref, cw_ref, cb_ref, wrg_ref, brg_ref, wig_ref, big_ref, lam_ref, bg_ref,
             dz_ref, dyr_ref, dyp_ref, mx_ref,
             gbg_ref, glam_ref, gbrg_ref, gbig_ref, gcb_ref, gcw_ref, gps_ref, gpw_ref, gwrg_ref, gwig_ref,
             pprojT, lru_w, wout_w, pbuf, lbuf, hbuf, qbuf, xbuf, a_s, g_s, dh_s, hk, pk, r_s, ig_s, xc_s, mu_s, f_s,
             dcar, sems):
        step = pl.program_id(0)
        i = nt - 1 - step
        t0 = i * tm

        @pl.when(step == 0)
        def _():
            _load_weights(gw_ref, [("pproj", pprojT, PW), ("lru", lru_w, D), ("wout", wout_w, D)], sems)
            for ref in (gbg_ref, glam_ref, gbrg_ref, gbig_ref, gcb_ref, gcw_ref, gps_ref, gpw_ref, gwrg_ref, gwig_ref):
                ref[...] = jnp.zeros_like(ref)
            qbuf[pl.ds(tm, HALO), :] = jnp.zeros((HALO, PW), F32)
            xbuf[pl.ds(tm, 8), :] = jnp.zeros((8, D), F32)
            dcar[...] = jnp.zeros_like(dcar)

        first = i == 0
        zprev = jnp.where(first, 0.0, zp_ref[...])
        hprev = jnp.where(first, 0.0, hp_ref[...])

        d_merged = _nt(dh1_ref[...].astype(_MXU), wout_w[...])

        g0 = _sigmoid(z_ref[:, pl.ds(PW + 2 * D, D)] + bg_ref[pl.ds(0, 1), :])
        g1 = _sigmoid(z_ref[:, pl.ds(PW + 3 * D, D)] + bg_ref[pl.ds(1, 1), :])
        dz0 = d_merged * yp_ref[...].astype(F32) * g0 * (1.0 - g0)
        dz1 = d_merged * yr_ref[...].astype(F32) * g1 * (1.0 - g1)
        dz_ref[:, pl.ds(PW + 2 * D, D)] = dz0.astype(_MXU)
        dz_ref[:, pl.ds(PW + 3 * D, D)] = dz1.astype(_MXU)
        gbg_ref[pl.ds(0, 1), :] += jnp.sum(dz0, axis=0, keepdims=True)
        gbg_ref[pl.ds(1, 1), :] += jnp.sum(dz1, axis=0, keepdims=True)
        d_ypool = (d_merged * g0).astype(_MXU)
        d_ylru = (d_merged * g1).astype(_MXU)
        dyp_ref[...] = d_ypool
        dyr_ref[...] = d_ylru

        d_yl = _nt(d_ylru, lru_w[...])
        gel, dgel = _gelu_and_grad(z_ref[:, pl.ds(PW + D, D)])
        dz_ref[:, pl.ds(PW + D, D)] = (d_yl * h_ref[...] * dgel).astype(_MXU)
        g_full = d_yl * gel
        lbuf[pl.ds(0, HALO), :] = zprev[:, PW:PW + D]
        lbuf[pl.ds(HALO, tm), :] = z_ref[:, pl.ds(PW, D)]
        hbuf[pl.ds(0, 8), :] = hprev
        hbuf[pl.ds(8, tm), :] = h_ref[...]
        sp, sneg = _softplus_neg(lam_ref[...])
        start = (t0 + lax.broadcasted_iota(jnp.int32, (tm, HD), 0)) == 0
        for hh in range(HEADS):
            cs = pl.ds(hh * HD, HD)
            xc, r, ig, a, live, inv_mult, mult = _lru_gates_head(hh, lbuf, start, tm, cw_ref, cb_ref, wrg_ref,
                                                                 brg_ref, wig_ref, big_ref, sp)
            _to_segments(a_s, hh, a, tm)
            _to_segments(g_s, hh, g_full[:, hh * HD:(hh + 1) * HD], tm)
            r_s[:, cs] = r
            ig_s[:, cs] = ig
            xc_s[:, cs] = xc
            mu_s[:, cs] = mult
            f_s[:, cs] = jnp.where(live, -(a * a) * inv_mult, 0.0)
        _segment_scan(a_s, g_s, dh_s, hk, pk, dcar, tm, reverse=True)
        for hh in range(HEADS):
            cs = pl.ds(hh * HD, HD)
            a = _from_segments(a_s, hh, tm)
            r = r_s[:, cs]
            ig = ig_s[:, cs]
            xc = xc_s[:, cs]
            mult = mu_s[:, cs]
            dh = _from_segments(dh_s, hh, tm)
            d_mult = dh * ig * xc
            d_loga = dh * hbuf[pl.ds(7, tm), cs] * a + d_mult * f_s[:, cs]
            glam_ref[:, cs] += jnp.sum(d_loga * (LRU_C * r) * sneg[:, hh * HD:(hh + 1) * HD], axis=0, keepdims=True)
            d_rpre = d_loga * (-LRU_C * sp[:, hh * HD:(hh + 1) * HD]) * r * (1.0 - r)
            d_igpre = dh * mult * xc * ig * (1.0 - ig)
            gbrg_ref[pl.ds(hh, 1), :] += jnp.sum(d_rpre, axis=0, keepdims=True)
            gbig_ref[pl.ds(hh, 1), :] += jnp.sum(d_igpre, axis=0, keepdims=True)
            drm = d_rpre.astype(_MXU)
            dim = d_igpre.astype(_MXU)
            xcm = xc.astype(_MXU)
            gwrg_ref[hh] += _tn(xcm, drm)
            gwig_ref[hh] += _tn(xcm, dim)
            d_xc = dh * mult * ig + _nt(drm, wrg_ref[hh]) + _nt(dim, wig_ref[hh])
            gcb_ref[:, cs] += jnp.sum(d_xc, axis=0, keepdims=True)
            for k in range(CONV):
                gcw_ref[pl.ds(k, 1), cs] += jnp.sum(d_xc * lbuf[pl.ds(HALO - (CONV - 1) + k, tm), cs], axis=0,
                                                    keepdims=True)
            xbuf[pl.ds(0, tm), cs] = d_xc
        dzl = cw_ref[pl.ds(CONV - 1, 1), :] * xbuf[pl.ds(0, tm), :]
        for k in range(CONV - 1):
            dzl = dzl + cw_ref[pl.ds(k, 1), :] * xbuf[pl.ds(CONV - 1 - k, tm), :]
        dz_ref[:, pl.ds(PW, D)] = dzl.astype(_MXU)
        xbuf[pl.ds(tm, 8), :] = xbuf[pl.ds(0, 8), :]

        d_mixed = _nn(d_ypool, pprojT[...])
        pbuf[pl.ds(0, HALO), :] = zprev[:, 0:PW]
        pbuf[pl.ds(HALO, tm), :] = z_ref[:, pl.ds(0, PW)]
        pooled, mixed_pre = _pool_tile(pbuf, t0, tm, pw_ref, ps_ref)
        mp = jnp.concatenate(mixed_pre, axis=1)
        mx_ref[...] = (mp * ps_ref[...]).astype(_MXU)
        gps_ref[...] += jnp.sum(d_mixed * mp, axis=0, keepdims=True)
        d_mp = (d_mixed * ps_ref[...]).astype(_MXU)
        t = t0 + lax.broadcasted_iota(jnp.int32, (tm, GD), 0)
        d_pooled = []
        for g, w in enumerate(WINDOWS):
            dmg = d_mp[:, g * GD:(g + 1) * GD]
            gpw_ref[g] += _tn(pooled[g].astype(_MXU), dmg)
            dp = _nt(dmg, pw_ref[g])
            d_pooled.append(dp)
            qbuf[pl.ds(0, tm), pl.ds(g * GD, GD)] = dp / jnp.minimum(t + 1, w).astype(F32)
        for g, w in enumerate(WINDOWS):
            cs = pl.ds(g * GD, GD)
            acc = qbuf[pl.ds(0, tm), cs]
            for d in range(1, w):
                acc = acc + qbuf[pl.ds(d, tm), cs]
            dz_ref[:, cs] = (acc - d_pooled[g]).astype(_MXU)
        qbuf[pl.ds(tm, HALO), :] = qbuf[pl.ds(0, HALO), :]

    rev = lambda w: pl.BlockSpec((tm, w), lambda g: (nt - 1 - g, 0))
    prev = lambda rows, w: pl.BlockSpec((rows, w), lambda g: (jnp.maximum((nt - 1 - g) * (tm // rows) - 1, 0), 0))
    full = lambda a: pl.BlockSpec(a.shape, lambda g: (0,) * a.ndim)
    tok = lambda w, dt: jax.ShapeDtypeStruct((s, w), dt)
    acc_shapes = [(2, D), (1, D), (HEADS, HD), (HEADS, HD), (1, D), (CONV, D), (1, PW), (GROUPS, GD, GD),
                  (HEADS, HD, HD), (HEADS, HD, HD)]
    acc_specs = tuple(pl.BlockSpec(sh, lambda g, n=len(sh): (0,) * n) for sh in acc_shapes)
    seg_buf = pltpu.VMEM((HEADS, 8 * _seg_layout(tm)[1], HD), F32)
    tile_buf = pltpu.VMEM((tm, D), F32)
    return pl.pallas_call(
        body, name="mixer_bwd", grid=(nt,),
        out_shape=(tok(NIN, _MXU), tok(D, _MXU), tok(D, _MXU), tok(PW, _MXU))
        + tuple(jax.ShapeDtypeStruct(sh, F32) for sh in acc_shapes),
        in_specs=[rev(D), rev(NIN), prev(HALO, NIN), rev(D), prev(8, D), rev(D), rev(D),
                  pl.BlockSpec(memory_space=pl.ANY)] + [full(a) for a in small],
        out_specs=(rev(NIN), rev(D), rev(D), rev(PW)) + acc_specs,
        scratch_shapes=[pltpu.VMEM((D, PW), _MXU), pltpu.VMEM((D, D), _MXU), pltpu.VMEM((D, D), _MXU),
                        pltpu.VMEM((tm + HALO, PW), F32), pltpu.VMEM((tm + HALO, D), F32),
                        pltpu.VMEM((tm + 8, D), F32), pltpu.VMEM((tm + HALO, PW), F32), pltpu.VMEM((tm + 8, D), F32),
                        seg_buf, seg_buf, seg_buf, pltpu.VMEM((HEADS, tm, HD), F32), pltpu.VMEM((HEADS, tm, HD), F32),
                        tile_buf, tile_buf, tile_buf, tile_buf, tile_buf,
                        pltpu.VMEM((8, D), F32), pltpu.SemaphoreType.DMA((3 * NDEV,))],
        compiler_params=_cparams(1),
    )(dh1, z, z, h, h, y_pool, y_lru, gw, *small)


def _split3(a):
    hi = a.astype(jnp.bfloat16).astype(F32)
    mid = (a - hi).astype(jnp.bfloat16).astype(F32)
    lo = (a - hi - mid).astype(jnp.bfloat16).astype(F32)
    return jnp.stack([hi, mid, lo])


def _small_pack(parts):
    flat = jnp.concatenate([a.reshape(-1) for a in parts])
    return jnp.pad(flat, (0, NDEV * SMALL_ROWS * D - flat.shape[0])).reshape(NDEV * SMALL_ROWS, D)


def _small_unpack(packed, shapes):
    flat = packed.reshape(-1)
    out, o = [], 0
    for sh in shapes:
        n = math.prod(sh)
        out.append(flat[o:o + n].reshape(sh))
        o += n
    return out


def kernel(x, p, norm1_g, w_in, b_gate, pool_w, pool_scale, pool_proj, conv_w, conv_b, w_rg, b_rg, w_ig, b_ig, lru_lambda, lru_proj, w_out, norm2_g, w_ffn_in, w_ffn_out, ple_norm_g, w_ple_gate, w_ple_proj, final_g, loss_target, m_norm1_g, m_w_in, m_b_gate, m_pool_w, m_pool_scale, m_pool_proj, m_conv_w, m_conv_b, m_w_rg, m_b_rg, m_w_ig, m_b_ig, m_lru_lambda, m_lru_proj, m_w_out, m_norm2_g, m_w_ffn_in, m_w_ffn_out, m_ple_norm_g, m_w_ple_gate, m_w_ple_proj, m_final_g, v_norm1_g, v_w_in, v_b_gate, v_pool_w, v_pool_scale, v_pool_proj, v_conv_w, v_conv_b, v_w_rg, v_b_rg, v_w_ig, v_b_ig, v_lru_lambda, v_lru_proj, v_w_out, v_norm2_g, v_w_ffn_in, v_w_ffn_out, v_ple_norm_g, v_w_ple_gate, v_w_ple_proj, v_final_g):
    axes = ("x", "y", "c")
    me = 4 * lax.axis_index("x") + 2 * lax.axis_index("y") + lax.axis_index("c")
    x2 = x[0]
    p2 = p[0, 0]
    tgt = loss_target[0]

    n_small = (CONV + 2) * 128
    small_terms = _split3(jnp.concatenate([conv_w[0].reshape(-1), b_gate[0].reshape(-1)]))
    small_rows = jnp.pad(small_terms, ((0, 16 - 3), (0, D - n_small)))
    own = jnp.concatenate([
        w_in[0].T.astype(_MXU), w_ffn_in[0].T.astype(_MXU),
        jnp.pad(pool_proj[0].T, ((0, 0), (0, D - PW))).astype(_MXU),
        jnp.pad(w_ple_proj[0].T, ((0, 0), (0, D - PLE))).astype(_MXU),
        lru_proj[0].astype(_MXU), w_out[0].astype(_MXU), w_ffn_out[0].astype(_MXU), w_ple_gate[0].astype(_MXU),
        small_rows.astype(_MXU),
    ], axis=0)
    gw = _all_gather_weights(own)
    off = W_OFF["f32s"][0]
    st = gw[:, off:off + 3, :n_small].astype(F32)
    sf = st[:, 0] + st[:, 1] + st[:, 2]
    conv_w_full = sf[:, :CONV * 128].reshape(NDEV, CONV, 128).transpose(1, 0, 2).reshape(CONV, D)
    b_gate_full = sf[:, CONV * 128:].reshape(NDEV, 2, 128).transpose(1, 0, 2).reshape(2, D)

    small = (pool_w[0].astype(_MXU), pool_scale, conv_w_full, conv_b, w_rg[0].astype(_MXU), b_rg[0],
             w_ig[0].astype(_MXU), b_ig[0], lru_lambda, b_gate_full)

    u, z = _inproj_fwd(x2, norm1_g, gw)
    h, yl, merged, y_pool, y_lru, h1 = _mixer_fwd(z, x2, gw, small)
    v, gf, uf, h2 = _ffn_fwd(h1, norm2_g, gw)

    dh2, n3, dpg, de, loss_blk, g_ple_norm, g_final = _ple_loss_fwd_bwd(h2, p2, tgt, ple_norm_g, final_g.reshape(1, D), gw)
    loss = lax.psum(loss_blk[0, 0], axes)
    dff, act = _ffn_bwd_hidden(dh2, gf, uf, gw)
    dh1, g_norm2 = _proj_norm_bwd(dff, h1, dh2, norm2_g, gw, "wffn", 2 * FF, "ffn_bwd_in")
    (dz, d_ylru, d_ypool, mixed, g_bgate, g_lam, g_brg, g_big, g_convb, g_convw, g_pscale, g_poolw, g_wrg,
     g_wig) = _mixer_bwd(dh1, z, h, y_pool, y_lru, gw, small)
    grad_x, g_norm1 = _proj_norm_bwd(dz, x2, dh1, norm1_g, gw, "win", NIN, "inproj_bwd")

    small_shapes = [(1, D), (GROUPS, GD, GD), (1, PW), (1, D), (HEADS, HD, HD), (HEADS, HD), (HEADS, HD, HD),
                    (HEADS, HD), (1, D), (1, D), (1, D), (1, D), (2, D), (CONV, D)]
    small_part = _small_pack([g_norm1, g_poolw, g_pscale, g_convb, g_wrg, g_brg, g_wig, g_big, g_lam, g_norm2,
                              g_ple_norm, g_final, g_bgate, g_convw])
    riders = [_grad_matmul(n3, dpg, "grad_w_ple_gate"), _grad_matmul(yl, d_ylru, "grad_lru_proj"),
              _grad_matmul(merged, dh1, "grad_w_out"), _grad_matmul(d_ypool, mixed, "grad_pool_proj"),
              _grad_matmul(de, p2, "grad_w_ple_proj")]
    rs_wffn = _grad_matmul_rs(dff, v, "grad_w_ffn_in", 704, narrow=_MXU)
    rs_wffo = _grad_matmul_rs(act, dh2, "grad_w_ffn_out", 352, chips_per_block=2, extras=[small_part])
    rs_win = _grad_matmul_rs(dz, u, "grad_w_in", 576, extras=riders, narrow=_MXU)

    def reduced(parts, name):
        t_own, landed, _ = parts
        return _sum_arrays([t_own, landed[0], landed[1], landed[2]], "rs_sum_" + name)

    red_wffn = reduced(rs_wffn, "wffn")
    red_wffo = reduced(rs_wffo, "wffo")
    red_win = reduced(rs_win, "win")
    g_w_in = red_win[:576].T
    g_w_ffn_in = red_wffn.T
    g_w_ffn_out = red_wffo[:352]
    g_w_ple_gate, g_lru_proj, g_w_out = red_win[576:704], red_win[704:832], red_win[832:960]
    g_pool_proj = red_win[960:1088, :PW].T
    g_w_ple_proj = red_win[1088:1216, :PLE].T
    small_red = _all_gather_small(red_wffo[352:])
    (gs_norm1, gs_poolw, gs_pscale, gs_convb, gs_wrg, gs_brg, gs_wig, gs_big, gs_lam, gs_norm2, gs_ple_norm,
     gs_final, gs_bgate, gs_convw) = _small_unpack(small_red, small_shapes)
    g_b_gate = lax.dynamic_slice_in_dim(gs_bgate, me * 128, 128, axis=1)
    g_conv_w = lax.dynamic_slice_in_dim(gs_convw, me * 128, 128, axis=1)

    grads = {
        "norm1_g": gs_norm1, "w_in": g_w_in[None], "b_gate": g_b_gate[None], "pool_w": gs_poolw[None],
        "pool_scale": gs_pscale, "pool_proj": g_pool_proj[None], "conv_w": g_conv_w[None], "conv_b": gs_convb,
        "w_rg": gs_wrg[None], "b_rg": gs_brg[None], "w_ig": gs_wig[None], "b_ig": gs_big[None], "lru_lambda": gs_lam,
        "lru_proj": g_lru_proj[None], "w_out": g_w_out[None], "norm2_g": gs_norm2, "w_ffn_in": g_w_ffn_in[None],
        "w_ffn_out": g_w_ffn_out[None], "ple_norm_g": gs_ple_norm, "w_ple_gate": g_w_ple_gate[None],
        "w_ple_proj": g_w_ple_proj[None], "final_g": gs_final.reshape(D),
    }
    weights = dict(norm1_g=norm1_g, w_in=w_in, b_gate=b_gate, pool_w=pool_w, pool_scale=pool_scale, pool_proj=pool_proj,
                   conv_w=conv_w, conv_b=conv_b, w_rg=w_rg, b_rg=b_rg, w_ig=w_ig, b_ig=b_ig, lru_lambda=lru_lambda,
                   lru_proj=lru_proj, w_out=w_out, norm2_g=norm2_g, w_ffn_in=w_ffn_in, w_ffn_out=w_ffn_out,
                   ple_norm_g=ple_norm_g, w_ple_gate=w_ple_gate, w_ple_proj=w_ple_proj, final_g=final_g)
    moments_m = dict(norm1_g=m_norm1_g, w_in=m_w_in, b_gate=m_b_gate, pool_w=m_pool_w, pool_scale=m_pool_scale,
                     pool_proj=m_pool_proj, conv_w=m_conv_w, conv_b=m_conv_b, w_rg=m_w_rg, b_rg=m_b_rg, w_ig=m_w_ig,
                     b_ig=m_b_ig, lru_lambda=m_lru_lambda, lru_proj=m_lru_proj, w_out=m_w_out, norm2_g=m_norm2_g,
                     w_ffn_in=m_w_ffn_in, w_ffn_out=m_w_ffn_out, ple_norm_g=m_ple_norm_g, w_ple_gate=m_w_ple_gate,
                     w_ple_proj=m_w_ple_proj, final_g=m_final_g)
    moments_v = dict(norm1_g=v_norm1_g, w_in=v_w_in, b_gate=v_b_gate, pool_w=v_pool_w, pool_scale=v_pool_scale,
                     pool_proj=v_pool_proj, conv_w=v_conv_w, conv_b=v_conv_b, w_rg=v_w_rg, b_rg=v_b_rg, w_ig=v_w_ig,
                     b_ig=v_b_ig, lru_lambda=v_lru_lambda, lru_proj=v_lru_proj, w_out=v_w_out, norm2_g=v_norm2_g,
                     w_ffn_in=v_w_ffn_in, w_ffn_out=v_w_ffn_out, ple_norm_g=v_ple_norm_g, w_ple_gate=v_w_ple_gate,
                     w_ple_proj=v_w_ple_proj, final_g=v_final_g)
    names = list(weights)
    big = ("w_in", "w_ffn_in", "w_ffn_out", "lru_proj", "w_out", "w_ple_gate", "pool_proj", "w_ple_proj")
    delta, new_m, new_v = {}, {}, {}
    for n in big:
        sh = weights[n].shape
        as2d = lambda a: a.reshape(sh[-2], sh[-1])
        d_, m_, v_ = _adamw(as2d(weights[n]), as2d(grads[n]), as2d(moments_m[n]), as2d(moments_v[n]), "adamw_" + n)
        delta[n], new_m[n], new_v[n] = d_.reshape(sh), m_.reshape(sh), v_.reshape(sh)
    rest = [n for n in names if n not in big]
    rest_shapes = [weights[n].shape for n in rest]
    packed = [_small_pack([src[n] for n in rest]) for src in (weights, grads, moments_m, moments_v)]
    d_, m_, v_ = _adamw(*packed, "adamw_small")
    for n, a, b_, c_ in zip(rest, _small_unpack(d_, rest_shapes), _small_unpack(m_, rest_shapes),
                            _small_unpack(v_, rest_shapes)):
        delta[n], new_m[n], new_v[n] = a, b_, c_

    return (loss, grad_x[None], *[grads[n] for n in names], *[delta[n] for n in names],
            *[new_m[n] for n in names], *[new_v[n] for n in names])
```

```python
import functools
import math

import jax
import jax.numpy as jnp
from jax import lax
from jax.experimental import pallas as pl
from jax.experimental.pallas import tpu as pltpu

F32 = jnp.float32
D = 1024
NIN = 4608
PW = 512
FF = 2816
PLE = 256
HEADS, HD = 8, 128
GROUPS, GD = 4, 128
WINDOWS = (2, 4, 8, 16)
HALO = 16
CONV = 4
EPS = 1e-6
LRU_C = 8.0
NDEV = 8
MESH = pl.DeviceIdType.MESH

ADAM_LR, ADAM_B1, ADAM_B2, ADAM_EPS, ADAM_WD, ADAM_STEP = 0.001, 0.9, 0.999, 1e-08, 0.01, 10

_MXU = jnp.bfloat16
TM = 512
TM_SEQ = 256
VMEM_LIMIT = 56 * 1024 * 1024

W_FIRST = (("win", 576), ("f32s", 16))
W_SECOND = (("wffn", 704), ("pproj", 128), ("ple", 128), ("lru", 128), ("wout", 128), ("wffo", 352), ("wpg", 128))
W_OFF = {}
for _slabs in (W_FIRST, W_SECOND):
    _o = 0
    for _n, _r in _slabs:
        W_OFF[_n] = (_o, _r)
        _o += _r
SMALL_ROWS = 48


def _cparams(n_axes=1, vmem=VMEM_LIMIT):
    return pltpu.CompilerParams(dimension_semantics=("arbitrary",) * n_axes, vmem_limit_bytes=vmem)


def _my_pos():
    return lax.axis_index("x"), lax.axis_index("y"), lax.axis_index("c")


def _nt(a, b):
    return lax.dot_general(a, b, (((1,), (1,)), ((), ())), preferred_element_type=F32)


def _nn(a, b):
    return lax.dot_general(a, b, (((1,), (0,)), ((), ())), preferred_element_type=F32)


def _tn(a, b):
    return lax.dot_general(a, b, (((0,), (0,)), ((), ())), preferred_element_type=F32)


def _sigmoid(x):
    return 0.5 * jnp.tanh(0.5 * x) + 0.5


_GELU_K = math.sqrt(2.0 / math.pi)


def _gelu_and_grad(x):
    x2 = x * x
    inner = _GELU_K * (x + 0.044715 * x2 * x)
    t = jnp.tanh(inner)
    g = 0.5 * x * (1.0 + t)
    dg = 0.5 * (1.0 + t) + 0.5 * x * (1.0 - t * t) * _GELU_K * (1.0 + 3.0 * 0.044715 * x2)
    return g, dg


def _softplus_neg(lam):
    x = -lam
    t = jnp.exp(-jnp.abs(x))
    u = 1.0 + t
    l1p = jnp.where(u == 1.0, t, jnp.log(u) * t / (u - 1.0))
    return jnp.maximum(x, 0.0) + l1p, _sigmoid(x)


def _start_slab_loads(g_ref, name, dst_ref, sems, base, width=D):
    off, rows = W_OFF[name]
    copies = []
    for k in range(NDEV):
        if width == D:
            src = g_ref.at[k, pl.ds(off, rows), :]
        else:
            src = g_ref.at[k, pl.ds(off, rows), pl.ds(0, width)]
        cp = pltpu.make_async_copy(src, dst_ref.at[pl.ds(k * rows, rows), :], sems.at[base + k])
        cp.start()
        copies.append(cp)
    return copies


def _load_weights(g_ref, items, sems):
    copies = []
    for n, (name, dst, width) in enumerate(items):
        copies += _start_slab_loads(g_ref, name, dst, sems, n * NDEV, width)
    for cp in copies:
        cp.wait()


def _gather_phases(own_ref, out_ref, stage, send_sems, recv_sems, local_sem):
    x, y, c = _my_pos()
    me, sibling = (x, y, c), (x, y, 1 - c)
    chips = [(1 - x, y), (x, 1 - y), (1 - x, 1 - y)]

    def slab(px, py, pc):
        return out_ref.at[4 * px + 2 * py + pc]

    def copy(k, block, to, src=None):
        return pltpu.make_async_remote_copy(
            src_ref=slab(*block) if src is None else src, dst_ref=slab(*block),
            send_sem=send_sems.at[k], recv_sem=recv_sems.at[k], device_id=to, device_id_type=MESH)

    mine = pltpu.make_async_copy(stage, slab(*me), local_sem)
    first = [copy(0, me, sibling, src=stage)] + [copy(1 + j, me, (*chip, c), src=stage) for j, chip in enumerate(chips)]
    passed = [copy(4 + j, (*chip, c), sibling) for j, chip in enumerate(chips)]

    def send_mine():
        pltpu.sync_copy(own_ref, stage)
        mine.start()
        for cp in first:
            cp.start()

    def pass_on(js):
        for j in js:
            copy(1 + j, (*chips[j], c), me).wait_recv()
            passed[j].start()

    def finish():
        copy(0, sibling, me).wait_recv()
        for j, chip in enumerate(chips):
            copy(4 + j, (*chip, 1 - c), me).wait_recv()
        for cp in first + passed:
            cp.wait_send()
        mine.wait()

    return send_mine, pass_on, finish


def _all_gather_weights(own):
    rows, cols = own.shape

    def body(own_ref, out_ref, stage, send_sems, recv_sems, local_sem):
        send_mine, pass_on, finish = _gather_phases(own_ref, out_ref, stage, send_sems, recv_sems, local_sem)
        send_mine()
        pass_on((0, 1, 2))
        finish()

    return pl.pallas_call(
        body, name="ag_weights",
        out_shape=jax.ShapeDtypeStruct((NDEV, rows, cols), own.dtype),
        in_specs=[pl.BlockSpec(memory_space=pl.ANY)],
        out_specs=pl.BlockSpec(memory_space=pl.ANY),
        scratch_shapes=[pltpu.VMEM((rows, cols), own.dtype), pltpu.SemaphoreType.DMA((7,)),
                        pltpu.SemaphoreType.DMA((7,)), pltpu.SemaphoreType.DMA],
        compiler_params=pltpu.CompilerParams(vmem_limit_bytes=VMEM_LIMIT),
    )(own)


def _all_gather_small(piece):
    rows = piece.shape[0]

    def body(p_ref, out_ref, send_sems, recv_sems, local_sem):
        x, y, c = _my_pos()
        me = 4 * x + 2 * y + c
        mine = pltpu.make_async_copy(p_ref, out_ref.at[pl.ds(pl.multiple_of(me * rows, 8), rows), :], local_sem)
        mine.start()
        sends = []
        peers = []
        for r in range(1, NDEV):
            px = 1 - x if (r >> 2) & 1 else x
            py = 1 - y if (r >> 1) & 1 else y
            pc = 1 - c if r & 1 else c
            peers.append((px, py, pc))
            cp = pltpu.make_async_remote_copy(
                src_ref=p_ref, dst_ref=out_ref.at[pl.ds(pl.multiple_of(me * rows, 8), rows), :],
                send_sem=send_sems.at[r - 1], recv_sem=recv_sems.at[r - 1], device_id=(px, py, pc),
                device_id_type=MESH)
            cp.start()
            sends.append(cp)
        for r, (px, py, pc) in enumerate(peers):
            them = 4 * px + 2 * py + pc
            pltpu.make_async_remote_copy(
                src_ref=p_ref, dst_ref=out_ref.at[pl.ds(pl.multiple_of(them * rows, 8), rows), :],
                send_sem=send_sems.at[r], recv_sem=recv_sems.at[r], device_id=(px, py, pc),
                device_id_type=MESH).wait_recv()
        for cp in sends:
            cp.wait_send()
        mine.wait()

    return pl.pallas_call(
        body, name="ag_small",
        out_shape=jax.ShapeDtypeStruct((NDEV * rows, piece.shape[1]), piece.dtype),
        in_specs=[pl.BlockSpec(memory_space=pltpu.VMEM)],
        out_specs=pl.BlockSpec(memory_space=pl.ANY),
        scratch_shapes=[pltpu.SemaphoreType.DMA((7,)), pltpu.SemaphoreType.DMA((7,)), pltpu.SemaphoreType.DMA],
    )(piece)


def _row_block(rows, target=512, mult=8):
    b = min(rows, target) // mult * mult
    while rows % b:
        b -= mult
    return b


def _sum_arrays(arrs, name, narrow=None, target=464):
    rows, cols = arrs[0].shape
    br = _row_block(rows, target, 16)
    n = len(arrs)

    def body(*refs):
        acc = refs[0][...].astype(F32)
        for r in refs[1:n]:
            acc = acc + r[...].astype(F32)
        refs[n][...] = acc
        if narrow is not None:
            refs[n + 1][...] = acc.astype(narrow)

    spec = pl.BlockSpec((br, cols), lambda i: (i, 0))
    shape = jax.ShapeDtypeStruct((rows, cols), F32)
    if narrow is None:
        out_shape, out_specs = shape, spec
    else:
        out_shape, out_specs = (shape, jax.ShapeDtypeStruct((rows, cols), narrow)), (spec, spec)
    return pl.pallas_call(
        body, name=name, grid=(rows // br,), out_shape=out_shape,
        in_specs=[spec] * n, out_specs=out_specs, compiler_params=_cparams(1),
    )(*arrs)


def _adamw(w, g, m, v, name):
    rows, cols = w.shape
    br = _row_block(rows, 256)

    def body(w_ref, g_ref, m_ref, v_ref, d_ref, nm_ref, nv_ref):
        g_ = g_ref[...]
        m_ = ADAM_B1 * m_ref[...] + (1.0 - ADAM_B1) * g_
        v_ = ADAM_B2 * v_ref[...] + (1.0 - ADAM_B2) * (g_ * g_)
        m_hat = m_ / (1.0 - ADAM_B1 ** ADAM_STEP)
        v_hat = v_ / (1.0 - ADAM_B2 ** ADAM_STEP)
        d_ref[...] = -ADAM_LR * (m_hat / (jnp.sqrt(v_hat) + ADAM_EPS) + ADAM_WD * w_ref[...])
        nm_ref[...] = m_
        nv_ref[...] = v_

    spec = pl.BlockSpec((br, cols), lambda i: (i, 0))
    shape = jax.ShapeDtypeStruct((rows, cols), F32)
    return pl.pallas_call(
        body, name=name, grid=(rows // br,), out_shape=(shape, shape, shape),
        in_specs=[spec] * 4, out_specs=(spec, spec, spec), compiler_params=_cparams(1),
    )(w, g, m, v)


_CHIP_FLIPS = (2, 3, 1, 0)


def _grad_matmul(lhs, rhs, name):
    s, r = lhs.shape
    k = rhs.shape[1]
    tm = min(TM, s)

    def body(l_ref, r_ref, o_ref):
        @pl.when(pl.program_id(0) == 0)
        def _():
            o_ref[...] = jnp.zeros_like(o_ref)

        o_ref[:, pl.ds(0, k)] += _tn(l_ref[...].astype(_MXU), r_ref[...].astype(_MXU))

    return pl.pallas_call(
        body, name=name, grid=(s // tm,),
        out_shape=jax.ShapeDtypeStruct((r, D), F32),
        in_specs=[pl.BlockSpec((tm, r), lambda i: (i, 0)), pl.BlockSpec((tm, k), lambda i: (i, 0))],
        out_specs=pl.BlockSpec((r, D), lambda i: (0, 0)),
        compiler_params=_cparams(1),
    )(lhs, rhs)


def _grad_matmul_rs(lhs, rhs, name, rows, chips_per_block=1, extras=(), narrow=None):
    s, r8 = lhs.shape
    k = rhs.shape[1]
    tm = min(TM, s)
    nt = s // tm
    cpb = chips_per_block
    nblk = 4 // cpb
    nx = len(extras)
    ers = [e.shape[0] // NDEV for e in extras]
    er = sum(ers)
    srows = rows + er
    brows = 2 * cpb * srows
    wire = F32 if narrow is None else narrow
    mid = min(nt - 1, max(1, nt // 4))

    def flip_of(p):
        return jnp.where(p == 0, 2, jnp.where(p == 1, 3, jnp.where(p == 2, 1, 0)))

    def block_col(b):
        x, y, _ = _my_pos()
        mine = 2 * x + y
        if cpb == 1:
            return mine ^ flip_of(b)
        return jnp.where(b == 0, (mine >> 1) ^ 1, mine >> 1)

    def body(*refs):
        l_ref, r_ref = refs[:2]
        x_refs = refs[2:2 + nx]
        town_ref, lici_ref, ld2d_ref, acc, stage, send_buf, dsend, drecv, isend, irecv, xsem = refs[2 + nx:]
        b = pl.program_id(0)
        i = pl.program_id(1)
        x, y, c = _my_pos()
        mine = 2 * x + y
        sibling = (x, y, 1 - c)

        def chip_at(p):
            return mine ^ _CHIP_FLIPS[p]

        def slab_rows(p, parity):
            within = 0 if cpb == 1 else (chip_at(p) & 1) * 2
            return pl.ds(pl.multiple_of((within + parity) * srows, 8), srows)

        def push(p, slot):
            return pltpu.make_async_remote_copy(
                src_ref=acc.at[slot, slab_rows(p, 1 - c), :], dst_ref=ld2d_ref.at[p],
                send_sem=dsend.at[p], recv_sem=drecv.at[p], device_id=sibling, device_id_type=MESH)

        def ici(p):
            ch = chip_at(p)
            return pltpu.make_async_remote_copy(
                src_ref=send_buf.at[p % 2], dst_ref=lici_ref.at[p],
                send_sem=isend.at[p], recv_sem=irecv.at[p], device_id=(ch >> 1, ch & 1, c), device_id_type=MESH)

        def extra_loads(p, slot):
            copies = []
            within = 0 if cpb == 1 else (chip_at(p) & 1) * 2
            for parity in range(2):
                off = rows
                for n, (x_ref, e) in enumerate(zip(x_refs, ers)):
                    src = x_ref.at[pl.ds(pl.multiple_of((2 * chip_at(p) + parity) * e, 8), e), :]
                    dst = acc.at[slot, pl.ds(pl.multiple_of((within + parity) * srows + off, 8), e), :]
                    copies.append(pltpu.make_async_copy(src, dst, xsem.at[(p * 2 + parity) * nx + n]))
                    off += e
            return copies

        def combine(p, slot):
            push(p, slot).wait_recv()
            pltpu.sync_copy(ld2d_ref.at[p], stage)
            total = acc[slot, slab_rows(p, c), :] + stage[...]
            if p == 3:
                stage[...] = total
                pltpu.sync_copy(stage, town_ref)
            else:
                if p == 2:
                    ici(0).wait_send()
                send_buf[p % 2] = total.astype(wire)
                ici(p).start()

        for bb in range(nblk):
            slot = bb % 2
            positions = list(range(bb * cpb, (bb + 1) * cpb))

            @pl.when(jnp.logical_and(b == bb, i == 0))
            def _(bb=bb, slot=slot, positions=positions):
                if bb >= 2:
                    for p in range((bb - 2) * cpb, (bb - 1) * cpb):
                        push(p, slot).wait_send()
                for q in range(2 * cpb):
                    acc[slot, pl.ds(q * srows, rows), :] = jnp.zeros((rows, D), F32)
                for p in positions:
                    for cp in extra_loads(p, slot):
                        cp.start()

            if bb >= 1:
                @pl.when(jnp.logical_and(b == bb, i == mid))
                def _(bb=bb):
                    for p in range((bb - 1) * cpb, bb * cpb):
                        combine(p, (bb - 1) % 2)

        res = _tn(l_ref[...].astype(_MXU), r_ref[...].astype(_MXU))
        slot_now = b % 2
        for q in range(2 * cpb):
            acc[slot_now, pl.ds(q * srows, rows), pl.ds(0, k)] += res[q * rows:(q + 1) * rows, :]

        for bb in range(nblk):
            slot = bb % 2
            positions = list(range(bb * cpb, (bb + 1) * cpb))

            @pl.when(jnp.logical_and(b == bb, i == nt - 1))
            def _(bb=bb, slot=slot, positions=positions):
                for p in positions:
                    for cp in extra_loads(p, slot):
                        cp.wait()
                for p in positions:
                    push(p, slot).start()
                if bb == nblk - 1:
                    for p in positions:
                        combine(p, slot)
                    for p in range(max(0, (nblk - 2) * cpb), 4):
                        push(p, slot).wait_send()
                    for p in range(1, 3):
                        ici(p).wait_send()
                    for p in range(3):
                        ici(p).wait_recv()

    in_specs = [pl.BlockSpec((tm, 2 * cpb * rows), lambda b, i: (i, block_col(b))),
                pl.BlockSpec((tm, k), lambda b, i: (i, 0))]
    any_spec = pl.BlockSpec(memory_space=pl.ANY)
    in_specs += [any_spec] * nx
    args = [lhs, rhs, *extras]
    return pl.pallas_call(
        body, name=name, grid=(nblk, nt),
        out_shape=(jax.ShapeDtypeStruct((srows, D), F32), jax.ShapeDtypeStruct((3, srows, D), wire),
                   jax.ShapeDtypeStruct((4, srows, D), F32)),
        in_specs=in_specs, out_specs=(any_spec, any_spec, any_spec),
        scratch_shapes=[pltpu.VMEM((2, brows, D), F32), pltpu.VMEM((srows, D), F32), pltpu.VMEM((2, srows, D), wire),
                        pltpu.SemaphoreType.DMA((4,)), pltpu.SemaphoreType.DMA((4,)), pltpu.SemaphoreType.DMA((3,)),
                        pltpu.SemaphoreType.DMA((3,)), pltpu.SemaphoreType.DMA((max(1, 8 * nx),))],
        compiler_params=_cparams(2),
    )(*args)


def _inproj_fwd(x, g1, gw, own_second):
    s = x.shape[0]
    tm = min(TM, s)
    nt = s // tm
    nchunk = 4
    cw = NIN // nchunk
    rows2, cols2 = own_second.shape

    def body(x_ref, g1_ref, gw_ref, own_ref, u_ref, z_ref, gw2_ref, w_vmem, stage, sems, send_sems, recv_sems,
             local_sem):
        i = pl.program_id(0)
        send_mine, pass_on, finish = _gather_phases(own_ref, gw2_ref, stage, send_sems, recv_sems, local_sem)

        @pl.when(i == 0)
        def _():
            send_mine()
            _load_weights(gw_ref, [("win", w_vmem, D)], sems)

        @pl.when(i == nt // 2)
        def _():
            pass_on((0, 1))

        @pl.when(i == (7 * nt) // 8)
        def _():
            pass_on((2,))

        xv = x_ref[...]
        inv = lax.rsqrt(jnp.mean(xv * xv, axis=-1, keepdims=True) + EPS)
        u = (xv * inv * g1_ref[...]).astype(_MXU)
        u_ref[...] = u
        for ch in range(nchunk):
            z_ref[:, pl.ds(ch * cw, cw)] = _nt(u, w_vmem[pl.ds(ch * cw, cw), :])

        @pl.when(i == nt - 1)
        def _():
            finish()

    any_spec = pl.BlockSpec(memory_space=pl.ANY)
    return pl.pallas_call(
        body, name="inproj_fwd", grid=(nt,),
        out_shape=(jax.ShapeDtypeStruct((s, D), _MXU), jax.ShapeDtypeStruct((s, NIN), F32),
                   jax.ShapeDtypeStruct((NDEV, rows2, cols2), own_second.dtype)),
        in_specs=[pl.BlockSpec((tm, D), lambda i: (i, 0)), pl.BlockSpec((1, D), lambda i: (0, 0)), any_spec, any_spec],
        out_specs=(pl.BlockSpec((tm, D), lambda i: (i, 0)), pl.BlockSpec((tm, NIN), lambda i: (i, 0)), any_spec),
        scratch_shapes=[pltpu.VMEM((NIN, D), _MXU), pltpu.VMEM((rows2, cols2), own_second.dtype),
                        pltpu.SemaphoreType.DMA((NDEV,)), pltpu.SemaphoreType.DMA((7,)), pltpu.SemaphoreType.DMA((7,)),
                        pltpu.SemaphoreType.DMA],
        compiler_params=_cparams(1),
    )(x, g1, gw, own_second)


def _pool_tile(pbuf, t0, tm, pw_ref, scale_ref):
    t = t0 + lax.broadcasted_iota(jnp.int32, (tm, GD), 0)
    pooled, mixed_pre = [], []
    for g, w in enumerate(WINDOWS):
        cs = pl.ds(g * GD, GD)
        cur = pbuf[pl.ds(HALO, tm), cs]
        acc = cur
        for d in range(1, w):
            acc = acc + pbuf[pl.ds(HALO - d, tm), cs]
        cnt = jnp.minimum(t + 1, w).astype(F32)
        pg = acc / cnt - cur
        pooled.append(pg)
        mixed_pre.append(_nn(pg.astype(_MXU), pw_ref[g]))
    return pooled, mixed_pre


def _lru_gates_head(hh, lbuf, start, tm, cw_ref, cb_ref, wrg_ref, brg_ref, wig_ref, big_ref, sp):
    cs = pl.ds(hh * HD, HD)
    xc = cb_ref[:, cs] + cw_ref[pl.ds(CONV - 1, 1), cs] * lbuf[pl.ds(HALO, tm), cs]
    for k in range(CONV - 1):
        xc = xc + cw_ref[pl.ds(k, 1), cs] * lbuf[pl.ds(HALO - (CONV - 1) + k, tm), cs]
    xcm = xc.astype(_MXU)
    r = _sigmoid(_nn(xcm, wrg_ref[hh]) + brg_ref[pl.ds(hh, 1), :])
    ig = _sigmoid(_nn(xcm, wig_ref[hh]) + big_ref[pl.ds(hh, 1), :])
    a = jnp.exp(-LRU_C * r * sp[:, hh * HD:(hh + 1) * HD])
    one_m = 1.0 - a * a
    live = jnp.logical_and(one_m > 0.0, jnp.logical_not(start))
    inv_mult = lax.rsqrt(jnp.where(live, one_m, 1.0))
    mult = jnp.where(live, one_m * inv_mult, jnp.where(start, 1.0, 0.0))
    return xc, r, ig, a, live, inv_mult, mult


def _seg_layout(tm):
    seg = tm // 8
    return seg, seg + 8


def _to_segments(dst_ref, hh, val, tm):
    seg, pitch = _seg_layout(tm)
    for s in range(8):
        dst_ref[hh, pl.ds(s * pitch, seg), :] = val[s * seg:(s + 1) * seg, :]


def _from_segments(src_ref, hh, tm):
    seg, pitch = _seg_layout(tm)
    return jnp.concatenate([src_ref[hh, pl.ds(s * pitch, seg), :] for s in range(8)], axis=0)


def _segment_scan(a_ref, b_ref, out_ref, hk, pk, carry_ref, tm, reverse):
    seg, pitch = _seg_layout(tm)
    row = lax.broadcasted_iota(jnp.int32, (8, HD), 0)
    order = range(seg - 1, -1, -1) if reverse else range(seg)
    for hh in range(HEADS):
        cs = pl.ds(hh * HD, HD)
        if reverse:
            a0 = a_ref[hh, pl.ds(0, 8, stride=pitch), :]
            a_wrap = jnp.where(row <= 6, pltpu.roll(a0, 7, 0), 1.0)
        hv = jnp.zeros((8, HD), F32)
        pv = jnp.ones((8, HD), F32)
        for k in order:
            if not reverse:
                av = a_ref[hh, pl.ds(k, 8, stride=pitch), :]
            elif k + 1 < seg:
                av = a_ref[hh, pl.ds(k + 1, 8, stride=pitch), :]
            else:
                av = a_wrap
            hv = av * hv + b_ref[hh, pl.ds(k, 8, stride=pitch), :]
            pv = av * pv
            hk[hh, pl.ds(8 * k, 8), :] = hv
            pk[hh, pl.ds(8 * k, 8), :] = pv
        for d in (1, 2, 4):
            if reverse:
                keep, sh = row < 8 - d, 8 - d
            else:
                keep, sh = row >= d, d
            hv = hv + pv * jnp.where(keep, pltpu.roll(hv, sh, 0), 0.0)
            pv = pv * jnp.where(keep, pltpu.roll(pv, sh, 0), 1.0)
        cin = carry_ref[:, cs]
        ends = hv + pv * cin
        if reverse:
            enter = jnp.where(row <= 6, pltpu.roll(ends, 7, 0), cin)
            carry_ref[:, cs] = jnp.broadcast_to((a0 * ends)[0:1, :], (8, HD))
        else:
            enter = jnp.where(row >= 1, pltpu.roll(ends, 1, 0), cin)
            carry_ref[:, cs] = jnp.broadcast_to(ends[7:8, :], (8, HD))
        for k in range(seg):
            out_ref[hh, pl.ds(k, 8, stride=pitch), :] = hk[hh, pl.ds(8 * k, 8), :] + pk[hh, pl.ds(8 * k, 8), :] * enter


def _mixer_fwd(z, x, gw, small):
    s = x.shape[0]
    tm = min(TM_SEQ, s)
    (pool_w, pool_scale, conv_w, conv_b, w_rg, b_rg, w_ig, b_ig, lam, b_gate) = small

    def body(z_ref, x_ref, gw_ref, pw_ref, ps_ref, cw_ref, cb_ref, wrg_ref, brg_ref, wig_ref, big_ref, lam_ref,
             bg_ref, h_ref, yl_ref, mg_ref, yp_ref, yr_ref, h1_ref,
             pprojT, lru_w, wout_w, pbuf, lbuf, a_s, b_s, h_s, hk, pk, hcar, sems):
        i = pl.program_id(0)
        t0 = i * tm

        @pl.when(i == 0)
        def _():
            _load_weights(gw_ref, [("pproj", pprojT, PW), ("lru", lru_w, D), ("wout", wout_w, D)], sems)
            pbuf[pl.ds(0, HALO), :] = jnp.zeros((HALO, PW), F32)
            lbuf[pl.ds(0, HALO), :] = jnp.zeros((HALO, D), F32)
            hcar[...] = jnp.zeros_like(hcar)

        pbuf[pl.ds(HALO, tm), :] = z_ref[:, pl.ds(0, PW)]
        _, mixed_pre = _pool_tile(pbuf, t0, tm, pw_ref, ps_ref)
        mixed = jnp.concatenate(mixed_pre, axis=1) * ps_ref[...]
        y_pool = _nt(mixed.astype(_MXU), pprojT[...])
        pbuf[pl.ds(0, HALO), :] = pbuf[pl.ds(tm, HALO), :]

        lbuf[pl.ds(HALO, tm), :] = z_ref[:, pl.ds(PW, D)]
        sp, _ = _softplus_neg(lam_ref[...])
        start = (t0 + lax.broadcasted_iota(jnp.int32, (tm, HD), 0)) == 0
        for hh in range(HEADS):
            xc, r, ig, a, _, _, mult = _lru_gates_head(hh, lbuf, start, tm, cw_ref, cb_ref, wrg_ref, brg_ref,
                                                       wig_ref, big_ref, sp)
            _to_segments(a_s, hh, a, tm)
            _to_segments(b_s, hh, mult * ig * xc, tm)
        lbuf[pl.ds(0, HALO), :] = lbuf[pl.ds(tm, HALO), :]
        _segment_scan(a_s, b_s, h_s, hk, pk, hcar, tm, reverse=False)
        for hh in range(HEADS):
            h_ref[:, pl.ds(hh * HD, HD)] = _from_segments(h_s, hh, tm)
        gel, _ = _gelu_and_grad(z_ref[:, pl.ds(PW + D, D)])
        yl = (h_ref[...] * gel).astype(_MXU)
        yl_ref[...] = yl
        y_lru = _nn(yl, lru_w[...])

        g0 = _sigmoid(z_ref[:, pl.ds(PW + 2 * D, D)] + bg_ref[pl.ds(0, 1), :])
        g1 = _sigmoid(z_ref[:, pl.ds(PW + 3 * D, D)] + bg_ref[pl.ds(1, 1), :])
        merged = (g0 * y_pool + g1 * y_lru).astype(_MXU)
        mg_ref[...] = merged
        yp_ref[...] = y_pool.astype(_MXU)
        yr_ref[...] = y_lru.astype(_MXU)
        h1_ref[...] = x_ref[...] + _nn(merged, wout_w[...])

    tok = lambda w, dt: jax.ShapeDtypeStruct((s, w), dt)
    tspec = lambda w: pl.BlockSpec((tm, w), lambda i: (i, 0))
    full = lambda a: pl.BlockSpec(a.shape, lambda i: (0,) * a.ndim)
    seg_buf = pltpu.VMEM((HEADS, 8 * _seg_layout(tm)[1], HD), F32)
    return pl.pallas_call(
        body, name="mixer_fwd", grid=(s // tm,),
        out_shape=(tok(D, F32), tok(D, _MXU), tok(D, _MXU), tok(D, _MXU), tok(D, _MXU), tok(D, F32)),
        in_specs=[tspec(NIN), tspec(D), pl.BlockSpec(memory_space=pl.ANY)] + [full(a) for a in small],
        out_specs=(tspec(D),) * 6,
        scratch_shapes=[pltpu.VMEM((D, PW), _MXU), pltpu.VMEM((D, D), _MXU), pltpu.VMEM((D, D), _MXU),
                        pltpu.VMEM((tm + HALO, PW), F32), pltpu.VMEM((tm + HALO, D), F32),
                        seg_buf, seg_buf, seg_buf, pltpu.VMEM((HEADS, tm, HD), F32), pltpu.VMEM((HEADS, tm, HD), F32),
                        pltpu.VMEM((8, D), F32), pltpu.SemaphoreType.DMA((3 * NDEV,))],
        compiler_params=_cparams(1),
    )(z, x, gw, *small)


def _ffn_fwd(h1, g2, gw):
    s = h1.shape[0]
    tm = min(TM, s)
    half = FF // 2

    def body(h1_ref, g2_ref, gw_ref, v_ref, gf_ref, uf_ref, h2_ref, wffnT, wffo, sems):
        @pl.when(pl.program_id(0) == 0)
        def _():
            _load_weights(gw_ref, [("wffn", wffnT, D), ("wffo", wffo, D)], sems)

        hv = h1_ref[...]
        inv = lax.rsqrt(jnp.mean(hv * hv, axis=-1, keepdims=True) + EPS)
        v = (hv * inv * g2_ref[...]).astype(_MXU)
        v_ref[...] = v
        acc = hv
        for ch in range(2):
            cs = pl.ds(ch * half, half)
            gf = _nt(v, wffnT[pl.ds(ch * half, half), :]).astype(_MXU)
            uf = _nt(v, wffnT[pl.ds(FF + ch * half, half), :]).astype(_MXU)
            gf_ref[:, cs] = gf
            uf_ref[:, cs] = uf
            gf32 = gf.astype(F32)
            act = (gf32 * _sigmoid(gf32) * uf.astype(F32)).astype(_MXU)
            acc = acc + _nn(act, wffo[pl.ds(ch * half, half), :])
        h2_ref[...] = acc

    tspec = lambda w: pl.BlockSpec((tm, w), lambda i: (i, 0))
    return pl.pallas_call(
        body, name="ffn_fwd", grid=(s // tm,),
        out_shape=(jax.ShapeDtypeStruct((s, D), _MXU), jax.ShapeDtypeStruct((s, FF), _MXU),
                   jax.ShapeDtypeStruct((s, FF), _MXU), jax.ShapeDtypeStruct((s, D), F32)),
        in_specs=[tspec(D), pl.BlockSpec((1, D), lambda i: (0, 0)), pl.BlockSpec(memory_space=pl.ANY)],
        out_specs=(tspec(D), tspec(FF), tspec(FF), tspec(D)),
        scratch_shapes=[pltpu.VMEM((2 * FF, D), _MXU), pltpu.VMEM((FF, D), _MXU), pltpu.SemaphoreType.DMA((2 * NDEV,))],
        compiler_params=_cparams(1),
    )(h1, g2, gw)


def _rms_bwd(dy, xn, inv, g):
    dg = jnp.sum(dy * xn, axis=0, keepdims=True)
    dxn = dy * g
    dx = inv * (dxn - xn * jnp.mean(dxn * xn, axis=-1, keepdims=True))
    return dx, dg


def _ple_loss_fwd_bwd(h2, p, target, g3, gfin, gw):
    s = h2.shape[0]
    tm = min(TM, s)

    def body(h2_ref, p_ref, t_ref, g3_ref, gf_ref, gw_ref,
             dh2_ref, n3_ref, dpg_ref, de_ref, loss_ref, dg3_ref, dgf_ref, wpg, pleT, sems):
        i = pl.program_id(0)

        @pl.when(i == 0)
        def _():
            _load_weights(gw_ref, [("wpg", wpg, D), ("ple", pleT, PLE)], sems)
            loss_ref[...] = jnp.zeros_like(loss_ref)
            dg3_ref[...] = jnp.zeros_like(dg3_ref)
            dgf_ref[...] = jnp.zeros_like(dgf_ref)

        hv = h2_ref[...]
        inv3 = lax.rsqrt(jnp.mean(hv * hv, axis=-1, keepdims=True) + EPS)
        xn3 = hv * inv3
        n3 = (xn3 * g3_ref[...]).astype(_MXU)
        n3_ref[...] = n3
        pg = _sigmoid(_nn(n3, wpg[...]))
        e = _nt(p_ref[...].astype(_MXU), pleT[...])
        h3 = hv + pg * e
        invf = lax.rsqrt(jnp.mean(h3 * h3, axis=-1, keepdims=True) + EPS)
        xf = h3 * invf
        diff = xf * gf_ref[...] - t_ref[...]
        loss_ref[...] += jnp.sum(diff * diff) * (0.5 / D)
        dh3, dgf = _rms_bwd(diff * (1.0 / D), xf, invf, gf_ref[...])
        dgf_ref[...] += dgf
        de_ref[...] = (dh3 * pg).astype(_MXU)
        dpg = (dh3 * e * pg * (1.0 - pg)).astype(_MXU)
        dpg_ref[...] = dpg
        dn3 = _nt(dpg, wpg[...])
        dx3, dg3 = _rms_bwd(dn3, xn3, inv3, g3_ref[...])
        dg3_ref[...] += dg3
        dh2_ref[...] = dh3 + dx3

    tspec = lambda w: pl.BlockSpec((tm, w), lambda i: (i, 0))
    vec = pl.BlockSpec((1, D), lambda i: (0, 0))
    tok = lambda w, dt: jax.ShapeDtypeStruct((s, w), dt)
    return pl.pallas_call(
        body, name="ple_loss", grid=(s // tm,),
        out_shape=(tok(D, F32), tok(D, _MXU), tok(D, _MXU), tok(D, _MXU), jax.ShapeDtypeStruct((8, 128), F32),
                   jax.ShapeDtypeStruct((1, D), F32), jax.ShapeDtypeStruct((1, D), F32)),
        in_specs=[tspec(D), tspec(PLE), tspec(D), vec, vec, pl.BlockSpec(memory_space=pl.ANY)],
        out_specs=(tspec(D), tspec(D), tspec(D), tspec(D), pl.BlockSpec((8, 128), lambda i: (0, 0)), vec, vec),
        scratch_shapes=[pltpu.VMEM((D, D), _MXU), pltpu.VMEM((D, PLE), _MXU), pltpu.SemaphoreType.DMA((2 * NDEV,))],
        compiler_params=_cparams(1),
    )(h2, p, target, g3, gfin, gw)


def _ffn_bwd_hidden(dh2, gf, uf, gw):
    s = dh2.shape[0]
    tm = min(TM, s)
    half = FF // 2

    def body(dh2_ref, gf_ref, uf_ref, gw_ref, dff_ref, act_ref, wffo, sems):
        @pl.when(pl.program_id(0) == 0)
        def _():
            _load_weights(gw_ref, [("wffo", wffo, D)], sems)

        dm = dh2_ref[...].astype(_MXU)
        for ch in range(2):
            cs = pl.ds(ch * half, half)
            dact = _nt(dm, wffo[pl.ds(ch * half, half), :])
            gfv = gf_ref[:, cs].astype(F32)
            ufv = uf_ref[:, cs].astype(F32)
            sg = _sigmoid(gfv)
            silu = gfv * sg
            act_ref[:, cs] = (silu * ufv).astype(_MXU)
            dff_ref[:, pl.ds(ch * half, half)] = (dact * ufv * (sg * (1.0 + gfv * (1.0 - sg)))).astype(_MXU)
            dff_ref[:, pl.ds(FF + ch * half, half)] = (dact * silu).astype(_MXU)

    tspec = lambda w: pl.BlockSpec((tm, w), lambda i: (i, 0))
    return pl.pallas_call(
        body, name="ffn_bwd_hidden", grid=(s // tm,),
        out_shape=(jax.ShapeDtypeStruct((s, 2 * FF), _MXU), jax.ShapeDtypeStruct((s, FF), _MXU)),
        in_specs=[tspec(D), tspec(FF), tspec(FF), pl.BlockSpec(memory_space=pl.ANY)],
        out_specs=(tspec(2 * FF), tspec(FF)),
        scratch_shapes=[pltpu.VMEM((FF, D), _MXU), pltpu.SemaphoreType.DMA((NDEV,))],
        compiler_params=_cparams(1),
    )(dh2, gf, uf, gw)


def _proj_norm_bwd(dy, x, dres, g, gw, slab, width, name):
    s = x.shape[0]
    tm = min(TM, s)

    def body(dy_ref, x_ref, dr_ref, g_ref, gw_ref, dx_ref, dg_ref, wT, sems):
        @pl.when(pl.program_id(0) == 0)
        def _():
            _load_weights(gw_ref, [(slab, wT, D)], sems)
            dg_ref[...] = jnp.zeros_like(dg_ref)

        dv = _nn(dy_ref[...], wT[...])
        xv = x_ref[...]
        inv = lax.rsqrt(jnp.mean(xv * xv, axis=-1, keepdims=True) + EPS)
        dx, dg = _rms_bwd(dv, xv * inv, inv, g_ref[...])
        dg_ref[...] += dg
        dx_ref[...] = dr_ref[...] + dx

    tspec = lambda w: pl.BlockSpec((tm, w), lambda i: (i, 0))
    vec = pl.BlockSpec((1, D), lambda i: (0, 0))
    return pl.pallas_call(
        body, name=name, grid=(s // tm,),
        out_shape=(jax.ShapeDtypeStruct((s, D), F32), jax.ShapeDtypeStruct((1, D), F32)),
        in_specs=[tspec(width), tspec(D), tspec(D), vec, pl.BlockSpec(memory_space=pl.ANY)],
        out_specs=(tspec(D), vec),
        scratch_shapes=[pltpu.VMEM((width, D), _MXU), pltpu.SemaphoreType.DMA((NDEV,))],
        compiler_params=_cparams(1),
    )(dy, x, dres, g, gw)


def _mixer_bwd(dh1, z, h, y_pool, y_lru, gw, small):
    s = dh1.shape[0]
    tm = min(TM_SEQ, s)
    nt = s // tm
    (pool_w, pool_scale, conv_w, conv_b, w_rg, b_rg, w_ig, b_ig, lam, b_gate) = small

    def body(dh1_ref, z_ref, zp_ref, h_ref, hp_ref, yp_ref, yr_ref, gw_ref,
             pw_ref, ps_ref, cw_ref, cb_ref, wrg_ref, brg_ref, wig_ref, big_ref, lam_ref, bg_ref,
             dz_ref, dyr_ref, dyp_ref, mx_ref,
             gbg_ref, glam_ref, gbrg_ref, gbig_ref, gcb_ref, gcw_ref, gps_ref, gpw_ref, gwrg_ref, gwig_ref,
             pprojT, lru_w, wout_w, pbuf, lbuf, hbuf, qbuf, xbuf, a_s, g_s, dh_s, hk, pk, r_s, ig_s, xc_s, mu_s, f_s,
             dcar, sems):
        step = pl.program_id(0)
        i = nt - 1 - step
        t0 = i * tm

        @pl.when(step == 0)
        def _():
            _load_weights(gw_ref, [("pproj", pprojT, PW), ("lru", lru_w, D), ("wout", wout_w, D)], sems)
            for ref in (gbg_ref, glam_ref, gbrg_ref, gbig_ref, gcb_ref, gcw_ref, gps_ref, gpw_ref, gwrg_ref, gwig_ref):
                ref[...] = jnp.zeros_like(ref)
            qbuf[pl.ds(tm, HALO), :] = jnp.zeros((HALO, PW), F32)
            xbuf[pl.ds(tm, 8), :] = jnp.zeros((8, D), F32)
            dcar[...] = jnp.zeros_like(dcar)

        first = i == 0
        zprev = jnp.where(first, 0.0, zp_ref[...])
        hprev = jnp.where(first, 0.0, hp_ref[...])

        d_merged = _nt(dh1_ref[...].astype(_MXU), wout_w[...])

        g0 = _sigmoid(z_ref[:, pl.ds(PW + 2 * D, D)] + bg_ref[pl.ds(0, 1), :])
        g1 = _sigmoid(z_ref[:, pl.ds(PW + 3 * D, D)] + bg_ref[pl.ds(1, 1), :])
        dz0 = d_merged * yp_ref[...].astype(F32) * g0 * (1.0 - g0)
        dz1 = d_merged * yr_ref[...].astype(F32) * g1 * (1.0 - g1)
        dz_ref[:, pl.ds(PW + 2 * D, D)] = dz0.astype(_MXU)
        dz_ref[:, pl.ds(PW + 3 * D, D)] = dz1.astype(_MXU)
        gbg_ref[pl.ds(0, 1), :] += jnp.sum(dz0, axis=0, keepdims=True)
        gbg_ref[pl.ds(1, 1), :] += jnp.sum(dz1, axis=0, keepdims=True)
        d_ypool = (d_merged * g0).astype(_MXU)
        d_ylru = (d_merged * g1).astype(_MXU)
        dyp_ref[...] = d_ypool
        dyr_ref[...] = d_ylru

        d_yl = _nt(d_ylru, lru_w[...])
        gel, dgel = _gelu_and_grad(z_ref[:, pl.ds(PW + D, D)])
        dz_ref[:, pl.ds(PW + D, D)] = (d_yl * h_ref[...] * dgel).astype(_MXU)
        g_full = d_yl * gel
        lbuf[pl.ds(0, HALO), :] = zprev[:, PW:PW + D]
        lbuf[pl.ds(HALO, tm), :] = z_ref[:, pl.ds(PW, D)]
        hbuf[pl.ds(0, 8), :] = hprev
        hbuf[pl.ds(8, tm), :] = h_ref[...]
        sp, sneg = _softplus_neg(lam_ref[...])
        start = (t0 + lax.broadcasted_iota(jnp.int32, (tm, HD), 0)) == 0
        for hh in range(HEADS):
            cs = pl.ds(hh * HD, HD)
            xc, r, ig, a, live, inv_mult, mult = _lru_gates_head(hh, lbuf, start, tm, cw_ref, cb_ref, wrg_ref,
                                                                 brg_ref, wig_ref, big_ref, sp)
            _to_segments(a_s, hh, a, tm)
            _to_segments(g_s, hh, g_full[:, hh * HD:(hh + 1) * HD], tm)
            r_s[:, cs] = r
            ig_s[:, cs] = ig
            xc_s[:, cs] = xc
            mu_s[:, cs] = mult
            f_s[:, cs] = jnp.where(live, -(a * a) * inv_mult, 0.0)
        _segment_scan(a_s, g_s, dh_s, hk, pk, dcar, tm, reverse=True)
        for hh in range(HEADS):
            cs = pl.ds(hh * HD, HD)
            a = _from_segments(a_s, hh, tm)
            r = r_s[:, cs]
            ig = ig_s[:, cs]
            xc = xc_s[:, cs]
            mult = mu_s[:, cs]
            dh = _from_segments(dh_s, hh, tm)
            d_mult = dh * ig * xc
            d_loga = dh * hbuf[pl.ds(7, tm), cs] * a + d_mult * f_s[:, cs]
            glam_ref[:, cs] += jnp.sum(d_loga * (LRU_C * r) * sneg[:, hh * HD:(hh + 1) * HD], axis=0, keepdims=True)
            d_rpre = d_loga * (-LRU_C * sp[:, hh * HD:(hh + 1) * HD]) * r * (1.0 - r)
            d_igpre = dh * mult * xc * ig * (1.0 - ig)
            gbrg_ref[pl.ds(hh, 1), :] += jnp.sum(d_rpre, axis=0, keepdims=True)
            gbig_ref[pl.ds(hh, 1), :] += jnp.sum(d_igpre, axis=0, keepdims=True)
            drm = d_rpre.astype(_MXU)
            dim = d_igpre.astype(_MXU)
            xcm = xc.astype(_MXU)
            gwrg_ref[hh] += _tn(xcm, drm)
            gwig_ref[hh] += _tn(xcm, dim)
            d_xc = dh * mult * ig + _nt(drm, wrg_ref[hh]) + _nt(dim, wig_ref[hh])
            gcb_ref[:, cs] += jnp.sum(d_xc, axis=0, keepdims=True)
            for k in range(CONV):
                gcw_ref[pl.ds(k, 1), cs] += jnp.sum(d_xc * lbuf[pl.ds(HALO - (CONV - 1) + k, tm), cs], axis=0,
                                                    keepdims=True)
            xbuf[pl.ds(0, tm), cs] = d_xc
        dzl = cw_ref[pl.ds(CONV - 1, 1), :] * xbuf[pl.ds(0, tm), :]
        for k in range(CONV - 1):
            dzl = dzl + cw_ref[pl.ds(k, 1), :] * xbuf[pl.ds(CONV - 1 - k, tm), :]
        dz_ref[:, pl.ds(PW, D)] = dzl.astype(_MXU)
        xbuf[pl.ds(tm, 8), :] = xbuf[pl.ds(0, 8), :]

        d_mixed = _nn(d_ypool, pprojT[...])
        pbuf[pl.ds(0, HALO), :] = zprev[:, 0:PW]
        pbuf[pl.ds(HALO, tm), :] = z_ref[:, pl.ds(0, PW)]
        pooled, mixed_pre = _pool_tile(pbuf, t0, tm, pw_ref, ps_ref)
        mp = jnp.concatenate(mixed_pre, axis=1)
        mx_ref[...] = (mp * ps_ref[...]).astype(_MXU)
        gps_ref[...] += jnp.sum(d_mixed * mp, axis=0, keepdims=True)
        d_mp = (d_mixed * ps_ref[...]).astype(_MXU)
        t = t0 + lax.broadcasted_iota(jnp.int32, (tm, GD), 0)
        d_pooled = []
        for g, w in enumerate(WINDOWS):
            dmg = d_mp[:, g * GD:(g + 1) * GD]
            gpw_ref[g] += _tn(pooled[g].astype(_MXU), dmg)
            dp = _nt(dmg, pw_ref[g])
            d_pooled.append(dp)
            qbuf[pl.ds(0, tm), pl.ds(g * GD, GD)] = dp / jnp.minimum(t + 1, w).astype(F32)
        for g, w in enumerate(WINDOWS):
            cs = pl.ds(g * GD, GD)
            acc = qbuf[pl.ds(0, tm), cs]
            for d in range(1, w):
                acc = acc + qbuf[pl.ds(d, tm), cs]
            dz_ref[:, cs] = (acc - d_pooled[g]).astype(_MXU)
        qbuf[pl.ds(tm, HALO), :] = qbuf[pl.ds(0, HALO), :]

    rev = lambda w: pl.BlockSpec((tm, w), lambda g: (nt - 1 - g, 0))
    prev = lambda rows, w: pl.BlockSpec((rows, w), lambda g: (jnp.maximum((nt - 1 - g) * (tm // rows) - 1, 0), 0))
    full = lambda a: pl.BlockSpec(a.shape, lambda g: (0,) * a.ndim)
    tok = lambda w, dt: jax.ShapeDtypeStruct((s, w), dt)
    acc_shapes = [(2, D), (1, D), (HEADS, HD), (HEADS, HD), (1, D), (CONV, D), (1, PW), (GROUPS, GD, GD),
                  (HEADS, HD, HD), (HEADS, HD, HD)]
    acc_specs = tuple(pl.BlockSpec(sh, lambda g, n=len(sh): (0,) * n) for sh in acc_shapes)
    seg_buf = pltpu.VMEM((HEADS, 8 * _seg_layout(tm)[1], HD), F32)
    tile_buf = pltpu.VMEM((tm, D), F32)
    return pl.pallas_call(
        body, name="mixer_bwd", grid=(nt,),
        out_shape=(tok(NIN, _MXU), tok(D, _MXU), tok(D, _MXU), tok(PW, _MXU))
        + tuple(jax.ShapeDtypeStruct(sh, F32) for sh in acc_shapes),
        in_specs=[rev(D), rev(NIN), prev(HALO, NIN), rev(D), prev(8, D), rev(D), rev(D),
                  pl.BlockSpec(memory_space=pl.ANY)] + [full(a) for a in small],
        out_specs=(rev(NIN), rev(D), rev(D), rev(PW)) + acc_specs,
        scratch_shapes=[pltpu.VMEM((D, PW), _MXU), pltpu.VMEM((D, D), _MXU), pltpu.VMEM((D, D), _MXU),
                        pltpu.VMEM((tm + HALO, PW), F32), pltpu.VMEM((tm + HALO, D), F32),
                        pltpu.VMEM((tm + 8, D), F32), pltpu.VMEM((tm + HALO, PW), F32), pltpu.VMEM((tm + 8, D), F32),
                        seg_buf, seg_buf, seg_buf, pltpu.VMEM((HEADS, tm, HD), F32), pltpu.VMEM((HEADS, tm, HD), F32),
                        tile_buf, tile_buf, tile_buf, tile_buf, tile_buf,
                        pltpu.VMEM((8, D), F32), pltpu.SemaphoreType.DMA((3 * NDEV,))],
        compiler_params=_cparams(1),
    )(dh1, z, z, h, h, y_pool, y_lru, gw, *small)


def _split3(a):
    hi = a.astype(jnp.bfloat16).astype(F32)
    mid = (a - hi).astype(jnp.bfloat16).astype(F32)
    lo = (a - hi - mid).astype(jnp.bfloat16).astype(F32)
    return jnp.stack([hi, mid, lo])


def _small_pack(parts):
    flat = jnp.concatenate([a.reshape(-1) for a in parts])
    return jnp.pad(flat, (0, NDEV * SMALL_ROWS * D - flat.shape[0])).reshape(NDEV * SMALL_ROWS, D)


def _small_unpack(packed, shapes):
    flat = packed.reshape(-1)
    out, o = [], 0
    for sh in shapes:
        n = math.prod(sh)
        out.append(flat[o:o + n].reshape(sh))
        o += n
    return out


def kernel(x, p, norm1_g, w_in, b_gate, pool_w, pool_scale, pool_proj, conv_w, conv_b, w_rg, b_rg, w_ig, b_ig, lru_lambda, lru_proj, w_out, norm2_g, w_ffn_in, w_ffn_out, ple_norm_g, w_ple_gate, w_ple_proj, final_g, loss_target, m_norm1_g, m_w_in, m_b_gate, m_pool_w, m_pool_scale, m_pool_proj, m_conv_w, m_conv_b, m_w_rg, m_b_rg, m_w_ig, m_b_ig, m_lru_lambda, m_lru_proj, m_w_out, m_norm2_g, m_w_ffn_in, m_w_ffn_out, m_ple_norm_g, m_w_ple_gate, m_w_ple_proj, m_final_g, v_norm1_g, v_w_in, v_b_gate, v_pool_w, v_pool_scale, v_pool_proj, v_conv_w, v_conv_b, v_w_rg, v_b_rg, v_w_ig, v_b_ig, v_lru_lambda, v_lru_proj, v_w_out, v_norm2_g, v_w_ffn_in, v_w_ffn_out, v_ple_norm_g, v_w_ple_gate, v_w_ple_proj, v_final_g):
    axes = ("x", "y", "c")
    me = 4 * lax.axis_index("x") + 2 * lax.axis_index("y") + lax.axis_index("c")
    x2 = x[0]
    p2 = p[0, 0]
    tgt = loss_target[0]

    n_small = (CONV + 2) * 128
    small_terms = _split3(jnp.concatenate([conv_w[0].reshape(-1), b_gate[0].reshape(-1)]))
    small_rows = jnp.pad(small_terms, ((0, 16 - 3), (0, D - n_small)))
    own_first = jnp.concatenate([w_in[0].T.astype(_MXU), small_rows.astype(_MXU)], axis=0)
    own_second = jnp.concatenate([
        w_ffn_in[0].T.astype(_MXU),
        jnp.pad(pool_proj[0].T, ((0, 0), (0, D - PW))).astype(_MXU),
        jnp.pad(w_ple_proj[0].T, ((0, 0), (0, D - PLE))).astype(_MXU),
        lru_proj[0].astype(_MXU), w_out[0].astype(_MXU), w_ffn_out[0].astype(_MXU), w_ple_gate[0].astype(_MXU),
    ], axis=0)
    gw_first = _all_gather_weights(own_first)
    off = W_OFF["f32s"][0]
    st = gw_first[:, off:off + 3, :n_small].astype(F32)
    sf = st[:, 0] + st[:, 1] + st[:, 2]
    conv_w_full = sf[:, :CONV * 128].reshape(NDEV, CONV, 128).transpose(1, 0, 2).reshape(CONV, D)
    b_gate_full = sf[:, CONV * 128:].reshape(NDEV, 2, 128).transpose(1, 0, 2).reshape(2, D)

    small = (pool_w[0].astype(_MXU), pool_scale, conv_w_full, conv_b, w_rg[0].astype(_MXU), b_rg[0],
             w_ig[0].astype(_MXU), b_ig[0], lru_lambda, b_gate_full)

    u, z, gw = _inproj_fwd(x2, norm1_g, gw_first, own_second)
    h, yl, merged, y_pool, y_lru, h1 = _mixer_fwd(z, x2, gw, small)
    v, gf, uf, h2 = _ffn_fwd(h1, norm2_g, gw)

    dh2, n3, dpg, de, loss_blk, g_ple_norm, g_final = _ple_loss_fwd_bwd(h2, p2, tgt, ple_norm_g, final_g.reshape(1, D), gw)
    loss = lax.psum(loss_blk[0, 0], axes)
    dff, act = _ffn_bwd_hidden(dh2, gf, uf, gw)
    dh1, g_norm2 = _proj_norm_bwd(dff, h1, dh2, norm2_g, gw, "wffn", 2 * FF, "ffn_bwd_in")
    (dz, d_ylru, d_ypool, mixed, g_bgate, g_lam, g_brg, g_big, g_convb, g_convw, g_pscale, g_poolw, g_wrg,
     g_wig) = _mixer_bwd(dh1, z, h, y_pool, y_lru, gw, small)
    grad_x, g_norm1 = _proj_norm_bwd(dz, x2, dh1, norm1_g, gw_first, "win", NIN, "inproj_bwd")

    small_shapes = [(1, D), (GROUPS, GD, GD), (1, PW), (1, D), (HEADS, HD, HD), (HEADS, HD), (HEADS, HD, HD),
                    (HEADS, HD), (1, D), (1, D), (1, D), (1, D), (2, D), (CONV, D)]
    small_part = _small_pack([g_norm1, g_poolw, g_pscale, g_convb, g_wrg, g_brg, g_wig, g_big, g_lam, g_norm2,
                              g_ple_norm, g_final, g_bgate, g_convw])
    riders = [_grad_matmul(n3, dpg, "grad_w_ple_gate"), _grad_matmul(yl, d_ylru, "grad_lru_proj"),
              _grad_matmul(merged, dh1, "grad_w_out"), _grad_matmul(d_ypool, mixed, "grad_pool_proj"),
              _grad_matmul(de, p2, "grad_w_ple_proj")]
    rs_wffn = _grad_matmul_rs(dff, v, "grad_w_ffn_in", 704, narrow=_MXU)
    rs_wffo = _grad_matmul_rs(act, dh2, "grad_w_ffn_out", 352, chips_per_block=2, extras=[small_part])
    rs_win = _grad_matmul_rs(dz, u, "grad_w_in", 576, extras=riders, narrow=_MXU)

    def reduced(parts, name):
        t_own, landed, _ = parts
        return _sum_arrays([t_own, landed[0], landed[1], landed[2]], "rs_sum_" + name)

    red_wffn = reduced(rs_wffn, "wffn")
    red_wffo = reduced(rs_wffo, "wffo")
    red_win = reduced(rs_win, "win")
    g_w_in = red_win[:576].T
    g_w_ffn_in = red_wffn.T
    g_w_ffn_out = red_wffo[:352]
    g_w_ple_gate, g_lru_proj, g_w_out = red_win[576:704], red_win[704:832], red_win[832:960]
    g_pool_proj = red_win[960:1088, :PW].T
    g_w_ple_proj = red_win[1088:1216, :PLE].T
    small_red = _all_gather_small(red_wffo[352:])
    (gs_norm1, gs_poolw, gs_pscale, gs_convb, gs_wrg, gs_brg, gs_wig, gs_big, gs_lam, gs_norm2, gs_ple_norm,
     gs_final, gs_bgate, gs_convw) = _small_unpack(small_red, small_shapes)
    g_b_gate = lax.dynamic_slice_in_dim(gs_bgate, me * 128, 128, axis=1)
    g_conv_w = lax.dynamic_slice_in_dim(gs_convw, me * 128, 128, axis=1)

    grads = {
        "norm1_g": gs_norm1, "w_in": g_w_in[None], "b_gate": g_b_gate[None], "pool_w": gs_poolw[None],
        "pool_scale": gs_pscale, "pool_proj": g_pool_proj[None], "conv_w": g_conv_w[None], "conv_b": gs_convb,
        "w_rg": gs_wrg[None], "b_rg": gs_brg[None], "w_ig": gs_wig[None], "b_ig": gs_big[None], "lru_lambda": gs_lam,
        "lru_proj": g_lru_proj[None], "w_out": g_w_out[None], "norm2_g": gs_norm2, "w_ffn_in": g_w_ffn_in[None],
        "w_ffn_out": g_w_ffn_out[None], "ple_norm_g": gs_ple_norm, "w_ple_gate": g_w_ple_gate[None],
        "w_ple_proj": g_w_ple_proj[None], "final_g": gs_final.reshape(D),
    }
    weights = dict(norm1_g=norm1_g, w_in=w_in, b_gate=b_gate, pool_w=pool_w, pool_scale=pool_scale, pool_proj=pool_proj,
                   conv_w=conv_w, conv_b=conv_b, w_rg=w_rg, b_rg=b_rg, w_ig=w_ig, b_ig=b_ig, lru_lambda=lru_lambda,
                   lru_proj=lru_proj, w_out=w_out, norm2_g=norm2_g, w_ffn_in=w_ffn_in, w_ffn_out=w_ffn_out,
                   ple_norm_g=ple_norm_g, w_ple_gate=w_ple_gate, w_ple_proj=w_ple_proj, final_g=final_g)
    moments_m = dict(norm1_g=m_norm1_g, w_in=m_w_in, b_gate=m_b_gate, pool_w=m_pool_w, pool_scale=m_pool_scale,
                     pool_proj=m_pool_proj, conv_w=m_conv_w, conv_b=m_conv_b, w_rg=m_w_rg, b_rg=m_b_rg, w_ig=m_w_ig,
                     b_ig=m_b_ig, lru_lambda=m_lru_lambda, lru_proj=m_lru_proj, w_out=m_w_out, norm2_g=m_norm2_g,
                     w_ffn_in=m_w_ffn_in, w_ffn_out=m_w_ffn_out, ple_norm_g=m_ple_norm_g, w_ple_gate=m_w_ple_gate,
                     w_ple_proj=m_w_ple_proj, final_g=m_final_g)
    moments_v = dict(norm1_g=v_norm1_g, w_in=v_w_in, b_gate=v_b_gate, pool_w=v_pool_w, pool_scale=v_pool_scale,
                     pool_proj=v_pool_proj, conv_w=v_conv_w, conv_b=v_conv_b, w_rg=v_w_rg, b_rg=v_b_rg, w_ig=v_w_ig,
                     b_ig=v_b_ig, lru_lambda=v_lru_lambda, lru_proj=v_lru_proj, w_out=v_w_out, norm2_g=v_norm2_g,
                     w_ffn_in=v_w_ffn_in, w_ffn_out=v_w_ffn_out, ple_norm_g=v_ple_norm_g, w_ple_gate=v_w_ple_gate,
                     w_ple_proj=v_w_ple_proj, final_g=v_final_g)
    names = list(weights)
    big = ("w_in", "w_ffn_in", "w_ffn_out", "lru_proj", "w_out", "w_ple_gate", "pool_proj", "w_ple_proj")
    delta, new_m, new_v = {}, {}, {}
    for n in big:
        sh = weights[n].shape
        as2d = lambda a: a.reshape(sh[-2], sh[-1])
        d_, m_, v_ = _adamw(as2d(weights[n]), as2d(grads[n]), as2d(moments_m[n]), as2d(moments_v[n]), "adamw_" + n)
        delta[n], new_m[n], new_v[n] = d_.reshape(sh), m_.reshape(sh), v_.reshape(sh)
    rest = [n for n in names if n not in big]
    rest_shapes = [weights[n].shape for n in rest]
    packed = [_small_pack([src[n] for n in rest]) for src in (weights, grads, moments_m, moments_v)]
    d_, m_, v_ = _adamw(*packed, "adamw_small")
    for n, a, b_, c_ in zip(rest, _small_unpack(d_, rest_shapes), _small_unpack(m_, rest_shapes),
                            _small_unpack(v_, rest_shapes)):
        delta[n], new_m[n], new_v[n] = a, b_, c_

    return (loss, grad_x[None], *[grads[n] for n in names], *[delta[n] for n in names],
            *[new_m[n] for n in names], *[new_v[n] for n in names])
```

```python
import functools
import math

import jax
import jax.numpy as jnp
from jax import lax
from jax.experimental import pallas as pl
from jax.experimental.pallas import tpu as pltpu

F32 = jnp.float32
D = 1024
NIN = 4608
PW = 512
FF = 2816
PLE = 256
HEADS, HD = 8, 128
GROUPS, GD = 4, 128
WINDOWS = (2, 4, 8, 16)
HALO = 16
CONV = 4
EPS = 1e-6
LRU_C = 8.0
NDEV = 8
MESH = pl.DeviceIdType.MESH

ADAM_LR, ADAM_B1, ADAM_B2, ADAM_EPS, ADAM_WD, ADAM_STEP = 0.001, 0.9, 0.999, 1e-08, 0.01, 10

_MXU = jnp.bfloat16
TM = 512
TM_SEQ = 256
VMEM_LIMIT = 56 * 1024 * 1024

W_FIRST = (("win", 576), ("f32s", 16))
W_SECOND = (("wffn", 704), ("pproj", 128), ("ple", 128), ("lru", 128), ("wout", 128), ("wffo", 352), ("wpg", 128))
W_OFF = {}
for _slabs in (W_FIRST, W_SECOND):
    _o = 0
    for _n, _r in _slabs:
        W_OFF[_n] = (_o, _r)
        _o += _r
SMALL_ROWS = 48


def _cparams(n_axes=1, vmem=VMEM_LIMIT):
    return pltpu.CompilerParams(dimension_semantics=("arbitrary",) * n_axes, vmem_limit_bytes=vmem)


def _my_pos():
    return lax.axis_index("x"), lax.axis_index("y"), lax.axis_index("c")


def _nt(a, b):
    return lax.dot_general(a, b, (((1,), (1,)), ((), ())), preferred_element_type=F32)


def _nn(a, b):
    return lax.dot_general(a, b, (((1,), (0,)), ((), ())), preferred_element_type=F32)


def _tn(a, b):
    return lax.dot_general(a, b, (((0,), (0,)), ((), ())), preferred_element_type=F32)


def _sigmoid(x):
    return 0.5 * jnp.tanh(0.5 * x) + 0.5


_GELU_K = math.sqrt(2.0 / math.pi)


def _gelu_and_grad(x):
    x2 = x * x
    inner = _GELU_K * (x + 0.044715 * x2 * x)
    t = jnp.tanh(inner)
    g = 0.5 * x * (1.0 + t)
    dg = 0.5 * (1.0 + t) + 0.5 * x * (1.0 - t * t) * _GELU_K * (1.0 + 3.0 * 0.044715 * x2)
    return g, dg


def _softplus_neg(lam):
    x = -lam
    t = jnp.exp(-jnp.abs(x))
    u = 1.0 + t
    l1p = jnp.where(u == 1.0, t, jnp.log(u) * t / (u - 1.0))
    return jnp.maximum(x, 0.0) + l1p, _sigmoid(x)


def _start_slab_loads(g_ref, name, dst_ref, sems, base, width=D):
    off, rows = W_OFF[name]
    copies = []
    for k in range(NDEV):
        if width == D:
            src = g_ref.at[k, pl.ds(off, rows), :]
        else:
            src = g_ref.at[k, pl.ds(off, rows), pl.ds(0, width)]
        cp = pltpu.make_async_copy(src, dst_ref.at[pl.ds(k * rows, rows), :], sems.at[base + k])
        cp.start()
        copies.append(cp)
    return copies


def _load_weights(g_ref, items, sems):
    copies = []
    for n, (name, dst, width) in enumerate(items):
        copies += _start_slab_loads(g_ref, name, dst, sems, n * NDEV, width)
    for cp in copies:
        cp.wait()


def _gather_phases(own_ref, out_ref, stage, send_sems, recv_sems, local_sem):
    x, y, c = _my_pos()
    me, sibling = (x, y, c), (x, y, 1 - c)
    chips = [(1 - x, y), (x, 1 - y), (1 - x, 1 - y)]

    def slab(px, py, pc):
        return out_ref.at[4 * px + 2 * py + pc]

    def copy(k, block, to, src=None):
        return pltpu.make_async_remote_copy(
            src_ref=slab(*block) if src is None else src, dst_ref=slab(*block),
            send_sem=send_sems.at[k], recv_sem=recv_sems.at[k], device_id=to, device_id_type=MESH)

    mine = pltpu.make_async_copy(stage, slab(*me), local_sem)
    first = [copy(0, me, sibling, src=stage)] + [copy(1 + j, me, (*chip, c), src=stage) for j, chip in enumerate(chips)]
    passed = [copy(4 + j, (*chip, c), sibling) for j, chip in enumerate(chips)]

    def send_mine():
        pltpu.sync_copy(own_ref, stage)
        mine.start()
        for cp in first:
            cp.start()

    def pass_on(js):
        for j in js:
            copy(1 + j, (*chips[j], c), me).wait_recv()
            passed[j].start()

    def finish():
        copy(0, sibling, me).wait_recv()
        for j, chip in enumerate(chips):
            copy(4 + j, (*chip, 1 - c), me).wait_recv()
        for cp in first + passed:
            cp.wait_send()
        mine.wait()

    return send_mine, pass_on, finish


def _all_gather_weights(own):
    rows, cols = own.shape

    def body(own_ref, out_ref, stage, send_sems, recv_sems, local_sem):
        send_mine, pass_on, finish = _gather_phases(own_ref, out_ref, stage, send_sems, recv_sems, local_sem)
        send_mine()
        pass_on((0, 1, 2))
        finish()

    return pl.pallas_call(
        body, name="ag_weights",
        out_shape=jax.ShapeDtypeStruct((NDEV, rows, cols), own.dtype),
        in_specs=[pl.BlockSpec(memory_space=pl.ANY)],
        out_specs=pl.BlockSpec(memory_space=pl.ANY),
        scratch_shapes=[pltpu.VMEM((rows, cols), own.dtype), pltpu.SemaphoreType.DMA((7,)),
                        pltpu.SemaphoreType.DMA((7,)), pltpu.SemaphoreType.DMA],
        compiler_params=pltpu.CompilerParams(vmem_limit_bytes=VMEM_LIMIT),
    )(own)


def _all_gather_small(piece):
    rows = piece.shape[0]

    def body(p_ref, out_ref, send_sems, recv_sems, local_sem):
        x, y, c = _my_pos()
        me = 4 * x + 2 * y + c
        mine = pltpu.make_async_copy(p_ref, out_ref.at[pl.ds(pl.multiple_of(me * rows, 8), rows), :], local_sem)
        mine.start()
        sends = []
        peers = []
        for r in range(1, NDEV):
            px = 1 - x if (r >> 2) & 1 else x
            py = 1 - y if (r >> 1) & 1 else y
            pc = 1 - c if r & 1 else c
            peers.append((px, py, pc))
            cp = pltpu.make_async_remote_copy(
                src_ref=p_ref, dst_ref=out_ref.at[pl.ds(pl.multiple_of(me * rows, 8), rows), :],
                send_sem=send_sems.at[r - 1], recv_sem=recv_sems.at[r - 1], device_id=(px, py, pc),
                device_id_type=MESH)
            cp.start()
            sends.append(cp)
        for r, (px, py, pc) in enumerate(peers):
            them = 4 * px + 2 * py + pc
            pltpu.make_async_remote_copy(
                src_ref=p_ref, dst_ref=out_ref.at[pl.ds(pl.multiple_of(them * rows, 8), rows), :],
                send_sem=send_sems.at[r], recv_sem=recv_sems.at[r], device_id=(px, py, pc),
                device_id_type=MESH).wait_recv()
        for cp in sends:
            cp.wait_send()
        mine.wait()

    return pl.pallas_call(
        body, name="ag_small",
        out_shape=jax.ShapeDtypeStruct((NDEV * rows, piece.shape[1]), piece.dtype),
        in_specs=[pl.BlockSpec(memory_space=pltpu.VMEM)],
        out_specs=pl.BlockSpec(memory_space=pl.ANY),
        scratch_shapes=[pltpu.SemaphoreType.DMA((7,)), pltpu.SemaphoreType.DMA((7,)), pltpu.SemaphoreType.DMA],
    )(piece)


def _row_block(rows, target=512, mult=8):
    b = min(rows, target) // mult * mult
    while rows % b:
        b -= mult
    return b


def _sum_arrays(arrs, name, narrow=None, target=464):
    rows, cols = arrs[0].shape
    br = _row_block(rows, target, 16)
    n = len(arrs)

    def body(*refs):
        acc = refs[0][...].astype(F32)
        for r in refs[1:n]:
            acc = acc + r[...].astype(F32)
        refs[n][...] = acc
        if narrow is not None:
            refs[n + 1][...] = acc.astype(narrow)

    spec = pl.BlockSpec((br, cols), lambda i: (i, 0))
    shape = jax.ShapeDtypeStruct((rows, cols), F32)
    if narrow is None:
        out_shape, out_specs = shape, spec
    else:
        out_shape, out_specs = (shape, jax.ShapeDtypeStruct((rows, cols), narrow)), (spec, spec)
    return pl.pallas_call(
        body, name=name, grid=(rows // br,), out_shape=out_shape,
        in_specs=[spec] * n, out_specs=out_specs, compiler_params=_cparams(1),
    )(*arrs)


def _adamw(w, g, m, v, name):
    rows, cols = w.shape
    br = _row_block(rows, 256)

    def body(w_ref, g_ref, m_ref, v_ref, d_ref, nm_ref, nv_ref):
        g_ = g_ref[...]
        m_ = ADAM_B1 * m_ref[...] + (1.0 - ADAM_B1) * g_
        v_ = ADAM_B2 * v_ref[...] + (1.0 - ADAM_B2) * (g_ * g_)
        m_hat = m_ / (1.0 - ADAM_B1 ** ADAM_STEP)
        v_hat = v_ / (1.0 - ADAM_B2 ** ADAM_STEP)
        d_ref[...] = -ADAM_LR * (m_hat / (jnp.sqrt(v_hat) + ADAM_EPS) + ADAM_WD * w_ref[...])
        nm_ref[...] = m_
        nv_ref[...] = v_

    spec = pl.BlockSpec((br, cols), lambda i: (i, 0))
    shape = jax.ShapeDtypeStruct((rows, cols), F32)
    return pl.pallas_call(
        body, name=name, grid=(rows // br,), out_shape=(shape, shape, shape),
        in_specs=[spec] * 4, out_specs=(spec, spec, spec), compiler_params=_cparams(1),
    )(w, g, m, v)


_CHIP_FLIPS = (2, 3, 1, 0)


def _grad_matmul(lhs, rhs, name):
    s, r = lhs.shape
    k = rhs.shape[1]
    tm = min(TM, s)

    def body(l_ref, r_ref, o_ref):
        @pl.when(pl.program_id(0) == 0)
        def _():
            o_ref[...] = jnp.zeros_like(o_ref)

        o_ref[:, pl.ds(0, k)] += _tn(l_ref[...].astype(_MXU), r_ref[...].astype(_MXU))

    return pl.pallas_call(
        body, name=name, grid=(s // tm,),
        out_shape=jax.ShapeDtypeStruct((r, D), F32),
        in_specs=[pl.BlockSpec((tm, r), lambda i: (i, 0)), pl.BlockSpec((tm, k), lambda i: (i, 0))],
        out_specs=pl.BlockSpec((r, D), lambda i: (0, 0)),
        compiler_params=_cparams(1),
    )(lhs, rhs)


def _grad_matmul_rs(lhs, rhs, name, rows, chips_per_block=1, extras=(), narrow=None, tail=0):
    s, r8 = lhs.shape
    k = rhs.shape[1]
    tm = min(TM, s)
    nt = s // tm
    cpb = chips_per_block
    nblk = 4 // cpb
    nx = len(extras)
    ers = [e.shape[0] // NDEV for e in extras]
    er = sum(ers)
    srows = rows + er
    brows = 2 * cpb * srows
    groups = [(0, srows - tail, F32 if narrow is None else narrow)] + ([(srows - tail, tail, F32)] if tail else [])
    ng = len(groups)
    mid = min(nt - 1, max(1, nt // 4))

    def flip_of(p):
        return jnp.where(p == 0, 2, jnp.where(p == 1, 3, jnp.where(p == 2, 1, 0)))

    def block_col(b):
        x, y, _ = _my_pos()
        mine = 2 * x + y
        if cpb == 1:
            return mine ^ flip_of(b)
        return jnp.where(b == 0, (mine >> 1) ^ 1, mine >> 1)

    def body(*refs):
        l_ref, r_ref = refs[:2]
        x_refs = refs[2:2 + nx]
        rest = refs[2 + nx:]
        town_ref, ld2d_ref = rest[:2]
        lici_refs = rest[2:2 + ng]
        acc, stage = rest[2 + ng:4 + ng]
        send_bufs = rest[4 + ng:4 + 2 * ng]
        dsend, drecv, isend, irecv, xsem = rest[4 + 2 * ng:]
        b = pl.program_id(0)
        i = pl.program_id(1)
        x, y, c = _my_pos()
        mine = 2 * x + y
        sibling = (x, y, 1 - c)

        def chip_at(p):
            return mine ^ _CHIP_FLIPS[p]

        def slab_rows(p, parity):
            within = 0 if cpb == 1 else (chip_at(p) & 1) * 2
            return pl.ds(pl.multiple_of((within + parity) * srows, 8), srows)

        def push(p, slot):
            return pltpu.make_async_remote_copy(
                src_ref=acc.at[slot, slab_rows(p, 1 - c), :], dst_ref=ld2d_ref.at[p],
                send_sem=dsend.at[p], recv_sem=drecv.at[p], device_id=sibling, device_id_type=MESH)

        def ici(p):
            ch = chip_at(p)
            return [pltpu.make_async_remote_copy(
                src_ref=send_bufs[g].at[p % 2], dst_ref=lici_refs[g].at[p], send_sem=isend.at[3 * g + p],
                recv_sem=irecv.at[3 * g + p], device_id=(ch >> 1, ch & 1, c), device_id_type=MESH) for g in range(ng)]

        def extra_loads(p, slot):
            copies = []
            within = 0 if cpb == 1 else (chip_at(p) & 1) * 2
            for parity in range(2):
                off = rows
                for n, (x_ref, e) in enumerate(zip(x_refs, ers)):
                    src = x_ref.at[pl.ds(pl.multiple_of((2 * chip_at(p) + parity) * e, 8), e), :]
                    dst = acc.at[slot, pl.ds(pl.multiple_of((within + parity) * srows + off, 8), e), :]
                    copies.append(pltpu.make_async_copy(src, dst, xsem.at[(p * 2 + parity) * nx + n]))
                    off += e
            return copies

        def combine(p, slot):
            push(p, slot).wait_recv()
            pltpu.sync_copy(ld2d_ref.at[p], stage)
            total = acc[slot, slab_rows(p, c), :] + stage[...]
            if p == 3:
                stage[...] = total
                pltpu.sync_copy(stage, town_ref)
            else:
                if p == 2:
                    for cp in ici(0):
                        cp.wait_send()
                for g, (r0, n, dt) in enumerate(groups):
                    send_bufs[g][p % 2] = total[r0:r0 + n, :].astype(dt)
                for cp in ici(p):
                    cp.start()

        for bb in range(nblk):
            slot = bb % 2
            positions = list(range(bb * cpb, (bb + 1) * cpb))

            @pl.when(jnp.logical_and(b == bb, i == 0))
            def _(bb=bb, slot=slot, positions=positions):
                if bb >= 2:
                    for p in range((bb - 2) * cpb, (bb - 1) * cpb):
                        push(p, slot).wait_send()
                for q in range(2 * cpb):
                    acc[slot, pl.ds(q * srows, rows), :] = jnp.zeros((rows, D), F32)
                for p in positions:
                    for cp in extra_loads(p, slot):
                        cp.start()

            if bb >= 1:
                @pl.when(jnp.logical_and(b == bb, i == mid))
                def _(bb=bb):
                    for p in range((bb - 1) * cpb, bb * cpb):
                        combine(p, (bb - 1) % 2)

        res = _tn(l_ref[...].astype(_MXU), r_ref[...].astype(_MXU))
        slot_now = b % 2
        for q in range(2 * cpb):
            acc[slot_now, pl.ds(q * srows, rows), pl.ds(0, k)] += res[q * rows:(q + 1) * rows, :]

        for bb in range(nblk):
            slot = bb % 2
            positions = list(range(bb * cpb, (bb + 1) * cpb))

            @pl.when(jnp.logical_and(b == bb, i == nt - 1))
            def _(bb=bb, slot=slot, positions=positions):
                for p in positions:
                    for cp in extra_loads(p, slot):
                        cp.wait()
                for p in positions:
                    push(p, slot).start()
                if bb == nblk - 1:
                    for p in positions:
                        combine(p, slot)
                    for p in range(max(0, (nblk - 2) * cpb), 4):
                        push(p, slot).wait_send()
                    for p in range(1, 3):
                        for cp in ici(p):
                            cp.wait_send()
                    for p in range(3):
                        for cp in ici(p):
                            cp.wait_recv()

    in_specs = [pl.BlockSpec((tm, 2 * cpb * rows), lambda b, i: (i, block_col(b))),
                pl.BlockSpec((tm, k), lambda b, i: (i, 0))]
    any_spec = pl.BlockSpec(memory_space=pl.ANY)
    in_specs += [any_spec] * nx
    args = [lhs, rhs, *extras]
    outs = pl.pallas_call(
        body, name=name, grid=(nblk, nt),
        out_shape=(jax.ShapeDtypeStruct((srows, D), F32), jax.ShapeDtypeStruct((4, srows, D), F32))
        + tuple(jax.ShapeDtypeStruct((3, n, D), dt) for _, n, dt in groups),
        in_specs=in_specs, out_specs=(any_spec,) * (2 + ng),
        scratch_shapes=[pltpu.VMEM((2, brows, D), F32), pltpu.VMEM((srows, D), F32)]
        + [pltpu.VMEM((2, n, D), dt) for _, n, dt in groups]
        + [pltpu.SemaphoreType.DMA((4,)), pltpu.SemaphoreType.DMA((4,)), pltpu.SemaphoreType.DMA((3 * ng,)),
           pltpu.SemaphoreType.DMA((3 * ng,)), pltpu.SemaphoreType.DMA((max(1, 8 * nx),))],
        compiler_params=_cparams(2),
    )(*args)
    t_own = outs[0]
    return [(t_own[r0:r0 + n], landed) for (r0, n, _), landed in zip(groups, outs[2:])]


def _inproj_fwd(x, g1, gw, own_second):
    s = x.shape[0]
    tm = min(TM, s)
    nt = s // tm
    nchunk = 4
    cw = NIN // nchunk
    rows2, cols2 = own_second.shape

    def body(x_ref, g1_ref, gw_ref, own_ref, u_ref, z_ref, gw2_ref, w_vmem, stage, sems, send_sems, recv_sems,
             local_sem):
        i = pl.program_id(0)
        send_mine, pass_on, finish = _gather_phases(own_ref, gw2_ref, stage, send_sems, recv_sems, local_sem)

        @pl.when(i == 0)
        def _():
            send_mine()
            _load_weights(gw_ref, [("win", w_vmem, D)], sems)

        @pl.when(i == nt // 2)
        def _():
            pass_on((0, 1))

        @pl.when(i == (7 * nt) // 8)
        def _():
            pass_on((2,))

        xv = x_ref[...]
        inv = lax.rsqrt(jnp.mean(xv * xv, axis=-1, keepdims=True) + EPS)
        u = (xv * inv * g1_ref[...]).astype(_MXU)
        u_ref[...] = u
        for ch in range(nchunk):
            z_ref[:, pl.ds(ch * cw, cw)] = _nt(u, w_vmem[pl.ds(ch * cw, cw), :])

        @pl.when(i == nt - 1)
        def _():
            finish()

    any_spec = pl.BlockSpec(memory_space=pl.ANY)
    return pl.pallas_call(
        body, name="inproj_fwd", grid=(nt,),
        out_shape=(jax.ShapeDtypeStruct((s, D), _MXU), jax.ShapeDtypeStruct((s, NIN), F32),
                   jax.ShapeDtypeStruct((NDEV, rows2, cols2), own_second.dtype)),
        in_specs=[pl.BlockSpec((tm, D), lambda i: (i, 0)), pl.BlockSpec((1, D), lambda i: (0, 0)), any_spec, any_spec],
        out_specs=(pl.BlockSpec((tm, D), lambda i: (i, 0)), pl.BlockSpec((tm, NIN), lambda i: (i, 0)), any_spec),
        scratch_shapes=[pltpu.VMEM((NIN, D), _MXU), pltpu.VMEM((rows2, cols2), own_second.dtype),
                        pltpu.SemaphoreType.DMA((NDEV,)), pltpu.SemaphoreType.DMA((7,)), pltpu.SemaphoreType.DMA((7,)),
                        pltpu.SemaphoreType.DMA],
        compiler_params=_cparams(1),
    )(x, g1, gw, own_second)


def _pool_tile(pbuf, t0, tm, pw_ref, scale_ref):
    t = t0 + lax.broadcasted_iota(jnp.int32, (tm, GD), 0)
    pooled, mixed_pre = [], []
    for g, w in enumerate(WINDOWS):
        cs = pl.ds(g * GD, GD)
        cur = pbuf[pl.ds(HALO, tm), cs]
        acc = cur
        for d in range(1, w):
            acc = acc + pbuf[pl.ds(HALO - d, tm), cs]
        cnt = jnp.minimum(t + 1, w).astype(F32)
        pg = acc / cnt - cur
        pooled.append(pg)
        mixed_pre.append(_nn(pg.astype(_MXU), pw_ref[g]))
    return pooled, mixed_pre


def _lru_gates_head(hh, lbuf, start, tm, cw_ref, cb_ref, wrg_ref, brg_ref, wig_ref, big_ref, sp):
    cs = pl.ds(hh * HD, HD)
    xc = cb_ref[:, cs] + cw_ref[pl.ds(CONV - 1, 1), cs] * lbuf[pl.ds(HALO, tm), cs]
    for k in range(CONV - 1):
        xc = xc + cw_ref[pl.ds(k, 1), cs] * lbuf[pl.ds(HALO - (CONV - 1) + k, tm), cs]
    xcm = xc.astype(_MXU)
    r = _sigmoid(_nn(xcm, wrg_ref[hh]) + brg_ref[pl.ds(hh, 1), :])
    ig = _sigmoid(_nn(xcm, wig_ref[hh]) + big_ref[pl.ds(hh, 1), :])
    a = jnp.exp(-LRU_C * r * sp[:, hh * HD:(hh + 1) * HD])
    one_m = 1.0 - a * a
    live = jnp.logical_and(one_m > 0.0, jnp.logical_not(start))
    inv_mult = lax.rsqrt(jnp.where(live, one_m, 1.0))
    mult = jnp.where(live, one_m * inv_mult, jnp.where(start, 1.0, 0.0))
    return xc, r, ig, a, live, inv_mult, mult


def _seg_layout(tm):
    seg = tm // 8
    return seg, seg + 8


def _to_segments(dst_ref, hh, val, tm):
    seg, pitch = _seg_layout(tm)
    for s in range(8):
        dst_ref[hh, pl.ds(s * pitch, seg), :] = val[s * seg:(s + 1) * seg, :]


def _from_segments(src_ref, hh, tm):
    seg, pitch = _seg_layout(tm)
    return jnp.concatenate([src_ref[hh, pl.ds(s * pitch, seg), :] for s in range(8)], axis=0)


def _segment_scan(a_ref, b_ref, out_ref, hk, pk, carry_ref, tm, reverse):
    seg, pitch = _seg_layout(tm)
    row = lax.broadcasted_iota(jnp.int32, (8, HD), 0)
    order = range(seg - 1, -1, -1) if reverse else range(seg)
    for hh in range(HEADS):
        cs = pl.ds(hh * HD, HD)
        if reverse:
            a0 = a_ref[hh, pl.ds(0, 8, stride=pitch), :]
            a_wrap = jnp.where(row <= 6, pltpu.roll(a0, 7, 0), 1.0)
        hv = jnp.zeros((8, HD), F32)
        pv = jnp.ones((8, HD), F32)
        for k in order:
            if not reverse:
                av = a_ref[hh, pl.ds(k, 8, stride=pitch), :]
            elif k + 1 < seg:
                av = a_ref[hh, pl.ds(k + 1, 8, stride=pitch), :]
            else:
                av = a_wrap
            hv = av * hv + b_ref[hh, pl.ds(k, 8, stride=pitch), :]
            pv = av * pv
            hk[hh, pl.ds(8 * k, 8), :] = hv
            pk[hh, pl.ds(8 * k, 8), :] = pv
        for d in (1, 2, 4):
            if reverse:
                keep, sh = row < 8 - d, 8 - d
            else:
                keep, sh = row >= d, d
            hv = hv + pv * jnp.where(keep, pltpu.roll(hv, sh, 0), 0.0)
            pv = pv * jnp.where(keep, pltpu.roll(pv, sh, 0), 1.0)
        cin = carry_ref[:, cs]
        ends = hv + pv * cin
        if reverse:
            enter = jnp.where(row <= 6, pltpu.roll(ends, 7, 0), cin)
            carry_ref[:, cs] = jnp.broadcast_to((a0 * ends)[0:1, :], (8, HD))
        else:
            enter = jnp.where(row >= 1, pltpu.roll(ends, 1, 0), cin)
            carry_ref[:, cs] = jnp.broadcast_to(ends[7:8, :], (8, HD))
        for k in range(seg):
            out_ref[hh, pl.ds(k, 8, stride=pitch), :] = hk[hh, pl.ds(8 * k, 8), :] + pk[hh, pl.ds(8 * k, 8), :] * enter


def _mixer_fwd(z, x, gw, small):
    s = x.shape[0]
    tm = min(TM_SEQ, s)
    (pool_w, pool_scale, conv_w, conv_b, w_rg, b_rg, w_ig, b_ig, lam, b_gate) = small

    def body(z_ref, x_ref, gw_ref, pw_ref, ps_ref, cw_ref, cb_ref, wrg_ref, brg_ref, wig_ref, big_ref, lam_ref,
             bg_ref, h_ref, yl_ref, mg_ref, yp_ref, yr_ref, h1_ref, a_ref, r_ref, ig_ref, xc_ref,
             pprojT, lru_w, wout_w, pbuf, lbuf, a_s, b_s, h_s, hk, pk, hcar, sems):
        i = pl.program_id(0)
        t0 = i * tm

        @pl.when(i == 0)
        def _():
            _load_weights(gw_ref, [("pproj", pprojT, PW), ("lru", lru_w, D), ("wout", wout_w, D)], sems)
            pbuf[pl.ds(0, HALO), :] = jnp.zeros((HALO, PW), F32)
            lbuf[pl.ds(0, HALO), :] = jnp.zeros((HALO, D), F32)
            hcar[...] = jnp.zeros_like(hcar)

        pbuf[pl.ds(HALO, tm), :] = z_ref[:, pl.ds(0, PW)]
        _, mixed_pre = _pool_tile(pbuf, t0, tm, pw_ref, ps_ref)
        mixed = jnp.concatenate(mixed_pre, axis=1) * ps_ref[...]
        y_pool = _nt(mixed.astype(_MXU), pprojT[...])
        pbuf[pl.ds(0, HALO), :] = pbuf[pl.ds(tm, HALO), :]

        lbuf[pl.ds(HALO, tm), :] = z_ref[:, pl.ds(PW, D)]
        sp, _ = _softplus_neg(lam_ref[...])
        start = (t0 + lax.broadcasted_iota(jnp.int32, (tm, HD), 0)) == 0
        for hh in range(HEADS):
            xc, r, ig, a, _, _, mult = _lru_gates_head(hh, lbuf, start, tm, cw_ref, cb_ref, wrg_ref, brg_ref,
                                                       wig_ref, big_ref, sp)
            _to_segments(a_s, hh, a, tm)
            _to_segments(b_s, hh, mult * ig * xc, tm)
            cs = pl.ds(hh * HD, HD)
            a_ref[:, cs] = a
            r_ref[:, cs] = r.astype(_MXU)
            ig_ref[:, cs] = ig.astype(_MXU)
            xc_ref[:, cs] = xc.astype(_MXU)
        lbuf[pl.ds(0, HALO), :] = lbuf[pl.ds(tm, HALO), :]
        _segment_scan(a_s, b_s, h_s, hk, pk, hcar, tm, reverse=False)
        for hh in range(HEADS):
            h_ref[:, pl.ds(hh * HD, HD)] = _from_segments(h_s, hh, tm)
        gel, _ = _gelu_and_grad(z_ref[:, pl.ds(PW + D, D)])
        yl = (h_ref[...] * gel).astype(_MXU)
        yl_ref[...] = yl
        y_lru = _nn(yl, lru_w[...])

        g0 = _sigmoid(z_ref[:, pl.ds(PW + 2 * D, D)] + bg_ref[pl.ds(0, 1), :])
        g1 = _sigmoid(z_ref[:, pl.ds(PW + 3 * D, D)] + bg_ref[pl.ds(1, 1), :])
        merged = (g0 * y_pool + g1 * y_lru).astype(_MXU)
        mg_ref[...] = merged
        yp_ref[...] = y_pool.astype(_MXU)
        yr_ref[...] = y_lru.astype(_MXU)
        h1_ref[...] = x_ref[...] + _nn(merged, wout_w[...])

    tok = lambda w, dt: jax.ShapeDtypeStruct((s, w), dt)
    tspec = lambda w: pl.BlockSpec((tm, w), lambda i: (i, 0))
    full = lambda a: pl.BlockSpec(a.shape, lambda i: (0,) * a.ndim)
    seg_buf = pltpu.VMEM((HEADS, 8 * _seg_layout(tm)[1], HD), F32)
    return pl.pallas_call(
        body, name="mixer_fwd", grid=(s // tm,),
        out_shape=(tok(D, F32), tok(D, _MXU), tok(D, _MXU), tok(D, _MXU), tok(D, _MXU), tok(D, F32),
                   tok(D, F32), tok(D, _MXU), tok(D, _MXU), tok(D, _MXU)),
        in_specs=[tspec(NIN), tspec(D), pl.BlockSpec(memory_space=pl.ANY)] + [full(a) for a in small],
        out_specs=(tspec(D),) * 10,
        scratch_shapes=[pltpu.VMEM((D, PW), _MXU), pltpu.VMEM((D, D), _MXU), pltpu.VMEM((D, D), _MXU),
                        pltpu.VMEM((tm + HALO, PW), F32), pltpu.VMEM((tm + HALO, D), F32),
                        seg_buf, seg_buf, seg_buf, pltpu.VMEM((HEADS, tm, HD), F32), pltpu.VMEM((HEADS, tm, HD), F32),
                        pltpu.VMEM((8, D), F32), pltpu.SemaphoreType.DMA((3 * NDEV,))],
        compiler_params=_cparams(1),
    )(z, x, gw, *small)


def _ffn_fwd(h1, g2, gw):
    s = h1.shape[0]
    tm = min(TM, s)
    half = FF // 2

    def body(h1_ref, g2_ref, gw_ref, v_ref, gf_ref, uf_ref, h2_ref, wffnT, wffo, sems):
        @pl.when(pl.program_id(0) == 0)
        def _():
            _load_weights(gw_ref, [("wffn", wffnT, D), ("wffo", wffo, D)], sems)

        hv = h1_ref[...]
        inv = lax.rsqrt(jnp.mean(hv * hv, axis=-1, keepdims=True) + EPS)
        v = (hv * inv * g2_ref[...]).astype(_MXU)
        v_ref[...] = v
        acc = hv
        for ch in range(2):
            cs = pl.ds(ch * half, half)
            gf = _nt(v, wffnT[pl.ds(ch * half, half), :]).astype(_MXU)
            uf = _nt(v, wffnT[pl.ds(FF + ch * half, half), :]).astype(_MXU)
            gf_ref[:, cs] = gf
            uf_ref[:, cs] = uf
            gf32 = gf.astype(F32)
            act = (gf32 * _sigmoid(gf32) * uf.astype(F32)).astype(_MXU)
            acc = acc + _nn(act, wffo[pl.ds(ch * half, half), :])
        h2_ref[...] = acc

    tspec = lambda w: pl.BlockSpec((tm, w), lambda i: (i, 0))
    return pl.pallas_call(
        body, name="ffn_fwd", grid=(s // tm,),
        out_shape=(jax.ShapeDtypeStruct((s, D), _MXU), jax.ShapeDtypeStruct((s, FF), _MXU),
                   jax.ShapeDtypeStruct((s, FF), _MXU), jax.ShapeDtypeStruct((s, D), F32)),
        in_specs=[tspec(D), pl.BlockSpec((1, D), lambda i: (0, 0)), pl.BlockSpec(memory_space=pl.ANY)],
        out_specs=(tspec(D), tspec(FF), tspec(FF), tspec(D)),
        scratch_shapes=[pltpu.VMEM((2 * FF, D), _MXU), pltpu.VMEM((FF, D), _MXU), pltpu.SemaphoreType.DMA((2 * NDEV,))],
        compiler_params=_cparams(1),
    )(h1, g2, gw)


def _rms_bwd(dy, xn, inv, g):
    dg = jnp.sum(dy * xn, axis=0, keepdims=True)
    dxn = dy * g
    dx = inv * (dxn - xn * jnp.mean(dxn * xn, axis=-1, keepdims=True))
    return dx, dg


def _ple_loss_fwd_bwd(h2, p, target, g3, gfin, gw):
    s = h2.shape[0]
    tm = min(TM, s)

    def body(h2_ref, p_ref, t_ref, g3_ref, gf_ref, gw_ref,
             dh2_ref, n3_ref, dpg_ref, de_ref, loss_ref, dg3_ref, dgf_ref, wpg, pleT, sems):
        i = pl.program_id(0)

        @pl.when(i == 0)
        def _():
            _load_weights(gw_ref, [("wpg", wpg, D), ("ple", pleT, PLE)], sems)
            loss_ref[...] = jnp.zeros_like(loss_ref)
            dg3_ref[...] = jnp.zeros_like(dg3_ref)
            dgf_ref[...] = jnp.zeros_like(dgf_ref)

        hv = h2_ref[...]
        inv3 = lax.rsqrt(jnp.mean(hv * hv, axis=-1, keepdims=True) + EPS)
        xn3 = hv * inv3
        n3 = (xn3 * g3_ref[...]).astype(_MXU)
        n3_ref[...] = n3
        pg = _sigmoid(_nn(n3, wpg[...]))
        e = _nt(p_ref[...].astype(_MXU), pleT[...])
        h3 = hv + pg * e
        invf = lax.rsqrt(jnp.mean(h3 * h3, axis=-1, keepdims=True) + EPS)
        xf = h3 * invf
        diff = xf * gf_ref[...] - t_ref[...]
        loss_ref[...] += jnp.sum(diff * diff) * (0.5 / D)
        dh3, dgf = _rms_bwd(diff * (1.0 / D), xf, invf, gf_ref[...])
        dgf_ref[...] += dgf
        de_ref[...] = (dh3 * pg).astype(_MXU)
        dpg = (dh3 * e * pg * (1.0 - pg)).astype(_MXU)
        dpg_ref[...] = dpg
        dn3 = _nt(dpg, wpg[...])
        dx3, dg3 = _rms_bwd(dn3, xn3, inv3, g3_ref[...])
        dg3_ref[...] += dg3
        dh2_ref[...] = dh3 + dx3

    tspec = lambda w: pl.BlockSpec((tm, w), lambda i: (i, 0))
    vec = pl.BlockSpec((1, D), lambda i: (0, 0))
    tok = lambda w, dt: jax.ShapeDtypeStruct((s, w), dt)
    return pl.pallas_call(
        body, name="ple_loss", grid=(s // tm,),
        out_shape=(tok(D, F32), tok(D, _MXU), tok(D, _MXU), tok(D, _MXU), jax.ShapeDtypeStruct((8, 128), F32),
                   jax.ShapeDtypeStruct((1, D), F32), jax.ShapeDtypeStruct((1, D), F32)),
        in_specs=[tspec(D), tspec(PLE), tspec(D), vec, vec, pl.BlockSpec(memory_space=pl.ANY)],
        out_specs=(tspec(D), tspec(D), tspec(D), tspec(D), pl.BlockSpec((8, 128), lambda i: (0, 0)), vec, vec),
        scratch_shapes=[pltpu.VMEM((D, D), _MXU), pltpu.VMEM((D, PLE), _MXU), pltpu.SemaphoreType.DMA((2 * NDEV,))],
        compiler_params=_cparams(1),
    )(h2, p, target, g3, gfin, gw)


def _ffn_bwd_hidden(dh2, gf, uf, gw):
    s = dh2.shape[0]
    tm = min(TM, s)
    half = FF // 2

    def body(dh2_ref, gf_ref, uf_ref, gw_ref, dff_ref, act_ref, wffo, sems):
        @pl.when(pl.program_id(0) == 0)
        def _():
            _load_weights(gw_ref, [("wffo", wffo, D)], sems)

        dm = dh2_ref[...].astype(_MXU)
        for ch in range(2):
            cs = pl.ds(ch * half, half)
            dact = _nt(dm, wffo[pl.ds(ch * half, half), :])
            gfv = gf_ref[:, cs].astype(F32)
            ufv = uf_ref[:, cs].astype(F32)
            sg = _sigmoid(gfv)
            silu = gfv * sg
            act_ref[:, cs] = (silu * ufv).astype(_MXU)
            dff_ref[:, pl.ds(ch * half, half)] = (dact * ufv * (sg * (1.0 + gfv * (1.0 - sg)))).astype(_MXU)
            dff_ref[:, pl.ds(FF + ch * half, half)] = (dact * silu).astype(_MXU)

    tspec = lambda w: pl.BlockSpec((tm, w), lambda i: (i, 0))
    return pl.pallas_call(
        body, name="ffn_bwd_hidden", grid=(s // tm,),
        out_shape=(jax.ShapeDtypeStruct((s, 2 * FF), _MXU), jax.ShapeDtypeStruct((s, FF), _MXU)),
        in_specs=[tspec(D), tspec(FF), tspec(FF), pl.BlockSpec(memory_space=pl.ANY)],
        out_specs=(tspec(2 * FF), tspec(FF)),
        scratch_shapes=[pltpu.VMEM((FF, D), _MXU), pltpu.SemaphoreType.DMA((NDEV,))],
        compiler_params=_cparams(1),
    )(dh2, gf, uf, gw)


def _proj_norm_bwd(dy, x, dres, g, gw, slab, width, name):
    s = x.shape[0]
    tm = min(TM, s)

    def body(dy_ref, x_ref, dr_ref, g_ref, gw_ref, dx_ref, dg_ref, wT, sems):
        @pl.when(pl.program_id(0) == 0)
        def _():
            _load_weights(gw_ref, [(slab, wT, D)], sems)
            dg_ref[...] = jnp.zeros_like(dg_ref)

        dv = _nn(dy_ref[...], wT[...])
        xv = x_ref[...]
        inv = lax.rsqrt(jnp.mean(xv * xv, axis=-1, keepdims=True) + EPS)
        dx, dg = _rms_bwd(dv, xv * inv, inv, g_ref[...])
        dg_ref[...] += dg
        dx_ref[...] = dr_ref[...] + dx

    tspec = lambda w: pl.BlockSpec((tm, w), lambda i: (i, 0))
    vec = pl.BlockSpec((1, D), lambda i: (0, 0))
    return pl.pallas_call(
        body, name=name, grid=(s // tm,),
        out_shape=(jax.ShapeDtypeStruct((s, D), F32), jax.ShapeDtypeStruct((1, D), F32)),
        in_specs=[tspec(width), tspec(D), tspec(D), vec, pl.BlockSpec(memory_space=pl.ANY)],
        out_specs=(tspec(D), vec),
        scratch_shapes=[pltpu.VMEM((width, D), _MXU), pltpu.SemaphoreType.DMA((NDEV,))],
        compiler_params=_cparams(1),
    )(dy, x, dres, g, gw)


def _mixer_bwd(dh1, z, h, y_pool, y_lru, saved, gw, small):
    s = dh1.shape[0]
    tm = min(TM_SEQ, s)
    nt = s // tm
    (pool_w, pool_scale, conv_w, conv_b, w_rg, b_rg, w_ig, b_ig, lam, b_gate) = small

    def body(dh1_ref, z_ref, zp_ref, h_ref, hp_ref, yp_ref, yr_ref, a_ref, r_ref, ig_ref, xc_ref, gw_ref,
             pw_ref, ps_ref, cw_ref, cb_ref, wrg_ref, brg_ref, wig_ref, big_ref, lam_ref, bg_ref,
             dz_ref, dyr_ref, dyp_ref, mx_ref,
             gbg_ref, glam_ref, gbrg_ref, gbig_ref, gcb_ref, gcw_ref, gps_ref, gpw_ref, gwrg_ref, gwig_ref,
             pprojT, lru_w, wout_w, pbuf, lbuf, hbuf, qbuf, xbuf, a_s, g_s, dh_s, hk, pk, dcar, sems):
        step = pl.program_id(0)
        i = nt - 1 - step
        t0 = i * tm

        @pl.when(step == 0)
        def _():
            _load_weights(gw_ref, [("pproj", pprojT, PW), ("lru", lru_w, D), ("wout", wout_w, D)], sems)
            for ref in (gbg_ref, glam_ref, gbrg_ref, gbig_ref, gcb_ref, gcw_ref, gps_ref, gpw_ref, gwrg_ref, gwig_ref):
                ref[...] = jnp.zeros_like(ref)
            qbuf[pl.ds(tm, HALO), :] = jnp.zeros((HALO, PW), F32)
            xbuf[pl.ds(tm, 8), :] = jnp.zeros((8, D), F32)
            dcar[...] = jnp.zeros_like(dcar)

        first = i == 0
        zprev = jnp.where(first, 0.0, zp_ref[...])
        hprev = jnp.where(first, 0.0, hp_ref[...])

        d_merged = _nt(dh1_ref[...].astype(_MXU), wout_w[...])

        g0 = _sigmoid(z_ref[:, pl.ds(PW + 2 * D, D)] + bg_ref[pl.ds(0, 1), :])
        g1 = _sigmoid(z_ref[:, pl.ds(PW + 3 * D, D)] + bg_ref[pl.ds(1, 1), :])
        dz0 = d_merged * yp_ref[...].astype(F32) * g0 * (1.0 - g0)
        dz1 = d_merged * yr_ref[...].astype(F32) * g1 * (1.0 - g1)
        dz_ref[:, pl.ds(PW + 2 * D, D)] = dz0.astype(_MXU)
        dz_ref[:, pl.ds(PW + 3 * D, D)] = dz1.astype(_MXU)
        gbg_ref[pl.ds(0, 1), :] += jnp.sum(dz0, axis=0, keepdims=True)
        gbg_ref[pl.ds(1, 1), :] += jnp.sum(dz1, axis=0, keepdims=True)
        d_ypool = (d_merged * g0).astype(_MXU)
        d_ylru = (d_merged * g1).astype(_MXU)
        dyp_ref[...] = d_ypool
        dyr_ref[...] = d_ylru

        d_yl = _nt(d_ylru, lru_w[...])
        gel, dgel = _gelu_and_grad(z_ref[:, pl.ds(PW + D, D)])
        dz_ref[:, pl.ds(PW + D, D)] = (d_yl * h_ref[...] * dgel).astype(_MXU)
        g_full = d_yl * gel
        lbuf[pl.ds(0, HALO), :] = zprev[:, PW:PW + D]
        lbuf[pl.ds(HALO, tm), :] = z_ref[:, pl.ds(PW, D)]
        hbuf[pl.ds(0, 8), :] = hprev
        hbuf[pl.ds(8, tm), :] = h_ref[...]
        sp, sneg = _softplus_neg(lam_ref[...])
        start = (t0 + lax.broadcasted_iota(jnp.int32, (tm, HD), 0)) == 0
        for hh in range(HEADS):
            cs = pl.ds(hh * HD, HD)
            _to_segments(a_s, hh, a_ref[:, cs], tm)
            _to_segments(g_s, hh, g_full[:, hh * HD:(hh + 1) * HD], tm)
        _segment_scan(a_s, g_s, dh_s, hk, pk, dcar, tm, reverse=True)
        for hh in range(HEADS):
            cs = pl.ds(hh * HD, HD)
            a = a_ref[:, cs]
            r = r_ref[:, cs].astype(F32)
            ig = ig_ref[:, cs].astype(F32)
            xc = xc_ref[:, cs].astype(F32)
            a2 = a * a
            one_m = 1.0 - a2
            live = jnp.logical_and(one_m > 0.0, jnp.logical_not(start))
            inv_mult = lax.rsqrt(jnp.where(live, one_m, 1.0))
            mult = jnp.where(live, one_m * inv_mult, jnp.where(start, 1.0, 0.0))
            dh = _from_segments(dh_s, hh, tm)
            d_mult = dh * ig * xc
            d_loga = dh * hbuf[pl.ds(7, tm), cs] * a - jnp.where(live, d_mult * a2 * inv_mult, 0.0)
            glam_ref[:, cs] += jnp.sum(d_loga * (LRU_C * r) * sneg[:, hh * HD:(hh + 1) * HD], axis=0, keepdims=True)
            d_rpre = d_loga * (-LRU_C * sp[:, hh * HD:(hh + 1) * HD]) * r * (1.0 - r)
            d_igpre = dh * mult * xc * ig * (1.0 - ig)
            gbrg_ref[pl.ds(hh, 1), :] += jnp.sum(d_rpre, axis=0, keepdims=True)
            gbig_ref[pl.ds(hh, 1), :] += jnp.sum(d_igpre, axis=0, keepdims=True)
            drm = d_rpre.astype(_MXU)
            dim = d_igpre.astype(_MXU)
            xcm = xc.astype(_MXU)
            gwrg_ref[hh] += _tn(xcm, drm)
            gwig_ref[hh] += _tn(xcm, dim)
            d_xc = dh * mult * ig + _nt(drm, wrg_ref[hh]) + _nt(dim, wig_ref[hh])
            gcb_ref[:, cs] += jnp.sum(d_xc, axis=0, keepdims=True)
            for k in range(CONV):
                gcw_ref[pl.ds(k, 1), cs] += jnp.sum(d_xc * lbuf[pl.ds(HALO - (CONV - 1) + k, tm), cs], axis=0,
                                                    keepdims=True)
            xbuf[pl.ds(0, tm), cs] = d_xc
        dzl = cw_ref[pl.ds(CONV - 1, 1), :] * xbuf[pl.ds(0, tm), :]
        for k in range(CONV - 1):
            dzl = dzl + cw_ref[pl.ds(k, 1), :] * xbuf[pl.ds(CONV - 1 - k, tm), :]
        dz_ref[:, pl.ds(PW, D)] = dzl.astype(_MXU)
        xbuf[pl.ds(tm, 8), :] = xbuf[pl.ds(0, 8), :]

        d_mixed = _nn(d_ypool, pprojT[...])
        pbuf[pl.ds(0, HALO), :] = zprev[:, 0:PW]
        pbuf[pl.ds(HALO, tm), :] = z_ref[:, pl.ds(0, PW)]
        pooled, mixed_pre = _pool_tile(pbuf, t0, tm, pw_ref, ps_ref)
        mp = jnp.concatenate(mixed_pre, axis=1)
        mx_ref[...] = (mp * ps_ref[...]).astype(_MXU)
        gps_ref[...] += jnp.sum(d_mixed * mp, axis=0, keepdims=True)
        d_mp = (d_mixed * ps_ref[...]).astype(_MXU)
        t = t0 + lax.broadcasted_iota(jnp.int32, (tm, GD), 0)
        d_pooled = []
        for g, w in enumerate(WINDOWS):
            dmg = d_mp[:, g * GD:(g + 1) * GD]
            gpw_ref[g] += _tn(pooled[g].astype(_MXU), dmg)
            dp = _nt(dmg, pw_ref[g])
            d_pooled.append(dp)
            qbuf[pl.ds(0, tm), pl.ds(g * GD, GD)] = dp / jnp.minimum(t + 1, w).astype(F32)
        for g, w in enumerate(WINDOWS):
            cs = pl.ds(g * GD, GD)
            acc = qbuf[pl.ds(0, tm), cs]
            for d in range(1, w):
                acc = acc + qbuf[pl.ds(d, tm), cs]
            dz_ref[:, cs] = (acc - d_pooled[g]).astype(_MXU)
        qbuf[pl.ds(tm, HALO), :] = qbuf[pl.ds(0, HALO), :]

    rev = lambda w: pl.BlockSpec((tm, w), lambda g: (nt - 1 - g, 0))
    prev = lambda rows, w: pl.BlockSpec((rows, w), lambda g: (jnp.maximum((nt - 1 - g) * (tm // rows) - 1, 0), 0))
    full = lambda a: pl.BlockSpec(a.shape, lambda g: (0,) * a.ndim)
    tok = lambda w, dt: jax.ShapeDtypeStruct((s, w), dt)
    acc_shapes = [(2, D), (1, D), (HEADS, HD), (HEADS, HD), (1, D), (CONV, D), (1, PW), (GROUPS, GD, GD),
                  (HEADS, HD, HD), (HEADS, HD, HD)]
    acc_specs = tuple(pl.BlockSpec(sh, lambda g, n=len(sh): (0,) * n) for sh in acc_shapes)
    seg_buf = pltpu.VMEM((HEADS, 8 * _seg_layout(tm)[1], HD), F32)
    a_in, r_in, ig_in, xc_in = saved
    return pl.pallas_call(
        body, name="mixer_bwd", grid=(nt,),
        out_shape=(tok(NIN, _MXU), tok(D, _MXU), tok(D, _MXU), tok(PW, _MXU))
        + tuple(jax.ShapeDtypeStruct(sh, F32) for sh in acc_shapes),
        in_specs=[rev(D), rev(NIN), prev(HALO, NIN), rev(D), prev(8, D), rev(D), rev(D), rev(D), rev(D), rev(D), rev(D),
                  pl.BlockSpec(memory_space=pl.ANY)] + [full(a) for a in small],
        out_specs=(rev(NIN), rev(D), rev(D), rev(PW)) + acc_specs,
        scratch_shapes=[pltpu.VMEM((D, PW), _MXU), pltpu.VMEM((D, D), _MXU), pltpu.VMEM((D, D), _MXU),
                        pltpu.VMEM((tm + HALO, PW), F32), pltpu.VMEM((tm + HALO, D), F32),
                        pltpu.VMEM((tm + 8, D), F32), pltpu.VMEM((tm + HALO, PW), F32), pltpu.VMEM((tm + 8, D), F32),
                        seg_buf, seg_buf, seg_buf, pltpu.VMEM((HEADS, tm, HD), F32), pltpu.VMEM((HEADS, tm, HD), F32),
                        pltpu.VMEM((8, D), F32), pltpu.SemaphoreType.DMA((3 * NDEV,))],
        compiler_params=_cparams(1),
    )(dh1, z, z, h, h, y_pool, y_lru, a_in, r_in, ig_in, xc_in, gw, *small)


def _split3(a):
    hi = a.astype(jnp.bfloat16).astype(F32)
    mid = (a - hi).astype(jnp.bfloat16).astype(F32)
    lo = (a - hi - mid).astype(jnp.bfloat16).astype(F32)
    return jnp.stack([hi, mid, lo])


def _small_pack(parts):
    flat = jnp.concatenate([a.reshape(-1) for a in parts])
    return jnp.pad(flat, (0, NDEV * SMALL_ROWS * D - flat.shape[0])).reshape(NDEV * SMALL_ROWS, D)


def _small_unpack(packed, shapes):
    flat = packed.reshape(-1)
    out, o = [], 0
    for sh in shapes:
        n = math.prod(sh)
        out.append(flat[o:o + n].reshape(sh))
        o += n
    return out


def kernel(x, p, norm1_g, w_in, b_gate, pool_w, pool_scale, pool_proj, conv_w, conv_b, w_rg, b_rg, w_ig, b_ig, lru_lambda, lru_proj, w_out, norm2_g, w_ffn_in, w_ffn_out, ple_norm_g, w_ple_gate, w_ple_proj, final_g, loss_target, m_norm1_g, m_w_in, m_b_gate, m_pool_w, m_pool_scale, m_pool_proj, m_conv_w, m_conv_b, m_w_rg, m_b_rg, m_w_ig, m_b_ig, m_lru_lambda, m_lru_proj, m_w_out, m_norm2_g, m_w_ffn_in, m_w_ffn_out, m_ple_norm_g, m_w_ple_gate, m_w_ple_proj, m_final_g, v_norm1_g, v_w_in, v_b_gate, v_pool_w, v_pool_scale, v_pool_proj, v_conv_w, v_conv_b, v_w_rg, v_b_rg, v_w_ig, v_b_ig, v_lru_lambda, v_lru_proj, v_w_out, v_norm2_g, v_w_ffn_in, v_w_ffn_out, v_ple_norm_g, v_w_ple_gate, v_w_ple_proj, v_final_g):
    axes = ("x", "y", "c")
    me = 4 * lax.axis_index("x") + 2 * lax.axis_index("y") + lax.axis_index("c")
    x2 = x[0]
    p2 = p[0, 0]
    tgt = loss_target[0]

    n_small = (CONV + 2) * 128
    small_terms = _split3(jnp.concatenate([conv_w[0].reshape(-1), b_gate[0].reshape(-1)]))
    small_rows = jnp.pad(small_terms, ((0, 16 - 3), (0, D - n_small)))
    own_first = jnp.concatenate([w_in[0].T.astype(_MXU), small_rows.astype(_MXU)], axis=0)
    own_second = jnp.concatenate([
        w_ffn_in[0].T.astype(_MXU),
        jnp.pad(pool_proj[0].T, ((0, 0), (0, D - PW))).astype(_MXU),
        jnp.pad(w_ple_proj[0].T, ((0, 0), (0, D - PLE))).astype(_MXU),
        lru_proj[0].astype(_MXU), w_out[0].astype(_MXU), w_ffn_out[0].astype(_MXU), w_ple_gate[0].astype(_MXU),
    ], axis=0)
    gw_first = _all_gather_weights(own_first)
    off = W_OFF["f32s"][0]
    st = gw_first[:, off:off + 3, :n_small].astype(F32)
    sf = st[:, 0] + st[:, 1] + st[:, 2]
    conv_w_full = sf[:, :CONV * 128].reshape(NDEV, CONV, 128).transpose(1, 0, 2).reshape(CONV, D)
    b_gate_full = sf[:, CONV * 128:].reshape(NDEV, 2, 128).transpose(1, 0, 2).reshape(2, D)

    small = (pool_w[0].astype(_MXU), pool_scale, conv_w_full, conv_b, w_rg[0].astype(_MXU), b_rg[0],
             w_ig[0].astype(_MXU), b_ig[0], lru_lambda, b_gate_full)

    u, z, gw = _inproj_fwd(x2, norm1_g, gw_first, own_second)
    h, yl, merged, y_pool, y_lru, h1, *saved = _mixer_fwd(z, x2, gw, small)
    v, gf, uf, h2 = _ffn_fwd(h1, norm2_g, gw)

    dh2, n3, dpg, de, loss_blk, g_ple_norm, g_final = _ple_loss_fwd_bwd(h2, p2, tgt, ple_norm_g, final_g.reshape(1, D), gw)
    dff, act = _ffn_bwd_hidden(dh2, gf, uf, gw)
    dh1, g_norm2 = _proj_norm_bwd(dff, h1, dh2, norm2_g, gw, "wffn", 2 * FF, "ffn_bwd_in")
    (dz, d_ylru, d_ypool, mixed, g_bgate, g_lam, g_brg, g_big, g_convb, g_convw, g_pscale, g_poolw, g_wrg,
     g_wig) = _mixer_bwd(dh1, z, h, y_pool, y_lru, saved, gw, small)
    grad_x, g_norm1 = _proj_norm_bwd(dz, x2, dh1, norm1_g, gw_first, "win", NIN, "inproj_bwd")

    small_shapes = [(1, D), (GROUPS, GD, GD), (1, PW), (1, D), (HEADS, HD, HD), (HEADS, HD), (HEADS, HD, HD),
                    (HEADS, HD), (1, D), (1, D), (1, D), (1, D), (2, D), (CONV, D), (1, 1)]
    small_part = _small_pack([g_norm1, g_poolw, g_pscale, g_convb, g_wrg, g_brg, g_wig, g_big, g_lam, g_norm2,
                              g_ple_norm, g_final, g_bgate, g_convw, loss_blk[0:1, 0:1]])
    riders = [_grad_matmul(n3, dpg, "grad_w_ple_gate"), _grad_matmul(yl, d_ylru, "grad_lru_proj"),
              _grad_matmul(merged, dh1, "grad_w_out"), _grad_matmul(d_ypool, mixed, "grad_pool_proj"),
              _grad_matmul(de, p2, "grad_w_ple_proj")]
    part_wffo = _grad_matmul(act, dh2, "grad_w_ffn_out")
    rs_wffn = _grad_matmul_rs(dff, v, "grad_w_ffn_in", 704, extras=[part_wffo], narrow=_MXU)
    rs_win = _grad_matmul_rs(dz, u, "grad_w_in", 576, extras=riders + [small_part], narrow=_MXU, tail=SMALL_ROWS)

    def reduced(parts, name):
        return [_sum_arrays([t_own, landed[0], landed[1], landed[2]], "rs_sum_" + name + str(n))
                for n, (t_own, landed) in enumerate(parts)]

    red_wffn, = reduced(rs_wffn, "wffn")
    red_win, red_small = reduced(rs_win, "win")
    g_w_in = red_win[:576].T
    g_w_ffn_in = red_wffn[:704].T
    g_w_ffn_out = red_wffn[704:]
    g_w_ple_gate, g_lru_proj, g_w_out = red_win[576:704], red_win[704:832], red_win[832:960]
    g_pool_proj = red_win[960:1088, :PW].T
    g_w_ple_proj = red_win[1088:1216, :PLE].T
    small_red = _all_gather_small(red_small)
    (gs_norm1, gs_poolw, gs_pscale, gs_convb, gs_wrg, gs_brg, gs_wig, gs_big, gs_lam, gs_norm2, gs_ple_norm,
     gs_final, gs_bgate, gs_convw, loss_sum) = _small_unpack(small_red, small_shapes)
    loss = loss_sum[0, 0]
    g_b_gate = lax.dynamic_slice_in_dim(gs_bgate, me * 128, 128, axis=1)
    g_conv_w = lax.dynamic_slice_in_dim(gs_convw, me * 128, 128, axis=1)

    grads = {
        "norm1_g": gs_norm1, "w_in": g_w_in[None], "b_gate": g_b_gate[None], "pool_w": gs_poolw[None],
        "pool_scale": gs_pscale, "pool_proj": g_pool_proj[None], "conv_w": g_conv_w[None], "conv_b": gs_convb,
        "w_rg": gs_wrg[None], "b_rg": gs_brg[None], "w_ig": gs_wig[None], "b_ig": gs_big[None], "lru_lambda": gs_lam,
        "lru_proj": g_lru_proj[None], "w_out": g_w_out[None], "norm2_g": gs_norm2, "w_ffn_in": g_w_ffn_in[None],
        "w_ffn_out": g_w_ffn_out[None], "ple_norm_g": gs_ple_norm, "w_ple_gate": g_w_ple_gate[None],
        "w_ple_proj": g_w_ple_proj[None], "final_g": gs_final.reshape(D),
    }
    weights = dict(norm1_g=norm1_g, w_in=w_in, b_gate=b_gate, pool_w=pool_w, pool_scale=pool_scale, pool_proj=pool_proj,
                   conv_w=conv_w, conv_b=conv_b, w_rg=w_rg, b_rg=b_rg, w_ig=w_ig, b_ig=b_ig, lru_lambda=lru_lambda,
                   lru_proj=lru_proj, w_out=w_out, norm2_g=norm2_g, w_ffn_in=w_ffn_in, w_ffn_out=w_ffn_out,
                   ple_norm_g=ple_norm_g, w_ple_gate=w_ple_gate, w_ple_proj=w_ple_proj, final_g=final_g)
    moments_m = dict(norm1_g=m_norm1_g, w_in=m_w_in, b_gate=m_b_gate, pool_w=m_pool_w, pool_scale=m_pool_scale,
                     pool_proj=m_pool_proj, conv_w=m_conv_w, conv_b=m_conv_b, w_rg=m_w_rg, b_rg=m_b_rg, w_ig=m_w_ig,
                     b_ig=m_b_ig, lru_lambda=m_lru_lambda, lru_proj=m_lru_proj, w_out=m_w_out, norm2_g=m_norm2_g,
                     w_ffn_in=m_w_ffn_in, w_ffn_out=m_w_ffn_out, ple_norm_g=m_ple_norm_g, w_ple_gate=m_w_ple_gate,
                     w_ple_proj=m_w_ple_proj, final_g=m_final_g)
    moments_v = dict(norm1_g=v_norm1_g, w_in=v_w_in, b_gate=v_b_gate, pool_w=v_pool_w, pool_scale=v_pool_scale,
                     pool_proj=v_pool_proj, conv_w=v_conv_w, conv_b=v_conv_b, w_rg=v_w_rg, b_rg=v_b_rg, w_ig=v_w_ig,
                     b_ig=v_b_ig, lru_lambda=v_lru_lambda, lru_proj=v_lru_proj, w_out=v_w_out, norm2_g=v_norm2_g,
                     w_ffn_in=v_w_ffn_in, w_ffn_out=v_w_ffn_out, ple_norm_g=v_ple_norm_g, w_ple_gate=v_w_ple_gate,
                     w_ple_proj=v_w_ple_proj, final_g=v_final_g)
    names = list(weights)
    big = ("w_in", "w_ffn_in", "w_ffn_out", "lru_proj", "w_out", "w_ple_gate", "pool_proj", "w_ple_proj")
    delta, new_m, new_v = {}, {}, {}
    for n in big:
        sh = weights[n].shape
        as2d = lambda a: a.reshape(sh[-2], sh[-1])
        d_, m_, v_ = _adamw(as2d(weights[n]), as2d(grads[n]), as2d(moments_m[n]), as2d(moments_v[n]), "adamw_" + n)
        delta[n], new_m[n], new_v[n] = d_.reshape(sh), m_.reshape(sh), v_.reshape(sh)
    rest = [n for n in names if n not in big]
    rest_shapes = [weights[n].shape for n in rest]
    packed = [_small_pack([src[n] for n in rest]) for src in (weights, grads, moments_m, moments_v)]
    d_, m_, v_ = _adamw(*packed, "adamw_small")
    for n, a, b_, c_ in zip(rest, _small_unpack(d_, rest_shapes), _small_unpack(m_, rest_shapes),
                            _small_unpack(v_, rest_shapes)):
        delta[n], new_m[n], new_v[n] = a, b_, c_

    return (loss, grad_x[None], *[grads[n] for n in names], *[delta[n] for n in names],
            *[new_m[n] for n in names], *[new_v[n] for n in names])
```

```python
import functools
import math

import jax
import jax.numpy as jnp
from jax import lax
from jax.experimental import pallas as pl
from jax.experimental.pallas import tpu as pltpu

F32 = jnp.float32
D = 1024
NIN = 4608
PW = 512
FF = 2816
PLE = 256
HEADS, HD = 8, 128
GROUPS, GD = 4, 128
WINDOWS = (2, 4, 8, 16)
HALO = 16
CONV = 4
EPS = 1e-6
LRU_C = 8.0
NDEV = 8
MESH = pl.DeviceIdType.MESH

ADAM_LR, ADAM_B1, ADAM_B2, ADAM_EPS, ADAM_WD, ADAM_STEP = 0.001, 0.9, 0.999, 1e-08, 0.01, 10

_MXU = jnp.bfloat16
TM = 512
TM_SEQ = 256
VMEM_LIMIT = 56 * 1024 * 1024
W_FIRST = (("win", 576), ("f32s", 16))
W_SECOND = (("wffn", 704), ("pproj", 128), ("ple", 128), ("lru", 128), ("wout", 128), ("wffo", 352), ("wpg", 128))
W_OFF = {}
for _slabs in (W_FIRST, W_SECOND):
    _o = 0
    for _n, _r in _slabs:
        W_OFF[_n] = (_o, _r)
        _o += _r
SMALL_ROWS = 48


def _cparams(n_axes=1, vmem=VMEM_LIMIT):
    return pltpu.CompilerParams(dimension_semantics=("arbitrary",) * n_axes, vmem_limit_bytes=vmem)


def _my_pos():
    return lax.axis_index("x"), lax.axis_index("y"), lax.axis_index("c")


def _nt(a, b):
    return lax.dot_general(a, b, (((1,), (1,)), ((), ())), preferred_element_type=F32)


def _nn(a, b):
    return lax.dot_general(a, b, (((1,), (0,)), ((), ())), preferred_element_type=F32)


def _tn(a, b):
    return lax.dot_general(a, b, (((0,), (0,)), ((), ())), preferred_element_type=F32)


def _sigmoid(x):
    return 0.5 * jnp.tanh(0.5 * x) + 0.5


_GELU_K = math.sqrt(2.0 / math.pi)


def _gelu_and_grad(x):
    x2 = x * x
    inner = _GELU_K * (x + 0.044715 * x2 * x)
    t = jnp.tanh(inner)
    g = 0.5 * x * (1.0 + t)
    dg = 0.5 * (1.0 + t) + 0.5 * x * (1.0 - t * t) * _GELU_K * (1.0 + 3.0 * 0.044715 * x2)
    return g, dg


def _softplus_neg(lam):
    x = -lam
    t = jnp.exp(-jnp.abs(x))
    u = 1.0 + t
    l1p = jnp.where(u == 1.0, t, jnp.log(u) * t / (u - 1.0))
    return jnp.maximum(x, 0.0) + l1p, _sigmoid(x)


def _start_slab_loads(g_ref, name, dst_ref, sems, base, width=D):
    off, rows = W_OFF[name]
    copies = []
    for k in range(NDEV):
        if width == D:
            src = g_ref.at[k, pl.ds(off, rows), :]
        else:
            src = g_ref.at[k, pl.ds(off, rows), pl.ds(0, width)]
        cp = pltpu.make_async_copy(src, dst_ref.at[pl.ds(k * rows, rows), :], sems.at[base + k])
        cp.start()
        copies.append(cp)
    return copies


def _load_weights(g_ref, items, sems):
    copies = []
    for n, (name, dst, width) in enumerate(items):
        copies += _start_slab_loads(g_ref, name, dst, sems, n * NDEV, width)
    for cp in copies:
        cp.wait()


def _gather_phases(own_ref, out_ref, stage, send_sems, recv_sems, local_sem):
    x, y, c = _my_pos()
    me, sibling = (x, y, c), (x, y, 1 - c)
    chips = [(1 - x, y), (x, 1 - y), (1 - x, 1 - y)]

    def slab(px, py, pc):
        return out_ref.at[4 * px + 2 * py + pc]

    def copy(k, block, to, src=None):
        return pltpu.make_async_remote_copy(
            src_ref=slab(*block) if src is None else src, dst_ref=slab(*block),
            send_sem=send_sems.at[k], recv_sem=recv_sems.at[k], device_id=to, device_id_type=MESH)

    mine = pltpu.make_async_copy(stage, slab(*me), local_sem)
    first = [copy(0, me, sibling, src=stage)] + [copy(1 + j, me, (*chip, c), src=stage) for j, chip in enumerate(chips)]
    passed = [copy(4 + j, (*chip, c), sibling) for j, chip in enumerate(chips)]

    def send_mine():
        pltpu.sync_copy(own_ref, stage)
        mine.start()
        for cp in first:
            cp.start()

    def pass_on(js):
        for j in js:
            copy(1 + j, (*chips[j], c), me).wait_recv()
            passed[j].start()

    def finish():
        copy(0, sibling, me).wait_recv()
        for j, chip in enumerate(chips):
            copy(4 + j, (*chip, 1 - c), me).wait_recv()
        for cp in first + passed:
            cp.wait_send()
        mine.wait()

    return send_mine, pass_on, finish


def _all_gather_weights(own):
    rows, cols = own.shape

    def body(own_ref, out_ref, stage, send_sems, recv_sems, local_sem):
        send_mine, pass_on, finish = _gather_phases(own_ref, out_ref, stage, send_sems, recv_sems, local_sem)
        send_mine()
        pass_on((0, 1, 2))
        finish()

    return pl.pallas_call(
        body, name="ag_weights",
        out_shape=jax.ShapeDtypeStruct((NDEV, rows, cols), own.dtype),
        in_specs=[pl.BlockSpec(memory_space=pl.ANY)],
        out_specs=pl.BlockSpec(memory_space=pl.ANY),
        scratch_shapes=[pltpu.VMEM((rows, cols), own.dtype), pltpu.SemaphoreType.DMA((7,)),
                        pltpu.SemaphoreType.DMA((7,)), pltpu.SemaphoreType.DMA],
        compiler_params=pltpu.CompilerParams(vmem_limit_bytes=VMEM_LIMIT),
    )(own)


def _all_gather_small(piece):
    rows = piece.shape[0]

    def body(p_ref, out_ref, send_sems, recv_sems, local_sem):
        x, y, c = _my_pos()
        me = 4 * x + 2 * y + c
        mine = pltpu.make_async_copy(p_ref, out_ref.at[pl.ds(pl.multiple_of(me * rows, 8), rows), :], local_sem)
        mine.start()
        sends = []
        peers = []
        for r in range(1, NDEV):
            px = 1 - x if (r >> 2) & 1 else x
            py = 1 - y if (r >> 1) & 1 else y
            pc = 1 - c if r & 1 else c
            peers.append((px, py, pc))
            cp = pltpu.make_async_remote_copy(
                src_ref=p_ref, dst_ref=out_ref.at[pl.ds(pl.multiple_of(me * rows, 8), rows), :],
                send_sem=send_sems.at[r - 1], recv_sem=recv_sems.at[r - 1], device_id=(px, py, pc),
                device_id_type=MESH)
            cp.start()
            sends.append(cp)
        for r, (px, py, pc) in enumerate(peers):
            them = 4 * px + 2 * py + pc
            pltpu.make_async_remote_copy(
                src_ref=p_ref, dst_ref=out_ref.at[pl.ds(pl.multiple_of(them * rows, 8), rows), :],
                send_sem=send_sems.at[r], recv_sem=recv_sems.at[r], device_id=(px, py, pc),
                device_id_type=MESH).wait_recv()
        for cp in sends:
            cp.wait_send()
        mine.wait()

    return pl.pallas_call(
        body, name="ag_small",
        out_shape=jax.ShapeDtypeStruct((NDEV * rows, piece.shape[1]), piece.dtype),
        in_specs=[pl.BlockSpec(memory_space=pltpu.VMEM)],
        out_specs=pl.BlockSpec(memory_space=pl.ANY),
        scratch_shapes=[pltpu.SemaphoreType.DMA((7,)), pltpu.SemaphoreType.DMA((7,)), pltpu.SemaphoreType.DMA],
    )(piece)


def _row_block(rows, target=512, mult=8):
    b = min(rows, target) // mult * mult
    while rows % b:
        b -= mult
    return b


def _sum_arrays(arrs, name, narrow=None, target=464):
    rows, cols = arrs[0].shape
    br = _row_block(rows, target, 16)
    n = len(arrs)

    def body(*refs):
        acc = refs[0][...].astype(F32)
        for r in refs[1:n]:
            acc = acc + r[...].astype(F32)
        refs[n][...] = acc
        if narrow is not None:
            refs[n + 1][...] = acc.astype(narrow)

    spec = pl.BlockSpec((br, cols), lambda i: (i, 0))
    shape = jax.ShapeDtypeStruct((rows, cols), F32)
    if narrow is None:
        out_shape, out_specs = shape, spec
    else:
        out_shape, out_specs = (shape, jax.ShapeDtypeStruct((rows, cols), narrow)), (spec, spec)
    return pl.pallas_call(
        body, name=name, grid=(rows // br,), out_shape=out_shape,
        in_specs=[spec] * n, out_specs=out_specs, compiler_params=_cparams(1),
    )(*arrs)


def _adamw(w, g, m, v, name):
    rows, cols = w.shape
    br = _row_block(rows, 256)

    def body(w_ref, g_ref, m_ref, v_ref, d_ref, nm_ref, nv_ref):
        g_ = g_ref[...]
        m_ = ADAM_B1 * m_ref[...] + (1.0 - ADAM_B1) * g_
        v_ = ADAM_B2 * v_ref[...] + (1.0 - ADAM_B2) * (g_ * g_)
        m_hat = m_ / (1.0 - ADAM_B1 ** ADAM_STEP)
        v_hat = v_ / (1.0 - ADAM_B2 ** ADAM_STEP)
        d_ref[...] = -ADAM_LR * (m_hat / (jnp.sqrt(v_hat) + ADAM_EPS) + ADAM_WD * w_ref[...])
        nm_ref[...] = m_
        nv_ref[...] = v_

    spec = pl.BlockSpec((br, cols), lambda i: (i, 0))
    shape = jax.ShapeDtypeStruct((rows, cols), F32)
    return pl.pallas_call(
        body, name=name, grid=(rows // br,), out_shape=(shape, shape, shape),
        in_specs=[spec] * 4, out_specs=(spec, spec, spec), compiler_params=_cparams(1),
    )(w, g, m, v)


_CHIP_FLIPS = (2, 3, 1, 0)


def _grad_matmul(lhs, rhs, name):
    s, r = lhs.shape
    k = rhs.shape[1]
    tm = min(TM, s)

    def body(l_ref, r_ref, o_ref):
        @pl.when(pl.program_id(0) == 0)
        def _():
            o_ref[...] = jnp.zeros_like(o_ref)

        o_ref[:, pl.ds(0, k)] += _tn(l_ref[...].astype(_MXU), r_ref[...].astype(_MXU))

    return pl.pallas_call(
        body, name=name, grid=(s // tm,),
        out_shape=jax.ShapeDtypeStruct((r, D), F32),
        in_specs=[pl.BlockSpec((tm, r), lambda i: (i, 0)), pl.BlockSpec((tm, k), lambda i: (i, 0))],
        out_specs=pl.BlockSpec((r, D), lambda i: (0, 0)),
        compiler_params=_cparams(1),
    )(lhs, rhs)


def _grad_matmul_rs(lhs, rhs, name, rows, chips_per_block=1, extras=(), narrow=None, tail=0):
    s, r8 = lhs.shape
    k = rhs.shape[1]
    tm = min(TM, s)
    nt = s // tm
    cpb = chips_per_block
    nblk = 4 // cpb
    nx = len(extras)
    ers = [e.shape[0] // NDEV for e in extras]
    er = sum(ers)
    srows = rows + er
    brows = 2 * cpb * srows
    groups = [(0, srows - tail, F32 if narrow is None else narrow)] + ([(srows - tail, tail, F32)] if tail else [])
    ng = len(groups)
    mid = min(nt - 1, max(1, nt // 4))

    def flip_of(p):
        return jnp.where(p == 0, 2, jnp.where(p == 1, 3, jnp.where(p == 2, 1, 0)))

    def block_col(b):
        x, y, _ = _my_pos()
        mine = 2 * x + y
        if cpb == 1:
            return mine ^ flip_of(b)
        return jnp.where(b == 0, (mine >> 1) ^ 1, mine >> 1)

    def body(*refs):
        l_ref, r_ref = refs[:2]
        x_refs = refs[2:2 + nx]
        rest = refs[2 + nx:]
        town_ref, ld2d_ref = rest[:2]
        lici_refs = rest[2:2 + ng]
        acc, stage = rest[2 + ng:4 + ng]
        send_bufs = rest[4 + ng:4 + 2 * ng]
        dsend, drecv, isend, irecv, xsem = rest[4 + 2 * ng:]
        b = pl.program_id(0)
        i = pl.program_id(1)
        x, y, c = _my_pos()
        mine = 2 * x + y
        sibling = (x, y, 1 - c)

        def chip_at(p):
            return mine ^ _CHIP_FLIPS[p]

        def slab_rows(p, parity):
            within = 0 if cpb == 1 else (chip_at(p) & 1) * 2
            return pl.ds(pl.multiple_of((within + parity) * srows, 8), srows)

        def push(p, slot):
            return pltpu.make_async_remote_copy(
                src_ref=acc.at[slot, slab_rows(p, 1 - c), :], dst_ref=ld2d_ref.at[p],
                send_sem=dsend.at[p], recv_sem=drecv.at[p], device_id=sibling, device_id_type=MESH)

        def ici(p):
            ch = chip_at(p)
            return [pltpu.make_async_remote_copy(
                src_ref=send_bufs[g].at[p % 2], dst_ref=lici_refs[g].at[p], send_sem=isend.at[3 * g + p],
                recv_sem=irecv.at[3 * g + p], device_id=(ch >> 1, ch & 1, c), device_id_type=MESH) for g in range(ng)]

        def extra_loads(p, slot):
            copies = []
            within = 0 if cpb == 1 else (chip_at(p) & 1) * 2
            for parity in range(2):
                off = rows
                for n, (x_ref, e) in enumerate(zip(x_refs, ers)):
                    src = x_ref.at[pl.ds(pl.multiple_of((2 * chip_at(p) + parity) * e, 8), e), :]
                    dst = acc.at[slot, pl.ds(pl.multiple_of((within + parity) * srows + off, 8), e), :]
                    copies.append(pltpu.make_async_copy(src, dst, xsem.at[(p * 2 + parity) * nx + n]))
                    off += e
            return copies

        def combine(p, slot):
            push(p, slot).wait_recv()
            pltpu.sync_copy(ld2d_ref.at[p], stage)
            total = acc[slot, slab_rows(p, c), :] + stage[...]
            if p == 3:
                stage[...] = total
                pltpu.sync_copy(stage, town_ref)
            else:
                if p == 2:
                    for cp in ici(0):
                        cp.wait_send()
                for g, (r0, n, dt) in enumerate(groups):
                    send_bufs[g][p % 2] = total[r0:r0 + n, :].astype(dt)
                for cp in ici(p):
                    cp.start()

        for bb in range(nblk):
            slot = bb % 2
            positions = list(range(bb * cpb, (bb + 1) * cpb))

            @pl.when(jnp.logical_and(b == bb, i == 0))
            def _(bb=bb, slot=slot, positions=positions):
                if bb >= 2:
                    for p in range((bb - 2) * cpb, (bb - 1) * cpb):
                        push(p, slot).wait_send()
                for q in range(2 * cpb):
                    acc[slot, pl.ds(q * srows, rows), :] = jnp.zeros((rows, D), F32)
                for p in positions:
                    for cp in extra_loads(p, slot):
                        cp.start()

            if bb >= 1:
                @pl.when(jnp.logical_and(b == bb, i == mid))
                def _(bb=bb):
                    for p in range((bb - 1) * cpb, bb * cpb):
                        combine(p, (bb - 1) % 2)

        res = _tn(l_ref[...].astype(_MXU), r_ref[...].astype(_MXU))
        slot_now = b % 2
        for q in range(2 * cpb):
            acc[slot_now, pl.ds(q * srows, rows), pl.ds(0, k)] += res[q * rows:(q + 1) * rows, :]

        for bb in range(nblk):
            slot = bb % 2
            positions = list(range(bb * cpb, (bb + 1) * cpb))

            @pl.when(jnp.logical_and(b == bb, i == nt - 1))
            def _(bb=bb, slot=slot, positions=positions):
                for p in positions:
                    for cp in extra_loads(p, slot):
                        cp.wait()
                for p in positions:
                    push(p, slot).start()
                if bb == nblk - 1:
                    for p in positions:
                        combine(p, slot)
                    for p in range(max(0, (nblk - 2) * cpb), 4):
                        push(p, slot).wait_send()
                    for p in range(1, 3):
                        for cp in ici(p):
                            cp.wait_send()
                    for p in range(3):
                        for cp in ici(p):
                            cp.wait_recv()

    in_specs = [pl.BlockSpec((tm, 2 * cpb * rows), lambda b, i: (i, block_col(b))),
                pl.BlockSpec((tm, k), lambda b, i: (i, 0))]
    any_spec = pl.BlockSpec(memory_space=pl.ANY)
    in_specs += [any_spec] * nx
    args = [lhs, rhs, *extras]
    outs = pl.pallas_call(
        body, name=name, grid=(nblk, nt),
        out_shape=(jax.ShapeDtypeStruct((srows, D), F32), jax.ShapeDtypeStruct((4, srows, D), F32))
        + tuple(jax.ShapeDtypeStruct((3, n, D), dt) for _, n, dt in groups),
        in_specs=in_specs, out_specs=(any_spec,) * (2 + ng),
        scratch_shapes=[pltpu.VMEM((2, brows, D), F32), pltpu.VMEM((srows, D), F32)]
        + [pltpu.VMEM((2, n, D), dt) for _, n, dt in groups]
        + [pltpu.SemaphoreType.DMA((4,)), pltpu.SemaphoreType.DMA((4,)), pltpu.SemaphoreType.DMA((3 * ng,)),
           pltpu.SemaphoreType.DMA((3 * ng,)), pltpu.SemaphoreType.DMA((max(1, 8 * nx),))],
        compiler_params=_cparams(2),
    )(*args)
    t_own = outs[0]
    return [(t_own[r0:r0 + n], landed) for (r0, n, _), landed in zip(groups, outs[2:])]


def _inproj_fwd(x, g1, gw, own_second):
    s = x.shape[0]
    tm = min(TM, s)
    nt = s // tm
    nchunk = 4
    cw = NIN // nchunk
    rows2, cols2 = own_second.shape

    def body(x_ref, g1_ref, gw_ref, own_ref, u_ref, z_ref, gw2_ref, w_vmem, stage, sems, send_sems, recv_sems,
             local_sem):
        i = pl.program_id(0)
        send_mine, pass_on, finish = _gather_phases(own_ref, gw2_ref, stage, send_sems, recv_sems, local_sem)

        @pl.when(i == 0)
        def _():
            send_mine()
            _load_weights(gw_ref, [("win", w_vmem, D)], sems)

        @pl.when(i == nt // 2)
        def _():
            pass_on((0, 1))

        @pl.when(i == (7 * nt) // 8)
        def _():
            pass_on((2,))

        xv = x_ref[...]
        inv = lax.rsqrt(jnp.mean(xv * xv, axis=-1, keepdims=True) + EPS)
        u = (xv * inv * g1_ref[...]).astype(_MXU)
        u_ref[...] = u
        for ch in range(nchunk):
            z_ref[:, pl.ds(ch * cw, cw)] = _nt(u, w_vmem[pl.ds(ch * cw, cw), :])

        @pl.when(i == nt - 1)
        def _():
            finish()

    any_spec = pl.BlockSpec(memory_space=pl.ANY)
    return pl.pallas_call(
        body, name="inproj_fwd", grid=(nt,),
        out_shape=(jax.ShapeDtypeStruct((s, D), _MXU), jax.ShapeDtypeStruct((s, NIN), F32),
                   jax.ShapeDtypeStruct((NDEV, rows2, cols2), own_second.dtype)),
        in_specs=[pl.BlockSpec((tm, D), lambda i: (i, 0)), pl.BlockSpec((1, D), lambda i: (0, 0)), any_spec, any_spec],
        out_specs=(pl.BlockSpec((tm, D), lambda i: (i, 0)), pl.BlockSpec((tm, NIN), lambda i: (i, 0)), any_spec),
        scratch_shapes=[pltpu.VMEM((NIN, D), _MXU), pltpu.VMEM((rows2, cols2), own_second.dtype),
                        pltpu.SemaphoreType.DMA((NDEV,)), pltpu.SemaphoreType.DMA((7,)), pltpu.SemaphoreType.DMA((7,)),
                        pltpu.SemaphoreType.DMA],
        compiler_params=_cparams(1),
    )(x, g1, gw, own_second)


def _pool_tile(pbuf, t0, tm, pw_ref, scale_ref):
    t = t0 + lax.broadcasted_iota(jnp.int32, (tm, GD), 0)
    pooled, mixed_pre = [], []
    for g, w in enumerate(WINDOWS):
        cs = pl.ds(g * GD, GD)
        cur = pbuf[pl.ds(HALO, tm), cs]
        acc = cur
        for d in range(1, w):
            acc = acc + pbuf[pl.ds(HALO - d, tm), cs]
        cnt = jnp.minimum(t + 1, w).astype(F32)
        pg = acc / cnt - cur
        pooled.append(pg)
        mixed_pre.append(_nn(pg.astype(_MXU), pw_ref[g]))
    return pooled, mixed_pre


def _lru_gates_head(hh, lbuf, start, tm, cw_ref, cb_ref, wrg_ref, brg_ref, wig_ref, big_ref, sp):
    cs = pl.ds(hh * HD, HD)
    xc = cb_ref[:, cs] + cw_ref[pl.ds(CONV - 1, 1), cs] * lbuf[pl.ds(HALO, tm), cs]
    for k in range(CONV - 1):
        xc = xc + cw_ref[pl.ds(k, 1), cs] * lbuf[pl.ds(HALO - (CONV - 1) + k, tm), cs]
    xcm = xc.astype(_MXU)
    r = _sigmoid(_nn(xcm, wrg_ref[hh]) + brg_ref[pl.ds(hh, 1), :])
    ig = _sigmoid(_nn(xcm, wig_ref[hh]) + big_ref[pl.ds(hh, 1), :])
    a = jnp.exp(-LRU_C * r * sp[:, hh * HD:(hh + 1) * HD])
    one_m = 1.0 - a * a
    live = jnp.logical_and(one_m > 0.0, jnp.logical_not(start))
    inv_mult = lax.rsqrt(jnp.where(live, one_m, 1.0))
    mult = jnp.where(live, one_m * inv_mult, jnp.where(start, 1.0, 0.0))
    return xc, r, ig, a, live, inv_mult, mult


def _seg_layout(tm):
    seg = tm // 8
    return seg, seg + 8


def _to_segments(dst_ref, hh, val, tm):
    seg, pitch = _seg_layout(tm)
    for s in range(8):
        dst_ref[hh, pl.ds(s * pitch, seg), :] = val[s * seg:(s + 1) * seg, :]


def _from_segments(src_ref, hh, tm):
    seg, pitch = _seg_layout(tm)
    return jnp.concatenate([src_ref[hh, pl.ds(s * pitch, seg), :] for s in range(8)], axis=0)


def _segment_scan(a_ref, b_ref, out_ref, hk, pk, carry_ref, tm, reverse):
    seg, pitch = _seg_layout(tm)
    row = lax.broadcasted_iota(jnp.int32, (8, HD), 0)
    order = range(seg - 1, -1, -1) if reverse else range(seg)
    for hh in range(HEADS):
        cs = pl.ds(hh * HD, HD)
        if reverse:
            a0 = a_ref[hh, pl.ds(0, 8, stride=pitch), :]
            a_wrap = jnp.where(row <= 6, pltpu.roll(a0, 7, 0), 1.0)
        hv = jnp.zeros((8, HD), F32)
        pv = jnp.ones((8, HD), F32)
        for k in order:
            if not reverse:
                av = a_ref[hh, pl.ds(k, 8, stride=pitch), :]
            elif k + 1 < seg:
                av = a_ref[hh, pl.ds(k + 1, 8, stride=pitch), :]
            else:
                av = a_wrap
            hv = av * hv + b_ref[hh, pl.ds(k, 8, stride=pitch), :]
            pv = av * pv
            hk[hh, pl.ds(8 * k, 8), :] = hv
            pk[hh, pl.ds(8 * k, 8), :] = pv
        for d in (1, 2, 4):
            if reverse:
                keep, sh = row < 8 - d, 8 - d
            else:
                keep, sh = row >= d, d
            hv = hv + pv * jnp.where(keep, pltpu.roll(hv, sh, 0), 0.0)
            pv = pv * jnp.where(keep, pltpu.roll(pv, sh, 0), 1.0)
        cin = carry_ref[:, cs]
        ends = hv + pv * cin
        if reverse:
            enter = jnp.where(row <= 6, pltpu.roll(ends, 7, 0), cin)
            carry_ref[:, cs] = jnp.broadcast_to((a0 * ends)[0:1, :], (8, HD))
        else:
            enter = jnp.where(row >= 1, pltpu.roll(ends, 1, 0), cin)
            carry_ref[:, cs] = jnp.broadcast_to(ends[7:8, :], (8, HD))
        for k in range(seg):
            out_ref[hh, pl.ds(k, 8, stride=pitch), :] = hk[hh, pl.ds(8 * k, 8), :] + pk[hh, pl.ds(8 * k, 8), :] * enter


def _mixer_fwd(z, x, gw, small):
    s = x.shape[0]
    tm = min(TM_SEQ, s)
    (pool_w, pool_scale, conv_w, conv_b, w_rg, b_rg, w_ig, b_ig, lam, b_gate) = small

    def body(z_ref, x_ref, gw_ref, pw_ref, ps_ref, cw_ref, cb_ref, wrg_ref, brg_ref, wig_ref, big_ref, lam_ref,
             bg_ref, h_ref, yl_ref, mg_ref, yp_ref, yr_ref, h1_ref, a_ref, r_ref, ig_ref, xc_ref,
             pprojT, lru_w, wout_w, pbuf, lbuf, a_s, b_s, h_s, hk, pk, hcar, sems):
        i = pl.program_id(0)
        t0 = i * tm

        @pl.when(i == 0)
        def _():
            _load_weights(gw_ref, [("pproj", pprojT, PW), ("lru", lru_w, D), ("wout", wout_w, D)], sems)
            pbuf[pl.ds(0, HALO), :] = jnp.zeros((HALO, PW), F32)
            lbuf[pl.ds(0, HALO), :] = jnp.zeros((HALO, D), F32)
            hcar[...] = jnp.zeros_like(hcar)

        pbuf[pl.ds(HALO, tm), :] = z_ref[:, pl.ds(0, PW)]
        _, mixed_pre = _pool_tile(pbuf, t0, tm, pw_ref, ps_ref)
        mixed = jnp.concatenate(mixed_pre, axis=1) * ps_ref[...]
        y_pool = _nt(mixed.astype(_MXU), pprojT[...])
        pbuf[pl.ds(0, HALO), :] = pbuf[pl.ds(tm, HALO), :]

        lbuf[pl.ds(HALO, tm), :] = z_ref[:, pl.ds(PW, D)]
        sp, _ = _softplus_neg(lam_ref[...])
        start = (t0 + lax.broadcasted_iota(jnp.int32, (tm, HD), 0)) == 0
        for hh in range(HEADS):
            xc, r, ig, a, _, _, mult = _lru_gates_head(hh, lbuf, start, tm, cw_ref, cb_ref, wrg_ref, brg_ref,
                                                       wig_ref, big_ref, sp)
            _to_segments(a_s, hh, a, tm)
            _to_segments(b_s, hh, mult * ig * xc, tm)
            cs = pl.ds(hh * HD, HD)
            a_ref[:, cs] = a
            r_ref[:, cs] = r.astype(_MXU)
            ig_ref[:, cs] = ig.astype(_MXU)
            xc_ref[:, cs] = xc.astype(_MXU)
        lbuf[pl.ds(0, HALO), :] = lbuf[pl.ds(tm, HALO), :]
        _segment_scan(a_s, b_s, h_s, hk, pk, hcar, tm, reverse=False)
        for hh in range(HEADS):
            h_ref[:, pl.ds(hh * HD, HD)] = _from_segments(h_s, hh, tm)
        gel, _ = _gelu_and_grad(z_ref[:, pl.ds(PW + D, D)])
        yl = (h_ref[...] * gel).astype(_MXU)
        yl_ref[...] = yl
        y_lru = _nn(yl, lru_w[...])

        g0 = _sigmoid(z_ref[:, pl.ds(PW + 2 * D, D)] + bg_ref[pl.ds(0, 1), :])
        g1 = _sigmoid(z_ref[:, pl.ds(PW + 3 * D, D)] + bg_ref[pl.ds(1, 1), :])
        merged = (g0 * y_pool + g1 * y_lru).astype(_MXU)
        mg_ref[...] = merged
        yp_ref[...] = y_pool.astype(_MXU)
        yr_ref[...] = y_lru.astype(_MXU)
        h1_ref[...] = x_ref[...] + _nn(merged, wout_w[...])

    tok = lambda w, dt: jax.ShapeDtypeStruct((s, w), dt)
    tspec = lambda w: pl.BlockSpec((tm, w), lambda i: (i, 0))
    full = lambda a: pl.BlockSpec(a.shape, lambda i: (0,) * a.ndim)
    seg_buf = pltpu.VMEM((HEADS, 8 * _seg_layout(tm)[1], HD), F32)
    return pl.pallas_call(
        body, name="mixer_fwd", grid=(s // tm,),
        out_shape=(tok(D, F32), tok(D, _MXU), tok(D, _MXU), tok(D, _MXU), tok(D, _MXU), tok(D, F32),
                   tok(D, F32), tok(D, _MXU), tok(D, _MXU), tok(D, _MXU)),
        in_specs=[tspec(NIN), tspec(D), pl.BlockSpec(memory_space=pl.ANY)] + [full(a) for a in small],
        out_specs=(tspec(D),) * 10,
        scratch_shapes=[pltpu.VMEM((D, PW), _MXU), pltpu.VMEM((D, D), _MXU), pltpu.VMEM((D, D), _MXU),
                        pltpu.VMEM((tm + HALO, PW), F32), pltpu.VMEM((tm + HALO, D), F32),
                        seg_buf, seg_buf, seg_buf, pltpu.VMEM((HEADS, tm, HD), F32), pltpu.VMEM((HEADS, tm, HD), F32),
                        pltpu.VMEM((8, D), F32), pltpu.SemaphoreType.DMA((3 * NDEV,))],
        compiler_params=_cparams(1),
    )(z, x, gw, *small)


def _ffn_fwd(h1, g2, gw):
    s = h1.shape[0]
    tm = min(TM, s)
    half = FF // 2

    def body(h1_ref, g2_ref, gw_ref, v_ref, gf_ref, uf_ref, h2_ref, wffnT, wffo, sems):
        @pl.when(pl.program_id(0) == 0)
        def _():
            _load_weights(gw_ref, [("wffn", wffnT, D), ("wffo", wffo, D)], sems)

        hv = h1_ref[...]
        inv = lax.rsqrt(jnp.mean(hv * hv, axis=-1, keepdims=True) + EPS)
        v = (hv * inv * g2_ref[...]).astype(_MXU)
        v_ref[...] = v
        acc = hv
        for ch in range(2):
            cs = pl.ds(ch * half, half)
            gf = _nt(v, wffnT[pl.ds(ch * half, half), :]).astype(_MXU)
            uf = _nt(v, wffnT[pl.ds(FF + ch * half, half), :]).astype(_MXU)
            gf_ref[:, cs] = gf
            uf_ref[:, cs] = uf
            gf32 = gf.astype(F32)
            act = (gf32 * _sigmoid(gf32) * uf.astype(F32)).astype(_MXU)
            acc = acc + _nn(act, wffo[pl.ds(ch * half, half), :])
        h2_ref[...] = acc

    tspec = lambda w: pl.BlockSpec((tm, w), lambda i: (i, 0))
    return pl.pallas_call(
        body, name="ffn_fwd", grid=(s // tm,),
        out_shape=(jax.ShapeDtypeStruct((s, D), _MXU), jax.ShapeDtypeStruct((s, FF), _MXU),
                   jax.ShapeDtypeStruct((s, FF), _MXU), jax.ShapeDtypeStruct((s, D), F32)),
        in_specs=[tspec(D), pl.BlockSpec((1, D), lambda i: (0, 0)), pl.BlockSpec(memory_space=pl.ANY)],
        out_specs=(tspec(D), tspec(FF), tspec(FF), tspec(D)),
        scratch_shapes=[pltpu.VMEM((2 * FF, D), _MXU), pltpu.VMEM((FF, D), _MXU), pltpu.SemaphoreType.DMA((2 * NDEV,))],
        compiler_params=_cparams(1),
    )(h1, g2, gw)


def _rms_bwd(dy, xn, inv, g):
    dg = jnp.sum(dy * xn, axis=0, keepdims=True)
    dxn = dy * g
    dx = inv * (dxn - xn * jnp.mean(dxn * xn, axis=-1, keepdims=True))
    return dx, dg


def _ple_loss_fwd_bwd(h2, p, target, g3, gfin, gw):
    s = h2.shape[0]
    tm = min(TM, s)

    def body(h2_ref, p_ref, t_ref, g3_ref, gf_ref, gw_ref,
             dh2_ref, n3_ref, dpg_ref, de_ref, loss_ref, dg3_ref, dgf_ref, wpg, pleT, sems):
        i = pl.program_id(0)

        @pl.when(i == 0)
        def _():
            _load_weights(gw_ref, [("wpg", wpg, D), ("ple", pleT, PLE)], sems)
            loss_ref[...] = jnp.zeros_like(loss_ref)
            dg3_ref[...] = jnp.zeros_like(dg3_ref)
            dgf_ref[...] = jnp.zeros_like(dgf_ref)

        hv = h2_ref[...]
        inv3 = lax.rsqrt(jnp.mean(hv * hv, axis=-1, keepdims=True) + EPS)
        xn3 = hv * inv3
        n3 = (xn3 * g3_ref[...]).astype(_MXU)
        n3_ref[...] = n3
        pg = _sigmoid(_nn(n3, wpg[...]))
        e = _nt(p_ref[...].astype(_MXU), pleT[...])
        h3 = hv + pg * e
        invf = lax.rsqrt(jnp.mean(h3 * h3, axis=-1, keepdims=True) + EPS)
        xf = h3 * invf
        diff = xf * gf_ref[...] - t_ref[...]
        loss_ref[...] += jnp.sum(diff * diff) * (0.5 / D)
        dh3, dgf = _rms_bwd(diff * (1.0 / D), xf, invf, gf_ref[...])
        dgf_ref[...] += dgf
        de_ref[...] = (dh3 * pg).astype(_MXU)
        dpg = (dh3 * e * pg * (1.0 - pg)).astype(_MXU)
        dpg_ref[...] = dpg
        dn3 = _nt(dpg, wpg[...])
        dx3, dg3 = _rms_bwd(dn3, xn3, inv3, g3_ref[...])
        dg3_ref[...] += dg3
        dh2_ref[...] = dh3 + dx3

    tspec = lambda w: pl.BlockSpec((tm, w), lambda i: (i, 0))
    vec = pl.BlockSpec((1, D), lambda i: (0, 0))
    tok = lambda w, dt: jax.ShapeDtypeStruct((s, w), dt)
    return pl.pallas_call(
        body, name="ple_loss", grid=(s // tm,),
        out_shape=(tok(D, F32), tok(D, _MXU), tok(D, _MXU), tok(D, _MXU), jax.ShapeDtypeStruct((8, 128), F32),
                   jax.ShapeDtypeStruct((1, D), F32), jax.ShapeDtypeStruct((1, D), F32)),
        in_specs=[tspec(D), tspec(PLE), tspec(D), vec, vec, pl.BlockSpec(memory_space=pl.ANY)],
        out_specs=(tspec(D), tspec(D), tspec(D), tspec(D), pl.BlockSpec((8, 128), lambda i: (0, 0)), vec, vec),
        scratch_shapes=[pltpu.VMEM((D, D), _MXU), pltpu.VMEM((D, PLE), _MXU), pltpu.SemaphoreType.DMA((2 * NDEV,))],
        compiler_params=_cparams(1),
    )(h2, p, target, g3, gfin, gw)


def _ffn_bwd_hidden(dh2, gf, uf, gw):
    s = dh2.shape[0]
    tm = min(TM, s)
    half = FF // 2

    def body(dh2_ref, gf_ref, uf_ref, gw_ref, dff_ref, act_ref, wffo, sems):
        @pl.when(pl.program_id(0) == 0)
        def _():
            _load_weights(gw_ref, [("wffo", wffo, D)], sems)

        dm = dh2_ref[...].astype(_MXU)
        for ch in range(2):
            cs = pl.ds(ch * half, half)
            dact = _nt(dm, wffo[pl.ds(ch * half, half), :])
            gfv = gf_ref[:, cs].astype(F32)
            ufv = uf_ref[:, cs].astype(F32)
            sg = _sigmoid(gfv)
            silu = gfv * sg
            act_ref[:, cs] = (silu * ufv).astype(_MXU)
            dff_ref[:, pl.ds(ch * half, half)] = (dact * ufv * (sg * (1.0 + gfv * (1.0 - sg)))).astype(_MXU)
            dff_ref[:, pl.ds(FF + ch * half, half)] = (dact * silu).astype(_MXU)

    tspec = lambda w: pl.BlockSpec((tm, w), lambda i: (i, 0))
    return pl.pallas_call(
        body, name="ffn_bwd_hidden", grid=(s // tm,),
        out_shape=(jax.ShapeDtypeStruct((s, 2 * FF), _MXU), jax.ShapeDtypeStruct((s, FF), _MXU)),
        in_specs=[tspec(D), tspec(FF), tspec(FF), pl.BlockSpec(memory_space=pl.ANY)],
        out_specs=(tspec(2 * FF), tspec(FF)),
        scratch_shapes=[pltpu.VMEM((FF, D), _MXU), pltpu.SemaphoreType.DMA((NDEV,))],
        compiler_params=_cparams(1),
    )(dh2, gf, uf, gw)


def _proj_norm_bwd(dy, x, dres, g, gw, slab, width, name):
    s = x.shape[0]
    tm = min(TM, s)

    def body(dy_ref, x_ref, dr_ref, g_ref, gw_ref, dx_ref, dg_ref, wT, sems):
        @pl.when(pl.program_id(0) == 0)
        def _():
            _load_weights(gw_ref, [(slab, wT, D)], sems)
            dg_ref[...] = jnp.zeros_like(dg_ref)

        dv = _nn(dy_ref[...], wT[...])
        xv = x_ref[...]
        inv = lax.rsqrt(jnp.mean(xv * xv, axis=-1, keepdims=True) + EPS)
        dx, dg = _rms_bwd(dv, xv * inv, inv, g_ref[...])
        dg_ref[...] += dg
        dx_ref[...] = dr_ref[...] + dx

    tspec = lambda w: pl.BlockSpec((tm, w), lambda i: (i, 0))
    vec = pl.BlockSpec((1, D), lambda i: (0, 0))
    return pl.pallas_call(
        body, name=name, grid=(s // tm,),
        out_shape=(jax.ShapeDtypeStruct((s, D), F32), jax.ShapeDtypeStruct((1, D), F32)),
        in_specs=[tspec(width), tspec(D), tspec(D), vec, pl.BlockSpec(memory_space=pl.ANY)],
        out_specs=(tspec(D), vec),
        scratch_shapes=[pltpu.VMEM((width, D), _MXU), pltpu.SemaphoreType.DMA((NDEV,))],
        compiler_params=_cparams(1),
    )(dy, x, dres, g, gw)


def _mixer_bwd(dh1, z, h, y_pool, y_lru, saved, gw, small):
    s = dh1.shape[0]
    tm = min(TM_SEQ, s)
    nt = s // tm
    (pool_w, pool_scale, conv_w, conv_b, w_rg, b_rg, w_ig, b_ig, lam, b_gate) = small

    def body(dh1_ref, z_ref, zp_ref, h_ref, hp_ref, yp_ref, yr_ref, a_ref, r_ref, ig_ref, xc_ref, gw_ref,
             pw_ref, ps_ref, cw_ref, cb_ref, wrg_ref, brg_ref, wig_ref, big_ref, lam_ref, bg_ref,
             dz_ref, dyr_ref, dyp_ref, mx_ref,
             gbg_ref, glam_ref, gbrg_ref, gbig_ref, gcb_ref, gcw_ref, gps_ref, gpw_ref, gwrg_ref, gwig_ref,
             pprojT, lru_w, wout_w, pbuf, hbuf, qbuf, xbuf, a_s, g_s, dh_s, hk, pk, dcar, sems):
        step = pl.program_id(0)
        i = nt - 1 - step
        t0 = i * tm

        @pl.when(step == 0)
        def _():
            _load_weights(gw_ref, [("pproj", pprojT, PW), ("lru", lru_w, D), ("wout", wout_w, D)], sems)
            for ref in (gbg_ref, glam_ref, gbrg_ref, gbig_ref, gcb_ref, gcw_ref, gps_ref, gpw_ref, gwrg_ref, gwig_ref):
                ref[...] = jnp.zeros_like(ref)
            qbuf[pl.ds(tm, HALO), :] = jnp.zeros((HALO, PW), F32)
            xbuf[pl.ds(tm, 8), :] = jnp.zeros((8, D), F32)
            dcar[...] = jnp.zeros_like(dcar)

        first = i == 0
        zprev = jnp.where(first, 0.0, zp_ref[...])
        hprev = jnp.where(first, 0.0, hp_ref[...])

        d_merged = _nt(dh1_ref[...].astype(_MXU), wout_w[...])

        g0 = _sigmoid(z_ref[:, pl.ds(PW + 2 * D, D)] + bg_ref[pl.ds(0, 1), :])
        g1 = _sigmoid(z_ref[:, pl.ds(PW + 3 * D, D)] + bg_ref[pl.ds(1, 1), :])
        dz0 = d_merged * yp_ref[...].astype(F32) * g0 * (1.0 - g0)
        dz1 = d_merged * yr_ref[...].astype(F32) * g1 * (1.0 - g1)
        dz_ref[:, pl.ds(PW + 2 * D, D)] = dz0.astype(_MXU)
        dz_ref[:, pl.ds(PW + 3 * D, D)] = dz1.astype(_MXU)
        gbg_ref[pl.ds(0, 1), :] += jnp.sum(dz0, axis=0, keepdims=True)
        gbg_ref[pl.ds(1, 1), :] += jnp.sum(dz1, axis=0, keepdims=True)
        d_ypool = (d_merged * g0).astype(_MXU)
        d_ylru = (d_merged * g1).astype(_MXU)
        dyp_ref[...] = d_ypool
        dyr_ref[...] = d_ylru

        d_yl = _nt(d_ylru, lru_w[...])
        gel, dgel = _gelu_and_grad(z_ref[:, pl.ds(PW + D, D)])
        dz_ref[:, pl.ds(PW + D, D)] = (d_yl * h_ref[...] * dgel).astype(_MXU)
        g_full = d_yl * gel
        hbuf[pl.ds(0, 8), :] = hprev
        hbuf[pl.ds(8, tm), :] = h_ref[...]
        sp, sneg = _softplus_neg(lam_ref[...])
        start = (t0 + lax.broadcasted_iota(jnp.int32, (tm, HD), 0)) == 0
        for hh in range(HEADS):
            cs = pl.ds(hh * HD, HD)
            _to_segments(a_s, hh, a_ref[:, cs], tm)
            _to_segments(g_s, hh, g_full[:, hh * HD:(hh + 1) * HD], tm)
        _segment_scan(a_s, g_s, dh_s, hk, pk, dcar, tm, reverse=True)
        for hh in range(HEADS):
            cs = pl.ds(hh * HD, HD)
            a = a_ref[:, cs]
            r = r_ref[:, cs].astype(F32)
            ig = ig_ref[:, cs].astype(F32)
            xc = xc_ref[:, cs].astype(F32)
            a2 = a * a
            one_m = 1.0 - a2
            live = jnp.logical_and(one_m > 0.0, jnp.logical_not(start))
            inv_mult = lax.rsqrt(jnp.where(live, one_m, 1.0))
            mult = jnp.where(live, one_m * inv_mult, jnp.where(start, 1.0, 0.0))
            dh = _from_segments(dh_s, hh, tm)
            d_mult = dh * ig * xc
            d_loga = dh * hbuf[pl.ds(7, tm), cs] * a - jnp.where(live, d_mult * a2 * inv_mult, 0.0)
            glam_ref[:, cs] += jnp.sum(d_loga * (LRU_C * r) * sneg[:, hh * HD:(hh + 1) * HD], axis=0, keepdims=True)
            d_rpre = d_loga * (-LRU_C * sp[:, hh * HD:(hh + 1) * HD]) * r * (1.0 - r)
            d_igpre = dh * mult * xc * ig * (1.0 - ig)
            gbrg_ref[pl.ds(hh, 1), :] += jnp.sum(d_rpre, axis=0, keepdims=True)
            gbig_ref[pl.ds(hh, 1), :] += jnp.sum(d_igpre, axis=0, keepdims=True)
            drm = d_rpre.astype(_MXU)
            dim = d_igpre.astype(_MXU)
            xcm = xc_ref[:, cs]
            gwrg_ref[hh] += _tn(xcm, drm)
            gwig_ref[hh] += _tn(xcm, dim)
            d_xc = dh * mult * ig + _nt(drm, wrg_ref[hh]) + _nt(dim, wig_ref[hh])
            gcb_ref[:, cs] += jnp.sum(d_xc, axis=0, keepdims=True)
            xbuf[pl.ds(0, tm), cs] = d_xc
        zl = z_ref[:, pl.ds(PW, D)]
        dzl = None
        for k in range(CONV):
            shifted = xbuf[pl.ds(CONV - 1 - k, tm), :]
            term = cw_ref[pl.ds(k, 1), :] * shifted
            dzl = term if dzl is None else dzl + term
            gcw_ref[pl.ds(k, 1), :] += jnp.sum(shifted * zl, axis=0, keepdims=True)
        dz_ref[:, pl.ds(PW, D)] = dzl.astype(_MXU)
        xbuf[pl.ds(tm, 8), :] = xbuf[pl.ds(0, 8), :]

        d_mixed = _nn(d_ypool, pprojT[...])
        pbuf[pl.ds(0, HALO), :] = zprev
        pbuf[pl.ds(HALO, tm), :] = z_ref[:, pl.ds(0, PW)]
        pooled, mixed_pre = _pool_tile(pbuf, t0, tm, pw_ref, ps_ref)
        mp = jnp.concatenate(mixed_pre, axis=1)
        mx_ref[...] = (mp * ps_ref[...]).astype(_MXU)
        gps_ref[...] += jnp.sum(d_mixed * mp, axis=0, keepdims=True)
        d_mp = (d_mixed * ps_ref[...]).astype(_MXU)
        t = t0 + lax.broadcasted_iota(jnp.int32, (tm, GD), 0)
        d_pooled = []
        for g, w in enumerate(WINDOWS):
            dmg = d_mp[:, g * GD:(g + 1) * GD]
            gpw_ref[g] += _tn(pooled[g].astype(_MXU), dmg)
            dp = _nt(dmg, pw_ref[g])
            d_pooled.append(dp)
            qbuf[pl.ds(0, tm), pl.ds(g * GD, GD)] = dp / jnp.minimum(t + 1, w).astype(F32)
        for g, w in enumerate(WINDOWS):
            cs = pl.ds(g * GD, GD)
            acc = qbuf[pl.ds(0, tm), cs]
            for d in range(1, w):
                acc = acc + qbuf[pl.ds(d, tm), cs]
            dz_ref[:, cs] = (acc - d_pooled[g]).astype(_MXU)
        qbuf[pl.ds(tm, HALO), :] = qbuf[pl.ds(0, HALO), :]

    rev = lambda w: pl.BlockSpec((tm, w), lambda g: (nt - 1 - g, 0))
    prev = lambda rows, w: pl.BlockSpec((rows, w), lambda g: (jnp.maximum((nt - 1 - g) * (tm // rows) - 1, 0), 0))
    full = lambda a: pl.BlockSpec(a.shape, lambda g: (0,) * a.ndim)
    tok = lambda w, dt: jax.ShapeDtypeStruct((s, w), dt)
    acc_shapes = [(2, D), (1, D), (HEADS, HD), (HEADS, HD), (1, D), (CONV, D), (1, PW), (GROUPS, GD, GD),
                  (HEADS, HD, HD), (HEADS, HD, HD)]
    acc_specs = tuple(pl.BlockSpec(sh, lambda g, n=len(sh): (0,) * n) for sh in acc_shapes)
    seg_buf = pltpu.VMEM((HEADS, 8 * _seg_layout(tm)[1], HD), F32)
    a_in, r_in, ig_in, xc_in = saved
    return pl.pallas_call(
        body, name="mixer_bwd", grid=(nt,),
        out_shape=(tok(NIN, _MXU), tok(D, _MXU), tok(D, _MXU), tok(PW, _MXU))
        + tuple(jax.ShapeDtypeStruct(sh, F32) for sh in acc_shapes),
        in_specs=[rev(D), rev(NIN), prev(HALO, PW), rev(D), prev(8, D), rev(D), rev(D), rev(D), rev(D), rev(D), rev(D),
                  pl.BlockSpec(memory_space=pl.ANY)] + [full(a) for a in small],
        out_specs=(rev(NIN), rev(D), rev(D), rev(PW)) + acc_specs,
        scratch_shapes=[pltpu.VMEM((D, PW), _MXU), pltpu.VMEM((D, D), _MXU), pltpu.VMEM((D, D), _MXU),
                        pltpu.VMEM((tm + HALO, PW), F32),
                        pltpu.VMEM((tm + 8, D), F32), pltpu.VMEM((tm + HALO, PW), F32), pltpu.VMEM((tm + 8, D), F32),
                        seg_buf, seg_buf, seg_buf, pltpu.VMEM((HEADS, tm, HD), F32), pltpu.VMEM((HEADS, tm, HD), F32),
                        pltpu.VMEM((8, D), F32), pltpu.SemaphoreType.DMA((3 * NDEV,))],
        compiler_params=_cparams(1),
    )(dh1, z, z, h, h, y_pool, y_lru, a_in, r_in, ig_in, xc_in, gw, *small)


def _split3(a):
    hi = a.astype(jnp.bfloat16).astype(F32)
    mid = (a - hi).astype(jnp.bfloat16).astype(F32)
    lo = (a - hi - mid).astype(jnp.bfloat16).astype(F32)
    return jnp.stack([hi, mid, lo])


def _small_pack(parts):
    flat = jnp.concatenate([a.reshape(-1) for a in parts])
    return jnp.pad(flat, (0, NDEV * SMALL_ROWS * D - flat.shape[0])).reshape(NDEV * SMALL_ROWS, D)


def _small_unpack(packed, shapes):
    flat = packed.reshape(-1)
    out, o = [], 0
    for sh in shapes:
        n = math.prod(sh)
        out.append(flat[o:o + n].reshape(sh))
        o += n
    return out


def kernel(x, p, norm1_g, w_in, b_gate, pool_w, pool_scale, pool_proj, conv_w, conv_b, w_rg, b_rg, w_ig, b_ig, lru_lambda, lru_proj, w_out, norm2_g, w_ffn_in, w_ffn_out, ple_norm_g, w_ple_gate, w_ple_proj, final_g, loss_target, m_norm1_g, m_w_in, m_b_gate, m_pool_w, m_pool_scale, m_pool_proj, m_conv_w, m_conv_b, m_w_rg, m_b_rg, m_w_ig, m_b_ig, m_lru_lambda, m_lru_proj, m_w_out, m_norm2_g, m_w_ffn_in, m_w_ffn_out, m_ple_norm_g, m_w_ple_gate, m_w_ple_proj, m_final_g, v_norm1_g, v_w_in, v_b_gate, v_pool_w, v_pool_scale, v_pool_proj, v_conv_w, v_conv_b, v_w_rg, v_b_rg, v_w_ig, v_b_ig, v_lru_lambda, v_lru_proj, v_w_out, v_norm2_g, v_w_ffn_in, v_w_ffn_out, v_ple_norm_g, v_w_ple_gate, v_w_ple_proj, v_final_g):
    axes = ("x", "y", "c")
    me = 4 * lax.axis_index("x") + 2 * lax.axis_index("y") + lax.axis_index("c")
    x2 = x[0]
    p2 = p[0, 0]
    tgt = loss_target[0]

    n_small = (CONV + 2) * 128
    small_terms = _split3(jnp.concatenate([conv_w[0].reshape(-1), b_gate[0].reshape(-1)]))
    small_rows = jnp.pad(small_terms, ((0, 16 - 3), (0, D - n_small)))
    own_first = jnp.concatenate([w_in[0].T.astype(_MXU), small_rows.astype(_MXU)], axis=0)
    own_second = jnp.concatenate([
        w_ffn_in[0].T.astype(_MXU),
        jnp.pad(pool_proj[0].T, ((0, 0), (0, D - PW))).astype(_MXU),
        jnp.pad(w_ple_proj[0].T, ((0, 0), (0, D - PLE))).astype(_MXU),
        lru_proj[0].astype(_MXU), w_out[0].astype(_MXU), w_ffn_out[0].astype(_MXU), w_ple_gate[0].astype(_MXU),
    ], axis=0)
    gw_first = _all_gather_weights(own_first)
    off = W_OFF["f32s"][0]
    st = gw_first[:, off:off + 3, :n_small].astype(F32)
    sf = st[:, 0] + st[:, 1] + st[:, 2]
    conv_w_full = sf[:, :CONV * 128].reshape(NDEV, CONV, 128).transpose(1, 0, 2).reshape(CONV, D)
    b_gate_full = sf[:, CONV * 128:].reshape(NDEV, 2, 128).transpose(1, 0, 2).reshape(2, D)

    small = (pool_w[0].astype(_MXU), pool_scale, conv_w_full, conv_b, w_rg[0].astype(_MXU), b_rg[0],
             w_ig[0].astype(_MXU), b_ig[0], lru_lambda, b_gate_full)

    u, z, gw = _inproj_fwd(x2, norm1_g, gw_first, own_second)
    h, yl, merged, y_pool, y_lru, h1, *saved = _mixer_fwd(z, x2, gw, small)
    v, gf, uf, h2 = _ffn_fwd(h1, norm2_g, gw)

    dh2, n3, dpg, de, loss_blk, g_ple_norm, g_final = _ple_loss_fwd_bwd(h2, p2, tgt, ple_norm_g, final_g.reshape(1, D), gw)
    dff, act = _ffn_bwd_hidden(dh2, gf, uf, gw)
    dh1, g_norm2 = _proj_norm_bwd(dff, h1, dh2, norm2_g, gw, "wffn", 2 * FF, "ffn_bwd_in")
    (dz, d_ylru, d_ypool, mixed, g_bgate, g_lam, g_brg, g_big, g_convb, g_convw, g_pscale, g_poolw, g_wrg,
     g_wig) = _mixer_bwd(dh1, z, h, y_pool, y_lru, saved, gw, small)
    grad_x, g_norm1 = _proj_norm_bwd(dz, x2, dh1, norm1_g, gw_first, "win", NIN, "inproj_bwd")

    small_shapes = [(1, D), (GROUPS, GD, GD), (1, PW), (1, D), (HEADS, HD, HD), (HEADS, HD), (HEADS, HD, HD),
                    (HEADS, HD), (1, D), (1, D), (1, D), (1, D), (2, D), (CONV, D), (1, 1)]
    small_part = _small_pack([g_norm1, g_poolw, g_pscale, g_convb, g_wrg, g_brg, g_wig, g_big, g_lam, g_norm2,
                              g_ple_norm, g_final, g_bgate, g_convw, loss_blk[0:1, 0:1]])
    riders = [_grad_matmul(n3, dpg, "grad_w_ple_gate"), _grad_matmul(yl, d_ylru, "grad_lru_proj"),
              _grad_matmul(merged, dh1, "grad_w_out"), _grad_matmul(d_ypool, mixed, "grad_pool_proj"),
              _grad_matmul(de, p2, "grad_w_ple_proj")]
    part_wffo = _grad_matmul(act, dh2, "grad_w_ffn_out")
    rs_wffn = _grad_matmul_rs(dff, v, "grad_w_ffn_in", 704, extras=[part_wffo], narrow=_MXU)
    rs_win = _grad_matmul_rs(dz, u, "grad_w_in", 576, extras=riders + [small_part], narrow=_MXU, tail=SMALL_ROWS)

    def reduced(parts, name):
        return [_sum_arrays([t_own, landed[0], landed[1], landed[2]], "rs_sum_" + name + str(n))
                for n, (t_own, landed) in enumerate(parts)]

    red_wffn, = reduced(rs_wffn, "wffn")
    red_win, red_small = reduced(rs_win, "win")
    g_w_in = red_win[:576].T
    g_w_ffn_in = red_wffn[:704].T
    g_w_ffn_out = red_wffn[704:]
    g_w_ple_gate, g_lru_proj, g_w_out = red_win[576:704], red_win[704:832], red_win[832:960]
    g_pool_proj = red_win[960:1088, :PW].T
    g_w_ple_proj = red_win[1088:1216, :PLE].T
    small_red = _all_gather_small(red_small)
    (gs_norm1, gs_poolw, gs_pscale, gs_convb, gs_wrg, gs_brg, gs_wig, gs_big, gs_lam, gs_norm2, gs_ple_norm,
     gs_final, gs_bgate, gs_convw, loss_sum) = _small_unpack(small_red, small_shapes)
    loss = loss_sum[0, 0]
    g_b_gate = lax.dynamic_slice_in_dim(gs_bgate, me * 128, 128, axis=1)
    g_conv_w = lax.dynamic_slice_in_dim(gs_convw, me * 128, 128, axis=1)

    grads = {
        "norm1_g": gs_norm1, "w_in": g_w_in[None], "b_gate": g_b_gate[None], "pool_w": gs_poolw[None],
        "pool_scale": gs_pscale, "pool_proj": g_pool_proj[None], "conv_w": g_conv_w[None], "conv_b": gs_convb,
        "w_rg": gs_wrg[None], "b_rg": gs_brg[None], "w_ig": gs_wig[None], "b_ig": gs_big[None], "lru_lambda": gs_lam,
        "lru_proj": g_lru_proj[None], "w_out": g_w_out[None], "norm2_g": gs_norm2, "w_ffn_in": g_w_ffn_in[None],
        "w_ffn_out": g_w_ffn_out[None], "ple_norm_g": gs_ple_norm, "w_ple_gate": g_w_ple_gate[None],
        "w_ple_proj": g_w_ple_proj[None], "final_g": gs_final.reshape(D),
    }
    weights = dict(norm1_g=norm1_g, w_in=w_in, b_gate=b_gate, pool_w=pool_w, pool_scale=pool_scale, pool_proj=pool_proj,
                   conv_w=conv_w, conv_b=conv_b, w_rg=w_rg, b_rg=b_rg, w_ig=w_ig, b_ig=b_ig, lru_lambda=lru_lambda,
                   lru_proj=lru_proj, w_out=w_out, norm2_g=norm2_g, w_ffn_in=w_ffn_in, w_ffn_out=w_ffn_out,
                   ple_norm_g=ple_norm_g, w_ple_gate=w_ple_gate, w_ple_proj=w_ple_proj, final_g=final_g)
    moments_m = dict(norm1_g=m_norm1_g, w_in=m_w_in, b_gate=m_b_gate, pool_w=m_pool_w, pool_scale=m_pool_scale,
                     pool_proj=m_pool_proj, conv_w=m_conv_w, conv_b=m_conv_b, w_rg=m_w_rg, b_rg=m_b_rg, w_ig=m_w_ig,
                     b_ig=m_b_ig, lru_lambda=m_lru_lambda, lru_proj=m_lru_proj, w_out=m_w_out, norm2_g=m_norm2_g,
                     w_ffn_in=m_w_ffn_in, w_ffn_out=m_w_ffn_out, ple_norm_g=m_ple_norm_g, w_ple_gate=m_w_ple_gate,
                     w_ple_proj=m_w_ple_proj, final_g=m_final_g)
    moments_v = dict(norm1_g=v_norm1_g, w_in=v_w_in, b_gate=v_b_gate, pool_w=v_pool_w, pool_scale=v_pool_scale,
                     pool_proj=v_pool_proj, conv_w=v_conv_w, conv_b=v_conv_b, w_rg=v_w_rg, b_rg=v_b_rg, w_ig=v_w_ig,
                     b_ig=v_b_ig, lru_lambda=v_lru_lambda, lru_proj=v_lru_proj, w_out=v_w_out, norm2_g=v_norm2_g,
                     w_ffn_in=v_w_ffn_in, w_ffn_out=v_w_ffn_out, ple_norm_g=v_ple_norm_g, w_ple_gate=v_w_ple_gate,
                     w_ple_proj=v_w_ple_proj, final_g=v_final_g)
    names = list(weights)
    big = ("w_in", "w_ffn_in", "w_ffn_out", "lru_proj", "w_out", "w_ple_gate", "pool_proj", "w_ple_proj")
    slab_space = {"w_in": red_win[:576], "w_ffn_in": red_wffn[:704]}
    delta, new_m, new_v = {}, {}, {}
    for n in big:
        sh = weights[n].shape
        if n in slab_space:
            as2d = lambda a: a[0].T
            back = lambda a: a.T[None]
            g2d = slab_space[n]
        else:
            as2d = lambda a: a.reshape(sh[-2], sh[-1])
            back = lambda a: a.reshape(sh)
            g2d = as2d(grads[n])
        d_, m_, v_ = _adamw(as2d(weights[n]), g2d, as2d(moments_m[n]), as2d(moments_v[n]), "adamw_" + n)
        delta[n], new_m[n], new_v[n] = back(d_), back(m_), back(v_)
    rest = [n for n in names if n not in big]
    rest_shapes = [weights[n].shape for n in rest]
    packed = [_small_pack([src[n] for n in rest]) for src in (weights, grads, moments_m, moments_v)]
    d_, m_, v_ = _adamw(*packed, "adamw_small")
    for n, a, b_, c_ in zip(rest, _small_unpack(d_, rest_shapes), _small_unpack(m_, rest_shapes),
                            _small_unpack(v_, rest_shapes)):
        delta[n], new_m[n], new_v[n] = a, b_, c_

    return (loss, grad_x[None], *[grads[n] for n in names], *[delta[n] for n in names],
            *[new_m[n] for n in names], *[new_v[n] for n in names])
```

```python
import functools
import math

import jax
import jax.numpy as jnp
from jax import lax
from jax.experimental import pallas as pl
from jax.experimental.pallas import tpu as pltpu

F32 = jnp.float32
D = 1024
NIN = 4608
PW = 512
FF = 2816
PLE = 256
HEADS, HD = 8, 128
GROUPS, GD = 4, 128
WINDOWS = (2, 4, 8, 16)
HALO = 16
CONV = 4
EPS = 1e-6
LRU_C = 8.0
NDEV = 8
MESH = pl.DeviceIdType.MESH

ADAM_LR, ADAM_B1, ADAM_B2, ADAM_EPS, ADAM_WD, ADAM_STEP = 0.001, 0.9, 0.999, 1e-08, 0.01, 10

_MXU = jnp.bfloat16
TM = 512
TM_SEQ = 256
VMEM_LIMIT = 56 * 1024 * 1024
W_FIRST = (("win", 576), ("f32s", 16))
W_SECOND = (("wffn", 704), ("pproj", 128), ("ple", 128), ("lru", 128), ("wout", 128), ("wffo", 352), ("wpg", 128))
W_OFF = {}
for _slabs in (W_FIRST, W_SECOND):
    _o = 0
    for _n, _r in _slabs:
        W_OFF[_n] = (_o, _r)
        _o += _r
SMALL_ROWS = 48


def _cparams(n_axes=1, vmem=VMEM_LIMIT):
    return pltpu.CompilerParams(dimension_semantics=("arbitrary",) * n_axes, vmem_limit_bytes=vmem)


def _my_pos():
    return lax.axis_index("x"), lax.axis_index("y"), lax.axis_index("c")


def _nt(a, b):
    return lax.dot_general(a, b, (((1,), (1,)), ((), ())), preferred_element_type=F32)


def _nn(a, b):
    return lax.dot_general(a, b, (((1,), (0,)), ((), ())), preferred_element_type=F32)


def _tn(a, b):
    return lax.dot_general(a, b, (((0,), (0,)), ((), ())), preferred_element_type=F32)


def _sigmoid(x):
    return 0.5 * jnp.tanh(0.5 * x) + 0.5


_GELU_K = math.sqrt(2.0 / math.pi)


def _gelu_and_grad(x):
    x2 = x * x
    inner = _GELU_K * (x + 0.044715 * x2 * x)
    t = jnp.tanh(inner)
    g = 0.5 * x * (1.0 + t)
    dg = 0.5 * (1.0 + t) + 0.5 * x * (1.0 - t * t) * _GELU_K * (1.0 + 3.0 * 0.044715 * x2)
    return g, dg


def _softplus_neg(lam):
    x = -lam
    t = jnp.exp(-jnp.abs(x))
    u = 1.0 + t
    l1p = jnp.where(u == 1.0, t, jnp.log(u) * t / (u - 1.0))
    return jnp.maximum(x, 0.0) + l1p, _sigmoid(x)


def _start_slab_loads(g_ref, name, dst_ref, sems, base, width=D):
    off, rows = W_OFF[name]
    copies = []
    for k in range(NDEV):
        if width == D:
            src = g_ref.at[k, pl.ds(off, rows), :]
        else:
            src = g_ref.at[k, pl.ds(off, rows), pl.ds(0, width)]
        cp = pltpu.make_async_copy(src, dst_ref.at[pl.ds(k * rows, rows), :], sems.at[base + k])
        cp.start()
        copies.append(cp)
    return copies


def _load_weights(g_ref, items, sems):
    copies = []
    for n, (name, dst, width) in enumerate(items):
        copies += _start_slab_loads(g_ref, name, dst, sems, n * NDEV, width)
    for cp in copies:
        cp.wait()


def _gather_phases(own_ref, out_ref, stage, send_sems, recv_sems, local_sem):
    x, y, c = _my_pos()
    me, sibling = (x, y, c), (x, y, 1 - c)
    chips = [(1 - x, y), (x, 1 - y), (1 - x, 1 - y)]

    def slab(px, py, pc):
        return out_ref.at[4 * px + 2 * py + pc]

    def copy(k, block, to, src=None):
        return pltpu.make_async_remote_copy(
            src_ref=slab(*block) if src is None else src, dst_ref=slab(*block),
            send_sem=send_sems.at[k], recv_sem=recv_sems.at[k], device_id=to, device_id_type=MESH)

    mine = pltpu.make_async_copy(stage, slab(*me), local_sem)
    first = [copy(0, me, sibling, src=stage)] + [copy(1 + j, me, (*chip, c), src=stage) for j, chip in enumerate(chips)]
    passed = [copy(4 + j, (*chip, c), sibling) for j, chip in enumerate(chips)]

    def send_mine():
        pltpu.sync_copy(own_ref, stage)
        mine.start()
        for cp in first:
            cp.start()

    def pass_on(js):
        for j in js:
            copy(1 + j, (*chips[j], c), me).wait_recv()
            passed[j].start()

    def finish():
        copy(0, sibling, me).wait_recv()
        for j, chip in enumerate(chips):
            copy(4 + j, (*chip, 1 - c), me).wait_recv()
        for cp in first + passed:
            cp.wait_send()
        mine.wait()

    return send_mine, pass_on, finish


def _all_gather_weights(own):
    rows, cols = own.shape

    def body(own_ref, out_ref, stage, send_sems, recv_sems, local_sem):
        send_mine, pass_on, finish = _gather_phases(own_ref, out_ref, stage, send_sems, recv_sems, local_sem)
        send_mine()
        pass_on((0, 1, 2))
        finish()

    return pl.pallas_call(
        body, name="ag_weights",
        out_shape=jax.ShapeDtypeStruct((NDEV, rows, cols), own.dtype),
        in_specs=[pl.BlockSpec(memory_space=pl.ANY)],
        out_specs=pl.BlockSpec(memory_space=pl.ANY),
        scratch_shapes=[pltpu.VMEM((rows, cols), own.dtype), pltpu.SemaphoreType.DMA((7,)),
                        pltpu.SemaphoreType.DMA((7,)), pltpu.SemaphoreType.DMA],
        compiler_params=pltpu.CompilerParams(vmem_limit_bytes=VMEM_LIMIT),
    )(own)


def _all_gather_small(piece):
    rows = piece.shape[0]

    def body(p_ref, out_ref, send_sems, recv_sems, local_sem):
        x, y, c = _my_pos()
        me = 4 * x + 2 * y + c
        mine = pltpu.make_async_copy(p_ref, out_ref.at[pl.ds(pl.multiple_of(me * rows, 8), rows), :], local_sem)
        mine.start()
        sends = []
        peers = []
        for r in range(1, NDEV):
            px = 1 - x if (r >> 2) & 1 else x
            py = 1 - y if (r >> 1) & 1 else y
            pc = 1 - c if r & 1 else c
            peers.append((px, py, pc))
            cp = pltpu.make_async_remote_copy(
                src_ref=p_ref, dst_ref=out_ref.at[pl.ds(pl.multiple_of(me * rows, 8), rows), :],
                send_sem=send_sems.at[r - 1], recv_sem=recv_sems.at[r - 1], device_id=(px, py, pc),
                device_id_type=MESH)
            cp.start()
            sends.append(cp)
        for r, (px, py, pc) in enumerate(peers):
            them = 4 * px + 2 * py + pc
            pltpu.make_async_remote_copy(
                src_ref=p_ref, dst_ref=out_ref.at[pl.ds(pl.multiple_of(them * rows, 8), rows), :],
                send_sem=send_sems.at[r], recv_sem=recv_sems.at[r], device_id=(px, py, pc),
                device_id_type=MESH).wait_recv()
        for cp in sends:
            cp.wait_send()
        mine.wait()

    return pl.pallas_call(
        body, name="ag_small",
        out_shape=jax.ShapeDtypeStruct((NDEV * rows, piece.shape[1]), piece.dtype),
        in_specs=[pl.BlockSpec(memory_space=pltpu.VMEM)],
        out_specs=pl.BlockSpec(memory_space=pl.ANY),
        scratch_shapes=[pltpu.SemaphoreType.DMA((7,)), pltpu.SemaphoreType.DMA((7,)), pltpu.SemaphoreType.DMA],
    )(piece)


def _row_block(rows, target=512, mult=8):
    b = min(rows, target) // mult * mult
    while rows % b:
        b -= mult
    return b


def _sum_arrays(arrs, name, narrow=None, target=464):
    rows, cols = arrs[0].shape
    br = _row_block(rows, target, 16)
    n = len(arrs)

    def body(*refs):
        acc = refs[0][...].astype(F32)
        for r in refs[1:n]:
            acc = acc + r[...].astype(F32)
        refs[n][...] = acc
        if narrow is not None:
            refs[n + 1][...] = acc.astype(narrow)

    spec = pl.BlockSpec((br, cols), lambda i: (i, 0))
    shape = jax.ShapeDtypeStruct((rows, cols), F32)
    if narrow is None:
        out_shape, out_specs = shape, spec
    else:
        out_shape, out_specs = (shape, jax.ShapeDtypeStruct((rows, cols), narrow)), (spec, spec)
    return pl.pallas_call(
        body, name=name, grid=(rows // br,), out_shape=out_shape,
        in_specs=[spec] * n, out_specs=out_specs, compiler_params=_cparams(1),
    )(*arrs)


def _adamw(w, g, m, v, name):
    rows, cols = w.shape
    br = _row_block(rows, 256)

    def body(w_ref, g_ref, m_ref, v_ref, d_ref, nm_ref, nv_ref):
        g_ = g_ref[...]
        m_ = ADAM_B1 * m_ref[...] + (1.0 - ADAM_B1) * g_
        v_ = ADAM_B2 * v_ref[...] + (1.0 - ADAM_B2) * (g_ * g_)
        m_hat = m_ / (1.0 - ADAM_B1 ** ADAM_STEP)
        v_hat = v_ / (1.0 - ADAM_B2 ** ADAM_STEP)
        d_ref[...] = -ADAM_LR * (m_hat / (jnp.sqrt(v_hat) + ADAM_EPS) + ADAM_WD * w_ref[...])
        nm_ref[...] = m_
        nv_ref[...] = v_

    spec = pl.BlockSpec((br, cols), lambda i: (i, 0))
    shape = jax.ShapeDtypeStruct((rows, cols), F32)
    return pl.pallas_call(
        body, name=name, grid=(rows // br,), out_shape=(shape, shape, shape),
        in_specs=[spec] * 4, out_specs=(spec, spec, spec), compiler_params=_cparams(1),
    )(w, g, m, v)


_CHIP_FLIPS = (2, 3, 1, 0)


def _grad_matmul(lhs, rhs, name):
    s, r = lhs.shape
    k = rhs.shape[1]
    tm = min(TM, s)

    def body(l_ref, r_ref, o_ref):
        @pl.when(pl.program_id(0) == 0)
        def _():
            o_ref[...] = jnp.zeros_like(o_ref)

        o_ref[:, pl.ds(0, k)] += _tn(l_ref[...].astype(_MXU), r_ref[...].astype(_MXU))

    return pl.pallas_call(
        body, name=name, grid=(s // tm,),
        out_shape=jax.ShapeDtypeStruct((r, D), F32),
        in_specs=[pl.BlockSpec((tm, r), lambda i: (i, 0)), pl.BlockSpec((tm, k), lambda i: (i, 0))],
        out_specs=pl.BlockSpec((r, D), lambda i: (0, 0)),
        compiler_params=_cparams(1),
    )(lhs, rhs)


def _grad_matmul_rs(lhs, rhs, name, rows, extras=(), narrow=None, tail=0):
    s, r8 = lhs.shape
    k = rhs.shape[1]
    tm = min(TM, s)
    nt = s // tm
    cpb = 1
    nblk = 4 // cpb
    nx = len(extras)
    ers = [e.shape[0] // NDEV for e in extras]
    er = sum(ers)
    srows = rows + er
    brows = 2 * cpb * srows
    groups = [(0, srows - tail, F32 if narrow is None else narrow)] + ([(srows - tail, tail, F32)] if tail else [])
    ng = len(groups)
    mid = min(nt - 1, max(1, nt // 4))

    def flip_of(p):
        return jnp.where(p == 0, 2, jnp.where(p == 1, 3, jnp.where(p == 2, 1, 0)))

    def block_col(b):
        x, y, _ = _my_pos()
        return (2 * x + y) ^ flip_of(b)

    def body(*refs):
        l_ref, r_ref = refs[:2]
        x_refs = refs[2:2 + nx]
        rest = refs[2 + nx:]
        town_ref = rest[0]
        lici_refs = rest[1:1 + ng]
        acc, stage = rest[1 + ng:3 + ng]
        send_bufs = rest[3 + ng:3 + 2 * ng]
        dsend, drecv, isend, irecv, xsem = rest[3 + 2 * ng:]
        b = pl.program_id(0)
        i = pl.program_id(1)
        x, y, c = _my_pos()
        mine = 2 * x + y
        sibling = (x, y, 1 - c)

        def chip_at(p):
            return mine ^ _CHIP_FLIPS[p]

        def slab_rows(p, parity):
            within = 0 if cpb == 1 else (chip_at(p) & 1) * 2
            return pl.ds(pl.multiple_of((within + parity) * srows, 8), srows)

        def push(p, slot):
            return pltpu.make_async_remote_copy(
                src_ref=acc.at[slot, slab_rows(p, 1 - c), :], dst_ref=stage.at[p % 2],
                send_sem=dsend.at[p], recv_sem=drecv.at[p], device_id=sibling, device_id_type=MESH)

        def ici(p):
            ch = chip_at(p)
            return [pltpu.make_async_remote_copy(
                src_ref=send_bufs[g].at[p % 2], dst_ref=lici_refs[g].at[p], send_sem=isend.at[3 * g + p],
                recv_sem=irecv.at[3 * g + p], device_id=(ch >> 1, ch & 1, c), device_id_type=MESH) for g in range(ng)]

        def extra_loads(p, slot):
            copies = []
            within = 0 if cpb == 1 else (chip_at(p) & 1) * 2
            for parity in range(2):
                off = rows
                for n, (x_ref, e) in enumerate(zip(x_refs, ers)):
                    src = x_ref.at[pl.ds(pl.multiple_of((2 * chip_at(p) + parity) * e, 8), e), :]
                    dst = acc.at[slot, pl.ds(pl.multiple_of((within + parity) * srows + off, 8), e), :]
                    copies.append(pltpu.make_async_copy(src, dst, xsem.at[(p * 2 + parity) * nx + n]))
                    off += e
            return copies

        def combine(p, slot):
            push(p, slot).wait_recv()
            total = acc[slot, slab_rows(p, c), :] + stage[p % 2]
            if p == 3:
                stage[p % 2] = total
                pltpu.sync_copy(stage.at[p % 2], town_ref)
            else:
                if p == 2:
                    for cp in ici(0):
                        cp.wait_send()
                for g, (r0, n, dt) in enumerate(groups):
                    send_bufs[g][p % 2] = total[r0:r0 + n, :].astype(dt)
                for cp in ici(p):
                    cp.start()

        for bb in range(nblk):
            slot = bb % 2
            positions = list(range(bb * cpb, (bb + 1) * cpb))

            @pl.when(jnp.logical_and(b == bb, i == 0))
            def _(bb=bb, slot=slot, positions=positions):
                if bb >= 2:
                    for p in range((bb - 2) * cpb, (bb - 1) * cpb):
                        push(p, slot).wait_send()
                for q in range(2 * cpb):
                    acc[slot, pl.ds(q * srows, rows), :] = jnp.zeros((rows, D), F32)
                for p in positions:
                    for cp in extra_loads(p, slot):
                        cp.start()

            if bb >= 1:
                @pl.when(jnp.logical_and(b == bb, i == mid))
                def _(bb=bb):
                    for p in range((bb - 1) * cpb, bb * cpb):
                        combine(p, (bb - 1) % 2)

        res = _tn(l_ref[...].astype(_MXU), r_ref[...].astype(_MXU))
        slot_now = b % 2
        for q in range(2 * cpb):
            acc[slot_now, pl.ds(q * srows, rows), pl.ds(0, k)] += res[q * rows:(q + 1) * rows, :]

        for bb in range(nblk):
            slot = bb % 2
            positions = list(range(bb * cpb, (bb + 1) * cpb))

            @pl.when(jnp.logical_and(b == bb, i == nt - 1))
            def _(bb=bb, slot=slot, positions=positions):
                for p in positions:
                    for cp in extra_loads(p, slot):
                        cp.wait()
                for p in positions:
                    push(p, slot).start()
                if bb == nblk - 1:
                    for p in positions:
                        combine(p, slot)
                    for p in range(max(0, (nblk - 2) * cpb), 4):
                        push(p, slot).wait_send()
                    for p in range(1, 3):
                        for cp in ici(p):
                            cp.wait_send()
                    for p in range(3):
                        for cp in ici(p):
                            cp.wait_recv()

    in_specs = [pl.BlockSpec((tm, 2 * cpb * rows), lambda b, i: (i, block_col(b))),
                pl.BlockSpec((tm, k), lambda b, i: (i, 0))]
    any_spec = pl.BlockSpec(memory_space=pl.ANY)
    in_specs += [any_spec] * nx
    args = [lhs, rhs, *extras]
    outs = pl.pallas_call(
        body, name=name, grid=(nblk, nt),
        out_shape=(jax.ShapeDtypeStruct((srows, D), F32),)
        + tuple(jax.ShapeDtypeStruct((3, n, D), dt) for _, n, dt in groups),
        in_specs=in_specs, out_specs=(any_spec,) * (1 + ng),
        scratch_shapes=[pltpu.VMEM((2, brows, D), F32), pltpu.VMEM((2, srows, D), F32)]
        + [pltpu.VMEM((2, n, D), dt) for _, n, dt in groups]
        + [pltpu.SemaphoreType.DMA((4,)), pltpu.SemaphoreType.DMA((4,)), pltpu.SemaphoreType.DMA((3 * ng,)),
           pltpu.SemaphoreType.DMA((3 * ng,)), pltpu.SemaphoreType.DMA((max(1, 8 * nx),))],
        compiler_params=_cparams(2),
    )(*args)
    t_own = outs[0]
    return [(t_own[r0:r0 + n], landed) for (r0, n, _), landed in zip(groups, outs[1:])]


def _inproj_fwd(x, g1, gw, own_second):
    s = x.shape[0]
    tm = min(TM, s)
    nt = s // tm
    nchunk = 4
    cw = NIN // nchunk
    rows2, cols2 = own_second.shape

    def body(x_ref, g1_ref, gw_ref, own_ref, u_ref, z_ref, gw2_ref, w_vmem, stage, sems, send_sems, recv_sems,
             local_sem):
        i = pl.program_id(0)
        send_mine, pass_on, finish = _gather_phases(own_ref, gw2_ref, stage, send_sems, recv_sems, local_sem)

        @pl.when(i == 0)
        def _():
            send_mine()
            _load_weights(gw_ref, [("win", w_vmem, D)], sems)

        @pl.when(i == nt // 2)
        def _():
            pass_on((0, 1))

        @pl.when(i == (7 * nt) // 8)
        def _():
            pass_on((2,))

        xv = x_ref[...]
        inv = lax.rsqrt(jnp.mean(xv * xv, axis=-1, keepdims=True) + EPS)
        u = (xv * inv * g1_ref[...]).astype(_MXU)
        u_ref[...] = u
        for ch in range(nchunk):
            z_ref[:, pl.ds(ch * cw, cw)] = _nt(u, w_vmem[pl.ds(ch * cw, cw), :])

        @pl.when(i == nt - 1)
        def _():
            finish()

    any_spec = pl.BlockSpec(memory_space=pl.ANY)
    return pl.pallas_call(
        body, name="inproj_fwd", grid=(nt,),
        out_shape=(jax.ShapeDtypeStruct((s, D), _MXU), jax.ShapeDtypeStruct((s, NIN), F32),
                   jax.ShapeDtypeStruct((NDEV, rows2, cols2), own_second.dtype)),
        in_specs=[pl.BlockSpec((tm, D), lambda i: (i, 0)), pl.BlockSpec((1, D), lambda i: (0, 0)), any_spec, any_spec],
        out_specs=(pl.BlockSpec((tm, D), lambda i: (i, 0)), pl.BlockSpec((tm, NIN), lambda i: (i, 0)), any_spec),
        scratch_shapes=[pltpu.VMEM((NIN, D), _MXU), pltpu.VMEM((rows2, cols2), own_second.dtype),
                        pltpu.SemaphoreType.DMA((NDEV,)), pltpu.SemaphoreType.DMA((7,)), pltpu.SemaphoreType.DMA((7,)),
                        pltpu.SemaphoreType.DMA],
        compiler_params=_cparams(1),
    )(x, g1, gw, own_second)


def _pool_tile(pbuf, t0, tm, pw_ref, scale_ref):
    t = t0 + lax.broadcasted_iota(jnp.int32, (tm, GD), 0)
    pooled, mixed_pre = [], []
    for g, w in enumerate(WINDOWS):
        cs = pl.ds(g * GD, GD)
        cur = pbuf[pl.ds(HALO, tm), cs]
        acc = cur
        for d in range(1, w):
            acc = acc + pbuf[pl.ds(HALO - d, tm), cs]
        cnt = jnp.minimum(t + 1, w).astype(F32)
        pg = acc / cnt - cur
        pooled.append(pg)
        mixed_pre.append(_nn(pg.astype(_MXU), pw_ref[g]))
    return pooled, mixed_pre


def _lru_gates_head(hh, lbuf, start, tm, cw_ref, cb_ref, wrg_ref, brg_ref, wig_ref, big_ref, sp):
    cs = pl.ds(hh * HD, HD)
    xc = cb_ref[:, cs] + cw_ref[pl.ds(CONV - 1, 1), cs] * lbuf[pl.ds(HALO, tm), cs]
    for k in range(CONV - 1):
        xc = xc + cw_ref[pl.ds(k, 1), cs] * lbuf[pl.ds(HALO - (CONV - 1) + k, tm), cs]
    xcm = xc.astype(_MXU)
    r = _sigmoid(_nn(xcm, wrg_ref[hh]) + brg_ref[pl.ds(hh, 1), :])
    ig = _sigmoid(_nn(xcm, wig_ref[hh]) + big_ref[pl.ds(hh, 1), :])
    a = jnp.exp(-LRU_C * r * sp[:, hh * HD:(hh + 1) * HD])
    one_m = 1.0 - a * a
    live = jnp.logical_and(one_m > 0.0, jnp.logical_not(start))
    inv_mult = lax.rsqrt(jnp.where(live, one_m, 1.0))
    mult = jnp.where(live, one_m * inv_mult, jnp.where(start, 1.0, 0.0))
    return xc, r, ig, a, live, inv_mult, mult


def _seg_layout(tm):
    seg = tm // 8
    return seg, seg + 8


def _to_segments(dst_ref, hh, val, tm):
    seg, pitch = _seg_layout(tm)
    for s in range(8):
        dst_ref[hh, pl.ds(s * pitch, seg), :] = val[s * seg:(s + 1) * seg, :]


def _from_segments(src_ref, hh, tm):
    seg, pitch = _seg_layout(tm)
    return jnp.concatenate([src_ref[hh, pl.ds(s * pitch, seg), :] for s in range(8)], axis=0)


def _segment_scan(a_ref, b_ref, out_ref, hk, pk, carry_ref, tm, reverse):
    seg, pitch = _seg_layout(tm)
    row = lax.broadcasted_iota(jnp.int32, (8, HD), 0)
    order = range(seg - 1, -1, -1) if reverse else range(seg)
    for hh in range(HEADS):
        cs = pl.ds(hh * HD, HD)
        if reverse:
            a0 = a_ref[hh, pl.ds(0, 8, stride=pitch), :]
            a_wrap = jnp.where(row <= 6, pltpu.roll(a0, 7, 0), 1.0)
        hv = jnp.zeros((8, HD), F32)
        pv = jnp.ones((8, HD), F32)
        for k in order:
            if not reverse:
                av = a_ref[hh, pl.ds(k, 8, stride=pitch), :]
            elif k + 1 < seg:
                av = a_ref[hh, pl.ds(k + 1, 8, stride=pitch), :]
            else:
                av = a_wrap
            hv = av * hv + b_ref[hh, pl.ds(k, 8, stride=pitch), :]
            pv = av * pv
            hk[hh, pl.ds(8 * k, 8), :] = hv
            pk[hh, pl.ds(8 * k, 8), :] = pv
        for d in (1, 2, 4):
            if reverse:
                keep, sh = row < 8 - d, 8 - d
            else:
                keep, sh = row >= d, d
            hv = hv + pv * jnp.where(keep, pltpu.roll(hv, sh, 0), 0.0)
            pv = pv * jnp.where(keep, pltpu.roll(pv, sh, 0), 1.0)
        cin = carry_ref[:, cs]
        ends = hv + pv * cin
        if reverse:
            enter = jnp.where(row <= 6, pltpu.roll(ends, 7, 0), cin)
            carry_ref[:, cs] = jnp.broadcast_to((a0 * ends)[0:1, :], (8, HD))
        else:
            enter = jnp.where(row >= 1, pltpu.roll(ends, 1, 0), cin)
            carry_ref[:, cs] = jnp.broadcast_to(ends[7:8, :], (8, HD))
        for k in range(seg):
            out_ref[hh, pl.ds(k, 8, stride=pitch), :] = hk[hh, pl.ds(8 * k, 8), :] + pk[hh, pl.ds(8 * k, 8), :] * enter


def _mixer_fwd(z, x, gw, small):
    s = x.shape[0]
    tm = min(TM_SEQ, s)
    (pool_w, pool_scale, conv_w, conv_b, w_rg, b_rg, w_ig, b_ig, lam, b_gate) = small

    def body(z_ref, x_ref, gw_ref, pw_ref, ps_ref, cw_ref, cb_ref, wrg_ref, brg_ref, wig_ref, big_ref, lam_ref,
             bg_ref, h_ref, yl_ref, mg_ref, yp_ref, yr_ref, h1_ref, a_ref, r_ref, ig_ref, xc_ref,
             pprojT, lru_w, wout_w, pbuf, lbuf, a_s, b_s, h_s, hk, pk, hcar, sems):
        i = pl.program_id(0)
        t0 = i * tm

        @pl.when(i == 0)
        def _():
            _load_weights(gw_ref, [("pproj", pprojT, PW), ("lru", lru_w, D), ("wout", wout_w, D)], sems)
            pbuf[pl.ds(0, HALO), :] = jnp.zeros((HALO, PW), F32)
            lbuf[pl.ds(0, HALO), :] = jnp.zeros((HALO, D), F32)
            hcar[...] = jnp.zeros_like(hcar)

        pbuf[pl.ds(HALO, tm), :] = z_ref[:, pl.ds(0, PW)]
        _, mixed_pre = _pool_tile(pbuf, t0, tm, pw_ref, ps_ref)
        mixed = jnp.concatenate(mixed_pre, axis=1) * ps_ref[...]
        y_pool = _nt(mixed.astype(_MXU), pprojT[...])
        pbuf[pl.ds(0, HALO), :] = pbuf[pl.ds(tm, HALO), :]

        lbuf[pl.ds(HALO, tm), :] = z_ref[:, pl.ds(PW, D)]
        sp, _ = _softplus_neg(lam_ref[...])
        start = (t0 + lax.broadcasted_iota(jnp.int32, (tm, HD), 0)) == 0
        for hh in range(HEADS):
            xc, r, ig, a, _, _, mult = _lru_gates_head(hh, lbuf, start, tm, cw_ref, cb_ref, wrg_ref, brg_ref,
                                                       wig_ref, big_ref, sp)
            _to_segments(a_s, hh, a, tm)
            _to_segments(b_s, hh, mult * ig * xc, tm)
            cs = pl.ds(hh * HD, HD)
            a_ref[:, cs] = a
            r_ref[:, cs] = r.astype(_MXU)
            ig_ref[:, cs] = ig.astype(_MXU)
            xc_ref[:, cs] = xc.astype(_MXU)
        lbuf[pl.ds(0, HALO), :] = lbuf[pl.ds(tm, HALO), :]
        _segment_scan(a_s, b_s, h_s, hk, pk, hcar, tm, reverse=False)
        for hh in range(HEADS):
            h_ref[:, pl.ds(hh * HD, HD)] = _from_segments(h_s, hh, tm)
        gel, _ = _gelu_and_grad(z_ref[:, pl.ds(PW + D, D)])
        yl = (h_ref[...] * gel).astype(_MXU)
        yl_ref[...] = yl
        y_lru = _nn(yl, lru_w[...])

        g0 = _sigmoid(z_ref[:, pl.ds(PW + 2 * D, D)] + bg_ref[pl.ds(0, 1), :])
        g1 = _sigmoid(z_ref[:, pl.ds(PW + 3 * D, D)] + bg_ref[pl.ds(1, 1), :])
        merged = (g0 * y_pool + g1 * y_lru).astype(_MXU)
        mg_ref[...] = merged
        yp_ref[...] = y_pool.astype(_MXU)
        yr_ref[...] = y_lru.astype(_MXU)
        h1_ref[...] = x_ref[...] + _nn(merged, wout_w[...])

    tok = lambda w, dt: jax.ShapeDtypeStruct((s, w), dt)
    tspec = lambda w: pl.BlockSpec((tm, w), lambda i: (i, 0))
    full = lambda a: pl.BlockSpec(a.shape, lambda i: (0,) * a.ndim)
    seg_buf = pltpu.VMEM((HEADS, 8 * _seg_layout(tm)[1], HD), F32)
    return pl.pallas_call(
        body, name="mixer_fwd", grid=(s // tm,),
        out_shape=(tok(D, F32), tok(D, _MXU), tok(D, _MXU), tok(D, _MXU), tok(D, _MXU), tok(D, F32),
                   tok(D, F32), tok(D, _MXU), tok(D, _MXU), tok(D, _MXU)),
        in_specs=[tspec(NIN), tspec(D), pl.BlockSpec(memory_space=pl.ANY)] + [full(a) for a in small],
        out_specs=(tspec(D),) * 10,
        scratch_shapes=[pltpu.VMEM((D, PW), _MXU), pltpu.VMEM((D, D), _MXU), pltpu.VMEM((D, D), _MXU),
                        pltpu.VMEM((tm + HALO, PW), F32), pltpu.VMEM((tm + HALO, D), F32),
                        seg_buf, seg_buf, seg_buf, pltpu.VMEM((HEADS, tm, HD), F32), pltpu.VMEM((HEADS, tm, HD), F32),
                        pltpu.VMEM((8, D), F32), pltpu.SemaphoreType.DMA((3 * NDEV,))],
        compiler_params=_cparams(1),
    )(z, x, gw, *small)


def _ffn_fwd(h1, g2, gw):
    s = h1.shape[0]
    tm = min(TM, s)
    half = FF // 2

    def body(h1_ref, g2_ref, gw_ref, v_ref, gf_ref, uf_ref, h2_ref, wffnT, wffo, sems):
        @pl.when(pl.program_id(0) == 0)
        def _():
            _load_weights(gw_ref, [("wffn", wffnT, D), ("wffo", wffo, D)], sems)

        hv = h1_ref[...]
        inv = lax.rsqrt(jnp.mean(hv * hv, axis=-1, keepdims=True) + EPS)
        v = (hv * inv * g2_ref[...]).astype(_MXU)
        v_ref[...] = v
        acc = hv
        for ch in range(2):
            cs = pl.ds(ch * half, half)
            gf = _nt(v, wffnT[pl.ds(ch * half, half), :]).astype(_MXU)
            uf = _nt(v, wffnT[pl.ds(FF + ch * half, half), :]).astype(_MXU)
            gf_ref[:, cs] = gf
            uf_ref[:, cs] = uf
            gf32 = gf.astype(F32)
            act = (gf32 * _sigmoid(gf32) * uf.astype(F32)).astype(_MXU)
            acc = acc + _nn(act, wffo[pl.ds(ch * half, half), :])
        h2_ref[...] = acc

    tspec = lambda w: pl.BlockSpec((tm, w), lambda i: (i, 0))
    return pl.pallas_call(
        body, name="ffn_fwd", grid=(s // tm,),
        out_shape=(jax.ShapeDtypeStruct((s, D), _MXU), jax.ShapeDtypeStruct((s, FF), _MXU),
                   jax.ShapeDtypeStruct((s, FF), _MXU), jax.ShapeDtypeStruct((s, D), F32)),
        in_specs=[tspec(D), pl.BlockSpec((1, D), lambda i: (0, 0)), pl.BlockSpec(memory_space=pl.ANY)],
        out_specs=(tspec(D), tspec(FF), tspec(FF), tspec(D)),
        scratch_shapes=[pltpu.VMEM((2 * FF, D), _MXU), pltpu.VMEM((FF, D), _MXU), pltpu.SemaphoreType.DMA((2 * NDEV,))],
        compiler_params=_cparams(1),
    )(h1, g2, gw)


def _rms_bwd(dy, xn, inv, g):
    dg = jnp.sum(dy * xn, axis=0, keepdims=True)
    dxn = dy * g
    dx = inv * (dxn - xn * jnp.mean(dxn * xn, axis=-1, keepdims=True))
    return dx, dg


def _ple_loss_fwd_bwd(h2, p, target, g3, gfin, gw):
    s = h2.shape[0]
    tm = min(TM, s)

    def body(h2_ref, p_ref, t_ref, g3_ref, gf_ref, gw_ref,
             dh2_ref, n3_ref, dpg_ref, de_ref, loss_ref, dg3_ref, dgf_ref, wpg, pleT, sems):
        i = pl.program_id(0)

        @pl.when(i == 0)
        def _():
            _load_weights(gw_ref, [("wpg", wpg, D), ("ple", pleT, PLE)], sems)
            loss_ref[...] = jnp.zeros_like(loss_ref)
            dg3_ref[...] = jnp.zeros_like(dg3_ref)
            dgf_ref[...] = jnp.zeros_like(dgf_ref)

        hv = h2_ref[...]
        inv3 = lax.rsqrt(jnp.mean(hv * hv, axis=-1, keepdims=True) + EPS)
        xn3 = hv * inv3
        n3 = (xn3 * g3_ref[...]).astype(_MXU)
        n3_ref[...] = n3
        pg = _sigmoid(_nn(n3, wpg[...]))
        e = _nt(p_ref[...].astype(_MXU), pleT[...])
        h3 = hv + pg * e
        invf = lax.rsqrt(jnp.mean(h3 * h3, axis=-1, keepdims=True) + EPS)
        xf = h3 * invf
        diff = xf * gf_ref[...] - t_ref[...]
        loss_ref[...] += jnp.sum(diff * diff) * (0.5 / D)
        dh3, dgf = _rms_bwd(diff * (1.0 / D), xf, invf, gf_ref[...])
        dgf_ref[...] += dgf
        de_ref[...] = (dh3 * pg).astype(_MXU)
        dpg = (dh3 * e * pg * (1.0 - pg)).astype(_MXU)
        dpg_ref[...] = dpg
        dn3 = _nt(dpg, wpg[...])
        dx3, dg3 = _rms_bwd(dn3, xn3, inv3, g3_ref[...])
        dg3_ref[...] += dg3
        dh2_ref[...] = dh3 + dx3

    tspec = lambda w: pl.BlockSpec((tm, w), lambda i: (i, 0))
    vec = pl.BlockSpec((1, D), lambda i: (0, 0))
    tok = lambda w, dt: jax.ShapeDtypeStruct((s, w), dt)
    return pl.pallas_call(
        body, name="ple_loss", grid=(s // tm,),
        out_shape=(tok(D, F32), tok(D, _MXU), tok(D, _MXU), tok(D, _MXU), jax.ShapeDtypeStruct((8, 128), F32),
                   jax.ShapeDtypeStruct((1, D), F32), jax.ShapeDtypeStruct((1, D), F32)),
        in_specs=[tspec(D), tspec(PLE), tspec(D), vec, vec, pl.BlockSpec(memory_space=pl.ANY)],
        out_specs=(tspec(D), tspec(D), tspec(D), tspec(D), pl.BlockSpec((8, 128), lambda i: (0, 0)), vec, vec),
        scratch_shapes=[pltpu.VMEM((D, D), _MXU), pltpu.VMEM((D, PLE), _MXU), pltpu.SemaphoreType.DMA((2 * NDEV,))],
        compiler_params=_cparams(1),
    )(h2, p, target, g3, gfin, gw)


def _ffn_bwd_hidden(dh2, gf, uf, gw):
    s = dh2.shape[0]
    tm = min(TM, s)
    half = FF // 2

    def body(dh2_ref, gf_ref, uf_ref, gw_ref, dff_ref, act_ref, wffo, sems):
        @pl.when(pl.program_id(0) == 0)
        def _():
            _load_weights(gw_ref, [("wffo", wffo, D)], sems)

        dm = dh2_ref[...].astype(_MXU)
        for ch in range(2):
            cs = pl.ds(ch * half, half)
            dact = _nt(dm, wffo[pl.ds(ch * half, half), :])
            gfv = gf_ref[:, cs].astype(F32)
            ufv = uf_ref[:, cs].astype(F32)
            sg = _sigmoid(gfv)
            silu = gfv * sg
            act_ref[:, cs] = (silu * ufv).astype(_MXU)
            dff_ref[:, pl.ds(ch * half, half)] = (dact * ufv * (sg * (1.0 + gfv * (1.0 - sg)))).astype(_MXU)
            dff_ref[:, pl.ds(FF + ch * half, half)] = (dact * silu).astype(_MXU)

    tspec = lambda w: pl.BlockSpec((tm, w), lambda i: (i, 0))
    return pl.pallas_call(
        body, name="ffn_bwd_hidden", grid=(s // tm,),
        out_shape=(jax.ShapeDtypeStruct((s, 2 * FF), _MXU), jax.ShapeDtypeStruct((s, FF), _MXU)),
        in_specs=[tspec(D), tspec(FF), tspec(FF), pl.BlockSpec(memory_space=pl.ANY)],
        out_specs=(tspec(2 * FF), tspec(FF)),
        scratch_shapes=[pltpu.VMEM((FF, D), _MXU), pltpu.SemaphoreType.DMA((NDEV,))],
        compiler_params=_cparams(1),
    )(dh2, gf, uf, gw)


def _proj_norm_bwd(dy, x, dres, g, gw, slab, width, name):
    s = x.shape[0]
    tm = min(TM, s)

    def body(dy_ref, x_ref, dr_ref, g_ref, gw_ref, dx_ref, dg_ref, wT, sems):
        @pl.when(pl.program_id(0) == 0)
        def _():
            _load_weights(gw_ref, [(slab, wT, D)], sems)
            dg_ref[...] = jnp.zeros_like(dg_ref)

        dv = _nn(dy_ref[...], wT[...])
        xv = x_ref[...]
        inv = lax.rsqrt(jnp.mean(xv * xv, axis=-1, keepdims=True) + EPS)
        dx, dg = _rms_bwd(dv, xv * inv, inv, g_ref[...])
        dg_ref[...] += dg
        dx_ref[...] = dr_ref[...] + dx

    tspec = lambda w: pl.BlockSpec((tm, w), lambda i: (i, 0))
    vec = pl.BlockSpec((1, D), lambda i: (0, 0))
    return pl.pallas_call(
        body, name=name, grid=(s // tm,),
        out_shape=(jax.ShapeDtypeStruct((s, D), F32), jax.ShapeDtypeStruct((1, D), F32)),
        in_specs=[tspec(width), tspec(D), tspec(D), vec, pl.BlockSpec(memory_space=pl.ANY)],
        out_specs=(tspec(D), vec),
        scratch_shapes=[pltpu.VMEM((width, D), _MXU), pltpu.SemaphoreType.DMA((NDEV,))],
        compiler_params=_cparams(1),
    )(dy, x, dres, g, gw)


def _mixer_bwd(dh1, z, h, y_pool, y_lru, saved, gw, small):
    s = dh1.shape[0]
    tm = min(TM_SEQ, s)
    nt = s // tm
    (pool_w, pool_scale, conv_w, conv_b, w_rg, b_rg, w_ig, b_ig, lam, b_gate) = small

    def body(dh1_ref, z_ref, zp_ref, h_ref, hp_ref, yp_ref, yr_ref, a_ref, r_ref, ig_ref, xc_ref, gw_ref,
             pw_ref, ps_ref, cw_ref, cb_ref, wrg_ref, brg_ref, wig_ref, big_ref, lam_ref, bg_ref,
             dz_ref, dyr_ref, dyp_ref, mx_ref,
             gbg_ref, glam_ref, gbrg_ref, gbig_ref, gcb_ref, gcw_ref, gps_ref, gpw_ref, gwrg_ref, gwig_ref,
             pprojT, lru_w, wout_w, pbuf, lbuf, hbuf, qbuf, xbuf, a_s, g_s, dh_s, hk, pk, dcar, sems):
        step = pl.program_id(0)
        i = nt - 1 - step
        t0 = i * tm

        @pl.when(step == 0)
        def _():
            _load_weights(gw_ref, [("pproj", pprojT, PW), ("lru", lru_w, D), ("wout", wout_w, D)], sems)
            for ref in (gbg_ref, glam_ref, gbrg_ref, gbig_ref, gcb_ref, gcw_ref, gps_ref, gpw_ref, gwrg_ref, gwig_ref):
                ref[...] = jnp.zeros_like(ref)
            qbuf[pl.ds(tm, HALO), :] = jnp.zeros((HALO, PW), F32)
            xbuf[pl.ds(tm, 8), :] = jnp.zeros((8, D), F32)
            dcar[...] = jnp.zeros_like(dcar)

        first = i == 0
        zprev = jnp.where(first, 0.0, zp_ref[...])
        hprev = jnp.where(first, 0.0, hp_ref[...])

        d_merged = _nt(dh1_ref[...].astype(_MXU), wout_w[...])

        g0 = _sigmoid(z_ref[:, pl.ds(PW + 2 * D, D)] + bg_ref[pl.ds(0, 1), :])
        g1 = _sigmoid(z_ref[:, pl.ds(PW + 3 * D, D)] + bg_ref[pl.ds(1, 1), :])
        dz0 = d_merged * yp_ref[...].astype(F32) * g0 * (1.0 - g0)
        dz1 = d_merged * yr_ref[...].astype(F32) * g1 * (1.0 - g1)
        dz_ref[:, pl.ds(PW + 2 * D, D)] = dz0.astype(_MXU)
        dz_ref[:, pl.ds(PW + 3 * D, D)] = dz1.astype(_MXU)
        gbg_ref[pl.ds(0, 1), :] += jnp.sum(dz0, axis=0, keepdims=True)
        gbg_ref[pl.ds(1, 1), :] += jnp.sum(dz1, axis=0, keepdims=True)
        d_ypool = (d_merged * g0).astype(_MXU)
        d_ylru = (d_merged * g1).astype(_MXU)
        dyp_ref[...] = d_ypool
        dyr_ref[...] = d_ylru

        d_yl = _nt(d_ylru, lru_w[...])
        gel, dgel = _gelu_and_grad(z_ref[:, pl.ds(PW + D, D)])
        dz_ref[:, pl.ds(PW + D, D)] = (d_yl * h_ref[...] * dgel).astype(_MXU)
        g_full = d_yl * gel
        lbuf[pl.ds(0, HALO), :] = zprev[:, PW:PW + D]
        lbuf[pl.ds(HALO, tm), :] = z_ref[:, pl.ds(PW, D)]
        hbuf[pl.ds(0, 8), :] = hprev
        hbuf[pl.ds(8, tm), :] = h_ref[...]
        sp, sneg = _softplus_neg(lam_ref[...])
        start = (t0 + lax.broadcasted_iota(jnp.int32, (tm, HD), 0)) == 0
        for hh in range(HEADS):
            cs = pl.ds(hh * HD, HD)
            _to_segments(a_s, hh, a_ref[:, cs], tm)
            _to_segments(g_s, hh, g_full[:, hh * HD:(hh + 1) * HD], tm)
        _segment_scan(a_s, g_s, dh_s, hk, pk, dcar, tm, reverse=True)
        for hh in range(HEADS):
            cs = pl.ds(hh * HD, HD)
            a = a_ref[:, cs]
            r = r_ref[:, cs].astype(F32)
            ig = ig_ref[:, cs].astype(F32)
            xc = xc_ref[:, cs].astype(F32)
            a2 = a * a
            one_m = 1.0 - a2
            live = jnp.logical_and(one_m > 0.0, jnp.logical_not(start))
            inv_mult = lax.rsqrt(jnp.where(live, one_m, 1.0))
            mult = jnp.where(live, one_m * inv_mult, jnp.where(start, 1.0, 0.0))
            dh = _from_segments(dh_s, hh, tm)
            d_mult = dh * ig * xc
            d_loga = dh * hbuf[pl.ds(7, tm), cs] * a - jnp.where(live, d_mult * a2 * inv_mult, 0.0)
            glam_ref[:, cs] += jnp.sum(d_loga * (LRU_C * r) * sneg[:, hh * HD:(hh + 1) * HD], axis=0, keepdims=True)
            d_rpre = d_loga * (-LRU_C * sp[:, hh * HD:(hh + 1) * HD]) * r * (1.0 - r)
            d_igpre = dh * mult * xc * ig * (1.0 - ig)
            gbrg_ref[pl.ds(hh, 1), :] += jnp.sum(d_rpre, axis=0, keepdims=True)
            gbig_ref[pl.ds(hh, 1), :] += jnp.sum(d_igpre, axis=0, keepdims=True)
            drm = d_rpre.astype(_MXU)
            dim = d_igpre.astype(_MXU)
            xcm = xc.astype(_MXU)
            gwrg_ref[hh] += _tn(xcm, drm)
            gwig_ref[hh] += _tn(xcm, dim)
            d_xc = dh * mult * ig + _nt(drm, wrg_ref[hh]) + _nt(dim, wig_ref[hh])
            gcb_ref[:, cs] += jnp.sum(d_xc, axis=0, keepdims=True)
            for k in range(CONV):
                gcw_ref[pl.ds(k, 1), cs] += jnp.sum(d_xc * lbuf[pl.ds(HALO - (CONV - 1) + k, tm), cs], axis=0,
                                                    keepdims=True)
            xbuf[pl.ds(0, tm), cs] = d_xc
        dzl = cw_ref[pl.ds(CONV - 1, 1), :] * xbuf[pl.ds(0, tm), :]
        for k in range(CONV - 1):
            dzl = dzl + cw_ref[pl.ds(k, 1), :] * xbuf[pl.ds(CONV - 1 - k, tm), :]
        dz_ref[:, pl.ds(PW, D)] = dzl.astype(_MXU)
        xbuf[pl.ds(tm, 8), :] = xbuf[pl.ds(0, 8), :]

        d_mixed = _nn(d_ypool, pprojT[...])
        pbuf[pl.ds(0, HALO), :] = zprev[:, 0:PW]
        pbuf[pl.ds(HALO, tm), :] = z_ref[:, pl.ds(0, PW)]
        pooled, mixed_pre = _pool_tile(pbuf, t0, tm, pw_ref, ps_ref)
        mp = jnp.concatenate(mixed_pre, axis=1)
        mx_ref[...] = (mp * ps_ref[...]).astype(_MXU)
        gps_ref[...] += jnp.sum(d_mixed * mp, axis=0, keepdims=True)
        d_mp = (d_mixed * ps_ref[...]).astype(_MXU)
        t = t0 + lax.broadcasted_iota(jnp.int32, (tm, GD), 0)
        d_pooled = []
        for g, w in enumerate(WINDOWS):
            dmg = d_mp[:, g * GD:(g + 1) * GD]
            gpw_ref[g] += _tn(pooled[g].astype(_MXU), dmg)
            dp = _nt(dmg, pw_ref[g])
            d_pooled.append(dp)
            qbuf[pl.ds(0, tm), pl.ds(g * GD, GD)] = dp / jnp.minimum(t + 1, w).astype(F32)
        for g, w in enumerate(WINDOWS):
            cs = pl.ds(g * GD, GD)
            acc = qbuf[pl.ds(0, tm), cs]
            for d in range(1, w):
                acc = acc + qbuf[pl.ds(d, tm), cs]
            dz_ref[:, cs] = (acc - d_pooled[g]).astype(_MXU)
        qbuf[pl.ds(tm, HALO), :] = qbuf[pl.ds(0, HALO), :]

    rev = lambda w: pl.BlockSpec((tm, w), lambda g: (nt - 1 - g, 0))
    prev = lambda rows, w: pl.BlockSpec((rows, w), lambda g: (jnp.maximum((nt - 1 - g) * (tm // rows) - 1, 0), 0))
    full = lambda a: pl.BlockSpec(a.shape, lambda g: (0,) * a.ndim)
    tok = lambda w, dt: jax.ShapeDtypeStruct((s, w), dt)
    acc_shapes = [(2, D), (1, D), (HEADS, HD), (HEADS, HD), (1, D), (CONV, D), (1, PW), (GROUPS, GD, GD),
                  (HEADS, HD, HD), (HEADS, HD, HD)]
    acc_specs = tuple(pl.BlockSpec(sh, lambda g, n=len(sh): (0,) * n) for sh in acc_shapes)
    seg_buf = pltpu.VMEM((HEADS, 8 * _seg_layout(tm)[1], HD), F32)
    a_in, r_in, ig_in, xc_in = saved
    return pl.pallas_call(
        body, name="mixer_bwd", grid=(nt,),
        out_shape=(tok(NIN, _MXU), tok(D, _MXU), tok(D, _MXU), tok(PW, _MXU))
        + tuple(jax.ShapeDtypeStruct(sh, F32) for sh in acc_shapes),
        in_specs=[rev(D), rev(NIN), prev(HALO, NIN), rev(D), prev(8, D), rev(D), rev(D), rev(D), rev(D), rev(D), rev(D),
                  pl.BlockSpec(memory_space=pl.ANY)] + [full(a) for a in small],
        out_specs=(rev(NIN), rev(D), rev(D), rev(PW)) + acc_specs,
        scratch_shapes=[pltpu.VMEM((D, PW), _MXU), pltpu.VMEM((D, D), _MXU), pltpu.VMEM((D, D), _MXU),
                        pltpu.VMEM((tm + HALO, PW), F32), pltpu.VMEM((tm + HALO, D), F32),
                        pltpu.VMEM((tm + 8, D), F32), pltpu.VMEM((tm + HALO, PW), F32), pltpu.VMEM((tm + 8, D), F32),
                        seg_buf, seg_buf, seg_buf, pltpu.VMEM((HEADS, tm, HD), F32), pltpu.VMEM((HEADS, tm, HD), F32),
                        pltpu.VMEM((8, D), F32), pltpu.SemaphoreType.DMA((3 * NDEV,))],
        compiler_params=_cparams(1),
    )(dh1, z, z, h, h, y_pool, y_lru, a_in, r_in, ig_in, xc_in, gw, *small)


def _split3(a):
    hi = a.astype(jnp.bfloat16).astype(F32)
    mid = (a - hi).astype(jnp.bfloat16).astype(F32)
    lo = (a - hi - mid).astype(jnp.bfloat16).astype(F32)
    return jnp.stack([hi, mid, lo])


def _small_pack(parts):
    flat = jnp.concatenate([a.reshape(-1) for a in parts])
    return jnp.pad(flat, (0, NDEV * SMALL_ROWS * D - flat.shape[0])).reshape(NDEV * SMALL_ROWS, D)


def _small_unpack(packed, shapes):
    flat = packed.reshape(-1)
    out, o = [], 0
    for sh in shapes:
        n = math.prod(sh)
        out.append(flat[o:o + n].reshape(sh))
        o += n
    return out


def kernel(x, p, norm1_g, w_in, b_gate, pool_w, pool_scale, pool_proj, conv_w, conv_b, w_rg, b_rg, w_ig, b_ig, lru_lambda, lru_proj, w_out, norm2_g, w_ffn_in, w_ffn_out, ple_norm_g, w_ple_gate, w_ple_proj, final_g, loss_target, m_norm1_g, m_w_in, m_b_gate, m_pool_w, m_pool_scale, m_pool_proj, m_conv_w, m_conv_b, m_w_rg, m_b_rg, m_w_ig, m_b_ig, m_lru_lambda, m_lru_proj, m_w_out, m_norm2_g, m_w_ffn_in, m_w_ffn_out, m_ple_norm_g, m_w_ple_gate, m_w_ple_proj, m_final_g, v_norm1_g, v_w_in, v_b_gate, v_pool_w, v_pool_scale, v_pool_proj, v_conv_w, v_conv_b, v_w_rg, v_b_rg, v_w_ig, v_b_ig, v_lru_lambda, v_lru_proj, v_w_out, v_norm2_g, v_w_ffn_in, v_w_ffn_out, v_ple_norm_g, v_w_ple_gate, v_w_ple_proj, v_final_g):
    axes = ("x", "y", "c")
    me = 4 * lax.axis_index("x") + 2 * lax.axis_index("y") + lax.axis_index("c")
    x2 = x[0]
    p2 = p[0, 0]
    tgt = loss_target[0]

    n_small = (CONV + 2) * 128
    small_terms = _split3(jnp.concatenate([conv_w[0].reshape(-1), b_gate[0].reshape(-1)]))
    small_rows = jnp.pad(small_terms, ((0, 16 - 3), (0, D - n_small)))
    own_first = jnp.concatenate([w_in[0].T.astype(_MXU), small_rows.astype(_MXU)], axis=0)
    own_second = jnp.concatenate([
        w_ffn_in[0].T.astype(_MXU),
        jnp.pad(pool_proj[0].T, ((0, 0), (0, D - PW))).astype(_MXU),
        jnp.pad(w_ple_proj[0].T, ((0, 0), (0, D - PLE))).astype(_MXU),
        lru_proj[0].astype(_MXU), w_out[0].astype(_MXU), w_ffn_out[0].astype(_MXU), w_ple_gate[0].astype(_MXU),
    ], axis=0)
    gw_first = _all_gather_weights(own_first)
    off = W_OFF["f32s"][0]
    st = gw_first[:, off:off + 3, :n_small].astype(F32)
    sf = st[:, 0] + st[:, 1] + st[:, 2]
    conv_w_full = sf[:, :CONV * 128].reshape(NDEV, CONV, 128).transpose(1, 0, 2).reshape(CONV, D)
    b_gate_full = sf[:, CONV * 128:].reshape(NDEV, 2, 128).transpose(1, 0, 2).reshape(2, D)

    small = (pool_w[0].astype(_MXU), pool_scale, conv_w_full, conv_b, w_rg[0].astype(_MXU), b_rg[0],
             w_ig[0].astype(_MXU), b_ig[0], lru_lambda, b_gate_full)

    u, z, gw = _inproj_fwd(x2, norm1_g, gw_first, own_second)
    h, yl, merged, y_pool, y_lru, h1, *saved = _mixer_fwd(z, x2, gw, small)
    v, gf, uf, h2 = _ffn_fwd(h1, norm2_g, gw)

    dh2, n3, dpg, de, loss_blk, g_ple_norm, g_final = _ple_loss_fwd_bwd(h2, p2, tgt, ple_norm_g, final_g.reshape(1, D), gw)
    dff, act = _ffn_bwd_hidden(dh2, gf, uf, gw)
    dh1, g_norm2 = _proj_norm_bwd(dff, h1, dh2, norm2_g, gw, "wffn", 2 * FF, "ffn_bwd_in")
    (dz, d_ylru, d_ypool, mixed, g_bgate, g_lam, g_brg, g_big, g_convb, g_convw, g_pscale, g_poolw, g_wrg,
     g_wig) = _mixer_bwd(dh1, z, h, y_pool, y_lru, saved, gw, small)
    grad_x, g_norm1 = _proj_norm_bwd(dz, x2, dh1, norm1_g, gw_first, "win", NIN, "inproj_bwd")

    small_shapes = [(1, D), (GROUPS, GD, GD), (1, PW), (1, D), (HEADS, HD, HD), (HEADS, HD), (HEADS, HD, HD),
                    (HEADS, HD), (1, D), (1, D), (1, D), (1, D), (2, D), (CONV, D), (1, 1)]
    small_part = _small_pack([g_norm1, g_poolw, g_pscale, g_convb, g_wrg, g_brg, g_wig, g_big, g_lam, g_norm2,
                              g_ple_norm, g_final, g_bgate, g_convw, loss_blk[0:1, 0:1]])
    riders = [_grad_matmul(n3, dpg, "grad_w_ple_gate"), _grad_matmul(yl, d_ylru, "grad_lru_proj"),
              _grad_matmul(merged, dh1, "grad_w_out"), _grad_matmul(d_ypool, mixed, "grad_pool_proj"),
              _grad_matmul(de, p2, "grad_w_ple_proj")]
    part_wffo = _grad_matmul(act, dh2, "grad_w_ffn_out")
    rs_wffn = _grad_matmul_rs(dff, v, "grad_w_ffn_in", 704, extras=[part_wffo], narrow=_MXU)
    rs_win = _grad_matmul_rs(dz, u, "grad_w_in", 576, extras=riders + [small_part], narrow=_MXU, tail=SMALL_ROWS)

    def reduced(parts, name):
        return [_sum_arrays([t_own, landed[0], landed[1], landed[2]], "rs_sum_" + name + str(n))
                for n, (t_own, landed) in enumerate(parts)]

    red_wffn, = reduced(rs_wffn, "wffn")
    red_win, red_small = reduced(rs_win, "win")
    g_w_in = red_win[:576].T
    g_w_ffn_in = red_wffn[:704].T
    g_w_ffn_out = red_wffn[704:]
    g_w_ple_gate, g_lru_proj, g_w_out = red_win[576:704], red_win[704:832], red_win[832:960]
    g_pool_proj = red_win[960:1088, :PW].T
    g_w_ple_proj = red_win[1088:1216, :PLE].T
    small_red = _all_gather_small(red_small)
    (gs_norm1, gs_poolw, gs_pscale, gs_convb, gs_wrg, gs_brg, gs_wig, gs_big, gs_lam, gs_norm2, gs_ple_norm,
     gs_final, gs_bgate, gs_convw, loss_sum) = _small_unpack(small_red, small_shapes)
    loss = loss_sum[0, 0]
    g_b_gate = lax.dynamic_slice_in_dim(gs_bgate, me * 128, 128, axis=1)
    g_conv_w = lax.dynamic_slice_in_dim(gs_convw, me * 128, 128, axis=1)

    grads = {
        "norm1_g": gs_norm1, "w_in": g_w_in[None], "b_gate": g_b_gate[None], "pool_w": gs_poolw[None],
        "pool_scale": gs_pscale, "pool_proj": g_pool_proj[None], "conv_w": g_conv_w[None], "conv_b": gs_convb,
        "w_rg": gs_wrg[None], "b_rg": gs_brg[None], "w_ig": gs_wig[None], "b_ig": gs_big[None], "lru_lambda": gs_lam,
        "lru_proj": g_lru_proj[None], "w_out": g_w_out[None], "norm2_g": gs_norm2, "w_ffn_in": g_w_ffn_in[None],
        "w_ffn_out": g_w_ffn_out[None], "ple_norm_g": gs_ple_norm, "w_ple_gate": g_w_ple_gate[None],
        "w_ple_proj": g_w_ple_proj[None], "final_g": gs_final.reshape(D),
    }
    weights = dict(norm1_g=norm1_g, w_in=w_in, b_gate=b_gate, pool_w=pool_w, pool_scale=pool_scale, pool_proj=pool_proj,
                   conv_w=conv_w, conv_b=conv_b, w_rg=w_rg, b_rg=b_rg, w_ig=w_ig, b_ig=b_ig, lru_lambda=lru_lambda,
                   lru_proj=lru_proj, w_out=w_out, norm2_g=norm2_g, w_ffn_in=w_ffn_in, w_ffn_out=w_ffn_out,
                   ple_norm_g=ple_norm_g, w_ple_gate=w_ple_gate, w_ple_proj=w_ple_proj, final_g=final_g)
    moments_m = dict(norm1_g=m_norm1_g, w_in=m_w_in, b_gate=m_b_gate, pool_w=m_pool_w, pool_scale=m_pool_scale,
                     pool_proj=m_pool_proj, conv_w=m_conv_w, conv_b=m_conv_b, w_rg=m_w_rg, b_rg=m_b_rg, w_ig=m_w_ig,
                     b_ig=m_b_ig, lru_lambda=m_lru_lambda, lru_proj=m_lru_proj, w_out=m_w_out, norm2_g=m_norm2_g,
                     w_ffn_in=m_w_ffn_in, w_ffn_out=m_w_ffn_out, ple_norm_g=m_ple_norm_g, w_ple_gate=m_w_ple_gate,
                     w_ple_proj=m_w_ple_proj, final_g=m_final_g)
    moments_v = dict(norm1_g=v_norm1_g, w_in=v_w_in, b_gate=v_b_gate, pool_w=v_pool_w, pool_scale=v_pool_scale,
                     pool_proj=v_pool_proj, conv_w=v_conv_w, conv_b=v_conv_b, w_rg=v_w_rg, b_rg=v_b_rg, w_ig=v_w_ig,
                     b_ig=v_b_ig, lru_lambda=v_lru_lambda, lru_proj=v_lru_proj, w_out=v_w_out, norm2_g=v_norm2_g,
                     w_ffn_in=v_w_ffn_in, w_ffn_out=v_w_ffn_out, ple_norm_g=v_ple_norm_g, w_ple_gate=v_w_ple_gate,
                     w_ple_proj=v_w_ple_proj, final_g=v_final_g)
    names = list(weights)
    big = ("w_in", "w_ffn_in", "w_ffn_out", "lru_proj", "w_out", "w_ple_gate", "pool_proj", "w_ple_proj")
    slab_space = {"w_in": red_win[:576], "w_ffn_in": red_wffn[:704]}
    delta, new_m, new_v = {}, {}, {}
    for n in big:
        sh = weights[n].shape
        if n in slab_space:
            as2d = lambda a: a[0].T
            back = lambda a: a.T[None]
            g2d = slab_space[n]
        else:
            as2d = lambda a: a.reshape(sh[-2], sh[-1])
            back = lambda a: a.reshape(sh)
            g2d = as2d(grads[n])
        d_, m_, v_ = _adamw(as2d(weights[n]), g2d, as2d(moments_m[n]), as2d(moments_v[n]), "adamw_" + n)
        delta[n], new_m[n], new_v[n] = back(d_), back(m_), back(v_)
    rest = [n for n in names if n not in big]
    rest_shapes = [weights[n].shape for n in rest]
    packed = [_small_pack([src[n] for n in rest]) for src in (weights, grads, moments_m, moments_v)]
    d_, m_, v_ = _adamw(*packed, "adamw_small")
    for n, a, b_, c_ in zip(rest, _small_unpack(d_, rest_shapes), _small_unpack(m_, rest_shapes),
                            _small_unpack(v_, rest_shapes)):
        delta[n], new_m[n], new_v[n] = a, b_, c_

    return (loss, grad_x[None], *[grads[n] for n in names], *[delta[n] for n in names],
            *[new_m[n] for n in names], *[new_v[n] for n in names])
```

```python
import functools
import math

import jax
import jax.numpy as jnp
from jax import lax
from jax.experimental import pallas as pl
from jax.experimental.pallas import tpu as pltpu

F32 = jnp.float32
D = 1024
NIN = 4608
PW = 512
FF = 2816
PLE = 256
HEADS, HD = 8, 128
GROUPS, GD = 4, 128
WINDOWS = (2, 4, 8, 16)
HALO = 16
CONV = 4
EPS = 1e-6
LRU_C = 8.0
NDEV = 8
MESH = pl.DeviceIdType.MESH

ADAM_LR, ADAM_B1, ADAM_B2, ADAM_EPS, ADAM_WD, ADAM_STEP = 0.001, 0.9, 0.999, 1e-08, 0.01, 10

_MXU = jnp.bfloat16
TM = 512
TM_SEQ = 256
VMEM_LIMIT = 56 * 1024 * 1024
W_FIRST = (("win", 576), ("f32s", 16))
W_SECOND = (("wffn", 704), ("pproj", 128), ("ple", 128), ("lru", 128), ("wout", 128), ("wffo", 352), ("wpg", 128))
W_OFF = {}
for _slabs in (W_FIRST, W_SECOND):
    _o = 0
    for _n, _r in _slabs:
        W_OFF[_n] = (_o, _r)
        _o += _r
SMALL_ROWS = 48


def _cparams(n_axes=1, vmem=VMEM_LIMIT):
    return pltpu.CompilerParams(dimension_semantics=("arbitrary",) * n_axes, vmem_limit_bytes=vmem)


def _my_pos():
    return lax.axis_index("x"), lax.axis_index("y"), lax.axis_index("c")


def _nt(a, b):
    return lax.dot_general(a, b, (((1,), (1,)), ((), ())), preferred_element_type=F32)


def _nn(a, b):
    return lax.dot_general(a, b, (((1,), (0,)), ((), ())), preferred_element_type=F32)


def _tn(a, b):
    return lax.dot_general(a, b, (((0,), (0,)), ((), ())), preferred_element_type=F32)


def _sigmoid(x):
    return 0.5 * jnp.tanh(0.5 * x) + 0.5


_GELU_K = math.sqrt(2.0 / math.pi)


def _gelu_and_grad(x):
    x2 = x * x
    inner = _GELU_K * (x + 0.044715 * x2 * x)
    t = jnp.tanh(inner)
    g = 0.5 * x * (1.0 + t)
    dg = 0.5 * (1.0 + t) + 0.5 * x * (1.0 - t * t) * _GELU_K * (1.0 + 3.0 * 0.044715 * x2)
    return g, dg


def _softplus_neg(lam):
    x = -lam
    t = jnp.exp(-jnp.abs(x))
    u = 1.0 + t
    l1p = jnp.where(u == 1.0, t, jnp.log(u) * t / (u - 1.0))
    return jnp.maximum(x, 0.0) + l1p, _sigmoid(x)


def _start_slab_loads(g_ref, name, dst_ref, sems, base, width=D):
    off, rows = W_OFF[name]
    copies = []
    for k in range(NDEV):
        if width == D:
            src = g_ref.at[k, pl.ds(off, rows), :]
        else:
            src = g_ref.at[k, pl.ds(off, rows), pl.ds(0, width)]
        cp = pltpu.make_async_copy(src, dst_ref.at[pl.ds(k * rows, rows), :], sems.at[base + k])
        cp.start()
        copies.append(cp)
    return copies


def _load_weights(g_ref, items, sems):
    copies = []
    for n, (name, dst, width) in enumerate(items):
        copies += _start_slab_loads(g_ref, name, dst, sems, n * NDEV, width)
    for cp in copies:
        cp.wait()


def _gather_phases(own_ref, out_ref, stage, send_sems, recv_sems, local_sem):
    x, y, c = _my_pos()
    me, sibling = (x, y, c), (x, y, 1 - c)
    chips = [(1 - x, y), (x, 1 - y), (1 - x, 1 - y)]

    def slab(px, py, pc):
        return out_ref.at[4 * px + 2 * py + pc]

    def copy(k, block, to, src=None):
        return pltpu.make_async_remote_copy(
            src_ref=slab(*block) if src is None else src, dst_ref=slab(*block),
            send_sem=send_sems.at[k], recv_sem=recv_sems.at[k], device_id=to, device_id_type=MESH)

    mine = pltpu.make_async_copy(stage, slab(*me), local_sem)
    first = [copy(0, me, sibling, src=stage)] + [copy(1 + j, me, (*chip, c), src=stage) for j, chip in enumerate(chips)]
    passed = [copy(4 + j, (*chip, c), sibling) for j, chip in enumerate(chips)]

    def send_mine():
        pltpu.sync_copy(own_ref, stage)
        mine.start()
        for cp in first:
            cp.start()

    def pass_on(js):
        for j in js:
            copy(1 + j, (*chips[j], c), me).wait_recv()
            passed[j].start()

    def finish():
        copy(0, sibling, me).wait_recv()
        for j, chip in enumerate(chips):
            copy(4 + j, (*chip, 1 - c), me).wait_recv()
        for cp in first + passed:
            cp.wait_send()
        mine.wait()

    return send_mine, pass_on, finish


def _all_gather_weights(own):
    rows, cols = own.shape

    def body(own_ref, out_ref, stage, send_sems, recv_sems, local_sem):
        send_mine, pass_on, finish = _gather_phases(own_ref, out_ref, stage, send_sems, recv_sems, local_sem)
        send_mine()
        pass_on((0, 1, 2))
        finish()

    return pl.pallas_call(
        body, name="ag_weights",
        out_shape=jax.ShapeDtypeStruct((NDEV, rows, cols), own.dtype),
        in_specs=[pl.BlockSpec(memory_space=pl.ANY)],
        out_specs=pl.BlockSpec(memory_space=pl.ANY),
        scratch_shapes=[pltpu.VMEM((rows, cols), own.dtype), pltpu.SemaphoreType.DMA((7,)),
                        pltpu.SemaphoreType.DMA((7,)), pltpu.SemaphoreType.DMA],
        compiler_params=pltpu.CompilerParams(vmem_limit_bytes=VMEM_LIMIT),
    )(own)


def _all_gather_small(piece):
    rows = piece.shape[0]

    def body(p_ref, out_ref, send_sems, recv_sems, local_sem):
        x, y, c = _my_pos()
        me = 4 * x + 2 * y + c
        mine = pltpu.make_async_copy(p_ref, out_ref.at[pl.ds(pl.multiple_of(me * rows, 8), rows), :], local_sem)
        mine.start()
        sends = []
        peers = []
        for r in range(1, NDEV):
            px = 1 - x if (r >> 2) & 1 else x
            py = 1 - y if (r >> 1) & 1 else y
            pc = 1 - c if r & 1 else c
            peers.append((px, py, pc))
            cp = pltpu.make_async_remote_copy(
                src_ref=p_ref, dst_ref=out_ref.at[pl.ds(pl.multiple_of(me * rows, 8), rows), :],
                send_sem=send_sems.at[r - 1], recv_sem=recv_sems.at[r - 1], device_id=(px, py, pc),
                device_id_type=MESH)
            cp.start()
            sends.append(cp)
        for r, (px, py, pc) in enumerate(peers):
            them = 4 * px + 2 * py + pc
            pltpu.make_async_remote_copy(
                src_ref=p_ref, dst_ref=out_ref.at[pl.ds(pl.multiple_of(them * rows, 8), rows), :],
                send_sem=send_sems.at[r], recv_sem=recv_sems.at[r], device_id=(px, py, pc),
                device_id_type=MESH).wait_recv()
        for cp in sends:
            cp.wait_send()
        mine.wait()

    return pl.pallas_call(
        body, name="ag_small",
        out_shape=jax.ShapeDtypeStruct((NDEV * rows, piece.shape[1]), piece.dtype),
        in_specs=[pl.BlockSpec(memory_space=pltpu.VMEM)],
        out_specs=pl.BlockSpec(memory_space=pl.ANY),
        scratch_shapes=[pltpu.SemaphoreType.DMA((7,)), pltpu.SemaphoreType.DMA((7,)), pltpu.SemaphoreType.DMA],
    )(piece)


def _row_block(rows, target=512, mult=8):
    b = min(rows, target) // mult * mult
    while rows % b:
        b -= mult
    return b


def _sum_arrays(arrs, name, narrow=None, target=464):
    rows, cols = arrs[0].shape
    br = _row_block(rows, target, 16)
    n = len(arrs)

    def body(*refs):
        acc = refs[0][...].astype(F32)
        for r in refs[1:n]:
            acc = acc + r[...].astype(F32)
        refs[n][...] = acc
        if narrow is not None:
            refs[n + 1][...] = acc.astype(narrow)

    spec = pl.BlockSpec((br, cols), lambda i: (i, 0))
    shape = jax.ShapeDtypeStruct((rows, cols), F32)
    if narrow is None:
        out_shape, out_specs = shape, spec
    else:
        out_shape, out_specs = (shape, jax.ShapeDtypeStruct((rows, cols), narrow)), (spec, spec)
    return pl.pallas_call(
        body, name=name, grid=(rows // br,), out_shape=out_shape,
        in_specs=[spec] * n, out_specs=out_specs, compiler_params=_cparams(1),
    )(*arrs)


def _adamw(w, g, m, v, name):
    rows, cols = w.shape
    br = _row_block(rows, 256)

    def body(w_ref, g_ref, m_ref, v_ref, d_ref, nm_ref, nv_ref):
        g_ = g_ref[...]
        m_ = ADAM_B1 * m_ref[...] + (1.0 - ADAM_B1) * g_
        v_ = ADAM_B2 * v_ref[...] + (1.0 - ADAM_B2) * (g_ * g_)
        m_hat = m_ / (1.0 - ADAM_B1 ** ADAM_STEP)
        v_hat = v_ / (1.0 - ADAM_B2 ** ADAM_STEP)
        d_ref[...] = -ADAM_LR * (m_hat / (jnp.sqrt(v_hat) + ADAM_EPS) + ADAM_WD * w_ref[...])
        nm_ref[...] = m_
        nv_ref[...] = v_

    spec = pl.BlockSpec((br, cols), lambda i: (i, 0))
    shape = jax.ShapeDtypeStruct((rows, cols), F32)
    return pl.pallas_call(
        body, name=name, grid=(rows // br,), out_shape=(shape, shape, shape),
        in_specs=[spec] * 4, out_specs=(spec, spec, spec), compiler_params=_cparams(1),
    )(w, g, m, v)


_CHIP_FLIPS = (2, 3, 1, 0)


def _grad_matmul(lhs, rhs, name):
    s, r = lhs.shape
    k = rhs.shape[1]
    tm = min(TM, s)

    def body(l_ref, r_ref, o_ref):
        @pl.when(pl.program_id(0) == 0)
        def _():
            o_ref[...] = jnp.zeros_like(o_ref)

        o_ref[:, pl.ds(0, k)] += _tn(l_ref[...].astype(_MXU), r_ref[...].astype(_MXU))

    return pl.pallas_call(
        body, name=name, grid=(s // tm,),
        out_shape=jax.ShapeDtypeStruct((r, D), F32),
        in_specs=[pl.BlockSpec((tm, r), lambda i: (i, 0)), pl.BlockSpec((tm, k), lambda i: (i, 0))],
        out_specs=pl.BlockSpec((r, D), lambda i: (0, 0)),
        compiler_params=_cparams(1),
    )(lhs, rhs)


def _grad_matmul_rs(lhs, rhs, name, rows, extras=(), narrow=None, tail=0):
    s, r8 = lhs.shape
    k = rhs.shape[1]
    tm = min(TM, s)
    nt = s // tm
    cpb = 1
    nblk = 4 // cpb
    nx = len(extras)
    ers = [e.shape[0] // NDEV for e in extras]
    er = sum(ers)
    srows = rows + er
    brows = 2 * cpb * srows
    groups = [(0, srows - tail, F32 if narrow is None else narrow)] + ([(srows - tail, tail, F32)] if tail else [])
    ng = len(groups)
    mid = min(nt - 1, max(1, nt // 4))

    def flip_of(p):
        return jnp.where(p == 0, 2, jnp.where(p == 1, 3, jnp.where(p == 2, 1, 0)))

    def block_col(b):
        x, y, _ = _my_pos()
        return (2 * x + y) ^ flip_of(b)

    def body(*refs):
        l_ref, r_ref = refs[:2]
        x_refs = refs[2:2 + nx]
        rest = refs[2 + nx:]
        town_ref = rest[0]
        lici_refs = rest[1:1 + ng]
        acc, stage = rest[1 + ng:3 + ng]
        send_bufs = rest[3 + ng:3 + 2 * ng]
        dsend, drecv, isend, irecv, xsem = rest[3 + 2 * ng:]
        b = pl.program_id(0)
        i = pl.program_id(1)
        x, y, c = _my_pos()
        mine = 2 * x + y
        sibling = (x, y, 1 - c)

        def chip_at(p):
            return mine ^ _CHIP_FLIPS[p]

        def slab_rows(p, parity):
            within = 0 if cpb == 1 else (chip_at(p) & 1) * 2
            return pl.ds(pl.multiple_of((within + parity) * srows, 8), srows)

        def push(p, slot):
            return pltpu.make_async_remote_copy(
                src_ref=acc.at[slot, slab_rows(p, 1 - c), :], dst_ref=stage.at[p % 2],
                send_sem=dsend.at[p], recv_sem=drecv.at[p], device_id=sibling, device_id_type=MESH)

        def ici(p):
            ch = chip_at(p)
            return [pltpu.make_async_remote_copy(
                src_ref=send_bufs[g].at[p % 2], dst_ref=lici_refs[g].at[p], send_sem=isend.at[3 * g + p],
                recv_sem=irecv.at[3 * g + p], device_id=(ch >> 1, ch & 1, c), device_id_type=MESH) for g in range(ng)]

        def extra_loads(p, slot):
            copies = []
            within = 0 if cpb == 1 else (chip_at(p) & 1) * 2
            for parity in range(2):
                off = rows
                for n, (x_ref, e) in enumerate(zip(x_refs, ers)):
                    src = x_ref.at[pl.ds(pl.multiple_of((2 * chip_at(p) + parity) * e, 8), e), :]
                    dst = acc.at[slot, pl.ds(pl.multiple_of((within + parity) * srows + off, 8), e), :]
                    copies.append(pltpu.make_async_copy(src, dst, xsem.at[(p * 2 + parity) * nx + n]))
                    off += e
            return copies

        def combine(p, slot):
            push(p, slot).wait_recv()
            total = acc[slot, slab_rows(p, c), :] + stage[p % 2]
            if p == 3:
                stage[p % 2] = total
                pltpu.sync_copy(stage.at[p % 2], town_ref)
            else:
                if p == 2:
                    for cp in ici(0):
                        cp.wait_send()
                for g, (r0, n, dt) in enumerate(groups):
                    send_bufs[g][p % 2] = total[r0:r0 + n, :].astype(dt)
                for cp in ici(p):
                    cp.start()

        for bb in range(nblk):
            slot = bb % 2
            positions = list(range(bb * cpb, (bb + 1) * cpb))

            @pl.when(jnp.logical_and(b == bb, i == 0))
            def _(bb=bb, slot=slot, positions=positions):
                if bb >= 2:
                    for p in range((bb - 2) * cpb, (bb - 1) * cpb):
                        push(p, slot).wait_send()
                for q in range(2 * cpb):
                    acc[slot, pl.ds(q * srows, rows), :] = jnp.zeros((rows, D), F32)
                for p in positions:
                    for cp in extra_loads(p, slot):
                        cp.start()

            if bb >= 1:
                @pl.when(jnp.logical_and(b == bb, i == mid))
                def _(bb=bb):
                    for p in range((bb - 1) * cpb, bb * cpb):
                        combine(p, (bb - 1) % 2)

        res = _tn(l_ref[...].astype(_MXU), r_ref[...].astype(_MXU))
        slot_now = b % 2
        for q in range(2 * cpb):
            acc[slot_now, pl.ds(q * srows, rows), pl.ds(0, k)] += res[q * rows:(q + 1) * rows, :]

        for bb in range(nblk):
            slot = bb % 2
            positions = list(range(bb * cpb, (bb + 1) * cpb))

            @pl.when(jnp.logical_and(b == bb, i == nt - 1))
            def _(bb=bb, slot=slot, positions=positions):
                for p in positions:
                    for cp in extra_loads(p, slot):
                        cp.wait()
                for p in positions:
                    push(p, slot).start()
                if bb == nblk - 1:
                    for p in positions:
                        combine(p, slot)
                    for p in range(max(0, (nblk - 2) * cpb), 4):
                        push(p, slot).wait_send()
                    for p in range(1, 3):
                        for cp in ici(p):
                            cp.wait_send()
                    for p in range(3):
                        for cp in ici(p):
                            cp.wait_recv()

    in_specs = [pl.BlockSpec((tm, 2 * cpb * rows), lambda b, i: (i, block_col(b))),
                pl.BlockSpec((tm, k), lambda b, i: (i, 0))]
    any_spec = pl.BlockSpec(memory_space=pl.ANY)
    in_specs += [any_spec] * nx
    args = [lhs, rhs, *extras]
    outs = pl.pallas_call(
        body, name=name, grid=(nblk, nt),
        out_shape=(jax.ShapeDtypeStruct((srows, D), F32),)
        + tuple(jax.ShapeDtypeStruct((3, n, D), dt) for _, n, dt in groups),
        in_specs=in_specs, out_specs=(any_spec,) * (1 + ng),
        scratch_shapes=[pltpu.VMEM((2, brows, D), F32), pltpu.VMEM((2, srows, D), F32)]
        + [pltpu.VMEM((2, n, D), dt) for _, n, dt in groups]
        + [pltpu.SemaphoreType.DMA((4,)), pltpu.SemaphoreType.DMA((4,)), pltpu.SemaphoreType.DMA((3 * ng,)),
           pltpu.SemaphoreType.DMA((3 * ng,)), pltpu.SemaphoreType.DMA((max(1, 8 * nx),))],
        compiler_params=_cparams(2),
    )(*args)
    t_own = outs[0]
    return [(t_own[r0:r0 + n], landed) for (r0, n, _), landed in zip(groups, outs[1:])]


def _inproj_fwd(x, g1, gw, own_second):
    s = x.shape[0]
    tm = min(TM, s)
    nt = s // tm
    nchunk = 4
    cw = NIN // nchunk
    rows2, cols2 = own_second.shape

    def body(x_ref, g1_ref, gw_ref, own_ref, u_ref, z_ref, gw2_ref, w_vmem, stage, sems, send_sems, recv_sems,
             local_sem):
        i = pl.program_id(0)
        send_mine, pass_on, finish = _gather_phases(own_ref, gw2_ref, stage, send_sems, recv_sems, local_sem)

        @pl.when(i == 0)
        def _():
            send_mine()
            _load_weights(gw_ref, [("win", w_vmem, D)], sems)

        @pl.when(i == nt // 2)
        def _():
            pass_on((0, 1))

        @pl.when(i == (7 * nt) // 8)
        def _():
            pass_on((2,))

        xv = x_ref[...]
        inv = lax.rsqrt(jnp.mean(xv * xv, axis=-1, keepdims=True) + EPS)
        u = (xv * inv * g1_ref[...]).astype(_MXU)
        u_ref[...] = u
        for ch in range(nchunk):
            z_ref[:, pl.ds(ch * cw, cw)] = _nt(u, w_vmem[pl.ds(ch * cw, cw), :])

        @pl.when(i == nt - 1)
        def _():
            finish()

    any_spec = pl.BlockSpec(memory_space=pl.ANY)
    return pl.pallas_call(
        body, name="inproj_fwd", grid=(nt,),
        out_shape=(jax.ShapeDtypeStruct((s, D), _MXU), jax.ShapeDtypeStruct((s, NIN), F32),
                   jax.ShapeDtypeStruct((NDEV, rows2, cols2), own_second.dtype)),
        in_specs=[pl.BlockSpec((tm, D), lambda i: (i, 0)), pl.BlockSpec((1, D), lambda i: (0, 0)), any_spec, any_spec],
        out_specs=(pl.BlockSpec((tm, D), lambda i: (i, 0)), pl.BlockSpec((tm, NIN), lambda i: (i, 0)), any_spec),
        scratch_shapes=[pltpu.VMEM((NIN, D), _MXU), pltpu.VMEM((rows2, cols2), own_second.dtype),
                        pltpu.SemaphoreType.DMA((NDEV,)), pltpu.SemaphoreType.DMA((7,)), pltpu.SemaphoreType.DMA((7,)),
                        pltpu.SemaphoreType.DMA],
        compiler_params=_cparams(1),
    )(x, g1, gw, own_second)


def _pool_tile(pbuf, t0, tm, pw_ref, scale_ref):
    t = t0 + lax.broadcasted_iota(jnp.int32, (tm, GD), 0)
    pooled, mixed_pre = [], []
    for g, w in enumerate(WINDOWS):
        cs = pl.ds(g * GD, GD)
        cur = pbuf[pl.ds(HALO, tm), cs]
        acc = cur
        for d in range(1, w):
            acc = acc + pbuf[pl.ds(HALO - d, tm), cs]
        cnt = jnp.minimum(t + 1, w).astype(F32)
        pg = acc / cnt - cur
        pooled.append(pg)
        mixed_pre.append(_nn(pg.astype(_MXU), pw_ref[g]))
    return pooled, mixed_pre


def _lru_gates_head(hh, lbuf, start, tm, cw_ref, cb_ref, wrg_ref, brg_ref, wig_ref, big_ref, sp):
    cs = pl.ds(hh * HD, HD)
    xc = cb_ref[:, cs] + cw_ref[pl.ds(CONV - 1, 1), cs] * lbuf[pl.ds(HALO, tm), cs]
    for k in range(CONV - 1):
        xc = xc + cw_ref[pl.ds(k, 1), cs] * lbuf[pl.ds(HALO - (CONV - 1) + k, tm), cs]
    xcm = xc.astype(_MXU)
    r = _sigmoid(_nn(xcm, wrg_ref[hh]) + brg_ref[pl.ds(hh, 1), :])
    ig = _sigmoid(_nn(xcm, wig_ref[hh]) + big_ref[pl.ds(hh, 1), :])
    a = jnp.exp(-LRU_C * r * sp[:, hh * HD:(hh + 1) * HD])
    one_m = 1.0 - a * a
    live = jnp.logical_and(one_m > 0.0, jnp.logical_not(start))
    inv_mult = lax.rsqrt(jnp.where(live, one_m, 1.0))
    mult = jnp.where(live, one_m * inv_mult, jnp.where(start, 1.0, 0.0))
    return xc, r, ig, a, live, inv_mult, mult


def _seg_layout(tm):
    seg = tm // 8
    return seg, seg + 8


def _to_segments(dst_ref, hh, val, tm):
    seg, pitch = _seg_layout(tm)
    for s in range(8):
        dst_ref[hh, pl.ds(s * pitch, seg), :] = val[s * seg:(s + 1) * seg, :]


def _from_segments(src_ref, hh, tm):
    seg, pitch = _seg_layout(tm)
    return jnp.concatenate([src_ref[hh, pl.ds(s * pitch, seg), :] for s in range(8)], axis=0)


def _segment_scan(a_ref, b_ref, out_ref, hk, pk, carry_ref, tm, reverse):
    seg, pitch = _seg_layout(tm)
    row = lax.broadcasted_iota(jnp.int32, (8, HD), 0)
    order = range(seg - 1, -1, -1) if reverse else range(seg)
    for hh in range(HEADS):
        cs = pl.ds(hh * HD, HD)
        if reverse:
            a0 = a_ref[hh, pl.ds(0, 8, stride=pitch), :]
            a_wrap = jnp.where(row <= 6, pltpu.roll(a0, 7, 0), 1.0)
        hv = jnp.zeros((8, HD), F32)
        pv = jnp.ones((8, HD), F32)
        for k in order:
            if not reverse:
                av = a_ref[hh, pl.ds(k, 8, stride=pitch), :]
            elif k + 1 < seg:
                av = a_ref[hh, pl.ds(k + 1, 8, stride=pitch), :]
            else:
                av = a_wrap
            hv = av * hv + b_ref[hh, pl.ds(k, 8, stride=pitch), :]
            pv = av * pv
            hk[hh, pl.ds(8 * k, 8), :] = hv
            pk[hh, pl.ds(8 * k, 8), :] = pv
        for d in (1, 2, 4):
            if reverse:
                keep, sh = row < 8 - d, 8 - d
            else:
                keep, sh = row >= d, d
            hv = hv + pv * jnp.where(keep, pltpu.roll(hv, sh, 0), 0.0)
            pv = pv * jnp.where(keep, pltpu.roll(pv, sh, 0), 1.0)
        cin = carry_ref[:, cs]
        ends = hv + pv * cin
        if reverse:
            enter = jnp.where(row <= 6, pltpu.roll(ends, 7, 0), cin)
            carry_ref[:, cs] = jnp.broadcast_to((a0 * ends)[0:1, :], (8, HD))
        else:
            enter = jnp.where(row >= 1, pltpu.roll(ends, 1, 0), cin)
            carry_ref[:, cs] = jnp.broadcast_to(ends[7:8, :], (8, HD))
        for k in range(seg):
            out_ref[hh, pl.ds(k, 8, stride=pitch), :] = hk[hh, pl.ds(8 * k, 8), :] + pk[hh, pl.ds(8 * k, 8), :] * enter


def _mixer_fwd(z, x, gw, small):
    s = x.shape[0]
    tm = min(TM_SEQ, s)
    (pool_w, pool_scale, conv_w, conv_b, w_rg, b_rg, w_ig, b_ig, lam, b_gate) = small

    def body(z_ref, x_ref, gw_ref, pw_ref, ps_ref, cw_ref, cb_ref, wrg_ref, brg_ref, wig_ref, big_ref, lam_ref,
             bg_ref, h_ref, yl_ref, mg_ref, yp_ref, yr_ref, h1_ref, a_ref, r_ref, ig_ref, xc_ref,
             pprojT, lru_w, wout_w, pbuf, lbuf, a_s, b_s, h_s, hk, pk, hcar, sems):
        i = pl.program_id(0)
        t0 = i * tm

        @pl.when(i == 0)
        def _():
            _load_weights(gw_ref, [("pproj", pprojT, PW), ("lru", lru_w, D), ("wout", wout_w, D)], sems)
            pbuf[pl.ds(0, HALO), :] = jnp.zeros((HALO, PW), F32)
            lbuf[pl.ds(0, HALO), :] = jnp.zeros((HALO, D), F32)
            hcar[...] = jnp.zeros_like(hcar)

        pbuf[pl.ds(HALO, tm), :] = z_ref[:, pl.ds(0, PW)]
        _, mixed_pre = _pool_tile(pbuf, t0, tm, pw_ref, ps_ref)
        mixed = jnp.concatenate(mixed_pre, axis=1) * ps_ref[...]
        y_pool = _nt(mixed.astype(_MXU), pprojT[...])
        pbuf[pl.ds(0, HALO), :] = pbuf[pl.ds(tm, HALO), :]

        lbuf[pl.ds(HALO, tm), :] = z_ref[:, pl.ds(PW, D)]
        sp, _ = _softplus_neg(lam_ref[...])
        start = (t0 + lax.broadcasted_iota(jnp.int32, (tm, HD), 0)) == 0
        for hh in range(HEADS):
            xc, r, ig, a, _, _, mult = _lru_gates_head(hh, lbuf, start, tm, cw_ref, cb_ref, wrg_ref, brg_ref,
                                                       wig_ref, big_ref, sp)
            _to_segments(a_s, hh, a, tm)
            _to_segments(b_s, hh, mult * ig * xc, tm)
            cs = pl.ds(hh * HD, HD)
            a_ref[:, cs] = a
            r_ref[:, cs] = r.astype(_MXU)
            ig_ref[:, cs] = ig.astype(_MXU)
            xc_ref[:, cs] = xc.astype(_MXU)
        lbuf[pl.ds(0, HALO), :] = lbuf[pl.ds(tm, HALO), :]
        _segment_scan(a_s, b_s, h_s, hk, pk, hcar, tm, reverse=False)
        for hh in range(HEADS):
            h_ref[:, pl.ds(hh * HD, HD)] = _from_segments(h_s, hh, tm)
        gel, _ = _gelu_and_grad(z_ref[:, pl.ds(PW + D, D)])
        yl = (h_ref[...] * gel).astype(_MXU)
        yl_ref[...] = yl
        y_lru = _nn(yl, lru_w[...])

        g0 = _sigmoid(z_ref[:, pl.ds(PW + 2 * D, D)] + bg_ref[pl.ds(0, 1), :])
        g1 = _sigmoid(z_ref[:, pl.ds(PW + 3 * D, D)] + bg_ref[pl.ds(1, 1), :])
        merged = (g0 * y_pool + g1 * y_lru).astype(_MXU)
        mg_ref[...] = merged
        yp_ref[...] = y_pool.astype(_MXU)
        yr_ref[...] = y_lru.astype(_MXU)
        h1_ref[...] = x_ref[...] + _nn(merged, wout_w[...])

    tok = lambda w, dt: jax.ShapeDtypeStruct((s, w), dt)
    tspec = lambda w: pl.BlockSpec((tm, w), lambda i: (i, 0))
    full = lambda a: pl.BlockSpec(a.shape, lambda i: (0,) * a.ndim)
    seg_buf = pltpu.VMEM((HEADS, 8 * _seg_layout(tm)[1], HD), F32)
    return pl.pallas_call(
        body, name="mixer_fwd", grid=(s // tm,),
        out_shape=(tok(D, F32), tok(D, _MXU), tok(D, _MXU), tok(D, _MXU), tok(D, _MXU), tok(D, F32),
                   tok(D, F32), tok(D, _MXU), tok(D, _MXU), tok(D, _MXU)),
        in_specs=[tspec(NIN), tspec(D), pl.BlockSpec(memory_space=pl.ANY)] + [full(a) for a in small],
        out_specs=(tspec(D),) * 10,
        scratch_shapes=[pltpu.VMEM((D, PW), _MXU), pltpu.VMEM((D, D), _MXU), pltpu.VMEM((D, D), _MXU),
                        pltpu.VMEM((tm + HALO, PW), F32), pltpu.VMEM((tm + HALO, D), F32),
                        seg_buf, seg_buf, seg_buf, pltpu.VMEM((HEADS, tm, HD), F32), pltpu.VMEM((HEADS, tm, HD), F32),
                        pltpu.VMEM((8, D), F32), pltpu.SemaphoreType.DMA((3 * NDEV,))],
        compiler_params=_cparams(1),
    )(z, x, gw, *small)


def _ffn_fwd(h1, g2, gw):
    s = h1.shape[0]
    tm = min(TM, s)
    half = FF // 2

    def body(h1_ref, g2_ref, gw_ref, v_ref, gf_ref, uf_ref, h2_ref, wffnT, wffo, sems):
        @pl.when(pl.program_id(0) == 0)
        def _():
            _load_weights(gw_ref, [("wffn", wffnT, D), ("wffo", wffo, D)], sems)

        hv = h1_ref[...]
        inv = lax.rsqrt(jnp.mean(hv * hv, axis=-1, keepdims=True) + EPS)
        v = (hv * inv * g2_ref[...]).astype(_MXU)
        v_ref[...] = v
        acc = hv
        for ch in range(2):
            cs = pl.ds(ch * half, half)
            gf = _nt(v, wffnT[pl.ds(ch * half, half), :]).astype(_MXU)
            uf = _nt(v, wffnT[pl.ds(FF + ch * half, half), :]).astype(_MXU)
            gf_ref[:, cs] = gf
            uf_ref[:, cs] = uf
            gf32 = gf.astype(F32)
            act = (gf32 * _sigmoid(gf32) * uf.astype(F32)).astype(_MXU)
            acc = acc + _nn(act, wffo[pl.ds(ch * half, half), :])
        h2_ref[...] = acc

    tspec = lambda w: pl.BlockSpec((tm, w), lambda i: (i, 0))
    return pl.pallas_call(
        body, name="ffn_fwd", grid=(s // tm,),
        out_shape=(jax.ShapeDtypeStruct((s, D), _MXU), jax.ShapeDtypeStruct((s, FF), _MXU),
                   jax.ShapeDtypeStruct((s, FF), _MXU), jax.ShapeDtypeStruct((s, D), F32)),
        in_specs=[tspec(D), pl.BlockSpec((1, D), lambda i: (0, 0)), pl.BlockSpec(memory_space=pl.ANY)],
        out_specs=(tspec(D), tspec(FF), tspec(FF), tspec(D)),
        scratch_shapes=[pltpu.VMEM((2 * FF, D), _MXU), pltpu.VMEM((FF, D), _MXU), pltpu.SemaphoreType.DMA((2 * NDEV,))],
        compiler_params=_cparams(1),
    )(h1, g2, gw)


def _rms_bwd(dy, xn, inv, g):
    dg = jnp.sum(dy * xn, axis=0, keepdims=True)
    dxn = dy * g
    dx = inv * (dxn - xn * jnp.mean(dxn * xn, axis=-1, keepdims=True))
    return dx, dg


def _ple_loss_fwd_bwd(h2, p, target, g3, gfin, gw):
    s = h2.shape[0]
    tm = min(TM, s)

    def body(h2_ref, p_ref, t_ref, g3_ref, gf_ref, gw_ref,
             dh2_ref, loss_ref, dg3_ref, dgf_ref, gwpg_ref, gple_ref, wpg, pleT, sems):
        i = pl.program_id(0)

        @pl.when(i == 0)
        def _():
            _load_weights(gw_ref, [("wpg", wpg, D), ("ple", pleT, PLE)], sems)
            for ref in (loss_ref, dg3_ref, dgf_ref, gwpg_ref, gple_ref):
                ref[...] = jnp.zeros_like(ref)

        hv = h2_ref[...]
        inv3 = lax.rsqrt(jnp.mean(hv * hv, axis=-1, keepdims=True) + EPS)
        xn3 = hv * inv3
        n3 = (xn3 * g3_ref[...]).astype(_MXU)
        pg = _sigmoid(_nn(n3, wpg[...]))
        pm = p_ref[...].astype(_MXU)
        e = _nt(pm, pleT[...])
        h3 = hv + pg * e
        invf = lax.rsqrt(jnp.mean(h3 * h3, axis=-1, keepdims=True) + EPS)
        xf = h3 * invf
        diff = xf * gf_ref[...] - t_ref[...]
        loss_ref[...] += jnp.sum(diff * diff) * (0.5 / D)
        dh3, dgf = _rms_bwd(diff * (1.0 / D), xf, invf, gf_ref[...])
        dgf_ref[...] += dgf
        gple_ref[:, pl.ds(0, PLE)] += _tn((dh3 * pg).astype(_MXU), pm)
        dpg = (dh3 * e * pg * (1.0 - pg)).astype(_MXU)
        gwpg_ref[...] += _tn(n3, dpg)
        dn3 = _nt(dpg, wpg[...])
        dx3, dg3 = _rms_bwd(dn3, xn3, inv3, g3_ref[...])
        dg3_ref[...] += dg3
        dh2_ref[...] = dh3 + dx3

    tspec = lambda w: pl.BlockSpec((tm, w), lambda i: (i, 0))
    vec = pl.BlockSpec((1, D), lambda i: (0, 0))
    mat = pl.BlockSpec((D, D), lambda i: (0, 0))
    return pl.pallas_call(
        body, name="ple_loss", grid=(s // tm,),
        out_shape=(jax.ShapeDtypeStruct((s, D), F32), jax.ShapeDtypeStruct((8, 128), F32),
                   jax.ShapeDtypeStruct((1, D), F32), jax.ShapeDtypeStruct((1, D), F32),
                   jax.ShapeDtypeStruct((D, D), F32), jax.ShapeDtypeStruct((D, D), F32)),
        in_specs=[tspec(D), tspec(PLE), tspec(D), vec, vec, pl.BlockSpec(memory_space=pl.ANY)],
        out_specs=(tspec(D), pl.BlockSpec((8, 128), lambda i: (0, 0)), vec, vec, mat, mat),
        scratch_shapes=[pltpu.VMEM((D, D), _MXU), pltpu.VMEM((D, PLE), _MXU), pltpu.SemaphoreType.DMA((2 * NDEV,))],
        compiler_params=_cparams(1),
    )(h2, p, target, g3, gfin, gw)


def _ffn_bwd_hidden(dh2, gf, uf, gw):
    s = dh2.shape[0]
    tm = min(TM, s)
    nt = s // tm
    half = FF // 2

    def body(dh2_ref, gf_ref, uf_ref, gw_ref, dff_ref, gwo_ref, wffo, gacc, sems):
        i = pl.program_id(0)

        @pl.when(i == 0)
        def _():
            _load_weights(gw_ref, [("wffo", wffo, D)], sems)
            gacc[...] = jnp.zeros_like(gacc)

        dm = dh2_ref[...].astype(_MXU)
        for ch in range(2):
            cs = pl.ds(ch * half, half)
            dact = _nt(dm, wffo[cs, :])
            gfv = gf_ref[:, cs].astype(F32)
            ufv = uf_ref[:, cs].astype(F32)
            sg = _sigmoid(gfv)
            silu = gfv * sg
            gacc[cs, :] += _tn((silu * ufv).astype(_MXU), dm)
            dff_ref[:, pl.ds(ch * half, half)] = (dact * ufv * (sg * (1.0 + gfv * (1.0 - sg)))).astype(_MXU)
            dff_ref[:, pl.ds(FF + ch * half, half)] = (dact * silu).astype(_MXU)

        @pl.when(i == nt - 1)
        def _():
            pltpu.sync_copy(gacc, gwo_ref)

    tspec = lambda w: pl.BlockSpec((tm, w), lambda i: (i, 0))
    return pl.pallas_call(
        body, name="ffn_bwd_hidden", grid=(nt,),
        out_shape=(jax.ShapeDtypeStruct((s, 2 * FF), _MXU), jax.ShapeDtypeStruct((FF, D), F32)),
        in_specs=[tspec(D), tspec(FF), tspec(FF), pl.BlockSpec(memory_space=pl.ANY)],
        out_specs=(tspec(2 * FF), pl.BlockSpec(memory_space=pl.ANY)),
        scratch_shapes=[pltpu.VMEM((FF, D), _MXU), pltpu.VMEM((FF, D), F32), pltpu.SemaphoreType.DMA((NDEV,))],
        compiler_params=_cparams(1),
    )(dh2, gf, uf, gw)


def _proj_norm_bwd(dy, x, dres, g, gw, slab, width, name):
    s = x.shape[0]
    tm = min(TM, s)

    def body(dy_ref, x_ref, dr_ref, g_ref, gw_ref, dx_ref, dg_ref, wT, sems):
        @pl.when(pl.program_id(0) == 0)
        def _():
            _load_weights(gw_ref, [(slab, wT, D)], sems)
            dg_ref[...] = jnp.zeros_like(dg_ref)

        dv = _nn(dy_ref[...], wT[...])
        xv = x_ref[...]
        inv = lax.rsqrt(jnp.mean(xv * xv, axis=-1, keepdims=True) + EPS)
        dx, dg = _rms_bwd(dv, xv * inv, inv, g_ref[...])
        dg_ref[...] += dg
        dx_ref[...] = dr_ref[...] + dx

    tspec = lambda w: pl.BlockSpec((tm, w), lambda i: (i, 0))
    vec = pl.BlockSpec((1, D), lambda i: (0, 0))
    return pl.pallas_call(
        body, name=name, grid=(s // tm,),
        out_shape=(jax.ShapeDtypeStruct((s, D), F32), jax.ShapeDtypeStruct((1, D), F32)),
        in_specs=[tspec(width), tspec(D), tspec(D), vec, pl.BlockSpec(memory_space=pl.ANY)],
        out_specs=(tspec(D), vec),
        scratch_shapes=[pltpu.VMEM((width, D), _MXU), pltpu.SemaphoreType.DMA((NDEV,))],
        compiler_params=_cparams(1),
    )(dy, x, dres, g, gw)


def _mixer_bwd(dh1, z, h, y_pool, y_lru, saved, gw, small):
    s = dh1.shape[0]
    tm = min(TM_SEQ, s)
    nt = s // tm
    (pool_w, pool_scale, conv_w, conv_b, w_rg, b_rg, w_ig, b_ig, lam, b_gate) = small

    def body(dh1_ref, z_ref, zp_ref, h_ref, hp_ref, yp_ref, yr_ref, a_ref, r_ref, ig_ref, xc_ref, gw_ref,
             pw_ref, ps_ref, cw_ref, cb_ref, wrg_ref, brg_ref, wig_ref, big_ref, lam_ref, bg_ref,
             dz_ref, dyr_ref, dyp_ref, mx_ref,
             gbg_ref, glam_ref, gbrg_ref, gbig_ref, gcb_ref, gcw_ref, gps_ref, gpw_ref, gwrg_ref, gwig_ref,
             pprojT, lru_w, wout_w, pbuf, lbuf, hbuf, qbuf, xbuf, a_s, g_s, dh_s, hk, pk, dcar, sems):
        step = pl.program_id(0)
        i = nt - 1 - step
        t0 = i * tm

        @pl.when(step == 0)
        def _():
            _load_weights(gw_ref, [("pproj", pprojT, PW), ("lru", lru_w, D), ("wout", wout_w, D)], sems)
            for ref in (gbg_ref, glam_ref, gbrg_ref, gbig_ref, gcb_ref, gcw_ref, gps_ref, gpw_ref, gwrg_ref, gwig_ref):
                ref[...] = jnp.zeros_like(ref)
            qbuf[pl.ds(tm, HALO), :] = jnp.zeros((HALO, PW), F32)
            xbuf[pl.ds(tm, 8), :] = jnp.zeros((8, D), F32)
            dcar[...] = jnp.zeros_like(dcar)

        first = i == 0
        zprev = jnp.where(first, 0.0, zp_ref[...])
        hprev = jnp.where(first, 0.0, hp_ref[...])

        d_merged = _nt(dh1_ref[...].astype(_MXU), wout_w[...])

        g0 = _sigmoid(z_ref[:, pl.ds(PW + 2 * D, D)] + bg_ref[pl.ds(0, 1), :])
        g1 = _sigmoid(z_ref[:, pl.ds(PW + 3 * D, D)] + bg_ref[pl.ds(1, 1), :])
        dz0 = d_merged * yp_ref[...].astype(F32) * g0 * (1.0 - g0)
        dz1 = d_merged * yr_ref[...].astype(F32) * g1 * (1.0 - g1)
        dz_ref[:, pl.ds(PW + 2 * D, D)] = dz0.astype(_MXU)
        dz_ref[:, pl.ds(PW + 3 * D, D)] = dz1.astype(_MXU)
        gbg_ref[pl.ds(0, 1), :] += jnp.sum(dz0, axis=0, keepdims=True)
        gbg_ref[pl.ds(1, 1), :] += jnp.sum(dz1, axis=0, keepdims=True)
        d_ypool = (d_merged * g0).astype(_MXU)
        d_ylru = (d_merged * g1).astype(_MXU)
        dyp_ref[...] = d_ypool
        dyr_ref[...] = d_ylru

        d_yl = _nt(d_ylru, lru_w[...])
        gel, dgel = _gelu_and_grad(z_ref[:, pl.ds(PW + D, D)])
        dz_ref[:, pl.ds(PW + D, D)] = (d_yl * h_ref[...] * dgel).astype(_MXU)
        g_full = d_yl * gel
        lbuf[pl.ds(0, HALO), :] = zprev[:, PW:PW + D]
        lbuf[pl.ds(HALO, tm), :] = z_ref[:, pl.ds(PW, D)]
        hbuf[pl.ds(0, 8), :] = hprev
        hbuf[pl.ds(8, tm), :] = h_ref[...]
        sp, sneg = _softplus_neg(lam_ref[...])
        start = (t0 + lax.broadcasted_iota(jnp.int32, (tm, HD), 0)) == 0
        for hh in range(HEADS):
            cs = pl.ds(hh * HD, HD)
            _to_segments(a_s, hh, a_ref[:, cs], tm)
            _to_segments(g_s, hh, g_full[:, hh * HD:(hh + 1) * HD], tm)
        _segment_scan(a_s, g_s, dh_s, hk, pk, dcar, tm, reverse=True)
        for hh in range(HEADS):
            cs = pl.ds(hh * HD, HD)
            a = a_ref[:, cs]
            r = r_ref[:, cs].astype(F32)
            ig = ig_ref[:, cs].astype(F32)
            xc = xc_ref[:, cs].astype(F32)
            a2 = a * a
            one_m = 1.0 - a2
            live = jnp.logical_and(one_m > 0.0, jnp.logical_not(start))
            inv_mult = lax.rsqrt(jnp.where(live, one_m, 1.0))
            mult = jnp.where(live, one_m * inv_mult, jnp.where(start, 1.0, 0.0))
            dh = _from_segments(dh_s, hh, tm)
            d_mult = dh * ig * xc
            d_loga = dh * hbuf[pl.ds(7, tm), cs] * a - jnp.where(live, d_mult * a2 * inv_mult, 0.0)
            glam_ref[:, cs] += jnp.sum(d_loga * (LRU_C * r) * sneg[:, hh * HD:(hh + 1) * HD], axis=0, keepdims=True)
            d_rpre = d_loga * (-LRU_C * sp[:, hh * HD:(hh + 1) * HD]) * r * (1.0 - r)
            d_igpre = dh * mult * xc * ig * (1.0 - ig)
            gbrg_ref[pl.ds(hh, 1), :] += jnp.sum(d_rpre, axis=0, keepdims=True)
            gbig_ref[pl.ds(hh, 1), :] += jnp.sum(d_igpre, axis=0, keepdims=True)
            drm = d_rpre.astype(_MXU)
            dim = d_igpre.astype(_MXU)
            xcm = xc.astype(_MXU)
            gwrg_ref[hh] += _tn(xcm, drm)
            gwig_ref[hh] += _tn(xcm, dim)
            d_xc = dh * mult * ig + _nt(drm, wrg_ref[hh]) + _nt(dim, wig_ref[hh])
            gcb_ref[:, cs] += jnp.sum(d_xc, axis=0, keepdims=True)
            for k in range(CONV):
                gcw_ref[pl.ds(k, 1), cs] += jnp.sum(d_xc * lbuf[pl.ds(HALO - (CONV - 1) + k, tm), cs], axis=0,
                                                    keepdims=True)
            xbuf[pl.ds(0, tm), cs] = d_xc
        dzl = cw_ref[pl.ds(CONV - 1, 1), :] * xbuf[pl.ds(0, tm), :]
        for k in range(CONV - 1):
            dzl = dzl + cw_ref[pl.ds(k, 1), :] * xbuf[pl.ds(CONV - 1 - k, tm), :]
        dz_ref[:, pl.ds(PW, D)] = dzl.astype(_MXU)
        xbuf[pl.ds(tm, 8), :] = xbuf[pl.ds(0, 8), :]

        d_mixed = _nn(d_ypool, pprojT[...])
        pbuf[pl.ds(0, HALO), :] = zprev[:, 0:PW]
        pbuf[pl.ds(HALO, tm), :] = z_ref[:, pl.ds(0, PW)]
        pooled, mixed_pre = _pool_tile(pbuf, t0, tm, pw_ref, ps_ref)
        mp = jnp.concatenate(mixed_pre, axis=1)
        mx_ref[...] = (mp * ps_ref[...]).astype(_MXU)
        gps_ref[...] += jnp.sum(d_mixed * mp, axis=0, keepdims=True)
        d_mp = (d_mixed * ps_ref[...]).astype(_MXU)
        t = t0 + lax.broadcasted_iota(jnp.int32, (tm, GD), 0)
        d_pooled = []
        for g, w in enumerate(WINDOWS):
            dmg = d_mp[:, g * GD:(g + 1) * GD]
            gpw_ref[g] += _tn(pooled[g].astype(_MXU), dmg)
            dp = _nt(dmg, pw_ref[g])
            d_pooled.append(dp)
            qbuf[pl.ds(0, tm), pl.ds(g * GD, GD)] = dp / jnp.minimum(t + 1, w).astype(F32)
        for g, w in enumerate(WINDOWS):
            cs = pl.ds(g * GD, GD)
            acc = qbuf[pl.ds(0, tm), cs]
            for d in range(1, w):
                acc = acc + qbuf[pl.ds(d, tm), cs]
            dz_ref[:, cs] = (acc - d_pooled[g]).astype(_MXU)
        qbuf[pl.ds(tm, HALO), :] = qbuf[pl.ds(0, HALO), :]

    rev = lambda w: pl.BlockSpec((tm, w), lambda g: (nt - 1 - g, 0))
    prev = lambda rows, w: pl.BlockSpec((rows, w), lambda g: (jnp.maximum((nt - 1 - g) * (tm // rows) - 1, 0), 0))
    full = lambda a: pl.BlockSpec(a.shape, lambda g: (0,) * a.ndim)
    tok = lambda w, dt: jax.ShapeDtypeStruct((s, w), dt)
    acc_shapes = [(2, D), (1, D), (HEADS, HD), (HEADS, HD), (1, D), (CONV, D), (1, PW), (GROUPS, GD, GD),
                  (HEADS, HD, HD), (HEADS, HD, HD)]
    acc_specs = tuple(pl.BlockSpec(sh, lambda g, n=len(sh): (0,) * n) for sh in acc_shapes)
    seg_buf = pltpu.VMEM((HEADS, 8 * _seg_layout(tm)[1], HD), F32)
    a_in, r_in, ig_in, xc_in = saved
    return pl.pallas_call(
        body, name="mixer_bwd", grid=(nt,),
        out_shape=(tok(NIN, _MXU), tok(D, _MXU), tok(D, _MXU), tok(PW, _MXU))
        + tuple(jax.ShapeDtypeStruct(sh, F32) for sh in acc_shapes),
        in_specs=[rev(D), rev(NIN), prev(HALO, NIN), rev(D), prev(8, D), rev(D), rev(D), rev(D), rev(D), rev(D), rev(D),
                  pl.BlockSpec(memory_space=pl.ANY)] + [full(a) for a in small],
        out_specs=(rev(NIN), rev(D), rev(D), rev(PW)) + acc_specs,
        scratch_shapes=[pltpu.VMEM((D, PW), _MXU), pltpu.VMEM((D, D), _MXU), pltpu.VMEM((D, D), _MXU),
                        pltpu.VMEM((tm + HALO, PW), F32), pltpu.VMEM((tm + HALO, D), F32),
                        pltpu.VMEM((tm + 8, D), F32), pltpu.VMEM((tm + HALO, PW), F32), pltpu.VMEM((tm + 8, D), F32),
                        seg_buf, seg_buf, seg_buf, pltpu.VMEM((HEADS, tm, HD), F32), pltpu.VMEM((HEADS, tm, HD), F32),
                        pltpu.VMEM((8, D), F32), pltpu.SemaphoreType.DMA((3 * NDEV,))],
        compiler_params=_cparams(1),
    )(dh1, z, z, h, h, y_pool, y_lru, a_in, r_in, ig_in, xc_in, gw, *small)


def _split3(a):
    hi = a.astype(jnp.bfloat16).astype(F32)
    mid = (a - hi).astype(jnp.bfloat16).astype(F32)
    lo = (a - hi - mid).astype(jnp.bfloat16).astype(F32)
    return jnp.stack([hi, mid, lo])


def _small_pack(parts):
    flat = jnp.concatenate([a.reshape(-1) for a in parts])
    return jnp.pad(flat, (0, NDEV * SMALL_ROWS * D - flat.shape[0])).reshape(NDEV * SMALL_ROWS, D)


def _small_unpack(packed, shapes):
    flat = packed.reshape(-1)
    out, o = [], 0
    for sh in shapes:
        n = math.prod(sh)
        out.append(flat[o:o + n].reshape(sh))
        o += n
    return out


def kernel(x, p, norm1_g, w_in, b_gate, pool_w, pool_scale, pool_proj, conv_w, conv_b, w_rg, b_rg, w_ig, b_ig, lru_lambda, lru_proj, w_out, norm2_g, w_ffn_in, w_ffn_out, ple_norm_g, w_ple_gate, w_ple_proj, final_g, loss_target, m_norm1_g, m_w_in, m_b_gate, m_pool_w, m_pool_scale, m_pool_proj, m_conv_w, m_conv_b, m_w_rg, m_b_rg, m_w_ig, m_b_ig, m_lru_lambda, m_lru_proj, m_w_out, m_norm2_g, m_w_ffn_in, m_w_ffn_out, m_ple_norm_g, m_w_ple_gate, m_w_ple_proj, m_final_g, v_norm1_g, v_w_in, v_b_gate, v_pool_w, v_pool_scale, v_pool_proj, v_conv_w, v_conv_b, v_w_rg, v_b_rg, v_w_ig, v_b_ig, v_lru_lambda, v_lru_proj, v_w_out, v_norm2_g, v_w_ffn_in, v_w_ffn_out, v_ple_norm_g, v_w_ple_gate, v_w_ple_proj, v_final_g):
    axes = ("x", "y", "c")
    me = 4 * lax.axis_index("x") + 2 * lax.axis_index("y") + lax.axis_index("c")
    x2 = x[0]
    p2 = p[0, 0]
    tgt = loss_target[0]

    n_small = (CONV + 2) * 128
    small_terms = _split3(jnp.concatenate([conv_w[0].reshape(-1), b_gate[0].reshape(-1)]))
    small_rows = jnp.pad(small_terms, ((0, 16 - 3), (0, D - n_small)))
    own_first = jnp.concatenate([w_in[0].T.astype(_MXU), small_rows.astype(_MXU)], axis=0)
    own_second = jnp.concatenate([
        w_ffn_in[0].T.astype(_MXU),
        jnp.pad(pool_proj[0].T, ((0, 0), (0, D - PW))).astype(_MXU),
        jnp.pad(w_ple_proj[0].T, ((0, 0), (0, D - PLE))).astype(_MXU),
        lru_proj[0].astype(_MXU), w_out[0].astype(_MXU), w_ffn_out[0].astype(_MXU), w_ple_gate[0].astype(_MXU),
    ], axis=0)
    gw_first = _all_gather_weights(own_first)
    off = W_OFF["f32s"][0]
    st = gw_first[:, off:off + 3, :n_small].astype(F32)
    sf = st[:, 0] + st[:, 1] + st[:, 2]
    conv_w_full = sf[:, :CONV * 128].reshape(NDEV, CONV, 128).transpose(1, 0, 2).reshape(CONV, D)
    b_gate_full = sf[:, CONV * 128:].reshape(NDEV, 2, 128).transpose(1, 0, 2).reshape(2, D)

    small = (pool_w[0].astype(_MXU), pool_scale, conv_w_full, conv_b, w_rg[0].astype(_MXU), b_rg[0],
             w_ig[0].astype(_MXU), b_ig[0], lru_lambda, b_gate_full)

    u, z, gw = _inproj_fwd(x2, norm1_g, gw_first, own_second)
    h, yl, merged, y_pool, y_lru, h1, *saved = _mixer_fwd(z, x2, gw, small)
    v, gf, uf, h2 = _ffn_fwd(h1, norm2_g, gw)

    dh2, loss_blk, g_ple_norm, g_final, part_wpg, part_ple = _ple_loss_fwd_bwd(h2, p2, tgt, ple_norm_g,
                                                                               final_g.reshape(1, D), gw)
    dff, part_wffo = _ffn_bwd_hidden(dh2, gf, uf, gw)
    dh1, g_norm2 = _proj_norm_bwd(dff, h1, dh2, norm2_g, gw, "wffn", 2 * FF, "ffn_bwd_in")
    (dz, d_ylru, d_ypool, mixed, g_bgate, g_lam, g_brg, g_big, g_convb, g_convw, g_pscale, g_poolw, g_wrg,
     g_wig) = _mixer_bwd(dh1, z, h, y_pool, y_lru, saved, gw, small)
    grad_x, g_norm1 = _proj_norm_bwd(dz, x2, dh1, norm1_g, gw_first, "win", NIN, "inproj_bwd")

    small_shapes = [(1, D), (GROUPS, GD, GD), (1, PW), (1, D), (HEADS, HD, HD), (HEADS, HD), (HEADS, HD, HD),
                    (HEADS, HD), (1, D), (1, D), (1, D), (1, D), (2, D), (CONV, D), (1, 1)]
    small_part = _small_pack([g_norm1, g_poolw, g_pscale, g_convb, g_wrg, g_brg, g_wig, g_big, g_lam, g_norm2,
                              g_ple_norm, g_final, g_bgate, g_convw, loss_blk[0:1, 0:1]])
    riders = [part_wpg, _grad_matmul(yl, d_ylru, "grad_lru_proj"), _grad_matmul(merged, dh1, "grad_w_out"),
              _grad_matmul(d_ypool, mixed, "grad_pool_proj"), part_ple]
    rs_wffn = _grad_matmul_rs(dff, v, "grad_w_ffn_in", 704, extras=[part_wffo], narrow=_MXU)
    rs_win = _grad_matmul_rs(dz, u, "grad_w_in", 576, extras=riders + [small_part], narrow=_MXU, tail=SMALL_ROWS)

    def reduced(parts, name):
        return [_sum_arrays([t_own, landed[0], landed[1], landed[2]], "rs_sum_" + name + str(n))
                for n, (t_own, landed) in enumerate(parts)]

    red_wffn, = reduced(rs_wffn, "wffn")
    red_win, red_small = reduced(rs_win, "win")
    g_w_in = red_win[:576].T
    g_w_ffn_in = red_wffn[:704].T
    g_w_ffn_out = red_wffn[704:]
    g_w_ple_gate, g_lru_proj, g_w_out = red_win[576:704], red_win[704:832], red_win[832:960]
    g_pool_proj = red_win[960:1088, :PW].T
    g_w_ple_proj = red_win[1088:1216, :PLE].T
    small_red = _all_gather_small(red_small)
    (gs_norm1, gs_poolw, gs_pscale, gs_convb, gs_wrg, gs_brg, gs_wig, gs_big, gs_lam, gs_norm2, gs_ple_norm,
     gs_final, gs_bgate, gs_convw, loss_sum) = _small_unpack(small_red, small_shapes)
    loss = loss_sum[0, 0]
    g_b_gate = lax.dynamic_slice_in_dim(gs_bgate, me * 128, 128, axis=1)
    g_conv_w = lax.dynamic_slice_in_dim(gs_convw, me * 128, 128, axis=1)

    grads = {
        "norm1_g": gs_norm1, "w_in": g_w_in[None], "b_gate": g_b_gate[None], "pool_w": gs_poolw[None],
        "pool_scale": gs_pscale, "pool_proj": g_pool_proj[None], "conv_w": g_conv_w[None], "conv_b": gs_convb,
        "w_rg": gs_wrg[None], "b_rg": gs_brg[None], "w_ig": gs_wig[None], "b_ig": gs_big[None], "lru_lambda": gs_lam,
        "lru_proj": g_lru_proj[None], "w_out": g_w_out[None], "norm2_g": gs_norm2, "w_ffn_in": g_w_ffn_in[None],
        "w_ffn_out": g_w_ffn_out[None], "ple_norm_g": gs_ple_norm, "w_ple_gate": g_w_ple_gate[None],
        "w_ple_proj": g_w_ple_proj[None], "final_g": gs_final.reshape(D),
    }
    weights = dict(norm1_g=norm1_g, w_in=w_in, b_gate=b_gate, pool_w=pool_w, pool_scale=pool_scale, pool_proj=pool_proj,
                   conv_w=conv_w, conv_b=conv_b, w_rg=w_rg, b_rg=b_rg, w_ig=w_ig, b_ig=b_ig, lru_lambda=lru_lambda,
                   lru_proj=lru_proj, w_out=w_out, norm2_g=norm2_g, w_ffn_in=w_ffn_in, w_ffn_out=w_ffn_out,
                   ple_norm_g=ple_norm_g, w_ple_gate=w_ple_gate, w_ple_proj=w_ple_proj, final_g=final_g)
    moments_m = dict(norm1_g=m_norm1_g, w_in=m_w_in, b_gate=m_b_gate, pool_w=m_pool_w, pool_scale=m_pool_scale,
                     pool_proj=m_pool_proj, conv_w=m_conv_w, conv_b=m_conv_b, w_rg=m_w_rg, b_rg=m_b_rg, w_ig=m_w_ig,
                     b_ig=m_b_ig, lru_lambda=m_lru_lambda, lru_proj=m_lru_proj, w_out=m_w_out, norm2_g=m_norm2_g,
                     w_ffn_in=m_w_ffn_in, w_ffn_out=m_w_ffn_out, ple_norm_g=m_ple_norm_g, w_ple_gate=m_w_ple_gate,
                     w_ple_proj=m_w_ple_proj, final_g=m_final_g)
    moments_v = dict(norm1_g=v_norm1_g, w_in=v_w_in, b_gate=v_b_gate, pool_w=v_pool_w, pool_scale=v_pool_scale,
                     pool_proj=v_pool_proj, conv_w=v_conv_w, conv_b=v_conv_b, w_rg=v_w_rg, b_rg=v_b_rg, w_ig=v_w_ig,
                     b_ig=v_b_ig, lru_lambda=v_lru_lambda, lru_proj=v_lru_proj, w_out=v_w_out, norm2_g=v_norm2_g,
                     w_ffn_in=v_w_ffn_in, w_ffn_out=v_w_ffn_out, ple_norm_g=v_ple_norm_g, w_ple_gate=v_w_ple_gate,
                     w_ple_proj=v_w_ple_proj, final_g=v_final_g)
    names = list(weights)
    big = ("w_in", "w_ffn_in", "w_ffn_out", "lru_proj", "w_out", "w_ple_gate", "pool_proj", "w_ple_proj")
    slab_space = {"w_in": red_win[:576], "w_ffn_in": red_wffn[:704]}
    delta, new_m, new_v = {}, {}, {}
    for n in big:
        sh = weights[n].shape
        if n in slab_space:
            as2d = lambda a: a[0].T
            back = lambda a: a.T[None]
            g2d = slab_space[n]
        else:
            as2d = lambda a: a.reshape(sh[-2], sh[-1])
            back = lambda a: a.reshape(sh)
            g2d = as2d(grads[n])
        d_, m_, v_ = _adamw(as2d(weights[n]), g2d, as2d(moments_m[n]), as2d(moments_v[n]), "adamw_" + n)
        delta[n], new_m[n], new_v[n] = back(d_), back(m_), back(v_)
    rest = [n for n in names if n not in big]
    rest_shapes = [weights[n].shape for n in rest]
    packed = [_small_pack([src[n] for n in rest]) for src in (weights, grads, moments_m, moments_v)]
    d_, m_, v_ = _adamw(*packed, "adamw_small")
    for n, a, b_, c_ in zip(rest, _small_unpack(d_, rest_shapes), _small_unpack(m_, rest_shapes),
                            _small_unpack(v_, rest_shapes)):
        delta[n], new_m[n], new_v[n] = a, b_, c_

    return (loss, grad_x[None], *[grads[n] for n in names], *[delta[n] for n in names],
            *[new_m[n] for n in names], *[new_v[n] for n in names])
```

```python
import functools
import math

import jax
import jax.numpy as jnp
from jax import lax
from jax.experimental import pallas as pl
from jax.experimental.pallas import tpu as pltpu

F32 = jnp.float32
D = 1024
NIN = 4608
PW = 512
FF = 2816
PLE = 256
HEADS, HD = 8, 128
GROUPS, GD = 4, 128
WINDOWS = (2, 4, 8, 16)
HALO = 16
CONV = 4
EPS = 1e-6
LRU_C = 8.0
NDEV = 8
MESH = pl.DeviceIdType.MESH

ADAM_LR, ADAM_B1, ADAM_B2, ADAM_EPS, ADAM_WD, ADAM_STEP = 0.001, 0.9, 0.999, 1e-08, 0.01, 10

_MXU = jnp.bfloat16
TM = 512
TM_SEQ = 256
VMEM_LIMIT = 56 * 1024 * 1024
W_FIRST = (("win", 576), ("f32s", 16))
W_SECOND = (("wffn", 704), ("pproj", 128), ("ple", 128), ("lru", 128), ("wout", 128), ("wffo", 352), ("wpg", 128))
W_OFF = {}
for _slabs in (W_FIRST, W_SECOND):
    _o = 0
    for _n, _r in _slabs:
        W_OFF[_n] = (_o, _r)
        _o += _r
SMALL_ROWS = 48


def _cparams(n_axes=1, vmem=VMEM_LIMIT):
    return pltpu.CompilerParams(dimension_semantics=("arbitrary",) * n_axes, vmem_limit_bytes=vmem)


def _my_pos():
    return lax.axis_index("x"), lax.axis_index("y"), lax.axis_index("c")


def _nt(a, b):
    return lax.dot_general(a, b, (((1,), (1,)), ((), ())), preferred_element_type=F32)


def _nn(a, b):
    return lax.dot_general(a, b, (((1,), (0,)), ((), ())), preferred_element_type=F32)


def _tn(a, b):
    return lax.dot_general(a, b, (((0,), (0,)), ((), ())), preferred_element_type=F32)


def _sigmoid(x):
    return 0.5 * jnp.tanh(0.5 * x) + 0.5


_GELU_K = math.sqrt(2.0 / math.pi)


def _gelu_and_grad(x):
    x2 = x * x
    inner = _GELU_K * (x + 0.044715 * x2 * x)
    t = jnp.tanh(inner)
    g = 0.5 * x * (1.0 + t)
    dg = 0.5 * (1.0 + t) + 0.5 * x * (1.0 - t * t) * _GELU_K * (1.0 + 3.0 * 0.044715 * x2)
    return g, dg


def _softplus_neg(lam):
    x = -lam
    t = jnp.exp(-jnp.abs(x))
    u = 1.0 + t
    l1p = jnp.where(u == 1.0, t, jnp.log(u) * t / (u - 1.0))
    return jnp.maximum(x, 0.0) + l1p, _sigmoid(x)


def _start_slab_loads(g_ref, name, dst_ref, sems, base, width=D):
    off, rows = W_OFF[name]
    copies = []
    for k in range(NDEV):
        if width == D:
            src = g_ref.at[k, pl.ds(off, rows), :]
        else:
            src = g_ref.at[k, pl.ds(off, rows), pl.ds(0, width)]
        cp = pltpu.make_async_copy(src, dst_ref.at[pl.ds(k * rows, rows), :], sems.at[base + k])
        cp.start()
        copies.append(cp)
    return copies


def _load_weights(g_ref, items, sems):
    copies = []
    for n, (name, dst, width) in enumerate(items):
        copies += _start_slab_loads(g_ref, name, dst, sems, n * NDEV, width)
    for cp in copies:
        cp.wait()


def _gather_phases(own_ref, out_ref, stage, send_sems, recv_sems, local_sem):
    x, y, c = _my_pos()
    me, sibling = (x, y, c), (x, y, 1 - c)
    chips = [(1 - x, y), (x, 1 - y), (1 - x, 1 - y)]

    def slab(px, py, pc):
        return out_ref.at[4 * px + 2 * py + pc]

    def copy(k, block, to, src=None):
        return pltpu.make_async_remote_copy(
            src_ref=slab(*block) if src is None else src, dst_ref=slab(*block),
            send_sem=send_sems.at[k], recv_sem=recv_sems.at[k], device_id=to, device_id_type=MESH)

    mine = pltpu.make_async_copy(stage, slab(*me), local_sem)
    first = [copy(0, me, sibling, src=stage)] + [copy(1 + j, me, (*chip, c), src=stage) for j, chip in enumerate(chips)]
    passed = [copy(4 + j, (*chip, c), sibling) for j, chip in enumerate(chips)]

    def send_mine():
        pltpu.sync_copy(own_ref, stage)
        mine.start()
        for cp in first:
            cp.start()

    def pass_on(js):
        for j in js:
            copy(1 + j, (*chips[j], c), me).wait_recv()
            passed[j].start()

    def finish():
        copy(0, sibling, me).wait_recv()
        for j, chip in enumerate(chips):
            copy(4 + j, (*chip, 1 - c), me).wait_recv()
        for cp in first + passed:
            cp.wait_send()
        mine.wait()

    return send_mine, pass_on, finish


def _all_gather_weights(own):
    rows, cols = own.shape

    def body(own_ref, out_ref, stage, send_sems, recv_sems, local_sem):
        send_mine, pass_on, finish = _gather_phases(own_ref, out_ref, stage, send_sems, recv_sems, local_sem)
        send_mine()
        pass_on((0, 1, 2))
        finish()

    return pl.pallas_call(
        body, name="ag_weights",
        out_shape=jax.ShapeDtypeStruct((NDEV, rows, cols), own.dtype),
        in_specs=[pl.BlockSpec(memory_space=pl.ANY)],
        out_specs=pl.BlockSpec(memory_space=pl.ANY),
        scratch_shapes=[pltpu.VMEM((rows, cols), own.dtype), pltpu.SemaphoreType.DMA((7,)),
                        pltpu.SemaphoreType.DMA((7,)), pltpu.SemaphoreType.DMA],
        compiler_params=pltpu.CompilerParams(vmem_limit_bytes=VMEM_LIMIT),
    )(own)


def _all_gather_small(piece):
    rows = piece.shape[0]

    def body(p_ref, out_ref, send_sems, recv_sems, local_sem):
        x, y, c = _my_pos()
        me = 4 * x + 2 * y + c
        mine = pltpu.make_async_copy(p_ref, out_ref.at[pl.ds(pl.multiple_of(me * rows, 8), rows), :], local_sem)
        mine.start()
        sends = []
        peers = []
        for r in range(1, NDEV):
            px = 1 - x if (r >> 2) & 1 else x
            py = 1 - y if (r >> 1) & 1 else y
            pc = 1 - c if r & 1 else c
            peers.append((px, py, pc))
            cp = pltpu.make_async_remote_copy(
                src_ref=p_ref, dst_ref=out_ref.at[pl.ds(pl.multiple_of(me * rows, 8), rows), :],
                send_sem=send_sems.at[r - 1], recv_sem=recv_sems.at[r - 1], device_id=(px, py, pc),
                device_id_type=MESH)
            cp.start()
            sends.append(cp)
        for r, (px, py, pc) in enumerate(peers):
            them = 4 * px + 2 * py + pc
            pltpu.make_async_remote_copy(
                src_ref=p_ref, dst_ref=out_ref.at[pl.ds(pl.multiple_of(them * rows, 8), rows), :],
                send_sem=send_sems.at[r], recv_sem=recv_sems.at[r], device_id=(px, py, pc),
                device_id_type=MESH).wait_recv()
        for cp in sends:
            cp.wait_send()
        mine.wait()

    return pl.pallas_call(
        body, name="ag_small",
        out_shape=jax.ShapeDtypeStruct((NDEV * rows, piece.shape[1]), piece.dtype),
        in_specs=[pl.BlockSpec(memory_space=pltpu.VMEM)],
        out_specs=pl.BlockSpec(memory_space=pl.ANY),
        scratch_shapes=[pltpu.SemaphoreType.DMA((7,)), pltpu.SemaphoreType.DMA((7,)), pltpu.SemaphoreType.DMA],
    )(piece)


def _row_block(rows, target=512, mult=8):
    b = min(rows, target) // mult * mult
    while rows % b:
        b -= mult
    return b


def _sum_arrays(arrs, name, narrow=None, target=464):
    rows, cols = arrs[0].shape
    br = _row_block(rows, target, 16)
    n = len(arrs)

    def body(*refs):
        acc = refs[0][...].astype(F32)
        for r in refs[1:n]:
            acc = acc + r[...].astype(F32)
        refs[n][...] = acc
        if narrow is not None:
            refs[n + 1][...] = acc.astype(narrow)

    spec = pl.BlockSpec((br, cols), lambda i: (i, 0))
    shape = jax.ShapeDtypeStruct((rows, cols), F32)
    if narrow is None:
        out_shape, out_specs = shape, spec
    else:
        out_shape, out_specs = (shape, jax.ShapeDtypeStruct((rows, cols), narrow)), (spec, spec)
    return pl.pallas_call(
        body, name=name, grid=(rows // br,), out_shape=out_shape,
        in_specs=[spec] * n, out_specs=out_specs, compiler_params=_cparams(1),
    )(*arrs)


def _adamw(w, g, m, v, name):
    rows, cols = w.shape
    br = _row_block(rows, 256)

    def body(w_ref, g_ref, m_ref, v_ref, d_ref, nm_ref, nv_ref):
        g_ = g_ref[...]
        m_ = ADAM_B1 * m_ref[...] + (1.0 - ADAM_B1) * g_
        v_ = ADAM_B2 * v_ref[...] + (1.0 - ADAM_B2) * (g_ * g_)
        m_hat = m_ / (1.0 - ADAM_B1 ** ADAM_STEP)
        v_hat = v_ / (1.0 - ADAM_B2 ** ADAM_STEP)
        d_ref[...] = -ADAM_LR * (m_hat / (jnp.sqrt(v_hat) + ADAM_EPS) + ADAM_WD * w_ref[...])
        nm_ref[...] = m_
        nv_ref[...] = v_

    spec = pl.BlockSpec((br, cols), lambda i: (i, 0))
    shape = jax.ShapeDtypeStruct((rows, cols), F32)
    return pl.pallas_call(
        body, name=name, grid=(rows // br,), out_shape=(shape, shape, shape),
        in_specs=[spec] * 4, out_specs=(spec, spec, spec), compiler_params=_cparams(1),
    )(w, g, m, v)


_CHIP_FLIPS = (2, 3, 1, 0)


def _grad_matmul(lhs, rhs, name):
    s, r = lhs.shape
    k = rhs.shape[1]
    tm = min(TM, s)

    def body(l_ref, r_ref, o_ref):
        @pl.when(pl.program_id(0) == 0)
        def _():
            o_ref[...] = jnp.zeros_like(o_ref)

        o_ref[:, pl.ds(0, k)] += _tn(l_ref[...].astype(_MXU), r_ref[...].astype(_MXU))

    return pl.pallas_call(
        body, name=name, grid=(s // tm,),
        out_shape=jax.ShapeDtypeStruct((r, D), F32),
        in_specs=[pl.BlockSpec((tm, r), lambda i: (i, 0)), pl.BlockSpec((tm, k), lambda i: (i, 0))],
        out_specs=pl.BlockSpec((r, D), lambda i: (0, 0)),
        compiler_params=_cparams(1),
    )(lhs, rhs)


def _grad_matmul_rs(lhs, rhs, name, rows, extras=(), narrow=None, tail=0):
    s, r8 = lhs.shape
    k = rhs.shape[1]
    tm = min(TM, s)
    nt = s // tm
    cpb = 1
    nblk = 4 // cpb
    nx = len(extras)
    ers = [e.shape[0] // NDEV for e in extras]
    er = sum(ers)
    srows = rows + er
    brows = 2 * cpb * srows
    groups = [(0, srows - tail, F32 if narrow is None else narrow)] + ([(srows - tail, tail, F32)] if tail else [])
    ng = len(groups)
    mid = min(nt - 1, max(1, nt // 4))

    def flip_of(p):
        return jnp.where(p == 0, 2, jnp.where(p == 1, 3, jnp.where(p == 2, 1, 0)))

    def block_col(b):
        x, y, _ = _my_pos()
        return (2 * x + y) ^ flip_of(b)

    def body(*refs):
        l_ref, r_ref = refs[:2]
        x_refs = refs[2:2 + nx]
        rest = refs[2 + nx:]
        town_ref = rest[0]
        lici_refs = rest[1:1 + ng]
        acc, stage = rest[1 + ng:3 + ng]
        send_bufs = rest[3 + ng:3 + 2 * ng]
        dsend, drecv, isend, irecv, xsem = rest[3 + 2 * ng:]
        b = pl.program_id(0)
        i = pl.program_id(1)
        x, y, c = _my_pos()
        mine = 2 * x + y
        sibling = (x, y, 1 - c)

        def chip_at(p):
            return mine ^ _CHIP_FLIPS[p]

        def slab_rows(p, parity):
            within = 0 if cpb == 1 else (chip_at(p) & 1) * 2
            return pl.ds(pl.multiple_of((within + parity) * srows, 8), srows)

        def push(p, slot):
            return pltpu.make_async_remote_copy(
                src_ref=acc.at[slot, slab_rows(p, 1 - c), :], dst_ref=stage.at[p % 2],
                send_sem=dsend.at[p], recv_sem=drecv.at[p], device_id=sibling, device_id_type=MESH)

        def ici(p):
            ch = chip_at(p)
            return [pltpu.make_async_remote_copy(
                src_ref=send_bufs[g].at[p % 2], dst_ref=lici_refs[g].at[p], send_sem=isend.at[3 * g + p],
                recv_sem=irecv.at[3 * g + p], device_id=(ch >> 1, ch & 1, c), device_id_type=MESH) for g in range(ng)]

        def extra_loads(p, slot):
            copies = []
            within = 0 if cpb == 1 else (chip_at(p) & 1) * 2
            for parity in range(2):
                off = rows
                for n, (x_ref, e) in enumerate(zip(x_refs, ers)):
                    src = x_ref.at[pl.ds(pl.multiple_of((2 * chip_at(p) + parity) * e, 8), e), :]
                    dst = acc.at[slot, pl.ds(pl.multiple_of((within + parity) * srows + off, 8), e), :]
                    copies.append(pltpu.make_async_copy(src, dst, xsem.at[(p * 2 + parity) * nx + n]))
                    off += e
            return copies

        def combine(p, slot):
            push(p, slot).wait_recv()
            total = acc[slot, slab_rows(p, c), :] + stage[p % 2]
            if p == 3:
                stage[p % 2] = total
                pltpu.sync_copy(stage.at[p % 2], town_ref)
            else:
                if p == 2:
                    for cp in ici(0):
                        cp.wait_send()
                for g, (r0, n, dt) in enumerate(groups):
                    send_bufs[g][p % 2] = total[r0:r0 + n, :].astype(dt)
                for cp in ici(p):
                    cp.start()

        for bb in range(nblk):
            slot = bb % 2
            positions = list(range(bb * cpb, (bb + 1) * cpb))

            @pl.when(jnp.logical_and(b == bb, i == 0))
            def _(bb=bb, slot=slot, positions=positions):
                if bb >= 2:
                    for p in range((bb - 2) * cpb, (bb - 1) * cpb):
                        push(p, slot).wait_send()
                for q in range(2 * cpb):
                    acc[slot, pl.ds(q * srows, rows), :] = jnp.zeros((rows, D), F32)
                for p in positions:
                    for cp in extra_loads(p, slot):
                        cp.start()

            if bb >= 1:
                @pl.when(jnp.logical_and(b == bb, i == mid))
                def _(bb=bb):
                    for p in range((bb - 1) * cpb, bb * cpb):
                        combine(p, (bb - 1) % 2)

        res = _tn(l_ref[...].astype(_MXU), r_ref[...].astype(_MXU))
        slot_now = b % 2
        for q in range(2 * cpb):
            acc[slot_now, pl.ds(q * srows, rows), pl.ds(0, k)] += res[q * rows:(q + 1) * rows, :]

        for bb in range(nblk):
            slot = bb % 2
            positions = list(range(bb * cpb, (bb + 1) * cpb))

            @pl.when(jnp.logical_and(b == bb, i == nt - 1))
            def _(bb=bb, slot=slot, positions=positions):
                for p in positions:
                    for cp in extra_loads(p, slot):
                        cp.wait()
                for p in positions:
                    push(p, slot).start()
                if bb == nblk - 1:
                    for p in positions:
                        combine(p, slot)
                    for p in range(max(0, (nblk - 2) * cpb), 4):
                        push(p, slot).wait_send()
                    for p in range(1, 3):
                        for cp in ici(p):
                            cp.wait_send()
                    for p in range(3):
                        for cp in ici(p):
                            cp.wait_recv()

    in_specs = [pl.BlockSpec((tm, 2 * cpb * rows), lambda b, i: (i, block_col(b))),
                pl.BlockSpec((tm, k), lambda b, i: (i, 0))]
    any_spec = pl.BlockSpec(memory_space=pl.ANY)
    in_specs += [any_spec] * nx
    args = [lhs, rhs, *extras]
    outs = pl.pallas_call(
        body, name=name, grid=(nblk, nt),
        out_shape=(jax.ShapeDtypeStruct((srows, D), F32),)
        + tuple(jax.ShapeDtypeStruct((3, n, D), dt) for _, n, dt in groups),
        in_specs=in_specs, out_specs=(any_spec,) * (1 + ng),
        scratch_shapes=[pltpu.VMEM((2, brows, D), F32), pltpu.VMEM((2, srows, D), F32)]
        + [pltpu.VMEM((2, n, D), dt) for _, n, dt in groups]
        + [pltpu.SemaphoreType.DMA((4,)), pltpu.SemaphoreType.DMA((4,)), pltpu.SemaphoreType.DMA((3 * ng,)),
           pltpu.SemaphoreType.DMA((3 * ng,)), pltpu.SemaphoreType.DMA((max(1, 8 * nx),))],
        compiler_params=_cparams(2),
    )(*args)
    t_own = outs[0]
    return [(t_own[r0:r0 + n], landed) for (r0, n, _), landed in zip(groups, outs[1:])]


def _inproj_fwd(x, g1, gw, own_second):
    s = x.shape[0]
    tm = min(TM, s)
    nt = s // tm
    nchunk = 4
    cw = NIN // nchunk
    rows2, cols2 = own_second.shape

    def body(x_ref, g1_ref, gw_ref, own_ref, u_ref, z_ref, gw2_ref, w_vmem, stage, sems, send_sems, recv_sems,
             local_sem):
        i = pl.program_id(0)
        send_mine, pass_on, finish = _gather_phases(own_ref, gw2_ref, stage, send_sems, recv_sems, local_sem)

        @pl.when(i == 0)
        def _():
            send_mine()
            _load_weights(gw_ref, [("win", w_vmem, D)], sems)

        @pl.when(i == nt // 2)
        def _():
            pass_on((0, 1))

        @pl.when(i == (7 * nt) // 8)
        def _():
            pass_on((2,))

        xv = x_ref[...]
        inv = lax.rsqrt(jnp.mean(xv * xv, axis=-1, keepdims=True) + EPS)
        u = (xv * inv * g1_ref[...]).astype(_MXU)
        u_ref[...] = u
        for ch in range(nchunk):
            z_ref[:, pl.ds(ch * cw, cw)] = _nt(u, w_vmem[pl.ds(ch * cw, cw), :])

        @pl.when(i == nt - 1)
        def _():
            finish()

    any_spec = pl.BlockSpec(memory_space=pl.ANY)
    return pl.pallas_call(
        body, name="inproj_fwd", grid=(nt,),
        out_shape=(jax.ShapeDtypeStruct((s, D), _MXU), jax.ShapeDtypeStruct((s, NIN), F32),
                   jax.ShapeDtypeStruct((NDEV, rows2, cols2), own_second.dtype)),
        in_specs=[pl.BlockSpec((tm, D), lambda i: (i, 0)), pl.BlockSpec((1, D), lambda i: (0, 0)), any_spec, any_spec],
        out_specs=(pl.BlockSpec((tm, D), lambda i: (i, 0)), pl.BlockSpec((tm, NIN), lambda i: (i, 0)), any_spec),
        scratch_shapes=[pltpu.VMEM((NIN, D), _MXU), pltpu.VMEM((rows2, cols2), own_second.dtype),
                        pltpu.SemaphoreType.DMA((NDEV,)), pltpu.SemaphoreType.DMA((7,)), pltpu.SemaphoreType.DMA((7,)),
                        pltpu.SemaphoreType.DMA],
        compiler_params=_cparams(1),
    )(x, g1, gw, own_second)


def _pool_tile(pbuf, t0, tm, pw_ref, scale_ref):
    t = t0 + lax.broadcasted_iota(jnp.int32, (tm, GD), 0)
    pooled, mixed_pre = [], []
    for g, w in enumerate(WINDOWS):
        cs = pl.ds(g * GD, GD)
        cur = pbuf[pl.ds(HALO, tm), cs]
        acc = cur
        for d in range(1, w):
            acc = acc + pbuf[pl.ds(HALO - d, tm), cs]
        cnt = jnp.minimum(t + 1, w).astype(F32)
        pg = acc / cnt - cur
        pooled.append(pg)
        mixed_pre.append(_nn(pg.astype(_MXU), pw_ref[g]))
    return pooled, mixed_pre


def _lru_gates_head(hh, lbuf, start, tm, cw_ref, cb_ref, wrg_ref, brg_ref, wig_ref, big_ref, sp):
    cs = pl.ds(hh * HD, HD)
    xc = cb_ref[:, cs] + cw_ref[pl.ds(CONV - 1, 1), cs] * lbuf[pl.ds(HALO, tm), cs]
    for k in range(CONV - 1):
        xc = xc + cw_ref[pl.ds(k, 1), cs] * lbuf[pl.ds(HALO - (CONV - 1) + k, tm), cs]
    xcm = xc.astype(_MXU)
    r = _sigmoid(_nn(xcm, wrg_ref[hh]) + brg_ref[pl.ds(hh, 1), :])
    ig = _sigmoid(_nn(xcm, wig_ref[hh]) + big_ref[pl.ds(hh, 1), :])
    a = jnp.exp(-LRU_C * r * sp[:, hh * HD:(hh + 1) * HD])
    one_m = 1.0 - a * a
    live = jnp.logical_and(one_m > 0.0, jnp.logical_not(start))
    inv_mult = lax.rsqrt(jnp.where(live, one_m, 1.0))
    mult = jnp.where(live, one_m * inv_mult, jnp.where(start, 1.0, 0.0))
    return xc, r, ig, a, live, inv_mult, mult


def _seg_layout(tm):
    seg = tm // 8
    return seg, seg + 8


def _to_segments(dst_ref, hh, val, tm):
    seg, pitch = _seg_layout(tm)
    for s in range(8):
        dst_ref[hh, pl.ds(s * pitch, seg), :] = val[s * seg:(s + 1) * seg, :]


def _from_segments(src_ref, hh, tm):
    seg, pitch = _seg_layout(tm)
    return jnp.concatenate([src_ref[hh, pl.ds(s * pitch, seg), :] for s in range(8)], axis=0)


def _segment_scan(a_ref, b_ref, out_ref, hk, pk, carry_ref, tm, reverse):
    seg, pitch = _seg_layout(tm)
    row = lax.broadcasted_iota(jnp.int32, (8, HD), 0)
    order = range(seg - 1, -1, -1) if reverse else range(seg)
    for hh in range(HEADS):
        cs = pl.ds(hh * HD, HD)
        if reverse:
            a0 = a_ref[hh, pl.ds(0, 8, stride=pitch), :]
            a_wrap = jnp.where(row <= 6, pltpu.roll(a0, 7, 0), 1.0)
        hv = jnp.zeros((8, HD), F32)
        pv = jnp.ones((8, HD), F32)
        for k in order:
            if not reverse:
                av = a_ref[hh, pl.ds(k, 8, stride=pitch), :]
            elif k + 1 < seg:
                av = a_ref[hh, pl.ds(k + 1, 8, stride=pitch), :]
            else:
                av = a_wrap
            hv = av * hv + b_ref[hh, pl.ds(k, 8, stride=pitch), :]
            pv = av * pv
            hk[hh, pl.ds(8 * k, 8), :] = hv
            pk[hh, pl.ds(8 * k, 8), :] = pv
        for d in (1, 2, 4):
            if reverse:
                keep, sh = row < 8 - d, 8 - d
            else:
                keep, sh = row >= d, d
            hv = hv + pv * jnp.where(keep, pltpu.roll(hv, sh, 0), 0.0)
            pv = pv * jnp.where(keep, pltpu.roll(pv, sh, 0), 1.0)
        cin = carry_ref[:, cs]
        ends = hv + pv * cin
        if reverse:
            enter = jnp.where(row <= 6, pltpu.roll(ends, 7, 0), cin)
            carry_ref[:, cs] = jnp.broadcast_to((a0 * ends)[0:1, :], (8, HD))
        else:
            enter = jnp.where(row >= 1, pltpu.roll(ends, 1, 0), cin)
            carry_ref[:, cs] = jnp.broadcast_to(ends[7:8, :], (8, HD))
        for k in range(seg):
            out_ref[hh, pl.ds(k, 8, stride=pitch), :] = hk[hh, pl.ds(8 * k, 8), :] + pk[hh, pl.ds(8 * k, 8), :] * enter


def _mixer_fwd(z, x, gw, small):
    s = x.shape[0]
    tm = min(TM_SEQ, s)
    (pool_w, pool_scale, conv_w, conv_b, w_rg, b_rg, w_ig, b_ig, lam, b_gate) = small

    def body(z_ref, x_ref, gw_ref, pw_ref, ps_ref, cw_ref, cb_ref, wrg_ref, brg_ref, wig_ref, big_ref, lam_ref,
             bg_ref, h_ref, yl_ref, mg_ref, yp_ref, yr_ref, h1_ref, a_ref, r_ref, ig_ref, xc_ref,
             pprojT, lru_w, wout_w, pbuf, lbuf, a_s, b_s, h_s, hk, pk, hcar, sems):
        i = pl.program_id(0)
        t0 = i * tm

        @pl.when(i == 0)
        def _():
            _load_weights(gw_ref, [("pproj", pprojT, PW), ("lru", lru_w, D), ("wout", wout_w, D)], sems)
            pbuf[pl.ds(0, HALO), :] = jnp.zeros((HALO, PW), F32)
            lbuf[pl.ds(0, HALO), :] = jnp.zeros((HALO, D), F32)
            hcar[...] = jnp.zeros_like(hcar)

        pbuf[pl.ds(HALO, tm), :] = z_ref[:, pl.ds(0, PW)]
        _, mixed_pre = _pool_tile(pbuf, t0, tm, pw_ref, ps_ref)
        mixed = jnp.concatenate(mixed_pre, axis=1) * ps_ref[...]
        y_pool = _nt(mixed.astype(_MXU), pprojT[...])
        pbuf[pl.ds(0, HALO), :] = pbuf[pl.ds(tm, HALO), :]

        lbuf[pl.ds(HALO, tm), :] = z_ref[:, pl.ds(PW, D)]
        sp, _ = _softplus_neg(lam_ref[...])
        start = (t0 + lax.broadcasted_iota(jnp.int32, (tm, HD), 0)) == 0
        for hh in range(HEADS):
            xc, r, ig, a, _, _, mult = _lru_gates_head(hh, lbuf, start, tm, cw_ref, cb_ref, wrg_ref, brg_ref,
                                                       wig_ref, big_ref, sp)
            _to_segments(a_s, hh, a, tm)
            _to_segments(b_s, hh, mult * ig * xc, tm)
            cs = pl.ds(hh * HD, HD)
            a_ref[:, cs] = a
            r_ref[:, cs] = r.astype(_MXU)
            ig_ref[:, cs] = ig.astype(_MXU)
            xc_ref[:, cs] = xc.astype(_MXU)
        lbuf[pl.ds(0, HALO), :] = lbuf[pl.ds(tm, HALO), :]
        _segment_scan(a_s, b_s, h_s, hk, pk, hcar, tm, reverse=False)
        for hh in range(HEADS):
            h_ref[:, pl.ds(hh * HD, HD)] = _from_segments(h_s, hh, tm)
        gel, _ = _gelu_and_grad(z_ref[:, pl.ds(PW + D, D)])
        yl = (h_ref[...] * gel).astype(_MXU)
        yl_ref[...] = yl
        y_lru = _nn(yl, lru_w[...])

        g0 = _sigmoid(z_ref[:, pl.ds(PW + 2 * D, D)] + bg_ref[pl.ds(0, 1), :])
        g1 = _sigmoid(z_ref[:, pl.ds(PW + 3 * D, D)] + bg_ref[pl.ds(1, 1), :])
        merged = (g0 * y_pool + g1 * y_lru).astype(_MXU)
        mg_ref[...] = merged
        yp_ref[...] = y_pool.astype(_MXU)
        yr_ref[...] = y_lru.astype(_MXU)
        h1_ref[...] = x_ref[...] + _nn(merged, wout_w[...])

    tok = lambda w, dt: jax.ShapeDtypeStruct((s, w), dt)
    tspec = lambda w: pl.BlockSpec((tm, w), lambda i: (i, 0))
    full = lambda a: pl.BlockSpec(a.shape, lambda i: (0,) * a.ndim)
    seg_buf = pltpu.VMEM((HEADS, 8 * _seg_layout(tm)[1], HD), F32)
    return pl.pallas_call(
        body, name="mixer_fwd", grid=(s // tm,),
        out_shape=(tok(D, F32), tok(D, _MXU), tok(D, _MXU), tok(D, _MXU), tok(D, _MXU), tok(D, F32),
                   tok(D, F32), tok(D, _MXU), tok(D, _MXU), tok(D, _MXU)),
        in_specs=[tspec(NIN), tspec(D), pl.BlockSpec(memory_space=pl.ANY)] + [full(a) for a in small],
        out_specs=(tspec(D),) * 10,
        scratch_shapes=[pltpu.VMEM((D, PW), _MXU), pltpu.VMEM((D, D), _MXU), pltpu.VMEM((D, D), _MXU),
                        pltpu.VMEM((tm + HALO, PW), F32), pltpu.VMEM((tm + HALO, D), F32),
                        seg_buf, seg_buf, seg_buf, pltpu.VMEM((HEADS, tm, HD), F32), pltpu.VMEM((HEADS, tm, HD), F32),
                        pltpu.VMEM((8, D), F32), pltpu.SemaphoreType.DMA((3 * NDEV,))],
        compiler_params=_cparams(1),
    )(z, x, gw, *small)


def _ffn_fwd(h1, g2, gw):
    s = h1.shape[0]
    tm = min(TM, s)
    half = FF // 2

    def body(h1_ref, g2_ref, gw_ref, v_ref, gf_ref, uf_ref, h2_ref, wffnT, wffo, sems):
        @pl.when(pl.program_id(0) == 0)
        def _():
            _load_weights(gw_ref, [("wffn", wffnT, D), ("wffo", wffo, D)], sems)

        hv = h1_ref[...]
        inv = lax.rsqrt(jnp.mean(hv * hv, axis=-1, keepdims=True) + EPS)
        v = (hv * inv * g2_ref[...]).astype(_MXU)
        v_ref[...] = v
        acc = hv
        for ch in range(2):
            cs = pl.ds(ch * half, half)
            gf = _nt(v, wffnT[pl.ds(ch * half, half), :]).astype(_MXU)
            uf = _nt(v, wffnT[pl.ds(FF + ch * half, half), :]).astype(_MXU)
            gf_ref[:, cs] = gf
            uf_ref[:, cs] = uf
            gf32 = gf.astype(F32)
            act = (gf32 * _sigmoid(gf32) * uf.astype(F32)).astype(_MXU)
            acc = acc + _nn(act, wffo[pl.ds(ch * half, half), :])
        h2_ref[...] = acc

    tspec = lambda w: pl.BlockSpec((tm, w), lambda i: (i, 0))
    return pl.pallas_call(
        body, name="ffn_fwd", grid=(s // tm,),
        out_shape=(jax.ShapeDtypeStruct((s, D), _MXU), jax.ShapeDtypeStruct((s, FF), _MXU),
                   jax.ShapeDtypeStruct((s, FF), _MXU), jax.ShapeDtypeStruct((s, D), F32)),
        in_specs=[tspec(D), pl.BlockSpec((1, D), lambda i: (0, 0)), pl.BlockSpec(memory_space=pl.ANY)],
        out_specs=(tspec(D), tspec(FF), tspec(FF), tspec(D)),
        scratch_shapes=[pltpu.VMEM((2 * FF, D), _MXU), pltpu.VMEM((FF, D), _MXU), pltpu.SemaphoreType.DMA((2 * NDEV,))],
        compiler_params=_cparams(1),
    )(h1, g2, gw)


def _rms_bwd(dy, xn, inv, g):
    dg = jnp.sum(dy * xn, axis=0, keepdims=True)
    dxn = dy * g
    dx = inv * (dxn - xn * jnp.mean(dxn * xn, axis=-1, keepdims=True))
    return dx, dg


def _ple_loss_fwd_bwd(h2, p, target, g3, gfin, gw):
    s = h2.shape[0]
    tm = min(TM, s)

    def body(h2_ref, p_ref, t_ref, g3_ref, gf_ref, gw_ref,
             dh2_ref, loss_ref, dg3_ref, dgf_ref, gwpg_ref, gple_ref, wpg, pleT, sems):
        i = pl.program_id(0)

        @pl.when(i == 0)
        def _():
            _load_weights(gw_ref, [("wpg", wpg, D), ("ple", pleT, PLE)], sems)
            for ref in (loss_ref, dg3_ref, dgf_ref, gwpg_ref, gple_ref):
                ref[...] = jnp.zeros_like(ref)

        hv = h2_ref[...]
        inv3 = lax.rsqrt(jnp.mean(hv * hv, axis=-1, keepdims=True) + EPS)
        xn3 = hv * inv3
        n3 = (xn3 * g3_ref[...]).astype(_MXU)
        pg = _sigmoid(_nn(n3, wpg[...]))
        pm = p_ref[...].astype(_MXU)
        e = _nt(pm, pleT[...])
        h3 = hv + pg * e
        invf = lax.rsqrt(jnp.mean(h3 * h3, axis=-1, keepdims=True) + EPS)
        xf = h3 * invf
        diff = xf * gf_ref[...] - t_ref[...]
        loss_ref[...] += jnp.sum(diff * diff) * (0.5 / D)
        dh3, dgf = _rms_bwd(diff * (1.0 / D), xf, invf, gf_ref[...])
        dgf_ref[...] += dgf
        gple_ref[:, pl.ds(0, PLE)] += _tn((dh3 * pg).astype(_MXU), pm)
        dpg = (dh3 * e * pg * (1.0 - pg)).astype(_MXU)
        gwpg_ref[...] += _tn(n3, dpg)
        dn3 = _nt(dpg, wpg[...])
        dx3, dg3 = _rms_bwd(dn3, xn3, inv3, g3_ref[...])
        dg3_ref[...] += dg3
        dh2_ref[...] = dh3 + dx3

    tspec = lambda w: pl.BlockSpec((tm, w), lambda i: (i, 0))
    vec = pl.BlockSpec((1, D), lambda i: (0, 0))
    mat = pl.BlockSpec((D, D), lambda i: (0, 0))
    return pl.pallas_call(
        body, name="ple_loss", grid=(s // tm,),
        out_shape=(jax.ShapeDtypeStruct((s, D), F32), jax.ShapeDtypeStruct((8, 128), F32),
                   jax.ShapeDtypeStruct((1, D), F32), jax.ShapeDtypeStruct((1, D), F32),
                   jax.ShapeDtypeStruct((D, D), F32), jax.ShapeDtypeStruct((D, D), F32)),
        in_specs=[tspec(D), tspec(PLE), tspec(D), vec, vec, pl.BlockSpec(memory_space=pl.ANY)],
        out_specs=(tspec(D), pl.BlockSpec((8, 128), lambda i: (0, 0)), vec, vec, mat, mat),
        scratch_shapes=[pltpu.VMEM((D, D), _MXU), pltpu.VMEM((D, PLE), _MXU), pltpu.SemaphoreType.DMA((2 * NDEV,))],
        compiler_params=_cparams(1),
    )(h2, p, target, g3, gfin, gw)


def _ffn_bwd_hidden(dh2, gf, uf, gw):
    s = dh2.shape[0]
    tm = min(TM, s)
    nt = s // tm
    half = FF // 2

    def body(dh2_ref, gf_ref, uf_ref, gw_ref, dff_ref, gwo_ref, wffo, gacc, sems):
        i = pl.program_id(0)

        @pl.when(i == 0)
        def _():
            _load_weights(gw_ref, [("wffo", wffo, D)], sems)
            gacc[...] = jnp.zeros_like(gacc)

        dm = dh2_ref[...].astype(_MXU)
        for ch in range(2):
            cs = pl.ds(ch * half, half)
            dact = _nt(dm, wffo[cs, :])
            gfv = gf_ref[:, cs].astype(F32)
            ufv = uf_ref[:, cs].astype(F32)
            sg = _sigmoid(gfv)
            silu = gfv * sg
            gacc[cs, :] += _tn((silu * ufv).astype(_MXU), dm)
            dff_ref[:, pl.ds(ch * half, half)] = (dact * ufv * (sg * (1.0 + gfv * (1.0 - sg)))).astype(_MXU)
            dff_ref[:, pl.ds(FF + ch * half, half)] = (dact * silu).astype(_MXU)

        @pl.when(i == nt - 1)
        def _():
            pltpu.sync_copy(gacc, gwo_ref)

    tspec = lambda w: pl.BlockSpec((tm, w), lambda i: (i, 0))
    return pl.pallas_call(
        body, name="ffn_bwd_hidden", grid=(nt,),
        out_shape=(jax.ShapeDtypeStruct((s, 2 * FF), _MXU), jax.ShapeDtypeStruct((FF, D), F32)),
        in_specs=[tspec(D), tspec(FF), tspec(FF), pl.BlockSpec(memory_space=pl.ANY)],
        out_specs=(tspec(2 * FF), pl.BlockSpec(memory_space=pl.ANY)),
        scratch_shapes=[pltpu.VMEM((FF, D), _MXU), pltpu.VMEM((FF, D), F32), pltpu.SemaphoreType.DMA((NDEV,))],
        compiler_params=_cparams(1),
    )(dh2, gf, uf, gw)


def _proj_norm_bwd(dy, x, dres, g, gw, slab, width, name, lhs=None):
    s = x.shape[0]
    tm = min(TM, s)
    nl = 0 if lhs is None else 1

    def body(*refs):
        dy_ref, x_ref, dr_ref, g_ref = refs[:4]
        l_refs = refs[4:4 + nl]
        gw_ref, dx_ref, dg_ref = refs[4 + nl:7 + nl]
        gl_refs = refs[7 + nl:7 + 2 * nl]
        wT, sems = refs[7 + 2 * nl:]

        @pl.when(pl.program_id(0) == 0)
        def _():
            _load_weights(gw_ref, [(slab, wT, D)], sems)
            dg_ref[...] = jnp.zeros_like(dg_ref)
            for ref in gl_refs:
                ref[...] = jnp.zeros_like(ref)

        dv = _nn(dy_ref[...], wT[...])
        xv = x_ref[...]
        inv = lax.rsqrt(jnp.mean(xv * xv, axis=-1, keepdims=True) + EPS)
        dx, dg = _rms_bwd(dv, xv * inv, inv, g_ref[...])
        dg_ref[...] += dg
        dr = dr_ref[...]
        dx_ref[...] = dr + dx
        for l_ref, gl_ref in zip(l_refs, gl_refs):
            gl_ref[...] += _tn(l_ref[...], dr.astype(_MXU))

    tspec = lambda w: pl.BlockSpec((tm, w), lambda i: (i, 0))
    vec = pl.BlockSpec((1, D), lambda i: (0, 0))
    mat = pl.BlockSpec((D, D), lambda i: (0, 0))
    return pl.pallas_call(
        body, name=name, grid=(s // tm,),
        out_shape=(jax.ShapeDtypeStruct((s, D), F32), jax.ShapeDtypeStruct((1, D), F32))
        + (jax.ShapeDtypeStruct((D, D), F32),) * nl,
        in_specs=[tspec(width), tspec(D), tspec(D), vec] + [tspec(D)] * nl + [pl.BlockSpec(memory_space=pl.ANY)],
        out_specs=(tspec(D), vec) + (mat,) * nl,
        scratch_shapes=[pltpu.VMEM((width, D), _MXU), pltpu.SemaphoreType.DMA((NDEV,))],
        compiler_params=_cparams(1),
    )(dy, x, dres, g, *([] if lhs is None else [lhs]), gw)


def _mixer_bwd(dh1, z, h, y_pool, y_lru, saved, gw, small):
    s = dh1.shape[0]
    tm = min(TM_SEQ, s)
    nt = s // tm
    (pool_w, pool_scale, conv_w, conv_b, w_rg, b_rg, w_ig, b_ig, lam, b_gate) = small

    def body(dh1_ref, z_ref, zp_ref, h_ref, hp_ref, yp_ref, yr_ref, a_ref, r_ref, ig_ref, xc_ref, gw_ref,
             pw_ref, ps_ref, cw_ref, cb_ref, wrg_ref, brg_ref, wig_ref, big_ref, lam_ref, bg_ref,
             dz_ref, dyr_ref, dyp_ref, mx_ref,
             gbg_ref, glam_ref, gbrg_ref, gbig_ref, gcb_ref, gcw_ref, gps_ref, gpw_ref, gwrg_ref, gwig_ref,
             pprojT, lru_w, wout_w, pbuf, lbuf, hbuf, qbuf, xbuf, a_s, g_s, dh_s, hk, pk, dcar, sems):
        step = pl.program_id(0)
        i = nt - 1 - step
        t0 = i * tm

        @pl.when(step == 0)
        def _():
            _load_weights(gw_ref, [("pproj", pprojT, PW), ("lru", lru_w, D), ("wout", wout_w, D)], sems)
            for ref in (gbg_ref, glam_ref, gbrg_ref, gbig_ref, gcb_ref, gcw_ref, gps_ref, gpw_ref, gwrg_ref, gwig_ref):
                ref[...] = jnp.zeros_like(ref)
            qbuf[pl.ds(tm, HALO), :] = jnp.zeros((HALO, PW), F32)
            xbuf[pl.ds(tm, 8), :] = jnp.zeros((8, D), F32)
            dcar[...] = jnp.zeros_like(dcar)

        first = i == 0
        zprev = jnp.where(first, 0.0, zp_ref[...])
        hprev = jnp.where(first, 0.0, hp_ref[...])

        d_merged = _nt(dh1_ref[...].astype(_MXU), wout_w[...])

        g0 = _sigmoid(z_ref[:, pl.ds(PW + 2 * D, D)] + bg_ref[pl.ds(0, 1), :])
        g1 = _sigmoid(z_ref[:, pl.ds(PW + 3 * D, D)] + bg_ref[pl.ds(1, 1), :])
        dz0 = d_merged * yp_ref[...].astype(F32) * g0 * (1.0 - g0)
        dz1 = d_merged * yr_ref[...].astype(F32) * g1 * (1.0 - g1)
        dz_ref[:, pl.ds(PW + 2 * D, D)] = dz0.astype(_MXU)
        dz_ref[:, pl.ds(PW + 3 * D, D)] = dz1.astype(_MXU)
        gbg_ref[pl.ds(0, 1), :] += jnp.sum(dz0, axis=0, keepdims=True)
        gbg_ref[pl.ds(1, 1), :] += jnp.sum(dz1, axis=0, keepdims=True)
        d_ypool = (d_merged * g0).astype(_MXU)
        d_ylru = (d_merged * g1).astype(_MXU)
        dyp_ref[...] = d_ypool
        dyr_ref[...] = d_ylru

        d_yl = _nt(d_ylru, lru_w[...])
        gel, dgel = _gelu_and_grad(z_ref[:, pl.ds(PW + D, D)])
        dz_ref[:, pl.ds(PW + D, D)] = (d_yl * h_ref[...] * dgel).astype(_MXU)
        g_full = d_yl * gel
        lbuf[pl.ds(0, HALO), :] = zprev[:, PW:PW + D]
        lbuf[pl.ds(HALO, tm), :] = z_ref[:, pl.ds(PW, D)]
        hbuf[pl.ds(0, 8), :] = hprev
        hbuf[pl.ds(8, tm), :] = h_ref[...]
        sp, sneg = _softplus_neg(lam_ref[...])
        start = (t0 + lax.broadcasted_iota(jnp.int32, (tm, HD), 0)) == 0
        for hh in range(HEADS):
            cs = pl.ds(hh * HD, HD)
            _to_segments(a_s, hh, a_ref[:, cs], tm)
            _to_segments(g_s, hh, g_full[:, hh * HD:(hh + 1) * HD], tm)
        _segment_scan(a_s, g_s, dh_s, hk, pk, dcar, tm, reverse=True)
        for hh in range(HEADS):
            cs = pl.ds(hh * HD, HD)
            a = a_ref[:, cs]
            r = r_ref[:, cs].astype(F32)
            ig = ig_ref[:, cs].astype(F32)
            xc = xc_ref[:, cs].astype(F32)
            a2 = a * a
            one_m = 1.0 - a2
            live = jnp.logical_and(one_m > 0.0, jnp.logical_not(start))
            inv_mult = lax.rsqrt(jnp.where(live, one_m, 1.0))
            mult = jnp.where(live, one_m * inv_mult, jnp.where(start, 1.0, 0.0))
            dh = _from_segments(dh_s, hh, tm)
            d_mult = dh * ig * xc
            d_loga = dh * hbuf[pl.ds(7, tm), cs] * a - jnp.where(live, d_mult * a2 * inv_mult, 0.0)
            glam_ref[:, cs] += jnp.sum(d_loga * (LRU_C * r) * sneg[:, hh * HD:(hh + 1) * HD], axis=0, keepdims=True)
            d_rpre = d_loga * (-LRU_C * sp[:, hh * HD:(hh + 1) * HD]) * r * (1.0 - r)
            d_igpre = dh * mult * xc * ig * (1.0 - ig)
            gbrg_ref[pl.ds(hh, 1), :] += jnp.sum(d_rpre, axis=0, keepdims=True)
            gbig_ref[pl.ds(hh, 1), :] += jnp.sum(d_igpre, axis=0, keepdims=True)
            drm = d_rpre.astype(_MXU)
            dim = d_igpre.astype(_MXU)
            xcm = xc.astype(_MXU)
            gwrg_ref[hh] += _tn(xcm, drm)
            gwig_ref[hh] += _tn(xcm, dim)
            d_xc = dh * mult * ig + _nt(drm, wrg_ref[hh]) + _nt(dim, wig_ref[hh])
            gcb_ref[:, cs] += jnp.sum(d_xc, axis=0, keepdims=True)
            for k in range(CONV):
                gcw_ref[pl.ds(k, 1), cs] += jnp.sum(d_xc * lbuf[pl.ds(HALO - (CONV - 1) + k, tm), cs], axis=0,
                                                    keepdims=True)
            xbuf[pl.ds(0, tm), cs] = d_xc
        dzl = cw_ref[pl.ds(CONV - 1, 1), :] * xbuf[pl.ds(0, tm), :]
        for k in range(CONV - 1):
            dzl = dzl + cw_ref[pl.ds(k, 1), :] * xbuf[pl.ds(CONV - 1 - k, tm), :]
        dz_ref[:, pl.ds(PW, D)] = dzl.astype(_MXU)
        xbuf[pl.ds(tm, 8), :] = xbuf[pl.ds(0, 8), :]

        d_mixed = _nn(d_ypool, pprojT[...])
        pbuf[pl.ds(0, HALO), :] = zprev[:, 0:PW]
        pbuf[pl.ds(HALO, tm), :] = z_ref[:, pl.ds(0, PW)]
        pooled, mixed_pre = _pool_tile(pbuf, t0, tm, pw_ref, ps_ref)
        mp = jnp.concatenate(mixed_pre, axis=1)
        mx_ref[...] = (mp * ps_ref[...]).astype(_MXU)
        gps_ref[...] += jnp.sum(d_mixed * mp, axis=0, keepdims=True)
        d_mp = (d_mixed * ps_ref[...]).astype(_MXU)
        t = t0 + lax.broadcasted_iota(jnp.int32, (tm, GD), 0)
        d_pooled = []
        for g, w in enumerate(WINDOWS):
            dmg = d_mp[:, g * GD:(g + 1) * GD]
            gpw_ref[g] += _tn(pooled[g].astype(_MXU), dmg)
            dp = _nt(dmg, pw_ref[g])
            d_pooled.append(dp)
            qbuf[pl.ds(0, tm), pl.ds(g * GD, GD)] = dp / jnp.minimum(t + 1, w).astype(F32)
        for g, w in enumerate(WINDOWS):
            cs = pl.ds(g * GD, GD)
            acc = qbuf[pl.ds(0, tm), cs]
            for d in range(1, w):
                acc = acc + qbuf[pl.ds(d, tm), cs]
            dz_ref[:, cs] = (acc - d_pooled[g]).astype(_MXU)
        qbuf[pl.ds(tm, HALO), :] = qbuf[pl.ds(0, HALO), :]

    rev = lambda w: pl.BlockSpec((tm, w), lambda g: (nt - 1 - g, 0))
    prev = lambda rows, w: pl.BlockSpec((rows, w), lambda g: (jnp.maximum((nt - 1 - g) * (tm // rows) - 1, 0), 0))
    full = lambda a: pl.BlockSpec(a.shape, lambda g: (0,) * a.ndim)
    tok = lambda w, dt: jax.ShapeDtypeStruct((s, w), dt)
    acc_shapes = [(2, D), (1, D), (HEADS, HD), (HEADS, HD), (1, D), (CONV, D), (1, PW), (GROUPS, GD, GD),
                  (HEADS, HD, HD), (HEADS, HD, HD)]
    acc_specs = tuple(pl.BlockSpec(sh, lambda g, n=len(sh): (0,) * n) for sh in acc_shapes)
    seg_buf = pltpu.VMEM((HEADS, 8 * _seg_layout(tm)[1], HD), F32)
    a_in, r_in, ig_in, xc_in = saved
    return pl.pallas_call(
        body, name="mixer_bwd", grid=(nt,),
        out_shape=(tok(NIN, _MXU), tok(D, _MXU), tok(D, _MXU), tok(PW, _MXU))
        + tuple(jax.ShapeDtypeStruct(sh, F32) for sh in acc_shapes),
        in_specs=[rev(D), rev(NIN), prev(HALO, NIN), rev(D), prev(8, D), rev(D), rev(D), rev(D), rev(D), rev(D), rev(D),
                  pl.BlockSpec(memory_space=pl.ANY)] + [full(a) for a in small],
        out_specs=(rev(NIN), rev(D), rev(D), rev(PW)) + acc_specs,
        scratch_shapes=[pltpu.VMEM((D, PW), _MXU), pltpu.VMEM((D, D), _MXU), pltpu.VMEM((D, D), _MXU),
                        pltpu.VMEM((tm + HALO, PW), F32), pltpu.VMEM((tm + HALO, D), F32),
                        pltpu.VMEM((tm + 8, D), F32), pltpu.VMEM((tm + HALO, PW), F32), pltpu.VMEM((tm + 8, D), F32),
                        seg_buf, seg_buf, seg_buf, pltpu.VMEM((HEADS, tm, HD), F32), pltpu.VMEM((HEADS, tm, HD), F32),
                        pltpu.VMEM((8, D), F32), pltpu.SemaphoreType.DMA((3 * NDEV,))],
        compiler_params=_cparams(1),
    )(dh1, z, z, h, h, y_pool, y_lru, a_in, r_in, ig_in, xc_in, gw, *small)


def _split3(a):
    hi = a.astype(jnp.bfloat16).astype(F32)
    mid = (a - hi).astype(jnp.bfloat16).astype(F32)
    lo = (a - hi - mid).astype(jnp.bfloat16).astype(F32)
    return jnp.stack([hi, mid, lo])


def _small_pack(parts):
    flat = jnp.concatenate([a.reshape(-1) for a in parts])
    return jnp.pad(flat, (0, NDEV * SMALL_ROWS * D - flat.shape[0])).reshape(NDEV * SMALL_ROWS, D)


def _small_unpack(packed, shapes):
    flat = packed.reshape(-1)
    out, o = [], 0
    for sh in shapes:
        n = math.prod(sh)
        out.append(flat[o:o + n].reshape(sh))
        o += n
    return out


def kernel(x, p, norm1_g, w_in, b_gate, pool_w, pool_scale, pool_proj, conv_w, conv_b, w_rg, b_rg, w_ig, b_ig, lru_lambda, lru_proj, w_out, norm2_g, w_ffn_in, w_ffn_out, ple_norm_g, w_ple_gate, w_ple_proj, final_g, loss_target, m_norm1_g, m_w_in, m_b_gate, m_pool_w, m_pool_scale, m_pool_proj, m_conv_w, m_conv_b, m_w_rg, m_b_rg, m_w_ig, m_b_ig, m_lru_lambda, m_lru_proj, m_w_out, m_norm2_g, m_w_ffn_in, m_w_ffn_out, m_ple_norm_g, m_w_ple_gate, m_w_ple_proj, m_final_g, v_norm1_g, v_w_in, v_b_gate, v_pool_w, v_pool_scale, v_pool_proj, v_conv_w, v_conv_b, v_w_rg, v_b_rg, v_w_ig, v_b_ig, v_lru_lambda, v_lru_proj, v_w_out, v_norm2_g, v_w_ffn_in, v_w_ffn_out, v_ple_norm_g, v_w_ple_gate, v_w_ple_proj, v_final_g):
    axes = ("x", "y", "c")
    me = 4 * lax.axis_index("x") + 2 * lax.axis_index("y") + lax.axis_index("c")
    x2 = x[0]
    p2 = p[0, 0]
    tgt = loss_target[0]

    n_small = (CONV + 2) * 128
    small_terms = _split3(jnp.concatenate([conv_w[0].reshape(-1), b_gate[0].reshape(-1)]))
    small_rows = jnp.pad(small_terms, ((0, 16 - 3), (0, D - n_small)))
    own_first = jnp.concatenate([w_in[0].T.astype(_MXU), small_rows.astype(_MXU)], axis=0)
    own_second = jnp.concatenate([
        w_ffn_in[0].T.astype(_MXU),
        jnp.pad(pool_proj[0].T, ((0, 0), (0, D - PW))).astype(_MXU),
        jnp.pad(w_ple_proj[0].T, ((0, 0), (0, D - PLE))).astype(_MXU),
        lru_proj[0].astype(_MXU), w_out[0].astype(_MXU), w_ffn_out[0].astype(_MXU), w_ple_gate[0].astype(_MXU),
    ], axis=0)
    gw_first = _all_gather_weights(own_first)
    off = W_OFF["f32s"][0]
    st = gw_first[:, off:off + 3, :n_small].astype(F32)
    sf = st[:, 0] + st[:, 1] + st[:, 2]
    conv_w_full = sf[:, :CONV * 128].reshape(NDEV, CONV, 128).transpose(1, 0, 2).reshape(CONV, D)
    b_gate_full = sf[:, CONV * 128:].reshape(NDEV, 2, 128).transpose(1, 0, 2).reshape(2, D)

    small = (pool_w[0].astype(_MXU), pool_scale, conv_w_full, conv_b, w_rg[0].astype(_MXU), b_rg[0],
             w_ig[0].astype(_MXU), b_ig[0], lru_lambda, b_gate_full)

    u, z, gw = _inproj_fwd(x2, norm1_g, gw_first, own_second)
    h, yl, merged, y_pool, y_lru, h1, *saved = _mixer_fwd(z, x2, gw, small)
    v, gf, uf, h2 = _ffn_fwd(h1, norm2_g, gw)

    dh2, loss_blk, g_ple_norm, g_final, part_wpg, part_ple = _ple_loss_fwd_bwd(h2, p2, tgt, ple_norm_g,
                                                                               final_g.reshape(1, D), gw)
    dff, part_wffo = _ffn_bwd_hidden(dh2, gf, uf, gw)
    dh1, g_norm2 = _proj_norm_bwd(dff, h1, dh2, norm2_g, gw, "wffn", 2 * FF, "ffn_bwd_in")
    (dz, d_ylru, d_ypool, mixed, g_bgate, g_lam, g_brg, g_big, g_convb, g_convw, g_pscale, g_poolw, g_wrg,
     g_wig) = _mixer_bwd(dh1, z, h, y_pool, y_lru, saved, gw, small)
    grad_x, g_norm1, part_wout = _proj_norm_bwd(dz, x2, dh1, norm1_g, gw_first, "win", NIN, "inproj_bwd", lhs=merged)

    small_shapes = [(1, D), (GROUPS, GD, GD), (1, PW), (1, D), (HEADS, HD, HD), (HEADS, HD), (HEADS, HD, HD),
                    (HEADS, HD), (1, D), (1, D), (1, D), (1, D), (2, D), (CONV, D), (1, 1)]
    small_part = _small_pack([g_norm1, g_poolw, g_pscale, g_convb, g_wrg, g_brg, g_wig, g_big, g_lam, g_norm2,
                              g_ple_norm, g_final, g_bgate, g_convw, loss_blk[0:1, 0:1]])
    riders = [part_wpg, _grad_matmul(yl, d_ylru, "grad_lru_proj"), part_wout,
              _grad_matmul(d_ypool, mixed, "grad_pool_proj"), part_ple]
    rs_wffn = _grad_matmul_rs(dff, v, "grad_w_ffn_in", 704, extras=[part_wffo], narrow=_MXU)
    rs_win = _grad_matmul_rs(dz, u, "grad_w_in", 576, extras=riders + [small_part], narrow=_MXU, tail=SMALL_ROWS)

    def reduced(parts, name):
        return [_sum_arrays([t_own, landed[0], landed[1], landed[2]], "rs_sum_" + name + str(n))
                for n, (t_own, landed) in enumerate(parts)]

    red_wffn, = reduced(rs_wffn, "wffn")
    red_win, red_small = reduced(rs_win, "win")
    g_w_in = red_win[:576].T
    g_w_ffn_in = red_wffn[:704].T
    g_w_ffn_out = red_wffn[704:]
    g_w_ple_gate, g_lru_proj, g_w_out = red_win[576:704], red_win[704:832], red_win[832:960]
    g_pool_proj = red_win[960:1088, :PW].T
    g_w_ple_proj = red_win[1088:1216, :PLE].T
    small_red = _all_gather_small(red_small)
    (gs_norm1, gs_poolw, gs_pscale, gs_convb, gs_wrg, gs_brg, gs_wig, gs_big, gs_lam, gs_norm2, gs_ple_norm,
     gs_final, gs_bgate, gs_convw, loss_sum) = _small_unpack(small_red, small_shapes)
    loss = loss_sum[0, 0]
    g_b_gate = lax.dynamic_slice_in_dim(gs_bgate, me * 128, 128, axis=1)
    g_conv_w = lax.dynamic_slice_in_dim(gs_convw, me * 128, 128, axis=1)

    grads = {
        "norm1_g": gs_norm1, "w_in": g_w_in[None], "b_gate": g_b_gate[None], "pool_w": gs_poolw[None],
        "pool_scale": gs_pscale, "pool_proj": g_pool_proj[None], "conv_w": g_conv_w[None], "conv_b": gs_convb,
        "w_rg": gs_wrg[None], "b_rg": gs_brg[None], "w_ig": gs_wig[None], "b_ig": gs_big[None], "lru_lambda": gs_lam,
        "lru_proj": g_lru_proj[None], "w_out": g_w_out[None], "norm2_g": gs_norm2, "w_ffn_in": g_w_ffn_in[None],
        "w_ffn_out": g_w_ffn_out[None], "ple_norm_g": gs_ple_norm, "w_ple_gate": g_w_ple_gate[None],
        "w_ple_proj": g_w_ple_proj[None], "final_g": gs_final.reshape(D),
    }
    weights = dict(norm1_g=norm1_g, w_in=w_in, b_gate=b_gate, pool_w=pool_w, pool_scale=pool_scale, pool_proj=pool_proj,
                   conv_w=conv_w, conv_b=conv_b, w_rg=w_rg, b_rg=b_rg, w_ig=w_ig, b_ig=b_ig, lru_lambda=lru_lambda,
                   lru_proj=lru_proj, w_out=w_out, norm2_g=norm2_g, w_ffn_in=w_ffn_in, w_ffn_out=w_ffn_out,
                   ple_norm_g=ple_norm_g, w_ple_gate=w_ple_gate, w_ple_proj=w_ple_proj, final_g=final_g)
    moments_m = dict(norm1_g=m_norm1_g, w_in=m_w_in, b_gate=m_b_gate, pool_w=m_pool_w, pool_scale=m_pool_scale,
                     pool_proj=m_pool_proj, conv_w=m_conv_w, conv_b=m_conv_b, w_rg=m_w_rg, b_rg=m_b_rg, w_ig=m_w_ig,
                     b_ig=m_b_ig, lru_lambda=m_lru_lambda, lru_proj=m_lru_proj, w_out=m_w_out, norm2_g=m_norm2_g,
                     w_ffn_in=m_w_ffn_in, w_ffn_out=m_w_ffn_out, ple_norm_g=m_ple_norm_g, w_ple_gate=m_w_ple_gate,
                     w_ple_proj=m_w_ple_proj, final_g=m_final_g)
    moments_v = dict(norm1_g=v_norm1_g, w_in=v_w_in, b_gate=v_b_gate, pool_w=v_pool_w, pool_scale=v_pool_scale,
                     pool_proj=v_pool_proj, conv_w=v_conv_w, conv_b=v_conv_b, w_rg=v_w_rg, b_rg=v_b_rg, w_ig=v_w_ig,
                     b_ig=v_b_ig, lru_lambda=v_lru_lambda, lru_proj=v_lru_proj, w_out=v_w_out, norm2_g=v_norm2_g,
                     w_ffn_in=v_w_ffn_in, w_ffn_out=v_w_ffn_out, ple_norm_g=v_ple_norm_g, w_ple_gate=v_w_ple_gate,
                     w_ple_proj=v_w_ple_proj, final_g=v_final_g)
    names = list(weights)
    big = ("w_in", "w_ffn_in", "w_ffn_out", "lru_proj", "w_out", "w_ple_gate", "pool_proj", "w_ple_proj")
    slab_space = {"w_in": red_win[:576], "w_ffn_in": red_wffn[:704]}
    delta, new_m, new_v = {}, {}, {}
    for n in big:
        sh = weights[n].shape
        if n in slab_space:
            as2d = lambda a: a[0].T
            back = lambda a: a.T[None]
            g2d = slab_space[n]
        else:
            as2d = lambda a: a.reshape(sh[-2], sh[-1])
            back = lambda a: a.reshape(sh)
            g2d = as2d(grads[n])
        d_, m_, v_ = _adamw(as2d(weights[n]), g2d, as2d(moments_m[n]), as2d(moments_v[n]), "adamw_" + n)
        delta[n], new_m[n], new_v[n] = back(d_), back(m_), back(v_)
    rest = [n for n in names if n not in big]
    rest_shapes = [weights[n].shape for n in rest]
    packed = [_small_pack([src[n] for n in rest]) for src in (weights, grads, moments_m, moments_v)]
    d_, m_, v_ = _adamw(*packed, "adamw_small")
    for n, a, b_, c_ in zip(rest, _small_unpack(d_, rest_shapes), _small_unpack(m_, rest_shapes),
                            _small_unpack(v_, rest_shapes)):
        delta[n], new_m[n], new_v[n] = a, b_, c_

    return (loss, grad_x[None], *[grads[n] for n in names], *[delta[n] for n in names],
            *[new_m[n] for n in names], *[new_v[n] for n in names])
```

```python
import functools
import math

import jax
import jax.numpy as jnp
from jax import lax
from jax.experimental import pallas as pl
from jax.experimental.pallas import tpu as pltpu

F32 = jnp.float32
D = 1024
NIN = 4608
PW = 512
FF = 2816
PLE = 256
HEADS, HD = 8, 128
GROUPS, GD = 4, 128
WINDOWS = (2, 4, 8, 16)
HALO = 16
CONV = 4
EPS = 1e-6
LRU_C = 8.0
NDEV = 8
MESH = pl.DeviceIdType.MESH

ADAM_LR, ADAM_B1, ADAM_B2, ADAM_EPS, ADAM_WD, ADAM_STEP = 0.001, 0.9, 0.999, 1e-08, 0.01, 10

_MXU = jnp.bfloat16
TM = 512
TM_SEQ = 256
VMEM_LIMIT = 56 * 1024 * 1024
W_FIRST = (("win", 576), ("f32s", 16))
W_SECOND = (("wffn", 704), ("pproj", 128), ("ple", 128), ("lru", 128), ("wout", 128), ("wffo", 352), ("wpg", 128))
W_OFF = {}
for _slabs in (W_FIRST, W_SECOND):
    _o = 0
    for _n, _r in _slabs:
        W_OFF[_n] = (_o, _r)
        _o += _r
SMALL_ROWS = 48


def _cparams(n_axes=1, vmem=VMEM_LIMIT):
    return pltpu.CompilerParams(dimension_semantics=("arbitrary",) * n_axes, vmem_limit_bytes=vmem)


def _my_pos():
    return lax.axis_index("x"), lax.axis_index("y"), lax.axis_index("c")


def _nt(a, b):
    return lax.dot_general(a, b, (((1,), (1,)), ((), ())), preferred_element_type=F32)


def _nn(a, b):
    return lax.dot_general(a, b, (((1,), (0,)), ((), ())), preferred_element_type=F32)


def _tn(a, b):
    return lax.dot_general(a, b, (((0,), (0,)), ((), ())), preferred_element_type=F32)


def _sigmoid(x):
    return 0.5 * jnp.tanh(0.5 * x) + 0.5


_GELU_K = math.sqrt(2.0 / math.pi)


def _gelu_and_grad(x):
    x2 = x * x
    inner = _GELU_K * (x + 0.044715 * x2 * x)
    t = jnp.tanh(inner)
    g = 0.5 * x * (1.0 + t)
    dg = 0.5 * (1.0 + t) + 0.5 * x * (1.0 - t * t) * _GELU_K * (1.0 + 3.0 * 0.044715 * x2)
    return g, dg


def _softplus_neg(lam):
    x = -lam
    t = jnp.exp(-jnp.abs(x))
    u = 1.0 + t
    l1p = jnp.where(u == 1.0, t, jnp.log(u) * t / (u - 1.0))
    return jnp.maximum(x, 0.0) + l1p, _sigmoid(x)


def _start_slab_loads(g_ref, name, dst_ref, sems, base, width=D):
    off, rows = W_OFF[name]
    copies = []
    for k in range(NDEV):
        if width == D:
            src = g_ref.at[k, pl.ds(off, rows), :]
        else:
            src = g_ref.at[k, pl.ds(off, rows), pl.ds(0, width)]
        cp = pltpu.make_async_copy(src, dst_ref.at[pl.ds(k * rows, rows), :], sems.at[base + k])
        cp.start()
        copies.append(cp)
    return copies


def _load_weights(g_ref, items, sems):
    copies = []
    for n, (name, dst, width) in enumerate(items):
        copies += _start_slab_loads(g_ref, name, dst, sems, n * NDEV, width)
    for cp in copies:
        cp.wait()


def _gather_phases(own_ref, out_ref, stage, send_sems, recv_sems, local_sem):
    x, y, c = _my_pos()
    me, sibling = (x, y, c), (x, y, 1 - c)
    chips = [(1 - x, y), (x, 1 - y), (1 - x, 1 - y)]

    def slab(px, py, pc):
        return out_ref.at[4 * px + 2 * py + pc]

    def copy(k, block, to, src=None):
        return pltpu.make_async_remote_copy(
            src_ref=slab(*block) if src is None else src, dst_ref=slab(*block),
            send_sem=send_sems.at[k], recv_sem=recv_sems.at[k], device_id=to, device_id_type=MESH)

    mine = pltpu.make_async_copy(stage, slab(*me), local_sem)
    first = [copy(0, me, sibling, src=stage)] + [copy(1 + j, me, (*chip, c), src=stage) for j, chip in enumerate(chips)]
    passed = [copy(4 + j, (*chip, c), sibling) for j, chip in enumerate(chips)]

    def send_mine():
        pltpu.sync_copy(own_ref, stage)
        mine.start()
        for cp in first:
            cp.start()

    def pass_on(js):
        for j in js:
            copy(1 + j, (*chips[j], c), me).wait_recv()
            passed[j].start()

    def finish():
        copy(0, sibling, me).wait_recv()
        for j, chip in enumerate(chips):
            copy(4 + j, (*chip, 1 - c), me).wait_recv()
        for cp in first + passed:
            cp.wait_send()
        mine.wait()

    return send_mine, pass_on, finish


def _all_gather_weights(own):
    rows, cols = own.shape

    def body(own_ref, out_ref, stage, send_sems, recv_sems, local_sem):
        send_mine, pass_on, finish = _gather_phases(own_ref, out_ref, stage, send_sems, recv_sems, local_sem)
        send_mine()
        pass_on((0, 1, 2))
        finish()

    return pl.pallas_call(
        body, name="ag_weights",
        out_shape=jax.ShapeDtypeStruct((NDEV, rows, cols), own.dtype),
        in_specs=[pl.BlockSpec(memory_space=pl.ANY)],
        out_specs=pl.BlockSpec(memory_space=pl.ANY),
        scratch_shapes=[pltpu.VMEM((rows, cols), own.dtype), pltpu.SemaphoreType.DMA((7,)),
                        pltpu.SemaphoreType.DMA((7,)), pltpu.SemaphoreType.DMA],
        compiler_params=pltpu.CompilerParams(vmem_limit_bytes=VMEM_LIMIT),
    )(own)


def _all_gather_small(piece):
    rows = piece.shape[0]

    def body(p_ref, out_ref, send_sems, recv_sems, local_sem):
        x, y, c = _my_pos()
        me = 4 * x + 2 * y + c
        mine = pltpu.make_async_copy(p_ref, out_ref.at[pl.ds(pl.multiple_of(me * rows, 8), rows), :], local_sem)
        mine.start()
        sends = []
        peers = []
        for r in range(1, NDEV):
            px = 1 - x if (r >> 2) & 1 else x
            py = 1 - y if (r >> 1) & 1 else y
            pc = 1 - c if r & 1 else c
            peers.append((px, py, pc))
            cp = pltpu.make_async_remote_copy(
                src_ref=p_ref, dst_ref=out_ref.at[pl.ds(pl.multiple_of(me * rows, 8), rows), :],
                send_sem=send_sems.at[r - 1], recv_sem=recv_sems.at[r - 1], device_id=(px, py, pc),
                device_id_type=MESH)
            cp.start()
            sends.append(cp)
        for r, (px, py, pc) in enumerate(peers):
            them = 4 * px + 2 * py + pc
            pltpu.make_async_remote_copy(
                src_ref=p_ref, dst_ref=out_ref.at[pl.ds(pl.multiple_of(them * rows, 8), rows), :],
                send_sem=send_sems.at[r], recv_sem=recv_sems.at[r], device_id=(px, py, pc),
                device_id_type=MESH).wait_recv()
        for cp in sends:
            cp.wait_send()
        mine.wait()

    return pl.pallas_call(
        body, name="ag_small",
        out_shape=jax.ShapeDtypeStruct((NDEV * rows, piece.shape[1]), piece.dtype),
        in_specs=[pl.BlockSpec(memory_space=pltpu.VMEM)],
        out_specs=pl.BlockSpec(memory_space=pl.ANY),
        scratch_shapes=[pltpu.SemaphoreType.DMA((7,)), pltpu.SemaphoreType.DMA((7,)), pltpu.SemaphoreType.DMA],
    )(piece)


def _row_block(rows, target=512, mult=8):
    b = min(rows, target) // mult * mult
    while rows % b:
        b -= mult
    return b


def _sum_arrays(arrs, name, narrow=None, target=464):
    rows, cols = arrs[0].shape
    br = _row_block(rows, target, 16)
    n = len(arrs)

    def body(*refs):
        acc = refs[0][...].astype(F32)
        for r in refs[1:n]:
            acc = acc + r[...].astype(F32)
        refs[n][...] = acc
        if narrow is not None:
            refs[n + 1][...] = acc.astype(narrow)

    spec = pl.BlockSpec((br, cols), lambda i: (i, 0))
    shape = jax.ShapeDtypeStruct((rows, cols), F32)
    if narrow is None:
        out_shape, out_specs = shape, spec
    else:
        out_shape, out_specs = (shape, jax.ShapeDtypeStruct((rows, cols), narrow)), (spec, spec)
    return pl.pallas_call(
        body, name=name, grid=(rows // br,), out_shape=out_shape,
        in_specs=[spec] * n, out_specs=out_specs, compiler_params=_cparams(1),
    )(*arrs)


def _adamw(w, g, m, v, name):
    rows, cols = w.shape
    br = _row_block(rows, 256)

    def body(w_ref, g_ref, m_ref, v_ref, d_ref, nm_ref, nv_ref):
        g_ = g_ref[...]
        m_ = ADAM_B1 * m_ref[...] + (1.0 - ADAM_B1) * g_
        v_ = ADAM_B2 * v_ref[...] + (1.0 - ADAM_B2) * (g_ * g_)
        m_hat = m_ / (1.0 - ADAM_B1 ** ADAM_STEP)
        v_hat = v_ / (1.0 - ADAM_B2 ** ADAM_STEP)
        d_ref[...] = -ADAM_LR * (m_hat / (jnp.sqrt(v_hat) + ADAM_EPS) + ADAM_WD * w_ref[...])
        nm_ref[...] = m_
        nv_ref[...] = v_

    spec = pl.BlockSpec((br, cols), lambda i: (i, 0))
    shape = jax.ShapeDtypeStruct((rows, cols), F32)
    return pl.pallas_call(
        body, name=name, grid=(rows // br,), out_shape=(shape, shape, shape),
        in_specs=[spec] * 4, out_specs=(spec, spec, spec), compiler_params=_cparams(1),
    )(w, g, m, v)


_CHIP_FLIPS = (2, 3, 1, 0)


def _grad_matmul(lhs, rhs, name):
    s, r = lhs.shape
    k = rhs.shape[1]
    tm = min(TM, s)

    def body(l_ref, r_ref, o_ref):
        @pl.when(pl.program_id(0) == 0)
        def _():
            o_ref[...] = jnp.zeros_like(o_ref)

        o_ref[:, pl.ds(0, k)] += _tn(l_ref[...].astype(_MXU), r_ref[...].astype(_MXU))

    return pl.pallas_call(
        body, name=name, grid=(s // tm,),
        out_shape=jax.ShapeDtypeStruct((r, D), F32),
        in_specs=[pl.BlockSpec((tm, r), lambda i: (i, 0)), pl.BlockSpec((tm, k), lambda i: (i, 0))],
        out_specs=pl.BlockSpec((r, D), lambda i: (0, 0)),
        compiler_params=_cparams(1),
    )(lhs, rhs)


def _grad_matmul_rs(lhs, rhs, name, rows, extras=(), narrow=None, tail=0):
    s, r8 = lhs.shape
    k = rhs.shape[1]
    tm = min(TM, s)
    nt = s // tm
    cpb = 1
    nblk = 4 // cpb
    nx = len(extras)
    ers = [e.shape[0] // NDEV for e in extras]
    er = sum(ers)
    srows = rows + er
    brows = 2 * cpb * srows
    groups = [(0, srows - tail, F32 if narrow is None else narrow)] + ([(srows - tail, tail, F32)] if tail else [])
    ng = len(groups)
    mid = min(nt - 1, max(1, nt // 4))

    def flip_of(p):
        return jnp.where(p == 0, 2, jnp.where(p == 1, 3, jnp.where(p == 2, 1, 0)))

    def block_col(b):
        x, y, _ = _my_pos()
        return (2 * x + y) ^ flip_of(b)

    def body(*refs):
        l_ref, r_ref = refs[:2]
        x_refs = refs[2:2 + nx]
        rest = refs[2 + nx:]
        town_ref = rest[0]
        lici_refs = rest[1:1 + ng]
        acc, stage = rest[1 + ng:3 + ng]
        send_bufs = rest[3 + ng:3 + 2 * ng]
        dsend, drecv, isend, irecv, xsem = rest[3 + 2 * ng:]
        b = pl.program_id(0)
        i = pl.program_id(1)
        x, y, c = _my_pos()
        mine = 2 * x + y
        sibling = (x, y, 1 - c)

        def chip_at(p):
            return mine ^ _CHIP_FLIPS[p]

        def slab_rows(p, parity):
            within = 0 if cpb == 1 else (chip_at(p) & 1) * 2
            return pl.ds(pl.multiple_of((within + parity) * srows, 8), srows)

        def push(p, slot):
            return pltpu.make_async_remote_copy(
                src_ref=acc.at[slot, slab_rows(p, 1 - c), :], dst_ref=stage.at[p % 2],
                send_sem=dsend.at[p], recv_sem=drecv.at[p], device_id=sibling, device_id_type=MESH)

        def ici(p):
            ch = chip_at(p)
            return [pltpu.make_async_remote_copy(
                src_ref=send_bufs[g].at[p % 2], dst_ref=lici_refs[g].at[p], send_sem=isend.at[3 * g + p],
                recv_sem=irecv.at[3 * g + p], device_id=(ch >> 1, ch & 1, c), device_id_type=MESH) for g in range(ng)]

        def extra_loads(p, slot):
            copies = []
            within = 0 if cpb == 1 else (chip_at(p) & 1) * 2
            for parity in range(2):
                off = rows
                for n, (x_ref, e) in enumerate(zip(x_refs, ers)):
                    src = x_ref.at[pl.ds(pl.multiple_of((2 * chip_at(p) + parity) * e, 8), e), :]
                    dst = acc.at[slot, pl.ds(pl.multiple_of((within + parity) * srows + off, 8), e), :]
                    copies.append(pltpu.make_async_copy(src, dst, xsem.at[(p * 2 + parity) * nx + n]))
                    off += e
            return copies

        def combine(p, slot):
            push(p, slot).wait_recv()
            total = acc[slot, slab_rows(p, c), :] + stage[p % 2]
            if p == 3:
                stage[p % 2] = total
                pltpu.sync_copy(stage.at[p % 2], town_ref)
            else:
                if p == 2:
                    for cp in ici(0):
                        cp.wait_send()
                for g, (r0, n, dt) in enumerate(groups):
                    send_bufs[g][p % 2] = total[r0:r0 + n, :].astype(dt)
                for cp in ici(p):
                    cp.start()

        for bb in range(nblk):
            slot = bb % 2
            positions = list(range(bb * cpb, (bb + 1) * cpb))

            @pl.when(jnp.logical_and(b == bb, i == 0))
            def _(bb=bb, slot=slot, positions=positions):
                if bb >= 2:
                    for p in range((bb - 2) * cpb, (bb - 1) * cpb):
                        push(p, slot).wait_send()
                for q in range(2 * cpb):
                    acc[slot, pl.ds(q * srows, rows), :] = jnp.zeros((rows, D), F32)
                for p in positions:
                    for cp in extra_loads(p, slot):
                        cp.start()

            if bb >= 1:
                @pl.when(jnp.logical_and(b == bb, i == mid))
                def _(bb=bb):
                    for p in range((bb - 1) * cpb, bb * cpb):
                        combine(p, (bb - 1) % 2)

        res = _tn(l_ref[...].astype(_MXU), r_ref[...].astype(_MXU))
        slot_now = b % 2
        for q in range(2 * cpb):
            acc[slot_now, pl.ds(q * srows, rows), pl.ds(0, k)] += res[q * rows:(q + 1) * rows, :]

        for bb in range(nblk):
            slot = bb % 2
            positions = list(range(bb * cpb, (bb + 1) * cpb))

            @pl.when(jnp.logical_and(b == bb, i == nt - 1))
            def _(bb=bb, slot=slot, positions=positions):
                for p in positions:
                    for cp in extra_loads(p, slot):
                        cp.wait()
                for p in positions:
                    push(p, slot).start()
                if bb == nblk - 1:
                    for p in positions:
                        combine(p, slot)
                    for p in range(max(0, (nblk - 2) * cpb), 4):
                        push(p, slot).wait_send()
                    for p in range(1, 3):
                        for cp in ici(p):
                            cp.wait_send()
                    for p in range(3):
                        for cp in ici(p):
                            cp.wait_recv()

    in_specs = [pl.BlockSpec((tm, 2 * cpb * rows), lambda b, i: (i, block_col(b))),
                pl.BlockSpec((tm, k), lambda b, i: (i, 0))]
    any_spec = pl.BlockSpec(memory_space=pl.ANY)
    in_specs += [any_spec] * nx
    args = [lhs, rhs, *extras]
    outs = pl.pallas_call(
        body, name=name, grid=(nblk, nt),
        out_shape=(jax.ShapeDtypeStruct((srows, D), F32),)
        + tuple(jax.ShapeDtypeStruct((3, n, D), dt) for _, n, dt in groups),
        in_specs=in_specs, out_specs=(any_spec,) * (1 + ng),
        scratch_shapes=[pltpu.VMEM((2, brows, D), F32), pltpu.VMEM((2, srows, D), F32)]
        + [pltpu.VMEM((2, n, D), dt) for _, n, dt in groups]
        + [pltpu.SemaphoreType.DMA((4,)), pltpu.SemaphoreType.DMA((4,)), pltpu.SemaphoreType.DMA((3 * ng,)),
           pltpu.SemaphoreType.DMA((3 * ng,)), pltpu.SemaphoreType.DMA((max(1, 8 * nx),))],
        compiler_params=_cparams(2),
    )(*args)
    t_own = outs[0]
    return [(t_own[r0:r0 + n], landed) for (r0, n, _), landed in zip(groups, outs[1:])]


def _inproj_fwd(x, g1, gw, own_second):
    s = x.shape[0]
    tm = min(TM, s)
    nt = s // tm
    nchunk = 4
    cw = NIN // nchunk
    rows2, cols2 = own_second.shape

    def body(x_ref, g1_ref, gw_ref, own_ref, u_ref, z_ref, gw2_ref, w_vmem, stage, sems, send_sems, recv_sems,
             local_sem):
        i = pl.program_id(0)
        send_mine, pass_on, finish = _gather_phases(own_ref, gw2_ref, stage, send_sems, recv_sems, local_sem)

        @pl.when(i == 0)
        def _():
            send_mine()
            _load_weights(gw_ref, [("win", w_vmem, D)], sems)

        @pl.when(i == nt // 2)
        def _():
            pass_on((0, 1))

        @pl.when(i == (7 * nt) // 8)
        def _():
            pass_on((2,))

        xv = x_ref[...]
        inv = lax.rsqrt(jnp.mean(xv * xv, axis=-1, keepdims=True) + EPS)
        u = (xv * inv * g1_ref[...]).astype(_MXU)
        u_ref[...] = u
        for ch in range(nchunk):
            z_ref[:, pl.ds(ch * cw, cw)] = _nt(u, w_vmem[pl.ds(ch * cw, cw), :])

        @pl.when(i == nt - 1)
        def _():
            finish()

    any_spec = pl.BlockSpec(memory_space=pl.ANY)
    return pl.pallas_call(
        body, name="inproj_fwd", grid=(nt,),
        out_shape=(jax.ShapeDtypeStruct((s, D), _MXU), jax.ShapeDtypeStruct((s, NIN), F32),
                   jax.ShapeDtypeStruct((NDEV, rows2, cols2), own_second.dtype)),
        in_specs=[pl.BlockSpec((tm, D), lambda i: (i, 0)), pl.BlockSpec((1, D), lambda i: (0, 0)), any_spec, any_spec],
        out_specs=(pl.BlockSpec((tm, D), lambda i: (i, 0)), pl.BlockSpec((tm, NIN), lambda i: (i, 0)), any_spec),
        scratch_shapes=[pltpu.VMEM((NIN, D), _MXU), pltpu.VMEM((rows2, cols2), own_second.dtype),
                        pltpu.SemaphoreType.DMA((NDEV,)), pltpu.SemaphoreType.DMA((7,)), pltpu.SemaphoreType.DMA((7,)),
                        pltpu.SemaphoreType.DMA],
        compiler_params=_cparams(1),
    )(x, g1, gw, own_second)


def _pool_tile(pbuf, t0, tm, pw_ref, scale_ref):
    t = t0 + lax.broadcasted_iota(jnp.int32, (tm, GD), 0)
    pooled, mixed_pre = [], []
    for g, w in enumerate(WINDOWS):
        cs = pl.ds(g * GD, GD)
        cur = pbuf[pl.ds(HALO, tm), cs]
        acc = cur
        for d in range(1, w):
            acc = acc + pbuf[pl.ds(HALO - d, tm), cs]
        cnt = jnp.minimum(t + 1, w).astype(F32)
        pg = acc / cnt - cur
        pooled.append(pg)
        mixed_pre.append(_nn(pg.astype(_MXU), pw_ref[g]))
    return pooled, mixed_pre


def _lru_gates_head(hh, lbuf, start, tm, cw_ref, cb_ref, wrg_ref, brg_ref, wig_ref, big_ref, sp):
    cs = pl.ds(hh * HD, HD)
    xc = cb_ref[:, cs] + cw_ref[pl.ds(CONV - 1, 1), cs] * lbuf[pl.ds(HALO, tm), cs]
    for k in range(CONV - 1):
        xc = xc + cw_ref[pl.ds(k, 1), cs] * lbuf[pl.ds(HALO - (CONV - 1) + k, tm), cs]
    xcm = xc.astype(_MXU)
    r = _sigmoid(_nn(xcm, wrg_ref[hh]) + brg_ref[pl.ds(hh, 1), :])
    ig = _sigmoid(_nn(xcm, wig_ref[hh]) + big_ref[pl.ds(hh, 1), :])
    a = jnp.exp(-LRU_C * r * sp[:, hh * HD:(hh + 1) * HD])
    one_m = 1.0 - a * a
    live = jnp.logical_and(one_m > 0.0, jnp.logical_not(start))
    inv_mult = lax.rsqrt(jnp.where(live, one_m, 1.0))
    mult = jnp.where(live, one_m * inv_mult, jnp.where(start, 1.0, 0.0))
    return xc, r, ig, a, live, inv_mult, mult


def _seg_layout(tm):
    seg = tm // 8
    return seg, seg + 8


def _to_segments(dst_ref, hh, val, tm):
    seg, pitch = _seg_layout(tm)
    for s in range(8):
        dst_ref[hh, pl.ds(s * pitch, seg), :] = val[s * seg:(s + 1) * seg, :]


def _from_segments(src_ref, hh, tm):
    seg, pitch = _seg_layout(tm)
    return jnp.concatenate([src_ref[hh, pl.ds(s * pitch, seg), :] for s in range(8)], axis=0)


def _segment_scan(a_ref, b_ref, out_ref, hk, pk, carry_ref, tm, reverse):
    seg, pitch = _seg_layout(tm)
    row = lax.broadcasted_iota(jnp.int32, (8, HD), 0)
    order = range(seg - 1, -1, -1) if reverse else range(seg)
    for hh in range(HEADS):
        cs = pl.ds(hh * HD, HD)
        if reverse:
            a0 = a_ref[hh, pl.ds(0, 8, stride=pitch), :]
            a_wrap = jnp.where(row <= 6, pltpu.roll(a0, 7, 0), 1.0)
        hv = jnp.zeros((8, HD), F32)
        pv = jnp.ones((8, HD), F32)
        for k in order:
            if not reverse:
                av = a_ref[hh, pl.ds(k, 8, stride=pitch), :]
            elif k + 1 < seg:
                av = a_ref[hh, pl.ds(k + 1, 8, stride=pitch), :]
            else:
                av = a_wrap
            hv = av * hv + b_ref[hh, pl.ds(k, 8, stride=pitch), :]
            pv = av * pv
            hk[hh, pl.ds(8 * k, 8), :] = hv
            pk[hh, pl.ds(8 * k, 8), :] = pv
        for d in (1, 2, 4):
            if reverse:
                keep, sh = row < 8 - d, 8 - d
            else:
                keep, sh = row >= d, d
            hv = hv + pv * jnp.where(keep, pltpu.roll(hv, sh, 0), 0.0)
            pv = pv * jnp.where(keep, pltpu.roll(pv, sh, 0), 1.0)
        cin = carry_ref[:, cs]
        ends = hv + pv * cin
        if reverse:
            enter = jnp.where(row <= 6, pltpu.roll(ends, 7, 0), cin)
            carry_ref[:, cs] = jnp.broadcast_to((a0 * ends)[0:1, :], (8, HD))
        else:
            enter = jnp.where(row >= 1, pltpu.roll(ends, 1, 0), cin)
            carry_ref[:, cs] = jnp.broadcast_to(ends[7:8, :], (8, HD))
        for k in range(seg):
            out_ref[hh, pl.ds(k, 8, stride=pitch), :] = hk[hh, pl.ds(8 * k, 8), :] + pk[hh, pl.ds(8 * k, 8), :] * enter


def _mixer_fwd(z, x, gw, small):
    s = x.shape[0]
    tm = min(TM_SEQ, s)
    (pool_w, pool_scale, conv_w, conv_b, w_rg, b_rg, w_ig, b_ig, lam, b_gate) = small

    def body(z_ref, x_ref, gw_ref, pw_ref, ps_ref, cw_ref, cb_ref, wrg_ref, brg_ref, wig_ref, big_ref, lam_ref,
             bg_ref, h_ref, yl_ref, mg_ref, yp_ref, yr_ref, h1_ref, a_ref, r_ref, ig_ref, xc_ref,
             pprojT, lru_w, wout_w, pbuf, lbuf, a_s, b_s, h_s, hk, pk, hcar, sems):
        i = pl.program_id(0)
        t0 = i * tm

        @pl.when(i == 0)
        def _():
            _load_weights(gw_ref, [("pproj", pprojT, PW), ("lru", lru_w, D), ("wout", wout_w, D)], sems)
            pbuf[pl.ds(0, HALO), :] = jnp.zeros((HALO, PW), F32)
            lbuf[pl.ds(0, HALO), :] = jnp.zeros((HALO, D), F32)
            hcar[...] = jnp.zeros_like(hcar)

        pbuf[pl.ds(HALO, tm), :] = z_ref[:, pl.ds(0, PW)]
        _, mixed_pre = _pool_tile(pbuf, t0, tm, pw_ref, ps_ref)
        mixed = jnp.concatenate(mixed_pre, axis=1) * ps_ref[...]
        y_pool = _nt(mixed.astype(_MXU), pprojT[...])
        pbuf[pl.ds(0, HALO), :] = pbuf[pl.ds(tm, HALO), :]

        lbuf[pl.ds(HALO, tm), :] = z_ref[:, pl.ds(PW, D)]
        sp, _ = _softplus_neg(lam_ref[...])
        start = (t0 + lax.broadcasted_iota(jnp.int32, (tm, HD), 0)) == 0
        for hh in range(HEADS):
            xc, r, ig, a, _, _, mult = _lru_gates_head(hh, lbuf, start, tm, cw_ref, cb_ref, wrg_ref, brg_ref,
                                                       wig_ref, big_ref, sp)
            _to_segments(a_s, hh, a, tm)
            _to_segments(b_s, hh, mult * ig * xc, tm)
            cs = pl.ds(hh * HD, HD)
            a_ref[:, cs] = a
            r_ref[:, cs] = r.astype(_MXU)
            ig_ref[:, cs] = ig.astype(_MXU)
            xc_ref[:, cs] = xc.astype(_MXU)
        lbuf[pl.ds(0, HALO), :] = lbuf[pl.ds(tm, HALO), :]
        _segment_scan(a_s, b_s, h_s, hk, pk, hcar, tm, reverse=False)
        for hh in range(HEADS):
            h_ref[:, pl.ds(hh * HD, HD)] = _from_segments(h_s, hh, tm)
        gel, _ = _gelu_and_grad(z_ref[:, pl.ds(PW + D, D)])
        yl = (h_ref[...] * gel).astype(_MXU)
        yl_ref[...] = yl
        y_lru = _nn(yl, lru_w[...])

        g0 = _sigmoid(z_ref[:, pl.ds(PW + 2 * D, D)] + bg_ref[pl.ds(0, 1), :])
        g1 = _sigmoid(z_ref[:, pl.ds(PW + 3 * D, D)] + bg_ref[pl.ds(1, 1), :])
        merged = (g0 * y_pool + g1 * y_lru).astype(_MXU)
        mg_ref[...] = merged
        yp_ref[...] = y_pool.astype(_MXU)
        yr_ref[...] = y_lru.astype(_MXU)
        h1_ref[...] = x_ref[...] + _nn(merged, wout_w[...])

    tok = lambda w, dt: jax.ShapeDtypeStruct((s, w), dt)
    tspec = lambda w: pl.BlockSpec((tm, w), lambda i: (i, 0))
    full = lambda a: pl.BlockSpec(a.shape, lambda i: (0,) * a.ndim)
    seg_buf = pltpu.VMEM((HEADS, 8 * _seg_layout(tm)[1], HD), F32)
    return pl.pallas_call(
        body, name="mixer_fwd", grid=(s // tm,),
        out_shape=(tok(D, F32), tok(D, _MXU), tok(D, _MXU), tok(D, _MXU), tok(D, _MXU), tok(D, F32),
                   tok(D, F32), tok(D, _MXU), tok(D, _MXU), tok(D, _MXU)),
        in_specs=[tspec(NIN), tspec(D), pl.BlockSpec(memory_space=pl.ANY)] + [full(a) for a in small],
        out_specs=(tspec(D),) * 10,
        scratch_shapes=[pltpu.VMEM((D, PW), _MXU), pltpu.VMEM((D, D), _MXU), pltpu.VMEM((D, D), _MXU),
                        pltpu.VMEM((tm + HALO, PW), F32), pltpu.VMEM((tm + HALO, D), F32),
                        seg_buf, seg_buf, seg_buf, pltpu.VMEM((HEADS, tm, HD), F32), pltpu.VMEM((HEADS, tm, HD), F32),
                        pltpu.VMEM((8, D), F32), pltpu.SemaphoreType.DMA((3 * NDEV,))],
        compiler_params=_cparams(1),
    )(z, x, gw, *small)


def _ffn_fwd(h1, g2, gw):
    s = h1.shape[0]
    tm = min(TM, s)
    half = FF // 2

    def body(h1_ref, g2_ref, gw_ref, v_ref, gf_ref, uf_ref, h2_ref, wffnT, wffo, sems):
        @pl.when(pl.program_id(0) == 0)
        def _():
            _load_weights(gw_ref, [("wffn", wffnT, D), ("wffo", wffo, D)], sems)

        hv = h1_ref[...]
        inv = lax.rsqrt(jnp.mean(hv * hv, axis=-1, keepdims=True) + EPS)
        v = (hv * inv * g2_ref[...]).astype(_MXU)
        v_ref[...] = v
        acc = hv
        for ch in range(2):
            cs = pl.ds(ch * half, half)
            gf = _nt(v, wffnT[pl.ds(ch * half, half), :]).astype(_MXU)
            uf = _nt(v, wffnT[pl.ds(FF + ch * half, half), :]).astype(_MXU)
            gf_ref[:, cs] = gf
            uf_ref[:, cs] = uf
            gf32 = gf.astype(F32)
            act = (gf32 * _sigmoid(gf32) * uf.astype(F32)).astype(_MXU)
            acc = acc + _nn(act, wffo[pl.ds(ch * half, half), :])
        h2_ref[...] = acc

    tspec = lambda w: pl.BlockSpec((tm, w), lambda i: (i, 0))
    return pl.pallas_call(
        body, name="ffn_fwd", grid=(s // tm,),
        out_shape=(jax.ShapeDtypeStruct((s, D), _MXU), jax.ShapeDtypeStruct((s, FF), _MXU),
                   jax.ShapeDtypeStruct((s, FF), _MXU), jax.ShapeDtypeStruct((s, D), F32)),
        in_specs=[tspec(D), pl.BlockSpec((1, D), lambda i: (0, 0)), pl.BlockSpec(memory_space=pl.ANY)],
        out_specs=(tspec(D), tspec(FF), tspec(FF), tspec(D)),
        scratch_shapes=[pltpu.VMEM((2 * FF, D), _MXU), pltpu.VMEM((FF, D), _MXU), pltpu.SemaphoreType.DMA((2 * NDEV,))],
        compiler_params=_cparams(1),
    )(h1, g2, gw)


def _rms_bwd(dy, xn, inv, g):
    dg = jnp.sum(dy * xn, axis=0, keepdims=True)
    dxn = dy * g
    dx = inv * (dxn - xn * jnp.mean(dxn * xn, axis=-1, keepdims=True))
    return dx, dg


def _ple_loss_fwd_bwd(h2, p, target, g3, gfin, gw):
    s = h2.shape[0]
    tm = min(TM, s)

    def body(h2_ref, p_ref, t_ref, g3_ref, gf_ref, gw_ref,
             dh2_ref, loss_ref, dg3_ref, dgf_ref, gwpg_ref, gple_ref, wpg, pleT, sems):
        i = pl.program_id(0)

        @pl.when(i == 0)
        def _():
            _load_weights(gw_ref, [("wpg", wpg, D), ("ple", pleT, PLE)], sems)
            for ref in (loss_ref, dg3_ref, dgf_ref, gwpg_ref, gple_ref):
                ref[...] = jnp.zeros_like(ref)

        hv = h2_ref[...]
        inv3 = lax.rsqrt(jnp.mean(hv * hv, axis=-1, keepdims=True) + EPS)
        xn3 = hv * inv3
        n3 = (xn3 * g3_ref[...]).astype(_MXU)
        pg = _sigmoid(_nn(n3, wpg[...]))
        pm = p_ref[...].astype(_MXU)
        e = _nt(pm, pleT[...])
        h3 = hv + pg * e
        invf = lax.rsqrt(jnp.mean(h3 * h3, axis=-1, keepdims=True) + EPS)
        xf = h3 * invf
        diff = xf * gf_ref[...] - t_ref[...]
        loss_ref[...] += jnp.sum(diff * diff) * (0.5 / D)
        dh3, dgf = _rms_bwd(diff * (1.0 / D), xf, invf, gf_ref[...])
        dgf_ref[...] += dgf
        gple_ref[:, pl.ds(0, PLE)] += _tn((dh3 * pg).astype(_MXU), pm)
        dpg = (dh3 * e * pg * (1.0 - pg)).astype(_MXU)
        gwpg_ref[...] += _tn(n3, dpg)
        dn3 = _nt(dpg, wpg[...])
        dx3, dg3 = _rms_bwd(dn3, xn3, inv3, g3_ref[...])
        dg3_ref[...] += dg3
        dh2_ref[...] = dh3 + dx3

    tspec = lambda w: pl.BlockSpec((tm, w), lambda i: (i, 0))
    vec = pl.BlockSpec((1, D), lambda i: (0, 0))
    mat = pl.BlockSpec((D, D), lambda i: (0, 0))
    return pl.pallas_call(
        body, name="ple_loss", grid=(s // tm,),
        out_shape=(jax.ShapeDtypeStruct((s, D), F32), jax.ShapeDtypeStruct((8, 128), F32),
                   jax.ShapeDtypeStruct((1, D), F32), jax.ShapeDtypeStruct((1, D), F32),
                   jax.ShapeDtypeStruct((D, D), F32), jax.ShapeDtypeStruct((D, D), F32)),
        in_specs=[tspec(D), tspec(PLE), tspec(D), vec, vec, pl.BlockSpec(memory_space=pl.ANY)],
        out_specs=(tspec(D), pl.BlockSpec((8, 128), lambda i: (0, 0)), vec, vec, mat, mat),
        scratch_shapes=[pltpu.VMEM((D, D), _MXU), pltpu.VMEM((D, PLE), _MXU), pltpu.SemaphoreType.DMA((2 * NDEV,))],
        compiler_params=_cparams(1),
    )(h2, p, target, g3, gfin, gw)


def _ffn_bwd_hidden(dh2, gf, uf, gw):
    s = dh2.shape[0]
    tm = min(TM, s)
    nt = s // tm
    half = FF // 2

    def body(dh2_ref, gf_ref, uf_ref, gw_ref, dff_ref, gwo_ref, wffo, gacc, sems):
        i = pl.program_id(0)

        @pl.when(i == 0)
        def _():
            _load_weights(gw_ref, [("wffo", wffo, D)], sems)
            gacc[...] = jnp.zeros_like(gacc)

        dm = dh2_ref[...].astype(_MXU)
        for ch in range(2):
            cs = pl.ds(ch * half, half)
            dact = _nt(dm, wffo[cs, :])
            gfv = gf_ref[:, cs].astype(F32)
            ufv = uf_ref[:, cs].astype(F32)
            sg = _sigmoid(gfv)
            silu = gfv * sg
            gacc[cs, :] += _tn((silu * ufv).astype(_MXU), dm)
            dff_ref[:, pl.ds(ch * half, half)] = (dact * ufv * (sg * (1.0 + gfv * (1.0 - sg)))).astype(_MXU)
            dff_ref[:, pl.ds(FF + ch * half, half)] = (dact * silu).astype(_MXU)

        @pl.when(i == nt - 1)
        def _():
            pltpu.sync_copy(gacc, gwo_ref)

    tspec = lambda w: pl.BlockSpec((tm, w), lambda i: (i, 0))
    return pl.pallas_call(
        body, name="ffn_bwd_hidden", grid=(nt,),
        out_shape=(jax.ShapeDtypeStruct((s, 2 * FF), _MXU), jax.ShapeDtypeStruct((FF, D), F32)),
        in_specs=[tspec(D), tspec(FF), tspec(FF), pl.BlockSpec(memory_space=pl.ANY)],
        out_specs=(tspec(2 * FF), pl.BlockSpec(memory_space=pl.ANY)),
        scratch_shapes=[pltpu.VMEM((FF, D), _MXU), pltpu.VMEM((FF, D), F32), pltpu.SemaphoreType.DMA((NDEV,))],
        compiler_params=_cparams(1),
    )(dh2, gf, uf, gw)


def _proj_norm_bwd(dy, x, dres, g, gw, slab, width, name, lhs=None):
    s = x.shape[0]
    tm = min(TM, s)
    nl = 0 if lhs is None else 1

    def body(*refs):
        dy_ref, x_ref, dr_ref, g_ref = refs[:4]
        l_refs = refs[4:4 + nl]
        gw_ref, dx_ref, dg_ref = refs[4 + nl:7 + nl]
        gl_refs = refs[7 + nl:7 + 2 * nl]
        wT, sems = refs[7 + 2 * nl:]

        @pl.when(pl.program_id(0) == 0)
        def _():
            _load_weights(gw_ref, [(slab, wT, D)], sems)
            dg_ref[...] = jnp.zeros_like(dg_ref)
            for ref in gl_refs:
                ref[...] = jnp.zeros_like(ref)

        dv = _nn(dy_ref[...], wT[...])
        xv = x_ref[...]
        inv = lax.rsqrt(jnp.mean(xv * xv, axis=-1, keepdims=True) + EPS)
        dx, dg = _rms_bwd(dv, xv * inv, inv, g_ref[...])
        dg_ref[...] += dg
        dr = dr_ref[...]
        dx_ref[...] = dr + dx
        for l_ref, gl_ref in zip(l_refs, gl_refs):
            gl_ref[...] += _tn(l_ref[...], dr.astype(_MXU))

    tspec = lambda w: pl.BlockSpec((tm, w), lambda i: (i, 0))
    vec = pl.BlockSpec((1, D), lambda i: (0, 0))
    mat = pl.BlockSpec((D, D), lambda i: (0, 0))
    return pl.pallas_call(
        body, name=name, grid=(s // tm,),
        out_shape=(jax.ShapeDtypeStruct((s, D), F32), jax.ShapeDtypeStruct((1, D), F32))
        + (jax.ShapeDtypeStruct((D, D), F32),) * nl,
        in_specs=[tspec(width), tspec(D), tspec(D), vec] + [tspec(D)] * nl + [pl.BlockSpec(memory_space=pl.ANY)],
        out_specs=(tspec(D), vec) + (mat,) * nl,
        scratch_shapes=[pltpu.VMEM((width, D), _MXU), pltpu.SemaphoreType.DMA((NDEV,))],
        compiler_params=_cparams(1),
    )(dy, x, dres, g, *([] if lhs is None else [lhs]), gw)


def _mixer_bwd(dh1, z, h, y_pool, y_lru, saved, gw, small):
    s = dh1.shape[0]
    tm = min(TM_SEQ, s)
    nt = s // tm
    (pool_w, pool_scale, conv_w, conv_b, w_rg, b_rg, w_ig, b_ig, lam, b_gate) = small

    def body(dh1_ref, z_ref, zp_ref, h_ref, hp_ref, yp_ref, yr_ref, a_ref, r_ref, ig_ref, xc_ref, gw_ref,
             pw_ref, ps_ref, cw_ref, cb_ref, wrg_ref, brg_ref, wig_ref, big_ref, lam_ref, bg_ref,
             dz_ref, dyr_ref, dyp_ref, mx_ref,
             gbg_ref, glam_ref, gbrg_ref, gbig_ref, gcb_ref, gcw_ref, gps_ref, gpw_ref, gwrg_ref, gwig_ref,
             pprojT, lru_w, wout_w, pbuf, lbuf, hbuf, qbuf, xbuf, a_s, g_s, dh_s, hk, pk, dcar, sems):
        step = pl.program_id(0)
        i = nt - 1 - step
        t0 = i * tm

        @pl.when(step == 0)
        def _():
            _load_weights(gw_ref, [("pproj", pprojT, PW), ("lru", lru_w, D), ("wout", wout_w, D)], sems)
            for ref in (gbg_ref, glam_ref, gbrg_ref, gbig_ref, gcb_ref, gcw_ref, gps_ref, gpw_ref, gwrg_ref, gwig_ref):
                ref[...] = jnp.zeros_like(ref)
            qbuf[pl.ds(tm, HALO), :] = jnp.zeros((HALO, PW), F32)
            xbuf[pl.ds(tm, 8), :] = jnp.zeros((8, D), F32)
            dcar[...] = jnp.zeros_like(dcar)

        first = i == 0
        zprev = jnp.where(first, 0.0, zp_ref[...])
        hprev = jnp.where(first, 0.0, hp_ref[...])

        d_merged = _nt(dh1_ref[...].astype(_MXU), wout_w[...])

        g0 = _sigmoid(z_ref[:, pl.ds(PW + 2 * D, D)] + bg_ref[pl.ds(0, 1), :])
        g1 = _sigmoid(z_ref[:, pl.ds(PW + 3 * D, D)] + bg_ref[pl.ds(1, 1), :])
        dz0 = d_merged * yp_ref[...].astype(F32) * g0 * (1.0 - g0)
        dz1 = d_merged * yr_ref[...].astype(F32) * g1 * (1.0 - g1)
        dz_ref[:, pl.ds(PW + 2 * D, D)] = dz0.astype(_MXU)
        dz_ref[:, pl.ds(PW + 3 * D, D)] = dz1.astype(_MXU)
        gbg_ref[pl.ds(0, 1), :] += jnp.sum(dz0, axis=0, keepdims=True)
        gbg_ref[pl.ds(1, 1), :] += jnp.sum(dz1, axis=0, keepdims=True)
        d_ypool = (d_merged * g0).astype(_MXU)
        d_ylru = (d_merged * g1).astype(_MXU)
        dyp_ref[...] = d_ypool
        dyr_ref[...] = d_ylru

        d_yl = _nt(d_ylru, lru_w[...])
        gel, dgel = _gelu_and_grad(z_ref[:, pl.ds(PW + D, D)])
        dz_ref[:, pl.ds(PW + D, D)] = (d_yl * h_ref[...] * dgel).astype(_MXU)
        g_full = d_yl * gel
        lbuf[pl.ds(0, HALO), :] = zprev[:, PW:PW + D]
        lbuf[pl.ds(HALO, tm), :] = z_ref[:, pl.ds(PW, D)]
        hbuf[pl.ds(0, 8), :] = hprev
        hbuf[pl.ds(8, tm), :] = h_ref[...]
        sp, sneg = _softplus_neg(lam_ref[...])
        start = (t0 + lax.broadcasted_iota(jnp.int32, (tm, HD), 0)) == 0
        for hh in range(HEADS):
            cs = pl.ds(hh * HD, HD)
            _to_segments(a_s, hh, a_ref[:, cs], tm)
            _to_segments(g_s, hh, g_full[:, hh * HD:(hh + 1) * HD], tm)
        _segment_scan(a_s, g_s, dh_s, hk, pk, dcar, tm, reverse=True)
        for hh in range(HEADS):
            cs = pl.ds(hh * HD, HD)
            a = a_ref[:, cs]
            r = r_ref[:, cs].astype(F32)
            ig = ig_ref[:, cs].astype(F32)
            xc = xc_ref[:, cs].astype(F32)
            a2 = a * a
            one_m = 1.0 - a2
            live = jnp.logical_and(one_m > 0.0, jnp.logical_not(start))
            inv_mult = lax.rsqrt(jnp.where(live, one_m, 1.0))
            mult = jnp.where(live, one_m * inv_mult, jnp.where(start, 1.0, 0.0))
            dh = _from_segments(dh_s, hh, tm)
            d_mult = dh * ig * xc
            d_loga = dh * hbuf[pl.ds(7, tm), cs] * a - jnp.where(live, d_mult * a2 * inv_mult, 0.0)
            glam_ref[:, cs] += jnp.sum(d_loga * (LRU_C * r) * sneg[:, hh * HD:(hh + 1) * HD], axis=0, keepdims=True)
            d_rpre = d_loga * (-LRU_C * sp[:, hh * HD:(hh + 1) * HD]) * r * (1.0 - r)
            d_igpre = dh * mult * xc * ig * (1.0 - ig)
            gbrg_ref[pl.ds(hh, 1), :] += jnp.sum(d_rpre, axis=0, keepdims=True)
            gbig_ref[pl.ds(hh, 1), :] += jnp.sum(d_igpre, axis=0, keepdims=True)
            drm = d_rpre.astype(_MXU)
            dim = d_igpre.astype(_MXU)
            xcm = xc.astype(_MXU)
            gwrg_ref[hh] += _tn(xcm, drm)
            gwig_ref[hh] += _tn(xcm, dim)
            d_xc = dh * mult * ig + _nt(drm, wrg_ref[hh]) + _nt(dim, wig_ref[hh])
            gcb_ref[:, cs] += jnp.sum(d_xc, axis=0, keepdims=True)
            for k in range(CONV):
                gcw_ref[pl.ds(k, 1), cs] += jnp.sum(d_xc * lbuf[pl.ds(HALO - (CONV - 1) + k, tm), cs], axis=0,
                                                    keepdims=True)
            xbuf[pl.ds(0, tm), cs] = d_xc
        dzl = cw_ref[pl.ds(CONV - 1, 1), :] * xbuf[pl.ds(0, tm), :]
        for k in range(CONV - 1):
            dzl = dzl + cw_ref[pl.ds(k, 1), :] * xbuf[pl.ds(CONV - 1 - k, tm), :]
        dz_ref[:, pl.ds(PW, D)] = dzl.astype(_MXU)
        xbuf[pl.ds(tm, 8), :] = xbuf[pl.ds(0, 8), :]

        d_mixed = _nn(d_ypool, pprojT[...])
        pbuf[pl.ds(0, HALO), :] = zprev[:, 0:PW]
        pbuf[pl.ds(HALO, tm), :] = z_ref[:, pl.ds(0, PW)]
        pooled, mixed_pre = _pool_tile(pbuf, t0, tm, pw_ref, ps_ref)
        mp = jnp.concatenate(mixed_pre, axis=1)
        mx_ref[...] = (mp * ps_ref[...]).astype(_MXU)
        gps_ref[...] += jnp.sum(d_mixed * mp, axis=0, keepdims=True)
        d_mp = (d_mixed * ps_ref[...]).astype(_MXU)
        t = t0 + lax.broadcasted_iota(jnp.int32, (tm, GD), 0)
        d_pooled = []
        for g, w in enumerate(WINDOWS):
            dmg = d_mp[:, g * GD:(g + 1) * GD]
            gpw_ref[g] += _tn(pooled[g].astype(_MXU), dmg)
            dp = _nt(dmg, pw_ref[g])
            d_pooled.append(dp)
            qbuf[pl.ds(0, tm), pl.ds(g * GD, GD)] = dp / jnp.minimum(t + 1, w).astype(F32)
        for g, w in enumerate(WINDOWS):
            cs = pl.ds(g * GD, GD)
            acc = qbuf[pl.ds(0, tm), cs]
            for d in range(1, w):
                acc = acc + qbuf[pl.ds(d, tm), cs]
            dz_ref[:, cs] = (acc - d_pooled[g]).astype(_MXU)
        qbuf[pl.ds(tm, HALO), :] = qbuf[pl.ds(0, HALO), :]

    rev = lambda w: pl.BlockSpec((tm, w), lambda g: (nt - 1 - g, 0))
    prev = lambda rows, w: pl.BlockSpec((rows, w), lambda g: (jnp.maximum((nt - 1 - g) * (tm // rows) - 1, 0), 0))
    full = lambda a: pl.BlockSpec(a.shape, lambda g: (0,) * a.ndim)
    tok = lambda w, dt: jax.ShapeDtypeStruct((s, w), dt)
    acc_shapes = [(2, D), (1, D), (HEADS, HD), (HEADS, HD), (1, D), (CONV, D), (1, PW), (GROUPS, GD, GD),
                  (HEADS, HD, HD), (HEADS, HD, HD)]
    acc_specs = tuple(pl.BlockSpec(sh, lambda g, n=len(sh): (0,) * n) for sh in acc_shapes)
    seg_buf = pltpu.VMEM((HEADS, 8 * _seg_layout(tm)[1], HD), F32)
    a_in, r_in, ig_in, xc_in = saved
    return pl.pallas_call(
        body, name="mixer_bwd", grid=(nt,),
        out_shape=(tok(NIN, _MXU), tok(D, _MXU), tok(D, _MXU), tok(PW, _MXU))
        + tuple(jax.ShapeDtypeStruct(sh, F32) for sh in acc_shapes),
        in_specs=[rev(D), rev(NIN), prev(HALO, NIN), rev(D), prev(8, D), rev(D), rev(D), rev(D), rev(D), rev(D), rev(D),
                  pl.BlockSpec(memory_space=pl.ANY)] + [full(a) for a in small],
        out_specs=(rev(NIN), rev(D), rev(D), rev(PW)) + acc_specs,
        scratch_shapes=[pltpu.VMEM((D, PW), _MXU), pltpu.VMEM((D, D), _MXU), pltpu.VMEM((D, D), _MXU),
                        pltpu.VMEM((tm + HALO, PW), F32), pltpu.VMEM((tm + HALO, D), F32),
                        pltpu.VMEM((tm + 8, D), F32), pltpu.VMEM((tm + HALO, PW), F32), pltpu.VMEM((tm + 8, D), F32),
                        seg_buf, seg_buf, seg_buf, pltpu.VMEM((HEADS, tm, HD), F32), pltpu.VMEM((HEADS, tm, HD), F32),
                        pltpu.VMEM((8, D), F32), pltpu.SemaphoreType.DMA((3 * NDEV,))],
        compiler_params=_cparams(1),
    )(dh1, z, z, h, h, y_pool, y_lru, a_in, r_in, ig_in, xc_in, gw, *small)


def _split3(a):
    hi = a.astype(jnp.bfloat16).astype(F32)
    mid = (a - hi).astype(jnp.bfloat16).astype(F32)
    lo = (a - hi - mid).astype(jnp.bfloat16).astype(F32)
    return jnp.stack([hi, mid, lo])


def _small_pack(parts):
    flat = jnp.concatenate([a.reshape(-1) for a in parts])
    return jnp.pad(flat, (0, NDEV * SMALL_ROWS * D - flat.shape[0])).reshape(NDEV * SMALL_ROWS, D)


def _small_unpack(packed, shapes):
    flat = packed.reshape(-1)
    out, o = [], 0
    for sh in shapes:
        n = math.prod(sh)
        out.append(flat[o:o + n].reshape(sh))
        o += n
    return out


def kernel(x, p, norm1_g, w_in, b_gate, pool_w, pool_scale, pool_proj, conv_w, conv_b, w_rg, b_rg, w_ig, b_ig, lru_lambda, lru_proj, w_out, norm2_g, w_ffn_in, w_ffn_out, ple_norm_g, w_ple_gate, w_ple_proj, final_g, loss_target, m_norm1_g, m_w_in, m_b_gate, m_pool_w, m_pool_scale, m_pool_proj, m_conv_w, m_conv_b, m_w_rg, m_b_rg, m_w_ig, m_b_ig, m_lru_lambda, m_lru_proj, m_w_out, m_norm2_g, m_w_ffn_in, m_w_ffn_out, m_ple_norm_g, m_w_ple_gate, m_w_ple_proj, m_final_g, v_norm1_g, v_w_in, v_b_gate, v_pool_w, v_pool_scale, v_pool_proj, v_conv_w, v_conv_b, v_w_rg, v_b_rg, v_w_ig, v_b_ig, v_lru_lambda, v_lru_proj, v_w_out, v_norm2_g, v_w_ffn_in, v_w_ffn_out, v_ple_norm_g, v_w_ple_gate, v_w_ple_proj, v_final_g):
    axes = ("x", "y", "c")
    me = 4 * lax.axis_index("x") + 2 * lax.axis_index("y") + lax.axis_index("c")
    x2 = x[0]
    p2 = p[0, 0]
    tgt = loss_target[0]

    n_small = (CONV + 2) * 128
    small_terms = _split3(jnp.concatenate([conv_w[0].reshape(-1), b_gate[0].reshape(-1)]))
    small_rows = jnp.pad(small_terms, ((0, 16 - 3), (0, D - n_small)))
    own_first = jnp.concatenate([w_in[0].T.astype(_MXU), small_rows.astype(_MXU)], axis=0)
    own_second = jnp.concatenate([
        w_ffn_in[0].T.astype(_MXU),
        jnp.pad(pool_proj[0].T, ((0, 0), (0, D - PW))).astype(_MXU),
        jnp.pad(w_ple_proj[0].T, ((0, 0), (0, D - PLE))).astype(_MXU),
        lru_proj[0].astype(_MXU), w_out[0].astype(_MXU), w_ffn_out[0].astype(_MXU), w_ple_gate[0].astype(_MXU),
    ], axis=0)
    gw_first = _all_gather_weights(own_first)
    off = W_OFF["f32s"][0]
    st = gw_first[:, off:off + 3, :n_small].astype(F32)
    sf = st[:, 0] + st[:, 1] + st[:, 2]
    conv_w_full = sf[:, :CONV * 128].reshape(NDEV, CONV, 128).transpose(1, 0, 2).reshape(CONV, D)
    b_gate_full = sf[:, CONV * 128:].reshape(NDEV, 2, 128).transpose(1, 0, 2).reshape(2, D)

    small = (pool_w[0].astype(_MXU), pool_scale, conv_w_full, conv_b, w_rg[0].astype(_MXU), b_rg[0],
             w_ig[0].astype(_MXU), b_ig[0], lru_lambda, b_gate_full)

    u, z, gw = _inproj_fwd(x2, norm1_g, gw_first, own_second)
    h, yl, merged, y_pool, y_lru, h1, *saved = _mixer_fwd(z, x2, gw, small)
    v, gf, uf, h2 = _ffn_fwd(h1, norm2_g, gw)

    dh2, loss_blk, g_ple_norm, g_final, part_wpg, part_ple = _ple_loss_fwd_bwd(h2, p2, tgt, ple_norm_g,
                                                                               final_g.reshape(1, D), gw)
    dff, part_wffo = _ffn_bwd_hidden(dh2, gf, uf, gw)
    dh1, g_norm2 = _proj_norm_bwd(dff, h1, dh2, norm2_g, gw, "wffn", 2 * FF, "ffn_bwd_in")
    (dz, d_ylru, d_ypool, mixed, g_bgate, g_lam, g_brg, g_big, g_convb, g_convw, g_pscale, g_poolw, g_wrg,
     g_wig) = _mixer_bwd(dh1, z, h, y_pool, y_lru, saved, gw, small)
    grad_x, g_norm1, part_wout = _proj_norm_bwd(dz, x2, dh1, norm1_g, gw_first, "win", NIN, "inproj_bwd", lhs=merged)

    small_shapes = [(1, D), (GROUPS, GD, GD), (1, PW), (1, D), (HEADS, HD, HD), (HEADS, HD), (HEADS, HD, HD),
                    (HEADS, HD), (1, D), (1, D), (1, D), (1, D), (2, D), (CONV, D), (1, 1)]
    small_part = _small_pack([g_norm1, g_poolw, g_pscale, g_convb, g_wrg, g_brg, g_wig, g_big, g_lam, g_norm2,
                              g_ple_norm, g_final, g_bgate, g_convw, loss_blk[0:1, 0:1]])
    riders = [_grad_matmul(yl, d_ylru, "grad_lru_proj"), part_wout, _grad_matmul(d_ypool, mixed, "grad_pool_proj")]
    rs_wffn = _grad_matmul_rs(dff, v, "grad_w_ffn_in", 704, extras=[part_wffo, part_wpg, part_ple], narrow=_MXU)
    rs_win = _grad_matmul_rs(dz, u, "grad_w_in", 576, extras=riders + [small_part], narrow=_MXU, tail=SMALL_ROWS)

    def reduced(parts, name):
        return [_sum_arrays([t_own, landed[0], landed[1], landed[2]], "rs_sum_" + name + str(n))
                for n, (t_own, landed) in enumerate(parts)]

    red_wffn, = reduced(rs_wffn, "wffn")
    red_win, red_small = reduced(rs_win, "win")
    g_w_in = red_win[:576].T
    g_w_ffn_in = red_wffn[:704].T
    g_w_ffn_out = red_wffn[704:1056]
    g_w_ple_gate = red_wffn[1056:1184]
    g_w_ple_proj = red_wffn[1184:1312, :PLE].T
    g_lru_proj, g_w_out = red_win[576:704], red_win[704:832]
    g_pool_proj = red_win[832:960, :PW].T
    small_red = _all_gather_small(red_small)
    (gs_norm1, gs_poolw, gs_pscale, gs_convb, gs_wrg, gs_brg, gs_wig, gs_big, gs_lam, gs_norm2, gs_ple_norm,
     gs_final, gs_bgate, gs_convw, loss_sum) = _small_unpack(small_red, small_shapes)
    loss = loss_sum[0, 0]
    g_b_gate = lax.dynamic_slice_in_dim(gs_bgate, me * 128, 128, axis=1)
    g_conv_w = lax.dynamic_slice_in_dim(gs_convw, me * 128, 128, axis=1)

    grads = {
        "norm1_g": gs_norm1, "w_in": g_w_in[None], "b_gate": g_b_gate[None], "pool_w": gs_poolw[None],
        "pool_scale": gs_pscale, "pool_proj": g_pool_proj[None], "conv_w": g_conv_w[None], "conv_b": gs_convb,
        "w_rg": gs_wrg[None], "b_rg": gs_brg[None], "w_ig": gs_wig[None], "b_ig": gs_big[None], "lru_lambda": gs_lam,
        "lru_proj": g_lru_proj[None], "w_out": g_w_out[None], "norm2_g": gs_norm2, "w_ffn_in": g_w_ffn_in[None],
        "w_ffn_out": g_w_ffn_out[None], "ple_norm_g": gs_ple_norm, "w_ple_gate": g_w_ple_gate[None],
        "w_ple_proj": g_w_ple_proj[None], "final_g": gs_final.reshape(D),
    }
    weights = dict(norm1_g=norm1_g, w_in=w_in, b_gate=b_gate, pool_w=pool_w, pool_scale=pool_scale, pool_proj=pool_proj,
                   conv_w=conv_w, conv_b=conv_b, w_rg=w_rg, b_rg=b_rg, w_ig=w_ig, b_ig=b_ig, lru_lambda=lru_lambda,
                   lru_proj=lru_proj, w_out=w_out, norm2_g=norm2_g, w_ffn_in=w_ffn_in, w_ffn_out=w_ffn_out,
                   ple_norm_g=ple_norm_g, w_ple_gate=w_ple_gate, w_ple_proj=w_ple_proj, final_g=final_g)
    moments_m = dict(norm1_g=m_norm1_g, w_in=m_w_in, b_gate=m_b_gate, pool_w=m_pool_w, pool_scale=m_pool_scale,
                     pool_proj=m_pool_proj, conv_w=m_conv_w, conv_b=m_conv_b, w_rg=m_w_rg, b_rg=m_b_rg, w_ig=m_w_ig,
                     b_ig=m_b_ig, lru_lambda=m_lru_lambda, lru_proj=m_lru_proj, w_out=m_w_out, norm2_g=m_norm2_g,
                     w_ffn_in=m_w_ffn_in, w_ffn_out=m_w_ffn_out, ple_norm_g=m_ple_norm_g, w_ple_gate=m_w_ple_gate,
                     w_ple_proj=m_w_ple_proj, final_g=m_final_g)
    moments_v = dict(norm1_g=v_norm1_g, w_in=v_w_in, b_gate=v_b_gate, pool_w=v_pool_w, pool_scale=v_pool_scale,
                     pool_proj=v_pool_proj, conv_w=v_conv_w, conv_b=v_conv_b, w_rg=v_w_rg, b_rg=v_b_rg, w_ig=v_w_ig,
                     b_ig=v_b_ig, lru_lambda=v_lru_lambda, lru_proj=v_lru_proj, w_out=v_w_out, norm2_g=v_norm2_g,
                     w_ffn_in=v_w_ffn_in, w_ffn_out=v_w_ffn_out, ple_norm_g=v_ple_norm_g, w_ple_gate=v_w_ple_gate,
                     w_ple_proj=v_w_ple_proj, final_g=v_final_g)
    names = list(weights)
    big = ("w_in", "w_ffn_in", "w_ffn_out", "lru_proj", "w_out", "w_ple_gate", "pool_proj", "w_ple_proj")
    slab_space = {"w_in": red_win[:576], "w_ffn_in": red_wffn[:704]}
    delta, new_m, new_v = {}, {}, {}
    for n in big:
        sh = weights[n].shape
        if n in slab_space:
            as2d = lambda a: a[0].T
            back = lambda a: a.T[None]
            g2d = slab_space[n]
        else:
            as2d = lambda a: a.reshape(sh[-2], sh[-1])
            back = lambda a: a.reshape(sh)
            g2d = as2d(grads[n])
        d_, m_, v_ = _adamw(as2d(weights[n]), g2d, as2d(moments_m[n]), as2d(moments_v[n]), "adamw_" + n)
        delta[n], new_m[n], new_v[n] = back(d_), back(m_), back(v_)
    rest = [n for n in names if n not in big]
    rest_shapes = [weights[n].shape for n in rest]
    packed = [_small_pack([src[n] for n in rest]) for src in (weights, grads, moments_m, moments_v)]
    d_, m_, v_ = _adamw(*packed, "adamw_small")
    for n, a, b_, c_ in zip(rest, _small_unpack(d_, rest_shapes), _small_unpack(m_, rest_shapes),
                            _small_unpack(v_, rest_shapes)):
        delta[n], new_m[n], new_v[n] = a, b_, c_

    return (loss, grad_x[None], *[grads[n] for n in names], *[delta[n] for n in names],
            *[new_m[n] for n in names], *[new_v[n] for n in names])
```

```python
import functools
import math

import jax
import jax.numpy as jnp
from jax import lax
from jax.experimental import pallas as pl
from jax.experimental.pallas import tpu as pltpu

F32 = jnp.float32
D = 1024
NIN = 4608
PW = 512
FF = 2816
PLE = 256
HEADS, HD = 8, 128
GROUPS, GD = 4, 128
WINDOWS = (2, 4, 8, 16)
HALO = 16
CONV = 4
EPS = 1e-6
LRU_C = 8.0
NDEV = 8
MESH = pl.DeviceIdType.MESH

ADAM_LR, ADAM_B1, ADAM_B2, ADAM_EPS, ADAM_WD, ADAM_STEP = 0.001, 0.9, 0.999, 1e-08, 0.01, 10

_MXU = jnp.bfloat16
TM = 512
TM_SEQ = 256
VMEM_LIMIT = 56 * 1024 * 1024
W_FIRST = (("win", 576), ("f32s", 16))
W_SECOND = (("wffn", 704), ("pproj", 128), ("ple", 128), ("lru", 128), ("wout", 128), ("wffo", 352), ("wpg", 128))
W_OFF = {}
for _slabs in (W_FIRST, W_SECOND):
    _o = 0
    for _n, _r in _slabs:
        W_OFF[_n] = (_o, _r)
        _o += _r
SMALL_ROWS = 48


def _cparams(n_axes=1, vmem=VMEM_LIMIT):
    return pltpu.CompilerParams(dimension_semantics=("arbitrary",) * n_axes, vmem_limit_bytes=vmem)


def _my_pos():
    return lax.axis_index("x"), lax.axis_index("y"), lax.axis_index("c")


def _nt(a, b):
    return lax.dot_general(a, b, (((1,), (1,)), ((), ())), preferred_element_type=F32)


def _nn(a, b):
    return lax.dot_general(a, b, (((1,), (0,)), ((), ())), preferred_element_type=F32)


def _tn(a, b):
    return lax.dot_general(a, b, (((0,), (0,)), ((), ())), preferred_element_type=F32)


def _sigmoid(x):
    return 0.5 * jnp.tanh(0.5 * x) + 0.5


_GELU_K = math.sqrt(2.0 / math.pi)


def _gelu_and_grad(x):
    x2 = x * x
    inner = _GELU_K * (x + 0.044715 * x2 * x)
    t = jnp.tanh(inner)
    g = 0.5 * x * (1.0 + t)
    dg = 0.5 * (1.0 + t) + 0.5 * x * (1.0 - t * t) * _GELU_K * (1.0 + 3.0 * 0.044715 * x2)
    return g, dg


def _softplus_neg(lam):
    x = -lam
    t = jnp.exp(-jnp.abs(x))
    u = 1.0 + t
    l1p = jnp.where(u == 1.0, t, jnp.log(u) * t / (u - 1.0))
    return jnp.maximum(x, 0.0) + l1p, _sigmoid(x)


def _start_slab_loads(g_ref, name, dst_ref, sems, base, width=D):
    off, rows = W_OFF[name]
    copies = []
    for k in range(NDEV):
        if width == D:
            src = g_ref.at[k, pl.ds(off, rows), :]
        else:
            src = g_ref.at[k, pl.ds(off, rows), pl.ds(0, width)]
        cp = pltpu.make_async_copy(src, dst_ref.at[pl.ds(k * rows, rows), :], sems.at[base + k])
        cp.start()
        copies.append(cp)
    return copies


def _load_weights(g_ref, items, sems):
    copies = []
    for n, (name, dst, width) in enumerate(items):
        copies += _start_slab_loads(g_ref, name, dst, sems, n * NDEV, width)
    for cp in copies:
        cp.wait()


class _Gather:
    def __init__(self, own_ref, out_ref, stage, send_sems, recv_sems, local_sem):
        x, y, c = _my_pos()
        self.c = c
        self.me, self.sibling = (x, y, c), (x, y, 1 - c)
        self.chips = [(1 - x, y), (x, 1 - y), (1 - x, 1 - y)]
        self.own_ref, self.out_ref, self.stage = own_ref, out_ref, stage
        self.send_sems, self.recv_sems = send_sems, recv_sems
        self.mine = pltpu.make_async_copy(stage, self.slab(*self.me), local_sem)
        self.first = [self.copy(0, self.me, self.sibling, src=stage)] + [
            self.copy(1 + j, self.me, (*chip, c), src=stage) for j, chip in enumerate(self.chips)]
        self.passed = [self.copy(4 + j, (*chip, c), self.sibling) for j, chip in enumerate(self.chips)]

    def slab(self, px, py, pc):
        return self.out_ref.at[4 * px + 2 * py + pc]

    def copy(self, k, block, to, src=None):
        return pltpu.make_async_remote_copy(
            src_ref=self.slab(*block) if src is None else src, dst_ref=self.slab(*block),
            send_sem=self.send_sems.at[k], recv_sem=self.recv_sems.at[k], device_id=to, device_id_type=MESH)

    def send_mine(self, far=True):
        pltpu.sync_copy(self.own_ref, self.stage)
        self.mine.start()
        for cp in self.first[:3]:
            cp.start()
        if far:
            self.send_far()

    def send_far(self):
        self.first[3].start()

    def pass_on(self, js):
        for j in js:
            self.copy(1 + j, (*self.chips[j], self.c), self.me).wait_recv()
            self.passed[j].start()

    def wait_sibling(self):
        self.copy(0, self.sibling, self.me).wait_recv()

    def wait_passed(self, js):
        for j in js:
            self.copy(4 + j, (*self.chips[j], 1 - self.c), self.me).wait_recv()

    def finish_sends(self):
        for cp in self.first + self.passed:
            cp.wait_send()
        self.mine.wait()


def _all_gather_small(piece):
    rows = piece.shape[0]

    def body(p_ref, out_ref, send_sems, recv_sems, local_sem):
        x, y, c = _my_pos()
        me = 4 * x + 2 * y + c
        mine = pltpu.make_async_copy(p_ref, out_ref.at[pl.ds(pl.multiple_of(me * rows, 8), rows), :], local_sem)
        mine.start()
        sends = []
        peers = []
        for r in range(1, NDEV):
            px = 1 - x if (r >> 2) & 1 else x
            py = 1 - y if (r >> 1) & 1 else y
            pc = 1 - c if r & 1 else c
            peers.append((px, py, pc))
            cp = pltpu.make_async_remote_copy(
                src_ref=p_ref, dst_ref=out_ref.at[pl.ds(pl.multiple_of(me * rows, 8), rows), :],
                send_sem=send_sems.at[r - 1], recv_sem=recv_sems.at[r - 1], device_id=(px, py, pc),
                device_id_type=MESH)
            cp.start()
            sends.append(cp)
        for r, (px, py, pc) in enumerate(peers):
            them = 4 * px + 2 * py + pc
            pltpu.make_async_remote_copy(
                src_ref=p_ref, dst_ref=out_ref.at[pl.ds(pl.multiple_of(them * rows, 8), rows), :],
                send_sem=send_sems.at[r], recv_sem=recv_sems.at[r], device_id=(px, py, pc),
                device_id_type=MESH).wait_recv()
        for cp in sends:
            cp.wait_send()
        mine.wait()

    return pl.pallas_call(
        body, name="ag_small",
        out_shape=jax.ShapeDtypeStruct((NDEV * rows, piece.shape[1]), piece.dtype),
        in_specs=[pl.BlockSpec(memory_space=pltpu.VMEM)],
        out_specs=pl.BlockSpec(memory_space=pl.ANY),
        scratch_shapes=[pltpu.SemaphoreType.DMA((7,)), pltpu.SemaphoreType.DMA((7,)), pltpu.SemaphoreType.DMA],
    )(piece)


def _row_block(rows, target=512, mult=8):
    b = min(rows, target) // mult * mult
    while rows % b:
        b -= mult
    return b


def _sum_arrays(arrs, name, narrow=None, target=704):
    rows, cols = arrs[0].shape
    br = _row_block(rows, target, 16)
    n = len(arrs)

    def body(*refs):
        acc = refs[0][...].astype(F32)
        for r in refs[1:n]:
            acc = acc + r[...].astype(F32)
        refs[n][...] = acc
        if narrow is not None:
            refs[n + 1][...] = acc.astype(narrow)

    spec = pl.BlockSpec((br, cols), lambda i: (i, 0))
    shape = jax.ShapeDtypeStruct((rows, cols), F32)
    if narrow is None:
        out_shape, out_specs = shape, spec
    else:
        out_shape, out_specs = (shape, jax.ShapeDtypeStruct((rows, cols), narrow)), (spec, spec)
    return pl.pallas_call(
        body, name=name, grid=(rows // br,), out_shape=out_shape,
        in_specs=[spec] * n, out_specs=out_specs, compiler_params=_cparams(1),
    )(*arrs)


def _adamw(w, g, m, v, name):
    rows, cols = w.shape
    br = _row_block(rows, 256)

    def body(w_ref, g_ref, m_ref, v_ref, d_ref, nm_ref, nv_ref):
        g_ = g_ref[...]
        m_ = ADAM_B1 * m_ref[...] + (1.0 - ADAM_B1) * g_
        v_ = ADAM_B2 * v_ref[...] + (1.0 - ADAM_B2) * (g_ * g_)
        m_hat = m_ / (1.0 - ADAM_B1 ** ADAM_STEP)
        v_hat = v_ / (1.0 - ADAM_B2 ** ADAM_STEP)
        d_ref[...] = -ADAM_LR * (m_hat / (jnp.sqrt(v_hat) + ADAM_EPS) + ADAM_WD * w_ref[...])
        nm_ref[...] = m_
        nv_ref[...] = v_

    spec = pl.BlockSpec((br, cols), lambda i: (i, 0))
    shape = jax.ShapeDtypeStruct((rows, cols), F32)
    return pl.pallas_call(
        body, name=name, grid=(rows // br,), out_shape=(shape, shape, shape),
        in_specs=[spec] * 4, out_specs=(spec, spec, spec), compiler_params=_cparams(1),
    )(w, g, m, v)


_CHIP_FLIPS = (2, 3, 1, 0)


def _grad_matmul(lhs, rhs, name):
    s, r = lhs.shape
    k = rhs.shape[1]
    tm = min(TM, s)

    def body(l_ref, r_ref, o_ref):
        @pl.when(pl.program_id(0) == 0)
        def _():
            o_ref[...] = jnp.zeros_like(o_ref)

        o_ref[:, pl.ds(0, k)] += _tn(l_ref[...].astype(_MXU), r_ref[...].astype(_MXU))

    return pl.pallas_call(
        body, name=name, grid=(s // tm,),
        out_shape=jax.ShapeDtypeStruct((r, D), F32),
        in_specs=[pl.BlockSpec((tm, r), lambda i: (i, 0)), pl.BlockSpec((tm, k), lambda i: (i, 0))],
        out_specs=pl.BlockSpec((r, D), lambda i: (0, 0)),
        compiler_params=_cparams(1),
    )(lhs, rhs)


def _grad_matmul_rs(lhs, rhs, name, rows, extras=(), narrow=None, tail=0):
    s, r8 = lhs.shape
    k = rhs.shape[1]
    tm = min(TM, s)
    nt = s // tm
    cpb = 1
    nblk = 4 // cpb
    nx = len(extras)
    ers = [e.shape[0] // NDEV for e in extras]
    er = sum(ers)
    srows = rows + er
    brows = 2 * cpb * srows
    groups = [(0, srows - tail, F32 if narrow is None else narrow)] + ([(srows - tail, tail, F32)] if tail else [])
    ng = len(groups)
    mid = min(nt - 1, max(1, nt // 4))

    def flip_of(p):
        return jnp.where(p == 0, 2, jnp.where(p == 1, 3, jnp.where(p == 2, 1, 0)))

    def block_col(b):
        x, y, _ = _my_pos()
        return (2 * x + y) ^ flip_of(b)

    def body(*refs):
        l_ref, r_ref = refs[:2]
        x_refs = refs[2:2 + nx]
        rest = refs[2 + nx:]
        town_ref = rest[0]
        lici_refs = rest[1:1 + ng]
        acc, stage = rest[1 + ng:3 + ng]
        send_bufs = rest[3 + ng:3 + 2 * ng]
        dsend, drecv, isend, irecv, xsem = rest[3 + 2 * ng:]
        b = pl.program_id(0)
        i = pl.program_id(1)
        x, y, c = _my_pos()
        mine = 2 * x + y
        sibling = (x, y, 1 - c)

        def chip_at(p):
            return mine ^ _CHIP_FLIPS[p]

        def slab_rows(p, parity):
            within = 0 if cpb == 1 else (chip_at(p) & 1) * 2
            return pl.ds(pl.multiple_of((within + parity) * srows, 8), srows)

        def push(p, slot):
            return pltpu.make_async_remote_copy(
                src_ref=acc.at[slot, slab_rows(p, 1 - c), :], dst_ref=stage.at[p % 2],
                send_sem=dsend.at[p], recv_sem=drecv.at[p], device_id=sibling, device_id_type=MESH)

        def ici(p):
            ch = chip_at(p)
            return [pltpu.make_async_remote_copy(
                src_ref=send_bufs[g].at[p % 2], dst_ref=lici_refs[g].at[p], send_sem=isend.at[3 * g + p],
                recv_sem=irecv.at[3 * g + p], device_id=(ch >> 1, ch & 1, c), device_id_type=MESH) for g in range(ng)]

        def extra_loads(p, slot):
            copies = []
            within = 0 if cpb == 1 else (chip_at(p) & 1) * 2
            for parity in range(2):
                off = rows
                for n, (x_ref, e) in enumerate(zip(x_refs, ers)):
                    src = x_ref.at[pl.ds(pl.multiple_of((2 * chip_at(p) + parity) * e, 8), e), :]
                    dst = acc.at[slot, pl.ds(pl.multiple_of((within + parity) * srows + off, 8), e), :]
                    copies.append(pltpu.make_async_copy(src, dst, xsem.at[(p * 2 + parity) * nx + n]))
                    off += e
            return copies

        def combine(p, slot):
            push(p, slot).wait_recv()
            total = acc[slot, slab_rows(p, c), :] + stage[p % 2]
            if p == 3:
                stage[p % 2] = total
                pltpu.sync_copy(stage.at[p % 2], town_ref)
            else:
                if p == 2:
                    for cp in ici(0):
                        cp.wait_send()
                for g, (r0, n, dt) in enumerate(groups):
                    send_bufs[g][p % 2] = total[r0:r0 + n, :].astype(dt)
                for cp in ici(p):
                    cp.start()

        for bb in range(nblk):
            slot = bb % 2
            positions = list(range(bb * cpb, (bb + 1) * cpb))

            @pl.when(jnp.logical_and(b == bb, i == 0))
            def _(bb=bb, slot=slot, positions=positions):
                if bb >= 2:
                    for p in range((bb - 2) * cpb, (bb - 1) * cpb):
                        push(p, slot).wait_send()
                for q in range(2 * cpb):
                    acc[slot, pl.ds(q * srows, rows), :] = jnp.zeros((rows, D), F32)
                for p in positions:
                    for cp in extra_loads(p, slot):
                        cp.start()

            if bb >= 1:
                @pl.when(jnp.logical_and(b == bb, i == mid))
                def _(bb=bb):
                    for p in range((bb - 1) * cpb, bb * cpb):
                        combine(p, (bb - 1) % 2)

        res = _tn(l_ref[...].astype(_MXU), r_ref[...].astype(_MXU))
        slot_now = b % 2
        for q in range(2 * cpb):
            acc[slot_now, pl.ds(q * srows, rows), pl.ds(0, k)] += res[q * rows:(q + 1) * rows, :]

        for bb in range(nblk):
            slot = bb % 2
            positions = list(range(bb * cpb, (bb + 1) * cpb))

            @pl.when(jnp.logical_and(b == bb, i == nt - 1))
            def _(bb=bb, slot=slot, positions=positions):
                for p in positions:
                    for cp in extra_loads(p, slot):
                        cp.wait()
                for p in positions:
                    push(p, slot).start()
                if bb == nblk - 1:
                    for p in positions:
                        combine(p, slot)
                    for p in range(max(0, (nblk - 2) * cpb), 4):
                        push(p, slot).wait_send()
                    for p in range(1, 3):
                        for cp in ici(p):
                            cp.wait_send()
                    for p in range(3):
                        for cp in ici(p):
                            cp.wait_recv()

    in_specs = [pl.BlockSpec((tm, 2 * cpb * rows), lambda b, i: (i, block_col(b))),
                pl.BlockSpec((tm, k), lambda b, i: (i, 0))]
    any_spec = pl.BlockSpec(memory_space=pl.ANY)
    in_specs += [any_spec] * nx
    args = [lhs, rhs, *extras]
    outs = pl.pallas_call(
        body, name=name, grid=(nblk, nt),
        out_shape=(jax.ShapeDtypeStruct((srows, D), F32),)
        + tuple(jax.ShapeDtypeStruct((3, n, D), dt) for _, n, dt in groups),
        in_specs=in_specs, out_specs=(any_spec,) * (1 + ng),
        scratch_shapes=[pltpu.VMEM((2, brows, D), F32), pltpu.VMEM((2, srows, D), F32)]
        + [pltpu.VMEM((2, n, D), dt) for _, n, dt in groups]
        + [pltpu.SemaphoreType.DMA((4,)), pltpu.SemaphoreType.DMA((4,)), pltpu.SemaphoreType.DMA((3 * ng,)),
           pltpu.SemaphoreType.DMA((3 * ng,)), pltpu.SemaphoreType.DMA((max(1, 8 * nx),))],
        compiler_params=_cparams(2),
    )(*args)
    t_own = outs[0]
    return [(t_own[r0:r0 + n], landed) for (r0, n, _), landed in zip(groups, outs[1:])]


def _inproj_fwd(x, g1, own_first, own_second):
    s = x.shape[0]
    tm = min(TM, s)
    nt = s // tm
    rows1, rows2 = own_first.shape[0], own_second.shape[0]
    wrows = W_OFF["win"][1]
    cw = 2 * wrows

    def chip_col(b):
        px, py, _ = _my_pos()
        return (2 * px + py) ^ jnp.where(b == 0, 0, jnp.where(b == 1, 2, jnp.where(b == 2, 1, 3)))

    def body(x_ref, g1_ref, own1_ref, own2_ref, u_ref, z_ref, gw1_ref, gw2_ref, w_vmem, stage1, stage2, sems,
             send1, recv1, local1, send2, recv2, local2):
        b = pl.program_id(0)
        i = pl.program_id(1)
        ga = _Gather(own1_ref, gw1_ref, stage1, send1, recv1, local1)
        gb = _Gather(own2_ref, gw2_ref, stage2, send2, recv2, local2)
        c = ga.c

        def load_chip(px, py, own_too):
            copies = []
            for pc in range(2):
                dst = w_vmem.at[pl.ds(pc * wrows, wrows), :]
                copies.append(pltpu.make_async_copy(gw1_ref.at[4 * px + 2 * py + pc, pl.ds(0, wrows), :], dst,
                                                    sems.at[pc]))
            if own_too:
                mine_dst = w_vmem.at[pl.ds(pl.multiple_of(c * wrows, 16), wrows), :]
                copies[0] = pltpu.make_async_copy(own1_ref.at[pl.ds(0, wrows), :], mine_dst, sems.at[0])
                theirs_dst = w_vmem.at[pl.ds(pl.multiple_of((1 - c) * wrows, 16), wrows), :]
                copies[1] = pltpu.make_async_copy(gw1_ref.at[4 * px + 2 * py + 1 - c, pl.ds(0, wrows), :], theirs_dst,
                                                  sems.at[1])
            for cp in copies:
                cp.start()
            for cp in copies:
                cp.wait()

        @pl.when(jnp.logical_and(b == 0, i == 0))
        def _():
            ga.send_mine(far=False)
            ga.wait_sibling()
            load_chip(ga.me[0], ga.me[1], True)

        @pl.when(jnp.logical_and(b == 0, i == nt // 2))
        def _():
            ga.send_far()
            gb.send_mine()

        @pl.when(jnp.logical_and(b == 0, i == (3 * nt) // 4))
        def _():
            ga.pass_on((0, 1))

        @pl.when(jnp.logical_and(b == 1, i == (3 * nt) // 4))
        def _():
            ga.pass_on((2,))

        for j in range(3):
            @pl.when(jnp.logical_and(b == j + 1, i == 0))
            def _(j=j):
                ga.wait_passed((j,))
                load_chip(ga.chips[j][0], ga.chips[j][1], False)

        @pl.when(jnp.logical_and(b == 2, i == nt // 2))
        def _():
            gb.pass_on((0, 1))

        @pl.when(jnp.logical_and(b == 3, i == (3 * nt) // 4))
        def _():
            gb.pass_on((2,))

        xv = x_ref[...]
        inv = lax.rsqrt(jnp.mean(xv * xv, axis=-1, keepdims=True) + EPS)
        u = (xv * inv * g1_ref[...]).astype(_MXU)

        @pl.when(b == 0)
        def _():
            u_ref[...] = u

        z_ref[...] = _nt(u, w_vmem[...])

        @pl.when(jnp.logical_and(b == 3, i == nt - 1))
        def _():
            ga.finish_sends()
            gb.wait_sibling()
            gb.wait_passed((0, 1, 2))
            gb.finish_sends()

    any_spec = pl.BlockSpec(memory_space=pl.ANY)
    dma7 = pltpu.SemaphoreType.DMA((7,))
    return pl.pallas_call(
        body, name="inproj_fwd", grid=(4, nt),
        out_shape=(jax.ShapeDtypeStruct((s, D), _MXU), jax.ShapeDtypeStruct((s, NIN), F32),
                   jax.ShapeDtypeStruct((NDEV, rows1, D), own_first.dtype),
                   jax.ShapeDtypeStruct((NDEV, rows2, D), own_second.dtype)),
        in_specs=[pl.BlockSpec((tm, D), lambda b, i: (i, 0)), pl.BlockSpec((1, D), lambda b, i: (0, 0)),
                  any_spec, any_spec],
        out_specs=(pl.BlockSpec((tm, D), lambda b, i: (jnp.where(b == 0, i, nt - 1), 0)),
                   pl.BlockSpec((tm, cw), lambda b, i: (i, chip_col(b))), any_spec, any_spec),
        scratch_shapes=[pltpu.VMEM((cw, D), _MXU), pltpu.VMEM((rows1, D), own_first.dtype),
                        pltpu.VMEM((rows2, D), own_second.dtype), pltpu.SemaphoreType.DMA((2,)),
                        dma7, dma7, pltpu.SemaphoreType.DMA, dma7, dma7, pltpu.SemaphoreType.DMA],
        compiler_params=_cparams(2),
    )(x, g1, own_first, own_second)


def _pool_tile(pbuf, t0, tm, pw_ref, scale_ref):
    t = t0 + lax.broadcasted_iota(jnp.int32, (tm, GD), 0)
    pooled, mixed_pre = [], []
    for g, w in enumerate(WINDOWS):
        cs = pl.ds(g * GD, GD)
        cur = pbuf[pl.ds(HALO, tm), cs]
        acc = cur
        for d in range(1, w):
            acc = acc + pbuf[pl.ds(HALO - d, tm), cs]
        cnt = jnp.minimum(t + 1, w).astype(F32)
        pg = acc / cnt - cur
        pooled.append(pg)
        mixed_pre.append(_nn(pg.astype(_MXU), pw_ref[g]))
    return pooled, mixed_pre


def _lru_gates_head(hh, lbuf, start, tm, cw_ref, cb_ref, wrg_ref, brg_ref, wig_ref, big_ref, sp):
    cs = pl.ds(hh * HD, HD)
    xc = cb_ref[:, cs] + cw_ref[pl.ds(CONV - 1, 1), cs] * lbuf[pl.ds(HALO, tm), cs]
    for k in range(CONV - 1):
        xc = xc + cw_ref[pl.ds(k, 1), cs] * lbuf[pl.ds(HALO - (CONV - 1) + k, tm), cs]
    xcm = xc.astype(_MXU)
    r = _sigmoid(_nn(xcm, wrg_ref[hh]) + brg_ref[pl.ds(hh, 1), :])
    ig = _sigmoid(_nn(xcm, wig_ref[hh]) + big_ref[pl.ds(hh, 1), :])
    a = jnp.exp(-LRU_C * r * sp[:, hh * HD:(hh + 1) * HD])
    one_m = 1.0 - a * a
    live = jnp.logical_and(one_m > 0.0, jnp.logical_not(start))
    inv_mult = lax.rsqrt(jnp.where(live, one_m, 1.0))
    mult = jnp.where(live, one_m * inv_mult, jnp.where(start, 1.0, 0.0))
    return xc, r, ig, a, live, inv_mult, mult


def _seg_layout(tm):
    seg = tm // 8
    return seg, seg + 8


def _to_segments(dst_ref, hh, val, tm):
    seg, pitch = _seg_layout(tm)
    for s in range(8):
        dst_ref[hh, pl.ds(s * pitch, seg), :] = val[s * seg:(s + 1) * seg, :]


def _from_segments(src_ref, hh, tm):
    seg, pitch = _seg_layout(tm)
    return jnp.concatenate([src_ref[hh, pl.ds(s * pitch, seg), :] for s in range(8)], axis=0)


def _segment_scan(a_ref, b_ref, out_ref, hk, pk, carry_ref, tm, reverse):
    seg, pitch = _seg_layout(tm)
    row = lax.broadcasted_iota(jnp.int32, (8, HD), 0)
    order = range(seg - 1, -1, -1) if reverse else range(seg)
    for hh in range(HEADS):
        cs = pl.ds(hh * HD, HD)
        if reverse:
            a0 = a_ref[hh, pl.ds(0, 8, stride=pitch), :]
            a_wrap = jnp.where(row <= 6, pltpu.roll(a0, 7, 0), 1.0)
        hv = jnp.zeros((8, HD), F32)
        pv = jnp.ones((8, HD), F32)
        for k in order:
            if not reverse:
                av = a_ref[hh, pl.ds(k, 8, stride=pitch), :]
            elif k + 1 < seg:
                av = a_ref[hh, pl.ds(k + 1, 8, stride=pitch), :]
            else:
                av = a_wrap
            hv = av * hv + b_ref[hh, pl.ds(k, 8, stride=pitch), :]
            pv = av * pv
            hk[hh, pl.ds(8 * k, 8), :] = hv
            pk[hh, pl.ds(8 * k, 8), :] = pv
        for d in (1, 2, 4):
            if reverse:
                keep, sh = row < 8 - d, 8 - d
            else:
                keep, sh = row >= d, d
            hv = hv + pv * jnp.where(keep, pltpu.roll(hv, sh, 0), 0.0)
            pv = pv * jnp.where(keep, pltpu.roll(pv, sh, 0), 1.0)
        cin = carry_ref[:, cs]
        ends = hv + pv * cin
        if reverse:
            enter = jnp.where(row <= 6, pltpu.roll(ends, 7, 0), cin)
            carry_ref[:, cs] = jnp.broadcast_to((a0 * ends)[0:1, :], (8, HD))
        else:
            enter = jnp.where(row >= 1, pltpu.roll(ends, 1, 0), cin)
            carry_ref[:, cs] = jnp.broadcast_to(ends[7:8, :], (8, HD))
        for k in range(seg):
            out_ref[hh, pl.ds(k, 8, stride=pitch), :] = hk[hh, pl.ds(8 * k, 8), :] + pk[hh, pl.ds(8 * k, 8), :] * enter


def _mixer_fwd(z, x, gw, small):
    s = x.shape[0]
    tm = min(TM_SEQ, s)
    (pool_w, pool_scale, conv_w, conv_b, w_rg, b_rg, w_ig, b_ig, lam, b_gate) = small

    def body(z_ref, x_ref, gw_ref, pw_ref, ps_ref, cw_ref, cb_ref, wrg_ref, brg_ref, wig_ref, big_ref, lam_ref,
             bg_ref, h_ref, yl_ref, mg_ref, yp_ref, yr_ref, h1_ref, a_ref, r_ref, ig_ref, xc_ref,
             pprojT, lru_w, wout_w, pbuf, lbuf, a_s, b_s, h_s, hk, pk, hcar, sems):
        i = pl.program_id(0)
        t0 = i * tm

        @pl.when(i == 0)
        def _():
            _load_weights(gw_ref, [("pproj", pprojT, PW), ("lru", lru_w, D), ("wout", wout_w, D)], sems)
            pbuf[pl.ds(0, HALO), :] = jnp.zeros((HALO, PW), F32)
            lbuf[pl.ds(0, HALO), :] = jnp.zeros((HALO, D), F32)
            hcar[...] = jnp.zeros_like(hcar)

        pbuf[pl.ds(HALO, tm), :] = z_ref[:, pl.ds(0, PW)]
        _, mixed_pre = _pool_tile(pbuf, t0, tm, pw_ref, ps_ref)
        mixed = jnp.concatenate(mixed_pre, axis=1) * ps_ref[...]
        y_pool = _nt(mixed.astype(_MXU), pprojT[...])
        pbuf[pl.ds(0, HALO), :] = pbuf[pl.ds(tm, HALO), :]

        lbuf[pl.ds(HALO, tm), :] = z_ref[:, pl.ds(PW, D)]
        sp, _ = _softplus_neg(lam_ref[...])
        start = (t0 + lax.broadcasted_iota(jnp.int32, (tm, HD), 0)) == 0
        for hh in range(HEADS):
            xc, r, ig, a, _, _, mult = _lru_gates_head(hh, lbuf, start, tm, cw_ref, cb_ref, wrg_ref, brg_ref,
                                                       wig_ref, big_ref, sp)
            _to_segments(a_s, hh, a, tm)
            _to_segments(b_s, hh, mult * ig * xc, tm)
            cs = pl.ds(hh * HD, HD)
            a_ref[:, cs] = a
            r_ref[:, cs] = r.astype(_MXU)
            ig_ref[:, cs] = ig.astype(_MXU)
            xc_ref[:, cs] = xc.astype(_MXU)
        lbuf[pl.ds(0, HALO), :] = lbuf[pl.ds(tm, HALO), :]
        _segment_scan(a_s, b_s, h_s, hk, pk, hcar, tm, reverse=False)
        for hh in range(HEADS):
            h_ref[:, pl.ds(hh * HD, HD)] = _from_segments(h_s, hh, tm)
        gel, _ = _gelu_and_grad(z_ref[:, pl.ds(PW + D, D)])
        yl = (h_ref[...] * gel).astype(_MXU)
        yl_ref[...] = yl
        y_lru = _nn(yl, lru_w[...])

        g0 = _sigmoid(z_ref[:, pl.ds(PW + 2 * D, D)] + bg_ref[pl.ds(0, 1), :])
        g1 = _sigmoid(z_ref[:, pl.ds(PW + 3 * D, D)] + bg_ref[pl.ds(1, 1), :])
        merged = (g0 * y_pool + g1 * y_lru).astype(_MXU)
        mg_ref[...] = merged
        yp_ref[...] = y_pool.astype(_MXU)
        yr_ref[...] = y_lru.astype(_MXU)
        h1_ref[...] = x_ref[...] + _nn(merged, wout_w[...])

    tok = lambda w, dt: jax.ShapeDtypeStruct((s, w), dt)
    tspec = lambda w: pl.BlockSpec((tm, w), lambda i: (i, 0))
    full = lambda a: pl.BlockSpec(a.shape, lambda i: (0,) * a.ndim)
    seg_buf = pltpu.VMEM((HEADS, 8 * _seg_layout(tm)[1], HD), F32)
    return pl.pallas_call(
        body, name="mixer_fwd", grid=(s // tm,),
        out_shape=(tok(D, F32), tok(D, _MXU), tok(D, _MXU), tok(D, _MXU), tok(D, _MXU), tok(D, F32),
                   tok(D, F32), tok(D, _MXU), tok(D, _MXU), tok(D, _MXU)),
        in_specs=[tspec(NIN), tspec(D), pl.BlockSpec(memory_space=pl.ANY)] + [full(a) for a in small],
        out_specs=(tspec(D),) * 10,
        scratch_shapes=[pltpu.VMEM((D, PW), _MXU), pltpu.VMEM((D, D), _MXU), pltpu.VMEM((D, D), _MXU),
                        pltpu.VMEM((tm + HALO, PW), F32), pltpu.VMEM((tm + HALO, D), F32),
                        seg_buf, seg_buf, seg_buf, pltpu.VMEM((HEADS, tm, HD), F32), pltpu.VMEM((HEADS, tm, HD), F32),
                        pltpu.VMEM((8, D), F32), pltpu.SemaphoreType.DMA((3 * NDEV,))],
        compiler_params=_cparams(1),
    )(z, x, gw, *small)


def _ffn_fwd(h1, g2, gw):
    s = h1.shape[0]
    tm = min(TM, s)
    half = FF // 2

    def body(h1_ref, g2_ref, gw_ref, v_ref, gf_ref, uf_ref, h2_ref, wffnT, wffo, sems):
        @pl.when(pl.program_id(0) == 0)
        def _():
            _load_weights(gw_ref, [("wffn", wffnT, D), ("wffo", wffo, D)], sems)

        hv = h1_ref[...]
        inv = lax.rsqrt(jnp.mean(hv * hv, axis=-1, keepdims=True) + EPS)
        v = (hv * inv * g2_ref[...]).astype(_MXU)
        v_ref[...] = v
        acc = hv
        for ch in range(2):
            cs = pl.ds(ch * half, half)
            gf = _nt(v, wffnT[pl.ds(ch * half, half), :]).astype(_MXU)
            uf = _nt(v, wffnT[pl.ds(FF + ch * half, half), :]).astype(_MXU)
            gf_ref[:, cs] = gf
            uf_ref[:, cs] = uf
            gf32 = gf.astype(F32)
            act = (gf32 * _sigmoid(gf32) * uf.astype(F32)).astype(_MXU)
            acc = acc + _nn(act, wffo[pl.ds(ch * half, half), :])
        h2_ref[...] = acc

    tspec = lambda w: pl.BlockSpec((tm, w), lambda i: (i, 0))
    return pl.pallas_call(
        body, name="ffn_fwd", grid=(s // tm,),
        out_shape=(jax.ShapeDtypeStruct((s, D), _MXU), jax.ShapeDtypeStruct((s, FF), _MXU),
                   jax.ShapeDtypeStruct((s, FF), _MXU), jax.ShapeDtypeStruct((s, D), F32)),
        in_specs=[tspec(D), pl.BlockSpec((1, D), lambda i: (0, 0)), pl.BlockSpec(memory_space=pl.ANY)],
        out_specs=(tspec(D), tspec(FF), tspec(FF), tspec(D)),
        scratch_shapes=[pltpu.VMEM((2 * FF, D), _MXU), pltpu.VMEM((FF, D), _MXU), pltpu.SemaphoreType.DMA((2 * NDEV,))],
        compiler_params=_cparams(1),
    )(h1, g2, gw)


def _rms_bwd(dy, xn, inv, g):
    dg = jnp.sum(dy * xn, axis=0, keepdims=True)
    dxn = dy * g
    dx = inv * (dxn - xn * jnp.mean(dxn * xn, axis=-1, keepdims=True))
    return dx, dg


def _ple_loss_fwd_bwd(h2, p, target, g3, gfin, gw):
    s = h2.shape[0]
    tm = min(TM, s)

    def body(h2_ref, p_ref, t_ref, g3_ref, gf_ref, gw_ref,
             dh2_ref, loss_ref, dg3_ref, dgf_ref, gwpg_ref, gple_ref, wpg, pleT, sems):
        i = pl.program_id(0)

        @pl.when(i == 0)
        def _():
            _load_weights(gw_ref, [("wpg", wpg, D), ("ple", pleT, PLE)], sems)
            for ref in (loss_ref, dg3_ref, dgf_ref, gwpg_ref, gple_ref):
                ref[...] = jnp.zeros_like(ref)

        hv = h2_ref[...]
        inv3 = lax.rsqrt(jnp.mean(hv * hv, axis=-1, keepdims=True) + EPS)
        xn3 = hv * inv3
        n3 = (xn3 * g3_ref[...]).astype(_MXU)
        pg = _sigmoid(_nn(n3, wpg[...]))
        pm = p_ref[...].astype(_MXU)
        e = _nt(pm, pleT[...])
        h3 = hv + pg * e
        invf = lax.rsqrt(jnp.mean(h3 * h3, axis=-1, keepdims=True) + EPS)
        xf = h3 * invf
        diff = xf * gf_ref[...] - t_ref[...]
        loss_ref[...] += jnp.sum(diff * diff) * (0.5 / D)
        dh3, dgf = _rms_bwd(diff * (1.0 / D), xf, invf, gf_ref[...])
        dgf_ref[...] += dgf
        gple_ref[:, pl.ds(0, PLE)] += _tn((dh3 * pg).astype(_MXU), pm)
        dpg = (dh3 * e * pg * (1.0 - pg)).astype(_MXU)
        gwpg_ref[...] += _tn(n3, dpg)
        dn3 = _nt(dpg, wpg[...])
        dx3, dg3 = _rms_bwd(dn3, xn3, inv3, g3_ref[...])
        dg3_ref[...] += dg3
        dh2_ref[...] = dh3 + dx3

    tspec = lambda w: pl.BlockSpec((tm, w), lambda i: (i, 0))
    vec = pl.BlockSpec((1, D), lambda i: (0, 0))
    mat = pl.BlockSpec((D, D), lambda i: (0, 0))
    return pl.pallas_call(
        body, name="ple_loss", grid=(s // tm,),
        out_shape=(jax.ShapeDtypeStruct((s, D), F32), jax.ShapeDtypeStruct((8, 128), F32),
                   jax.ShapeDtypeStruct((1, D), F32), jax.ShapeDtypeStruct((1, D), F32),
                   jax.ShapeDtypeStruct((D, D), F32), jax.ShapeDtypeStruct((D, D), F32)),
        in_specs=[tspec(D), tspec(PLE), tspec(D), vec, vec, pl.BlockSpec(memory_space=pl.ANY)],
        out_specs=(tspec(D), pl.BlockSpec((8, 128), lambda i: (0, 0)), vec, vec, mat, mat),
        scratch_shapes=[pltpu.VMEM((D, D), _MXU), pltpu.VMEM((D, PLE), _MXU), pltpu.SemaphoreType.DMA((2 * NDEV,))],
        compiler_params=_cparams(1),
    )(h2, p, target, g3, gfin, gw)


def _ffn_bwd_hidden(dh2, gf, uf, gw):
    s = dh2.shape[0]
    tm = min(TM, s)
    nt = s // tm
    half = FF // 2

    def body(dh2_ref, gf_ref, uf_ref, gw_ref, dff_ref, gwo_ref, wffo, gacc, sems):
        i = pl.program_id(0)

        @pl.when(i == 0)
        def _():
            _load_weights(gw_ref, [("wffo", wffo, D)], sems)
            gacc[...] = jnp.zeros_like(gacc)

        dm = dh2_ref[...].astype(_MXU)
        for ch in range(2):
            cs = pl.ds(ch * half, half)
            dact = _nt(dm, wffo[cs, :])
            gfv = gf_ref[:, cs].astype(F32)
            ufv = uf_ref[:, cs].astype(F32)
            sg = _sigmoid(gfv)
            silu = gfv * sg
            gacc[cs, :] += _tn((silu * ufv).astype(_MXU), dm)
            dff_ref[:, pl.ds(ch * half, half)] = (dact * ufv * (sg * (1.0 + gfv * (1.0 - sg)))).astype(_MXU)
            dff_ref[:, pl.ds(FF + ch * half, half)] = (dact * silu).astype(_MXU)

        @pl.when(i == nt - 1)
        def _():
            pltpu.sync_copy(gacc, gwo_ref)

    tspec = lambda w: pl.BlockSpec((tm, w), lambda i: (i, 0))
    return pl.pallas_call(
        body, name="ffn_bwd_hidden", grid=(nt,),
        out_shape=(jax.ShapeDtypeStruct((s, 2 * FF), _MXU), jax.ShapeDtypeStruct((FF, D), F32)),
        in_specs=[tspec(D), tspec(FF), tspec(FF), pl.BlockSpec(memory_space=pl.ANY)],
        out_specs=(tspec(2 * FF), pl.BlockSpec(memory_space=pl.ANY)),
        scratch_shapes=[pltpu.VMEM((FF, D), _MXU), pltpu.VMEM((FF, D), F32), pltpu.SemaphoreType.DMA((NDEV,))],
        compiler_params=_cparams(1),
    )(dh2, gf, uf, gw)


def _proj_norm_bwd(dy, x, dres, g, gw, slab, width, name, lhs=None):
    s = x.shape[0]
    tm = min(TM, s)
    nl = 0 if lhs is None else 1

    def body(*refs):
        dy_ref, x_ref, dr_ref, g_ref = refs[:4]
        l_refs = refs[4:4 + nl]
        gw_ref, dx_ref, dg_ref = refs[4 + nl:7 + nl]
        gl_refs = refs[7 + nl:7 + 2 * nl]
        wT, sems = refs[7 + 2 * nl:]

        @pl.when(pl.program_id(0) == 0)
        def _():
            _load_weights(gw_ref, [(slab, wT, D)], sems)
            dg_ref[...] = jnp.zeros_like(dg_ref)
            for ref in gl_refs:
                ref[...] = jnp.zeros_like(ref)

        dv = _nn(dy_ref[...], wT[...])
        xv = x_ref[...]
        inv = lax.rsqrt(jnp.mean(xv * xv, axis=-1, keepdims=True) + EPS)
        dx, dg = _rms_bwd(dv, xv * inv, inv, g_ref[...])
        dg_ref[...] += dg
        dr = dr_ref[...]
        dx_ref[...] = dr + dx
        for l_ref, gl_ref in zip(l_refs, gl_refs):
            gl_ref[...] += _tn(l_ref[...], dr.astype(_MXU))

    tspec = lambda w: pl.BlockSpec((tm, w), lambda i: (i, 0))
    vec = pl.BlockSpec((1, D), lambda i: (0, 0))
    mat = pl.BlockSpec((D, D), lambda i: (0, 0))
    return pl.pallas_call(
        body, name=name, grid=(s // tm,),
        out_shape=(jax.ShapeDtypeStruct((s, D), F32), jax.ShapeDtypeStruct((1, D), F32))
        + (jax.ShapeDtypeStruct((D, D), F32),) * nl,
        in_specs=[tspec(width), tspec(D), tspec(D), vec] + [tspec(D)] * nl + [pl.BlockSpec(memory_space=pl.ANY)],
        out_specs=(tspec(D), vec) + (mat,) * nl,
        scratch_shapes=[pltpu.VMEM((width, D), _MXU), pltpu.SemaphoreType.DMA((NDEV,))],
        compiler_params=_cparams(1),
    )(dy, x, dres, g, *([] if lhs is None else [lhs]), gw)


def _mixer_bwd(dh1, z, h, y_pool, y_lru, saved, gw, small):
    s = dh1.shape[0]
    tm = min(TM_SEQ, s)
    nt = s // tm
    (pool_w, pool_scale, conv_w, conv_b, w_rg, b_rg, w_ig, b_ig, lam, b_gate) = small

    def body(dh1_ref, z_ref, zp_ref, h_ref, hp_ref, yp_ref, yr_ref, a_ref, r_ref, ig_ref, xc_ref, gw_ref,
             pw_ref, ps_ref, cw_ref, cb_ref, wrg_ref, brg_ref, wig_ref, big_ref, lam_ref, bg_ref,
             dz_ref, dyr_ref, dyp_ref, mx_ref,
             gbg_ref, glam_ref, gbrg_ref, gbig_ref, gcb_ref, gcw_ref, gps_ref, gpw_ref, gwrg_ref, gwig_ref,
             pprojT, lru_w, wout_w, pbuf, lbuf, hbuf, qbuf, xbuf, a_s, g_s, dh_s, hk, pk, dcar, sems):
        step = pl.program_id(0)
        i = nt - 1 - step
        t0 = i * tm

        @pl.when(step == 0)
        def _():
            _load_weights(gw_ref, [("pproj", pprojT, PW), ("lru", lru_w, D), ("wout", wout_w, D)], sems)
            for ref in (gbg_ref, glam_ref, gbrg_ref, gbig_ref, gcb_ref, gcw_ref, gps_ref, gpw_ref, gwrg_ref, gwig_ref):
                ref[...] = jnp.zeros_like(ref)
            qbuf[pl.ds(tm, HALO), :] = jnp.zeros((HALO, PW), F32)
            xbuf[pl.ds(tm, 8), :] = jnp.zeros((8, D), F32)
            dcar[...] = jnp.zeros_like(dcar)

        first = i == 0
        zprev = jnp.where(first, 0.0, zp_ref[...])
        hprev = jnp.where(first, 0.0, hp_ref[...])

        d_merged = _nt(dh1_ref[...].astype(_MXU), wout_w[...])

        g0 = _sigmoid(z_ref[:, pl.ds(PW + 2 * D, D)] + bg_ref[pl.ds(0, 1), :])
        g1 = _sigmoid(z_ref[:, pl.ds(PW + 3 * D, D)] + bg_ref[pl.ds(1, 1), :])
        dz0 = d_merged * yp_ref[...].astype(F32) * g0 * (1.0 - g0)
        dz1 = d_merged * yr_ref[...].astype(F32) * g1 * (1.0 - g1)
        dz_ref[:, pl.ds(PW + 2 * D, D)] = dz0.astype(_MXU)
        dz_ref[:, pl.ds(PW + 3 * D, D)] = dz1.astype(_MXU)
        gbg_ref[pl.ds(0, 1), :] += jnp.sum(dz0, axis=0, keepdims=True)
        gbg_ref[pl.ds(1, 1), :] += jnp.sum(dz1, axis=0, keepdims=True)
        d_ypool = (d_merged * g0).astype(_MXU)
        d_ylru = (d_merged * g1).astype(_MXU)
        dyp_ref[...] = d_ypool
        dyr_ref[...] = d_ylru

        d_yl = _nt(d_ylru, lru_w[...])
        gel, dgel = _gelu_and_grad(z_ref[:, pl.ds(PW + D, D)])
        dz_ref[:, pl.ds(PW + D, D)] = (d_yl * h_ref[...] * dgel).astype(_MXU)
        g_full = d_yl * gel
        lbuf[pl.ds(0, HALO), :] = zprev[:, PW:PW + D]
        lbuf[pl.ds(HALO, tm), :] = z_ref[:, pl.ds(PW, D)]
        hbuf[pl.ds(0, 8), :] = hprev
        hbuf[pl.ds(8, tm), :] = h_ref[...]
        sp, sneg = _softplus_neg(lam_ref[...])
        start = (t0 + lax.broadcasted_iota(jnp.int32, (tm, HD), 0)) == 0
        for hh in range(HEADS):
            cs = pl.ds(hh * HD, HD)
            _to_segments(a_s, hh, a_ref[:, cs], tm)
            _to_segments(g_s, hh, g_full[:, hh * HD:(hh + 1) * HD], tm)
        _segment_scan(a_s, g_s, dh_s, hk, pk, dcar, tm, reverse=True)
        for hh in range(HEADS):
            cs = pl.ds(hh * HD, HD)
            a = a_ref[:, cs]
            r = r_ref[:, cs].astype(F32)
            ig = ig_ref[:, cs].astype(F32)
            xc = xc_ref[:, cs].astype(F32)
            a2 = a * a
            one_m = 1.0 - a2
            live = jnp.logical_and(one_m > 0.0, jnp.logical_not(start))
            inv_mult = lax.rsqrt(jnp.where(live, one_m, 1.0))
            mult = jnp.where(live, one_m * inv_mult, jnp.where(start, 1.0, 0.0))
            dh = _from_segments(dh_s, hh, tm)
            d_mult = dh * ig * xc
            d_loga = dh * hbuf[pl.ds(7, tm), cs] * a - jnp.where(live, d_mult * a2 * inv_mult, 0.0)
            glam_ref[:, cs] += jnp.sum(d_loga * (LRU_C * r) * sneg[:, hh * HD:(hh + 1) * HD], axis=0, keepdims=True)
            d_rpre = d_loga * (-LRU_C * sp[:, hh * HD:(hh + 1) * HD]) * r * (1.0 - r)
            d_igpre = dh * mult * xc * ig * (1.0 - ig)
            gbrg_ref[pl.ds(hh, 1), :] += jnp.sum(d_rpre, axis=0, keepdims=True)
            gbig_ref[pl.ds(hh, 1), :] += jnp.sum(d_igpre, axis=0, keepdims=True)
            drm = d_rpre.astype(_MXU)
            dim = d_igpre.astype(_MXU)
            xcm = xc.astype(_MXU)
            gwrg_ref[hh] += _tn(xcm, drm)
            gwig_ref[hh] += _tn(xcm, dim)
            d_xc = dh * mult * ig + _nt(drm, wrg_ref[hh]) + _nt(dim, wig_ref[hh])
            gcb_ref[:, cs] += jnp.sum(d_xc, axis=0, keepdims=True)
            for k in range(CONV):
                gcw_ref[pl.ds(k, 1), cs] += jnp.sum(d_xc * lbuf[pl.ds(HALO - (CONV - 1) + k, tm), cs], axis=0,
                                                    keepdims=True)
            xbuf[pl.ds(0, tm), cs] = d_xc
        dzl = cw_ref[pl.ds(CONV - 1, 1), :] * xbuf[pl.ds(0, tm), :]
        for k in range(CONV - 1):
            dzl = dzl + cw_ref[pl.ds(k, 1), :] * xbuf[pl.ds(CONV - 1 - k, tm), :]
        dz_ref[:, pl.ds(PW, D)] = dzl.astype(_MXU)
        xbuf[pl.ds(tm, 8), :] = xbuf[pl.ds(0, 8), :]

        d_mixed = _nn(d_ypool, pprojT[...])
        pbuf[pl.ds(0, HALO), :] = zprev[:, 0:PW]
        pbuf[pl.ds(HALO, tm), :] = z_ref[:, pl.ds(0, PW)]
        pooled, mixed_pre = _pool_tile(pbuf, t0, tm, pw_ref, ps_ref)
        mp = jnp.concatenate(mixed_pre, axis=1)
        mx_ref[...] = (mp * ps_ref[...]).astype(_MXU)
        gps_ref[...] += jnp.sum(d_mixed * mp, axis=0, keepdims=True)
        d_mp = (d_mixed * ps_ref[...]).astype(_MXU)
        t = t0 + lax.broadcasted_iota(jnp.int32, (tm, GD), 0)
        d_pooled = []
        for g, w in enumerate(WINDOWS):
            dmg = d_mp[:, g * GD:(g + 1) * GD]
            gpw_ref[g] += _tn(pooled[g].astype(_MXU), dmg)
            dp = _nt(dmg, pw_ref[g])
            d_pooled.append(dp)
            qbuf[pl.ds(0, tm), pl.ds(g * GD, GD)] = dp / jnp.minimum(t + 1, w).astype(F32)
        for g, w in enumerate(WINDOWS):
            cs = pl.ds(g * GD, GD)
            acc = qbuf[pl.ds(0, tm), cs]
            for d in range(1, w):
                acc = acc + qbuf[pl.ds(d, tm), cs]
            dz_ref[:, cs] = (acc - d_pooled[g]).astype(_MXU)
        qbuf[pl.ds(tm, HALO), :] = qbuf[pl.ds(0, HALO), :]

    rev = lambda w: pl.BlockSpec((tm, w), lambda g: (nt - 1 - g, 0))
    prev = lambda rows, w: pl.BlockSpec((rows, w), lambda g: (jnp.maximum((nt - 1 - g) * (tm // rows) - 1, 0), 0))
    full = lambda a: pl.BlockSpec(a.shape, lambda g: (0,) * a.ndim)
    tok = lambda w, dt: jax.ShapeDtypeStruct((s, w), dt)
    acc_shapes = [(2, D), (1, D), (HEADS, HD), (HEADS, HD), (1, D), (CONV, D), (1, PW), (GROUPS, GD, GD),
                  (HEADS, HD, HD), (HEADS, HD, HD)]
    acc_specs = tuple(pl.BlockSpec(sh, lambda g, n=len(sh): (0,) * n) for sh in acc_shapes)
    seg_buf = pltpu.VMEM((HEADS, 8 * _seg_layout(tm)[1], HD), F32)
    a_in, r_in, ig_in, xc_in = saved
    return pl.pallas_call(
        body, name="mixer_bwd", grid=(nt,),
        out_shape=(tok(NIN, _MXU), tok(D, _MXU), tok(D, _MXU), tok(PW, _MXU))
        + tuple(jax.ShapeDtypeStruct(sh, F32) for sh in acc_shapes),
        in_specs=[rev(D), rev(NIN), prev(HALO, NIN), rev(D), prev(8, D), rev(D), rev(D), rev(D), rev(D), rev(D), rev(D),
                  pl.BlockSpec(memory_space=pl.ANY)] + [full(a) for a in small],
        out_specs=(rev(NIN), rev(D), rev(D), rev(PW)) + acc_specs,
        scratch_shapes=[pltpu.VMEM((D, PW), _MXU), pltpu.VMEM((D, D), _MXU), pltpu.VMEM((D, D), _MXU),
                        pltpu.VMEM((tm + HALO, PW), F32), pltpu.VMEM((tm + HALO, D), F32),
                        pltpu.VMEM((tm + 8, D), F32), pltpu.VMEM((tm + HALO, PW), F32), pltpu.VMEM((tm + 8, D), F32),
                        seg_buf, seg_buf, seg_buf, pltpu.VMEM((HEADS, tm, HD), F32), pltpu.VMEM((HEADS, tm, HD), F32),
                        pltpu.VMEM((8, D), F32), pltpu.SemaphoreType.DMA((3 * NDEV,))],
        compiler_params=_cparams(1),
    )(dh1, z, z, h, h, y_pool, y_lru, a_in, r_in, ig_in, xc_in, gw, *small)


def _split3(a):
    hi = a.astype(jnp.bfloat16).astype(F32)
    mid = (a - hi).astype(jnp.bfloat16).astype(F32)
    lo = (a - hi - mid).astype(jnp.bfloat16).astype(F32)
    return jnp.stack([hi, mid, lo])


def _small_pack(parts):
    flat = jnp.concatenate([a.reshape(-1) for a in parts])
    return jnp.pad(flat, (0, NDEV * SMALL_ROWS * D - flat.shape[0])).reshape(NDEV * SMALL_ROWS, D)


def _small_unpack(packed, shapes):
    flat = packed.reshape(-1)
    out, o = [], 0
    for sh in shapes:
        n = math.prod(sh)
        out.append(flat[o:o + n].reshape(sh))
        o += n
    return out


def kernel(x, p, norm1_g, w_in, b_gate, pool_w, pool_scale, pool_proj, conv_w, conv_b, w_rg, b_rg, w_ig, b_ig, lru_lambda, lru_proj, w_out, norm2_g, w_ffn_in, w_ffn_out, ple_norm_g, w_ple_gate, w_ple_proj, final_g, loss_target, m_norm1_g, m_w_in, m_b_gate, m_pool_w, m_pool_scale, m_pool_proj, m_conv_w, m_conv_b, m_w_rg, m_b_rg, m_w_ig, m_b_ig, m_lru_lambda, m_lru_proj, m_w_out, m_norm2_g, m_w_ffn_in, m_w_ffn_out, m_ple_norm_g, m_w_ple_gate, m_w_ple_proj, m_final_g, v_norm1_g, v_w_in, v_b_gate, v_pool_w, v_pool_scale, v_pool_proj, v_conv_w, v_conv_b, v_w_rg, v_b_rg, v_w_ig, v_b_ig, v_lru_lambda, v_lru_proj, v_w_out, v_norm2_g, v_w_ffn_in, v_w_ffn_out, v_ple_norm_g, v_w_ple_gate, v_w_ple_proj, v_final_g):
    axes = ("x", "y", "c")
    me = 4 * lax.axis_index("x") + 2 * lax.axis_index("y") + lax.axis_index("c")
    x2 = x[0]
    p2 = p[0, 0]
    tgt = loss_target[0]

    n_small = (CONV + 2) * 128
    small_terms = _split3(jnp.concatenate([conv_w[0].reshape(-1), b_gate[0].reshape(-1)]))
    small_rows = jnp.pad(small_terms, ((0, 16 - 3), (0, D - n_small)))
    own_first = jnp.concatenate([w_in[0].T.astype(_MXU), small_rows.astype(_MXU)], axis=0)
    own_second = jnp.concatenate([
        w_ffn_in[0].T.astype(_MXU),
        jnp.pad(pool_proj[0].T, ((0, 0), (0, D - PW))).astype(_MXU),
        jnp.pad(w_ple_proj[0].T, ((0, 0), (0, D - PLE))).astype(_MXU),
        lru_proj[0].astype(_MXU), w_out[0].astype(_MXU), w_ffn_out[0].astype(_MXU), w_ple_gate[0].astype(_MXU),
    ], axis=0)
    u, z, gw_first, gw = _inproj_fwd(x2, norm1_g, own_first, own_second)
    off = W_OFF["f32s"][0]
    st = gw_first[:, off:off + 3, :n_small].astype(F32)
    sf = st[:, 0] + st[:, 1] + st[:, 2]
    conv_w_full = sf[:, :CONV * 128].reshape(NDEV, CONV, 128).transpose(1, 0, 2).reshape(CONV, D)
    b_gate_full = sf[:, CONV * 128:].reshape(NDEV, 2, 128).transpose(1, 0, 2).reshape(2, D)

    small = (pool_w[0].astype(_MXU), pool_scale, conv_w_full, conv_b, w_rg[0].astype(_MXU), b_rg[0],
             w_ig[0].astype(_MXU), b_ig[0], lru_lambda, b_gate_full)

    h, yl, merged, y_pool, y_lru, h1, *saved = _mixer_fwd(z, x2, gw, small)
    v, gf, uf, h2 = _ffn_fwd(h1, norm2_g, gw)

    dh2, loss_blk, g_ple_norm, g_final, part_wpg, part_ple = _ple_loss_fwd_bwd(h2, p2, tgt, ple_norm_g,
                                                                               final_g.reshape(1, D), gw)
    dff, part_wffo = _ffn_bwd_hidden(dh2, gf, uf, gw)
    dh1, g_norm2 = _proj_norm_bwd(dff, h1, dh2, norm2_g, gw, "wffn", 2 * FF, "ffn_bwd_in")
    (dz, d_ylru, d_ypool, mixed, g_bgate, g_lam, g_brg, g_big, g_convb, g_convw, g_pscale, g_poolw, g_wrg,
     g_wig) = _mixer_bwd(dh1, z, h, y_pool, y_lru, saved, gw, small)
    grad_x, g_norm1, part_wout = _proj_norm_bwd(dz, x2, dh1, norm1_g, gw_first, "win", NIN, "inproj_bwd", lhs=merged)

    small_shapes = [(1, D), (GROUPS, GD, GD), (1, PW), (1, D), (HEADS, HD, HD), (HEADS, HD), (HEADS, HD, HD),
                    (HEADS, HD), (1, D), (1, D), (1, D), (1, D), (2, D), (CONV, D), (1, 1)]
    small_part = _small_pack([g_norm1, g_poolw, g_pscale, g_convb, g_wrg, g_brg, g_wig, g_big, g_lam, g_norm2,
                              g_ple_norm, g_final, g_bgate, g_convw, loss_blk[0:1, 0:1]])
    riders = [_grad_matmul(yl, d_ylru, "grad_lru_proj"), part_wout, _grad_matmul(d_ypool, mixed, "grad_pool_proj")]
    rs_wffn = _grad_matmul_rs(dff, v, "grad_w_ffn_in", 704, extras=[part_wffo, part_wpg, part_ple], narrow=_MXU)
    rs_win = _grad_matmul_rs(dz, u, "grad_w_in", 576, extras=riders + [small_part], narrow=_MXU, tail=SMALL_ROWS)

    def reduced(parts, name):
        return [_sum_arrays([t_own, landed[0], landed[1], landed[2]], "rs_sum_" + name + str(n))
                for n, (t_own, landed) in enumerate(parts)]

    red_wffn, = reduced(rs_wffn, "wffn")
    red_win, red_small = reduced(rs_win, "win")
    g_w_in = red_win[:576].T
    g_w_ffn_in = red_wffn[:704].T
    g_w_ffn_out = red_wffn[704:1056]
    g_w_ple_gate = red_wffn[1056:1184]
    g_w_ple_proj = red_wffn[1184:1312, :PLE].T
    g_lru_proj, g_w_out = red_win[576:704], red_win[704:832]
    g_pool_proj = red_win[832:960, :PW].T
    small_red = _all_gather_small(red_small)
    (gs_norm1, gs_poolw, gs_pscale, gs_convb, gs_wrg, gs_brg, gs_wig, gs_big, gs_lam, gs_norm2, gs_ple_norm,
     gs_final, gs_bgate, gs_convw, loss_sum) = _small_unpack(small_red, small_shapes)
    loss = loss_sum[0, 0]
    g_b_gate = lax.dynamic_slice_in_dim(gs_bgate, me * 128, 128, axis=1)
    g_conv_w = lax.dynamic_slice_in_dim(gs_convw, me * 128, 128, axis=1)

    grads = {
        "norm1_g": gs_norm1, "w_in": g_w_in[None], "b_gate": g_b_gate[None], "pool_w": gs_poolw[None],
        "pool_scale": gs_pscale, "pool_proj": g_pool_proj[None], "conv_w": g_conv_w[None], "conv_b": gs_convb,
        "w_rg": gs_wrg[None], "b_rg": gs_brg[None], "w_ig": gs_wig[None], "b_ig": gs_big[None], "lru_lambda": gs_lam,
        "lru_proj": g_lru_proj[None], "w_out": g_w_out[None], "norm2_g": gs_norm2, "w_ffn_in": g_w_ffn_in[None],
        "w_ffn_out": g_w_ffn_out[None], "ple_norm_g": gs_ple_norm, "w_ple_gate": g_w_ple_gate[None],
        "w_ple_proj": g_w_ple_proj[None], "final_g": gs_final.reshape(D),
    }
    weights = dict(norm1_g=norm1_g, w_in=w_in, b_gate=b_gate, pool_w=pool_w, pool_scale=pool_scale, pool_proj=pool_proj,
                   conv_w=conv_w, conv_b=conv_b, w_rg=w_rg, b_rg=b_rg, w_ig=w_ig, b_ig=b_ig, lru_lambda=lru_lambda,
                   lru_proj=lru_proj, w_out=w_out, norm2_g=norm2_g, w_ffn_in=w_ffn_in, w_ffn_out=w_ffn_out,
                   ple_norm_g=ple_norm_g, w_ple_gate=w_ple_gate, w_ple_proj=w_ple_proj, final_g=final_g)
    moments_m = dict(norm1_g=m_norm1_g, w_in=m_w_in, b_gate=m_b_gate, pool_w=m_pool_w, pool_scale=m_pool_scale,
                     pool_proj=m_pool_proj, conv_w=m_conv_w, conv_b=m_conv_b, w_rg=m_w_rg, b_rg=m_b_rg, w_ig=m_w_ig,
                     b_ig=m_b_ig, lru_lambda=m_lru_lambda, lru_proj=m_lru_proj, w_out=m_w_out, norm2_g=m_norm2_g,
                     w_ffn_in=m_w_ffn_in, w_ffn_out=m_w_ffn_out, ple_norm_g=m_ple_norm_g, w_ple_gate=m_w_ple_gate,
                     w_ple_proj=m_w_ple_proj, final_g=m_final_g)
    moments_v = dict(norm1_g=v_norm1_g, w_in=v_w_in, b_gate=v_b_gate, pool_w=v_pool_w, pool_scale=v_pool_scale,
                     pool_proj=v_pool_proj, conv_w=v_conv_w, conv_b=v_conv_b, w_rg=v_w_rg, b_rg=v_b_rg, w_ig=v_w_ig,
                     b_ig=v_b_ig, lru_lambda=v_lru_lambda, lru_proj=v_lru_proj, w_out=v_w_out, norm2_g=v_norm2_g,
                     w_ffn_in=v_w_ffn_in, w_ffn_out=v_w_ffn_out, ple_norm_g=v_ple_norm_g, w_ple_gate=v_w_ple_gate,
                     w_ple_proj=v_w_ple_proj, final_g=v_final_g)
    names = list(weights)
    big = ("w_in", "w_ffn_in", "w_ffn_out", "lru_proj", "w_out", "w_ple_gate", "pool_proj", "w_ple_proj")
    slab_space = {"w_in": red_win[:576], "w_ffn_in": red_wffn[:704]}
    delta, new_m, new_v = {}, {}, {}
    for n in big:
        sh = weights[n].shape
        if n in slab_space:
            as2d = lambda a: a[0].T
            back = lambda a: a.T[None]
            g2d = slab_space[n]
        else:
            as2d = lambda a: a.reshape(sh[-2], sh[-1])
            back = lambda a: a.reshape(sh)
            g2d = as2d(grads[n])
        d_, m_, v_ = _adamw(as2d(weights[n]), g2d, as2d(moments_m[n]), as2d(moments_v[n]), "adamw_" + n)
        delta[n], new_m[n], new_v[n] = back(d_), back(m_), back(v_)
    rest = [n for n in names if n not in big]
    rest_shapes = [weights[n].shape for n in rest]
    packed = [_small_pack([src[n] for n in rest]) for src in (weights, grads, moments_m, moments_v)]
    d_, m_, v_ = _adamw(*packed, "adamw_small")
    for n, a, b_, c_ in zip(rest, _small_unpack(d_, rest_shapes), _small_unpack(m_, rest_shapes),
                            _small_unpack(v_, rest_shapes)):
        delta[n], new_m[n], new_v[n] = a, b_, c_

    return (loss, grad_x[None], *[grads[n] for n in names], *[delta[n] for n in names],
            *[new_m[n] for n in names], *[new_v[n] for n in names])
```

```python
import functools
import math

import jax
import jax.numpy as jnp
from jax import lax
from jax.experimental import pallas as pl
from jax.experimental.pallas import tpu as pltpu

F32 = jnp.float32
D = 1024
NIN = 4608
PW = 512
FF = 2816
PLE = 256
HEADS, HD = 8, 128
GROUPS, GD = 4, 128
WINDOWS = (2, 4, 8, 16)
HALO = 16
CONV = 4
EPS = 1e-6
LRU_C = 8.0
NDEV = 8
MESH = pl.DeviceIdType.MESH

ADAM_LR, ADAM_B1, ADAM_B2, ADAM_EPS, ADAM_WD, ADAM_STEP = 0.001, 0.9, 0.999, 1e-08, 0.01, 10

_MXU = jnp.bfloat16
TM = 512
TM_SEQ = 256
VMEM_LIMIT = 56 * 1024 * 1024
W_FIRST = (("win", 576), ("f32s", 16))
W_SECOND = (("wffn", 704), ("pproj", 128), ("ple", 128), ("lru", 128), ("wout", 128), ("wffo", 352), ("wpg", 128))
W_OFF = {}
for _slabs in (W_FIRST, W_SECOND):
    _o = 0
    for _n, _r in _slabs:
        W_OFF[_n] = (_o, _r)
        _o += _r
SMALL_ROWS = 48


def _cparams(n_axes=1, vmem=VMEM_LIMIT):
    return pltpu.CompilerParams(dimension_semantics=("arbitrary",) * n_axes, vmem_limit_bytes=vmem)


def _my_pos():
    return lax.axis_index("x"), lax.axis_index("y"), lax.axis_index("c")


def _nt(a, b):
    return lax.dot_general(a, b, (((1,), (1,)), ((), ())), preferred_element_type=F32)


def _nn(a, b):
    return lax.dot_general(a, b, (((1,), (0,)), ((), ())), preferred_element_type=F32)


def _tn(a, b):
    return lax.dot_general(a, b, (((0,), (0,)), ((), ())), preferred_element_type=F32)


def _sigmoid(x):
    return 0.5 * jnp.tanh(0.5 * x) + 0.5


_GELU_K = math.sqrt(2.0 / math.pi)


def _gelu_and_grad(x):
    x2 = x * x
    inner = _GELU_K * (x + 0.044715 * x2 * x)
    t = jnp.tanh(inner)
    g = 0.5 * x * (1.0 + t)
    dg = 0.5 * (1.0 + t) + 0.5 * x * (1.0 - t * t) * _GELU_K * (1.0 + 3.0 * 0.044715 * x2)
    return g, dg


def _softplus_neg(lam):
    x = -lam
    t = jnp.exp(-jnp.abs(x))
    u = 1.0 + t
    l1p = jnp.where(u == 1.0, t, jnp.log(u) * t / (u - 1.0))
    return jnp.maximum(x, 0.0) + l1p, _sigmoid(x)


def _start_slab_loads(g_ref, name, dst_ref, sems, base, width=D):
    off, rows = W_OFF[name]
    copies = []
    for k in range(NDEV):
        if width == D:
            src = g_ref.at[k, pl.ds(off, rows), :]
        else:
            src = g_ref.at[k, pl.ds(off, rows), pl.ds(0, width)]
        cp = pltpu.make_async_copy(src, dst_ref.at[pl.ds(k * rows, rows), :], sems.at[base + k])
        cp.start()
        copies.append(cp)
    return copies


def _load_weights(g_ref, items, sems):
    copies = []
    for n, (name, dst, width) in enumerate(items):
        copies += _start_slab_loads(g_ref, name, dst, sems, n * NDEV, width)
    for cp in copies:
        cp.wait()


class _Gather:
    def __init__(self, own_ref, out_ref, stage, send_sems, recv_sems, local_sem):
        x, y, c = _my_pos()
        self.c = c
        self.me, self.sibling = (x, y, c), (x, y, 1 - c)
        self.chips = [(1 - x, y), (x, 1 - y), (1 - x, 1 - y)]
        self.own_ref, self.out_ref, self.stage = own_ref, out_ref, stage
        self.send_sems, self.recv_sems = send_sems, recv_sems
        self.mine = pltpu.make_async_copy(stage, self.slab(*self.me), local_sem)
        self.first = [self.copy(0, self.me, self.sibling, src=stage)] + [
            self.copy(1 + j, self.me, (*chip, c), src=stage) for j, chip in enumerate(self.chips)]
        self.passed = [self.copy(4 + j, (*chip, c), self.sibling) for j, chip in enumerate(self.chips)]

    def slab(self, px, py, pc):
        return self.out_ref.at[4 * px + 2 * py + pc]

    def copy(self, k, block, to, src=None):
        return pltpu.make_async_remote_copy(
            src_ref=self.slab(*block) if src is None else src, dst_ref=self.slab(*block),
            send_sem=self.send_sems.at[k], recv_sem=self.recv_sems.at[k], device_id=to, device_id_type=MESH)

    def send_mine(self, far=True):
        pltpu.sync_copy(self.own_ref, self.stage)
        self.mine.start()
        for cp in self.first[:3]:
            cp.start()
        if far:
            self.send_far()

    def send_far(self):
        self.first[3].start()

    def pass_on(self, js):
        for j in js:
            self.copy(1 + j, (*self.chips[j], self.c), self.me).wait_recv()
            self.passed[j].start()

    def wait_sibling(self):
        self.copy(0, self.sibling, self.me).wait_recv()

    def wait_passed(self, js):
        for j in js:
            self.copy(4 + j, (*self.chips[j], 1 - self.c), self.me).wait_recv()

    def finish_sends(self):
        for cp in self.first + self.passed:
            cp.wait_send()
        self.mine.wait()


def _all_gather_small(piece):
    rows = piece.shape[0]

    def body(p_ref, out_ref, send_sems, recv_sems, local_sem):
        x, y, c = _my_pos()
        me = 4 * x + 2 * y + c
        mine = pltpu.make_async_copy(p_ref, out_ref.at[pl.ds(pl.multiple_of(me * rows, 8), rows), :], local_sem)
        mine.start()
        sends = []
        peers = []
        for r in range(1, NDEV):
            px = 1 - x if (r >> 2) & 1 else x
            py = 1 - y if (r >> 1) & 1 else y
            pc = 1 - c if r & 1 else c
            peers.append((px, py, pc))
            cp = pltpu.make_async_remote_copy(
                src_ref=p_ref, dst_ref=out_ref.at[pl.ds(pl.multiple_of(me * rows, 8), rows), :],
                send_sem=send_sems.at[r - 1], recv_sem=recv_sems.at[r - 1], device_id=(px, py, pc),
                device_id_type=MESH)
            cp.start()
            sends.append(cp)
        for r, (px, py, pc) in enumerate(peers):
            them = 4 * px + 2 * py + pc
            pltpu.make_async_remote_copy(
                src_ref=p_ref, dst_ref=out_ref.at[pl.ds(pl.multiple_of(them * rows, 8), rows), :],
                send_sem=send_sems.at[r], recv_sem=recv_sems.at[r], device_id=(px, py, pc),
                device_id_type=MESH).wait_recv()
        for cp in sends:
            cp.wait_send()
        mine.wait()

    return pl.pallas_call(
        body, name="ag_small",
        out_shape=jax.ShapeDtypeStruct((NDEV * rows, piece.shape[1]), piece.dtype),
        in_specs=[pl.BlockSpec(memory_space=pltpu.VMEM)],
        out_specs=pl.BlockSpec(memory_space=pl.ANY),
        scratch_shapes=[pltpu.SemaphoreType.DMA((7,)), pltpu.SemaphoreType.DMA((7,)), pltpu.SemaphoreType.DMA],
    )(piece)


def _row_block(rows, target=512, mult=8):
    b = min(rows, target) // mult * mult
    while rows % b:
        b -= mult
    return b


def _sum_arrays(arrs, name, narrow=None, target=704):
    rows, cols = arrs[0].shape
    br = _row_block(rows, target, 16)
    n = len(arrs)

    def body(*refs):
        acc = refs[0][...].astype(F32)
        for r in refs[1:n]:
            acc = acc + r[...].astype(F32)
        refs[n][...] = acc
        if narrow is not None:
            refs[n + 1][...] = acc.astype(narrow)

    spec = pl.BlockSpec((br, cols), lambda i: (i, 0))
    shape = jax.ShapeDtypeStruct((rows, cols), F32)
    if narrow is None:
        out_shape, out_specs = shape, spec
    else:
        out_shape, out_specs = (shape, jax.ShapeDtypeStruct((rows, cols), narrow)), (spec, spec)
    return pl.pallas_call(
        body, name=name, grid=(rows // br,), out_shape=out_shape,
        in_specs=[spec] * n, out_specs=out_specs, compiler_params=_cparams(1),
    )(*arrs)


def _adamw(w, g, m, v, name):
    rows, cols = w.shape
    br = _row_block(rows, 256)

    def body(w_ref, g_ref, m_ref, v_ref, d_ref, nm_ref, nv_ref):
        g_ = g_ref[...]
        m_ = ADAM_B1 * m_ref[...] + (1.0 - ADAM_B1) * g_
        v_ = ADAM_B2 * v_ref[...] + (1.0 - ADAM_B2) * (g_ * g_)
        m_hat = m_ / (1.0 - ADAM_B1 ** ADAM_STEP)
        v_hat = v_ / (1.0 - ADAM_B2 ** ADAM_STEP)
        d_ref[...] = -ADAM_LR * (m_hat / (jnp.sqrt(v_hat) + ADAM_EPS) + ADAM_WD * w_ref[...])
        nm_ref[...] = m_
        nv_ref[...] = v_

    spec = pl.BlockSpec((br, cols), lambda i: (i, 0))
    shape = jax.ShapeDtypeStruct((rows, cols), F32)
    return pl.pallas_call(
        body, name=name, grid=(rows // br,), out_shape=(shape, shape, shape),
        in_specs=[spec] * 4, out_specs=(spec, spec, spec), compiler_params=_cparams(1),
    )(w, g, m, v)


_CHIP_FLIPS = (2, 3, 1, 0)


def _grad_matmul(lhs, rhs, name):
    s, r = lhs.shape
    k = rhs.shape[1]
    tm = min(TM, s)

    def body(l_ref, r_ref, o_ref):
        @pl.when(pl.program_id(0) == 0)
        def _():
            o_ref[...] = jnp.zeros_like(o_ref)

        o_ref[:, pl.ds(0, k)] += _tn(l_ref[...].astype(_MXU), r_ref[...].astype(_MXU))

    return pl.pallas_call(
        body, name=name, grid=(s // tm,),
        out_shape=jax.ShapeDtypeStruct((r, D), F32),
        in_specs=[pl.BlockSpec((tm, r), lambda i: (i, 0)), pl.BlockSpec((tm, k), lambda i: (i, 0))],
        out_specs=pl.BlockSpec((r, D), lambda i: (0, 0)),
        compiler_params=_cparams(1),
    )(lhs, rhs)


def _grad_matmul_rs(lhs, rhs, name, rows, extras=(), narrow=None, tail=0):
    s, r8 = lhs.shape
    k = rhs.shape[1]
    tm = min(TM, s)
    nt = s // tm
    cpb = 1
    nblk = 4 // cpb
    nx = len(extras)
    ers = [e.shape[0] // NDEV for e in extras]
    er = sum(ers)
    srows = rows + er
    brows = 2 * cpb * srows
    groups = [(0, srows - tail, F32 if narrow is None else narrow)] + ([(srows - tail, tail, F32)] if tail else [])
    ng = len(groups)
    mid = min(nt - 1, max(1, nt // 4))

    def flip_of(p):
        return jnp.where(p == 0, 2, jnp.where(p == 1, 3, jnp.where(p == 2, 1, 0)))

    def block_col(b):
        x, y, _ = _my_pos()
        return (2 * x + y) ^ flip_of(b)

    def body(*refs):
        l_ref, r_ref = refs[:2]
        x_refs = refs[2:2 + nx]
        rest = refs[2 + nx:]
        town_ref = rest[0]
        lici_refs = rest[1:1 + ng]
        acc, stage = rest[1 + ng:3 + ng]
        send_bufs = rest[3 + ng:3 + 2 * ng]
        dsend, drecv, isend, irecv, xsem = rest[3 + 2 * ng:]
        b = pl.program_id(0)
        i = pl.program_id(1)
        x, y, c = _my_pos()
        mine = 2 * x + y
        sibling = (x, y, 1 - c)

        def chip_at(p):
            return mine ^ _CHIP_FLIPS[p]

        def slab_rows(p, parity):
            within = 0 if cpb == 1 else (chip_at(p) & 1) * 2
            return pl.ds(pl.multiple_of((within + parity) * srows, 8), srows)

        def push(p, slot):
            return pltpu.make_async_remote_copy(
                src_ref=acc.at[slot, slab_rows(p, 1 - c), :], dst_ref=stage.at[p % 2],
                send_sem=dsend.at[p], recv_sem=drecv.at[p], device_id=sibling, device_id_type=MESH)

        def ici(p):
            ch = chip_at(p)
            return [pltpu.make_async_remote_copy(
                src_ref=send_bufs[g].at[p % 2], dst_ref=lici_refs[g].at[p], send_sem=isend.at[3 * g + p],
                recv_sem=irecv.at[3 * g + p], device_id=(ch >> 1, ch & 1, c), device_id_type=MESH) for g in range(ng)]

        def extra_loads(p, slot):
            copies = []
            within = 0 if cpb == 1 else (chip_at(p) & 1) * 2
            for parity in range(2):
                off = rows
                for n, (x_ref, e) in enumerate(zip(x_refs, ers)):
                    src = x_ref.at[pl.ds(pl.multiple_of((2 * chip_at(p) + parity) * e, 8), e), :]
                    dst = acc.at[slot, pl.ds(pl.multiple_of((within + parity) * srows + off, 8), e), :]
                    copies.append(pltpu.make_async_copy(src, dst, xsem.at[(p * 2 + parity) * nx + n]))
                    off += e
            return copies

        def combine(p, slot):
            push(p, slot).wait_recv()
            total = acc[slot, slab_rows(p, c), :] + stage[p % 2]
            if p == 3:
                stage[p % 2] = total
                pltpu.sync_copy(stage.at[p % 2], town_ref)
            else:
                if p == 2:
                    for cp in ici(0):
                        cp.wait_send()
                for g, (r0, n, dt) in enumerate(groups):
                    send_bufs[g][p % 2] = total[r0:r0 + n, :].astype(dt)
                for cp in ici(p):
                    cp.start()

        for bb in range(nblk):
            slot = bb % 2
            positions = list(range(bb * cpb, (bb + 1) * cpb))

            @pl.when(jnp.logical_and(b == bb, i == 0))
            def _(bb=bb, slot=slot, positions=positions):
                if bb >= 2:
                    for p in range((bb - 2) * cpb, (bb - 1) * cpb):
                        push(p, slot).wait_send()
                for q in range(2 * cpb):
                    acc[slot, pl.ds(q * srows, rows), :] = jnp.zeros((rows, D), F32)
                for p in positions:
                    for cp in extra_loads(p, slot):
                        cp.start()

            if bb >= 1:
                @pl.when(jnp.logical_and(b == bb, i == mid))
                def _(bb=bb):
                    for p in range((bb - 1) * cpb, bb * cpb):
                        combine(p, (bb - 1) % 2)

        res = _tn(l_ref[...].astype(_MXU), r_ref[...].astype(_MXU))
        slot_now = b % 2
        for q in range(2 * cpb):
            acc[slot_now, pl.ds(q * srows, rows), pl.ds(0, k)] += res[q * rows:(q + 1) * rows, :]

        for bb in range(nblk):
            slot = bb % 2
            positions = list(range(bb * cpb, (bb + 1) * cpb))

            @pl.when(jnp.logical_and(b == bb, i == nt - 1))
            def _(bb=bb, slot=slot, positions=positions):
                for p in positions:
                    for cp in extra_loads(p, slot):
                        cp.wait()
                for p in positions:
                    push(p, slot).start()
                if bb == nblk - 1:
                    for p in positions:
                        combine(p, slot)
                    for p in range(max(0, (nblk - 2) * cpb), 4):
                        push(p, slot).wait_send()
                    for p in range(1, 3):
                        for cp in ici(p):
                            cp.wait_send()
                    for p in range(3):
                        for cp in ici(p):
                            cp.wait_recv()

    in_specs = [pl.BlockSpec((tm, 2 * cpb * rows), lambda b, i: (i, block_col(b))),
                pl.BlockSpec((tm, k), lambda b, i: (i, 0))]
    any_spec = pl.BlockSpec(memory_space=pl.ANY)
    in_specs += [any_spec] * nx
    args = [lhs, rhs, *extras]
    outs = pl.pallas_call(
        body, name=name, grid=(nblk, nt),
        out_shape=(jax.ShapeDtypeStruct((srows, D), F32),)
        + tuple(jax.ShapeDtypeStruct((3, n, D), dt) for _, n, dt in groups),
        in_specs=in_specs, out_specs=(any_spec,) * (1 + ng),
        scratch_shapes=[pltpu.VMEM((2, brows, D), F32), pltpu.VMEM((2, srows, D), F32)]
        + [pltpu.VMEM((2, n, D), dt) for _, n, dt in groups]
        + [pltpu.SemaphoreType.DMA((4,)), pltpu.SemaphoreType.DMA((4,)), pltpu.SemaphoreType.DMA((3 * ng,)),
           pltpu.SemaphoreType.DMA((3 * ng,)), pltpu.SemaphoreType.DMA((max(1, 8 * nx),))],
        compiler_params=_cparams(2),
    )(*args)
    t_own = outs[0]
    return [(t_own[r0:r0 + n], landed) for (r0, n, _), landed in zip(groups, outs[1:])]


def _inproj_fwd(x, g1, own_first, own_second):
    s = x.shape[0]
    tm = min(TM, s)
    nt = s // tm
    assert nt % 2 == 0
    rows1, rows2 = own_first.shape[0], own_second.shape[0]
    wrows = W_OFF["win"][1]
    cw = 2 * wrows

    def chip_col(b):
        px, py, _ = _my_pos()
        return (2 * px + py) ^ jnp.where(b == 0, 0, jnp.where(b == 1, 2, jnp.where(b == 2, 1, 3)))

    def body(x_ref, g1_ref, own1_ref, own2_ref, u_ref, z_ref, gw1_ref, gw2_ref, w_vmem, u_buf, stage1, stage2, sems,
             usem, send1, recv1, local1, send2, recv2, local2):
        b = pl.program_id(0)
        i = pl.program_id(1)
        ga = _Gather(own1_ref, gw1_ref, stage1, send1, recv1, local1)
        gb = _Gather(own2_ref, gw2_ref, stage2, send2, recv2, local2)
        c = ga.c

        def load_chip(px, py, own_too):
            copies = []
            for pc in range(2):
                dst = w_vmem.at[pl.ds(pc * wrows, wrows), :]
                copies.append(pltpu.make_async_copy(gw1_ref.at[4 * px + 2 * py + pc, pl.ds(0, wrows), :], dst,
                                                    sems.at[pc]))
            if own_too:
                mine_dst = w_vmem.at[pl.ds(pl.multiple_of(c * wrows, 16), wrows), :]
                copies[0] = pltpu.make_async_copy(own1_ref.at[pl.ds(0, wrows), :], mine_dst, sems.at[0])
                theirs_dst = w_vmem.at[pl.ds(pl.multiple_of((1 - c) * wrows, 16), wrows), :]
                copies[1] = pltpu.make_async_copy(gw1_ref.at[4 * px + 2 * py + 1 - c, pl.ds(0, wrows), :], theirs_dst,
                                                  sems.at[1])
            for cp in copies:
                cp.start()
            for cp in copies:
                cp.wait()

        @pl.when(jnp.logical_and(b == 0, i == 0))
        def _():
            ga.send_mine(far=False)
            ga.wait_sibling()
            load_chip(ga.me[0], ga.me[1], True)

        @pl.when(jnp.logical_and(b == 0, i == nt // 2))
        def _():
            ga.send_far()
            gb.send_mine()

        @pl.when(jnp.logical_and(b == 0, i == (3 * nt) // 4))
        def _():
            ga.pass_on((0, 1))

        @pl.when(jnp.logical_and(b == 1, i == (3 * nt) // 4))
        def _():
            ga.pass_on((2,))

        for j in range(3):
            @pl.when(jnp.logical_and(b == j + 1, i == 0))
            def _(j=j):
                ga.wait_passed((j,))
                load_chip(ga.chips[j][0], ga.chips[j][1], False)

        @pl.when(jnp.logical_and(b == 2, i == nt // 2))
        def _():
            gb.pass_on((0, 1))

        @pl.when(jnp.logical_and(b == 3, i == (3 * nt) // 4))
        def _():
            gb.pass_on((2,))

        slot = i % 2

        def u_write(t, sl):
            return pltpu.make_async_copy(u_buf.at[sl], u_ref.at[pl.ds(pl.multiple_of(t * tm, tm), tm), :], usem.at[sl])

        def u_read(t, sl):
            return pltpu.make_async_copy(u_ref.at[pl.ds(pl.multiple_of(t * tm, tm), tm), :], u_buf.at[sl], usem.at[sl])

        @pl.when(b == 0)
        def _():
            @pl.when(i >= 2)
            def _():
                u_write(i - 2, slot).wait()

            xv = x_ref[...]
            inv = lax.rsqrt(jnp.mean(xv * xv, axis=-1, keepdims=True) + EPS)
            u_buf[slot] = (xv * inv * g1_ref[...]).astype(_MXU)
            u_write(i, slot).start()

            @pl.when(i == nt - 1)
            def _():
                u_write(i - 1, 1 - slot).wait()
                u_write(i, slot).wait()
                u_read(0, 0).start()

        @pl.when(b > 0)
        def _():
            u_read(i, slot).wait()

            @pl.when(jnp.logical_or(b < 3, i < nt - 1))
            def _():
                u_read((i + 1) % nt, 1 - slot).start()

        z_ref[...] = _nt(u_buf[slot], w_vmem[...])

        @pl.when(jnp.logical_and(b == 3, i == nt - 1))
        def _():
            ga.finish_sends()
            gb.wait_sibling()
            gb.wait_passed((0, 1, 2))
            gb.finish_sends()

    any_spec = pl.BlockSpec(memory_space=pl.ANY)
    dma7 = pltpu.SemaphoreType.DMA((7,))
    return pl.pallas_call(
        body, name="inproj_fwd", grid=(4, nt),
        out_shape=(jax.ShapeDtypeStruct((s, D), _MXU), jax.ShapeDtypeStruct((s, NIN), F32),
                   jax.ShapeDtypeStruct((NDEV, rows1, D), own_first.dtype),
                   jax.ShapeDtypeStruct((NDEV, rows2, D), own_second.dtype)),
        in_specs=[pl.BlockSpec((tm, D), lambda b, i: (jnp.where(b == 0, i, nt - 1), 0)),
                  pl.BlockSpec((1, D), lambda b, i: (0, 0)), any_spec, any_spec],
        out_specs=(any_spec, pl.BlockSpec((tm, cw), lambda b, i: (i, chip_col(b))), any_spec, any_spec),
        scratch_shapes=[pltpu.VMEM((cw, D), _MXU), pltpu.VMEM((2, tm, D), _MXU), pltpu.VMEM((rows1, D), own_first.dtype),
                        pltpu.VMEM((rows2, D), own_second.dtype), pltpu.SemaphoreType.DMA((2,)),
                        pltpu.SemaphoreType.DMA((2,)),
                        dma7, dma7, pltpu.SemaphoreType.DMA, dma7, dma7, pltpu.SemaphoreType.DMA],
        compiler_params=_cparams(2),
    )(x, g1, own_first, own_second)


def _pool_tile(pbuf, t0, tm, pw_ref, scale_ref):
    t = t0 + lax.broadcasted_iota(jnp.int32, (tm, GD), 0)
    pooled, mixed_pre = [], []
    for g, w in enumerate(WINDOWS):
        cs = pl.ds(g * GD, GD)
        cur = pbuf[pl.ds(HALO, tm), cs]
        acc = cur
        for d in range(1, w):
            acc = acc + pbuf[pl.ds(HALO - d, tm), cs]
        cnt = jnp.minimum(t + 1, w).astype(F32)
        pg = acc / cnt - cur
        pooled.append(pg)
        mixed_pre.append(_nn(pg.astype(_MXU), pw_ref[g]))
    return pooled, mixed_pre


def _lru_gates_head(hh, lbuf, start, tm, cw_ref, cb_ref, wrg_ref, brg_ref, wig_ref, big_ref, sp):
    cs = pl.ds(hh * HD, HD)
    xc = cb_ref[:, cs] + cw_ref[pl.ds(CONV - 1, 1), cs] * lbuf[pl.ds(HALO, tm), cs]
    for k in range(CONV - 1):
        xc = xc + cw_ref[pl.ds(k, 1), cs] * lbuf[pl.ds(HALO - (CONV - 1) + k, tm), cs]
    xcm = xc.astype(_MXU)
    r = _sigmoid(_nn(xcm, wrg_ref[hh]) + brg_ref[pl.ds(hh, 1), :])
    ig = _sigmoid(_nn(xcm, wig_ref[hh]) + big_ref[pl.ds(hh, 1), :])
    a = jnp.exp(-LRU_C * r * sp[:, hh * HD:(hh + 1) * HD])
    one_m = 1.0 - a * a
    live = jnp.logical_and(one_m > 0.0, jnp.logical_not(start))
    inv_mult = lax.rsqrt(jnp.where(live, one_m, 1.0))
    mult = jnp.where(live, one_m * inv_mult, jnp.where(start, 1.0, 0.0))
    return xc, r, ig, a, live, inv_mult, mult


def _seg_layout(tm):
    seg = tm // 8
    return seg, seg + 8


def _to_segments(dst_ref, hh, val, tm):
    seg, pitch = _seg_layout(tm)
    for s in range(8):
        dst_ref[hh, pl.ds(s * pitch, seg), :] = val[s * seg:(s + 1) * seg, :]


def _from_segments(src_ref, hh, tm):
    seg, pitch = _seg_layout(tm)
    return jnp.concatenate([src_ref[hh, pl.ds(s * pitch, seg), :] for s in range(8)], axis=0)


def _segment_scan(a_ref, b_ref, out_ref, hk, pk, carry_ref, tm, reverse):
    seg, pitch = _seg_layout(tm)
    row = lax.broadcasted_iota(jnp.int32, (8, HD), 0)
    order = range(seg - 1, -1, -1) if reverse else range(seg)
    for hh in range(HEADS):
        cs = pl.ds(hh * HD, HD)
        if reverse:
            a0 = a_ref[hh, pl.ds(0, 8, stride=pitch), :]
            a_wrap = jnp.where(row <= 6, pltpu.roll(a0, 7, 0), 1.0)
        hv = jnp.zeros((8, HD), F32)
        pv = jnp.ones((8, HD), F32)
        for k in order:
            if not reverse:
                av = a_ref[hh, pl.ds(k, 8, stride=pitch), :]
            elif k + 1 < seg:
                av = a_ref[hh, pl.ds(k + 1, 8, stride=pitch), :]
            else:
                av = a_wrap
            hv = av * hv + b_ref[hh, pl.ds(k, 8, stride=pitch), :]
            pv = av * pv
            hk[hh, pl.ds(8 * k, 8), :] = hv
            pk[hh, pl.ds(8 * k, 8), :] = pv
        for d in (1, 2, 4):
            if reverse:
                keep, sh = row < 8 - d, 8 - d
            else:
                keep, sh = row >= d, d
            hv = hv + pv * jnp.where(keep, pltpu.roll(hv, sh, 0), 0.0)
            pv = pv * jnp.where(keep, pltpu.roll(pv, sh, 0), 1.0)
        cin = carry_ref[:, cs]
        ends = hv + pv * cin
        if reverse:
            enter = jnp.where(row <= 6, pltpu.roll(ends, 7, 0), cin)
            carry_ref[:, cs] = jnp.broadcast_to((a0 * ends)[0:1, :], (8, HD))
        else:
            enter = jnp.where(row >= 1, pltpu.roll(ends, 1, 0), cin)
            carry_ref[:, cs] = jnp.broadcast_to(ends[7:8, :], (8, HD))
        for k in range(seg):
            out_ref[hh, pl.ds(k, 8, stride=pitch), :] = hk[hh, pl.ds(8 * k, 8), :] + pk[hh, pl.ds(8 * k, 8), :] * enter


def _mixer_fwd(z, x, gw, small):
    s = x.shape[0]
    tm = min(TM_SEQ, s)
    (pool_w, pool_scale, conv_w, conv_b, w_rg, b_rg, w_ig, b_ig, lam, b_gate) = small

    def body(z_ref, x_ref, gw_ref, pw_ref, ps_ref, cw_ref, cb_ref, wrg_ref, brg_ref, wig_ref, big_ref, lam_ref,
             bg_ref, h_ref, yl_ref, mg_ref, yp_ref, yr_ref, h1_ref, a_ref, r_ref, ig_ref, xc_ref,
             pprojT, lru_w, wout_w, pbuf, lbuf, a_s, b_s, h_s, hk, pk, hcar, sems):
        i = pl.program_id(0)
        t0 = i * tm

        @pl.when(i == 0)
        def _():
            _load_weights(gw_ref, [("pproj", pprojT, PW), ("lru", lru_w, D), ("wout", wout_w, D)], sems)
            pbuf[pl.ds(0, HALO), :] = jnp.zeros((HALO, PW), F32)
            lbuf[pl.ds(0, HALO), :] = jnp.zeros((HALO, D), F32)
            hcar[...] = jnp.zeros_like(hcar)

        pbuf[pl.ds(HALO, tm), :] = z_ref[:, pl.ds(0, PW)]
        _, mixed_pre = _pool_tile(pbuf, t0, tm, pw_ref, ps_ref)
        mixed = jnp.concatenate(mixed_pre, axis=1) * ps_ref[...]
        y_pool = _nt(mixed.astype(_MXU), pprojT[...])
        pbuf[pl.ds(0, HALO), :] = pbuf[pl.ds(tm, HALO), :]

        lbuf[pl.ds(HALO, tm), :] = z_ref[:, pl.ds(PW, D)]
        sp, _ = _softplus_neg(lam_ref[...])
        start = (t0 + lax.broadcasted_iota(jnp.int32, (tm, HD), 0)) == 0
        for hh in range(HEADS):
            xc, r, ig, a, _, _, mult = _lru_gates_head(hh, lbuf, start, tm, cw_ref, cb_ref, wrg_ref, brg_ref,
                                                       wig_ref, big_ref, sp)
            _to_segments(a_s, hh, a, tm)
            _to_segments(b_s, hh, mult * ig * xc, tm)
            cs = pl.ds(hh * HD, HD)
            a_ref[:, cs] = a
            r_ref[:, cs] = r.astype(_MXU)
            ig_ref[:, cs] = ig.astype(_MXU)
            xc_ref[:, cs] = xc.astype(_MXU)
        lbuf[pl.ds(0, HALO), :] = lbuf[pl.ds(tm, HALO), :]
        _segment_scan(a_s, b_s, h_s, hk, pk, hcar, tm, reverse=False)
        for hh in range(HEADS):
            h_ref[:, pl.ds(hh * HD, HD)] = _from_segments(h_s, hh, tm)
        gel, _ = _gelu_and_grad(z_ref[:, pl.ds(PW + D, D)])
        yl = (h_ref[...] * gel).astype(_MXU)
        yl_ref[...] = yl
        y_lru = _nn(yl, lru_w[...])

        g0 = _sigmoid(z_ref[:, pl.ds(PW + 2 * D, D)] + bg_ref[pl.ds(0, 1), :])
        g1 = _sigmoid(z_ref[:, pl.ds(PW + 3 * D, D)] + bg_ref[pl.ds(1, 1), :])
        merged = (g0 * y_pool + g1 * y_lru).astype(_MXU)
        mg_ref[...] = merged
        yp_ref[...] = y_pool.astype(_MXU)
        yr_ref[...] = y_lru.astype(_MXU)
        h1_ref[...] = x_ref[...] + _nn(merged, wout_w[...])

    tok = lambda w, dt: jax.ShapeDtypeStruct((s, w), dt)
    tspec = lambda w: pl.BlockSpec((tm, w), lambda i: (i, 0))
    full = lambda a: pl.BlockSpec(a.shape, lambda i: (0,) * a.ndim)
    seg_buf = pltpu.VMEM((HEADS, 8 * _seg_layout(tm)[1], HD), F32)
    return pl.pallas_call(
        body, name="mixer_fwd", grid=(s // tm,),
        out_shape=(tok(D, F32), tok(D, _MXU), tok(D, _MXU), tok(D, _MXU), tok(D, _MXU), tok(D, F32),
                   tok(D, F32), tok(D, _MXU), tok(D, _MXU), tok(D, _MXU)),
        in_specs=[tspec(NIN), tspec(D), pl.BlockSpec(memory_space=pl.ANY)] + [full(a) for a in small],
        out_specs=(tspec(D),) * 10,
        scratch_shapes=[pltpu.VMEM((D, PW), _MXU), pltpu.VMEM((D, D), _MXU), pltpu.VMEM((D, D), _MXU),
                        pltpu.VMEM((tm + HALO, PW), F32), pltpu.VMEM((tm + HALO, D), F32),
                        seg_buf, seg_buf, seg_buf, pltpu.VMEM((HEADS, tm, HD), F32), pltpu.VMEM((HEADS, tm, HD), F32),
                        pltpu.VMEM((8, D), F32), pltpu.SemaphoreType.DMA((3 * NDEV,))],
        compiler_params=_cparams(1),
    )(z, x, gw, *small)


def _ffn_fwd(h1, g2, gw):
    s = h1.shape[0]
    tm = min(TM, s)
    half = FF // 2

    def body(h1_ref, g2_ref, gw_ref, v_ref, gf_ref, uf_ref, h2_ref, wffnT, wffo, sems):
        @pl.when(pl.program_id(0) == 0)
        def _():
            _load_weights(gw_ref, [("wffn", wffnT, D), ("wffo", wffo, D)], sems)

        hv = h1_ref[...]
        inv = lax.rsqrt(jnp.mean(hv * hv, axis=-1, keepdims=True) + EPS)
        v = (hv * inv * g2_ref[...]).astype(_MXU)
        v_ref[...] = v
        acc = hv
        for ch in range(2):
            cs = pl.ds(ch * half, half)
            gf = _nt(v, wffnT[pl.ds(ch * half, half), :]).astype(_MXU)
            uf = _nt(v, wffnT[pl.ds(FF + ch * half, half), :]).astype(_MXU)
            gf_ref[:, cs] = gf
            uf_ref[:, cs] = uf
            gf32 = gf.astype(F32)
            act = (gf32 * _sigmoid(gf32) * uf.astype(F32)).astype(_MXU)
            acc = acc + _nn(act, wffo[pl.ds(ch * half, half), :])
        h2_ref[...] = acc

    tspec = lambda w: pl.BlockSpec((tm, w), lambda i: (i, 0))
    return pl.pallas_call(
        body, name="ffn_fwd", grid=(s // tm,),
        out_shape=(jax.ShapeDtypeStruct((s, D), _MXU), jax.ShapeDtypeStruct((s, FF), _MXU),
                   jax.ShapeDtypeStruct((s, FF), _MXU), jax.ShapeDtypeStruct((s, D), F32)),
        in_specs=[tspec(D), pl.BlockSpec((1, D), lambda i: (0, 0)), pl.BlockSpec(memory_space=pl.ANY)],
        out_specs=(tspec(D), tspec(FF), tspec(FF), tspec(D)),
        scratch_shapes=[pltpu.VMEM((2 * FF, D), _MXU), pltpu.VMEM((FF, D), _MXU), pltpu.SemaphoreType.DMA((2 * NDEV,))],
        compiler_params=_cparams(1),
    )(h1, g2, gw)


def _rms_bwd(dy, xn, inv, g):
    dg = jnp.sum(dy * xn, axis=0, keepdims=True)
    dxn = dy * g
    dx = inv * (dxn - xn * jnp.mean(dxn * xn, axis=-1, keepdims=True))
    return dx, dg


def _ple_loss_fwd_bwd(h2, p, target, g3, gfin, gw):
    s = h2.shape[0]
    tm = min(TM, s)

    def body(h2_ref, p_ref, t_ref, g3_ref, gf_ref, gw_ref,
             dh2_ref, loss_ref, dg3_ref, dgf_ref, gwpg_ref, gple_ref, wpg, pleT, sems):
        i = pl.program_id(0)

        @pl.when(i == 0)
        def _():
            _load_weights(gw_ref, [("wpg", wpg, D), ("ple", pleT, PLE)], sems)
            for ref in (loss_ref, dg3_ref, dgf_ref, gwpg_ref, gple_ref):
                ref[...] = jnp.zeros_like(ref)

        hv = h2_ref[...]
        inv3 = lax.rsqrt(jnp.mean(hv * hv, axis=-1, keepdims=True) + EPS)
        xn3 = hv * inv3
        n3 = (xn3 * g3_ref[...]).astype(_MXU)
        pg = _sigmoid(_nn(n3, wpg[...]))
        pm = p_ref[...].astype(_MXU)
        e = _nt(pm, pleT[...])
        h3 = hv + pg * e
        invf = lax.rsqrt(jnp.mean(h3 * h3, axis=-1, keepdims=True) + EPS)
        xf = h3 * invf
        diff = xf * gf_ref[...] - t_ref[...]
        loss_ref[...] += jnp.sum(diff * diff) * (0.5 / D)
        dh3, dgf = _rms_bwd(diff * (1.0 / D), xf, invf, gf_ref[...])
        dgf_ref[...] += dgf
        gple_ref[:, pl.ds(0, PLE)] += _tn((dh3 * pg).astype(_MXU), pm)
        dpg = (dh3 * e * pg * (1.0 - pg)).astype(_MXU)
        gwpg_ref[...] += _tn(n3, dpg)
        dn3 = _nt(dpg, wpg[...])
        dx3, dg3 = _rms_bwd(dn3, xn3, inv3, g3_ref[...])
        dg3_ref[...] += dg3
        dh2_ref[...] = dh3 + dx3

    tspec = lambda w: pl.BlockSpec((tm, w), lambda i: (i, 0))
    vec = pl.BlockSpec((1, D), lambda i: (0, 0))
    mat = pl.BlockSpec((D, D), lambda i: (0, 0))
    return pl.pallas_call(
        body, name="ple_loss", grid=(s // tm,),
        out_shape=(jax.ShapeDtypeStruct((s, D), F32), jax.ShapeDtypeStruct((8, 128), F32),
                   jax.ShapeDtypeStruct((1, D), F32), jax.ShapeDtypeStruct((1, D), F32),
                   jax.ShapeDtypeStruct((D, D), F32), jax.ShapeDtypeStruct((D, D), F32)),
        in_specs=[tspec(D), tspec(PLE), tspec(D), vec, vec, pl.BlockSpec(memory_space=pl.ANY)],
        out_specs=(tspec(D), pl.BlockSpec((8, 128), lambda i: (0, 0)), vec, vec, mat, mat),
        scratch_shapes=[pltpu.VMEM((D, D), _MXU), pltpu.VMEM((D, PLE), _MXU), pltpu.SemaphoreType.DMA((2 * NDEV,))],
        compiler_params=_cparams(1),
    )(h2, p, target, g3, gfin, gw)


def _ffn_bwd_hidden(dh2, gf, uf, gw):
    s = dh2.shape[0]
    tm = min(TM, s)
    nt = s // tm
    half = FF // 2

    def body(dh2_ref, gf_ref, uf_ref, gw_ref, dff_ref, gwo_ref, wffo, gacc, sems):
        i = pl.program_id(0)

        @pl.when(i == 0)
        def _():
            _load_weights(gw_ref, [("wffo", wffo, D)], sems)
            gacc[...] = jnp.zeros_like(gacc)

        dm = dh2_ref[...].astype(_MXU)
        for ch in range(2):
            cs = pl.ds(ch * half, half)
            dact = _nt(dm, wffo[cs, :])
            gfv = gf_ref[:, cs].astype(F32)
            ufv = uf_ref[:, cs].astype(F32)
            sg = _sigmoid(gfv)
            silu = gfv * sg
            gacc[cs, :] += _tn((silu * ufv).astype(_MXU), dm)
            dff_ref[:, pl.ds(ch * half, half)] = (dact * ufv * (sg * (1.0 + gfv * (1.0 - sg)))).astype(_MXU)
            dff_ref[:, pl.ds(FF + ch * half, half)] = (dact * silu).astype(_MXU)

        @pl.when(i == nt - 1)
        def _():
            pltpu.sync_copy(gacc, gwo_ref)

    tspec = lambda w: pl.BlockSpec((tm, w), lambda i: (i, 0))
    return pl.pallas_call(
        body, name="ffn_bwd_hidden", grid=(nt,),
        out_shape=(jax.ShapeDtypeStruct((s, 2 * FF), _MXU), jax.ShapeDtypeStruct((FF, D), F32)),
        in_specs=[tspec(D), tspec(FF), tspec(FF), pl.BlockSpec(memory_space=pl.ANY)],
        out_specs=(tspec(2 * FF), pl.BlockSpec(memory_space=pl.ANY)),
        scratch_shapes=[pltpu.VMEM((FF, D), _MXU), pltpu.VMEM((FF, D), F32), pltpu.SemaphoreType.DMA((NDEV,))],
        compiler_params=_cparams(1),
    )(dh2, gf, uf, gw)


def _proj_norm_bwd(dy, x, dres, g, gw, slab, width, name, lhs=None):
    s = x.shape[0]
    tm = min(TM, s)
    nl = 0 if lhs is None else 1

    def body(*refs):
        dy_ref, x_ref, dr_ref, g_ref = refs[:4]
        l_refs = refs[4:4 + nl]
        gw_ref, dx_ref, dg_ref = refs[4 + nl:7 + nl]
        gl_refs = refs[7 + nl:7 + 2 * nl]
        wT, sems = refs[7 + 2 * nl:]

        @pl.when(pl.program_id(0) == 0)
        def _():
            _load_weights(gw_ref, [(slab, wT, D)], sems)
            dg_ref[...] = jnp.zeros_like(dg_ref)
            for ref in gl_refs:
                ref[...] = jnp.zeros_like(ref)

        dv = _nn(dy_ref[...], wT[...])
        xv = x_ref[...]
        inv = lax.rsqrt(jnp.mean(xv * xv, axis=-1, keepdims=True) + EPS)
        dx, dg = _rms_bwd(dv, xv * inv, inv, g_ref[...])
        dg_ref[...] += dg
        dr = dr_ref[...]
        dx_ref[...] = dr + dx
        for l_ref, gl_ref in zip(l_refs, gl_refs):
            gl_ref[...] += _tn(l_ref[...], dr.astype(_MXU))

    tspec = lambda w: pl.BlockSpec((tm, w), lambda i: (i, 0))
    vec = pl.BlockSpec((1, D), lambda i: (0, 0))
    mat = pl.BlockSpec((D, D), lambda i: (0, 0))
    return pl.pallas_call(
        body, name=name, grid=(s // tm,),
        out_shape=(jax.ShapeDtypeStruct((s, D), F32), jax.ShapeDtypeStruct((1, D), F32))
        + (jax.ShapeDtypeStruct((D, D), F32),) * nl,
        in_specs=[tspec(width), tspec(D), tspec(D), vec] + [tspec(D)] * nl + [pl.BlockSpec(memory_space=pl.ANY)],
        out_specs=(tspec(D), vec) + (mat,) * nl,
        scratch_shapes=[pltpu.VMEM((width, D), _MXU), pltpu.SemaphoreType.DMA((NDEV,))],
        compiler_params=_cparams(1),
    )(dy, x, dres, g, *([] if lhs is None else [lhs]), gw)


def _mixer_bwd(dh1, z, h, y_pool, y_lru, saved, gw, small):
    s = dh1.shape[0]
    tm = min(TM_SEQ, s)
    nt = s // tm
    (pool_w, pool_scale, conv_w, conv_b, w_rg, b_rg, w_ig, b_ig, lam, b_gate) = small

    def body(dh1_ref, z_ref, zp_ref, h_ref, hp_ref, yp_ref, yr_ref, a_ref, r_ref, ig_ref, xc_ref, gw_ref,
             pw_ref, ps_ref, cw_ref, cb_ref, wrg_ref, brg_ref, wig_ref, big_ref, lam_ref, bg_ref,
             dz_ref, dyr_ref, dyp_ref, mx_ref,
             gbg_ref, glam_ref, gbrg_ref, gbig_ref, gcb_ref, gcw_ref, gps_ref, gpw_ref, gwrg_ref, gwig_ref,
             pprojT, lru_w, wout_w, pbuf, lbuf, hbuf, qbuf, xbuf, a_s, g_s, dh_s, hk, pk, dcar, sems):
        step = pl.program_id(0)
        i = nt - 1 - step
        t0 = i * tm

        @pl.when(step == 0)
        def _():
            _load_weights(gw_ref, [("pproj", pprojT, PW), ("lru", lru_w, D), ("wout", wout_w, D)], sems)
            for ref in (gbg_ref, glam_ref, gbrg_ref, gbig_ref, gcb_ref, gcw_ref, gps_ref, gpw_ref, gwrg_ref, gwig_ref):
                ref[...] = jnp.zeros_like(ref)
            qbuf[pl.ds(tm, HALO), :] = jnp.zeros((HALO, PW), F32)
            xbuf[pl.ds(tm, 8), :] = jnp.zeros((8, D), F32)
            dcar[...] = jnp.zeros_like(dcar)

        first = i == 0
        zprev = jnp.where(first, 0.0, zp_ref[...])
        hprev = jnp.where(first, 0.0, hp_ref[...])

        d_merged = _nt(dh1_ref[...].astype(_MXU), wout_w[...])

        g0 = _sigmoid(z_ref[:, pl.ds(PW + 2 * D, D)] + bg_ref[pl.ds(0, 1), :])
        g1 = _sigmoid(z_ref[:, pl.ds(PW + 3 * D, D)] + bg_ref[pl.ds(1, 1), :])
        dz0 = d_merged * yp_ref[...].astype(F32) * g0 * (1.0 - g0)
        dz1 = d_merged * yr_ref[...].astype(F32) * g1 * (1.0 - g1)
        dz_ref[:, pl.ds(PW + 2 * D, D)] = dz0.astype(_MXU)
        dz_ref[:, pl.ds(PW + 3 * D, D)] = dz1.astype(_MXU)
        gbg_ref[pl.ds(0, 1), :] += jnp.sum(dz0, axis=0, keepdims=True)
        gbg_ref[pl.ds(1, 1), :] += jnp.sum(dz1, axis=0, keepdims=True)
        d_ypool = (d_merged * g0).astype(_MXU)
        d_ylru = (d_merged * g1).astype(_MXU)
        dyp_ref[...] = d_ypool
        dyr_ref[...] = d_ylru

        d_yl = _nt(d_ylru, lru_w[...])
        gel, dgel = _gelu_and_grad(z_ref[:, pl.ds(PW + D, D)])
        dz_ref[:, pl.ds(PW + D, D)] = (d_yl * h_ref[...] * dgel).astype(_MXU)
        g_full = d_yl * gel
        lbuf[pl.ds(0, HALO), :] = zprev[:, PW:PW + D]
        lbuf[pl.ds(HALO, tm), :] = z_ref[:, pl.ds(PW, D)]
        hbuf[pl.ds(0, 8), :] = hprev
        hbuf[pl.ds(8, tm), :] = h_ref[...]
        sp, sneg = _softplus_neg(lam_ref[...])
        start = (t0 + lax.broadcasted_iota(jnp.int32, (tm, HD), 0)) == 0
        for hh in range(HEADS):
            cs = pl.ds(hh * HD, HD)
            _to_segments(a_s, hh, a_ref[:, cs], tm)
            _to_segments(g_s, hh, g_full[:, hh * HD:(hh + 1) * HD], tm)
        _segment_scan(a_s, g_s, dh_s, hk, pk, dcar, tm, reverse=True)
        for hh in range(HEADS):
            cs = pl.ds(hh * HD, HD)
            a = a_ref[:, cs]
            r = r_ref[:, cs].astype(F32)
            ig = ig_ref[:, cs].astype(F32)
            xc = xc_ref[:, cs].astype(F32)
            a2 = a * a
            one_m = 1.0 - a2
            live = jnp.logical_and(one_m > 0.0, jnp.logical_not(start))
            inv_mult = lax.rsqrt(jnp.where(live, one_m, 1.0))
            mult = jnp.where(live, one_m * inv_mult, jnp.where(start, 1.0, 0.0))
            dh = _from_segments(dh_s, hh, tm)
            d_mult = dh * ig * xc
            d_loga = dh * hbuf[pl.ds(7, tm), cs] * a - jnp.where(live, d_mult * a2 * inv_mult, 0.0)
            glam_ref[:, cs] += jnp.sum(d_loga * (LRU_C * r) * sneg[:, hh * HD:(hh + 1) * HD], axis=0, keepdims=True)
            d_rpre = d_loga * (-LRU_C * sp[:, hh * HD:(hh + 1) * HD]) * r * (1.0 - r)
            d_igpre = dh * mult * xc * ig * (1.0 - ig)
            gbrg_ref[pl.ds(hh, 1), :] += jnp.sum(d_rpre, axis=0, keepdims=True)
            gbig_ref[pl.ds(hh, 1), :] += jnp.sum(d_igpre, axis=0, keepdims=True)
            drm = d_rpre.astype(_MXU)
            dim = d_igpre.astype(_MXU)
            xcm = xc.astype(_MXU)
            gwrg_ref[hh] += _tn(xcm, drm)
            gwig_ref[hh] += _tn(xcm, dim)
            d_xc = dh * mult * ig + _nt(drm, wrg_ref[hh]) + _nt(dim, wig_ref[hh])
            gcb_ref[:, cs] += jnp.sum(d_xc, axis=0, keepdims=True)
            for k in range(CONV):
                gcw_ref[pl.ds(k, 1), cs] += jnp.sum(d_xc * lbuf[pl.ds(HALO - (CONV - 1) + k, tm), cs], axis=0,
                                                    keepdims=True)
            xbuf[pl.ds(0, tm), cs] = d_xc
        dzl = cw_ref[pl.ds(CONV - 1, 1), :] * xbuf[pl.ds(0, tm), :]
        for k in range(CONV - 1):
            dzl = dzl + cw_ref[pl.ds(k, 1), :] * xbuf[pl.ds(CONV - 1 - k, tm), :]
        dz_ref[:, pl.ds(PW, D)] = dzl.astype(_MXU)
        xbuf[pl.ds(tm, 8), :] = xbuf[pl.ds(0, 8), :]

        d_mixed = _nn(d_ypool, pprojT[...])
        pbuf[pl.ds(0, HALO), :] = zprev[:, 0:PW]
        pbuf[pl.ds(HALO, tm), :] = z_ref[:, pl.ds(0, PW)]
        pooled, mixed_pre = _pool_tile(pbuf, t0, tm, pw_ref, ps_ref)
        mp = jnp.concatenate(mixed_pre, axis=1)
        mx_ref[...] = (mp * ps_ref[...]).astype(_MXU)
        gps_ref[...] += jnp.sum(d_mixed * mp, axis=0, keepdims=True)
        d_mp = (d_mixed * ps_ref[...]).astype(_MXU)
        t = t0 + lax.broadcasted_iota(jnp.int32, (tm, GD), 0)
        d_pooled = []
        for g, w in enumerate(WINDOWS):
            dmg = d_mp[:, g * GD:(g + 1) * GD]
            gpw_ref[g] += _tn(pooled[g].astype(_MXU), dmg)
            dp = _nt(dmg, pw_ref[g])
            d_pooled.append(dp)
            qbuf[pl.ds(0, tm), pl.ds(g * GD, GD)] = dp / jnp.minimum(t + 1, w).astype(F32)
        for g, w in enumerate(WINDOWS):
            cs = pl.ds(g * GD, GD)
            acc = qbuf[pl.ds(0, tm), cs]
            for d in range(1, w):
                acc = acc + qbuf[pl.ds(d, tm), cs]
            dz_ref[:, cs] = (acc - d_pooled[g]).astype(_MXU)
        qbuf[pl.ds(tm, HALO), :] = qbuf[pl.ds(0, HALO), :]

    rev = lambda w: pl.BlockSpec((tm, w), lambda g: (nt - 1 - g, 0))
    prev = lambda rows, w: pl.BlockSpec((rows, w), lambda g: (jnp.maximum((nt - 1 - g) * (tm // rows) - 1, 0), 0))
    full = lambda a: pl.BlockSpec(a.shape, lambda g: (0,) * a.ndim)
    tok = lambda w, dt: jax.ShapeDtypeStruct((s, w), dt)
    acc_shapes = [(2, D), (1, D), (HEADS, HD), (HEADS, HD), (1, D), (CONV, D), (1, PW), (GROUPS, GD, GD),
                  (HEADS, HD, HD), (HEADS, HD, HD)]
    acc_specs = tuple(pl.BlockSpec(sh, lambda g, n=len(sh): (0,) * n) for sh in acc_shapes)
    seg_buf = pltpu.VMEM((HEADS, 8 * _seg_layout(tm)[1], HD), F32)
    a_in, r_in, ig_in, xc_in = saved
    return pl.pallas_call(
        body, name="mixer_bwd", grid=(nt,),
        out_shape=(tok(NIN, _MXU), tok(D, _MXU), tok(D, _MXU), tok(PW, _MXU))
        + tuple(jax.ShapeDtypeStruct(sh, F32) for sh in acc_shapes),
        in_specs=[rev(D), rev(NIN), prev(HALO, NIN), rev(D), prev(8, D), rev(D), rev(D), rev(D), rev(D), rev(D), rev(D),
                  pl.BlockSpec(memory_space=pl.ANY)] + [full(a) for a in small],
        out_specs=(rev(NIN), rev(D), rev(D), rev(PW)) + acc_specs,
        scratch_shapes=[pltpu.VMEM((D, PW), _MXU), pltpu.VMEM((D, D), _MXU), pltpu.VMEM((D, D), _MXU),
                        pltpu.VMEM((tm + HALO, PW), F32), pltpu.VMEM((tm + HALO, D), F32),
                        pltpu.VMEM((tm + 8, D), F32), pltpu.VMEM((tm + HALO, PW), F32), pltpu.VMEM((tm + 8, D), F32),
                        seg_buf, seg_buf, seg_buf, pltpu.VMEM((HEADS, tm, HD), F32), pltpu.VMEM((HEADS, tm, HD), F32),
                        pltpu.VMEM((8, D), F32), pltpu.SemaphoreType.DMA((3 * NDEV,))],
        compiler_params=_cparams(1),
    )(dh1, z, z, h, h, y_pool, y_lru, a_in, r_in, ig_in, xc_in, gw, *small)


def _split3(a):
    hi = a.astype(jnp.bfloat16).astype(F32)
    mid = (a - hi).astype(jnp.bfloat16).astype(F32)
    lo = (a - hi - mid).astype(jnp.bfloat16).astype(F32)
    return jnp.stack([hi, mid, lo])


def _small_pack(parts):
    flat = jnp.concatenate([a.reshape(-1) for a in parts])
    return jnp.pad(flat, (0, NDEV * SMALL_ROWS * D - flat.shape[0])).reshape(NDEV * SMALL_ROWS, D)


def _small_unpack(packed, shapes):
    flat = packed.reshape(-1)
    out, o = [], 0
    for sh in shapes:
        n = math.prod(sh)
        out.append(flat[o:o + n].reshape(sh))
        o += n
    return out


def kernel(x, p, norm1_g, w_in, b_gate, pool_w, pool_scale, pool_proj, conv_w, conv_b, w_rg, b_rg, w_ig, b_ig, lru_lambda, lru_proj, w_out, norm2_g, w_ffn_in, w_ffn_out, ple_norm_g, w_ple_gate, w_ple_proj, final_g, loss_target, m_norm1_g, m_w_in, m_b_gate, m_pool_w, m_pool_scale, m_pool_proj, m_conv_w, m_conv_b, m_w_rg, m_b_rg, m_w_ig, m_b_ig, m_lru_lambda, m_lru_proj, m_w_out, m_norm2_g, m_w_ffn_in, m_w_ffn_out, m_ple_norm_g, m_w_ple_gate, m_w_ple_proj, m_final_g, v_norm1_g, v_w_in, v_b_gate, v_pool_w, v_pool_scale, v_pool_proj, v_conv_w, v_conv_b, v_w_rg, v_b_rg, v_w_ig, v_b_ig, v_lru_lambda, v_lru_proj, v_w_out, v_norm2_g, v_w_ffn_in, v_w_ffn_out, v_ple_norm_g, v_w_ple_gate, v_w_ple_proj, v_final_g):
    axes = ("x", "y", "c")
    me = 4 * lax.axis_index("x") + 2 * lax.axis_index("y") + lax.axis_index("c")
    x2 = x[0]
    p2 = p[0, 0]
    tgt = loss_target[0]

    n_small = (CONV + 2) * 128
    small_terms = _split3(jnp.concatenate([conv_w[0].reshape(-1), b_gate[0].reshape(-1)]))
    small_rows = jnp.pad(small_terms, ((0, 16 - 3), (0, D - n_small)))
    own_first = jnp.concatenate([w_in[0].T.astype(_MXU), small_rows.astype(_MXU)], axis=0)
    own_second = jnp.concatenate([
        w_ffn_in[0].T.astype(_MXU),
        jnp.pad(pool_proj[0].T, ((0, 0), (0, D - PW))).astype(_MXU),
        jnp.pad(w_ple_proj[0].T, ((0, 0), (0, D - PLE))).astype(_MXU),
        lru_proj[0].astype(_MXU), w_out[0].astype(_MXU), w_ffn_out[0].astype(_MXU), w_ple_gate[0].astype(_MXU),
    ], axis=0)
    u, z, gw_first, gw = _inproj_fwd(x2, norm1_g, own_first, own_second)
    off = W_OFF["f32s"][0]
    st = gw_first[:, off:off + 3, :n_small].astype(F32)
    sf = st[:, 0] + st[:, 1] + st[:, 2]
    conv_w_full = sf[:, :CONV * 128].reshape(NDEV, CONV, 128).transpose(1, 0, 2).reshape(CONV, D)
    b_gate_full = sf[:, CONV * 128:].reshape(NDEV, 2, 128).transpose(1, 0, 2).reshape(2, D)

    small = (pool_w[0].astype(_MXU), pool_scale, conv_w_full, conv_b, w_rg[0].astype(_MXU), b_rg[0],
             w_ig[0].astype(_MXU), b_ig[0], lru_lambda, b_gate_full)

    h, yl, merged, y_pool, y_lru, h1, *saved = _mixer_fwd(z, x2, gw, small)
    v, gf, uf, h2 = _ffn_fwd(h1, norm2_g, gw)

    dh2, loss_blk, g_ple_norm, g_final, part_wpg, part_ple = _ple_loss_fwd_bwd(h2, p2, tgt, ple_norm_g,
                                                                               final_g.reshape(1, D), gw)
    dff, part_wffo = _ffn_bwd_hidden(dh2, gf, uf, gw)
    dh1, g_norm2 = _proj_norm_bwd(dff, h1, dh2, norm2_g, gw, "wffn", 2 * FF, "ffn_bwd_in")
    (dz, d_ylru, d_ypool, mixed, g_bgate, g_lam, g_brg, g_big, g_convb, g_convw, g_pscale, g_poolw, g_wrg,
     g_wig) = _mixer_bwd(dh1, z, h, y_pool, y_lru, saved, gw, small)
    grad_x, g_norm1, part_wout = _proj_norm_bwd(dz, x2, dh1, norm1_g, gw_first, "win", NIN, "inproj_bwd", lhs=merged)

    small_shapes = [(1, D), (GROUPS, GD, GD), (1, PW), (1, D), (HEADS, HD, HD), (HEADS, HD), (HEADS, HD, HD),
                    (HEADS, HD), (1, D), (1, D), (1, D), (1, D), (2, D), (CONV, D), (1, 1)]
    small_part = _small_pack([g_norm1, g_poolw, g_pscale, g_convb, g_wrg, g_brg, g_wig, g_big, g_lam, g_norm2,
                              g_ple_norm, g_final, g_bgate, g_convw, loss_blk[0:1, 0:1]])
    riders = [_grad_matmul(yl, d_ylru, "grad_lru_proj"), part_wout, _grad_matmul(d_ypool, mixed, "grad_pool_proj")]
    rs_wffn = _grad_matmul_rs(dff, v, "grad_w_ffn_in", 704, extras=[part_wffo, part_wpg, part_ple], narrow=_MXU)
    rs_win = _grad_matmul_rs(dz, u, "grad_w_in", 576, extras=riders + [small_part], narrow=_MXU, tail=SMALL_ROWS)

    def reduced(parts, name):
        return [_sum_arrays([t_own, landed[0], landed[1], landed[2]], "rs_sum_" + name + str(n))
                for n, (t_own, landed) in enumerate(parts)]

    red_wffn, = reduced(rs_wffn, "wffn")
    red_win, red_small = reduced(rs_win, "win")
    g_w_in = red_win[:576].T
    g_w_ffn_in = red_wffn[:704].T
    g_w_ffn_out = red_wffn[704:1056]
    g_w_ple_gate = red_wffn[1056:1184]
    g_w_ple_proj = red_wffn[1184:1312, :PLE].T
    g_lru_proj, g_w_out = red_win[576:704], red_win[704:832]
    g_pool_proj = red_win[832:960, :PW].T
    small_red = _all_gather_small(red_small)
    (gs_norm1, gs_poolw, gs_pscale, gs_convb, gs_wrg, gs_brg, gs_wig, gs_big, gs_lam, gs_norm2, gs_ple_norm,
     gs_final, gs_bgate, gs_convw, loss_sum) = _small_unpack(small_red, small_shapes)
    loss = loss_sum[0, 0]
    g_b_gate = lax.dynamic_slice_in_dim(gs_bgate, me * 128, 128, axis=1)
    g_conv_w = lax.dynamic_slice_in_dim(gs_convw, me * 128, 128, axis=1)

    grads = {
        "norm1_g": gs_norm1, "w_in": g_w_in[None], "b_gate": g_b_gate[None], "pool_w": gs_poolw[None],
        "pool_scale": gs_pscale, "pool_proj": g_pool_proj[None], "conv_w": g_conv_w[None], "conv_b": gs_convb,
        "w_rg": gs_wrg[None], "b_rg": gs_brg[None], "w_ig": gs_wig[None], "b_ig": gs_big[None], "lru_lambda": gs_lam,
        "lru_proj": g_lru_proj[None], "w_out": g_w_out[None], "norm2_g": gs_norm2, "w_ffn_in": g_w_ffn_in[None],
        "w_ffn_out": g_w_ffn_out[None], "ple_norm_g": gs_ple_norm, "w_ple_gate": g_w_ple_gate[None],
        "w_ple_proj": g_w_ple_proj[None], "final_g": gs_final.reshape(D),
    }
    weights = dict(norm1_g=norm1_g, w_in=w_in, b_gate=b_gate, pool_w=pool_w, pool_scale=pool_scale, pool_proj=pool_proj,
                   conv_w=conv_w, conv_b=conv_b, w_rg=w_rg, b_rg=b_rg, w_ig=w_ig, b_ig=b_ig, lru_lambda=lru_lambda,
                   lru_proj=lru_proj, w_out=w_out, norm2_g=norm2_g, w_ffn_in=w_ffn_in, w_ffn_out=w_ffn_out,
                   ple_norm_g=ple_norm_g, w_ple_gate=w_ple_gate, w_ple_proj=w_ple_proj, final_g=final_g)
    moments_m = dict(norm1_g=m_norm1_g, w_in=m_w_in, b_gate=m_b_gate, pool_w=m_pool_w, pool_scale=m_pool_scale,
                     pool_proj=m_pool_proj, conv_w=m_conv_w, conv_b=m_conv_b, w_rg=m_w_rg, b_rg=m_b_rg, w_ig=m_w_ig,
                     b_ig=m_b_ig, lru_lambda=m_lru_lambda, lru_proj=m_lru_proj, w_out=m_w_out, norm2_g=m_norm2_g,
                     w_ffn_in=m_w_ffn_in, w_ffn_out=m_w_ffn_out, ple_norm_g=m_ple_norm_g, w_ple_gate=m_w_ple_gate,
                     w_ple_proj=m_w_ple_proj, final_g=m_final_g)
    moments_v = dict(norm1_g=v_norm1_g, w_in=v_w_in, b_gate=v_b_gate, pool_w=v_pool_w, pool_scale=v_pool_scale,
                     pool_proj=v_pool_proj, conv_w=v_conv_w, conv_b=v_conv_b, w_rg=v_w_rg, b_rg=v_b_rg, w_ig=v_w_ig,
                     b_ig=v_b_ig, lru_lambda=v_lru_lambda, lru_proj=v_lru_proj, w_out=v_w_out, norm2_g=v_norm2_g,
                     w_ffn_in=v_w_ffn_in, w_ffn_out=v_w_ffn_out, ple_norm_g=v_ple_norm_g, w_ple_gate=v_w_ple_gate,
                     w_ple_proj=v_w_ple_proj, final_g=v_final_g)
    names = list(weights)
    big = ("w_in", "w_ffn_in", "w_ffn_out", "lru_proj", "w_out", "w_ple_gate", "pool_proj", "w_ple_proj")
    slab_space = {"w_in": red_win[:576], "w_ffn_in": red_wffn[:704]}
    delta, new_m, new_v = {}, {}, {}
    for n in big:
        sh = weights[n].shape
        if n in slab_space:
            as2d = lambda a: a[0].T
            back = lambda a: a.T[None]
            g2d = slab_space[n]
        else:
            as2d = lambda a: a.reshape(sh[-2], sh[-1])
            back = lambda a: a.reshape(sh)
            g2d = as2d(grads[n])
        d_, m_, v_ = _adamw(as2d(weights[n]), g2d, as2d(moments_m[n]), as2d(moments_v[n]), "adamw_" + n)
        delta[n], new_m[n], new_v[n] = back(d_), back(m_), back(v_)
    rest = [n for n in names if n not in big]
    rest_shapes = [weights[n].shape for n in rest]
    packed = [_small_pack([src[n] for n in rest]) for src in (weights, grads, moments_m, moments_v)]
    d_, m_, v_ = _adamw(*packed, "adamw_small")
    for n, a, b_, c_ in zip(rest, _small_unpack(d_, rest_shapes), _small_unpack(m_, rest_shapes),
                            _small_unpack(v_, rest_shapes)):
        delta[n], new_m[n], new_v[n] = a, b_, c_

    return (loss, grad_x[None], *[grads[n] for n in names], *[delta[n] for n in names],
            *[new_m[n] for n in names], *[new_v[n] for n in names])
```

```python
import functools
import math

import jax
import jax.numpy as jnp
from jax import lax
from jax.experimental import pallas as pl
from jax.experimental.pallas import tpu as pltpu

F32 = jnp.float32
D = 1024
NIN = 4608
PW = 512
FF = 2816
PLE = 256
HEADS, HD = 8, 128
GROUPS, GD = 4, 128
WINDOWS = (2, 4, 8, 16)
HALO = 16
CONV = 4
EPS = 1e-6
LRU_C = 8.0
NDEV = 8
MESH = pl.DeviceIdType.MESH

ADAM_LR, ADAM_B1, ADAM_B2, ADAM_EPS, ADAM_WD, ADAM_STEP = 0.001, 0.9, 0.999, 1e-08, 0.01, 10

_MXU = jnp.bfloat16
TM = 512
TM_SEQ = 256
VMEM_LIMIT = 56 * 1024 * 1024
W_FIRST = (("win", 576), ("f32s", 16))
W_SECOND = (("wffn", 704), ("pproj", 128), ("ple", 128), ("lru", 128), ("wout", 128), ("wffo", 352), ("wpg", 128))
W_OFF = {}
for _slabs in (W_FIRST, W_SECOND):
    _o = 0
    for _n, _r in _slabs:
        W_OFF[_n] = (_o, _r)
        _o += _r
SMALL_ROWS = 48


def _cparams(n_axes=1, vmem=VMEM_LIMIT):
    return pltpu.CompilerParams(dimension_semantics=("arbitrary",) * n_axes, vmem_limit_bytes=vmem)


def _my_pos():
    return lax.axis_index("x"), lax.axis_index("y"), lax.axis_index("c")


def _nt(a, b):
    return lax.dot_general(a, b, (((1,), (1,)), ((), ())), preferred_element_type=F32)


def _nn(a, b):
    return lax.dot_general(a, b, (((1,), (0,)), ((), ())), preferred_element_type=F32)


def _tn(a, b):
    return lax.dot_general(a, b, (((0,), (0,)), ((), ())), preferred_element_type=F32)


def _sigmoid(x):
    return 0.5 * jnp.tanh(0.5 * x) + 0.5


_GELU_K = math.sqrt(2.0 / math.pi)


def _gelu_and_grad(x):
    x2 = x * x
    inner = _GELU_K * (x + 0.044715 * x2 * x)
    t = jnp.tanh(inner)
    g = 0.5 * x * (1.0 + t)
    dg = 0.5 * (1.0 + t) + 0.5 * x * (1.0 - t * t) * _GELU_K * (1.0 + 3.0 * 0.044715 * x2)
    return g, dg


def _softplus_neg(lam):
    x = -lam
    t = jnp.exp(-jnp.abs(x))
    u = 1.0 + t
    l1p = jnp.where(u == 1.0, t, jnp.log(u) * t / (u - 1.0))
    return jnp.maximum(x, 0.0) + l1p, _sigmoid(x)


def _start_slab_loads(g_ref, name, dst_ref, sems, base, width=D):
    off, rows = W_OFF[name]
    copies = []
    for k in range(NDEV):
        if width == D:
            src = g_ref.at[k, pl.ds(off, rows), :]
        else:
            src = g_ref.at[k, pl.ds(off, rows), pl.ds(0, width)]
        cp = pltpu.make_async_copy(src, dst_ref.at[pl.ds(k * rows, rows), :], sems.at[base + k])
        cp.start()
        copies.append(cp)
    return copies


def _load_weights(g_ref, items, sems):
    copies = []
    for n, (name, dst, width) in enumerate(items):
        copies += _start_slab_loads(g_ref, name, dst, sems, n * NDEV, width)
    for cp in copies:
        cp.wait()


class _Gather:
    def __init__(self, own_ref, out_ref, stage, send_sems, recv_sems, local_sem):
        x, y, c = _my_pos()
        self.c = c
        self.me, self.sibling = (x, y, c), (x, y, 1 - c)
        self.chips = [(1 - x, y), (x, 1 - y), (1 - x, 1 - y)]
        self.own_ref, self.out_ref, self.stage = own_ref, out_ref, stage
        self.send_sems, self.recv_sems = send_sems, recv_sems
        self.mine = pltpu.make_async_copy(stage, self.slab(*self.me), local_sem)
        self.first = [self.copy(0, self.me, self.sibling, src=stage)] + [
            self.copy(1 + j, self.me, (*chip, c), src=stage) for j, chip in enumerate(self.chips)]
        self.passed = [self.copy(4 + j, (*chip, c), self.sibling) for j, chip in enumerate(self.chips)]

    def slab(self, px, py, pc):
        return self.out_ref.at[4 * px + 2 * py + pc]

    def copy(self, k, block, to, src=None):
        return pltpu.make_async_remote_copy(
            src_ref=self.slab(*block) if src is None else src, dst_ref=self.slab(*block),
            send_sem=self.send_sems.at[k], recv_sem=self.recv_sems.at[k], device_id=to, device_id_type=MESH)

    def send_mine(self, far=True):
        pltpu.sync_copy(self.own_ref, self.stage)
        self.mine.start()
        for cp in self.first[:3]:
            cp.start()
        if far:
            self.send_far()

    def send_far(self):
        self.first[3].start()

    def pass_on(self, js):
        for j in js:
            self.copy(1 + j, (*self.chips[j], self.c), self.me).wait_recv()
            self.passed[j].start()

    def wait_sibling(self):
        self.copy(0, self.sibling, self.me).wait_recv()

    def wait_passed(self, js):
        for j in js:
            self.copy(4 + j, (*self.chips[j], 1 - self.c), self.me).wait_recv()

    def finish_sends(self):
        for cp in self.first + self.passed:
            cp.wait_send()
        self.mine.wait()


def _all_gather_small(piece):
    rows = piece.shape[0]

    def body(p_ref, out_ref, send_sems, recv_sems, local_sem):
        x, y, c = _my_pos()
        me = 4 * x + 2 * y + c
        mine = pltpu.make_async_copy(p_ref, out_ref.at[pl.ds(pl.multiple_of(me * rows, 8), rows), :], local_sem)
        mine.start()
        sends = []
        peers = []
        for r in range(1, NDEV):
            px = 1 - x if (r >> 2) & 1 else x
            py = 1 - y if (r >> 1) & 1 else y
            pc = 1 - c if r & 1 else c
            peers.append((px, py, pc))
            cp = pltpu.make_async_remote_copy(
                src_ref=p_ref, dst_ref=out_ref.at[pl.ds(pl.multiple_of(me * rows, 8), rows), :],
                send_sem=send_sems.at[r - 1], recv_sem=recv_sems.at[r - 1], device_id=(px, py, pc),
                device_id_type=MESH)
            cp.start()
            sends.append(cp)
        for r, (px, py, pc) in enumerate(peers):
            them = 4 * px + 2 * py + pc
            pltpu.make_async_remote_copy(
                src_ref=p_ref, dst_ref=out_ref.at[pl.ds(pl.multiple_of(them * rows, 8), rows), :],
                send_sem=send_sems.at[r], recv_sem=recv_sems.at[r], device_id=(px, py, pc),
                device_id_type=MESH).wait_recv()
        for cp in sends:
            cp.wait_send()
        mine.wait()

    return pl.pallas_call(
        body, name="ag_small",
        out_shape=jax.ShapeDtypeStruct((NDEV * rows, piece.shape[1]), piece.dtype),
        in_specs=[pl.BlockSpec(memory_space=pltpu.VMEM)],
        out_specs=pl.BlockSpec(memory_space=pl.ANY),
        scratch_shapes=[pltpu.SemaphoreType.DMA((7,)), pltpu.SemaphoreType.DMA((7,)), pltpu.SemaphoreType.DMA],
    )(piece)


def _row_block(rows, target=512, mult=8):
    b = min(rows, target) // mult * mult
    while rows % b:
        b -= mult
    return b


def _sum_arrays(arrs, name, narrow=None, target=704):
    rows, cols = arrs[0].shape
    br = _row_block(rows, target, 16)
    n = len(arrs)

    def body(*refs):
        acc = refs[0][...].astype(F32)
        for r in refs[1:n]:
            acc = acc + r[...].astype(F32)
        refs[n][...] = acc
        if narrow is not None:
            refs[n + 1][...] = acc.astype(narrow)

    spec = pl.BlockSpec((br, cols), lambda i: (i, 0))
    shape = jax.ShapeDtypeStruct((rows, cols), F32)
    if narrow is None:
        out_shape, out_specs = shape, spec
    else:
        out_shape, out_specs = (shape, jax.ShapeDtypeStruct((rows, cols), narrow)), (spec, spec)
    return pl.pallas_call(
        body, name=name, grid=(rows // br,), out_shape=out_shape,
        in_specs=[spec] * n, out_specs=out_specs, compiler_params=_cparams(1),
    )(*arrs)


def _adamw(w, g, m, v, name):
    rows, cols = w.shape
    br = _row_block(rows, 256)

    def body(w_ref, g_ref, m_ref, v_ref, d_ref, nm_ref, nv_ref):
        g_ = g_ref[...]
        m_ = ADAM_B1 * m_ref[...] + (1.0 - ADAM_B1) * g_
        v_ = ADAM_B2 * v_ref[...] + (1.0 - ADAM_B2) * (g_ * g_)
        m_hat = m_ / (1.0 - ADAM_B1 ** ADAM_STEP)
        v_hat = v_ / (1.0 - ADAM_B2 ** ADAM_STEP)
        d_ref[...] = -ADAM_LR * (m_hat / (jnp.sqrt(v_hat) + ADAM_EPS) + ADAM_WD * w_ref[...])
        nm_ref[...] = m_
        nv_ref[...] = v_

    spec = pl.BlockSpec((br, cols), lambda i: (i, 0))
    shape = jax.ShapeDtypeStruct((rows, cols), F32)
    return pl.pallas_call(
        body, name=name, grid=(rows // br,), out_shape=(shape, shape, shape),
        in_specs=[spec] * 4, out_specs=(spec, spec, spec), compiler_params=_cparams(1),
    )(w, g, m, v)


_CHIP_FLIPS = (2, 3, 1, 0)


def _grad_matmul(lhs, rhs, name):
    s, r = lhs.shape
    k = rhs.shape[1]
    tm = min(TM, s)

    def body(l_ref, r_ref, o_ref):
        @pl.when(pl.program_id(0) == 0)
        def _():
            o_ref[...] = jnp.zeros_like(o_ref)

        o_ref[:, pl.ds(0, k)] += _tn(l_ref[...].astype(_MXU), r_ref[...].astype(_MXU))

    return pl.pallas_call(
        body, name=name, grid=(s // tm,),
        out_shape=jax.ShapeDtypeStruct((r, D), F32),
        in_specs=[pl.BlockSpec((tm, r), lambda i: (i, 0)), pl.BlockSpec((tm, k), lambda i: (i, 0))],
        out_specs=pl.BlockSpec((r, D), lambda i: (0, 0)),
        compiler_params=_cparams(1),
    )(lhs, rhs)


def _grad_matmul_rs(lhs, rhs, name, rows, extras=(), narrow=None, tail=0):
    s, r8 = lhs.shape
    k = rhs.shape[1]
    tm = min(TM, s)
    nt = s // tm
    cpb = 1
    nblk = 4 // cpb
    nx = len(extras)
    ers = [e.shape[0] // NDEV for e in extras]
    er = sum(ers)
    srows = rows + er
    brows = 2 * cpb * srows
    groups = [(0, srows - tail, F32 if narrow is None else narrow)] + ([(srows - tail, tail, F32)] if tail else [])
    ng = len(groups)
    mid = min(nt - 1, max(1, nt // 4))

    def flip_of(p):
        return jnp.where(p == 0, 2, jnp.where(p == 1, 3, jnp.where(p == 2, 1, 0)))

    def block_col(b):
        x, y, _ = _my_pos()
        return (2 * x + y) ^ flip_of(b)

    def body(*refs):
        l_ref, r_ref = refs[:2]
        x_refs = refs[2:2 + nx]
        rest = refs[2 + nx:]
        town_ref = rest[0]
        lici_refs = rest[1:1 + ng]
        acc, stage = rest[1 + ng:3 + ng]
        send_bufs = rest[3 + ng:3 + 2 * ng]
        dsend, drecv, isend, irecv, xsem = rest[3 + 2 * ng:]
        b = pl.program_id(0)
        i = pl.program_id(1)
        x, y, c = _my_pos()
        mine = 2 * x + y
        sibling = (x, y, 1 - c)

        def chip_at(p):
            return mine ^ _CHIP_FLIPS[p]

        def slab_rows(p, parity):
            within = 0 if cpb == 1 else (chip_at(p) & 1) * 2
            return pl.ds(pl.multiple_of((within + parity) * srows, 8), srows)

        def push(p, slot):
            return pltpu.make_async_remote_copy(
                src_ref=acc.at[slot, slab_rows(p, 1 - c), :], dst_ref=stage.at[p % 2],
                send_sem=dsend.at[p], recv_sem=drecv.at[p], device_id=sibling, device_id_type=MESH)

        def ici(p):
            ch = chip_at(p)
            return [pltpu.make_async_remote_copy(
                src_ref=send_bufs[g].at[p % 2], dst_ref=lici_refs[g].at[p], send_sem=isend.at[3 * g + p],
                recv_sem=irecv.at[3 * g + p], device_id=(ch >> 1, ch & 1, c), device_id_type=MESH) for g in range(ng)]

        def extra_loads(p, slot):
            copies = []
            within = 0 if cpb == 1 else (chip_at(p) & 1) * 2
            for parity in range(2):
                off = rows
                for n, (x_ref, e) in enumerate(zip(x_refs, ers)):
                    src = x_ref.at[pl.ds(pl.multiple_of((2 * chip_at(p) + parity) * e, 8), e), :]
                    dst = acc.at[slot, pl.ds(pl.multiple_of((within + parity) * srows + off, 8), e), :]
                    copies.append(pltpu.make_async_copy(src, dst, xsem.at[(p * 2 + parity) * nx + n]))
                    off += e
            return copies

        def combine(p, slot):
            push(p, slot).wait_recv()
            total = acc[slot, slab_rows(p, c), :] + stage[p % 2]
            if p == 3:
                stage[p % 2] = total
                pltpu.sync_copy(stage.at[p % 2], town_ref)
            else:
                if p == 2:
                    for cp in ici(0):
                        cp.wait_send()
                for g, (r0, n, dt) in enumerate(groups):
                    send_bufs[g][p % 2] = total[r0:r0 + n, :].astype(dt)
                for cp in ici(p):
                    cp.start()

        for bb in range(nblk):
            slot = bb % 2
            positions = list(range(bb * cpb, (bb + 1) * cpb))

            @pl.when(jnp.logical_and(b == bb, i == 0))
            def _(bb=bb, slot=slot, positions=positions):
                if bb >= 2:
                    for p in range((bb - 2) * cpb, (bb - 1) * cpb):
                        push(p, slot).wait_send()
                for q in range(2 * cpb):
                    acc[slot, pl.ds(q * srows, rows), :] = jnp.zeros((rows, D), F32)
                for p in positions:
                    for cp in extra_loads(p, slot):
                        cp.start()

            if bb >= 1:
                @pl.when(jnp.logical_and(b == bb, i == mid))
                def _(bb=bb):
                    for p in range((bb - 1) * cpb, bb * cpb):
                        combine(p, (bb - 1) % 2)

        res = _tn(l_ref[...].astype(_MXU), r_ref[...].astype(_MXU))
        slot_now = b % 2
        for q in range(2 * cpb):
            acc[slot_now, pl.ds(q * srows, rows), pl.ds(0, k)] += res[q * rows:(q + 1) * rows, :]

        for bb in range(nblk):
            slot = bb % 2
            positions = list(range(bb * cpb, (bb + 1) * cpb))

            @pl.when(jnp.logical_and(b == bb, i == nt - 1))
            def _(bb=bb, slot=slot, positions=positions):
                for p in positions:
                    for cp in extra_loads(p, slot):
                        cp.wait()
                for p in positions:
                    push(p, slot).start()
                if bb == nblk - 1:
                    for p in positions:
                        combine(p, slot)
                    for p in range(max(0, (nblk - 2) * cpb), 4):
                        push(p, slot).wait_send()
                    for p in range(1, 3):
                        for cp in ici(p):
                            cp.wait_send()
                    for p in range(3):
                        for cp in ici(p):
                            cp.wait_recv()

    in_specs = [pl.BlockSpec((tm, 2 * cpb * rows), lambda b, i: (i, block_col(b))),
                pl.BlockSpec((tm, k), lambda b, i: (i, 0))]
    any_spec = pl.BlockSpec(memory_space=pl.ANY)
    in_specs += [any_spec] * nx
    args = [lhs, rhs, *extras]
    outs = pl.pallas_call(
        body, name=name, grid=(nblk, nt),
        out_shape=(jax.ShapeDtypeStruct((srows, D), F32),)
        + tuple(jax.ShapeDtypeStruct((3, n, D), dt) for _, n, dt in groups),
        in_specs=in_specs, out_specs=(any_spec,) * (1 + ng),
        scratch_shapes=[pltpu.VMEM((2, brows, D), F32), pltpu.VMEM((2, srows, D), F32)]
        + [pltpu.VMEM((2, n, D), dt) for _, n, dt in groups]
        + [pltpu.SemaphoreType.DMA((4,)), pltpu.SemaphoreType.DMA((4,)), pltpu.SemaphoreType.DMA((3 * ng,)),
           pltpu.SemaphoreType.DMA((3 * ng,)), pltpu.SemaphoreType.DMA((max(1, 8 * nx),))],
        compiler_params=_cparams(2),
    )(*args)
    t_own = outs[0]
    return [(t_own[r0:r0 + n], landed) for (r0, n, _), landed in zip(groups, outs[1:])]


def _inproj_fwd(x, g1, own_first, own_second):
    s = x.shape[0]
    tm = min(2 * TM, s // 2)
    nt = s // tm
    assert nt % 2 == 0
    rows1, rows2 = own_first.shape[0], own_second.shape[0]
    wrows = W_OFF["win"][1]
    cw = 2 * wrows

    def chip_col(b):
        px, py, _ = _my_pos()
        return (2 * px + py) ^ jnp.where(b == 0, 0, jnp.where(b == 1, 2, jnp.where(b == 2, 1, 3)))

    def body(x_ref, g1_ref, own1_ref, own2_ref, u_ref, z_ref, gw1_ref, gw2_ref, w_vmem, u_buf, stage1, stage2, sems,
             usem, send1, recv1, local1, send2, recv2, local2):
        b = pl.program_id(0)
        i = pl.program_id(1)
        ga = _Gather(own1_ref, gw1_ref, stage1, send1, recv1, local1)
        gb = _Gather(own2_ref, gw2_ref, stage2, send2, recv2, local2)
        c = ga.c

        def load_chip(px, py, own_too):
            copies = []
            for pc in range(2):
                dst = w_vmem.at[pl.ds(pc * wrows, wrows), :]
                copies.append(pltpu.make_async_copy(gw1_ref.at[4 * px + 2 * py + pc, pl.ds(0, wrows), :], dst,
                                                    sems.at[pc]))
            if own_too:
                mine_dst = w_vmem.at[pl.ds(pl.multiple_of(c * wrows, 16), wrows), :]
                copies[0] = pltpu.make_async_copy(own1_ref.at[pl.ds(0, wrows), :], mine_dst, sems.at[0])
                theirs_dst = w_vmem.at[pl.ds(pl.multiple_of((1 - c) * wrows, 16), wrows), :]
                copies[1] = pltpu.make_async_copy(gw1_ref.at[4 * px + 2 * py + 1 - c, pl.ds(0, wrows), :], theirs_dst,
                                                  sems.at[1])
            for cp in copies:
                cp.start()
            for cp in copies:
                cp.wait()

        @pl.when(jnp.logical_and(b == 0, i == 0))
        def _():
            ga.send_mine(far=False)
            ga.wait_sibling()
            load_chip(ga.me[0], ga.me[1], True)

        @pl.when(jnp.logical_and(b == 0, i == nt // 4))
        def _():
            ga.send_far()
            gb.send_mine()

        @pl.when(jnp.logical_and(b == 0, i == (3 * nt) // 4))
        def _():
            ga.pass_on((0, 1))

        @pl.when(jnp.logical_and(b == 1, i == (3 * nt) // 4))
        def _():
            ga.pass_on((2,))

        for j in range(3):
            @pl.when(jnp.logical_and(b == j + 1, i == 0))
            def _(j=j):
                ga.wait_passed((j,))
                load_chip(ga.chips[j][0], ga.chips[j][1], False)

        @pl.when(jnp.logical_and(b == 2, i == nt // 2))
        def _():
            gb.pass_on((0, 1))

        @pl.when(jnp.logical_and(b == 3, i == (3 * nt) // 4))
        def _():
            gb.pass_on((2,))

        slot = i % 2

        def u_write(t, sl):
            return pltpu.make_async_copy(u_buf.at[sl], u_ref.at[pl.ds(pl.multiple_of(t * tm, tm), tm), :], usem.at[sl])

        def u_read(t, sl):
            return pltpu.make_async_copy(u_ref.at[pl.ds(pl.multiple_of(t * tm, tm), tm), :], u_buf.at[sl], usem.at[sl])

        @pl.when(b == 0)
        def _():
            @pl.when(i >= 2)
            def _():
                u_write(i - 2, slot).wait()

            xv = x_ref[...]
            inv = lax.rsqrt(jnp.mean(xv * xv, axis=-1, keepdims=True) + EPS)
            u_buf[slot] = (xv * inv * g1_ref[...]).astype(_MXU)
            u_write(i, slot).start()

            @pl.when(i == nt - 1)
            def _():
                u_write(i - 1, 1 - slot).wait()
                u_write(i, slot).wait()
                u_read(0, 0).start()

        @pl.when(b > 0)
        def _():
            u_read(i, slot).wait()

            @pl.when(jnp.logical_or(b < 3, i < nt - 1))
            def _():
                u_read((i + 1) % nt, 1 - slot).start()

        z_ref[...] = _nt(u_buf[slot], w_vmem[...])

        @pl.when(jnp.logical_and(b == 3, i == nt - 1))
        def _():
            ga.finish_sends()
            gb.wait_sibling()
            gb.wait_passed((0, 1, 2))
            gb.finish_sends()

    any_spec = pl.BlockSpec(memory_space=pl.ANY)
    dma7 = pltpu.SemaphoreType.DMA((7,))
    return pl.pallas_call(
        body, name="inproj_fwd", grid=(4, nt),
        out_shape=(jax.ShapeDtypeStruct((s, D), _MXU), jax.ShapeDtypeStruct((s, NIN), F32),
                   jax.ShapeDtypeStruct((NDEV, rows1, D), own_first.dtype),
                   jax.ShapeDtypeStruct((NDEV, rows2, D), own_second.dtype)),
        in_specs=[pl.BlockSpec((tm, D), lambda b, i: (jnp.where(b == 0, i, nt - 1), 0)),
                  pl.BlockSpec((1, D), lambda b, i: (0, 0)), any_spec, any_spec],
        out_specs=(any_spec, pl.BlockSpec((tm, cw), lambda b, i: (i, chip_col(b))), any_spec, any_spec),
        scratch_shapes=[pltpu.VMEM((cw, D), _MXU), pltpu.VMEM((2, tm, D), _MXU), pltpu.VMEM((rows1, D), own_first.dtype),
                        pltpu.VMEM((rows2, D), own_second.dtype), pltpu.SemaphoreType.DMA((2,)),
                        pltpu.SemaphoreType.DMA((2,)),
                        dma7, dma7, pltpu.SemaphoreType.DMA, dma7, dma7, pltpu.SemaphoreType.DMA],
        compiler_params=_cparams(2),
    )(x, g1, own_first, own_second)


def _pool_tile(pbuf, t0, tm, pw_ref, scale_ref):
    t = t0 + lax.broadcasted_iota(jnp.int32, (tm, GD), 0)
    pooled, mixed_pre = [], []
    for g, w in enumerate(WINDOWS):
        cs = pl.ds(g * GD, GD)
        cur = pbuf[pl.ds(HALO, tm), cs]
        acc = cur
        for d in range(1, w):
            acc = acc + pbuf[pl.ds(HALO - d, tm), cs]
        cnt = jnp.minimum(t + 1, w).astype(F32)
        pg = acc / cnt - cur
        pooled.append(pg)
        mixed_pre.append(_nn(pg.astype(_MXU), pw_ref[g]))
    return pooled, mixed_pre


def _lru_gates_head(hh, lbuf, start, tm, cw_ref, cb_ref, wrg_ref, brg_ref, wig_ref, big_ref, sp):
    cs = pl.ds(hh * HD, HD)
    xc = cb_ref[:, cs] + cw_ref[pl.ds(CONV - 1, 1), cs] * lbuf[pl.ds(HALO, tm), cs]
    for k in range(CONV - 1):
        xc = xc + cw_ref[pl.ds(k, 1), cs] * lbuf[pl.ds(HALO - (CONV - 1) + k, tm), cs]
    xcm = xc.astype(_MXU)
    r = _sigmoid(_nn(xcm, wrg_ref[hh]) + brg_ref[pl.ds(hh, 1), :])
    ig = _sigmoid(_nn(xcm, wig_ref[hh]) + big_ref[pl.ds(hh, 1), :])
    a = jnp.exp(-LRU_C * r * sp[:, hh * HD:(hh + 1) * HD])
    one_m = 1.0 - a * a
    live = jnp.logical_and(one_m > 0.0, jnp.logical_not(start))
    inv_mult = lax.rsqrt(jnp.where(live, one_m, 1.0))
    mult = jnp.where(live, one_m * inv_mult, jnp.where(start, 1.0, 0.0))
    return xc, r, ig, a, live, inv_mult, mult


def _seg_layout(tm):
    seg = tm // 8
    return seg, seg + 8


def _to_segments(dst_ref, hh, val, tm):
    seg, pitch = _seg_layout(tm)
    for s in range(8):
        dst_ref[hh, pl.ds(s * pitch, seg), :] = val[s * seg:(s + 1) * seg, :]


def _from_segments(src_ref, hh, tm):
    seg, pitch = _seg_layout(tm)
    return jnp.concatenate([src_ref[hh, pl.ds(s * pitch, seg), :] for s in range(8)], axis=0)


def _segment_scan(a_ref, b_ref, out_ref, hk, pk, carry_ref, tm, reverse):
    seg, pitch = _seg_layout(tm)
    row = lax.broadcasted_iota(jnp.int32, (8, HD), 0)
    order = range(seg - 1, -1, -1) if reverse else range(seg)
    for hh in range(HEADS):
        cs = pl.ds(hh * HD, HD)
        if reverse:
            a0 = a_ref[hh, pl.ds(0, 8, stride=pitch), :]
            a_wrap = jnp.where(row <= 6, pltpu.roll(a0, 7, 0), 1.0)
        hv = jnp.zeros((8, HD), F32)
        pv = jnp.ones((8, HD), F32)
        for k in order:
            if not reverse:
                av = a_ref[hh, pl.ds(k, 8, stride=pitch), :]
            elif k + 1 < seg:
                av = a_ref[hh, pl.ds(k + 1, 8, stride=pitch), :]
            else:
                av = a_wrap
            hv = av * hv + b_ref[hh, pl.ds(k, 8, stride=pitch), :]
            pv = av * pv
            hk[hh, pl.ds(8 * k, 8), :] = hv
            pk[hh, pl.ds(8 * k, 8), :] = pv
        for d in (1, 2, 4):
            if reverse:
                keep, sh = row < 8 - d, 8 - d
            else:
                keep, sh = row >= d, d
            hv = hv + pv * jnp.where(keep, pltpu.roll(hv, sh, 0), 0.0)
            pv = pv * jnp.where(keep, pltpu.roll(pv, sh, 0), 1.0)
        cin = carry_ref[:, cs]
        ends = hv + pv * cin
        if reverse:
            enter = jnp.where(row <= 6, pltpu.roll(ends, 7, 0), cin)
            carry_ref[:, cs] = jnp.broadcast_to((a0 * ends)[0:1, :], (8, HD))
        else:
            enter = jnp.where(row >= 1, pltpu.roll(ends, 1, 0), cin)
            carry_ref[:, cs] = jnp.broadcast_to(ends[7:8, :], (8, HD))
        for k in range(seg):
            out_ref[hh, pl.ds(k, 8, stride=pitch), :] = hk[hh, pl.ds(8 * k, 8), :] + pk[hh, pl.ds(8 * k, 8), :] * enter


def _mixer_fwd(z, x, gw, small):
    s = x.shape[0]
    tm = min(TM_SEQ, s)
    (pool_w, pool_scale, conv_w, conv_b, w_rg, b_rg, w_ig, b_ig, lam, b_gate) = small

    def body(z_ref, x_ref, gw_ref, pw_ref, ps_ref, cw_ref, cb_ref, wrg_ref, brg_ref, wig_ref, big_ref, lam_ref,
             bg_ref, h_ref, yl_ref, mg_ref, yp_ref, yr_ref, h1_ref, a_ref, r_ref, ig_ref, xc_ref,
             pprojT, lru_w, wout_w, pbuf, lbuf, a_s, b_s, h_s, hk, pk, hcar, sems):
        i = pl.program_id(0)
        t0 = i * tm

        @pl.when(i == 0)
        def _():
            _load_weights(gw_ref, [("pproj", pprojT, PW), ("lru", lru_w, D), ("wout", wout_w, D)], sems)
            pbuf[pl.ds(0, HALO), :] = jnp.zeros((HALO, PW), F32)
            lbuf[pl.ds(0, HALO), :] = jnp.zeros((HALO, D), F32)
            hcar[...] = jnp.zeros_like(hcar)

        pbuf[pl.ds(HALO, tm), :] = z_ref[:, pl.ds(0, PW)]
        _, mixed_pre = _pool_tile(pbuf, t0, tm, pw_ref, ps_ref)
        mixed = jnp.concatenate(mixed_pre, axis=1) * ps_ref[...]
        y_pool = _nt(mixed.astype(_MXU), pprojT[...])
        pbuf[pl.ds(0, HALO), :] = pbuf[pl.ds(tm, HALO), :]

        lbuf[pl.ds(HALO, tm), :] = z_ref[:, pl.ds(PW, D)]
        sp, _ = _softplus_neg(lam_ref[...])
        start = (t0 + lax.broadcasted_iota(jnp.int32, (tm, HD), 0)) == 0
        for hh in range(HEADS):
            xc, r, ig, a, _, _, mult = _lru_gates_head(hh, lbuf, start, tm, cw_ref, cb_ref, wrg_ref, brg_ref,
                                                       wig_ref, big_ref, sp)
            _to_segments(a_s, hh, a, tm)
            _to_segments(b_s, hh, mult * ig * xc, tm)
            cs = pl.ds(hh * HD, HD)
            a_ref[:, cs] = a
            r_ref[:, cs] = r.astype(_MXU)
            ig_ref[:, cs] = ig.astype(_MXU)
            xc_ref[:, cs] = xc.astype(_MXU)
        lbuf[pl.ds(0, HALO), :] = lbuf[pl.ds(tm, HALO), :]
        _segment_scan(a_s, b_s, h_s, hk, pk, hcar, tm, reverse=False)
        for hh in range(HEADS):
            h_ref[:, pl.ds(hh * HD, HD)] = _from_segments(h_s, hh, tm)
        gel, _ = _gelu_and_grad(z_ref[:, pl.ds(PW + D, D)])
        yl = (h_ref[...] * gel).astype(_MXU)
        yl_ref[...] = yl
        y_lru = _nn(yl, lru_w[...])

        g0 = _sigmoid(z_ref[:, pl.ds(PW + 2 * D, D)] + bg_ref[pl.ds(0, 1), :])
        g1 = _sigmoid(z_ref[:, pl.ds(PW + 3 * D, D)] + bg_ref[pl.ds(1, 1), :])
        merged = (g0 * y_pool + g1 * y_lru).astype(_MXU)
        mg_ref[...] = merged
        yp_ref[...] = y_pool.astype(_MXU)
        yr_ref[...] = y_lru.astype(_MXU)
        h1_ref[...] = x_ref[...] + _nn(merged, wout_w[...])

    tok = lambda w, dt: jax.ShapeDtypeStruct((s, w), dt)
    tspec = lambda w: pl.BlockSpec((tm, w), lambda i: (i, 0))
    full = lambda a: pl.BlockSpec(a.shape, lambda i: (0,) * a.ndim)
    seg_buf = pltpu.VMEM((HEADS, 8 * _seg_layout(tm)[1], HD), F32)
    return pl.pallas_call(
        body, name="mixer_fwd", grid=(s // tm,),
        out_shape=(tok(D, F32), tok(D, _MXU), tok(D, _MXU), tok(D, _MXU), tok(D, _MXU), tok(D, F32),
                   tok(D, F32), tok(D, _MXU), tok(D, _MXU), tok(D, _MXU)),
        in_specs=[tspec(NIN), tspec(D), pl.BlockSpec(memory_space=pl.ANY)] + [full(a) for a in small],
        out_specs=(tspec(D),) * 10,
        scratch_shapes=[pltpu.VMEM((D, PW), _MXU), pltpu.VMEM((D, D), _MXU), pltpu.VMEM((D, D), _MXU),
                        pltpu.VMEM((tm + HALO, PW), F32), pltpu.VMEM((tm + HALO, D), F32),
                        seg_buf, seg_buf, seg_buf, pltpu.VMEM((HEADS, tm, HD), F32), pltpu.VMEM((HEADS, tm, HD), F32),
                        pltpu.VMEM((8, D), F32), pltpu.SemaphoreType.DMA((3 * NDEV,))],
        compiler_params=_cparams(1),
    )(z, x, gw, *small)


def _ffn_fwd(h1, g2, gw):
    s = h1.shape[0]
    tm = min(TM, s)
    half = FF // 2

    def body(h1_ref, g2_ref, gw_ref, v_ref, gf_ref, uf_ref, h2_ref, wffnT, wffo, sems):
        @pl.when(pl.program_id(0) == 0)
        def _():
            _load_weights(gw_ref, [("wffn", wffnT, D), ("wffo", wffo, D)], sems)

        hv = h1_ref[...]
        inv = lax.rsqrt(jnp.mean(hv * hv, axis=-1, keepdims=True) + EPS)
        v = (hv * inv * g2_ref[...]).astype(_MXU)
        v_ref[...] = v
        acc = hv
        for ch in range(2):
            cs = pl.ds(ch * half, half)
            gf = _nt(v, wffnT[pl.ds(ch * half, half), :]).astype(_MXU)
            uf = _nt(v, wffnT[pl.ds(FF + ch * half, half), :]).astype(_MXU)
            gf_ref[:, cs] = gf
            uf_ref[:, cs] = uf
            gf32 = gf.astype(F32)
            act = (gf32 * _sigmoid(gf32) * uf.astype(F32)).astype(_MXU)
            acc = acc + _nn(act, wffo[pl.ds(ch * half, half), :])
        h2_ref[...] = acc

    tspec = lambda w: pl.BlockSpec((tm, w), lambda i: (i, 0))
    return pl.pallas_call(
        body, name="ffn_fwd", grid=(s // tm,),
        out_shape=(jax.ShapeDtypeStruct((s, D), _MXU), jax.ShapeDtypeStruct((s, FF), _MXU),
                   jax.ShapeDtypeStruct((s, FF), _MXU), jax.ShapeDtypeStruct((s, D), F32)),
        in_specs=[tspec(D), pl.BlockSpec((1, D), lambda i: (0, 0)), pl.BlockSpec(memory_space=pl.ANY)],
        out_specs=(tspec(D), tspec(FF), tspec(FF), tspec(D)),
        scratch_shapes=[pltpu.VMEM((2 * FF, D), _MXU), pltpu.VMEM((FF, D), _MXU), pltpu.SemaphoreType.DMA((2 * NDEV,))],
        compiler_params=_cparams(1),
    )(h1, g2, gw)


def _rms_bwd(dy, xn, inv, g):
    dg = jnp.sum(dy * xn, axis=0, keepdims=True)
    dxn = dy * g
    dx = inv * (dxn - xn * jnp.mean(dxn * xn, axis=-1, keepdims=True))
    return dx, dg


def _ple_loss_fwd_bwd(h2, p, target, g3, gfin, gw):
    s = h2.shape[0]
    tm = min(TM, s)

    def body(h2_ref, p_ref, t_ref, g3_ref, gf_ref, gw_ref,
             dh2_ref, loss_ref, dg3_ref, dgf_ref, gwpg_ref, gple_ref, wpg, pleT, sems):
        i = pl.program_id(0)

        @pl.when(i == 0)
        def _():
            _load_weights(gw_ref, [("wpg", wpg, D), ("ple", pleT, PLE)], sems)
            for ref in (loss_ref, dg3_ref, dgf_ref, gwpg_ref, gple_ref):
                ref[...] = jnp.zeros_like(ref)

        hv = h2_ref[...]
        inv3 = lax.rsqrt(jnp.mean(hv * hv, axis=-1, keepdims=True) + EPS)
        xn3 = hv * inv3
        n3 = (xn3 * g3_ref[...]).astype(_MXU)
        pg = _sigmoid(_nn(n3, wpg[...]))
        pm = p_ref[...].astype(_MXU)
        e = _nt(pm, pleT[...])
        h3 = hv + pg * e
        invf = lax.rsqrt(jnp.mean(h3 * h3, axis=-1, keepdims=True) + EPS)
        xf = h3 * invf
        diff = xf * gf_ref[...] - t_ref[...]
        loss_ref[...] += jnp.sum(diff * diff) * (0.5 / D)
        dh3, dgf = _rms_bwd(diff * (1.0 / D), xf, invf, gf_ref[...])
        dgf_ref[...] += dgf
        gple_ref[:, pl.ds(0, PLE)] += _tn((dh3 * pg).astype(_MXU), pm)
        dpg = (dh3 * e * pg * (1.0 - pg)).astype(_MXU)
        gwpg_ref[...] += _tn(n3, dpg)
        dn3 = _nt(dpg, wpg[...])
        dx3, dg3 = _rms_bwd(dn3, xn3, inv3, g3_ref[...])
        dg3_ref[...] += dg3
        dh2_ref[...] = dh3 + dx3

    tspec = lambda w: pl.BlockSpec((tm, w), lambda i: (i, 0))
    vec = pl.BlockSpec((1, D), lambda i: (0, 0))
    mat = pl.BlockSpec((D, D), lambda i: (0, 0))
    return pl.pallas_call(
        body, name="ple_loss", grid=(s // tm,),
        out_shape=(jax.ShapeDtypeStruct((s, D), F32), jax.ShapeDtypeStruct((8, 128), F32),
                   jax.ShapeDtypeStruct((1, D), F32), jax.ShapeDtypeStruct((1, D), F32),
                   jax.ShapeDtypeStruct((D, D), F32), jax.ShapeDtypeStruct((D, D), F32)),
        in_specs=[tspec(D), tspec(PLE), tspec(D), vec, vec, pl.BlockSpec(memory_space=pl.ANY)],
        out_specs=(tspec(D), pl.BlockSpec((8, 128), lambda i: (0, 0)), vec, vec, mat, mat),
        scratch_shapes=[pltpu.VMEM((D, D), _MXU), pltpu.VMEM((D, PLE), _MXU), pltpu.SemaphoreType.DMA((2 * NDEV,))],
        compiler_params=_cparams(1),
    )(h2, p, target, g3, gfin, gw)


def _ffn_bwd_hidden(dh2, gf, uf, gw):
    s = dh2.shape[0]
    tm = min(TM, s)
    nt = s // tm
    half = FF // 2

    def body(dh2_ref, gf_ref, uf_ref, gw_ref, dff_ref, gwo_ref, wffo, gacc, sems):
        i = pl.program_id(0)

        @pl.when(i == 0)
        def _():
            _load_weights(gw_ref, [("wffo", wffo, D)], sems)
            gacc[...] = jnp.zeros_like(gacc)

        dm = dh2_ref[...].astype(_MXU)
        for ch in range(2):
            cs = pl.ds(ch * half, half)
            dact = _nt(dm, wffo[cs, :])
            gfv = gf_ref[:, cs].astype(F32)
            ufv = uf_ref[:, cs].astype(F32)
            sg = _sigmoid(gfv)
            silu = gfv * sg
            gacc[cs, :] += _tn((silu * ufv).astype(_MXU), dm)
            dff_ref[:, pl.ds(ch * half, half)] = (dact * ufv * (sg * (1.0 + gfv * (1.0 - sg)))).astype(_MXU)
            dff_ref[:, pl.ds(FF + ch * half, half)] = (dact * silu).astype(_MXU)

        @pl.when(i == nt - 1)
        def _():
            pltpu.sync_copy(gacc, gwo_ref)

    tspec = lambda w: pl.BlockSpec((tm, w), lambda i: (i, 0))
    return pl.pallas_call(
        body, name="ffn_bwd_hidden", grid=(nt,),
        out_shape=(jax.ShapeDtypeStruct((s, 2 * FF), _MXU), jax.ShapeDtypeStruct((FF, D), F32)),
        in_specs=[tspec(D), tspec(FF), tspec(FF), pl.BlockSpec(memory_space=pl.ANY)],
        out_specs=(tspec(2 * FF), pl.BlockSpec(memory_space=pl.ANY)),
        scratch_shapes=[pltpu.VMEM((FF, D), _MXU), pltpu.VMEM((FF, D), F32), pltpu.SemaphoreType.DMA((NDEV,))],
        compiler_params=_cparams(1),
    )(dh2, gf, uf, gw)


def _proj_norm_bwd(dy, x, dres, g, gw, slab, width, name, lhs=None):
    s = x.shape[0]
    tm = min(TM, s)
    nl = 0 if lhs is None else 1

    def body(*refs):
        dy_ref, x_ref, dr_ref, g_ref = refs[:4]
        l_refs = refs[4:4 + nl]
        gw_ref, dx_ref, dg_ref = refs[4 + nl:7 + nl]
        gl_refs = refs[7 + nl:7 + 2 * nl]
        wT, sems = refs[7 + 2 * nl:]

        @pl.when(pl.program_id(0) == 0)
        def _():
            _load_weights(gw_ref, [(slab, wT, D)], sems)
            dg_ref[...] = jnp.zeros_like(dg_ref)
            for ref in gl_refs:
                ref[...] = jnp.zeros_like(ref)

        dv = _nn(dy_ref[...], wT[...])
        xv = x_ref[...]
        inv = lax.rsqrt(jnp.mean(xv * xv, axis=-1, keepdims=True) + EPS)
        dx, dg = _rms_bwd(dv, xv * inv, inv, g_ref[...])
        dg_ref[...] += dg
        dr = dr_ref[...]
        dx_ref[...] = dr + dx
        for l_ref, gl_ref in zip(l_refs, gl_refs):
            gl_ref[...] += _tn(l_ref[...], dr.astype(_MXU))

    tspec = lambda w: pl.BlockSpec((tm, w), lambda i: (i, 0))
    vec = pl.BlockSpec((1, D), lambda i: (0, 0))
    mat = pl.BlockSpec((D, D), lambda i: (0, 0))
    return pl.pallas_call(
        body, name=name, grid=(s // tm,),
        out_shape=(jax.ShapeDtypeStruct((s, D), F32), jax.ShapeDtypeStruct((1, D), F32))
        + (jax.ShapeDtypeStruct((D, D), F32),) * nl,
        in_specs=[tspec(width), tspec(D), tspec(D), vec] + [tspec(D)] * nl + [pl.BlockSpec(memory_space=pl.ANY)],
        out_specs=(tspec(D), vec) + (mat,) * nl,
        scratch_shapes=[pltpu.VMEM((width, D), _MXU), pltpu.SemaphoreType.DMA((NDEV,))],
        compiler_params=_cparams(1),
    )(dy, x, dres, g, *([] if lhs is None else [lhs]), gw)


def _mixer_bwd(dh1, z, h, y_pool, y_lru, saved, gw, small):
    s = dh1.shape[0]
    tm = min(TM_SEQ, s)
    nt = s // tm
    (pool_w, pool_scale, conv_w, conv_b, w_rg, b_rg, w_ig, b_ig, lam, b_gate) = small

    def body(dh1_ref, z_ref, zp_ref, h_ref, hp_ref, yp_ref, yr_ref, a_ref, r_ref, ig_ref, xc_ref, gw_ref,
             pw_ref, ps_ref, cw_ref, cb_ref, wrg_ref, brg_ref, wig_ref, big_ref, lam_ref, bg_ref,
             dz_ref, dyr_ref, dyp_ref, mx_ref,
             gbg_ref, glam_ref, gbrg_ref, gbig_ref, gcb_ref, gcw_ref, gps_ref, gpw_ref, gwrg_ref, gwig_ref,
             pprojT, lru_w, wout_w, pbuf, lbuf, hbuf, qbuf, xbuf, a_s, g_s, dh_s, hk, pk, dcar, sems):
        step = pl.program_id(0)
        i = nt - 1 - step
        t0 = i * tm

        @pl.when(step == 0)
        def _():
            _load_weights(gw_ref, [("pproj", pprojT, PW), ("lru", lru_w, D), ("wout", wout_w, D)], sems)
            for ref in (gbg_ref, glam_ref, gbrg_ref, gbig_ref, gcb_ref, gcw_ref, gps_ref, gpw_ref, gwrg_ref, gwig_ref):
                ref[...] = jnp.zeros_like(ref)
            qbuf[pl.ds(tm, HALO), :] = jnp.zeros((HALO, PW), F32)
            xbuf[pl.ds(tm, 8), :] = jnp.zeros((8, D), F32)
            dcar[...] = jnp.zeros_like(dcar)

        first = i == 0
        zprev = jnp.where(first, 0.0, zp_ref[...])
        hprev = jnp.where(first, 0.0, hp_ref[...])

        d_merged = _nt(dh1_ref[...].astype(_MXU), wout_w[...])

        g0 = _sigmoid(z_ref[:, pl.ds(PW + 2 * D, D)] + bg_ref[pl.ds(0, 1), :])
        g1 = _sigmoid(z_ref[:, pl.ds(PW + 3 * D, D)] + bg_ref[pl.ds(1, 1), :])
        dz0 = d_merged * yp_ref[...].astype(F32) * g0 * (1.0 - g0)
        dz1 = d_merged * yr_ref[...].astype(F32) * g1 * (1.0 - g1)
        dz_ref[:, pl.ds(PW + 2 * D, D)] = dz0.astype(_MXU)
        dz_ref[:, pl.ds(PW + 3 * D, D)] = dz1.astype(_MXU)
        gbg_ref[pl.ds(0, 1), :] += jnp.sum(dz0, axis=0, keepdims=True)
        gbg_ref[pl.ds(1, 1), :] += jnp.sum(dz1, axis=0, keepdims=True)
        d_ypool = (d_merged * g0).astype(_MXU)
        d_ylru = (d_merged * g1).astype(_MXU)
        dyp_ref[...] = d_ypool
        dyr_ref[...] = d_ylru

        d_yl = _nt(d_ylru, lru_w[...])
        gel, dgel = _gelu_and_grad(z_ref[:, pl.ds(PW + D, D)])
        dz_ref[:, pl.ds(PW + D, D)] = (d_yl * h_ref[...] * dgel).astype(_MXU)
        g_full = d_yl * gel
        lbuf[pl.ds(0, HALO), :] = zprev[:, PW:PW + D]
        lbuf[pl.ds(HALO, tm), :] = z_ref[:, pl.ds(PW, D)]
        hbuf[pl.ds(0, 8), :] = hprev
        hbuf[pl.ds(8, tm), :] = h_ref[...]
        sp, sneg = _softplus_neg(lam_ref[...])
        start = (t0 + lax.broadcasted_iota(jnp.int32, (tm, HD), 0)) == 0
        for hh in range(HEADS):
            cs = pl.ds(hh * HD, HD)
            _to_segments(a_s, hh, a_ref[:, cs], tm)
            _to_segments(g_s, hh, g_full[:, hh * HD:(hh + 1) * HD], tm)
        _segment_scan(a_s, g_s, dh_s, hk, pk, dcar, tm, reverse=True)
        for hh in range(HEADS):
            cs = pl.ds(hh * HD, HD)
            a = a_ref[:, cs]
            r = r_ref[:, cs].astype(F32)
            ig = ig_ref[:, cs].astype(F32)
            xc = xc_ref[:, cs].astype(F32)
            a2 = a * a
            one_m = 1.0 - a2
            live = jnp.logical_and(one_m > 0.0, jnp.logical_not(start))
            inv_mult = lax.rsqrt(jnp.where(live, one_m, 1.0))
            mult = jnp.where(live, one_m * inv_mult, jnp.where(start, 1.0, 0.0))
            dh = _from_segments(dh_s, hh, tm)
            d_mult = dh * ig * xc
            d_loga = dh * hbuf[pl.ds(7, tm), cs] * a - jnp.where(live, d_mult * a2 * inv_mult, 0.0)
            glam_ref[:, cs] += jnp.sum(d_loga * (LRU_C * r) * sneg[:, hh * HD:(hh + 1) * HD], axis=0, keepdims=True)
            d_rpre = d_loga * (-LRU_C * sp[:, hh * HD:(hh + 1) * HD]) * r * (1.0 - r)
            d_igpre = dh * mult * xc * ig * (1.0 - ig)
            gbrg_ref[pl.ds(hh, 1), :] += jnp.sum(d_rpre, axis=0, keepdims=True)
            gbig_ref[pl.ds(hh, 1), :] += jnp.sum(d_igpre, axis=0, keepdims=True)
            drm = d_rpre.astype(_MXU)
            dim = d_igpre.astype(_MXU)
            xcm = xc.astype(_MXU)
            gwrg_ref[hh] += _tn(xcm, drm)
            gwig_ref[hh] += _tn(xcm, dim)
            d_xc = dh * mult * ig + _nt(drm, wrg_ref[hh]) + _nt(dim, wig_ref[hh])
            gcb_ref[:, cs] += jnp.sum(d_xc, axis=0, keepdims=True)
            for k in range(CONV):
                gcw_ref[pl.ds(k, 1), cs] += jnp.sum(d_xc * lbuf[pl.ds(HALO - (CONV - 1) + k, tm), cs], axis=0,
                                                    keepdims=True)
            xbuf[pl.ds(0, tm), cs] = d_xc
        dzl = cw_ref[pl.ds(CONV - 1, 1), :] * xbuf[pl.ds(0, tm), :]
        for k in range(CONV - 1):
            dzl = dzl + cw_ref[pl.ds(k, 1), :] * xbuf[pl.ds(CONV - 1 - k, tm), :]
        dz_ref[:, pl.ds(PW, D)] = dzl.astype(_MXU)
        xbuf[pl.ds(tm, 8), :] = xbuf[pl.ds(0, 8), :]

        d_mixed = _nn(d_ypool, pprojT[...])
        pbuf[pl.ds(0, HALO), :] = zprev[:, 0:PW]
        pbuf[pl.ds(HALO, tm), :] = z_ref[:, pl.ds(0, PW)]
        pooled, mixed_pre = _pool_tile(pbuf, t0, tm, pw_ref, ps_ref)
        mp = jnp.concatenate(mixed_pre, axis=1)
        mx_ref[...] = (mp * ps_ref[...]).astype(_MXU)
        gps_ref[...] += jnp.sum(d_mixed * mp, axis=0, keepdims=True)
        d_mp = (d_mixed * ps_ref[...]).astype(_MXU)
        t = t0 + lax.broadcasted_iota(jnp.int32, (tm, GD), 0)
        d_pooled = []
        for g, w in enumerate(WINDOWS):
            dmg = d_mp[:, g * GD:(g + 1) * GD]
            gpw_ref[g] += _tn(pooled[g].astype(_MXU), dmg)
            dp = _nt(dmg, pw_ref[g])
            d_pooled.append(dp)
            qbuf[pl.ds(0, tm), pl.ds(g * GD, GD)] = dp / jnp.minimum(t + 1, w).astype(F32)
        for g, w in enumerate(WINDOWS):
            cs = pl.ds(g * GD, GD)
            acc = qbuf[pl.ds(0, tm), cs]
            for d in range(1, w):
                acc = acc + qbuf[pl.ds(d, tm), cs]
            dz_ref[:, cs] = (acc - d_pooled[g]).astype(_MXU)
        qbuf[pl.ds(tm, HALO), :] = qbuf[pl.ds(0, HALO), :]

    rev = lambda w: pl.BlockSpec((tm, w), lambda g: (nt - 1 - g, 0))
    prev = lambda rows, w: pl.BlockSpec((rows, w), lambda g: (jnp.maximum((nt - 1 - g) * (tm // rows) - 1, 0), 0))
    full = lambda a: pl.BlockSpec(a.shape, lambda g: (0,) * a.ndim)
    tok = lambda w, dt: jax.ShapeDtypeStruct((s, w), dt)
    acc_shapes = [(2, D), (1, D), (HEADS, HD), (HEADS, HD), (1, D), (CONV, D), (1, PW), (GROUPS, GD, GD),
                  (HEADS, HD, HD), (HEADS, HD, HD)]
    acc_specs = tuple(pl.BlockSpec(sh, lambda g, n=len(sh): (0,) * n) for sh in acc_shapes)
    seg_buf = pltpu.VMEM((HEADS, 8 * _seg_layout(tm)[1], HD), F32)
    a_in, r_in, ig_in, xc_in = saved
    return pl.pallas_call(
        body, name="mixer_bwd", grid=(nt,),
        out_shape=(tok(NIN, _MXU), tok(D, _MXU), tok(D, _MXU), tok(PW, _MXU))
        + tuple(jax.ShapeDtypeStruct(sh, F32) for sh in acc_shapes),
        in_specs=[rev(D), rev(NIN), prev(HALO, NIN), rev(D), prev(8, D), rev(D), rev(D), rev(D), rev(D), rev(D), rev(D),
                  pl.BlockSpec(memory_space=pl.ANY)] + [full(a) for a in small],
        out_specs=(rev(NIN), rev(D), rev(D), rev(PW)) + acc_specs,
        scratch_shapes=[pltpu.VMEM((D, PW), _MXU), pltpu.VMEM((D, D), _MXU), pltpu.VMEM((D, D), _MXU),
                        pltpu.VMEM((tm + HALO, PW), F32), pltpu.VMEM((tm + HALO, D), F32),
                        pltpu.VMEM((tm + 8, D), F32), pltpu.VMEM((tm + HALO, PW), F32), pltpu.VMEM((tm + 8, D), F32),
                        seg_buf, seg_buf, seg_buf, pltpu.VMEM((HEADS, tm, HD), F32), pltpu.VMEM((HEADS, tm, HD), F32),
                        pltpu.VMEM((8, D), F32), pltpu.SemaphoreType.DMA((3 * NDEV,))],
        compiler_params=_cparams(1),
    )(dh1, z, z, h, h, y_pool, y_lru, a_in, r_in, ig_in, xc_in, gw, *small)


def _split3(a):
    hi = a.astype(jnp.bfloat16).astype(F32)
    mid = (a - hi).astype(jnp.bfloat16).astype(F32)
    lo = (a - hi - mid).astype(jnp.bfloat16).astype(F32)
    return jnp.stack([hi, mid, lo])


def _small_pack(parts):
    flat = jnp.concatenate([a.reshape(-1) for a in parts])
    return jnp.pad(flat, (0, NDEV * SMALL_ROWS * D - flat.shape[0])).reshape(NDEV * SMALL_ROWS, D)


def _small_unpack(packed, shapes):
    flat = packed.reshape(-1)
    out, o = [], 0
    for sh in shapes:
        n = math.prod(sh)
        out.append(flat[o:o + n].reshape(sh))
        o += n
    return out


def kernel(x, p, norm1_g, w_in, b_gate, pool_w, pool_scale, pool_proj, conv_w, conv_b, w_rg, b_rg, w_ig, b_ig, lru_lambda, lru_proj, w_out, norm2_g, w_ffn_in, w_ffn_out, ple_norm_g, w_ple_gate, w_ple_proj, final_g, loss_target, m_norm1_g, m_w_in, m_b_gate, m_pool_w, m_pool_scale, m_pool_proj, m_conv_w, m_conv_b, m_w_rg, m_b_rg, m_w_ig, m_b_ig, m_lru_lambda, m_lru_proj, m_w_out, m_norm2_g, m_w_ffn_in, m_w_ffn_out, m_ple_norm_g, m_w_ple_gate, m_w_ple_proj, m_final_g, v_norm1_g, v_w_in, v_b_gate, v_pool_w, v_pool_scale, v_pool_proj, v_conv_w, v_conv_b, v_w_rg, v_b_rg, v_w_ig, v_b_ig, v_lru_lambda, v_lru_proj, v_w_out, v_norm2_g, v_w_ffn_in, v_w_ffn_out, v_ple_norm_g, v_w_ple_gate, v_w_ple_proj, v_final_g):
    axes = ("x", "y", "c")
    me = 4 * lax.axis_index("x") + 2 * lax.axis_index("y") + lax.axis_index("c")
    x2 = x[0]
    p2 = p[0, 0]
    tgt = loss_target[0]

    n_small = (CONV + 2) * 128
    small_terms = _split3(jnp.concatenate([conv_w[0].reshape(-1), b_gate[0].reshape(-1)]))
    small_rows = jnp.pad(small_terms, ((0, 16 - 3), (0, D - n_small)))
    own_first = jnp.concatenate([w_in[0].T.astype(_MXU), small_rows.astype(_MXU)], axis=0)
    own_second = jnp.concatenate([
        w_ffn_in[0].T.astype(_MXU),
        jnp.pad(pool_proj[0].T, ((0, 0), (0, D - PW))).astype(_MXU),
        jnp.pad(w_ple_proj[0].T, ((0, 0), (0, D - PLE))).astype(_MXU),
        lru_proj[0].astype(_MXU), w_out[0].astype(_MXU), w_ffn_out[0].astype(_MXU), w_ple_gate[0].astype(_MXU),
    ], axis=0)
    u, z, gw_first, gw = _inproj_fwd(x2, norm1_g, own_first, own_second)
    off = W_OFF["f32s"][0]
    st = gw_first[:, off:off + 3, :n_small].astype(F32)
    sf = st[:, 0] + st[:, 1] + st[:, 2]
    conv_w_full = sf[:, :CONV * 128].reshape(NDEV, CONV, 128).transpose(1, 0, 2).reshape(CONV, D)
    b_gate_full = sf[:, CONV * 128:].reshape(NDEV, 2, 128).transpose(1, 0, 2).reshape(2, D)

    small = (pool_w[0].astype(_MXU), pool_scale, conv_w_full, conv_b, w_rg[0].astype(_MXU), b_rg[0],
             w_ig[0].astype(_MXU), b_ig[0], lru_lambda, b_gate_full)

    h, yl, merged, y_pool, y_lru, h1, *saved = _mixer_fwd(z, x2, gw, small)
    v, gf, uf, h2 = _ffn_fwd(h1, norm2_g, gw)

    dh2, loss_blk, g_ple_norm, g_final, part_wpg, part_ple = _ple_loss_fwd_bwd(h2, p2, tgt, ple_norm_g,
                                                                               final_g.reshape(1, D), gw)
    dff, part_wffo = _ffn_bwd_hidden(dh2, gf, uf, gw)
    dh1, g_norm2 = _proj_norm_bwd(dff, h1, dh2, norm2_g, gw, "wffn", 2 * FF, "ffn_bwd_in")
    (dz, d_ylru, d_ypool, mixed, g_bgate, g_lam, g_brg, g_big, g_convb, g_convw, g_pscale, g_poolw, g_wrg,
     g_wig) = _mixer_bwd(dh1, z, h, y_pool, y_lru, saved, gw, small)
    grad_x, g_norm1, part_wout = _proj_norm_bwd(dz, x2, dh1, norm1_g, gw_first, "win", NIN, "inproj_bwd", lhs=merged)

    small_shapes = [(1, D), (GROUPS, GD, GD), (1, PW), (1, D), (HEADS, HD, HD), (HEADS, HD), (HEADS, HD, HD),
                    (HEADS, HD), (1, D), (1, D), (1, D), (1, D), (2, D), (CONV, D), (1, 1)]
    small_part = _small_pack([g_norm1, g_poolw, g_pscale, g_convb, g_wrg, g_brg, g_wig, g_big, g_lam, g_norm2,
                              g_ple_norm, g_final, g_bgate, g_convw, loss_blk[0:1, 0:1]])
    riders = [_grad_matmul(yl, d_ylru, "grad_lru_proj"), part_wout, _grad_matmul(d_ypool, mixed, "grad_pool_proj")]
    rs_wffn = _grad_matmul_rs(dff, v, "grad_w_ffn_in", 704, extras=[part_wffo, part_wpg, part_ple], narrow=_MXU)
    rs_win = _grad_matmul_rs(dz, u, "grad_w_in", 576, extras=riders + [small_part], narrow=_MXU, tail=SMALL_ROWS)

    def reduced(parts, name):
        return [_sum_arrays([t_own, landed[0], landed[1], landed[2]], "rs_sum_" + name + str(n))
                for n, (t_own, landed) in enumerate(parts)]

    red_wffn, = reduced(rs_wffn, "wffn")
    red_win, red_small = reduced(rs_win, "win")
    g_w_in = red_win[:576].T
    g_w_ffn_in = red_wffn[:704].T
    g_w_ffn_out = red_wffn[704:1056]
    g_w_ple_gate = red_wffn[1056:1184]
    g_w_ple_proj = red_wffn[1184:1312, :PLE].T
    g_lru_proj, g_w_out = red_win[576:704], red_win[704:832]
    g_pool_proj = red_win[832:960, :PW].T
    small_red = _all_gather_small(red_small)
    (gs_norm1, gs_poolw, gs_pscale, gs_convb, gs_wrg, gs_brg, gs_wig, gs_big, gs_lam, gs_norm2, gs_ple_norm,
     gs_final, gs_bgate, gs_convw, loss_sum) = _small_unpack(small_red, small_shapes)
    loss = loss_sum[0, 0]
    g_b_gate = lax.dynamic_slice_in_dim(gs_bgate, me * 128, 128, axis=1)
    g_conv_w = lax.dynamic_slice_in_dim(gs_convw, me * 128, 128, axis=1)

    grads = {
        "norm1_g": gs_norm1, "w_in": g_w_in[None], "b_gate": g_b_gate[None], "pool_w": gs_poolw[None],
        "pool_scale": gs_pscale, "pool_proj": g_pool_proj[None], "conv_w": g_conv_w[None], "conv_b": gs_convb,
        "w_rg": gs_wrg[None], "b_rg": gs_brg[None], "w_ig": gs_wig[None], "b_ig": gs_big[None], "lru_lambda": gs_lam,
        "lru_proj": g_lru_proj[None], "w_out": g_w_out[None], "norm2_g": gs_norm2, "w_ffn_in": g_w_ffn_in[None],
        "w_ffn_out": g_w_ffn_out[None], "ple_norm_g": gs_ple_norm, "w_ple_gate": g_w_ple_gate[None],
        "w_ple_proj": g_w_ple_proj[None], "final_g": gs_final.reshape(D),
    }
    weights = dict(norm1_g=norm1_g, w_in=w_in, b_gate=b_gate, pool_w=pool_w, pool_scale=pool_scale, pool_proj=pool_proj,
                   conv_w=conv_w, conv_b=conv_b, w_rg=w_rg, b_rg=b_rg, w_ig=w_ig, b_ig=b_ig, lru_lambda=lru_lambda,
                   lru_proj=lru_proj, w_out=w_out, norm2_g=norm2_g, w_ffn_in=w_ffn_in, w_ffn_out=w_ffn_out,
                   ple_norm_g=ple_norm_g, w_ple_gate=w_ple_gate, w_ple_proj=w_ple_proj, final_g=final_g)
    moments_m = dict(norm1_g=m_norm1_g, w_in=m_w_in, b_gate=m_b_gate, pool_w=m_pool_w, pool_scale=m_pool_scale,
                     pool_proj=m_pool_proj, conv_w=m_conv_w, conv_b=m_conv_b, w_rg=m_w_rg, b_rg=m_b_rg, w_ig=m_w_ig,
                     b_ig=m_b_ig, lru_lambda=m_lru_lambda, lru_proj=m_lru_proj, w_out=m_w_out, norm2_g=m_norm2_g,
                     w_ffn_in=m_w_ffn_in, w_ffn_out=m_w_ffn_out, ple_norm_g=m_ple_norm_g, w_ple_gate=m_w_ple_gate,
                     w_ple_proj=m_w_ple_proj, final_g=m_final_g)
    moments_v = dict(norm1_g=v_norm1_g, w_in=v_w_in, b_gate=v_b_gate, pool_w=v_pool_w, pool_scale=v_pool_scale,
                     pool_proj=v_pool_proj, conv_w=v_conv_w, conv_b=v_conv_b, w_rg=v_w_rg, b_rg=v_b_rg, w_ig=v_w_ig,
                     b_ig=v_b_ig, lru_lambda=v_lru_lambda, lru_proj=v_lru_proj, w_out=v_w_out, norm2_g=v_norm2_g,
                     w_ffn_in=v_w_ffn_in, w_ffn_out=v_w_ffn_out, ple_norm_g=v_ple_norm_g, w_ple_gate=v_w_ple_gate,
                     w_ple_proj=v_w_ple_proj, final_g=v_final_g)
    names = list(weights)
    big = ("w_in", "w_ffn_in", "w_ffn_out", "lru_proj", "w_out", "w_ple_gate", "pool_proj", "w_ple_proj")
    slab_space = {"w_in": red_win[:576], "w_ffn_in": red_wffn[:704]}
    delta, new_m, new_v = {}, {}, {}
    for n in big:
        sh = weights[n].shape
        if n in slab_space:
            as2d = lambda a: a[0].T
            back = lambda a: a.T[None]
            g2d = slab_space[n]
        else:
            as2d = lambda a: a.reshape(sh[-2], sh[-1])
            back = lambda a: a.reshape(sh)
            g2d = as2d(grads[n])
        d_, m_, v_ = _adamw(as2d(weights[n]), g2d, as2d(moments_m[n]), as2d(moments_v[n]), "adamw_" + n)
        delta[n], new_m[n], new_v[n] = back(d_), back(m_), back(v_)
    rest = [n for n in names if n not in big]
    rest_shapes = [weights[n].shape for n in rest]
    packed = [_small_pack([src[n] for n in rest]) for src in (weights, grads, moments_m, moments_v)]
    d_, m_, v_ = _adamw(*packed, "adamw_small")
    for n, a, b_, c_ in zip(rest, _small_unpack(d_, rest_shapes), _small_unpack(m_, rest_shapes),
                            _small_unpack(v_, rest_shapes)):
        delta[n], new_m[n], new_v[n] = a, b_, c_

    return (loss, grad_x[None], *[grads[n] for n in names], *[delta[n] for n in names],
            *[new_m[n] for n in names], *[new_v[n] for n in names])
```

```python
import functools
import math

import jax
import jax.numpy as jnp
from jax import lax
from jax.experimental import pallas as pl
from jax.experimental.pallas import tpu as pltpu

F32 = jnp.float32
D = 1024
NIN = 4608
PW = 512
FF = 2816
PLE = 256
HEADS, HD = 8, 128
GROUPS, GD = 4, 128
WINDOWS = (2, 4, 8, 16)
HALO = 16
CONV = 4
EPS = 1e-6
LRU_C = 8.0
NDEV = 8
MESH = pl.DeviceIdType.MESH

ADAM_LR, ADAM_B1, ADAM_B2, ADAM_EPS, ADAM_WD, ADAM_STEP = 0.001, 0.9, 0.999, 1e-08, 0.01, 10

_MXU = jnp.bfloat16
TM = 512
TM_SEQ = 256
VMEM_LIMIT = 56 * 1024 * 1024
W_FIRST = (("win", 576), ("f32s", 16))
W_SECOND = (("pproj", 128), ("lru", 128), ("wout", 128))
W_THIRD = (("wffn", 704), ("ple", 128), ("wffo", 352), ("wpg", 128))
W_OFF = {}
for _slabs in (W_FIRST, W_SECOND, W_THIRD):
    _o = 0
    for _n, _r in _slabs:
        W_OFF[_n] = (_o, _r)
        _o += _r
SMALL_ROWS = 48


def _cparams(n_axes=1, vmem=VMEM_LIMIT):
    return pltpu.CompilerParams(dimension_semantics=("arbitrary",) * n_axes, vmem_limit_bytes=vmem)


def _my_pos():
    return lax.axis_index("x"), lax.axis_index("y"), lax.axis_index("c")


def _nt(a, b):
    return lax.dot_general(a, b, (((1,), (1,)), ((), ())), preferred_element_type=F32)


def _nn(a, b):
    return lax.dot_general(a, b, (((1,), (0,)), ((), ())), preferred_element_type=F32)


def _tn(a, b):
    return lax.dot_general(a, b, (((0,), (0,)), ((), ())), preferred_element_type=F32)


def _sigmoid(x):
    return 0.5 * jnp.tanh(0.5 * x) + 0.5


_GELU_K = math.sqrt(2.0 / math.pi)


def _gelu_and_grad(x):
    x2 = x * x
    inner = _GELU_K * (x + 0.044715 * x2 * x)
    t = jnp.tanh(inner)
    g = 0.5 * x * (1.0 + t)
    dg = 0.5 * (1.0 + t) + 0.5 * x * (1.0 - t * t) * _GELU_K * (1.0 + 3.0 * 0.044715 * x2)
    return g, dg


def _softplus_neg(lam):
    x = -lam
    t = jnp.exp(-jnp.abs(x))
    u = 1.0 + t
    l1p = jnp.where(u == 1.0, t, jnp.log(u) * t / (u - 1.0))
    return jnp.maximum(x, 0.0) + l1p, _sigmoid(x)


def _start_slab_loads(g_ref, name, dst_ref, sems, base, width=D):
    off, rows = W_OFF[name]
    copies = []
    for k in range(NDEV):
        if width == D:
            src = g_ref.at[k, pl.ds(off, rows), :]
        else:
            src = g_ref.at[k, pl.ds(off, rows), pl.ds(0, width)]
        cp = pltpu.make_async_copy(src, dst_ref.at[pl.ds(k * rows, rows), :], sems.at[base + k])
        cp.start()
        copies.append(cp)
    return copies


def _load_weights(g_ref, items, sems):
    copies = []
    for n, (name, dst, width) in enumerate(items):
        copies += _start_slab_loads(g_ref, name, dst, sems, n * NDEV, width)
    for cp in copies:
        cp.wait()


class _Gather:
    def __init__(self, own_ref, out_ref, stage, send_sems, recv_sems, local_sem):
        x, y, c = _my_pos()
        self.c = c
        self.me, self.sibling = (x, y, c), (x, y, 1 - c)
        self.chips = [(1 - x, y), (x, 1 - y), (1 - x, 1 - y)]
        self.own_ref, self.out_ref, self.stage = own_ref, out_ref, stage
        self.send_sems, self.recv_sems = send_sems, recv_sems
        self.mine = pltpu.make_async_copy(stage, self.slab(*self.me), local_sem)
        self.first = [self.copy(0, self.me, self.sibling, src=stage)] + [
            self.copy(1 + j, self.me, (*chip, c), src=stage) for j, chip in enumerate(self.chips)]
        self.passed = [self.copy(4 + j, (*chip, c), self.sibling) for j, chip in enumerate(self.chips)]

    def slab(self, px, py, pc):
        return self.out_ref.at[4 * px + 2 * py + pc]

    def copy(self, k, block, to, src=None):
        return pltpu.make_async_remote_copy(
            src_ref=self.slab(*block) if src is None else src, dst_ref=self.slab(*block),
            send_sem=self.send_sems.at[k], recv_sem=self.recv_sems.at[k], device_id=to, device_id_type=MESH)

    def send_mine(self, far=True):
        pltpu.sync_copy(self.own_ref, self.stage)
        self.mine.start()
        for cp in self.first[:3]:
            cp.start()
        if far:
            self.send_far()

    def send_far(self):
        self.first[3].start()

    def pass_on(self, js):
        for j in js:
            self.copy(1 + j, (*self.chips[j], self.c), self.me).wait_recv()
            self.passed[j].start()

    def wait_sibling(self):
        self.copy(0, self.sibling, self.me).wait_recv()

    def wait_passed(self, js):
        for j in js:
            self.copy(4 + j, (*self.chips[j], 1 - self.c), self.me).wait_recv()

    def finish_sends(self):
        for cp in self.first + self.passed:
            cp.wait_send()
        self.mine.wait()


def _all_gather_small(piece):
    rows = piece.shape[0]

    def body(p_ref, out_ref, send_sems, recv_sems, local_sem):
        x, y, c = _my_pos()
        me = 4 * x + 2 * y + c
        mine = pltpu.make_async_copy(p_ref, out_ref.at[pl.ds(pl.multiple_of(me * rows, 8), rows), :], local_sem)
        mine.start()
        sends = []
        peers = []
        for r in range(1, NDEV):
            px = 1 - x if (r >> 2) & 1 else x
            py = 1 - y if (r >> 1) & 1 else y
            pc = 1 - c if r & 1 else c
            peers.append((px, py, pc))
            cp = pltpu.make_async_remote_copy(
                src_ref=p_ref, dst_ref=out_ref.at[pl.ds(pl.multiple_of(me * rows, 8), rows), :],
                send_sem=send_sems.at[r - 1], recv_sem=recv_sems.at[r - 1], device_id=(px, py, pc),
                device_id_type=MESH)
            cp.start()
            sends.append(cp)
        for r, (px, py, pc) in enumerate(peers):
            them = 4 * px + 2 * py + pc
            pltpu.make_async_remote_copy(
                src_ref=p_ref, dst_ref=out_ref.at[pl.ds(pl.multiple_of(them * rows, 8), rows), :],
                send_sem=send_sems.at[r], recv_sem=recv_sems.at[r], device_id=(px, py, pc),
                device_id_type=MESH).wait_recv()
        for cp in sends:
            cp.wait_send()
        mine.wait()

    return pl.pallas_call(
        body, name="ag_small",
        out_shape=jax.ShapeDtypeStruct((NDEV * rows, piece.shape[1]), piece.dtype),
        in_specs=[pl.BlockSpec(memory_space=pltpu.VMEM)],
        out_specs=pl.BlockSpec(memory_space=pl.ANY),
        scratch_shapes=[pltpu.SemaphoreType.DMA((7,)), pltpu.SemaphoreType.DMA((7,)), pltpu.SemaphoreType.DMA],
    )(piece)


def _row_block(rows, target=512, mult=8):
    b = min(rows, target) // mult * mult
    while rows % b:
        b -= mult
    return b


def _sum_arrays(arrs, name, narrow=None, target=704):
    rows, cols = arrs[0].shape
    br = _row_block(rows, target, 16)
    n = len(arrs)

    def body(*refs):
        acc = refs[0][...].astype(F32)
        for r in refs[1:n]:
            acc = acc + r[...].astype(F32)
        refs[n][...] = acc
        if narrow is not None:
            refs[n + 1][...] = acc.astype(narrow)

    spec = pl.BlockSpec((br, cols), lambda i: (i, 0))
    shape = jax.ShapeDtypeStruct((rows, cols), F32)
    if narrow is None:
        out_shape, out_specs = shape, spec
    else:
        out_shape, out_specs = (shape, jax.ShapeDtypeStruct((rows, cols), narrow)), (spec, spec)
    return pl.pallas_call(
        body, name=name, grid=(rows // br,), out_shape=out_shape,
        in_specs=[spec] * n, out_specs=out_specs, compiler_params=_cparams(1),
    )(*arrs)


def _adamw(w, g, m, v, name):
    rows, cols = w.shape
    br = _row_block(rows, 256)

    def body(w_ref, g_ref, m_ref, v_ref, d_ref, nm_ref, nv_ref):
        g_ = g_ref[...]
        m_ = ADAM_B1 * m_ref[...] + (1.0 - ADAM_B1) * g_
        v_ = ADAM_B2 * v_ref[...] + (1.0 - ADAM_B2) * (g_ * g_)
        m_hat = m_ / (1.0 - ADAM_B1 ** ADAM_STEP)
        v_hat = v_ / (1.0 - ADAM_B2 ** ADAM_STEP)
        d_ref[...] = -ADAM_LR * (m_hat / (jnp.sqrt(v_hat) + ADAM_EPS) + ADAM_WD * w_ref[...])
        nm_ref[...] = m_
        nv_ref[...] = v_

    spec = pl.BlockSpec((br, cols), lambda i: (i, 0))
    shape = jax.ShapeDtypeStruct((rows, cols), F32)
    return pl.pallas_call(
        body, name=name, grid=(rows // br,), out_shape=(shape, shape, shape),
        in_specs=[spec] * 4, out_specs=(spec, spec, spec), compiler_params=_cparams(1),
    )(w, g, m, v)


_CHIP_FLIPS = (2, 3, 1, 0)


def _grad_matmul(lhs, rhs, name):
    s, r = lhs.shape
    k = rhs.shape[1]
    tm = min(TM, s)

    def body(l_ref, r_ref, o_ref):
        @pl.when(pl.program_id(0) == 0)
        def _():
            o_ref[...] = jnp.zeros_like(o_ref)

        o_ref[:, pl.ds(0, k)] += _tn(l_ref[...].astype(_MXU), r_ref[...].astype(_MXU))

    return pl.pallas_call(
        body, name=name, grid=(s // tm,),
        out_shape=jax.ShapeDtypeStruct((r, D), F32),
        in_specs=[pl.BlockSpec((tm, r), lambda i: (i, 0)), pl.BlockSpec((tm, k), lambda i: (i, 0))],
        out_specs=pl.BlockSpec((r, D), lambda i: (0, 0)),
        compiler_params=_cparams(1),
    )(lhs, rhs)


def _grad_matmul_rs(lhs, rhs, name, rows, extras=(), narrow=None, tail=0):
    s, r8 = lhs.shape
    k = rhs.shape[1]
    tm = min(TM, s)
    nt = s // tm
    cpb = 1
    nblk = 4 // cpb
    nx = len(extras)
    ers = [e.shape[0] // NDEV for e in extras]
    er = sum(ers)
    srows = rows + er
    brows = 2 * cpb * srows
    groups = [(0, srows - tail, F32 if narrow is None else narrow)] + ([(srows - tail, tail, F32)] if tail else [])
    ng = len(groups)
    mid = min(nt - 1, max(1, nt // 4))

    def flip_of(p):
        return jnp.where(p == 0, 2, jnp.where(p == 1, 3, jnp.where(p == 2, 1, 0)))

    def block_col(b):
        x, y, _ = _my_pos()
        return (2 * x + y) ^ flip_of(b)

    def body(*refs):
        l_ref, r_ref = refs[:2]
        x_refs = refs[2:2 + nx]
        rest = refs[2 + nx:]
        town_ref = rest[0]
        lici_refs = rest[1:1 + ng]
        acc, stage = rest[1 + ng:3 + ng]
        send_bufs = rest[3 + ng:3 + 2 * ng]
        dsend, drecv, isend, irecv, xsem = rest[3 + 2 * ng:]
        b = pl.program_id(0)
        i = pl.program_id(1)
        x, y, c = _my_pos()
        mine = 2 * x + y
        sibling = (x, y, 1 - c)

        def chip_at(p):
            return mine ^ _CHIP_FLIPS[p]

        def slab_rows(p, parity):
            within = 0 if cpb == 1 else (chip_at(p) & 1) * 2
            return pl.ds(pl.multiple_of((within + parity) * srows, 8), srows)

        def push(p, slot):
            return pltpu.make_async_remote_copy(
                src_ref=acc.at[slot, slab_rows(p, 1 - c), :], dst_ref=stage.at[p % 2],
                send_sem=dsend.at[p], recv_sem=drecv.at[p], device_id=sibling, device_id_type=MESH)

        def ici(p):
            ch = chip_at(p)
            return [pltpu.make_async_remote_copy(
                src_ref=send_bufs[g].at[p % 2], dst_ref=lici_refs[g].at[p], send_sem=isend.at[3 * g + p],
                recv_sem=irecv.at[3 * g + p], device_id=(ch >> 1, ch & 1, c), device_id_type=MESH) for g in range(ng)]

        def extra_loads(p, slot):
            copies = []
            within = 0 if cpb == 1 else (chip_at(p) & 1) * 2
            for parity in range(2):
                off = rows
                for n, (x_ref, e) in enumerate(zip(x_refs, ers)):
                    src = x_ref.at[pl.ds(pl.multiple_of((2 * chip_at(p) + parity) * e, 8), e), :]
                    dst = acc.at[slot, pl.ds(pl.multiple_of((within + parity) * srows + off, 8), e), :]
                    copies.append(pltpu.make_async_copy(src, dst, xsem.at[(p * 2 + parity) * nx + n]))
                    off += e
            return copies

        def combine(p, slot):
            push(p, slot).wait_recv()
            total = acc[slot, slab_rows(p, c), :] + stage[p % 2]
            if p == 3:
                stage[p % 2] = total
                pltpu.sync_copy(stage.at[p % 2], town_ref)
            else:
                if p == 2:
                    for cp in ici(0):
                        cp.wait_send()
                for g, (r0, n, dt) in enumerate(groups):
                    send_bufs[g][p % 2] = total[r0:r0 + n, :].astype(dt)
                for cp in ici(p):
                    cp.start()

        for bb in range(nblk):
            slot = bb % 2
            positions = list(range(bb * cpb, (bb + 1) * cpb))

            @pl.when(jnp.logical_and(b == bb, i == 0))
            def _(bb=bb, slot=slot, positions=positions):
                if bb >= 2:
                    for p in range((bb - 2) * cpb, (bb - 1) * cpb):
                        push(p, slot).wait_send()
                for q in range(2 * cpb):
                    acc[slot, pl.ds(q * srows, rows), :] = jnp.zeros((rows, D), F32)
                for p in positions:
                    for cp in extra_loads(p, slot):
                        cp.start()

            if bb >= 1:
                @pl.when(jnp.logical_and(b == bb, i == mid))
                def _(bb=bb):
                    for p in range((bb - 1) * cpb, bb * cpb):
                        combine(p, (bb - 1) % 2)

        res = _tn(l_ref[...].astype(_MXU), r_ref[...].astype(_MXU))
        slot_now = b % 2
        for q in range(2 * cpb):
            acc[slot_now, pl.ds(q * srows, rows), pl.ds(0, k)] += res[q * rows:(q + 1) * rows, :]

        for bb in range(nblk):
            slot = bb % 2
            positions = list(range(bb * cpb, (bb + 1) * cpb))

            @pl.when(jnp.logical_and(b == bb, i == nt - 1))
            def _(bb=bb, slot=slot, positions=positions):
                for p in positions:
                    for cp in extra_loads(p, slot):
                        cp.wait()
                for p in positions:
                    push(p, slot).start()
                if bb == nblk - 1:
                    for p in positions:
                        combine(p, slot)
                    for p in range(max(0, (nblk - 2) * cpb), 4):
                        push(p, slot).wait_send()
                    for p in range(1, 3):
                        for cp in ici(p):
                            cp.wait_send()
                    for p in range(3):
                        for cp in ici(p):
                            cp.wait_recv()

    in_specs = [pl.BlockSpec((tm, 2 * cpb * rows), lambda b, i: (i, block_col(b))),
                pl.BlockSpec((tm, k), lambda b, i: (i, 0))]
    any_spec = pl.BlockSpec(memory_space=pl.ANY)
    in_specs += [any_spec] * nx
    args = [lhs, rhs, *extras]
    outs = pl.pallas_call(
        body, name=name, grid=(nblk, nt),
        out_shape=(jax.ShapeDtypeStruct((srows, D), F32),)
        + tuple(jax.ShapeDtypeStruct((3, n, D), dt) for _, n, dt in groups),
        in_specs=in_specs, out_specs=(any_spec,) * (1 + ng),
        scratch_shapes=[pltpu.VMEM((2, brows, D), F32), pltpu.VMEM((2, srows, D), F32)]
        + [pltpu.VMEM((2, n, D), dt) for _, n, dt in groups]
        + [pltpu.SemaphoreType.DMA((4,)), pltpu.SemaphoreType.DMA((4,)), pltpu.SemaphoreType.DMA((3 * ng,)),
           pltpu.SemaphoreType.DMA((3 * ng,)), pltpu.SemaphoreType.DMA((max(1, 8 * nx),))],
        compiler_params=_cparams(2),
    )(*args)
    t_own = outs[0]
    return [(t_own[r0:r0 + n], landed) for (r0, n, _), landed in zip(groups, outs[1:])]


def _inproj_fwd(x, g1, own_first, own_second):
    s = x.shape[0]
    tm = min(2 * TM, s // 2)
    nt = s // tm
    assert nt % 2 == 0
    rows1, rows2 = own_first.shape[0], own_second.shape[0]
    wrows = W_OFF["win"][1]
    cw = 2 * wrows

    def chip_col(b):
        px, py, _ = _my_pos()
        return (2 * px + py) ^ jnp.where(b == 0, 0, jnp.where(b == 1, 2, jnp.where(b == 2, 1, 3)))

    def body(x_ref, g1_ref, own1_ref, own2_ref, u_ref, z_ref, gw1_ref, gw2_ref, w_vmem, u_buf, stage1, stage2, sems,
             usem, send1, recv1, local1, send2, recv2, local2):
        b = pl.program_id(0)
        i = pl.program_id(1)
        ga = _Gather(own1_ref, gw1_ref, stage1, send1, recv1, local1)
        gb = _Gather(own2_ref, gw2_ref, stage2, send2, recv2, local2)
        c = ga.c

        def load_chip(px, py, own_too):
            copies = []
            for pc in range(2):
                dst = w_vmem.at[pl.ds(pc * wrows, wrows), :]
                copies.append(pltpu.make_async_copy(gw1_ref.at[4 * px + 2 * py + pc, pl.ds(0, wrows), :], dst,
                                                    sems.at[pc]))
            if own_too:
                mine_dst = w_vmem.at[pl.ds(pl.multiple_of(c * wrows, 16), wrows), :]
                copies[0] = pltpu.make_async_copy(own1_ref.at[pl.ds(0, wrows), :], mine_dst, sems.at[0])
                theirs_dst = w_vmem.at[pl.ds(pl.multiple_of((1 - c) * wrows, 16), wrows), :]
                copies[1] = pltpu.make_async_copy(gw1_ref.at[4 * px + 2 * py + 1 - c, pl.ds(0, wrows), :], theirs_dst,
                                                  sems.at[1])
            for cp in copies:
                cp.start()
            for cp in copies:
                cp.wait()

        @pl.when(jnp.logical_and(b == 0, i == 0))
        def _():
            ga.send_mine(far=False)
            ga.wait_sibling()
            load_chip(ga.me[0], ga.me[1], True)

        @pl.when(jnp.logical_and(b == 0, i == nt // 4))
        def _():
            ga.send_far()
            gb.send_mine()

        @pl.when(jnp.logical_and(b == 0, i == (3 * nt) // 4))
        def _():
            ga.pass_on((0, 1))

        @pl.when(jnp.logical_and(b == 1, i == (3 * nt) // 4))
        def _():
            ga.pass_on((2,))

        for j in range(3):
            @pl.when(jnp.logical_and(b == j + 1, i == 0))
            def _(j=j):
                ga.wait_passed((j,))
                load_chip(ga.chips[j][0], ga.chips[j][1], False)

        @pl.when(jnp.logical_and(b == 2, i == nt // 2))
        def _():
            gb.pass_on((0, 1))

        @pl.when(jnp.logical_and(b == 3, i == (3 * nt) // 4))
        def _():
            gb.pass_on((2,))

        slot = i % 2

        def u_write(t, sl):
            return pltpu.make_async_copy(u_buf.at[sl], u_ref.at[pl.ds(pl.multiple_of(t * tm, tm), tm), :], usem.at[sl])

        def u_read(t, sl):
            return pltpu.make_async_copy(u_ref.at[pl.ds(pl.multiple_of(t * tm, tm), tm), :], u_buf.at[sl], usem.at[sl])

        @pl.when(b == 0)
        def _():
            @pl.when(i >= 2)
            def _():
                u_write(i - 2, slot).wait()

            xv = x_ref[...]
            inv = lax.rsqrt(jnp.mean(xv * xv, axis=-1, keepdims=True) + EPS)
            u_buf[slot] = (xv * inv * g1_ref[...]).astype(_MXU)
            u_write(i, slot).start()

            @pl.when(i == nt - 1)
            def _():
                u_write(i - 1, 1 - slot).wait()
                u_write(i, slot).wait()
                u_read(0, 0).start()

        @pl.when(b > 0)
        def _():
            u_read(i, slot).wait()

            @pl.when(jnp.logical_or(b < 3, i < nt - 1))
            def _():
                u_read((i + 1) % nt, 1 - slot).start()

        z_ref[...] = _nt(u_buf[slot], w_vmem[...])

        @pl.when(jnp.logical_and(b == 3, i == nt - 1))
        def _():
            ga.finish_sends()
            gb.wait_sibling()
            gb.wait_passed((0, 1, 2))
            gb.finish_sends()

    any_spec = pl.BlockSpec(memory_space=pl.ANY)
    dma7 = pltpu.SemaphoreType.DMA((7,))
    return pl.pallas_call(
        body, name="inproj_fwd", grid=(4, nt),
        out_shape=(jax.ShapeDtypeStruct((s, D), _MXU), jax.ShapeDtypeStruct((s, NIN), F32),
                   jax.ShapeDtypeStruct((NDEV, rows1, D), own_first.dtype),
                   jax.ShapeDtypeStruct((NDEV, rows2, D), own_second.dtype)),
        in_specs=[pl.BlockSpec((tm, D), lambda b, i: (jnp.where(b == 0, i, nt - 1), 0)),
                  pl.BlockSpec((1, D), lambda b, i: (0, 0)), any_spec, any_spec],
        out_specs=(any_spec, pl.BlockSpec((tm, cw), lambda b, i: (i, chip_col(b))), any_spec, any_spec),
        scratch_shapes=[pltpu.VMEM((cw, D), _MXU), pltpu.VMEM((2, tm, D), _MXU), pltpu.VMEM((rows1, D), own_first.dtype),
                        pltpu.VMEM((rows2, D), own_second.dtype), pltpu.SemaphoreType.DMA((2,)),
                        pltpu.SemaphoreType.DMA((2,)),
                        dma7, dma7, pltpu.SemaphoreType.DMA, dma7, dma7, pltpu.SemaphoreType.DMA],
        compiler_params=_cparams(2),
    )(x, g1, own_first, own_second)


def _pool_tile(pbuf, t0, tm, pw_ref, scale_ref):
    t = t0 + lax.broadcasted_iota(jnp.int32, (tm, GD), 0)
    pooled, mixed_pre = [], []
    for g, w in enumerate(WINDOWS):
        cs = pl.ds(g * GD, GD)
        cur = pbuf[pl.ds(HALO, tm), cs]
        acc = cur
        for d in range(1, w):
            acc = acc + pbuf[pl.ds(HALO - d, tm), cs]
        cnt = jnp.minimum(t + 1, w).astype(F32)
        pg = acc / cnt - cur
        pooled.append(pg)
        mixed_pre.append(_nn(pg.astype(_MXU), pw_ref[g]))
    return pooled, mixed_pre


def _lru_gates_head(hh, lbuf, start, tm, cw_ref, cb_ref, wrg_ref, brg_ref, wig_ref, big_ref, sp):
    cs = pl.ds(hh * HD, HD)
    xc = cb_ref[:, cs] + cw_ref[pl.ds(CONV - 1, 1), cs] * lbuf[pl.ds(HALO, tm), cs]
    for k in range(CONV - 1):
        xc = xc + cw_ref[pl.ds(k, 1), cs] * lbuf[pl.ds(HALO - (CONV - 1) + k, tm), cs]
    xcm = xc.astype(_MXU)
    r = _sigmoid(_nn(xcm, wrg_ref[hh]) + brg_ref[pl.ds(hh, 1), :])
    ig = _sigmoid(_nn(xcm, wig_ref[hh]) + big_ref[pl.ds(hh, 1), :])
    a = jnp.exp(-LRU_C * r * sp[:, hh * HD:(hh + 1) * HD])
    one_m = 1.0 - a * a
    live = jnp.logical_and(one_m > 0.0, jnp.logical_not(start))
    inv_mult = lax.rsqrt(jnp.where(live, one_m, 1.0))
    mult = jnp.where(live, one_m * inv_mult, jnp.where(start, 1.0, 0.0))
    return xc, r, ig, a, live, inv_mult, mult


def _seg_layout(tm):
    seg = tm // 8
    return seg, seg + 8


def _to_segments(dst_ref, hh, val, tm):
    seg, pitch = _seg_layout(tm)
    for s in range(8):
        dst_ref[hh, pl.ds(s * pitch, seg), :] = val[s * seg:(s + 1) * seg, :]


def _from_segments(src_ref, hh, tm):
    seg, pitch = _seg_layout(tm)
    return jnp.concatenate([src_ref[hh, pl.ds(s * pitch, seg), :] for s in range(8)], axis=0)


def _segment_scan(a_ref, b_ref, out_ref, hk, pk, carry_ref, tm, reverse):
    seg, pitch = _seg_layout(tm)
    row = lax.broadcasted_iota(jnp.int32, (8, HD), 0)
    order = range(seg - 1, -1, -1) if reverse else range(seg)
    for hh in range(HEADS):
        cs = pl.ds(hh * HD, HD)
        if reverse:
            a0 = a_ref[hh, pl.ds(0, 8, stride=pitch), :]
            a_wrap = jnp.where(row <= 6, pltpu.roll(a0, 7, 0), 1.0)
        hv = jnp.zeros((8, HD), F32)
        pv = jnp.ones((8, HD), F32)
        for k in order:
            if not reverse:
                av = a_ref[hh, pl.ds(k, 8, stride=pitch), :]
            elif k + 1 < seg:
                av = a_ref[hh, pl.ds(k + 1, 8, stride=pitch), :]
            else:
                av = a_wrap
            hv = av * hv + b_ref[hh, pl.ds(k, 8, stride=pitch), :]
            pv = av * pv
            hk[hh, pl.ds(8 * k, 8), :] = hv
            pk[hh, pl.ds(8 * k, 8), :] = pv
        for d in (1, 2, 4):
            if reverse:
                keep, sh = row < 8 - d, 8 - d
            else:
                keep, sh = row >= d, d
            hv = hv + pv * jnp.where(keep, pltpu.roll(hv, sh, 0), 0.0)
            pv = pv * jnp.where(keep, pltpu.roll(pv, sh, 0), 1.0)
        cin = carry_ref[:, cs]
        ends = hv + pv * cin
        if reverse:
            enter = jnp.where(row <= 6, pltpu.roll(ends, 7, 0), cin)
            carry_ref[:, cs] = jnp.broadcast_to((a0 * ends)[0:1, :], (8, HD))
        else:
            enter = jnp.where(row >= 1, pltpu.roll(ends, 1, 0), cin)
            carry_ref[:, cs] = jnp.broadcast_to(ends[7:8, :], (8, HD))
        for k in range(seg):
            out_ref[hh, pl.ds(k, 8, stride=pitch), :] = hk[hh, pl.ds(8 * k, 8), :] + pk[hh, pl.ds(8 * k, 8), :] * enter


def _mixer_fwd(z, x, gw, small, own_third):
    s = x.shape[0]
    tm = min(TM_SEQ, s)
    nt = s // tm
    rows3 = own_third.shape[0]
    (pool_w, pool_scale, conv_w, conv_b, w_rg, b_rg, w_ig, b_ig, lam, b_gate) = small

    def body(z_ref, x_ref, gw_ref, own3_ref, pw_ref, ps_ref, cw_ref, cb_ref, wrg_ref, brg_ref, wig_ref, big_ref, lam_ref,
             bg_ref, h_ref, yl_ref, mg_ref, yp_ref, yr_ref, h1_ref, a_ref, r_ref, ig_ref, xc_ref, gw3_ref,
             pprojT, lru_w, wout_w, pbuf, lbuf, a_s, b_s, h_s, hk, pk, hcar, sems, stage3, send3, recv3, local3):
        i = pl.program_id(0)
        t0 = i * tm
        gc = _Gather(own3_ref, gw3_ref, stage3, send3, recv3, local3)

        @pl.when(i == 0)
        def _():
            gc.send_mine()
            _load_weights(gw_ref, [("pproj", pprojT, PW), ("lru", lru_w, D), ("wout", wout_w, D)], sems)
            pbuf[pl.ds(0, HALO), :] = jnp.zeros((HALO, PW), F32)
            lbuf[pl.ds(0, HALO), :] = jnp.zeros((HALO, D), F32)
            hcar[...] = jnp.zeros_like(hcar)

        @pl.when(i == nt // 2)
        def _():
            gc.pass_on((0, 1))

        @pl.when(i == (3 * nt) // 4)
        def _():
            gc.pass_on((2,))

        pbuf[pl.ds(HALO, tm), :] = z_ref[:, pl.ds(0, PW)]
        _, mixed_pre = _pool_tile(pbuf, t0, tm, pw_ref, ps_ref)
        mixed = jnp.concatenate(mixed_pre, axis=1) * ps_ref[...]
        y_pool = _nt(mixed.astype(_MXU), pprojT[...])
        pbuf[pl.ds(0, HALO), :] = pbuf[pl.ds(tm, HALO), :]

        lbuf[pl.ds(HALO, tm), :] = z_ref[:, pl.ds(PW, D)]
        sp, _ = _softplus_neg(lam_ref[...])
        start = (t0 + lax.broadcasted_iota(jnp.int32, (tm, HD), 0)) == 0
        for hh in range(HEADS):
            xc, r, ig, a, _, _, mult = _lru_gates_head(hh, lbuf, start, tm, cw_ref, cb_ref, wrg_ref, brg_ref,
                                                       wig_ref, big_ref, sp)
            _to_segments(a_s, hh, a, tm)
            _to_segments(b_s, hh, mult * ig * xc, tm)
            cs = pl.ds(hh * HD, HD)
            a_ref[:, cs] = a
            r_ref[:, cs] = r.astype(_MXU)
            ig_ref[:, cs] = ig.astype(_MXU)
            xc_ref[:, cs] = xc.astype(_MXU)
        lbuf[pl.ds(0, HALO), :] = lbuf[pl.ds(tm, HALO), :]
        _segment_scan(a_s, b_s, h_s, hk, pk, hcar, tm, reverse=False)
        for hh in range(HEADS):
            h_ref[:, pl.ds(hh * HD, HD)] = _from_segments(h_s, hh, tm)
        gel, _ = _gelu_and_grad(z_ref[:, pl.ds(PW + D, D)])
        yl = (h_ref[...] * gel).astype(_MXU)
        yl_ref[...] = yl
        y_lru = _nn(yl, lru_w[...])

        g0 = _sigmoid(z_ref[:, pl.ds(PW + 2 * D, D)] + bg_ref[pl.ds(0, 1), :])
        g1 = _sigmoid(z_ref[:, pl.ds(PW + 3 * D, D)] + bg_ref[pl.ds(1, 1), :])
        merged = (g0 * y_pool + g1 * y_lru).astype(_MXU)
        mg_ref[...] = merged
        yp_ref[...] = y_pool.astype(_MXU)
        yr_ref[...] = y_lru.astype(_MXU)
        h1_ref[...] = x_ref[...] + _nn(merged, wout_w[...])

        @pl.when(i == nt - 1)
        def _():
            gc.wait_sibling()
            gc.wait_passed((0, 1, 2))
            gc.finish_sends()

    tok = lambda w, dt: jax.ShapeDtypeStruct((s, w), dt)
    tspec = lambda w: pl.BlockSpec((tm, w), lambda i: (i, 0))
    full = lambda a: pl.BlockSpec(a.shape, lambda i: (0,) * a.ndim)
    any_spec = pl.BlockSpec(memory_space=pl.ANY)
    seg_buf = pltpu.VMEM((HEADS, 8 * _seg_layout(tm)[1], HD), F32)
    dma7 = pltpu.SemaphoreType.DMA((7,))
    return pl.pallas_call(
        body, name="mixer_fwd", grid=(nt,),
        out_shape=(tok(D, F32), tok(D, _MXU), tok(D, _MXU), tok(D, _MXU), tok(D, _MXU), tok(D, F32),
                   tok(D, F32), tok(D, _MXU), tok(D, _MXU), tok(D, _MXU),
                   jax.ShapeDtypeStruct((NDEV, rows3, D), own_third.dtype)),
        in_specs=[tspec(NIN), tspec(D), any_spec, any_spec] + [full(a) for a in small],
        out_specs=(tspec(D),) * 10 + (any_spec,),
        scratch_shapes=[pltpu.VMEM((D, PW), _MXU), pltpu.VMEM((D, D), _MXU), pltpu.VMEM((D, D), _MXU),
                        pltpu.VMEM((tm + HALO, PW), F32), pltpu.VMEM((tm + HALO, D), F32),
                        seg_buf, seg_buf, seg_buf, pltpu.VMEM((HEADS, tm, HD), F32), pltpu.VMEM((HEADS, tm, HD), F32),
                        pltpu.VMEM((8, D), F32), pltpu.SemaphoreType.DMA((3 * NDEV,)),
                        pltpu.VMEM((rows3, D), own_third.dtype), dma7, dma7, pltpu.SemaphoreType.DMA],
        compiler_params=_cparams(1),
    )(z, x, gw, own_third, *small)


def _ffn_fwd(h1, g2, gw):
    s = h1.shape[0]
    tm = min(TM, s)
    half = FF // 2

    def body(h1_ref, g2_ref, gw_ref, v_ref, gf_ref, uf_ref, h2_ref, wffnT, wffo, sems):
        @pl.when(pl.program_id(0) == 0)
        def _():
            _load_weights(gw_ref, [("wffn", wffnT, D), ("wffo", wffo, D)], sems)

        hv = h1_ref[...]
        inv = lax.rsqrt(jnp.mean(hv * hv, axis=-1, keepdims=True) + EPS)
        v = (hv * inv * g2_ref[...]).astype(_MXU)
        v_ref[...] = v
        acc = hv
        for ch in range(2):
            cs = pl.ds(ch * half, half)
            gf = _nt(v, wffnT[pl.ds(ch * half, half), :]).astype(_MXU)
            uf = _nt(v, wffnT[pl.ds(FF + ch * half, half), :]).astype(_MXU)
            gf_ref[:, cs] = gf
            uf_ref[:, cs] = uf
            gf32 = gf.astype(F32)
            act = (gf32 * _sigmoid(gf32) * uf.astype(F32)).astype(_MXU)
            acc = acc + _nn(act, wffo[pl.ds(ch * half, half), :])
        h2_ref[...] = acc

    tspec = lambda w: pl.BlockSpec((tm, w), lambda i: (i, 0))
    return pl.pallas_call(
        body, name="ffn_fwd", grid=(s // tm,),
        out_shape=(jax.ShapeDtypeStruct((s, D), _MXU), jax.ShapeDtypeStruct((s, FF), _MXU),
                   jax.ShapeDtypeStruct((s, FF), _MXU), jax.ShapeDtypeStruct((s, D), F32)),
        in_specs=[tspec(D), pl.BlockSpec((1, D), lambda i: (0, 0)), pl.BlockSpec(memory_space=pl.ANY)],
        out_specs=(tspec(D), tspec(FF), tspec(FF), tspec(D)),
        scratch_shapes=[pltpu.VMEM((2 * FF, D), _MXU), pltpu.VMEM((FF, D), _MXU), pltpu.SemaphoreType.DMA((2 * NDEV,))],
        compiler_params=_cparams(1),
    )(h1, g2, gw)


def _rms_bwd(dy, xn, inv, g):
    dg = jnp.sum(dy * xn, axis=0, keepdims=True)
    dxn = dy * g
    dx = inv * (dxn - xn * jnp.mean(dxn * xn, axis=-1, keepdims=True))
    return dx, dg


def _ple_loss_fwd_bwd(h2, p, target, g3, gfin, gw):
    s = h2.shape[0]
    tm = min(TM, s)

    def body(h2_ref, p_ref, t_ref, g3_ref, gf_ref, gw_ref,
             dh2_ref, loss_ref, dg3_ref, dgf_ref, gwpg_ref, gple_ref, wpg, pleT, sems):
        i = pl.program_id(0)

        @pl.when(i == 0)
        def _():
            _load_weights(gw_ref, [("wpg", wpg, D), ("ple", pleT, PLE)], sems)
            for ref in (loss_ref, dg3_ref, dgf_ref, gwpg_ref, gple_ref):
                ref[...] = jnp.zeros_like(ref)

        hv = h2_ref[...]
        inv3 = lax.rsqrt(jnp.mean(hv * hv, axis=-1, keepdims=True) + EPS)
        xn3 = hv * inv3
        n3 = (xn3 * g3_ref[...]).astype(_MXU)
        pg = _sigmoid(_nn(n3, wpg[...]))
        pm = p_ref[...].astype(_MXU)
        e = _nt(pm, pleT[...])
        h3 = hv + pg * e
        invf = lax.rsqrt(jnp.mean(h3 * h3, axis=-1, keepdims=True) + EPS)
        xf = h3 * invf
        diff = xf * gf_ref[...] - t_ref[...]
        loss_ref[...] += jnp.sum(diff * diff) * (0.5 / D)
        dh3, dgf = _rms_bwd(diff * (1.0 / D), xf, invf, gf_ref[...])
        dgf_ref[...] += dgf
        gple_ref[:, pl.ds(0, PLE)] += _tn((dh3 * pg).astype(_MXU), pm)
        dpg = (dh3 * e * pg * (1.0 - pg)).astype(_MXU)
        gwpg_ref[...] += _tn(n3, dpg)
        dn3 = _nt(dpg, wpg[...])
        dx3, dg3 = _rms_bwd(dn3, xn3, inv3, g3_ref[...])
        dg3_ref[...] += dg3
        dh2_ref[...] = dh3 + dx3

    tspec = lambda w: pl.BlockSpec((tm, w), lambda i: (i, 0))
    vec = pl.BlockSpec((1, D), lambda i: (0, 0))
    mat = pl.BlockSpec((D, D), lambda i: (0, 0))
    return pl.pallas_call(
        body, name="ple_loss", grid=(s // tm,),
        out_shape=(jax.ShapeDtypeStruct((s, D), F32), jax.ShapeDtypeStruct((8, 128), F32),
                   jax.ShapeDtypeStruct((1, D), F32), jax.ShapeDtypeStruct((1, D), F32),
                   jax.ShapeDtypeStruct((D, D), F32), jax.ShapeDtypeStruct((D, D), F32)),
        in_specs=[tspec(D), tspec(PLE), tspec(D), vec, vec, pl.BlockSpec(memory_space=pl.ANY)],
        out_specs=(tspec(D), pl.BlockSpec((8, 128), lambda i: (0, 0)), vec, vec, mat, mat),
        scratch_shapes=[pltpu.VMEM((D, D), _MXU), pltpu.VMEM((D, PLE), _MXU), pltpu.SemaphoreType.DMA((2 * NDEV,))],
        compiler_params=_cparams(1),
    )(h2, p, target, g3, gfin, gw)


def _ffn_bwd_hidden(dh2, gf, uf, gw):
    s = dh2.shape[0]
    tm = min(TM, s)
    nt = s // tm
    half = FF // 2

    def body(dh2_ref, gf_ref, uf_ref, gw_ref, dff_ref, gwo_ref, wffo, gacc, sems):
        i = pl.program_id(0)

        @pl.when(i == 0)
        def _():
            _load_weights(gw_ref, [("wffo", wffo, D)], sems)
            gacc[...] = jnp.zeros_like(gacc)

        dm = dh2_ref[...].astype(_MXU)
        for ch in range(2):
            cs = pl.ds(ch * half, half)
            dact = _nt(dm, wffo[cs, :])
            gfv = gf_ref[:, cs].astype(F32)
            ufv = uf_ref[:, cs].astype(F32)
            sg = _sigmoid(gfv)
            silu = gfv * sg
            gacc[cs, :] += _tn((silu * ufv).astype(_MXU), dm)
            dff_ref[:, pl.ds(ch * half, half)] = (dact * ufv * (sg * (1.0 + gfv * (1.0 - sg)))).astype(_MXU)
            dff_ref[:, pl.ds(FF + ch * half, half)] = (dact * silu).astype(_MXU)

        @pl.when(i == nt - 1)
        def _():
            pltpu.sync_copy(gacc, gwo_ref)

    tspec = lambda w: pl.BlockSpec((tm, w), lambda i: (i, 0))
    return pl.pallas_call(
        body, name="ffn_bwd_hidden", grid=(nt,),
        out_shape=(jax.ShapeDtypeStruct((s, 2 * FF), _MXU), jax.ShapeDtypeStruct((FF, D), F32)),
        in_specs=[tspec(D), tspec(FF), tspec(FF), pl.BlockSpec(memory_space=pl.ANY)],
        out_specs=(tspec(2 * FF), pl.BlockSpec(memory_space=pl.ANY)),
        scratch_shapes=[pltpu.VMEM((FF, D), _MXU), pltpu.VMEM((FF, D), F32), pltpu.SemaphoreType.DMA((NDEV,))],
        compiler_params=_cparams(1),
    )(dh2, gf, uf, gw)


def _proj_norm_bwd(dy, x, dres, g, gw, slab, width, name, lhs=None):
    s = x.shape[0]
    tm = min(TM, s)
    nl = 0 if lhs is None else 1

    def body(*refs):
        dy_ref, x_ref, dr_ref, g_ref = refs[:4]
        l_refs = refs[4:4 + nl]
        gw_ref, dx_ref, dg_ref = refs[4 + nl:7 + nl]
        gl_refs = refs[7 + nl:7 + 2 * nl]
        wT, sems = refs[7 + 2 * nl:]

        @pl.when(pl.program_id(0) == 0)
        def _():
            _load_weights(gw_ref, [(slab, wT, D)], sems)
            dg_ref[...] = jnp.zeros_like(dg_ref)
            for ref in gl_refs:
                ref[...] = jnp.zeros_like(ref)

        dv = _nn(dy_ref[...], wT[...])
        xv = x_ref[...]
        inv = lax.rsqrt(jnp.mean(xv * xv, axis=-1, keepdims=True) + EPS)
        dx, dg = _rms_bwd(dv, xv * inv, inv, g_ref[...])
        dg_ref[...] += dg
        dr = dr_ref[...]
        dx_ref[...] = dr + dx
        for l_ref, gl_ref in zip(l_refs, gl_refs):
            gl_ref[...] += _tn(l_ref[...], dr.astype(_MXU))

    tspec = lambda w: pl.BlockSpec((tm, w), lambda i: (i, 0))
    vec = pl.BlockSpec((1, D), lambda i: (0, 0))
    mat = pl.BlockSpec((D, D), lambda i: (0, 0))
    return pl.pallas_call(
        body, name=name, grid=(s // tm,),
        out_shape=(jax.ShapeDtypeStruct((s, D), F32), jax.ShapeDtypeStruct((1, D), F32))
        + (jax.ShapeDtypeStruct((D, D), F32),) * nl,
        in_specs=[tspec(width), tspec(D), tspec(D), vec] + [tspec(D)] * nl + [pl.BlockSpec(memory_space=pl.ANY)],
        out_specs=(tspec(D), vec) + (mat,) * nl,
        scratch_shapes=[pltpu.VMEM((width, D), _MXU), pltpu.SemaphoreType.DMA((NDEV,))],
        compiler_params=_cparams(1),
    )(dy, x, dres, g, *([] if lhs is None else [lhs]), gw)


def _mixer_bwd(dh1, z, h, y_pool, y_lru, saved, gw, small):
    s = dh1.shape[0]
    tm = min(TM_SEQ, s)
    nt = s // tm
    (pool_w, pool_scale, conv_w, conv_b, w_rg, b_rg, w_ig, b_ig, lam, b_gate) = small

    def body(dh1_ref, z_ref, zp_ref, h_ref, hp_ref, yp_ref, yr_ref, a_ref, r_ref, ig_ref, xc_ref, gw_ref,
             pw_ref, ps_ref, cw_ref, cb_ref, wrg_ref, brg_ref, wig_ref, big_ref, lam_ref, bg_ref,
             dz_ref, dyr_ref, dyp_ref, mx_ref,
             gbg_ref, glam_ref, gbrg_ref, gbig_ref, gcb_ref, gcw_ref, gps_ref, gpw_ref, gwrg_ref, gwig_ref,
             pprojT, lru_w, wout_w, pbuf, lbuf, hbuf, qbuf, xbuf, a_s, g_s, dh_s, hk, pk, dcar, sems):
        step = pl.program_id(0)
        i = nt - 1 - step
        t0 = i * tm

        @pl.when(step == 0)
        def _():
            _load_weights(gw_ref, [("pproj", pprojT, PW), ("lru", lru_w, D), ("wout", wout_w, D)], sems)
            for ref in (gbg_ref, glam_ref, gbrg_ref, gbig_ref, gcb_ref, gcw_ref, gps_ref, gpw_ref, gwrg_ref, gwig_ref):
                ref[...] = jnp.zeros_like(ref)
            qbuf[pl.ds(tm, HALO), :] = jnp.zeros((HALO, PW), F32)
            xbuf[pl.ds(tm, 8), :] = jnp.zeros((8, D), F32)
            dcar[...] = jnp.zeros_like(dcar)

        first = i == 0
        zprev = jnp.where(first, 0.0, zp_ref[...])
        hprev = jnp.where(first, 0.0, hp_ref[...])

        d_merged = _nt(dh1_ref[...].astype(_MXU), wout_w[...])

        g0 = _sigmoid(z_ref[:, pl.ds(PW + 2 * D, D)] + bg_ref[pl.ds(0, 1), :])
        g1 = _sigmoid(z_ref[:, pl.ds(PW + 3 * D, D)] + bg_ref[pl.ds(1, 1), :])
        dz0 = d_merged * yp_ref[...].astype(F32) * g0 * (1.0 - g0)
        dz1 = d_merged * yr_ref[...].astype(F32) * g1 * (1.0 - g1)
        dz_ref[:, pl.ds(PW + 2 * D, D)] = dz0.astype(_MXU)
        dz_ref[:, pl.ds(PW + 3 * D, D)] = dz1.astype(_MXU)
        gbg_ref[pl.ds(0, 1), :] += jnp.sum(dz0, axis=0, keepdims=True)
        gbg_ref[pl.ds(1, 1), :] += jnp.sum(dz1, axis=0, keepdims=True)
        d_ypool = (d_merged * g0).astype(_MXU)
        d_ylru = (d_merged * g1).astype(_MXU)
        dyp_ref[...] = d_ypool
        dyr_ref[...] = d_ylru

        d_yl = _nt(d_ylru, lru_w[...])
        gel, dgel = _gelu_and_grad(z_ref[:, pl.ds(PW + D, D)])
        dz_ref[:, pl.ds(PW + D, D)] = (d_yl * h_ref[...] * dgel).astype(_MXU)
        g_full = d_yl * gel
        lbuf[pl.ds(0, HALO), :] = zprev[:, PW:PW + D]
        lbuf[pl.ds(HALO, tm), :] = z_ref[:, pl.ds(PW, D)]
        hbuf[pl.ds(0, 8), :] = hprev
        hbuf[pl.ds(8, tm), :] = h_ref[...]
        sp, sneg = _softplus_neg(lam_ref[...])
        start = (t0 + lax.broadcasted_iota(jnp.int32, (tm, HD), 0)) == 0
        for hh in range(HEADS):
            cs = pl.ds(hh * HD, HD)
            _to_segments(a_s, hh, a_ref[:, cs], tm)
            _to_segments(g_s, hh, g_full[:, hh * HD:(hh + 1) * HD], tm)
        _segment_scan(a_s, g_s, dh_s, hk, pk, dcar, tm, reverse=True)
        for hh in range(HEADS):
            cs = pl.ds(hh * HD, HD)
            a = a_ref[:, cs]
            r = r_ref[:, cs].astype(F32)
            ig = ig_ref[:, cs].astype(F32)
            xc = xc_ref[:, cs].astype(F32)
            a2 = a * a
            one_m = 1.0 - a2
            live = jnp.logical_and(one_m > 0.0, jnp.logical_not(start))
            inv_mult = lax.rsqrt(jnp.where(live, one_m, 1.0))
            mult = jnp.where(live, one_m * inv_mult, jnp.where(start, 1.0, 0.0))
            dh = _from_segments(dh_s, hh, tm)
            d_mult = dh * ig * xc
            d_loga = dh * hbuf[pl.ds(7, tm), cs] * a - jnp.where(live, d_mult * a2 * inv_mult, 0.0)
            glam_ref[:, cs] += jnp.sum(d_loga * (LRU_C * r) * sneg[:, hh * HD:(hh + 1) * HD], axis=0, keepdims=True)
            d_rpre = d_loga * (-LRU_C * sp[:, hh * HD:(hh + 1) * HD]) * r * (1.0 - r)
            d_igpre = dh * mult * xc * ig * (1.0 - ig)
            gbrg_ref[pl.ds(hh, 1), :] += jnp.sum(d_rpre, axis=0, keepdims=True)
            gbig_ref[pl.ds(hh, 1), :] += jnp.sum(d_igpre, axis=0, keepdims=True)
            drm = d_rpre.astype(_MXU)
            dim = d_igpre.astype(_MXU)
            xcm = xc.astype(_MXU)
            gwrg_ref[hh] += _tn(xcm, drm)
            gwig_ref[hh] += _tn(xcm, dim)
            d_xc = dh * mult * ig + _nt(drm, wrg_ref[hh]) + _nt(dim, wig_ref[hh])
            gcb_ref[:, cs] += jnp.sum(d_xc, axis=0, keepdims=True)
            for k in range(CONV):
                gcw_ref[pl.ds(k, 1), cs] += jnp.sum(d_xc * lbuf[pl.ds(HALO - (CONV - 1) + k, tm), cs], axis=0,
                                                    keepdims=True)
            xbuf[pl.ds(0, tm), cs] = d_xc
        dzl = cw_ref[pl.ds(CONV - 1, 1), :] * xbuf[pl.ds(0, tm), :]
        for k in range(CONV - 1):
            dzl = dzl + cw_ref[pl.ds(k, 1), :] * xbuf[pl.ds(CONV - 1 - k, tm), :]
        dz_ref[:, pl.ds(PW, D)] = dzl.astype(_MXU)
        xbuf[pl.ds(tm, 8), :] = xbuf[pl.ds(0, 8), :]

        d_mixed = _nn(d_ypool, pprojT[...])
        pbuf[pl.ds(0, HALO), :] = zprev[:, 0:PW]
        pbuf[pl.ds(HALO, tm), :] = z_ref[:, pl.ds(0, PW)]
        pooled, mixed_pre = _pool_tile(pbuf, t0, tm, pw_ref, ps_ref)
        mp = jnp.concatenate(mixed_pre, axis=1)
        mx_ref[...] = (mp * ps_ref[...]).astype(_MXU)
        gps_ref[...] += jnp.sum(d_mixed * mp, axis=0, keepdims=True)
        d_mp = (d_mixed * ps_ref[...]).astype(_MXU)
        t = t0 + lax.broadcasted_iota(jnp.int32, (tm, GD), 0)
        d_pooled = []
        for g, w in enumerate(WINDOWS):
            dmg = d_mp[:, g * GD:(g + 1) * GD]
            gpw_ref[g] += _tn(pooled[g].astype(_MXU), dmg)
            dp = _nt(dmg, pw_ref[g])
            d_pooled.append(dp)
            qbuf[pl.ds(0, tm), pl.ds(g * GD, GD)] = dp / jnp.minimum(t + 1, w).astype(F32)
        for g, w in enumerate(WINDOWS):
            cs = pl.ds(g * GD, GD)
            acc = qbuf[pl.ds(0, tm), cs]
            for d in range(1, w):
                acc = acc + qbuf[pl.ds(d, tm), cs]
            dz_ref[:, cs] = (acc - d_pooled[g]).astype(_MXU)
        qbuf[pl.ds(tm, HALO), :] = qbuf[pl.ds(0, HALO), :]

    rev = lambda w: pl.BlockSpec((tm, w), lambda g: (nt - 1 - g, 0))
    prev = lambda rows, w: pl.BlockSpec((rows, w), lambda g: (jnp.maximum((nt - 1 - g) * (tm // rows) - 1, 0), 0))
    full = lambda a: pl.BlockSpec(a.shape, lambda g: (0,) * a.ndim)
    tok = lambda w, dt: jax.ShapeDtypeStruct((s, w), dt)
    acc_shapes = [(2, D), (1, D), (HEADS, HD), (HEADS, HD), (1, D), (CONV, D), (1, PW), (GROUPS, GD, GD),
                  (HEADS, HD, HD), (HEADS, HD, HD)]
    acc_specs = tuple(pl.BlockSpec(sh, lambda g, n=len(sh): (0,) * n) for sh in acc_shapes)
    seg_buf = pltpu.VMEM((HEADS, 8 * _seg_layout(tm)[1], HD), F32)
    a_in, r_in, ig_in, xc_in = saved
    return pl.pallas_call(
        body, name="mixer_bwd", grid=(nt,),
        out_shape=(tok(NIN, _MXU), tok(D, _MXU), tok(D, _MXU), tok(PW, _MXU))
        + tuple(jax.ShapeDtypeStruct(sh, F32) for sh in acc_shapes),
        in_specs=[rev(D), rev(NIN), prev(HALO, NIN), rev(D), prev(8, D), rev(D), rev(D), rev(D), rev(D), rev(D), rev(D),
                  pl.BlockSpec(memory_space=pl.ANY)] + [full(a) for a in small],
        out_specs=(rev(NIN), rev(D), rev(D), rev(PW)) + acc_specs,
        scratch_shapes=[pltpu.VMEM((D, PW), _MXU), pltpu.VMEM((D, D), _MXU), pltpu.VMEM((D, D), _MXU),
                        pltpu.VMEM((tm + HALO, PW), F32), pltpu.VMEM((tm + HALO, D), F32),
                        pltpu.VMEM((tm + 8, D), F32), pltpu.VMEM((tm + HALO, PW), F32), pltpu.VMEM((tm + 8, D), F32),
                        seg_buf, seg_buf, seg_buf, pltpu.VMEM((HEADS, tm, HD), F32), pltpu.VMEM((HEADS, tm, HD), F32),
                        pltpu.VMEM((8, D), F32), pltpu.SemaphoreType.DMA((3 * NDEV,))],
        compiler_params=_cparams(1),
    )(dh1, z, z, h, h, y_pool, y_lru, a_in, r_in, ig_in, xc_in, gw, *small)


def _split3(a):
    hi = a.astype(jnp.bfloat16).astype(F32)
    mid = (a - hi).astype(jnp.bfloat16).astype(F32)
    lo = (a - hi - mid).astype(jnp.bfloat16).astype(F32)
    return jnp.stack([hi, mid, lo])


def _small_pack(parts):
    flat = jnp.concatenate([a.reshape(-1) for a in parts])
    return jnp.pad(flat, (0, NDEV * SMALL_ROWS * D - flat.shape[0])).reshape(NDEV * SMALL_ROWS, D)


def _small_unpack(packed, shapes):
    flat = packed.reshape(-1)
    out, o = [], 0
    for sh in shapes:
        n = math.prod(sh)
        out.append(flat[o:o + n].reshape(sh))
        o += n
    return out


def kernel(x, p, norm1_g, w_in, b_gate, pool_w, pool_scale, pool_proj, conv_w, conv_b, w_rg, b_rg, w_ig, b_ig, lru_lambda, lru_proj, w_out, norm2_g, w_ffn_in, w_ffn_out, ple_norm_g, w_ple_gate, w_ple_proj, final_g, loss_target, m_norm1_g, m_w_in, m_b_gate, m_pool_w, m_pool_scale, m_pool_proj, m_conv_w, m_conv_b, m_w_rg, m_b_rg, m_w_ig, m_b_ig, m_lru_lambda, m_lru_proj, m_w_out, m_norm2_g, m_w_ffn_in, m_w_ffn_out, m_ple_norm_g, m_w_ple_gate, m_w_ple_proj, m_final_g, v_norm1_g, v_w_in, v_b_gate, v_pool_w, v_pool_scale, v_pool_proj, v_conv_w, v_conv_b, v_w_rg, v_b_rg, v_w_ig, v_b_ig, v_lru_lambda, v_lru_proj, v_w_out, v_norm2_g, v_w_ffn_in, v_w_ffn_out, v_ple_norm_g, v_w_ple_gate, v_w_ple_proj, v_final_g):
    axes = ("x", "y", "c")
    me = 4 * lax.axis_index("x") + 2 * lax.axis_index("y") + lax.axis_index("c")
    x2 = x[0]
    p2 = p[0, 0]
    tgt = loss_target[0]

    n_small = (CONV + 2) * 128
    small_terms = _split3(jnp.concatenate([conv_w[0].reshape(-1), b_gate[0].reshape(-1)]))
    small_rows = jnp.pad(small_terms, ((0, 16 - 3), (0, D - n_small)))
    own_first = jnp.concatenate([w_in[0].T.astype(_MXU), small_rows.astype(_MXU)], axis=0)
    own_second = jnp.concatenate([
        jnp.pad(pool_proj[0].T, ((0, 0), (0, D - PW))).astype(_MXU), lru_proj[0].astype(_MXU), w_out[0].astype(_MXU),
    ], axis=0)
    own_third = jnp.concatenate([
        w_ffn_in[0].T.astype(_MXU), jnp.pad(w_ple_proj[0].T, ((0, 0), (0, D - PLE))).astype(_MXU),
        w_ffn_out[0].astype(_MXU), w_ple_gate[0].astype(_MXU),
    ], axis=0)
    u, z, gw_first, gw = _inproj_fwd(x2, norm1_g, own_first, own_second)
    off = W_OFF["f32s"][0]
    st = gw_first[:, off:off + 3, :n_small].astype(F32)
    sf = st[:, 0] + st[:, 1] + st[:, 2]
    conv_w_full = sf[:, :CONV * 128].reshape(NDEV, CONV, 128).transpose(1, 0, 2).reshape(CONV, D)
    b_gate_full = sf[:, CONV * 128:].reshape(NDEV, 2, 128).transpose(1, 0, 2).reshape(2, D)

    small = (pool_w[0].astype(_MXU), pool_scale, conv_w_full, conv_b, w_rg[0].astype(_MXU), b_rg[0],
             w_ig[0].astype(_MXU), b_ig[0], lru_lambda, b_gate_full)

    h, yl, merged, y_pool, y_lru, h1, *saved, gw_third = _mixer_fwd(z, x2, gw, small, own_third)
    v, gf, uf, h2 = _ffn_fwd(h1, norm2_g, gw_third)

    dh2, loss_blk, g_ple_norm, g_final, part_wpg, part_ple = _ple_loss_fwd_bwd(h2, p2, tgt, ple_norm_g,
                                                                               final_g.reshape(1, D), gw_third)
    dff, part_wffo = _ffn_bwd_hidden(dh2, gf, uf, gw_third)
    dh1, g_norm2 = _proj_norm_bwd(dff, h1, dh2, norm2_g, gw_third, "wffn", 2 * FF, "ffn_bwd_in")
    (dz, d_ylru, d_ypool, mixed, g_bgate, g_lam, g_brg, g_big, g_convb, g_convw, g_pscale, g_poolw, g_wrg,
     g_wig) = _mixer_bwd(dh1, z, h, y_pool, y_lru, saved, gw, small)
    grad_x, g_norm1, part_wout = _proj_norm_bwd(dz, x2, dh1, norm1_g, gw_first, "win", NIN, "inproj_bwd", lhs=merged)

    small_shapes = [(1, D), (GROUPS, GD, GD), (1, PW), (1, D), (HEADS, HD, HD), (HEADS, HD), (HEADS, HD, HD),
                    (HEADS, HD), (1, D), (1, D), (1, D), (1, D), (2, D), (CONV, D), (1, 1)]
    small_part = _small_pack([g_norm1, g_poolw, g_pscale, g_convb, g_wrg, g_brg, g_wig, g_big, g_lam, g_norm2,
                              g_ple_norm, g_final, g_bgate, g_convw, loss_blk[0:1, 0:1]])
    riders = [_grad_matmul(yl, d_ylru, "grad_lru_proj"), part_wout, _grad_matmul(d_ypool, mixed, "grad_pool_proj")]
    rs_wffn = _grad_matmul_rs(dff, v, "grad_w_ffn_in", 704, extras=[part_wffo, part_wpg, part_ple], narrow=_MXU)
    rs_win = _grad_matmul_rs(dz, u, "grad_w_in", 576, extras=riders + [small_part], narrow=_MXU, tail=SMALL_ROWS)

    def reduced(parts, name):
        return [_sum_arrays([t_own, landed[0], landed[1], landed[2]], "rs_sum_" + name + str(n))
                for n, (t_own, landed) in enumerate(parts)]

    red_wffn, = reduced(rs_wffn, "wffn")
    red_win, red_small = reduced(rs_win, "win")
    g_w_in = red_win[:576].T
    g_w_ffn_in = red_wffn[:704].T
    g_w_ffn_out = red_wffn[704:1056]
    g_w_ple_gate = red_wffn[1056:1184]
    g_w_ple_proj = red_wffn[1184:1312, :PLE].T
    g_lru_proj, g_w_out = red_win[576:704], red_win[704:832]
    g_pool_proj = red_win[832:960, :PW].T
    small_red = _all_gather_small(red_small)
    (gs_norm1, gs_poolw, gs_pscale, gs_convb, gs_wrg, gs_brg, gs_wig, gs_big, gs_lam, gs_norm2, gs_ple_norm,
     gs_final, gs_bgate, gs_convw, loss_sum) = _small_unpack(small_red, small_shapes)
    loss = loss_sum[0, 0]
    g_b_gate = lax.dynamic_slice_in_dim(gs_bgate, me * 128, 128, axis=1)
    g_conv_w = lax.dynamic_slice_in_dim(gs_convw, me * 128, 128, axis=1)

    grads = {
        "norm1_g": gs_norm1, "w_in": g_w_in[None], "b_gate": g_b_gate[None], "pool_w": gs_poolw[None],
        "pool_scale": gs_pscale, "pool_proj": g_pool_proj[None], "conv_w": g_conv_w[None], "conv_b": gs_convb,
        "w_rg": gs_wrg[None], "b_rg": gs_brg[None], "w_ig": gs_wig[None], "b_ig": gs_big[None], "lru_lambda": gs_lam,
        "lru_proj": g_lru_proj[None], "w_out": g_w_out[None], "norm2_g": gs_norm2, "w_ffn_in": g_w_ffn_in[None],
        "w_ffn_out": g_w_ffn_out[None], "ple_norm_g": gs_ple_norm, "w_ple_gate": g_w_ple_gate[None],
        "w_ple_proj": g_w_ple_proj[None], "final_g": gs_final.reshape(D),
    }
    weights = dict(norm1_g=norm1_g, w_in=w_in, b_gate=b_gate, pool_w=pool_w, pool_scale=pool_scale, pool_proj=pool_proj,
                   conv_w=conv_w, conv_b=conv_b, w_rg=w_rg, b_rg=b_rg, w_ig=w_ig, b_ig=b_ig, lru_lambda=lru_lambda,
                   lru_proj=lru_proj, w_out=w_out, norm2_g=norm2_g, w_ffn_in=w_ffn_in, w_ffn_out=w_ffn_out,
                   ple_norm_g=ple_norm_g, w_ple_gate=w_ple_gate, w_ple_proj=w_ple_proj, final_g=final_g)
    moments_m = dict(norm1_g=m_norm1_g, w_in=m_w_in, b_gate=m_b_gate, pool_w=m_pool_w, pool_scale=m_pool_scale,
                     pool_proj=m_pool_proj, conv_w=m_conv_w, conv_b=m_conv_b, w_rg=m_w_rg, b_rg=m_b_rg, w_ig=m_w_ig,
                     b_ig=m_b_ig, lru_lambda=m_lru_lambda, lru_proj=m_lru_proj, w_out=m_w_out, norm2_g=m_norm2_g,
                     w_ffn_in=m_w_ffn_in, w_ffn_out=m_w_ffn_out, ple_norm_g=m_ple_norm_g, w_ple_gate=m_w_ple_gate,
                     w_ple_proj=m_w_ple_proj, final_g=m_final_g)
    moments_v = dict(norm1_g=v_norm1_g, w_in=v_w_in, b_gate=v_b_gate, pool_w=v_pool_w, pool_scale=v_pool_scale,
                     pool_proj=v_pool_proj, conv_w=v_conv_w, conv_b=v_conv_b, w_rg=v_w_rg, b_rg=v_b_rg, w_ig=v_w_ig,
                     b_ig=v_b_ig, lru_lambda=v_lru_lambda, lru_proj=v_lru_proj, w_out=v_w_out, norm2_g=v_norm2_g,
                     w_ffn_in=v_w_ffn_in, w_ffn_out=v_w_ffn_out, ple_norm_g=v_ple_norm_g, w_ple_gate=v_w_ple_gate,
                     w_ple_proj=v_w_ple_proj, final_g=v_final_g)
    names = list(weights)
    big = ("w_in", "w_ffn_in", "w_ffn_out", "lru_proj", "w_out", "w_ple_gate", "pool_proj", "w_ple_proj")
    slab_space = {"w_in": red_win[:576], "w_ffn_in": red_wffn[:704]}
    delta, new_m, new_v = {}, {}, {}
    for n in big:
        sh = weights[n].shape
        if n in slab_space:
            as2d = lambda a: a[0].T
            back = lambda a: a.T[None]
            g2d = slab_space[n]
        else:
            as2d = lambda a: a.reshape(sh[-2], sh[-1])
            back = lambda a: a.reshape(sh)
            g2d = as2d(grads[n])
        d_, m_, v_ = _adamw(as2d(weights[n]), g2d, as2d(moments_m[n]), as2d(moments_v[n]), "adamw_" + n)
        delta[n], new_m[n], new_v[n] = back(d_), back(m_), back(v_)
    rest = [n for n in names if n not in big]
    rest_shapes = [weights[n].shape for n in rest]
    packed = [_small_pack([src[n] for n in rest]) for src in (weights, grads, moments_m, moments_v)]
    d_, m_, v_ = _adamw(*packed, "adamw_small")
    for n, a, b_, c_ in zip(rest, _small_unpack(d_, rest_shapes), _small_unpack(m_, rest_shapes),
                            _small_unpack(v_, rest_shapes)):
        delta[n], new_m[n], new_v[n] = a, b_, c_

    return (loss, grad_x[None], *[grads[n] for n in names], *[delta[n] for n in names],
            *[new_m[n] for n in names], *[new_v[n] for n in names])
```

```python
import functools
import math

import jax
import jax.numpy as jnp
from jax import lax
from jax.experimental import pallas as pl
from jax.experimental.pallas import tpu as pltpu

F32 = jnp.float32
D = 1024
NIN = 4608
PW = 512
FF = 2816
PLE = 256
HEADS, HD = 8, 128
GROUPS, GD = 4, 128
WINDOWS = (2, 4, 8, 16)
HALO = 16
CONV = 4
EPS = 1e-6
LRU_C = 8.0
NDEV = 8
MESH = pl.DeviceIdType.MESH

ADAM_LR, ADAM_B1, ADAM_B2, ADAM_EPS, ADAM_WD, ADAM_STEP = 0.001, 0.9, 0.999, 1e-08, 0.01, 10

_MXU = jnp.bfloat16
TM = 512
TM_SEQ = 256
VMEM_LIMIT = 56 * 1024 * 1024
W_FIRST = (("win", 576), ("f32s", 16))
W_SECOND = (("pproj", 128), ("lru", 128), ("wout", 128))
W_THIRD = (("wffn", 704), ("ple", 128), ("wffo", 352), ("wpg", 128))
W_OFF = {}
for _slabs in (W_FIRST, W_SECOND, W_THIRD):
    _o = 0
    for _n, _r in _slabs:
        W_OFF[_n] = (_o, _r)
        _o += _r
SMALL_ROWS = 48


def _cparams(n_axes=1, vmem=VMEM_LIMIT):
    return pltpu.CompilerParams(dimension_semantics=("arbitrary",) * n_axes, vmem_limit_bytes=vmem)


def _my_pos():
    return lax.axis_index("x"), lax.axis_index("y"), lax.axis_index("c")


def _nt(a, b):
    return lax.dot_general(a, b, (((1,), (1,)), ((), ())), preferred_element_type=F32)


def _nn(a, b):
    return lax.dot_general(a, b, (((1,), (0,)), ((), ())), preferred_element_type=F32)


def _tn(a, b):
    return lax.dot_general(a, b, (((0,), (0,)), ((), ())), preferred_element_type=F32)


def _sigmoid(x):
    return 0.5 * jnp.tanh(0.5 * x) + 0.5


_GELU_K = math.sqrt(2.0 / math.pi)


def _gelu_and_grad(x):
    x2 = x * x
    inner = _GELU_K * (x + 0.044715 * x2 * x)
    t = jnp.tanh(inner)
    g = 0.5 * x * (1.0 + t)
    dg = 0.5 * (1.0 + t) + 0.5 * x * (1.0 - t * t) * _GELU_K * (1.0 + 3.0 * 0.044715 * x2)
    return g, dg


def _softplus_neg(lam):
    x = -lam
    t = jnp.exp(-jnp.abs(x))
    u = 1.0 + t
    l1p = jnp.where(u == 1.0, t, jnp.log(u) * t / (u - 1.0))
    return jnp.maximum(x, 0.0) + l1p, _sigmoid(x)


def _start_slab_loads(g_ref, name, dst_ref, sems, base, width=D):
    off, rows = W_OFF[name]
    copies = []
    for k in range(NDEV):
        if width == D:
            src = g_ref.at[k, pl.ds(off, rows), :]
        else:
            src = g_ref.at[k, pl.ds(off, rows), pl.ds(0, width)]
        cp = pltpu.make_async_copy(src, dst_ref.at[pl.ds(k * rows, rows), :], sems.at[base + k])
        cp.start()
        copies.append(cp)
    return copies


def _load_weights(g_ref, items, sems):
    copies = []
    for n, (name, dst, width) in enumerate(items):
        copies += _start_slab_loads(g_ref, name, dst, sems, n * NDEV, width)
    for cp in copies:
        cp.wait()


class _Gather:
    def __init__(self, own_ref, out_ref, stage, send_sems, recv_sems, local_sem):
        x, y, c = _my_pos()
        self.c = c
        self.me, self.sibling = (x, y, c), (x, y, 1 - c)
        self.chips = [(1 - x, y), (x, 1 - y), (1 - x, 1 - y)]
        self.own_ref, self.out_ref, self.stage = own_ref, out_ref, stage
        self.send_sems, self.recv_sems = send_sems, recv_sems
        self.mine = pltpu.make_async_copy(stage, self.slab(*self.me), local_sem)
        self.first = [self.copy(0, self.me, self.sibling, src=stage)] + [
            self.copy(1 + j, self.me, (*chip, c), src=stage) for j, chip in enumerate(self.chips)]
        self.passed = [self.copy(4 + j, (*chip, c), self.sibling) for j, chip in enumerate(self.chips)]

    def slab(self, px, py, pc):
        return self.out_ref.at[4 * px + 2 * py + pc]

    def copy(self, k, block, to, src=None):
        return pltpu.make_async_remote_copy(
            src_ref=self.slab(*block) if src is None else src, dst_ref=self.slab(*block),
            send_sem=self.send_sems.at[k], recv_sem=self.recv_sems.at[k], device_id=to, device_id_type=MESH)

    def send_mine(self, far=True):
        pltpu.sync_copy(self.own_ref, self.stage)
        self.mine.start()
        for cp in self.first[:3]:
            cp.start()
        if far:
            self.send_far()

    def send_far(self):
        self.first[3].start()

    def pass_on(self, js):
        for j in js:
            self.copy(1 + j, (*self.chips[j], self.c), self.me).wait_recv()
            self.passed[j].start()

    def wait_sibling(self):
        self.copy(0, self.sibling, self.me).wait_recv()

    def wait_passed(self, js):
        for j in js:
            self.copy(4 + j, (*self.chips[j], 1 - self.c), self.me).wait_recv()

    def finish_sends(self):
        for cp in self.first + self.passed:
            cp.wait_send()
        self.mine.wait()


def _all_gather_small(piece):
    rows = piece.shape[0]

    def body(p_ref, out_ref, send_sems, recv_sems, local_sem):
        x, y, c = _my_pos()
        me = 4 * x + 2 * y + c
        mine = pltpu.make_async_copy(p_ref, out_ref.at[pl.ds(pl.multiple_of(me * rows, 8), rows), :], local_sem)
        mine.start()
        sends = []
        peers = []
        for r in range(1, NDEV):
            px = 1 - x if (r >> 2) & 1 else x
            py = 1 - y if (r >> 1) & 1 else y
            pc = 1 - c if r & 1 else c
            peers.append((px, py, pc))
            cp = pltpu.make_async_remote_copy(
                src_ref=p_ref, dst_ref=out_ref.at[pl.ds(pl.multiple_of(me * rows, 8), rows), :],
                send_sem=send_sems.at[r - 1], recv_sem=recv_sems.at[r - 1], device_id=(px, py, pc),
                device_id_type=MESH)
            cp.start()
            sends.append(cp)
        for r, (px, py, pc) in enumerate(peers):
            them = 4 * px + 2 * py + pc
            pltpu.make_async_remote_copy(
                src_ref=p_ref, dst_ref=out_ref.at[pl.ds(pl.multiple_of(them * rows, 8), rows), :],
                send_sem=send_sems.at[r], recv_sem=recv_sems.at[r], device_id=(px, py, pc),
                device_id_type=MESH).wait_recv()
        for cp in sends:
            cp.wait_send()
        mine.wait()

    return pl.pallas_call(
        body, name="ag_small",
        out_shape=jax.ShapeDtypeStruct((NDEV * rows, piece.shape[1]), piece.dtype),
        in_specs=[pl.BlockSpec(memory_space=pltpu.VMEM)],
        out_specs=pl.BlockSpec(memory_space=pl.ANY),
        scratch_shapes=[pltpu.SemaphoreType.DMA((7,)), pltpu.SemaphoreType.DMA((7,)), pltpu.SemaphoreType.DMA],
    )(piece)


def _row_block(rows, target=512, mult=8):
    b = min(rows, target) // mult * mult
    while rows % b:
        b -= mult
    return b


def _sum_arrays(arrs, name, narrow=None, target=704):
    rows, cols = arrs[0].shape
    br = _row_block(rows, target, 16)
    n = len(arrs)

    def body(*refs):
        acc = refs[0][...].astype(F32)
        for r in refs[1:n]:
            acc = acc + r[...].astype(F32)
        refs[n][...] = acc
        if narrow is not None:
            refs[n + 1][...] = acc.astype(narrow)

    spec = pl.BlockSpec((br, cols), lambda i: (i, 0))
    shape = jax.ShapeDtypeStruct((rows, cols), F32)
    if narrow is None:
        out_shape, out_specs = shape, spec
    else:
        out_shape, out_specs = (shape, jax.ShapeDtypeStruct((rows, cols), narrow)), (spec, spec)
    return pl.pallas_call(
        body, name=name, grid=(rows // br,), out_shape=out_shape,
        in_specs=[spec] * n, out_specs=out_specs, compiler_params=_cparams(1),
    )(*arrs)


def _adamw(w, g, m, v, name):
    rows, cols = w.shape
    br = _row_block(rows, 256)

    def body(w_ref, g_ref, m_ref, v_ref, d_ref, nm_ref, nv_ref):
        g_ = g_ref[...]
        m_ = ADAM_B1 * m_ref[...] + (1.0 - ADAM_B1) * g_
        v_ = ADAM_B2 * v_ref[...] + (1.0 - ADAM_B2) * (g_ * g_)
        m_hat = m_ / (1.0 - ADAM_B1 ** ADAM_STEP)
        v_hat = v_ / (1.0 - ADAM_B2 ** ADAM_STEP)
        d_ref[...] = -ADAM_LR * (m_hat / (jnp.sqrt(v_hat) + ADAM_EPS) + ADAM_WD * w_ref[...])
        nm_ref[...] = m_
        nv_ref[...] = v_

    spec = pl.BlockSpec((br, cols), lambda i: (i, 0))
    shape = jax.ShapeDtypeStruct((rows, cols), F32)
    return pl.pallas_call(
        body, name=name, grid=(rows // br,), out_shape=(shape, shape, shape),
        in_specs=[spec] * 4, out_specs=(spec, spec, spec), compiler_params=_cparams(1),
    )(w, g, m, v)


_CHIP_FLIPS = (2, 3, 1, 0)


def _grad_matmul(lhs, rhs, name):
    s, r = lhs.shape
    k = rhs.shape[1]
    tm = min(4 * TM, s)

    def body(l_ref, r_ref, o_ref):
        @pl.when(pl.program_id(0) == 0)
        def _():
            o_ref[...] = jnp.zeros_like(o_ref)

        o_ref[:, pl.ds(0, k)] += _tn(l_ref[...].astype(_MXU), r_ref[...].astype(_MXU))

    return pl.pallas_call(
        body, name=name, grid=(s // tm,),
        out_shape=jax.ShapeDtypeStruct((r, D), F32),
        in_specs=[pl.BlockSpec((tm, r), lambda i: (i, 0)), pl.BlockSpec((tm, k), lambda i: (i, 0))],
        out_specs=pl.BlockSpec((r, D), lambda i: (0, 0)),
        compiler_params=_cparams(1),
    )(lhs, rhs)


def _grad_matmul_rs(lhs, rhs, name, rows, extras=(), narrow=None, tail=0):
    s, r8 = lhs.shape
    k = rhs.shape[1]
    tm = min(TM, s)
    nt = s // tm
    cpb = 1
    nblk = 4 // cpb
    nx = len(extras)
    ers = [e.shape[0] // NDEV for e in extras]
    er = sum(ers)
    srows = rows + er
    brows = 2 * cpb * srows
    groups = [(0, srows - tail, F32 if narrow is None else narrow)] + ([(srows - tail, tail, F32)] if tail else [])
    ng = len(groups)
    mid = min(nt - 1, max(1, nt // 6))

    def flip_of(p):
        return jnp.where(p == 0, 2, jnp.where(p == 1, 3, jnp.where(p == 2, 1, 0)))

    def block_col(b):
        x, y, _ = _my_pos()
        return (2 * x + y) ^ flip_of(b)

    def body(*refs):
        l_ref, r_ref = refs[:2]
        x_refs = refs[2:2 + nx]
        rest = refs[2 + nx:]
        town_ref = rest[0]
        lici_refs = rest[1:1 + ng]
        acc, stage = rest[1 + ng:3 + ng]
        send_bufs = rest[3 + ng:3 + 2 * ng]
        dsend, drecv, isend, irecv, xsem = rest[3 + 2 * ng:]
        b = pl.program_id(0)
        i = pl.program_id(1)
        x, y, c = _my_pos()
        mine = 2 * x + y
        sibling = (x, y, 1 - c)

        def chip_at(p):
            return mine ^ _CHIP_FLIPS[p]

        def slab_rows(p, parity):
            within = 0 if cpb == 1 else (chip_at(p) & 1) * 2
            return pl.ds(pl.multiple_of((within + parity) * srows, 8), srows)

        def push(p, slot):
            return pltpu.make_async_remote_copy(
                src_ref=acc.at[slot, slab_rows(p, 1 - c), :], dst_ref=stage.at[p % 2],
                send_sem=dsend.at[p], recv_sem=drecv.at[p], device_id=sibling, device_id_type=MESH)

        def ici(p):
            ch = chip_at(p)
            return [pltpu.make_async_remote_copy(
                src_ref=send_bufs[g].at[p % 2], dst_ref=lici_refs[g].at[p], send_sem=isend.at[3 * g + p],
                recv_sem=irecv.at[3 * g + p], device_id=(ch >> 1, ch & 1, c), device_id_type=MESH) for g in range(ng)]

        def extra_loads(p, slot):
            copies = []
            within = 0 if cpb == 1 else (chip_at(p) & 1) * 2
            for parity in range(2):
                off = rows
                for n, (x_ref, e) in enumerate(zip(x_refs, ers)):
                    src = x_ref.at[pl.ds(pl.multiple_of((2 * chip_at(p) + parity) * e, 8), e), :]
                    dst = acc.at[slot, pl.ds(pl.multiple_of((within + parity) * srows + off, 8), e), :]
                    copies.append(pltpu.make_async_copy(src, dst, xsem.at[(p * 2 + parity) * nx + n]))
                    off += e
            return copies

        def combine(p, slot):
            push(p, slot).wait_recv()
            total = acc[slot, slab_rows(p, c), :] + stage[p % 2]
            if p == 3:
                stage[p % 2] = total
                pltpu.sync_copy(stage.at[p % 2], town_ref)
            else:
                if p == 2:
                    for cp in ici(0):
                        cp.wait_send()
                for g, (r0, n, dt) in enumerate(groups):
                    send_bufs[g][p % 2] = total[r0:r0 + n, :].astype(dt)
                for cp in ici(p):
                    cp.start()

        for bb in range(nblk):
            slot = bb % 2
            positions = list(range(bb * cpb, (bb + 1) * cpb))

            @pl.when(jnp.logical_and(b == bb, i == 0))
            def _(bb=bb, slot=slot, positions=positions):
                if bb >= 2:
                    for p in range((bb - 2) * cpb, (bb - 1) * cpb):
                        push(p, slot).wait_send()
                for q in range(2 * cpb):
                    acc[slot, pl.ds(q * srows, rows), :] = jnp.zeros((rows, D), F32)
                for p in positions:
                    for cp in extra_loads(p, slot):
                        cp.start()

            if bb >= 1:
                @pl.when(jnp.logical_and(b == bb, i == mid))
                def _(bb=bb):
                    for p in range((bb - 1) * cpb, bb * cpb):
                        combine(p, (bb - 1) % 2)

        res = _tn(l_ref[...].astype(_MXU), r_ref[...].astype(_MXU))
        slot_now = b % 2
        for q in range(2 * cpb):
            acc[slot_now, pl.ds(q * srows, rows), pl.ds(0, k)] += res[q * rows:(q + 1) * rows, :]

        for bb in range(nblk):
            slot = bb % 2
            positions = list(range(bb * cpb, (bb + 1) * cpb))

            @pl.when(jnp.logical_and(b == bb, i == nt - 1))
            def _(bb=bb, slot=slot, positions=positions):
                for p in positions:
                    for cp in extra_loads(p, slot):
                        cp.wait()
                for p in positions:
                    push(p, slot).start()
                if bb == nblk - 1:
                    for p in positions:
                        combine(p, slot)
                    for p in range(max(0, (nblk - 2) * cpb), 4):
                        push(p, slot).wait_send()
                    for p in range(1, 3):
                        for cp in ici(p):
                            cp.wait_send()
                    for p in range(3):
                        for cp in ici(p):
                            cp.wait_recv()

    in_specs = [pl.BlockSpec((tm, 2 * cpb * rows), lambda b, i: (i, block_col(b))),
                pl.BlockSpec((tm, k), lambda b, i: (i, 0))]
    any_spec = pl.BlockSpec(memory_space=pl.ANY)
    in_specs += [any_spec] * nx
    args = [lhs, rhs, *extras]
    outs = pl.pallas_call(
        body, name=name, grid=(nblk, nt),
        out_shape=(jax.ShapeDtypeStruct((srows, D), F32),)
        + tuple(jax.ShapeDtypeStruct((3, n, D), dt) for _, n, dt in groups),
        in_specs=in_specs, out_specs=(any_spec,) * (1 + ng),
        scratch_shapes=[pltpu.VMEM((2, brows, D), F32), pltpu.VMEM((2, srows, D), F32)]
        + [pltpu.VMEM((2, n, D), dt) for _, n, dt in groups]
        + [pltpu.SemaphoreType.DMA((4,)), pltpu.SemaphoreType.DMA((4,)), pltpu.SemaphoreType.DMA((3 * ng,)),
           pltpu.SemaphoreType.DMA((3 * ng,)), pltpu.SemaphoreType.DMA((max(1, 8 * nx),))],
        compiler_params=_cparams(2),
    )(*args)
    t_own = outs[0]
    return [(t_own[r0:r0 + n], landed) for (r0, n, _), landed in zip(groups, outs[1:])]


def _inproj_fwd(x, g1, own_first, own_second):
    s = x.shape[0]
    tm = min(2 * TM, s // 2)
    nt = s // tm
    assert nt % 2 == 0
    rows1, rows2 = own_first.shape[0], own_second.shape[0]
    wrows = W_OFF["win"][1]
    cw = 2 * wrows

    def chip_col(b):
        px, py, _ = _my_pos()
        return (2 * px + py) ^ jnp.where(b == 0, 0, jnp.where(b == 1, 2, jnp.where(b == 2, 1, 3)))

    def body(x_ref, g1_ref, own1_ref, own2_ref, u_ref, z_ref, gw1_ref, gw2_ref, w_vmem, u_buf, stage1, stage2, sems,
             usem, send1, recv1, local1, send2, recv2, local2):
        b = pl.program_id(0)
        i = pl.program_id(1)
        ga = _Gather(own1_ref, gw1_ref, stage1, send1, recv1, local1)
        gb = _Gather(own2_ref, gw2_ref, stage2, send2, recv2, local2)
        c = ga.c

        def load_chip(px, py, own_too):
            copies = []
            for pc in range(2):
                dst = w_vmem.at[pl.ds(pc * wrows, wrows), :]
                copies.append(pltpu.make_async_copy(gw1_ref.at[4 * px + 2 * py + pc, pl.ds(0, wrows), :], dst,
                                                    sems.at[pc]))
            if own_too:
                mine_dst = w_vmem.at[pl.ds(pl.multiple_of(c * wrows, 16), wrows), :]
                copies[0] = pltpu.make_async_copy(own1_ref.at[pl.ds(0, wrows), :], mine_dst, sems.at[0])
                theirs_dst = w_vmem.at[pl.ds(pl.multiple_of((1 - c) * wrows, 16), wrows), :]
                copies[1] = pltpu.make_async_copy(gw1_ref.at[4 * px + 2 * py + 1 - c, pl.ds(0, wrows), :], theirs_dst,
                                                  sems.at[1])
            for cp in copies:
                cp.start()
            for cp in copies:
                cp.wait()

        @pl.when(jnp.logical_and(b == 0, i == 0))
        def _():
            ga.send_mine(far=False)
            ga.wait_sibling()
            load_chip(ga.me[0], ga.me[1], True)

        @pl.when(jnp.logical_and(b == 0, i == nt // 4))
        def _():
            ga.send_far()
            gb.send_mine()

        @pl.when(jnp.logical_and(b == 0, i == (3 * nt) // 4))
        def _():
            ga.pass_on((0, 1))

        @pl.when(jnp.logical_and(b == 1, i == (3 * nt) // 4))
        def _():
            ga.pass_on((2,))

        for j in range(3):
            @pl.when(jnp.logical_and(b == j + 1, i == 0))
            def _(j=j):
                ga.wait_passed((j,))
                load_chip(ga.chips[j][0], ga.chips[j][1], False)

        @pl.when(jnp.logical_and(b == 2, i == nt // 2))
        def _():
            gb.pass_on((0, 1))

        @pl.when(jnp.logical_and(b == 3, i == (3 * nt) // 4))
        def _():
            gb.pass_on((2,))

        slot = i % 2

        def u_write(t, sl):
            return pltpu.make_async_copy(u_buf.at[sl], u_ref.at[pl.ds(pl.multiple_of(t * tm, tm), tm), :], usem.at[sl])

        def u_read(t, sl):
            return pltpu.make_async_copy(u_ref.at[pl.ds(pl.multiple_of(t * tm, tm), tm), :], u_buf.at[sl], usem.at[sl])

        @pl.when(b == 0)
        def _():
            @pl.when(i >= 2)
            def _():
                u_write(i - 2, slot).wait()

            xv = x_ref[...]
            inv = lax.rsqrt(jnp.mean(xv * xv, axis=-1, keepdims=True) + EPS)
            u_buf[slot] = (xv * inv * g1_ref[...]).astype(_MXU)
            u_write(i, slot).start()

            @pl.when(i == nt - 1)
            def _():
                u_write(i - 1, 1 - slot).wait()
                u_write(i, slot).wait()
                u_read(0, 0).start()

        @pl.when(b > 0)
        def _():
            u_read(i, slot).wait()

            @pl.when(jnp.logical_or(b < 3, i < nt - 1))
            def _():
                u_read((i + 1) % nt, 1 - slot).start()

        z_ref[...] = _nt(u_buf[slot], w_vmem[...])

        @pl.when(jnp.logical_and(b == 3, i == nt - 1))
        def _():
            ga.finish_sends()
            gb.wait_sibling()
            gb.wait_passed((0, 1, 2))
            gb.finish_sends()

    any_spec = pl.BlockSpec(memory_space=pl.ANY)
    dma7 = pltpu.SemaphoreType.DMA((7,))
    return pl.pallas_call(
        body, name="inproj_fwd", grid=(4, nt),
        out_shape=(jax.ShapeDtypeStruct((s, D), _MXU), jax.ShapeDtypeStruct((s, NIN), F32),
                   jax.ShapeDtypeStruct((NDEV, rows1, D), own_first.dtype),
                   jax.ShapeDtypeStruct((NDEV, rows2, D), own_second.dtype)),
        in_specs=[pl.BlockSpec((tm, D), lambda b, i: (jnp.where(b == 0, i, nt - 1), 0)),
                  pl.BlockSpec((1, D), lambda b, i: (0, 0)), any_spec, any_spec],
        out_specs=(any_spec, pl.BlockSpec((tm, cw), lambda b, i: (i, chip_col(b))), any_spec, any_spec),
        scratch_shapes=[pltpu.VMEM((cw, D), _MXU), pltpu.VMEM((2, tm, D), _MXU), pltpu.VMEM((rows1, D), own_first.dtype),
                        pltpu.VMEM((rows2, D), own_second.dtype), pltpu.SemaphoreType.DMA((2,)),
                        pltpu.SemaphoreType.DMA((2,)),
                        dma7, dma7, pltpu.SemaphoreType.DMA, dma7, dma7, pltpu.SemaphoreType.DMA],
        compiler_params=_cparams(2),
    )(x, g1, own_first, own_second)


def _pool_tile(pbuf, t0, tm, pw_ref, scale_ref):
    t = t0 + lax.broadcasted_iota(jnp.int32, (tm, GD), 0)
    pooled, mixed_pre = [], []
    for g, w in enumerate(WINDOWS):
        cs = pl.ds(g * GD, GD)
        cur = pbuf[pl.ds(HALO, tm), cs]
        acc = cur
        for d in range(1, w):
            acc = acc + pbuf[pl.ds(HALO - d, tm), cs]
        cnt = jnp.minimum(t + 1, w).astype(F32)
        pg = acc / cnt - cur
        pooled.append(pg)
        mixed_pre.append(_nn(pg.astype(_MXU), pw_ref[g]))
    return pooled, mixed_pre


def _lru_gates_head(hh, lbuf, start, tm, cw_ref, cb_ref, wrg_ref, brg_ref, wig_ref, big_ref, sp):
    cs = pl.ds(hh * HD, HD)
    xc = cb_ref[:, cs] + cw_ref[pl.ds(CONV - 1, 1), cs] * lbuf[pl.ds(HALO, tm), cs]
    for k in range(CONV - 1):
        xc = xc + cw_ref[pl.ds(k, 1), cs] * lbuf[pl.ds(HALO - (CONV - 1) + k, tm), cs]
    xcm = xc.astype(_MXU)
    r = _sigmoid(_nn(xcm, wrg_ref[hh]) + brg_ref[pl.ds(hh, 1), :])
    ig = _sigmoid(_nn(xcm, wig_ref[hh]) + big_ref[pl.ds(hh, 1), :])
    a = jnp.exp(-LRU_C * r * sp[:, hh * HD:(hh + 1) * HD])
    one_m = 1.0 - a * a
    live = jnp.logical_and(one_m > 0.0, jnp.logical_not(start))
    inv_mult = lax.rsqrt(jnp.where(live, one_m, 1.0))
    mult = jnp.where(live, one_m * inv_mult, jnp.where(start, 1.0, 0.0))
    return xc, r, ig, a, live, inv_mult, mult


def _seg_layout(tm):
    seg = tm // 8
    return seg, seg + 8


def _to_segments(dst_ref, hh, val, tm):
    seg, pitch = _seg_layout(tm)
    for s in range(8):
        dst_ref[hh, pl.ds(s * pitch, seg), :] = val[s * seg:(s + 1) * seg, :]


def _from_segments(src_ref, hh, tm):
    seg, pitch = _seg_layout(tm)
    return jnp.concatenate([src_ref[hh, pl.ds(s * pitch, seg), :] for s in range(8)], axis=0)


def _segment_scan(a_ref, b_ref, out_ref, hk, pk, carry_ref, tm, reverse):
    seg, pitch = _seg_layout(tm)
    row = lax.broadcasted_iota(jnp.int32, (8, HD), 0)
    order = range(seg - 1, -1, -1) if reverse else range(seg)
    for hh in range(HEADS):
        cs = pl.ds(hh * HD, HD)
        if reverse:
            a0 = a_ref[hh, pl.ds(0, 8, stride=pitch), :]
            a_wrap = jnp.where(row <= 6, pltpu.roll(a0, 7, 0), 1.0)
        hv = jnp.zeros((8, HD), F32)
        pv = jnp.ones((8, HD), F32)
        for k in order:
            if not reverse:
                av = a_ref[hh, pl.ds(k, 8, stride=pitch), :]
            elif k + 1 < seg:
                av = a_ref[hh, pl.ds(k + 1, 8, stride=pitch), :]
            else:
                av = a_wrap
            hv = av * hv + b_ref[hh, pl.ds(k, 8, stride=pitch), :]
            pv = av * pv
            hk[hh, pl.ds(8 * k, 8), :] = hv
            pk[hh, pl.ds(8 * k, 8), :] = pv
        for d in (1, 2, 4):
            if reverse:
                keep, sh = row < 8 - d, 8 - d
            else:
                keep, sh = row >= d, d
            hv = hv + pv * jnp.where(keep, pltpu.roll(hv, sh, 0), 0.0)
            pv = pv * jnp.where(keep, pltpu.roll(pv, sh, 0), 1.0)
        cin = carry_ref[:, cs]
        ends = hv + pv * cin
        if reverse:
            enter = jnp.where(row <= 6, pltpu.roll(ends, 7, 0), cin)
            carry_ref[:, cs] = jnp.broadcast_to((a0 * ends)[0:1, :], (8, HD))
        else:
            enter = jnp.where(row >= 1, pltpu.roll(ends, 1, 0), cin)
            carry_ref[:, cs] = jnp.broadcast_to(ends[7:8, :], (8, HD))
        for k in range(seg):
            out_ref[hh, pl.ds(k, 8, stride=pitch), :] = hk[hh, pl.ds(8 * k, 8), :] + pk[hh, pl.ds(8 * k, 8), :] * enter


def _mixer_fwd(z, x, gw, small, own_third):
    s = x.shape[0]
    tm = min(TM_SEQ, s)
    nt = s // tm
    rows3 = own_third.shape[0]
    (pool_w, pool_scale, conv_w, conv_b, w_rg, b_rg, w_ig, b_ig, lam, b_gate) = small

    def body(z_ref, x_ref, gw_ref, own3_ref, pw_ref, ps_ref, cw_ref, cb_ref, wrg_ref, brg_ref, wig_ref, big_ref, lam_ref,
             bg_ref, h_ref, yl_ref, mg_ref, yp_ref, yr_ref, h1_ref, a_ref, r_ref, ig_ref, xc_ref, gw3_ref,
             pprojT, lru_w, wout_w, pbuf, lbuf, a_s, b_s, h_s, hk, pk, hcar, sems, stage3, send3, recv3, local3):
        i = pl.program_id(0)
        t0 = i * tm
        gc = _Gather(own3_ref, gw3_ref, stage3, send3, recv3, local3)

        @pl.when(i == 0)
        def _():
            gc.send_mine()
            _load_weights(gw_ref, [("pproj", pprojT, PW), ("lru", lru_w, D), ("wout", wout_w, D)], sems)
            pbuf[pl.ds(0, HALO), :] = jnp.zeros((HALO, PW), F32)
            lbuf[pl.ds(0, HALO), :] = jnp.zeros((HALO, D), F32)
            hcar[...] = jnp.zeros_like(hcar)

        @pl.when(i == nt // 2)
        def _():
            gc.pass_on((0, 1))

        @pl.when(i == (3 * nt) // 4)
        def _():
            gc.pass_on((2,))

        pbuf[pl.ds(HALO, tm), :] = z_ref[:, pl.ds(0, PW)]
        _, mixed_pre = _pool_tile(pbuf, t0, tm, pw_ref, ps_ref)
        mixed = jnp.concatenate(mixed_pre, axis=1) * ps_ref[...]
        y_pool = _nt(mixed.astype(_MXU), pprojT[...])
        pbuf[pl.ds(0, HALO), :] = pbuf[pl.ds(tm, HALO), :]

        lbuf[pl.ds(HALO, tm), :] = z_ref[:, pl.ds(PW, D)]
        sp, _ = _softplus_neg(lam_ref[...])
        start = (t0 + lax.broadcasted_iota(jnp.int32, (tm, HD), 0)) == 0
        for hh in range(HEADS):
            xc, r, ig, a, _, _, mult = _lru_gates_head(hh, lbuf, start, tm, cw_ref, cb_ref, wrg_ref, brg_ref,
                                                       wig_ref, big_ref, sp)
            _to_segments(a_s, hh, a, tm)
            _to_segments(b_s, hh, mult * ig * xc, tm)
            cs = pl.ds(hh * HD, HD)
            a_ref[:, cs] = a
            r_ref[:, cs] = r.astype(_MXU)
            ig_ref[:, cs] = ig.astype(_MXU)
            xc_ref[:, cs] = xc.astype(_MXU)
        lbuf[pl.ds(0, HALO), :] = lbuf[pl.ds(tm, HALO), :]
        _segment_scan(a_s, b_s, h_s, hk, pk, hcar, tm, reverse=False)
        for hh in range(HEADS):
            h_ref[:, pl.ds(hh * HD, HD)] = _from_segments(h_s, hh, tm)
        gel, _ = _gelu_and_grad(z_ref[:, pl.ds(PW + D, D)])
        yl = (h_ref[...] * gel).astype(_MXU)
        yl_ref[...] = yl
        y_lru = _nn(yl, lru_w[...])

        g0 = _sigmoid(z_ref[:, pl.ds(PW + 2 * D, D)] + bg_ref[pl.ds(0, 1), :])
        g1 = _sigmoid(z_ref[:, pl.ds(PW + 3 * D, D)] + bg_ref[pl.ds(1, 1), :])
        merged = (g0 * y_pool + g1 * y_lru).astype(_MXU)
        mg_ref[...] = merged
        yp_ref[...] = y_pool.astype(_MXU)
        yr_ref[...] = y_lru.astype(_MXU)
        h1_ref[...] = x_ref[...] + _nn(merged, wout_w[...])

        @pl.when(i == nt - 1)
        def _():
            gc.wait_sibling()
            gc.wait_passed((0, 1, 2))
            gc.finish_sends()

    tok = lambda w, dt: jax.ShapeDtypeStruct((s, w), dt)
    tspec = lambda w: pl.BlockSpec((tm, w), lambda i: (i, 0))
    full = lambda a: pl.BlockSpec(a.shape, lambda i: (0,) * a.ndim)
    any_spec = pl.BlockSpec(memory_space=pl.ANY)
    seg_buf = pltpu.VMEM((HEADS, 8 * _seg_layout(tm)[1], HD), F32)
    dma7 = pltpu.SemaphoreType.DMA((7,))
    return pl.pallas_call(
        body, name="mixer_fwd", grid=(nt,),
        out_shape=(tok(D, F32), tok(D, _MXU), tok(D, _MXU), tok(D, _MXU), tok(D, _MXU), tok(D, F32),
                   tok(D, F32), tok(D, _MXU), tok(D, _MXU), tok(D, _MXU),
                   jax.ShapeDtypeStruct((NDEV, rows3, D), own_third.dtype)),
        in_specs=[tspec(NIN), tspec(D), any_spec, any_spec] + [full(a) for a in small],
        out_specs=(tspec(D),) * 10 + (any_spec,),
        scratch_shapes=[pltpu.VMEM((D, PW), _MXU), pltpu.VMEM((D, D), _MXU), pltpu.VMEM((D, D), _MXU),
                        pltpu.VMEM((tm + HALO, PW), F32), pltpu.VMEM((tm + HALO, D), F32),
                        seg_buf, seg_buf, seg_buf, pltpu.VMEM((HEADS, tm, HD), F32), pltpu.VMEM((HEADS, tm, HD), F32),
                        pltpu.VMEM((8, D), F32), pltpu.SemaphoreType.DMA((3 * NDEV,)),
                        pltpu.VMEM((rows3, D), own_third.dtype), dma7, dma7, pltpu.SemaphoreType.DMA],
        compiler_params=_cparams(1),
    )(z, x, gw, own_third, *small)


def _ffn_fwd(h1, g2, gw):
    s = h1.shape[0]
    tm = min(TM, s)
    half = FF // 2

    def body(h1_ref, g2_ref, gw_ref, v_ref, gf_ref, uf_ref, h2_ref, wffnT, wffo, sems):
        @pl.when(pl.program_id(0) == 0)
        def _():
            _load_weights(gw_ref, [("wffn", wffnT, D), ("wffo", wffo, D)], sems)

        hv = h1_ref[...]
        inv = lax.rsqrt(jnp.mean(hv * hv, axis=-1, keepdims=True) + EPS)
        v = (hv * inv * g2_ref[...]).astype(_MXU)
        v_ref[...] = v
        acc = hv
        for ch in range(2):
            cs = pl.ds(ch * half, half)
            gf = _nt(v, wffnT[pl.ds(ch * half, half), :]).astype(_MXU)
            uf = _nt(v, wffnT[pl.ds(FF + ch * half, half), :]).astype(_MXU)
            gf_ref[:, cs] = gf
            uf_ref[:, cs] = uf
            gf32 = gf.astype(F32)
            act = (gf32 * _sigmoid(gf32) * uf.astype(F32)).astype(_MXU)
            acc = acc + _nn(act, wffo[pl.ds(ch * half, half), :])
        h2_ref[...] = acc

    tspec = lambda w: pl.BlockSpec((tm, w), lambda i: (i, 0))
    return pl.pallas_call(
        body, name="ffn_fwd", grid=(s // tm,),
        out_shape=(jax.ShapeDtypeStruct((s, D), _MXU), jax.ShapeDtypeStruct((s, FF), _MXU),
                   jax.ShapeDtypeStruct((s, FF), _MXU), jax.ShapeDtypeStruct((s, D), F32)),
        in_specs=[tspec(D), pl.BlockSpec((1, D), lambda i: (0, 0)), pl.BlockSpec(memory_space=pl.ANY)],
        out_specs=(tspec(D), tspec(FF), tspec(FF), tspec(D)),
        scratch_shapes=[pltpu.VMEM((2 * FF, D), _MXU), pltpu.VMEM((FF, D), _MXU), pltpu.SemaphoreType.DMA((2 * NDEV,))],
        compiler_params=_cparams(1),
    )(h1, g2, gw)


def _rms_bwd(dy, xn, inv, g):
    dg = jnp.sum(dy * xn, axis=0, keepdims=True)
    dxn = dy * g
    dx = inv * (dxn - xn * jnp.mean(dxn * xn, axis=-1, keepdims=True))
    return dx, dg


def _ple_loss_fwd_bwd(h2, p, target, g3, gfin, gw):
    s = h2.shape[0]
    tm = min(TM, s)

    def body(h2_ref, p_ref, t_ref, g3_ref, gf_ref, gw_ref,
             dh2_ref, loss_ref, dg3_ref, dgf_ref, gwpg_ref, gple_ref, wpg, pleT, sems):
        i = pl.program_id(0)

        @pl.when(i == 0)
        def _():
            _load_weights(gw_ref, [("wpg", wpg, D), ("ple", pleT, PLE)], sems)
            for ref in (loss_ref, dg3_ref, dgf_ref, gwpg_ref, gple_ref):
                ref[...] = jnp.zeros_like(ref)

        hv = h2_ref[...]
        inv3 = lax.rsqrt(jnp.mean(hv * hv, axis=-1, keepdims=True) + EPS)
        xn3 = hv * inv3
        n3 = (xn3 * g3_ref[...]).astype(_MXU)
        pg = _sigmoid(_nn(n3, wpg[...]))
        pm = p_ref[...].astype(_MXU)
        e = _nt(pm, pleT[...])
        h3 = hv + pg * e
        invf = lax.rsqrt(jnp.mean(h3 * h3, axis=-1, keepdims=True) + EPS)
        xf = h3 * invf
        diff = xf * gf_ref[...] - t_ref[...]
        loss_ref[...] += jnp.sum(diff * diff) * (0.5 / D)
        dh3, dgf = _rms_bwd(diff * (1.0 / D), xf, invf, gf_ref[...])
        dgf_ref[...] += dgf
        gple_ref[:, pl.ds(0, PLE)] += _tn((dh3 * pg).astype(_MXU), pm)
        dpg = (dh3 * e * pg * (1.0 - pg)).astype(_MXU)
        gwpg_ref[...] += _tn(n3, dpg)
        dn3 = _nt(dpg, wpg[...])
        dx3, dg3 = _rms_bwd(dn3, xn3, inv3, g3_ref[...])
        dg3_ref[...] += dg3
        dh2_ref[...] = dh3 + dx3

    tspec = lambda w: pl.BlockSpec((tm, w), lambda i: (i, 0))
    vec = pl.BlockSpec((1, D), lambda i: (0, 0))
    mat = pl.BlockSpec((D, D), lambda i: (0, 0))
    return pl.pallas_call(
        body, name="ple_loss", grid=(s // tm,),
        out_shape=(jax.ShapeDtypeStruct((s, D), F32), jax.ShapeDtypeStruct((8, 128), F32),
                   jax.ShapeDtypeStruct((1, D), F32), jax.ShapeDtypeStruct((1, D), F32),
                   jax.ShapeDtypeStruct((D, D), F32), jax.ShapeDtypeStruct((D, D), F32)),
        in_specs=[tspec(D), tspec(PLE), tspec(D), vec, vec, pl.BlockSpec(memory_space=pl.ANY)],
        out_specs=(tspec(D), pl.BlockSpec((8, 128), lambda i: (0, 0)), vec, vec, mat, mat),
        scratch_shapes=[pltpu.VMEM((D, D), _MXU), pltpu.VMEM((D, PLE), _MXU), pltpu.SemaphoreType.DMA((2 * NDEV,))],
        compiler_params=_cparams(1),
    )(h2, p, target, g3, gfin, gw)


def _ffn_bwd_hidden(dh2, gf, uf, gw):
    s = dh2.shape[0]
    tm = min(TM, s)
    nt = s // tm
    half = FF // 2

    def body(dh2_ref, gf_ref, uf_ref, gw_ref, dff_ref, gwo_ref, wffo, gacc, sems):
        i = pl.program_id(0)

        @pl.when(i == 0)
        def _():
            _load_weights(gw_ref, [("wffo", wffo, D)], sems)
            gacc[...] = jnp.zeros_like(gacc)

        dm = dh2_ref[...].astype(_MXU)
        for ch in range(2):
            cs = pl.ds(ch * half, half)
            dact = _nt(dm, wffo[cs, :])
            gfv = gf_ref[:, cs].astype(F32)
            ufv = uf_ref[:, cs].astype(F32)
            sg = _sigmoid(gfv)
            silu = gfv * sg
            gacc[cs, :] += _tn((silu * ufv).astype(_MXU), dm)
            dff_ref[:, pl.ds(ch * half, half)] = (dact * ufv * (sg * (1.0 + gfv * (1.0 - sg)))).astype(_MXU)
            dff_ref[:, pl.ds(FF + ch * half, half)] = (dact * silu).astype(_MXU)

        @pl.when(i == nt - 1)
        def _():
            pltpu.sync_copy(gacc, gwo_ref)

    tspec = lambda w: pl.BlockSpec((tm, w), lambda i: (i, 0))
    return pl.pallas_call(
        body, name="ffn_bwd_hidden", grid=(nt,),
        out_shape=(jax.ShapeDtypeStruct((s, 2 * FF), _MXU), jax.ShapeDtypeStruct((FF, D), F32)),
        in_specs=[tspec(D), tspec(FF), tspec(FF), pl.BlockSpec(memory_space=pl.ANY)],
        out_specs=(tspec(2 * FF), pl.BlockSpec(memory_space=pl.ANY)),
        scratch_shapes=[pltpu.VMEM((FF, D), _MXU), pltpu.VMEM((FF, D), F32), pltpu.SemaphoreType.DMA((NDEV,))],
        compiler_params=_cparams(1),
    )(dh2, gf, uf, gw)


def _proj_norm_bwd(dy, x, dres, g, gw, slab, width, name, lhs=None):
    s = x.shape[0]
    tm = min(TM, s)
    nl = 0 if lhs is None else 1

    def body(*refs):
        dy_ref, x_ref, dr_ref, g_ref = refs[:4]
        l_refs = refs[4:4 + nl]
        gw_ref, dx_ref, dg_ref = refs[4 + nl:7 + nl]
        gl_refs = refs[7 + nl:7 + 2 * nl]
        wT, sems = refs[7 + 2 * nl:]

        @pl.when(pl.program_id(0) == 0)
        def _():
            _load_weights(gw_ref, [(slab, wT, D)], sems)
            dg_ref[...] = jnp.zeros_like(dg_ref)
            for ref in gl_refs:
                ref[...] = jnp.zeros_like(ref)

        dv = _nn(dy_ref[...], wT[...])
        xv = x_ref[...]
        inv = lax.rsqrt(jnp.mean(xv * xv, axis=-1, keepdims=True) + EPS)
        dx, dg = _rms_bwd(dv, xv * inv, inv, g_ref[...])
        dg_ref[...] += dg
        dr = dr_ref[...]
        dx_ref[...] = dr + dx
        for l_ref, gl_ref in zip(l_refs, gl_refs):
            gl_ref[...] += _tn(l_ref[...], dr.astype(_MXU))

    tspec = lambda w: pl.BlockSpec((tm, w), lambda i: (i, 0))
    vec = pl.BlockSpec((1, D), lambda i: (0, 0))
    mat = pl.BlockSpec((D, D), lambda i: (0, 0))
    return pl.pallas_call(
        body, name=name, grid=(s // tm,),
        out_shape=(jax.ShapeDtypeStruct((s, D), F32), jax.ShapeDtypeStruct((1, D), F32))
        + (jax.ShapeDtypeStruct((D, D), F32),) * nl,
        in_specs=[tspec(width), tspec(D), tspec(D), vec] + [tspec(D)] * nl + [pl.BlockSpec(memory_space=pl.ANY)],
        out_specs=(tspec(D), vec) + (mat,) * nl,
        scratch_shapes=[pltpu.VMEM((width, D), _MXU), pltpu.SemaphoreType.DMA((NDEV,))],
        compiler_params=_cparams(1),
    )(dy, x, dres, g, *([] if lhs is None else [lhs]), gw)


def _mixer_bwd(dh1, z, h, y_pool, y_lru, saved, gw, small):
    s = dh1.shape[0]
    tm = min(TM_SEQ, s)
    nt = s // tm
    (pool_w, pool_scale, conv_w, conv_b, w_rg, b_rg, w_ig, b_ig, lam, b_gate) = small

    def body(dh1_ref, z_ref, zp_ref, h_ref, hp_ref, yp_ref, yr_ref, a_ref, r_ref, ig_ref, xc_ref, gw_ref,
             pw_ref, ps_ref, cw_ref, cb_ref, wrg_ref, brg_ref, wig_ref, big_ref, lam_ref, bg_ref,
             dz_ref, dyr_ref, dyp_ref, mx_ref,
             gbg_ref, glam_ref, gbrg_ref, gbig_ref, gcb_ref, gcw_ref, gps_ref, gpw_ref, gwrg_ref, gwig_ref,
             pprojT, lru_w, wout_w, pbuf, lbuf, hbuf, qbuf, xbuf, a_s, g_s, dh_s, hk, pk, dcar, sems):
        step = pl.program_id(0)
        i = nt - 1 - step
        t0 = i * tm

        @pl.when(step == 0)
        def _():
            _load_weights(gw_ref, [("pproj", pprojT, PW), ("lru", lru_w, D), ("wout", wout_w, D)], sems)
            for ref in (gbg_ref, glam_ref, gbrg_ref, gbig_ref, gcb_ref, gcw_ref, gps_ref, gpw_ref, gwrg_ref, gwig_ref):
                ref[...] = jnp.zeros_like(ref)
            qbuf[pl.ds(tm, HALO), :] = jnp.zeros((HALO, PW), F32)
            xbuf[pl.ds(tm, 8), :] = jnp.zeros((8, D), F32)
            dcar[...] = jnp.zeros_like(dcar)

        first = i == 0
        zprev = jnp.where(first, 0.0, zp_ref[...])
        hprev = jnp.where(first, 0.0, hp_ref[...])

        d_merged = _nt(dh1_ref[...].astype(_MXU), wout_w[...])

        g0 = _sigmoid(z_ref[:, pl.ds(PW + 2 * D, D)] + bg_ref[pl.ds(0, 1), :])
        g1 = _sigmoid(z_ref[:, pl.ds(PW + 3 * D, D)] + bg_ref[pl.ds(1, 1), :])
        dz0 = d_merged * yp_ref[...].astype(F32) * g0 * (1.0 - g0)
        dz1 = d_merged * yr_ref[...].astype(F32) * g1 * (1.0 - g1)
        dz_ref[:, pl.ds(PW + 2 * D, D)] = dz0.astype(_MXU)
        dz_ref[:, pl.ds(PW + 3 * D, D)] = dz1.astype(_MXU)
        gbg_ref[pl.ds(0, 1), :] += jnp.sum(dz0, axis=0, keepdims=True)
        gbg_ref[pl.ds(1, 1), :] += jnp.sum(dz1, axis=0, keepdims=True)
        d_ypool = (d_merged * g0).astype(_MXU)
        d_ylru = (d_merged * g1).astype(_MXU)
        dyp_ref[...] = d_ypool
        dyr_ref[...] = d_ylru

        d_yl = _nt(d_ylru, lru_w[...])
        gel, dgel = _gelu_and_grad(z_ref[:, pl.ds(PW + D, D)])
        dz_ref[:, pl.ds(PW + D, D)] = (d_yl * h_ref[...] * dgel).astype(_MXU)
        g_full = d_yl * gel
        lbuf[pl.ds(0, HALO), :] = zprev[:, PW:PW + D]
        lbuf[pl.ds(HALO, tm), :] = z_ref[:, pl.ds(PW, D)]
        hbuf[pl.ds(0, 8), :] = hprev
        hbuf[pl.ds(8, tm), :] = h_ref[...]
        sp, sneg = _softplus_neg(lam_ref[...])
        start = (t0 + lax.broadcasted_iota(jnp.int32, (tm, HD), 0)) == 0
        for hh in range(HEADS):
            cs = pl.ds(hh * HD, HD)
            _to_segments(a_s, hh, a_ref[:, cs], tm)
            _to_segments(g_s, hh, g_full[:, hh * HD:(hh + 1) * HD], tm)
        _segment_scan(a_s, g_s, dh_s, hk, pk, dcar, tm, reverse=True)
        for hh in range(HEADS):
            cs = pl.ds(hh * HD, HD)
            a = a_ref[:, cs]
            r = r_ref[:, cs].astype(F32)
            ig = ig_ref[:, cs].astype(F32)
            xc = xc_ref[:, cs].astype(F32)
            a2 = a * a
            one_m = 1.0 - a2
            live = jnp.logical_and(one_m > 0.0, jnp.logical_not(start))
            inv_mult = lax.rsqrt(jnp.where(live, one_m, 1.0))
            mult = jnp.where(live, one_m * inv_mult, jnp.where(start, 1.0, 0.0))
            dh = _from_segments(dh_s, hh, tm)
            d_mult = dh * ig * xc
            d_loga = dh * hbuf[pl.ds(7, tm), cs] * a - jnp.where(live, d_mult * a2 * inv_mult, 0.0)
            glam_ref[:, cs] += jnp.sum(d_loga * (LRU_C * r) * sneg[:, hh * HD:(hh + 1) * HD], axis=0, keepdims=True)
            d_rpre = d_loga * (-LRU_C * sp[:, hh * HD:(hh + 1) * HD]) * r * (1.0 - r)
            d_igpre = dh * mult * xc * ig * (1.0 - ig)
            gbrg_ref[pl.ds(hh, 1), :] += jnp.sum(d_rpre, axis=0, keepdims=True)
            gbig_ref[pl.ds(hh, 1), :] += jnp.sum(d_igpre, axis=0, keepdims=True)
            drm = d_rpre.astype(_MXU)
            dim = d_igpre.astype(_MXU)
            xcm = xc.astype(_MXU)
            gwrg_ref[hh] += _tn(xcm, drm)
            gwig_ref[hh] += _tn(xcm, dim)
            d_xc = dh * mult * ig + _nt(drm, wrg_ref[hh]) + _nt(dim, wig_ref[hh])
            gcb_ref[:, cs] += jnp.sum(d_xc, axis=0, keepdims=True)
            for k in range(CONV):
                gcw_ref[pl.ds(k, 1), cs] += jnp.sum(d_xc * lbuf[pl.ds(HALO - (CONV - 1) + k, tm), cs], axis=0,
                                                    keepdims=True)
            xbuf[pl.ds(0, tm), cs] = d_xc
        dzl = cw_ref[pl.ds(CONV - 1, 1), :] * xbuf[pl.ds(0, tm), :]
        for k in range(CONV - 1):
            dzl = dzl + cw_ref[pl.ds(k, 1), :] * xbuf[pl.ds(CONV - 1 - k, tm), :]
        dz_ref[:, pl.ds(PW, D)] = dzl.astype(_MXU)
        xbuf[pl.ds(tm, 8), :] = xbuf[pl.ds(0, 8), :]

        d_mixed = _nn(d_ypool, pprojT[...])
        pbuf[pl.ds(0, HALO), :] = zprev[:, 0:PW]
        pbuf[pl.ds(HALO, tm), :] = z_ref[:, pl.ds(0, PW)]
        pooled, mixed_pre = _pool_tile(pbuf, t0, tm, pw_ref, ps_ref)
        mp = jnp.concatenate(mixed_pre, axis=1)
        mx_ref[...] = (mp * ps_ref[...]).astype(_MXU)
        gps_ref[...] += jnp.sum(d_mixed * mp, axis=0, keepdims=True)
        d_mp = (d_mixed * ps_ref[...]).astype(_MXU)
        t = t0 + lax.broadcasted_iota(jnp.int32, (tm, GD), 0)
        d_pooled = []
        for g, w in enumerate(WINDOWS):
            dmg = d_mp[:, g * GD:(g + 1) * GD]
            gpw_ref[g] += _tn(pooled[g].astype(_MXU), dmg)
            dp = _nt(dmg, pw_ref[g])
            d_pooled.append(dp)
            qbuf[pl.ds(0, tm), pl.ds(g * GD, GD)] = dp / jnp.minimum(t + 1, w).astype(F32)
        for g, w in enumerate(WINDOWS):
            cs = pl.ds(g * GD, GD)
            acc = qbuf[pl.ds(0, tm), cs]
            for d in range(1, w):
                acc = acc + qbuf[pl.ds(d, tm), cs]
            dz_ref[:, cs] = (acc - d_pooled[g]).astype(_MXU)
        qbuf[pl.ds(tm, HALO), :] = qbuf[pl.ds(0, HALO), :]

    rev = lambda w: pl.BlockSpec((tm, w), lambda g: (nt - 1 - g, 0))
    prev = lambda rows, w: pl.BlockSpec((rows, w), lambda g: (jnp.maximum((nt - 1 - g) * (tm // rows) - 1, 0), 0))
    full = lambda a: pl.BlockSpec(a.shape, lambda g: (0,) * a.ndim)
    tok = lambda w, dt: jax.ShapeDtypeStruct((s, w), dt)
    acc_shapes = [(2, D), (1, D), (HEADS, HD), (HEADS, HD), (1, D), (CONV, D), (1, PW), (GROUPS, GD, GD),
                  (HEADS, HD, HD), (HEADS, HD, HD)]
    acc_specs = tuple(pl.BlockSpec(sh, lambda g, n=len(sh): (0,) * n) for sh in acc_shapes)
    seg_buf = pltpu.VMEM((HEADS, 8 * _seg_layout(tm)[1], HD), F32)
    a_in, r_in, ig_in, xc_in = saved
    return pl.pallas_call(
        body, name="mixer_bwd", grid=(nt,),
        out_shape=(tok(NIN, _MXU), tok(D, _MXU), tok(D, _MXU), tok(PW, _MXU))
        + tuple(jax.ShapeDtypeStruct(sh, F32) for sh in acc_shapes),
        in_specs=[rev(D), rev(NIN), prev(HALO, NIN), rev(D), prev(8, D), rev(D), rev(D), rev(D), rev(D), rev(D), rev(D),
                  pl.BlockSpec(memory_space=pl.ANY)] + [full(a) for a in small],
        out_specs=(rev(NIN), rev(D), rev(D), rev(PW)) + acc_specs,
        scratch_shapes=[pltpu.VMEM((D, PW), _MXU), pltpu.VMEM((D, D), _MXU), pltpu.VMEM((D, D), _MXU),
                        pltpu.VMEM((tm + HALO, PW), F32), pltpu.VMEM((tm + HALO, D), F32),
                        pltpu.VMEM((tm + 8, D), F32), pltpu.VMEM((tm + HALO, PW), F32), pltpu.VMEM((tm + 8, D), F32),
                        seg_buf, seg_buf, seg_buf, pltpu.VMEM((HEADS, tm, HD), F32), pltpu.VMEM((HEADS, tm, HD), F32),
                        pltpu.VMEM((8, D), F32), pltpu.SemaphoreType.DMA((3 * NDEV,))],
        compiler_params=_cparams(1),
    )(dh1, z, z, h, h, y_pool, y_lru, a_in, r_in, ig_in, xc_in, gw, *small)


def _split3(a):
    hi = a.astype(jnp.bfloat16).astype(F32)
    mid = (a - hi).astype(jnp.bfloat16).astype(F32)
    lo = (a - hi - mid).astype(jnp.bfloat16).astype(F32)
    return jnp.stack([hi, mid, lo])


def _small_pack(parts):
    flat = jnp.concatenate([a.reshape(-1) for a in parts])
    return jnp.pad(flat, (0, NDEV * SMALL_ROWS * D - flat.shape[0])).reshape(NDEV * SMALL_ROWS, D)


def _small_unpack(packed, shapes):
    flat = packed.reshape(-1)
    out, o = [], 0
    for sh in shapes:
        n = math.prod(sh)
        out.append(flat[o:o + n].reshape(sh))
        o += n
    return out


def kernel(x, p, norm1_g, w_in, b_gate, pool_w, pool_scale, pool_proj, conv_w, conv_b, w_rg, b_rg, w_ig, b_ig, lru_lambda, lru_proj, w_out, norm2_g, w_ffn_in, w_ffn_out, ple_norm_g, w_ple_gate, w_ple_proj, final_g, loss_target, m_norm1_g, m_w_in, m_b_gate, m_pool_w, m_pool_scale, m_pool_proj, m_conv_w, m_conv_b, m_w_rg, m_b_rg, m_w_ig, m_b_ig, m_lru_lambda, m_lru_proj, m_w_out, m_norm2_g, m_w_ffn_in, m_w_ffn_out, m_ple_norm_g, m_w_ple_gate, m_w_ple_proj, m_final_g, v_norm1_g, v_w_in, v_b_gate, v_pool_w, v_pool_scale, v_pool_proj, v_conv_w, v_conv_b, v_w_rg, v_b_rg, v_w_ig, v_b_ig, v_lru_lambda, v_lru_proj, v_w_out, v_norm2_g, v_w_ffn_in, v_w_ffn_out, v_ple_norm_g, v_w_ple_gate, v_w_ple_proj, v_final_g):
    axes = ("x", "y", "c")
    me = 4 * lax.axis_index("x") + 2 * lax.axis_index("y") + lax.axis_index("c")
    x2 = x[0]
    p2 = p[0, 0]
    tgt = loss_target[0]

    n_small = (CONV + 2) * 128
    small_terms = _split3(jnp.concatenate([conv_w[0].reshape(-1), b_gate[0].reshape(-1)]))
    small_rows = jnp.pad(small_terms, ((0, 16 - 3), (0, D - n_small)))
    own_first = jnp.concatenate([w_in[0].T.astype(_MXU), small_rows.astype(_MXU)], axis=0)
    own_second = jnp.concatenate([
        jnp.pad(pool_proj[0].T, ((0, 0), (0, D - PW))).astype(_MXU), lru_proj[0].astype(_MXU), w_out[0].astype(_MXU),
    ], axis=0)
    own_third = jnp.concatenate([
        w_ffn_in[0].T.astype(_MXU), jnp.pad(w_ple_proj[0].T, ((0, 0), (0, D - PLE))).astype(_MXU),
        w_ffn_out[0].astype(_MXU), w_ple_gate[0].astype(_MXU),
    ], axis=0)
    u, z, gw_first, gw = _inproj_fwd(x2, norm1_g, own_first, own_second)
    off = W_OFF["f32s"][0]
    st = gw_first[:, off:off + 3, :n_small].astype(F32)
    sf = st[:, 0] + st[:, 1] + st[:, 2]
    conv_w_full = sf[:, :CONV * 128].reshape(NDEV, CONV, 128).transpose(1, 0, 2).reshape(CONV, D)
    b_gate_full = sf[:, CONV * 128:].reshape(NDEV, 2, 128).transpose(1, 0, 2).reshape(2, D)

    small = (pool_w[0].astype(_MXU), pool_scale, conv_w_full, conv_b, w_rg[0].astype(_MXU), b_rg[0],
             w_ig[0].astype(_MXU), b_ig[0], lru_lambda, b_gate_full)

    h, yl, merged, y_pool, y_lru, h1, *saved, gw_third = _mixer_fwd(z, x2, gw, small, own_third)
    v, gf, uf, h2 = _ffn_fwd(h1, norm2_g, gw_third)

    dh2, loss_blk, g_ple_norm, g_final, part_wpg, part_ple = _ple_loss_fwd_bwd(h2, p2, tgt, ple_norm_g,
                                                                               final_g.reshape(1, D), gw_third)
    dff, part_wffo = _ffn_bwd_hidden(dh2, gf, uf, gw_third)
    dh1, g_norm2 = _proj_norm_bwd(dff, h1, dh2, norm2_g, gw_third, "wffn", 2 * FF, "ffn_bwd_in")
    (dz, d_ylru, d_ypool, mixed, g_bgate, g_lam, g_brg, g_big, g_convb, g_convw, g_pscale, g_poolw, g_wrg,
     g_wig) = _mixer_bwd(dh1, z, h, y_pool, y_lru, saved, gw, small)
    grad_x, g_norm1, part_wout = _proj_norm_bwd(dz, x2, dh1, norm1_g, gw_first, "win", NIN, "inproj_bwd", lhs=merged)

    small_shapes = [(1, D), (GROUPS, GD, GD), (1, PW), (1, D), (HEADS, HD, HD), (HEADS, HD), (HEADS, HD, HD),
                    (HEADS, HD), (1, D), (1, D), (1, D), (1, D), (2, D), (CONV, D), (1, 1)]
    small_part = _small_pack([g_norm1, g_poolw, g_pscale, g_convb, g_wrg, g_brg, g_wig, g_big, g_lam, g_norm2,
                              g_ple_norm, g_final, g_bgate, g_convw, loss_blk[0:1, 0:1]])
    riders = [_grad_matmul(yl, d_ylru, "grad_lru_proj"), part_wout, _grad_matmul(d_ypool, mixed, "grad_pool_proj")]
    rs_wffn = _grad_matmul_rs(dff, v, "grad_w_ffn_in", 704, extras=[part_wffo, part_wpg, part_ple], narrow=_MXU)
    rs_win = _grad_matmul_rs(dz, u, "grad_w_in", 576, extras=riders + [small_part], narrow=_MXU, tail=SMALL_ROWS)

    def reduced(parts, name):
        return [_sum_arrays([t_own, landed[0], landed[1], landed[2]], "rs_sum_" + name + str(n))
                for n, (t_own, landed) in enumerate(parts)]

    red_wffn, = reduced(rs_wffn, "wffn")
    red_win, red_small = reduced(rs_win, "win")
    g_w_in = red_win[:576].T
    g_w_ffn_in = red_wffn[:704].T
    g_w_ffn_out = red_wffn[704:1056]
    g_w_ple_gate = red_wffn[1056:1184]
    g_w_ple_proj = red_wffn[1184:1312, :PLE].T
    g_lru_proj, g_w_out = red_win[576:704], red_win[704:832]
    g_pool_proj = red_win[832:960, :PW].T
    small_red = _all_gather_small(red_small)
    (gs_norm1, gs_poolw, gs_pscale, gs_convb, gs_wrg, gs_brg, gs_wig, gs_big, gs_lam, gs_norm2, gs_ple_norm,
     gs_final, gs_bgate, gs_convw, loss_sum) = _small_unpack(small_red, small_shapes)
    loss = loss_sum[0, 0]
    g_b_gate = lax.dynamic_slice_in_dim(gs_bgate, me * 128, 128, axis=1)
    g_conv_w = lax.dynamic_slice_in_dim(gs_convw, me * 128, 128, axis=1)

    grads = {
        "norm1_g": gs_norm1, "w_in": g_w_in[None], "b_gate": g_b_gate[None], "pool_w": gs_poolw[None],
        "pool_scale": gs_pscale, "pool_proj": g_pool_proj[None], "conv_w": g_conv_w[None], "conv_b": gs_convb,
        "w_rg": gs_wrg[None], "b_rg": gs_brg[None], "w_ig": gs_wig[None], "b_ig": gs_big[None], "lru_lambda": gs_lam,
        "lru_proj": g_lru_proj[None], "w_out": g_w_out[None], "norm2_g": gs_norm2, "w_ffn_in": g_w_ffn_in[None],
        "w_ffn_out": g_w_ffn_out[None], "ple_norm_g": gs_ple_norm, "w_ple_gate": g_w_ple_gate[None],
        "w_ple_proj": g_w_ple_proj[None], "final_g": gs_final.reshape(D),
    }
    weights = dict(norm1_g=norm1_g, w_in=w_in, b_gate=b_gate, pool_w=pool_w, pool_scale=pool_scale, pool_proj=pool_proj,
                   conv_w=conv_w, conv_b=conv_b, w_rg=w_rg, b_rg=b_rg, w_ig=w_ig, b_ig=b_ig, lru_lambda=lru_lambda,
                   lru_proj=lru_proj, w_out=w_out, norm2_g=norm2_g, w_ffn_in=w_ffn_in, w_ffn_out=w_ffn_out,
                   ple_norm_g=ple_norm_g, w_ple_gate=w_ple_gate, w_ple_proj=w_ple_proj, final_g=final_g)
    moments_m = dict(norm1_g=m_norm1_g, w_in=m_w_in, b_gate=m_b_gate, pool_w=m_pool_w, pool_scale=m_pool_scale,
                     pool_proj=m_pool_proj, conv_w=m_conv_w, conv_b=m_conv_b, w_rg=m_w_rg, b_rg=m_b_rg, w_ig=m_w_ig,
                     b_ig=m_b_ig, lru_lambda=m_lru_lambda, lru_proj=m_lru_proj, w_out=m_w_out, norm2_g=m_norm2_g,
                     w_ffn_in=m_w_ffn_in, w_ffn_out=m_w_ffn_out, ple_norm_g=m_ple_norm_g, w_ple_gate=m_w_ple_gate,
                     w_ple_proj=m_w_ple_proj, final_g=m_final_g)
    moments_v = dict(norm1_g=v_norm1_g, w_in=v_w_in, b_gate=v_b_gate, pool_w=v_pool_w, pool_scale=v_pool_scale,
                     pool_proj=v_pool_proj, conv_w=v_conv_w, conv_b=v_conv_b, w_rg=v_w_rg, b_rg=v_b_rg, w_ig=v_w_ig,
                     b_ig=v_b_ig, lru_lambda=v_lru_lambda, lru_proj=v_lru_proj, w_out=v_w_out, norm2_g=v_norm2_g,
                     w_ffn_in=v_w_ffn_in, w_ffn_out=v_w_ffn_out, ple_norm_g=v_ple_norm_g, w_ple_gate=v_w_ple_gate,
                     w_ple_proj=v_w_ple_proj, final_g=v_final_g)
    names = list(weights)
    big = ("w_in", "w_ffn_in", "w_ffn_out", "lru_proj", "w_out", "w_ple_gate", "pool_proj", "w_ple_proj")
    slab_space = {"w_in": red_win[:576], "w_ffn_in": red_wffn[:704]}
    delta, new_m, new_v = {}, {}, {}
    for n in big:
        sh = weights[n].shape
        if n in slab_space:
            as2d = lambda a: a[0].T
            back = lambda a: a.T[None]
            g2d = slab_space[n]
        else:
            as2d = lambda a: a.reshape(sh[-2], sh[-1])
            back = lambda a: a.reshape(sh)
            g2d = as2d(grads[n])
        d_, m_, v_ = _adamw(as2d(weights[n]), g2d, as2d(moments_m[n]), as2d(moments_v[n]), "adamw_" + n)
        delta[n], new_m[n], new_v[n] = back(d_), back(m_), back(v_)
    rest = [n for n in names if n not in big]
    rest_shapes = [weights[n].shape for n in rest]
    packed = [_small_pack([src[n] for n in rest]) for src in (weights, grads, moments_m, moments_v)]
    d_, m_, v_ = _adamw(*packed, "adamw_small")
    for n, a, b_, c_ in zip(rest, _small_unpack(d_, rest_shapes), _small_unpack(m_, rest_shapes),
                            _small_unpack(v_, rest_shapes)):
        delta[n], new_m[n], new_v[n] = a, b_, c_

    return (loss, grad_x[None], *[grads[n] for n in names], *[delta[n] for n in names],
            *[new_m[n] for n in names], *[new_v[n] for n in names])
```

```python
import functools
import math

import jax
import jax.numpy as jnp
from jax import lax
from jax.experimental import pallas as pl
from jax.experimental.pallas import tpu as pltpu

F32 = jnp.float32
D = 1024
NIN = 4608
PW = 512
FF = 2816
MXU_DIM = 256
FF_CHUNKS = tuple((c0, min(4 * MXU_DIM, FF - c0)) for c0 in range(0, FF, 4 * MXU_DIM))
PLE = 256
HEADS, HD = 8, 128
GROUPS, GD = 4, 128
WINDOWS = (2, 4, 8, 16)
HALO = 16
CONV = 4
EPS = 1e-6
LRU_C = 8.0
NDEV = 8
MESH = pl.DeviceIdType.MESH

ADAM_LR, ADAM_B1, ADAM_B2, ADAM_EPS, ADAM_WD, ADAM_STEP = 0.001, 0.9, 0.999, 1e-08, 0.01, 10

_MXU = jnp.bfloat16
TM = 512
TM_SEQ = 256
VMEM_LIMIT = 56 * 1024 * 1024
W_FIRST = (("win", 576), ("f32s", 16))
W_SECOND = (("pproj", 128), ("lru", 128), ("wout", 128))
W_THIRD = (("wffn", 704), ("ple", 128), ("wffo", 352), ("wpg", 128))
W_OFF = {}
for _slabs in (W_FIRST, W_SECOND, W_THIRD):
    _o = 0
    for _n, _r in _slabs:
        W_OFF[_n] = (_o, _r)
        _o += _r
SMALL_ROWS = 48


def _cparams(n_axes=1, vmem=VMEM_LIMIT):
    return pltpu.CompilerParams(dimension_semantics=("arbitrary",) * n_axes, vmem_limit_bytes=vmem)


def _my_pos():
    return lax.axis_index("x"), lax.axis_index("y"), lax.axis_index("c")


def _nt(a, b):
    return lax.dot_general(a, b, (((1,), (1,)), ((), ())), preferred_element_type=F32)


def _nn(a, b):
    return lax.dot_general(a, b, (((1,), (0,)), ((), ())), preferred_element_type=F32)


def _tn(a, b):
    return lax.dot_general(a, b, (((0,), (0,)), ((), ())), preferred_element_type=F32)


def _sigmoid(x):
    return 0.5 * jnp.tanh(0.5 * x) + 0.5


_GELU_K = math.sqrt(2.0 / math.pi)


def _gelu_and_grad(x):
    x2 = x * x
    inner = _GELU_K * (x + 0.044715 * x2 * x)
    t = jnp.tanh(inner)
    g = 0.5 * x * (1.0 + t)
    dg = 0.5 * (1.0 + t) + 0.5 * x * (1.0 - t * t) * _GELU_K * (1.0 + 3.0 * 0.044715 * x2)
    return g, dg


def _softplus_neg(lam):
    x = -lam
    t = jnp.exp(-jnp.abs(x))
    u = 1.0 + t
    l1p = jnp.where(u == 1.0, t, jnp.log(u) * t / (u - 1.0))
    return jnp.maximum(x, 0.0) + l1p, _sigmoid(x)


def _start_slab_loads(g_ref, name, dst_ref, sems, base, width=D):
    off, rows = W_OFF[name]
    copies = []
    for k in range(NDEV):
        if width == D:
            src = g_ref.at[k, pl.ds(off, rows), :]
        else:
            src = g_ref.at[k, pl.ds(off, rows), pl.ds(0, width)]
        cp = pltpu.make_async_copy(src, dst_ref.at[pl.ds(k * rows, rows), :], sems.at[base + k])
        cp.start()
        copies.append(cp)
    return copies


def _load_weights(g_ref, items, sems):
    copies = []
    for n, (name, dst, width) in enumerate(items):
        copies += _start_slab_loads(g_ref, name, dst, sems, n * NDEV, width)
    for cp in copies:
        cp.wait()


class _Gather:
    def __init__(self, own_ref, out_ref, stage, send_sems, recv_sems, local_sem):
        x, y, c = _my_pos()
        self.c = c
        self.me, self.sibling = (x, y, c), (x, y, 1 - c)
        self.chips = [(1 - x, y), (x, 1 - y), (1 - x, 1 - y)]
        self.own_ref, self.out_ref, self.stage = own_ref, out_ref, stage
        self.send_sems, self.recv_sems = send_sems, recv_sems
        self.mine = pltpu.make_async_copy(stage, self.slab(*self.me), local_sem)
        self.first = [self.copy(0, self.me, self.sibling, src=stage)] + [
            self.copy(1 + j, self.me, (*chip, c), src=stage) for j, chip in enumerate(self.chips)]
        self.passed = [self.copy(4 + j, (*chip, c), self.sibling) for j, chip in enumerate(self.chips)]

    def slab(self, px, py, pc):
        return self.out_ref.at[4 * px + 2 * py + pc]

    def copy(self, k, block, to, src=None):
        return pltpu.make_async_remote_copy(
            src_ref=self.slab(*block) if src is None else src, dst_ref=self.slab(*block),
            send_sem=self.send_sems.at[k], recv_sem=self.recv_sems.at[k], device_id=to, device_id_type=MESH)

    def send_mine(self, far=True):
        pltpu.sync_copy(self.own_ref, self.stage)
        self.mine.start()
        for cp in self.first[:3]:
            cp.start()
        if far:
            self.send_far()

    def send_far(self):
        self.first[3].start()

    def pass_on(self, js):
        for j in js:
            self.copy(1 + j, (*self.chips[j], self.c), self.me).wait_recv()
            self.passed[j].start()

    def wait_sibling(self):
        self.copy(0, self.sibling, self.me).wait_recv()

    def wait_passed(self, js):
        for j in js:
            self.copy(4 + j, (*self.chips[j], 1 - self.c), self.me).wait_recv()

    def finish_sends(self):
        for cp in self.first + self.passed:
            cp.wait_send()
        self.mine.wait()


def _all_gather_small(piece):
    rows = piece.shape[0]

    def body(p_ref, out_ref, send_sems, recv_sems, local_sem):
        x, y, c = _my_pos()
        me = 4 * x + 2 * y + c
        mine = pltpu.make_async_copy(p_ref, out_ref.at[pl.ds(pl.multiple_of(me * rows, 8), rows), :], local_sem)
        mine.start()
        sends = []
        peers = []
        for r in range(1, NDEV):
            px = 1 - x if (r >> 2) & 1 else x
            py = 1 - y if (r >> 1) & 1 else y
            pc = 1 - c if r & 1 else c
            peers.append((px, py, pc))
            cp = pltpu.make_async_remote_copy(
                src_ref=p_ref, dst_ref=out_ref.at[pl.ds(pl.multiple_of(me * rows, 8), rows), :],
                send_sem=send_sems.at[r - 1], recv_sem=recv_sems.at[r - 1], device_id=(px, py, pc),
                device_id_type=MESH)
            cp.start()
            sends.append(cp)
        for r, (px, py, pc) in enumerate(peers):
            them = 4 * px + 2 * py + pc
            pltpu.make_async_remote_copy(
                src_ref=p_ref, dst_ref=out_ref.at[pl.ds(pl.multiple_of(them * rows, 8), rows), :],
                send_sem=send_sems.at[r], recv_sem=recv_sems.at[r], device_id=(px, py, pc),
                device_id_type=MESH).wait_recv()
        for cp in sends:
            cp.wait_send()
        mine.wait()

    return pl.pallas_call(
        body, name="ag_small",
        out_shape=jax.ShapeDtypeStruct((NDEV * rows, piece.shape[1]), piece.dtype),
        in_specs=[pl.BlockSpec(memory_space=pltpu.VMEM)],
        out_specs=pl.BlockSpec(memory_space=pl.ANY),
        scratch_shapes=[pltpu.SemaphoreType.DMA((7,)), pltpu.SemaphoreType.DMA((7,)), pltpu.SemaphoreType.DMA],
    )(piece)


def _row_block(rows, target=512, mult=8):
    b = min(rows, target) // mult * mult
    while rows % b:
        b -= mult
    return b


def _sum_arrays(arrs, name, narrow=None, target=704):
    rows, cols = arrs[0].shape
    br = _row_block(rows, target, 16)
    n = len(arrs)

    def body(*refs):
        acc = refs[0][...].astype(F32)
        for r in refs[1:n]:
            acc = acc + r[...].astype(F32)
        refs[n][...] = acc
        if narrow is not None:
            refs[n + 1][...] = acc.astype(narrow)

    spec = pl.BlockSpec((br, cols), lambda i: (i, 0))
    shape = jax.ShapeDtypeStruct((rows, cols), F32)
    if narrow is None:
        out_shape, out_specs = shape, spec
    else:
        out_shape, out_specs = (shape, jax.ShapeDtypeStruct((rows, cols), narrow)), (spec, spec)
    return pl.pallas_call(
        body, name=name, grid=(rows // br,), out_shape=out_shape,
        in_specs=[spec] * n, out_specs=out_specs, compiler_params=_cparams(1),
    )(*arrs)


def _adamw(w, g, m, v, name):
    rows, cols = w.shape
    br = _row_block(rows, 256)

    def body(w_ref, g_ref, m_ref, v_ref, d_ref, nm_ref, nv_ref):
        g_ = g_ref[...]
        m_ = ADAM_B1 * m_ref[...] + (1.0 - ADAM_B1) * g_
        v_ = ADAM_B2 * v_ref[...] + (1.0 - ADAM_B2) * (g_ * g_)
        m_hat = m_ / (1.0 - ADAM_B1 ** ADAM_STEP)
        v_hat = v_ / (1.0 - ADAM_B2 ** ADAM_STEP)
        d_ref[...] = -ADAM_LR * (m_hat / (jnp.sqrt(v_hat) + ADAM_EPS) + ADAM_WD * w_ref[...])
        nm_ref[...] = m_
        nv_ref[...] = v_

    spec = pl.BlockSpec((br, cols), lambda i: (i, 0))
    shape = jax.ShapeDtypeStruct((rows, cols), F32)
    return pl.pallas_call(
        body, name=name, grid=(rows // br,), out_shape=(shape, shape, shape),
        in_specs=[spec] * 4, out_specs=(spec, spec, spec), compiler_params=_cparams(1),
    )(w, g, m, v)


_CHIP_FLIPS = (2, 3, 1, 0)


def _grad_matmul(lhs, rhs, name):
    s, r = lhs.shape
    k = rhs.shape[1]
    tm = min(4 * TM, s)

    def body(l_ref, r_ref, o_ref):
        @pl.when(pl.program_id(0) == 0)
        def _():
            o_ref[...] = jnp.zeros_like(o_ref)

        o_ref[:, pl.ds(0, k)] += _tn(l_ref[...].astype(_MXU), r_ref[...].astype(_MXU))

    return pl.pallas_call(
        body, name=name, grid=(s // tm,),
        out_shape=jax.ShapeDtypeStruct((r, D), F32),
        in_specs=[pl.BlockSpec((tm, r), lambda i: (i, 0)), pl.BlockSpec((tm, k), lambda i: (i, 0))],
        out_specs=pl.BlockSpec((r, D), lambda i: (0, 0)),
        compiler_params=_cparams(1),
    )(lhs, rhs)


def _grad_matmul_rs(lhs, rhs, name, rows, extras=(), narrow=None, tail=0):
    s, r8 = lhs.shape
    k = rhs.shape[1]
    tm = min(TM, s)
    nt = s // tm
    cpb = 1
    nblk = 4 // cpb
    nx = len(extras)
    ers = [e.shape[0] // NDEV for e in extras]
    er = sum(ers)
    srows = rows + er
    brows = 2 * cpb * srows
    groups = [(0, srows - tail, F32 if narrow is None else narrow)] + ([(srows - tail, tail, F32)] if tail else [])
    ng = len(groups)
    mid = min(nt - 1, max(1, nt // 6))

    def flip_of(p):
        return jnp.where(p == 0, 2, jnp.where(p == 1, 3, jnp.where(p == 2, 1, 0)))

    def block_col(b):
        x, y, _ = _my_pos()
        return (2 * x + y) ^ flip_of(b)

    def body(*refs):
        l_ref, r_ref = refs[:2]
        x_refs = refs[2:2 + nx]
        rest = refs[2 + nx:]
        town_ref = rest[0]
        lici_refs = rest[1:1 + ng]
        acc, stage = rest[1 + ng:3 + ng]
        send_bufs = rest[3 + ng:3 + 2 * ng]
        dsend, drecv, isend, irecv, xsem = rest[3 + 2 * ng:]
        b = pl.program_id(0)
        i = pl.program_id(1)
        x, y, c = _my_pos()
        mine = 2 * x + y
        sibling = (x, y, 1 - c)

        def chip_at(p):
            return mine ^ _CHIP_FLIPS[p]

        def slab_rows(p, parity):
            within = 0 if cpb == 1 else (chip_at(p) & 1) * 2
            return pl.ds(pl.multiple_of((within + parity) * srows, 8), srows)

        def push(p, slot):
            return pltpu.make_async_remote_copy(
                src_ref=acc.at[slot, slab_rows(p, 1 - c), :], dst_ref=stage.at[p % 2],
                send_sem=dsend.at[p], recv_sem=drecv.at[p], device_id=sibling, device_id_type=MESH)

        def ici(p):
            ch = chip_at(p)
            return [pltpu.make_async_remote_copy(
                src_ref=send_bufs[g].at[p % 2], dst_ref=lici_refs[g].at[p], send_sem=isend.at[3 * g + p],
                recv_sem=irecv.at[3 * g + p], device_id=(ch >> 1, ch & 1, c), device_id_type=MESH) for g in range(ng)]

        def extra_loads(p, slot):
            copies = []
            within = 0 if cpb == 1 else (chip_at(p) & 1) * 2
            for parity in range(2):
                off = rows
                for n, (x_ref, e) in enumerate(zip(x_refs, ers)):
                    src = x_ref.at[pl.ds(pl.multiple_of((2 * chip_at(p) + parity) * e, 8), e), :]
                    dst = acc.at[slot, pl.ds(pl.multiple_of((within + parity) * srows + off, 8), e), :]
                    copies.append(pltpu.make_async_copy(src, dst, xsem.at[(p * 2 + parity) * nx + n]))
                    off += e
            return copies

        def combine(p, slot):
            push(p, slot).wait_recv()
            total = acc[slot, slab_rows(p, c), :] + stage[p % 2]
            if p == 3:
                stage[p % 2] = total
                pltpu.sync_copy(stage.at[p % 2], town_ref)
            else:
                if p == 2:
                    for cp in ici(0):
                        cp.wait_send()
                for g, (r0, n, dt) in enumerate(groups):
                    send_bufs[g][p % 2] = total[r0:r0 + n, :].astype(dt)
                for cp in ici(p):
                    cp.start()

        for bb in range(nblk):
            slot = bb % 2
            positions = list(range(bb * cpb, (bb + 1) * cpb))

            @pl.when(jnp.logical_and(b == bb, i == 0))
            def _(bb=bb, slot=slot, positions=positions):
                if bb >= 2:
                    for p in range((bb - 2) * cpb, (bb - 1) * cpb):
                        push(p, slot).wait_send()
                for q in range(2 * cpb):
                    acc[slot, pl.ds(q * srows, rows), :] = jnp.zeros((rows, D), F32)
                for p in positions:
                    for cp in extra_loads(p, slot):
                        cp.start()

            if bb >= 1:
                @pl.when(jnp.logical_and(b == bb, i == mid))
                def _(bb=bb):
                    for p in range((bb - 1) * cpb, bb * cpb):
                        combine(p, (bb - 1) % 2)

        res = _tn(l_ref[...].astype(_MXU), r_ref[...].astype(_MXU))
        slot_now = b % 2
        for q in range(2 * cpb):
            acc[slot_now, pl.ds(q * srows, rows), pl.ds(0, k)] += res[q * rows:(q + 1) * rows, :]

        for bb in range(nblk):
            slot = bb % 2
            positions = list(range(bb * cpb, (bb + 1) * cpb))

            @pl.when(jnp.logical_and(b == bb, i == nt - 1))
            def _(bb=bb, slot=slot, positions=positions):
                for p in positions:
                    for cp in extra_loads(p, slot):
                        cp.wait()
                for p in positions:
                    push(p, slot).start()
                if bb == nblk - 1:
                    for p in positions:
                        combine(p, slot)
                    for p in range(max(0, (nblk - 2) * cpb), 4):
                        push(p, slot).wait_send()
                    for p in range(1, 3):
                        for cp in ici(p):
                            cp.wait_send()
                    for p in range(3):
                        for cp in ici(p):
                            cp.wait_recv()

    in_specs = [pl.BlockSpec((tm, 2 * cpb * rows), lambda b, i: (i, block_col(b))),
                pl.BlockSpec((tm, k), lambda b, i: (i, 0))]
    any_spec = pl.BlockSpec(memory_space=pl.ANY)
    in_specs += [any_spec] * nx
    args = [lhs, rhs, *extras]
    outs = pl.pallas_call(
        body, name=name, grid=(nblk, nt),
        out_shape=(jax.ShapeDtypeStruct((srows, D), F32),)
        + tuple(jax.ShapeDtypeStruct((3, n, D), dt) for _, n, dt in groups),
        in_specs=in_specs, out_specs=(any_spec,) * (1 + ng),
        scratch_shapes=[pltpu.VMEM((2, brows, D), F32), pltpu.VMEM((2, srows, D), F32)]
        + [pltpu.VMEM((2, n, D), dt) for _, n, dt in groups]
        + [pltpu.SemaphoreType.DMA((4,)), pltpu.SemaphoreType.DMA((4,)), pltpu.SemaphoreType.DMA((3 * ng,)),
           pltpu.SemaphoreType.DMA((3 * ng,)), pltpu.SemaphoreType.DMA((max(1, 8 * nx),))],
        compiler_params=_cparams(2),
    )(*args)
    t_own = outs[0]
    return [(t_own[r0:r0 + n], landed) for (r0, n, _), landed in zip(groups, outs[1:])]


def _inproj_fwd(x, g1, own_first, own_second):
    s = x.shape[0]
    tm = min(2 * TM, s // 2)
    nt = s // tm
    assert nt % 2 == 0
    rows1, rows2 = own_first.shape[0], own_second.shape[0]
    wrows = W_OFF["win"][1]
    cw = 2 * wrows

    def chip_col(b):
        px, py, _ = _my_pos()
        return (2 * px + py) ^ jnp.where(b == 0, 0, jnp.where(b == 1, 2, jnp.where(b == 2, 1, 3)))

    def body(x_ref, g1_ref, own1_ref, own2_ref, u_ref, z_ref, gw1_ref, gw2_ref, w_vmem, u_buf, stage1, stage2, sems,
             usem, send1, recv1, local1, send2, recv2, local2):
        b = pl.program_id(0)
        i = pl.program_id(1)
        ga = _Gather(own1_ref, gw1_ref, stage1, send1, recv1, local1)
        gb = _Gather(own2_ref, gw2_ref, stage2, send2, recv2, local2)
        c = ga.c

        def load_chip(px, py, own_too):
            copies = []
            for pc in range(2):
                dst = w_vmem.at[pl.ds(pc * wrows, wrows), :]
                copies.append(pltpu.make_async_copy(gw1_ref.at[4 * px + 2 * py + pc, pl.ds(0, wrows), :], dst,
                                                    sems.at[pc]))
            if own_too:
                mine_dst = w_vmem.at[pl.ds(pl.multiple_of(c * wrows, 16), wrows), :]
                copies[0] = pltpu.make_async_copy(own1_ref.at[pl.ds(0, wrows), :], mine_dst, sems.at[0])
                theirs_dst = w_vmem.at[pl.ds(pl.multiple_of((1 - c) * wrows, 16), wrows), :]
                copies[1] = pltpu.make_async_copy(gw1_ref.at[4 * px + 2 * py + 1 - c, pl.ds(0, wrows), :], theirs_dst,
                                                  sems.at[1])
            for cp in copies:
                cp.start()
            for cp in copies:
                cp.wait()

        @pl.when(jnp.logical_and(b == 0, i == 0))
        def _():
            ga.send_mine(far=False)
            ga.wait_sibling()
            load_chip(ga.me[0], ga.me[1], True)

        @pl.when(jnp.logical_and(b == 0, i == nt // 4))
        def _():
            ga.send_far()
            gb.send_mine()

        @pl.when(jnp.logical_and(b == 0, i == (3 * nt) // 4))
        def _():
            ga.pass_on((0, 1))

        @pl.when(jnp.logical_and(b == 1, i == (3 * nt) // 4))
        def _():
            ga.pass_on((2,))

        for j in range(3):
            @pl.when(jnp.logical_and(b == j + 1, i == 0))
            def _(j=j):
                ga.wait_passed((j,))
                load_chip(ga.chips[j][0], ga.chips[j][1], False)

        @pl.when(jnp.logical_and(b == 2, i == nt // 2))
        def _():
            gb.pass_on((0, 1))

        @pl.when(jnp.logical_and(b == 3, i == (3 * nt) // 4))
        def _():
            gb.pass_on((2,))

        slot = i % 2

        def u_write(t, sl):
            return pltpu.make_async_copy(u_buf.at[sl], u_ref.at[pl.ds(pl.multiple_of(t * tm, tm), tm), :], usem.at[sl])

        def u_read(t, sl):
            return pltpu.make_async_copy(u_ref.at[pl.ds(pl.multiple_of(t * tm, tm), tm), :], u_buf.at[sl], usem.at[sl])

        @pl.when(b == 0)
        def _():
            @pl.when(i >= 2)
            def _():
                u_write(i - 2, slot).wait()

            xv = x_ref[...]
            inv = lax.rsqrt(jnp.mean(xv * xv, axis=-1, keepdims=True) + EPS)
            u_buf[slot] = (xv * inv * g1_ref[...]).astype(_MXU)
            u_write(i, slot).start()

            @pl.when(i == nt - 1)
            def _():
                u_write(i - 1, 1 - slot).wait()
                u_write(i, slot).wait()
                u_read(0, 0).start()

        @pl.when(b > 0)
        def _():
            u_read(i, slot).wait()

            @pl.when(jnp.logical_or(b < 3, i < nt - 1))
            def _():
                u_read((i + 1) % nt, 1 - slot).start()

        z_ref[...] = _nt(u_buf[slot], w_vmem[...])

        @pl.when(jnp.logical_and(b == 3, i == nt - 1))
        def _():
            ga.finish_sends()
            gb.wait_sibling()
            gb.wait_passed((0, 1, 2))
            gb.finish_sends()

    any_spec = pl.BlockSpec(memory_space=pl.ANY)
    dma7 = pltpu.SemaphoreType.DMA((7,))
    return pl.pallas_call(
        body, name="inproj_fwd", grid=(4, nt),
        out_shape=(jax.ShapeDtypeStruct((s, D), _MXU), jax.ShapeDtypeStruct((s, NIN), F32),
                   jax.ShapeDtypeStruct((NDEV, rows1, D), own_first.dtype),
                   jax.ShapeDtypeStruct((NDEV, rows2, D), own_second.dtype)),
        in_specs=[pl.BlockSpec((tm, D), lambda b, i: (jnp.where(b == 0, i, nt - 1), 0)),
                  pl.BlockSpec((1, D), lambda b, i: (0, 0)), any_spec, any_spec],
        out_specs=(any_spec, pl.BlockSpec((tm, cw), lambda b, i: (i, chip_col(b))), any_spec, any_spec),
        scratch_shapes=[pltpu.VMEM((cw, D), _MXU), pltpu.VMEM((2, tm, D), _MXU), pltpu.VMEM((rows1, D), own_first.dtype),
                        pltpu.VMEM((rows2, D), own_second.dtype), pltpu.SemaphoreType.DMA((2,)),
                        pltpu.SemaphoreType.DMA((2,)),
                        dma7, dma7, pltpu.SemaphoreType.DMA, dma7, dma7, pltpu.SemaphoreType.DMA],
        compiler_params=_cparams(2),
    )(x, g1, own_first, own_second)


def _pool_tile(pbuf, t0, tm, pw_ref, scale_ref):
    t = t0 + lax.broadcasted_iota(jnp.int32, (tm, GD), 0)
    pooled, mixed_pre = [], []
    for g, w in enumerate(WINDOWS):
        cs = pl.ds(g * GD, GD)
        cur = pbuf[pl.ds(HALO, tm), cs]
        acc = cur
        for d in range(1, w):
            acc = acc + pbuf[pl.ds(HALO - d, tm), cs]
        cnt = jnp.minimum(t + 1, w).astype(F32)
        pg = acc / cnt - cur
        pooled.append(pg)
        mixed_pre.append(_nn(pg.astype(_MXU), pw_ref[g]))
    return pooled, mixed_pre


def _lru_gates_head(hh, lbuf, start, tm, cw_ref, cb_ref, wrg_ref, brg_ref, wig_ref, big_ref, sp):
    cs = pl.ds(hh * HD, HD)
    xc = cb_ref[:, cs] + cw_ref[pl.ds(CONV - 1, 1), cs] * lbuf[pl.ds(HALO, tm), cs]
    for k in range(CONV - 1):
        xc = xc + cw_ref[pl.ds(k, 1), cs] * lbuf[pl.ds(HALO - (CONV - 1) + k, tm), cs]
    xcm = xc.astype(_MXU)
    r = _sigmoid(_nn(xcm, wrg_ref[hh]) + brg_ref[pl.ds(hh, 1), :])
    ig = _sigmoid(_nn(xcm, wig_ref[hh]) + big_ref[pl.ds(hh, 1), :])
    a = jnp.exp(-LRU_C * r * sp[:, hh * HD:(hh + 1) * HD])
    one_m = 1.0 - a * a
    live = jnp.logical_and(one_m > 0.0, jnp.logical_not(start))
    inv_mult = lax.rsqrt(jnp.where(live, one_m, 1.0))
    mult = jnp.where(live, one_m * inv_mult, jnp.where(start, 1.0, 0.0))
    return xc, r, ig, a, live, inv_mult, mult


def _seg_layout(tm):
    seg = tm // 8
    return seg, seg + 8


def _to_segments(dst_ref, hh, val, tm):
    seg, pitch = _seg_layout(tm)
    for s in range(8):
        dst_ref[hh, pl.ds(s * pitch, seg), :] = val[s * seg:(s + 1) * seg, :]


def _from_segments(src_ref, hh, tm):
    seg, pitch = _seg_layout(tm)
    return jnp.concatenate([src_ref[hh, pl.ds(s * pitch, seg), :] for s in range(8)], axis=0)


def _segment_scan(a_ref, b_ref, out_ref, hk, pk, carry_ref, tm, reverse):
    seg, pitch = _seg_layout(tm)
    row = lax.broadcasted_iota(jnp.int32, (8, HD), 0)
    order = range(seg - 1, -1, -1) if reverse else range(seg)
    for hh in range(HEADS):
        cs = pl.ds(hh * HD, HD)
        if reverse:
            a0 = a_ref[hh, pl.ds(0, 8, stride=pitch), :]
            a_wrap = jnp.where(row <= 6, pltpu.roll(a0, 7, 0), 1.0)
        hv = jnp.zeros((8, HD), F32)
        pv = jnp.ones((8, HD), F32)
        for k in order:
            if not reverse:
                av = a_ref[hh, pl.ds(k, 8, stride=pitch), :]
            elif k + 1 < seg:
                av = a_ref[hh, pl.ds(k + 1, 8, stride=pitch), :]
            else:
                av = a_wrap
            hv = av * hv + b_ref[hh, pl.ds(k, 8, stride=pitch), :]
            pv = av * pv
            hk[hh, pl.ds(8 * k, 8), :] = hv
            pk[hh, pl.ds(8 * k, 8), :] = pv
        for d in (1, 2, 4):
            if reverse:
                keep, sh = row < 8 - d, 8 - d
            else:
                keep, sh = row >= d, d
            hv = hv + pv * jnp.where(keep, pltpu.roll(hv, sh, 0), 0.0)
            pv = pv * jnp.where(keep, pltpu.roll(pv, sh, 0), 1.0)
        cin = carry_ref[:, cs]
        ends = hv + pv * cin
        if reverse:
            enter = jnp.where(row <= 6, pltpu.roll(ends, 7, 0), cin)
            carry_ref[:, cs] = jnp.broadcast_to((a0 * ends)[0:1, :], (8, HD))
        else:
            enter = jnp.where(row >= 1, pltpu.roll(ends, 1, 0), cin)
            carry_ref[:, cs] = jnp.broadcast_to(ends[7:8, :], (8, HD))
        for k in range(seg):
            out_ref[hh, pl.ds(k, 8, stride=pitch), :] = hk[hh, pl.ds(8 * k, 8), :] + pk[hh, pl.ds(8 * k, 8), :] * enter


def _mixer_fwd(z, x, gw, small, own_third):
    s = x.shape[0]
    tm = min(TM_SEQ, s)
    nt = s // tm
    rows3 = own_third.shape[0]
    (pool_w, pool_scale, conv_w, conv_b, w_rg, b_rg, w_ig, b_ig, lam, b_gate) = small

    def body(z_ref, x_ref, gw_ref, own3_ref, pw_ref, ps_ref, cw_ref, cb_ref, wrg_ref, brg_ref, wig_ref, big_ref, lam_ref,
             bg_ref, h_ref, yl_ref, mg_ref, yp_ref, yr_ref, h1_ref, a_ref, r_ref, ig_ref, xc_ref, gw3_ref,
             pprojT, lru_w, wout_w, pbuf, lbuf, a_s, b_s, h_s, hk, pk, hcar, sems, stage3, send3, recv3, local3):
        i = pl.program_id(0)
        t0 = i * tm
        gc = _Gather(own3_ref, gw3_ref, stage3, send3, recv3, local3)

        @pl.when(i == 0)
        def _():
            gc.send_mine()
            _load_weights(gw_ref, [("pproj", pprojT, PW), ("lru", lru_w, D), ("wout", wout_w, D)], sems)
            pbuf[pl.ds(0, HALO), :] = jnp.zeros((HALO, PW), F32)
            lbuf[pl.ds(0, HALO), :] = jnp.zeros((HALO, D), F32)
            hcar[...] = jnp.zeros_like(hcar)

        @pl.when(i == nt // 2)
        def _():
            gc.pass_on((0, 1))

        @pl.when(i == (3 * nt) // 4)
        def _():
            gc.pass_on((2,))

        pbuf[pl.ds(HALO, tm), :] = z_ref[:, pl.ds(0, PW)]
        _, mixed_pre = _pool_tile(pbuf, t0, tm, pw_ref, ps_ref)
        mixed = jnp.concatenate(mixed_pre, axis=1) * ps_ref[...]
        y_pool = _nt(mixed.astype(_MXU), pprojT[...])
        pbuf[pl.ds(0, HALO), :] = pbuf[pl.ds(tm, HALO), :]

        lbuf[pl.ds(HALO, tm), :] = z_ref[:, pl.ds(PW, D)]
        sp, _ = _softplus_neg(lam_ref[...])
        start = (t0 + lax.broadcasted_iota(jnp.int32, (tm, HD), 0)) == 0
        for hh in range(HEADS):
            xc, r, ig, a, _, _, mult = _lru_gates_head(hh, lbuf, start, tm, cw_ref, cb_ref, wrg_ref, brg_ref,
                                                       wig_ref, big_ref, sp)
            _to_segments(a_s, hh, a, tm)
            _to_segments(b_s, hh, mult * ig * xc, tm)
            cs = pl.ds(hh * HD, HD)
            a_ref[:, cs] = a
            r_ref[:, cs] = r.astype(_MXU)
            ig_ref[:, cs] = ig.astype(_MXU)
            xc_ref[:, cs] = xc.astype(_MXU)
        lbuf[pl.ds(0, HALO), :] = lbuf[pl.ds(tm, HALO), :]
        _segment_scan(a_s, b_s, h_s, hk, pk, hcar, tm, reverse=False)
        for hh in range(HEADS):
            h_ref[:, pl.ds(hh * HD, HD)] = _from_segments(h_s, hh, tm)
        gel, _ = _gelu_and_grad(z_ref[:, pl.ds(PW + D, D)])
        yl = (h_ref[...] * gel).astype(_MXU)
        yl_ref[...] = yl
        y_lru = _nn(yl, lru_w[...])

        g0 = _sigmoid(z_ref[:, pl.ds(PW + 2 * D, D)] + bg_ref[pl.ds(0, 1), :])
        g1 = _sigmoid(z_ref[:, pl.ds(PW + 3 * D, D)] + bg_ref[pl.ds(1, 1), :])
        merged = (g0 * y_pool + g1 * y_lru).astype(_MXU)
        mg_ref[...] = merged
        yp_ref[...] = y_pool.astype(_MXU)
        yr_ref[...] = y_lru.astype(_MXU)
        h1_ref[...] = x_ref[...] + _nn(merged, wout_w[...])

        @pl.when(i == nt - 1)
        def _():
            gc.wait_sibling()
            gc.wait_passed((0, 1, 2))
            gc.finish_sends()

    tok = lambda w, dt: jax.ShapeDtypeStruct((s, w), dt)
    tspec = lambda w: pl.BlockSpec((tm, w), lambda i: (i, 0))
    full = lambda a: pl.BlockSpec(a.shape, lambda i: (0,) * a.ndim)
    any_spec = pl.BlockSpec(memory_space=pl.ANY)
    seg_buf = pltpu.VMEM((HEADS, 8 * _seg_layout(tm)[1], HD), F32)
    dma7 = pltpu.SemaphoreType.DMA((7,))
    return pl.pallas_call(
        body, name="mixer_fwd", grid=(nt,),
        out_shape=(tok(D, F32), tok(D, _MXU), tok(D, _MXU), tok(D, _MXU), tok(D, _MXU), tok(D, F32),
                   tok(D, F32), tok(D, _MXU), tok(D, _MXU), tok(D, _MXU),
                   jax.ShapeDtypeStruct((NDEV, rows3, D), own_third.dtype)),
        in_specs=[tspec(NIN), tspec(D), any_spec, any_spec] + [full(a) for a in small],
        out_specs=(tspec(D),) * 10 + (any_spec,),
        scratch_shapes=[pltpu.VMEM((D, PW), _MXU), pltpu.VMEM((D, D), _MXU), pltpu.VMEM((D, D), _MXU),
                        pltpu.VMEM((tm + HALO, PW), F32), pltpu.VMEM((tm + HALO, D), F32),
                        seg_buf, seg_buf, seg_buf, pltpu.VMEM((HEADS, tm, HD), F32), pltpu.VMEM((HEADS, tm, HD), F32),
                        pltpu.VMEM((8, D), F32), pltpu.SemaphoreType.DMA((3 * NDEV,)),
                        pltpu.VMEM((rows3, D), own_third.dtype), dma7, dma7, pltpu.SemaphoreType.DMA],
        compiler_params=_cparams(1),
    )(z, x, gw, own_third, *small)


def _ffn_fwd(h1, g2, gw):
    s = h1.shape[0]
    tm = min(TM, s)

    def body(h1_ref, g2_ref, gw_ref, v_ref, gf_ref, uf_ref, h2_ref, wffnT, wffo, sems):
        @pl.when(pl.program_id(0) == 0)
        def _():
            _load_weights(gw_ref, [("wffn", wffnT, D), ("wffo", wffo, D)], sems)

        hv = h1_ref[...]
        inv = lax.rsqrt(jnp.mean(hv * hv, axis=-1, keepdims=True) + EPS)
        v = (hv * inv * g2_ref[...]).astype(_MXU)
        v_ref[...] = v
        acc = hv
        for c0, cn in FF_CHUNKS:
            cs = pl.ds(c0, cn)
            gf = _nt(v, wffnT[cs, :]).astype(_MXU)
            uf = _nt(v, wffnT[pl.ds(FF + c0, cn), :]).astype(_MXU)
            gf_ref[:, cs] = gf
            uf_ref[:, cs] = uf
            gf32 = gf.astype(F32)
            act = (gf32 * _sigmoid(gf32) * uf.astype(F32)).astype(_MXU)
            acc = acc + _nn(act, wffo[cs, :])
        h2_ref[...] = acc

    tspec = lambda w: pl.BlockSpec((tm, w), lambda i: (i, 0))
    return pl.pallas_call(
        body, name="ffn_fwd", grid=(s // tm,),
        out_shape=(jax.ShapeDtypeStruct((s, D), _MXU), jax.ShapeDtypeStruct((s, FF), _MXU),
                   jax.ShapeDtypeStruct((s, FF), _MXU), jax.ShapeDtypeStruct((s, D), F32)),
        in_specs=[tspec(D), pl.BlockSpec((1, D), lambda i: (0, 0)), pl.BlockSpec(memory_space=pl.ANY)],
        out_specs=(tspec(D), tspec(FF), tspec(FF), tspec(D)),
        scratch_shapes=[pltpu.VMEM((2 * FF, D), _MXU), pltpu.VMEM((FF, D), _MXU), pltpu.SemaphoreType.DMA((2 * NDEV,))],
        compiler_params=_cparams(1),
    )(h1, g2, gw)


def _rms_bwd(dy, xn, inv, g):
    dg = jnp.sum(dy * xn, axis=0, keepdims=True)
    dxn = dy * g
    dx = inv * (dxn - xn * jnp.mean(dxn * xn, axis=-1, keepdims=True))
    return dx, dg


def _ple_loss_fwd_bwd(h2, p, target, g3, gfin, gw):
    s = h2.shape[0]
    tm = min(TM, s)

    def body(h2_ref, p_ref, t_ref, g3_ref, gf_ref, gw_ref,
             dh2_ref, loss_ref, dg3_ref, dgf_ref, gwpg_ref, gple_ref, wpg, pleT, sems):
        i = pl.program_id(0)

        @pl.when(i == 0)
        def _():
            _load_weights(gw_ref, [("wpg", wpg, D), ("ple", pleT, PLE)], sems)
            for ref in (loss_ref, dg3_ref, dgf_ref, gwpg_ref, gple_ref):
                ref[...] = jnp.zeros_like(ref)

        hv = h2_ref[...]
        inv3 = lax.rsqrt(jnp.mean(hv * hv, axis=-1, keepdims=True) + EPS)
        xn3 = hv * inv3
        n3 = (xn3 * g3_ref[...]).astype(_MXU)
        pg = _sigmoid(_nn(n3, wpg[...]))
        pm = p_ref[...].astype(_MXU)
        e = _nt(pm, pleT[...])
        h3 = hv + pg * e
        invf = lax.rsqrt(jnp.mean(h3 * h3, axis=-1, keepdims=True) + EPS)
        xf = h3 * invf
        diff = xf * gf_ref[...] - t_ref[...]
        loss_ref[...] += jnp.sum(diff * diff) * (0.5 / D)
        dh3, dgf = _rms_bwd(diff * (1.0 / D), xf, invf, gf_ref[...])
        dgf_ref[...] += dgf
        gple_ref[:, pl.ds(0, PLE)] += _tn((dh3 * pg).astype(_MXU), pm)
        dpg = (dh3 * e * pg * (1.0 - pg)).astype(_MXU)
        gwpg_ref[...] += _tn(n3, dpg)
        dn3 = _nt(dpg, wpg[...])
        dx3, dg3 = _rms_bwd(dn3, xn3, inv3, g3_ref[...])
        dg3_ref[...] += dg3
        dh2_ref[...] = dh3 + dx3

    tspec = lambda w: pl.BlockSpec((tm, w), lambda i: (i, 0))
    vec = pl.BlockSpec((1, D), lambda i: (0, 0))
    mat = pl.BlockSpec((D, D), lambda i: (0, 0))
    return pl.pallas_call(
        body, name="ple_loss", grid=(s // tm,),
        out_shape=(jax.ShapeDtypeStruct((s, D), F32), jax.ShapeDtypeStruct((8, 128), F32),
                   jax.ShapeDtypeStruct((1, D), F32), jax.ShapeDtypeStruct((1, D), F32),
                   jax.ShapeDtypeStruct((D, D), F32), jax.ShapeDtypeStruct((D, D), F32)),
        in_specs=[tspec(D), tspec(PLE), tspec(D), vec, vec, pl.BlockSpec(memory_space=pl.ANY)],
        out_specs=(tspec(D), pl.BlockSpec((8, 128), lambda i: (0, 0)), vec, vec, mat, mat),
        scratch_shapes=[pltpu.VMEM((D, D), _MXU), pltpu.VMEM((D, PLE), _MXU), pltpu.SemaphoreType.DMA((2 * NDEV,))],
        compiler_params=_cparams(1),
    )(h2, p, target, g3, gfin, gw)


def _ffn_bwd_hidden(dh2, gf, uf, gw):
    s = dh2.shape[0]
    tm = min(TM, s)
    nt = s // tm

    def body(dh2_ref, gf_ref, uf_ref, gw_ref, dff_ref, gwo_ref, wffo, gacc, sems):
        i = pl.program_id(0)

        @pl.when(i == 0)
        def _():
            _load_weights(gw_ref, [("wffo", wffo, D)], sems)
            gacc[...] = jnp.zeros_like(gacc)

        dm = dh2_ref[...].astype(_MXU)
        for c0, cn in FF_CHUNKS:
            cs = pl.ds(c0, cn)
            dact = _nt(dm, wffo[cs, :])
            gfv = gf_ref[:, cs].astype(F32)
            ufv = uf_ref[:, cs].astype(F32)
            sg = _sigmoid(gfv)
            silu = gfv * sg
            gacc[cs, :] += _tn((silu * ufv).astype(_MXU), dm)
            dff_ref[:, cs] = (dact * ufv * (sg * (1.0 + gfv * (1.0 - sg)))).astype(_MXU)
            dff_ref[:, pl.ds(FF + c0, cn)] = (dact * silu).astype(_MXU)

        @pl.when(i == nt - 1)
        def _():
            pltpu.sync_copy(gacc, gwo_ref)

    tspec = lambda w: pl.BlockSpec((tm, w), lambda i: (i, 0))
    return pl.pallas_call(
        body, name="ffn_bwd_hidden", grid=(nt,),
        out_shape=(jax.ShapeDtypeStruct((s, 2 * FF), _MXU), jax.ShapeDtypeStruct((FF, D), F32)),
        in_specs=[tspec(D), tspec(FF), tspec(FF), pl.BlockSpec(memory_space=pl.ANY)],
        out_specs=(tspec(2 * FF), pl.BlockSpec(memory_space=pl.ANY)),
        scratch_shapes=[pltpu.VMEM((FF, D), _MXU), pltpu.VMEM((FF, D), F32), pltpu.SemaphoreType.DMA((NDEV,))],
        compiler_params=_cparams(1),
    )(dh2, gf, uf, gw)


def _proj_norm_bwd(dy, x, dres, g, gw, slab, width, name, lhs=None):
    s = x.shape[0]
    tm = min(TM, s)
    nl = 0 if lhs is None else 1

    def body(*refs):
        dy_ref, x_ref, dr_ref, g_ref = refs[:4]
        l_refs = refs[4:4 + nl]
        gw_ref, dx_ref, dg_ref = refs[4 + nl:7 + nl]
        gl_refs = refs[7 + nl:7 + 2 * nl]
        wT, sems = refs[7 + 2 * nl:]

        @pl.when(pl.program_id(0) == 0)
        def _():
            _load_weights(gw_ref, [(slab, wT, D)], sems)
            dg_ref[...] = jnp.zeros_like(dg_ref)
            for ref in gl_refs:
                ref[...] = jnp.zeros_like(ref)

        dv = _nn(dy_ref[...], wT[...])
        xv = x_ref[...]
        inv = lax.rsqrt(jnp.mean(xv * xv, axis=-1, keepdims=True) + EPS)
        dx, dg = _rms_bwd(dv, xv * inv, inv, g_ref[...])
        dg_ref[...] += dg
        dr = dr_ref[...]
        dx_ref[...] = dr + dx
        for l_ref, gl_ref in zip(l_refs, gl_refs):
            gl_ref[...] += _tn(l_ref[...], dr.astype(_MXU))

    tspec = lambda w: pl.BlockSpec((tm, w), lambda i: (i, 0))
    vec = pl.BlockSpec((1, D), lambda i: (0, 0))
    mat = pl.BlockSpec((D, D), lambda i: (0, 0))
    return pl.pallas_call(
        body, name=name, grid=(s // tm,),
        out_shape=(jax.ShapeDtypeStruct((s, D), F32), jax.ShapeDtypeStruct((1, D), F32))
        + (jax.ShapeDtypeStruct((D, D), F32),) * nl,
        in_specs=[tspec(width), tspec(D), tspec(D), vec] + [tspec(D)] * nl + [pl.BlockSpec(memory_space=pl.ANY)],
        out_specs=(tspec(D), vec) + (mat,) * nl,
        scratch_shapes=[pltpu.VMEM((width, D), _MXU), pltpu.SemaphoreType.DMA((NDEV,))],
        compiler_params=_cparams(1),
    )(dy, x, dres, g, *([] if lhs is None else [lhs]), gw)


def _mixer_bwd(dh1, z, h, y_pool, y_lru, saved, gw, small):
    s = dh1.shape[0]
    tm = min(TM_SEQ, s)
    nt = s // tm
    (pool_w, pool_scale, conv_w, conv_b, w_rg, b_rg, w_ig, b_ig, lam, b_gate) = small

    def body(dh1_ref, z_ref, zp_ref, h_ref, hp_ref, yp_ref, yr_ref, a_ref, r_ref, ig_ref, xc_ref, gw_ref,
             pw_ref, ps_ref, cw_ref, cb_ref, wrg_ref, brg_ref, wig_ref, big_ref, lam_ref, bg_ref,
             dz_ref, dyr_ref, dyp_ref, mx_ref,
             gbg_ref, glam_ref, gbrg_ref, gbig_ref, gcb_ref, gcw_ref, gps_ref, gpw_ref, gwrg_ref, gwig_ref,
             pprojT, lru_w, wout_w, pbuf, lbuf, hbuf, qbuf, xbuf, a_s, g_s, dh_s, hk, pk, dcar, sems):
        step = pl.program_id(0)
        i = nt - 1 - step
        t0 = i * tm

        @pl.when(step == 0)
        def _():
            _load_weights(gw_ref, [("pproj", pprojT, PW), ("lru", lru_w, D), ("wout", wout_w, D)], sems)
            for ref in (gbg_ref, glam_ref, gbrg_ref, gbig_ref, gcb_ref, gcw_ref, gps_ref, gpw_ref, gwrg_ref, gwig_ref):
                ref[...] = jnp.zeros_like(ref)
            qbuf[pl.ds(tm, HALO), :] = jnp.zeros((HALO, PW), F32)
            xbuf[pl.ds(tm, 8), :] = jnp.zeros((8, D), F32)
            dcar[...] = jnp.zeros_like(dcar)

        first = i == 0
        zprev = jnp.where(first, 0.0, zp_ref[...])
        hprev = jnp.where(first, 0.0, hp_ref[...])

        d_merged = _nt(dh1_ref[...].astype(_MXU), wout_w[...])

        g0 = _sigmoid(z_ref[:, pl.ds(PW + 2 * D, D)] + bg_ref[pl.ds(0, 1), :])
        g1 = _sigmoid(z_ref[:, pl.ds(PW + 3 * D, D)] + bg_ref[pl.ds(1, 1), :])
        dz0 = d_merged * yp_ref[...].astype(F32) * g0 * (1.0 - g0)
        dz1 = d_merged * yr_ref[...].astype(F32) * g1 * (1.0 - g1)
        dz_ref[:, pl.ds(PW + 2 * D, D)] = dz0.astype(_MXU)
        dz_ref[:, pl.ds(PW + 3 * D, D)] = dz1.astype(_MXU)
        gbg_ref[pl.ds(0, 1), :] += jnp.sum(dz0, axis=0, keepdims=True)
        gbg_ref[pl.ds(1, 1), :] += jnp.sum(dz1, axis=0, keepdims=True)
        d_ypool = (d_merged * g0).astype(_MXU)
        d_ylru = (d_merged * g1).astype(_MXU)
        dyp_ref[...] = d_ypool
        dyr_ref[...] = d_ylru

        d_yl = _nt(d_ylru, lru_w[...])
        gel, dgel = _gelu_and_grad(z_ref[:, pl.ds(PW + D, D)])
        dz_ref[:, pl.ds(PW + D, D)] = (d_yl * h_ref[...] * dgel).astype(_MXU)
        g_full = d_yl * gel
        lbuf[pl.ds(0, HALO), :] = zprev[:, PW:PW + D]
        lbuf[pl.ds(HALO, tm), :] = z_ref[:, pl.ds(PW, D)]
        hbuf[pl.ds(0, 8), :] = hprev
        hbuf[pl.ds(8, tm), :] = h_ref[...]
        sp, sneg = _softplus_neg(lam_ref[...])
        start = (t0 + lax.broadcasted_iota(jnp.int32, (tm, HD), 0)) == 0
        for hh in range(HEADS):
            cs = pl.ds(hh * HD, HD)
            _to_segments(a_s, hh, a_ref[:, cs], tm)
            _to_segments(g_s, hh, g_full[:, hh * HD:(hh + 1) * HD], tm)
        _segment_scan(a_s, g_s, dh_s, hk, pk, dcar, tm, reverse=True)
        for hh in range(HEADS):
            cs = pl.ds(hh * HD, HD)
            a = a_ref[:, cs]
            r = r_ref[:, cs].astype(F32)
            ig = ig_ref[:, cs].astype(F32)
            xc = xc_ref[:, cs].astype(F32)
            a2 = a * a
            one_m = 1.0 - a2
            live = jnp.logical_and(one_m > 0.0, jnp.logical_not(start))
            inv_mult = lax.rsqrt(jnp.where(live, one_m, 1.0))
            mult = jnp.where(live, one_m * inv_mult, jnp.where(start, 1.0, 0.0))
            dh = _from_segments(dh_s, hh, tm)
            d_mult = dh * ig * xc
            d_loga = dh * hbuf[pl.ds(7, tm), cs] * a - jnp.where(live, d_mult * a2 * inv_mult, 0.0)
            glam_ref[:, cs] += jnp.sum(d_loga * (LRU_C * r) * sneg[:, hh * HD:(hh + 1) * HD], axis=0, keepdims=True)
            d_rpre = d_loga * (-LRU_C * sp[:, hh * HD:(hh + 1) * HD]) * r * (1.0 - r)
            d_igpre = dh * mult * xc * ig * (1.0 - ig)
            gbrg_ref[pl.ds(hh, 1), :] += jnp.sum(d_rpre, axis=0, keepdims=True)
            gbig_ref[pl.ds(hh, 1), :] += jnp.sum(d_igpre, axis=0, keepdims=True)
            drm = d_rpre.astype(_MXU)
            dim = d_igpre.astype(_MXU)
            xcm = xc.astype(_MXU)
            gwrg_ref[hh] += _tn(xcm, drm)
            gwig_ref[hh] += _tn(xcm, dim)
            d_xc = dh * mult * ig + _nt(drm, wrg_ref[hh]) + _nt(dim, wig_ref[hh])
            gcb_ref[:, cs] += jnp.sum(d_xc, axis=0, keepdims=True)
            for k in range(CONV):
                gcw_ref[pl.ds(k, 1), cs] += jnp.sum(d_xc * lbuf[pl.ds(HALO - (CONV - 1) + k, tm), cs], axis=0,
                                                    keepdims=True)
            xbuf[pl.ds(0, tm), cs] = d_xc
        dzl = cw_ref[pl.ds(CONV - 1, 1), :] * xbuf[pl.ds(0, tm), :]
        for k in range(CONV - 1):
            dzl = dzl + cw_ref[pl.ds(k, 1), :] * xbuf[pl.ds(CONV - 1 - k, tm), :]
        dz_ref[:, pl.ds(PW, D)] = dzl.astype(_MXU)
        xbuf[pl.ds(tm, 8), :] = xbuf[pl.ds(0, 8), :]

        d_mixed = _nn(d_ypool, pprojT[...])
        pbuf[pl.ds(0, HALO), :] = zprev[:, 0:PW]
        pbuf[pl.ds(HALO, tm), :] = z_ref[:, pl.ds(0, PW)]
        pooled, mixed_pre = _pool_tile(pbuf, t0, tm, pw_ref, ps_ref)
        mp = jnp.concatenate(mixed_pre, axis=1)
        mx_ref[...] = (mp * ps_ref[...]).astype(_MXU)
        gps_ref[...] += jnp.sum(d_mixed * mp, axis=0, keepdims=True)
        d_mp = (d_mixed * ps_ref[...]).astype(_MXU)
        t = t0 + lax.broadcasted_iota(jnp.int32, (tm, GD), 0)
        d_pooled = []
        for g, w in enumerate(WINDOWS):
            dmg = d_mp[:, g * GD:(g + 1) * GD]
            gpw_ref[g] += _tn(pooled[g].astype(_MXU), dmg)
            dp = _nt(dmg, pw_ref[g])
            d_pooled.append(dp)
            qbuf[pl.ds(0, tm), pl.ds(g * GD, GD)] = dp / jnp.minimum(t + 1, w).astype(F32)
        for g, w in enumerate(WINDOWS):
            cs = pl.ds(g * GD, GD)
            acc = qbuf[pl.ds(0, tm), cs]
            for d in range(1, w):
                acc = acc + qbuf[pl.ds(d, tm), cs]
            dz_ref[:, cs] = (acc - d_pooled[g]).astype(_MXU)
        qbuf[pl.ds(tm, HALO), :] = qbuf[pl.ds(0, HALO), :]

    rev = lambda w: pl.BlockSpec((tm, w), lambda g: (nt - 1 - g, 0))
    prev = lambda rows, w: pl.BlockSpec((rows, w), lambda g: (jnp.maximum((nt - 1 - g) * (tm // rows) - 1, 0), 0))
    full = lambda a: pl.BlockSpec(a.shape, lambda g: (0,) * a.ndim)
    tok = lambda w, dt: jax.ShapeDtypeStruct((s, w), dt)
    acc_shapes = [(2, D), (1, D), (HEADS, HD), (HEADS, HD), (1, D), (CONV, D), (1, PW), (GROUPS, GD, GD),
                  (HEADS, HD, HD), (HEADS, HD, HD)]
    acc_specs = tuple(pl.BlockSpec(sh, lambda g, n=len(sh): (0,) * n) for sh in acc_shapes)
    seg_buf = pltpu.VMEM((HEADS, 8 * _seg_layout(tm)[1], HD), F32)
    a_in, r_in, ig_in, xc_in = saved
    return pl.pallas_call(
        body, name="mixer_bwd", grid=(nt,),
        out_shape=(tok(NIN, _MXU), tok(D, _MXU), tok(D, _MXU), tok(PW, _MXU))
        + tuple(jax.ShapeDtypeStruct(sh, F32) for sh in acc_shapes),
        in_specs=[rev(D), rev(NIN), prev(HALO, NIN), rev(D), prev(8, D), rev(D), rev(D), rev(D), rev(D), rev(D), rev(D),
                  pl.BlockSpec(memory_space=pl.ANY)] + [full(a) for a in small],
        out_specs=(rev(NIN), rev(D), rev(D), rev(PW)) + acc_specs,
        scratch_shapes=[pltpu.VMEM((D, PW), _MXU), pltpu.VMEM((D, D), _MXU), pltpu.VMEM((D, D), _MXU),
                        pltpu.VMEM((tm + HALO, PW), F32), pltpu.VMEM((tm + HALO, D), F32),
                        pltpu.VMEM((tm + 8, D), F32), pltpu.VMEM((tm + HALO, PW), F32), pltpu.VMEM((tm + 8, D), F32),
                        seg_buf, seg_buf, seg_buf, pltpu.VMEM((HEADS, tm, HD), F32), pltpu.VMEM((HEADS, tm, HD), F32),
                        pltpu.VMEM((8, D), F32), pltpu.SemaphoreType.DMA((3 * NDEV,))],
        compiler_params=_cparams(1),
    )(dh1, z, z, h, h, y_pool, y_lru, a_in, r_in, ig_in, xc_in, gw, *small)


def _split3(a):
    hi = a.astype(jnp.bfloat16).astype(F32)
    mid = (a - hi).astype(jnp.bfloat16).astype(F32)
    lo = (a - hi - mid).astype(jnp.bfloat16).astype(F32)
    return jnp.stack([hi, mid, lo])


def _small_pack(parts):
    flat = jnp.concatenate([a.reshape(-1) for a in parts])
    return jnp.pad(flat, (0, NDEV * SMALL_ROWS * D - flat.shape[0])).reshape(NDEV * SMALL_ROWS, D)


def _small_unpack(packed, shapes):
    flat = packed.reshape(-1)
    out, o = [], 0
    for sh in shapes:
        n = math.prod(sh)
        out.append(flat[o:o + n].reshape(sh))
        o += n
    return out


def kernel(x, p, norm1_g, w_in, b_gate, pool_w, pool_scale, pool_proj, conv_w, conv_b, w_rg, b_rg, w_ig, b_ig, lru_lambda, lru_proj, w_out, norm2_g, w_ffn_in, w_ffn_out, ple_norm_g, w_ple_gate, w_ple_proj, final_g, loss_target, m_norm1_g, m_w_in, m_b_gate, m_pool_w, m_pool_scale, m_pool_proj, m_conv_w, m_conv_b, m_w_rg, m_b_rg, m_w_ig, m_b_ig, m_lru_lambda, m_lru_proj, m_w_out, m_norm2_g, m_w_ffn_in, m_w_ffn_out, m_ple_norm_g, m_w_ple_gate, m_w_ple_proj, m_final_g, v_norm1_g, v_w_in, v_b_gate, v_pool_w, v_pool_scale, v_pool_proj, v_conv_w, v_conv_b, v_w_rg, v_b_rg, v_w_ig, v_b_ig, v_lru_lambda, v_lru_proj, v_w_out, v_norm2_g, v_w_ffn_in, v_w_ffn_out, v_ple_norm_g, v_w_ple_gate, v_w_ple_proj, v_final_g):
    axes = ("x", "y", "c")
    me = 4 * lax.axis_index("x") + 2 * lax.axis_index("y") + lax.axis_index("c")
    x2 = x[0]
    p2 = p[0, 0]
    tgt = loss_target[0]

    n_small = (CONV + 2) * 128
    small_terms = _split3(jnp.concatenate([conv_w[0].reshape(-1), b_gate[0].reshape(-1)]))
    small_rows = jnp.pad(small_terms, ((0, 16 - 3), (0, D - n_small)))
    own_first = jnp.concatenate([w_in[0].T.astype(_MXU), small_rows.astype(_MXU)], axis=0)
    own_second = jnp.concatenate([
        jnp.pad(pool_proj[0].T, ((0, 0), (0, D - PW))).astype(_MXU), lru_proj[0].astype(_MXU), w_out[0].astype(_MXU),
    ], axis=0)
    own_third = jnp.concatenate([
        w_ffn_in[0].T.astype(_MXU), jnp.pad(w_ple_proj[0].T, ((0, 0), (0, D - PLE))).astype(_MXU),
        w_ffn_out[0].astype(_MXU), w_ple_gate[0].astype(_MXU),
    ], axis=0)
    u, z, gw_first, gw = _inproj_fwd(x2, norm1_g, own_first, own_second)
    off = W_OFF["f32s"][0]
    st = gw_first[:, off:off + 3, :n_small].astype(F32)
    sf = st[:, 0] + st[:, 1] + st[:, 2]
    conv_w_full = sf[:, :CONV * 128].reshape(NDEV, CONV, 128).transpose(1, 0, 2).reshape(CONV, D)
    b_gate_full = sf[:, CONV * 128:].reshape(NDEV, 2, 128).transpose(1, 0, 2).reshape(2, D)

    small = (pool_w[0].astype(_MXU), pool_scale, conv_w_full, conv_b, w_rg[0].astype(_MXU), b_rg[0],
             w_ig[0].astype(_MXU), b_ig[0], lru_lambda, b_gate_full)

    h, yl, merged, y_pool, y_lru, h1, *saved, gw_third = _mixer_fwd(z, x2, gw, small, own_third)
    v, gf, uf, h2 = _ffn_fwd(h1, norm2_g, gw_third)

    dh2, loss_blk, g_ple_norm, g_final, part_wpg, part_ple = _ple_loss_fwd_bwd(h2, p2, tgt, ple_norm_g,
                                                                               final_g.reshape(1, D), gw_third)
    dff, part_wffo = _ffn_bwd_hidden(dh2, gf, uf, gw_third)
    dh1, g_norm2 = _proj_norm_bwd(dff, h1, dh2, norm2_g, gw_third, "wffn", 2 * FF, "ffn_bwd_in")
    (dz, d_ylru, d_ypool, mixed, g_bgate, g_lam, g_brg, g_big, g_convb, g_convw, g_pscale, g_poolw, g_wrg,
     g_wig) = _mixer_bwd(dh1, z, h, y_pool, y_lru, saved, gw, small)
    grad_x, g_norm1, part_wout = _proj_norm_bwd(dz, x2, dh1, norm1_g, gw_first, "win", NIN, "inproj_bwd", lhs=merged)

    small_shapes = [(1, D), (GROUPS, GD, GD), (1, PW), (1, D), (HEADS, HD, HD), (HEADS, HD), (HEADS, HD, HD),
                    (HEADS, HD), (1, D), (1, D), (1, D), (1, D), (2, D), (CONV, D), (1, 1)]
    small_part = _small_pack([g_norm1, g_poolw, g_pscale, g_convb, g_wrg, g_brg, g_wig, g_big, g_lam, g_norm2,
                              g_ple_norm, g_final, g_bgate, g_convw, loss_blk[0:1, 0:1]])
    riders = [_grad_matmul(yl, d_ylru, "grad_lru_proj"), part_wout, _grad_matmul(d_ypool, mixed, "grad_pool_proj")]
    rs_wffn = _grad_matmul_rs(dff, v, "grad_w_ffn_in", 704, extras=[part_wffo, part_wpg, part_ple], narrow=_MXU)
    rs_win = _grad_matmul_rs(dz, u, "grad_w_in", 576, extras=riders + [small_part], narrow=_MXU, tail=SMALL_ROWS)

    def reduced(parts, name):
        return [_sum_arrays([t_own, landed[0], landed[1], landed[2]], "rs_sum_" + name + str(n))
                for n, (t_own, landed) in enumerate(parts)]

    red_wffn, = reduced(rs_wffn, "wffn")
    red_win, red_small = reduced(rs_win, "win")
    g_w_in = red_win[:576].T
    g_w_ffn_in = red_wffn[:704].T
    g_w_ffn_out = red_wffn[704:1056]
    g_w_ple_gate = red_wffn[1056:1184]
    g_w_ple_proj = red_wffn[1184:1312, :PLE].T
    g_lru_proj, g_w_out = red_win[576:704], red_win[704:832]
    g_pool_proj = red_win[832:960, :PW].T
    small_red = _all_gather_small(red_small)
    (gs_norm1, gs_poolw, gs_pscale, gs_convb, gs_wrg, gs_brg, gs_wig, gs_big, gs_lam, gs_norm2, gs_ple_norm,
     gs_final, gs_bgate, gs_convw, loss_sum) = _small_unpack(small_red, small_shapes)
    loss = loss_sum[0, 0]
    g_b_gate = lax.dynamic_slice_in_dim(gs_bgate, me * 128, 128, axis=1)
    g_conv_w = lax.dynamic_slice_in_dim(gs_convw, me * 128, 128, axis=1)

    grads = {
        "norm1_g": gs_norm1, "w_in": g_w_in[None], "b_gate": g_b_gate[None], "pool_w": gs_poolw[None],
        "pool_scale": gs_pscale, "pool_proj": g_pool_proj[None], "conv_w": g_conv_w[None], "conv_b": gs_convb,
        "w_rg": gs_wrg[None], "b_rg": gs_brg[None], "w_ig": gs_wig[None], "b_ig": gs_big[None], "lru_lambda": gs_lam,
        "lru_proj": g_lru_proj[None], "w_out": g_w_out[None], "norm2_g": gs_norm2, "w_ffn_in": g_w_ffn_in[None],
        "w_ffn_out": g_w_ffn_out[None], "ple_norm_g": gs_ple_norm, "w_ple_gate": g_w_ple_gate[None],
        "w_ple_proj": g_w_ple_proj[None], "final_g": gs_final.reshape(D),
    }
    weights = dict(norm1_g=norm1_g, w_in=w_in, b_gate=b_gate, pool_w=pool_w, pool_scale=pool_scale, pool_proj=pool_proj,
                   conv_w=conv_w, conv_b=conv_b, w_rg=w_rg, b_rg=b_rg, w_ig=w_ig, b_ig=b_ig, lru_lambda=lru_lambda,
                   lru_proj=lru_proj, w_out=w_out, norm2_g=norm2_g, w_ffn_in=w_ffn_in, w_ffn_out=w_ffn_out,
                   ple_norm_g=ple_norm_g, w_ple_gate=w_ple_gate, w_ple_proj=w_ple_proj, final_g=final_g)
    moments_m = dict(norm1_g=m_norm1_g, w_in=m_w_in, b_gate=m_b_gate, pool_w=m_pool_w, pool_scale=m_pool_scale,
                     pool_proj=m_pool_proj, conv_w=m_conv_w, conv_b=m_conv_b, w_rg=m_w_rg, b_rg=m_b_rg, w_ig=m_w_ig,
                     b_ig=m_b_ig, lru_lambda=m_lru_lambda, lru_proj=m_lru_proj, w_out=m_w_out, norm2_g=m_norm2_g,
                     w_ffn_in=m_w_ffn_in, w_ffn_out=m_w_ffn_out, ple_norm_g=m_ple_norm_g, w_ple_gate=m_w_ple_gate,
                     w_ple_proj=m_w_ple_proj, final_g=m_final_g)
    moments_v = dict(norm1_g=v_norm1_g, w_in=v_w_in, b_gate=v_b_gate, pool_w=v_pool_w, pool_scale=v_pool_scale,
                     pool_proj=v_pool_proj, conv_w=v_conv_w, conv_b=v_conv_b, w_rg=v_w_rg, b_rg=v_b_rg, w_ig=v_w_ig,
                     b_ig=v_b_ig, lru_lambda=v_lru_lambda, lru_proj=v_lru_proj, w_out=v_w_out, norm2_g=v_norm2_g,
                     w_ffn_in=v_w_ffn_in, w_ffn_out=v_w_ffn_out, ple_norm_g=v_ple_norm_g, w_ple_gate=v_w_ple_gate,
                     w_ple_proj=v_w_ple_proj, final_g=v_final_g)
    names = list(weights)
    big = ("w_in", "w_ffn_in", "w_ffn_out", "lru_proj", "w_out", "w_ple_gate", "pool_proj", "w_ple_proj")
    slab_space = {"w_in": red_win[:576], "w_ffn_in": red_wffn[:704]}
    delta, new_m, new_v = {}, {}, {}
    for n in big:
        sh = weights[n].shape
        if n in slab_space:
            as2d = lambda a: a[0].T
            back = lambda a: a.T[None]
            g2d = slab_space[n]
        else:
            as2d = lambda a: a.reshape(sh[-2], sh[-1])
            back = lambda a: a.reshape(sh)
            g2d = as2d(grads[n])
        d_, m_, v_ = _adamw(as2d(weights[n]), g2d, as2d(moments_m[n]), as2d(moments_v[n]), "adamw_" + n)
        delta[n], new_m[n], new_v[n] = back(d_), back(m_), back(v_)
    rest = [n for n in names if n not in big]
    rest_shapes = [weights[n].shape for n in rest]
    packed = [_small_pack([src[n] for n in rest]) for src in (weights, grads, moments_m, moments_v)]
    d_, m_, v_ = _adamw(*packed, "adamw_small")
    for n, a, b_, c_ in zip(rest, _small_unpack(d_, rest_shapes), _small_unpack(m_, rest_shapes),
                            _small_unpack(v_, rest_shapes)):
        delta[n], new_m[n], new_v[n] = a, b_, c_

    return (loss, grad_x[None], *[grads[n] for n in names], *[delta[n] for n in names],
            *[new_m[n] for n in names], *[new_v[n] for n in names])
```

```python
import functools
import math

import jax
import jax.numpy as jnp
from jax import lax
from jax.experimental import pallas as pl
from jax.experimental.pallas import tpu as pltpu

F32 = jnp.float32
D = 1024
NIN = 4608
PW = 512
FF = 2816
MXU_DIM = 256
FF_CHUNKS = tuple((c0, min(4 * MXU_DIM, FF - c0)) for c0 in range(0, FF, 4 * MXU_DIM))
PLE = 256
HEADS, HD = 8, 128
GROUPS, GD = 4, 128
WINDOWS = (2, 4, 8, 16)
HALO = 16
CONV = 4
EPS = 1e-6
LRU_C = 8.0
NDEV = 8
MESH = pl.DeviceIdType.MESH

ADAM_LR, ADAM_B1, ADAM_B2, ADAM_EPS, ADAM_WD, ADAM_STEP = 0.001, 0.9, 0.999, 1e-08, 0.01, 10

_MXU = jnp.bfloat16
TM = 512
TM_SEQ = 256
VMEM_LIMIT = 56 * 1024 * 1024
VMEM_LIMIT_LARGE = 60 * 1024 * 1024
W_FIRST = (("win", 576), ("f32s", 16))
W_SECOND = (("pproj", 128), ("lru", 128), ("wout", 128))
W_THIRD = (("wffn", 704), ("ple", 128), ("wffo", 352), ("wpg", 128))
W_OFF = {}
for _slabs in (W_FIRST, W_SECOND, W_THIRD):
    _o = 0
    for _n, _r in _slabs:
        W_OFF[_n] = (_o, _r)
        _o += _r
SMALL_ROWS = 48


def _cparams(n_axes=1, vmem=VMEM_LIMIT):
    return pltpu.CompilerParams(dimension_semantics=("arbitrary",) * n_axes, vmem_limit_bytes=vmem)


def _my_pos():
    return lax.axis_index("x"), lax.axis_index("y"), lax.axis_index("c")


def _nt(a, b):
    return lax.dot_general(a, b, (((1,), (1,)), ((), ())), preferred_element_type=F32)


def _nn(a, b):
    return lax.dot_general(a, b, (((1,), (0,)), ((), ())), preferred_element_type=F32)


def _tn(a, b):
    return lax.dot_general(a, b, (((0,), (0,)), ((), ())), preferred_element_type=F32)


def _sigmoid(x):
    return 0.5 * jnp.tanh(0.5 * x) + 0.5


_GELU_K = math.sqrt(2.0 / math.pi)


def _gelu_and_grad(x):
    x2 = x * x
    inner = _GELU_K * (x + 0.044715 * x2 * x)
    t = jnp.tanh(inner)
    g = 0.5 * x * (1.0 + t)
    dg = 0.5 * (1.0 + t) + 0.5 * x * (1.0 - t * t) * _GELU_K * (1.0 + 3.0 * 0.044715 * x2)
    return g, dg


def _softplus_neg(lam):
    x = -lam
    t = jnp.exp(-jnp.abs(x))
    u = 1.0 + t
    l1p = jnp.where(u == 1.0, t, jnp.log(u) * t / (u - 1.0))
    return jnp.maximum(x, 0.0) + l1p, _sigmoid(x)


def _start_slab_loads(g_ref, name, dst_ref, sems, base, width=D):
    off, rows = W_OFF[name]
    copies = []
    for k in range(NDEV):
        if width == D:
            src = g_ref.at[k, pl.ds(off, rows), :]
        else:
            src = g_ref.at[k, pl.ds(off, rows), pl.ds(0, width)]
        cp = pltpu.make_async_copy(src, dst_ref.at[pl.ds(k * rows, rows), :], sems.at[base + k])
        cp.start()
        copies.append(cp)
    return copies


def _load_weights(g_ref, items, sems):
    copies = []
    for n, (name, dst, width) in enumerate(items):
        copies += _start_slab_loads(g_ref, name, dst, sems, n * NDEV, width)
    for cp in copies:
        cp.wait()


class _Gather:
    def __init__(self, own_ref, out_ref, stage, send_sems, recv_sems, local_sem):
        x, y, c = _my_pos()
        self.c = c
        self.me, self.sibling = (x, y, c), (x, y, 1 - c)
        self.chips = [(1 - x, y), (x, 1 - y), (1 - x, 1 - y)]
        self.own_ref, self.out_ref, self.stage = own_ref, out_ref, stage
        self.send_sems, self.recv_sems = send_sems, recv_sems
        self.mine = pltpu.make_async_copy(stage, self.slab(*self.me), local_sem)
        self.first = [self.copy(0, self.me, self.sibling, src=stage)] + [
            self.copy(1 + j, self.me, (*chip, c), src=stage) for j, chip in enumerate(self.chips)]
        self.passed = [self.copy(4 + j, (*chip, c), self.sibling) for j, chip in enumerate(self.chips)]

    def slab(self, px, py, pc):
        return self.out_ref.at[4 * px + 2 * py + pc]

    def copy(self, k, block, to, src=None):
        return pltpu.make_async_remote_copy(
            src_ref=self.slab(*block) if src is None else src, dst_ref=self.slab(*block),
            send_sem=self.send_sems.at[k], recv_sem=self.recv_sems.at[k], device_id=to, device_id_type=MESH)

    def send_mine(self, far=True):
        pltpu.sync_copy(self.own_ref, self.stage)
        self.mine.start()
        for cp in self.first[:3]:
            cp.start()
        if far:
            self.send_far()

    def send_far(self):
        self.first[3].start()

    def pass_on(self, js):
        for j in js:
            self.copy(1 + j, (*self.chips[j], self.c), self.me).wait_recv()
            self.passed[j].start()

    def wait_sibling(self):
        self.copy(0, self.sibling, self.me).wait_recv()

    def wait_passed(self, js):
        for j in js:
            self.copy(4 + j, (*self.chips[j], 1 - self.c), self.me).wait_recv()

    def finish_sends(self):
        for cp in self.first + self.passed:
            cp.wait_send()
        self.mine.wait()


def _all_gather_small(piece):
    rows = piece.shape[0]

    def body(p_ref, out_ref, send_sems, recv_sems, local_sem):
        x, y, c = _my_pos()
        me = 4 * x + 2 * y + c
        mine = pltpu.make_async_copy(p_ref, out_ref.at[pl.ds(pl.multiple_of(me * rows, 8), rows), :], local_sem)
        mine.start()
        sends = []
        peers = []
        for r in range(1, NDEV):
            px = 1 - x if (r >> 2) & 1 else x
            py = 1 - y if (r >> 1) & 1 else y
            pc = 1 - c if r & 1 else c
            peers.append((px, py, pc))
            cp = pltpu.make_async_remote_copy(
                src_ref=p_ref, dst_ref=out_ref.at[pl.ds(pl.multiple_of(me * rows, 8), rows), :],
                send_sem=send_sems.at[r - 1], recv_sem=recv_sems.at[r - 1], device_id=(px, py, pc),
                device_id_type=MESH)
            cp.start()
            sends.append(cp)
        for r, (px, py, pc) in enumerate(peers):
            them = 4 * px + 2 * py + pc
            pltpu.make_async_remote_copy(
                src_ref=p_ref, dst_ref=out_ref.at[pl.ds(pl.multiple_of(them * rows, 8), rows), :],
                send_sem=send_sems.at[r], recv_sem=recv_sems.at[r], device_id=(px, py, pc),
                device_id_type=MESH).wait_recv()
        for cp in sends:
            cp.wait_send()
        mine.wait()

    return pl.pallas_call(
        body, name="ag_small",
        out_shape=jax.ShapeDtypeStruct((NDEV * rows, piece.shape[1]), piece.dtype),
        in_specs=[pl.BlockSpec(memory_space=pltpu.VMEM)],
        out_specs=pl.BlockSpec(memory_space=pl.ANY),
        scratch_shapes=[pltpu.SemaphoreType.DMA((7,)), pltpu.SemaphoreType.DMA((7,)), pltpu.SemaphoreType.DMA],
    )(piece)


def _row_block(rows, target=512, mult=8):
    b = min(rows, target) // mult * mult
    while rows % b:
        b -= mult
    return b


def _sum_arrays(arrs, name, narrow=None, target=704):
    rows, cols = arrs[0].shape
    br = _row_block(rows, target, 16)
    n = len(arrs)

    def body(*refs):
        acc = refs[0][...].astype(F32)
        for r in refs[1:n]:
            acc = acc + r[...].astype(F32)
        refs[n][...] = acc
        if narrow is not None:
            refs[n + 1][...] = acc.astype(narrow)

    spec = pl.BlockSpec((br, cols), lambda i: (i, 0))
    shape = jax.ShapeDtypeStruct((rows, cols), F32)
    if narrow is None:
        out_shape, out_specs = shape, spec
    else:
        out_shape, out_specs = (shape, jax.ShapeDtypeStruct((rows, cols), narrow)), (spec, spec)
    return pl.pallas_call(
        body, name=name, grid=(rows // br,), out_shape=out_shape,
        in_specs=[spec] * n, out_specs=out_specs, compiler_params=_cparams(1),
    )(*arrs)


def _adamw(w, g, m, v, name):
    rows, cols = w.shape
    br = _row_block(rows, 256)

    def body(w_ref, g_ref, m_ref, v_ref, d_ref, nm_ref, nv_ref):
        g_ = g_ref[...]
        m_ = ADAM_B1 * m_ref[...] + (1.0 - ADAM_B1) * g_
        v_ = ADAM_B2 * v_ref[...] + (1.0 - ADAM_B2) * (g_ * g_)
        m_hat = m_ / (1.0 - ADAM_B1 ** ADAM_STEP)
        v_hat = v_ / (1.0 - ADAM_B2 ** ADAM_STEP)
        d_ref[...] = -ADAM_LR * (m_hat / (jnp.sqrt(v_hat) + ADAM_EPS) + ADAM_WD * w_ref[...])
        nm_ref[...] = m_
        nv_ref[...] = v_

    spec = pl.BlockSpec((br, cols), lambda i: (i, 0))
    shape = jax.ShapeDtypeStruct((rows, cols), F32)
    return pl.pallas_call(
        body, name=name, grid=(rows // br,), out_shape=(shape, shape, shape),
        in_specs=[spec] * 4, out_specs=(spec, spec, spec), compiler_params=_cparams(1),
    )(w, g, m, v)


_CHIP_FLIPS = (2, 3, 1, 0)


def _grad_matmul(lhs, rhs, name):
    s, r = lhs.shape
    k = rhs.shape[1]
    tm = min(4 * TM, s)

    def body(l_ref, r_ref, o_ref):
        @pl.when(pl.program_id(0) == 0)
        def _():
            o_ref[...] = jnp.zeros_like(o_ref)

        o_ref[:, pl.ds(0, k)] += _tn(l_ref[...].astype(_MXU), r_ref[...].astype(_MXU))

    return pl.pallas_call(
        body, name=name, grid=(s // tm,),
        out_shape=jax.ShapeDtypeStruct((r, D), F32),
        in_specs=[pl.BlockSpec((tm, r), lambda i: (i, 0)), pl.BlockSpec((tm, k), lambda i: (i, 0))],
        out_specs=pl.BlockSpec((r, D), lambda i: (0, 0)),
        compiler_params=_cparams(1),
    )(lhs, rhs)


def _grad_matmul_rs(lhs, rhs, name, rows, extras=(), narrow=None, tail=0):
    s, r8 = lhs.shape
    k = rhs.shape[1]
    cpb = 1
    nblk = 4 // cpb
    nx = len(extras)
    ers = [e.shape[0] // NDEV for e in extras]
    er = sum(ers)
    srows = rows + er
    brows = 2 * cpb * srows
    groups = [(0, srows - tail, F32 if narrow is None else narrow)] + ([(srows - tail, tail, F32)] if tail else [])
    ng = len(groups)
    resident = ((2 * brows + 2 * srows) * D * 4 + sum(2 * n * D * jnp.dtype(dt).itemsize for _, n, dt in groups)
                + 2 * cpb * rows * k * 4)
    per_token = 2 * (2 * cpb * rows * lhs.dtype.itemsize + k * rhs.dtype.itemsize)
    fitting = [t for t in (4 * TM, 2 * TM, TM)
               if s % t == 0 and 2 * t <= s and resident + t * per_token <= VMEM_LIMIT_LARGE * 9 // 10]
    tm = fitting[0] if fitting else min(TM, s)
    nt = s // tm
    mid = min(nt - 1, max(1, nt // 6))

    def flip_of(p):
        return jnp.where(p == 0, 2, jnp.where(p == 1, 3, jnp.where(p == 2, 1, 0)))

    def block_col(b):
        x, y, _ = _my_pos()
        return (2 * x + y) ^ flip_of(b)

    def body(*refs):
        l_ref, r_ref = refs[:2]
        x_refs = refs[2:2 + nx]
        rest = refs[2 + nx:]
        town_ref = rest[0]
        lici_refs = rest[1:1 + ng]
        acc, stage = rest[1 + ng:3 + ng]
        send_bufs = rest[3 + ng:3 + 2 * ng]
        dsend, drecv, isend, irecv, xsem = rest[3 + 2 * ng:]
        b = pl.program_id(0)
        i = pl.program_id(1)
        x, y, c = _my_pos()
        mine = 2 * x + y
        sibling = (x, y, 1 - c)

        def chip_at(p):
            return mine ^ _CHIP_FLIPS[p]

        def slab_rows(p, parity):
            within = 0 if cpb == 1 else (chip_at(p) & 1) * 2
            return pl.ds(pl.multiple_of((within + parity) * srows, 8), srows)

        def push(p, slot):
            return pltpu.make_async_remote_copy(
                src_ref=acc.at[slot, slab_rows(p, 1 - c), :], dst_ref=stage.at[p % 2],
                send_sem=dsend.at[p], recv_sem=drecv.at[p], device_id=sibling, device_id_type=MESH)

        def ici(p):
            ch = chip_at(p)
            return [pltpu.make_async_remote_copy(
                src_ref=send_bufs[g].at[p % 2], dst_ref=lici_refs[g].at[p], send_sem=isend.at[3 * g + p],
                recv_sem=irecv.at[3 * g + p], device_id=(ch >> 1, ch & 1, c), device_id_type=MESH) for g in range(ng)]

        def extra_loads(p, slot):
            copies = []
            within = 0 if cpb == 1 else (chip_at(p) & 1) * 2
            for parity in range(2):
                off = rows
                for n, (x_ref, e) in enumerate(zip(x_refs, ers)):
                    src = x_ref.at[pl.ds(pl.multiple_of((2 * chip_at(p) + parity) * e, 8), e), :]
                    dst = acc.at[slot, pl.ds(pl.multiple_of((within + parity) * srows + off, 8), e), :]
                    copies.append(pltpu.make_async_copy(src, dst, xsem.at[(p * 2 + parity) * nx + n]))
                    off += e
            return copies

        def combine(p, slot):
            push(p, slot).wait_recv()
            total = acc[slot, slab_rows(p, c), :] + stage[p % 2]
            if p == 3:
                stage[p % 2] = total
                pltpu.sync_copy(stage.at[p % 2], town_ref)
            else:
                if p == 2:
                    for cp in ici(0):
                        cp.wait_send()
                for g, (r0, n, dt) in enumerate(groups):
                    send_bufs[g][p % 2] = total[r0:r0 + n, :].astype(dt)
                for cp in ici(p):
                    cp.start()

        for bb in range(nblk):
            slot = bb % 2
            positions = list(range(bb * cpb, (bb + 1) * cpb))

            @pl.when(jnp.logical_and(b == bb, i == 0))
            def _(bb=bb, slot=slot, positions=positions):
                if bb >= 2:
                    for p in range((bb - 2) * cpb, (bb - 1) * cpb):
                        push(p, slot).wait_send()
                for q in range(2 * cpb):
                    acc[slot, pl.ds(q * srows, rows), :] = jnp.zeros((rows, D), F32)
                for p in positions:
                    for cp in extra_loads(p, slot):
                        cp.start()

            if bb >= 1:
                @pl.when(jnp.logical_and(b == bb, i == mid))
                def _(bb=bb):
                    for p in range((bb - 1) * cpb, bb * cpb):
                        combine(p, (bb - 1) % 2)

        res = _tn(l_ref[...].astype(_MXU), r_ref[...].astype(_MXU))
        slot_now = b % 2
        for q in range(2 * cpb):
            acc[slot_now, pl.ds(q * srows, rows), pl.ds(0, k)] += res[q * rows:(q + 1) * rows, :]

        for bb in range(nblk):
            slot = bb % 2
            positions = list(range(bb * cpb, (bb + 1) * cpb))

            @pl.when(jnp.logical_and(b == bb, i == nt - 1))
            def _(bb=bb, slot=slot, positions=positions):
                for p in positions:
                    for cp in extra_loads(p, slot):
                        cp.wait()
                for p in positions:
                    push(p, slot).start()
                if bb == nblk - 1:
                    for p in positions:
                        combine(p, slot)
                    for p in range(max(0, (nblk - 2) * cpb), 4):
                        push(p, slot).wait_send()
                    for p in range(1, 3):
                        for cp in ici(p):
                            cp.wait_send()
                    for p in range(3):
                        for cp in ici(p):
                            cp.wait_recv()

    in_specs = [pl.BlockSpec((tm, 2 * cpb * rows), lambda b, i: (i, block_col(b))),
                pl.BlockSpec((tm, k), lambda b, i: (i, 0))]
    any_spec = pl.BlockSpec(memory_space=pl.ANY)
    in_specs += [any_spec] * nx
    args = [lhs, rhs, *extras]
    outs = pl.pallas_call(
        body, name=name, grid=(nblk, nt),
        out_shape=(jax.ShapeDtypeStruct((srows, D), F32),)
        + tuple(jax.ShapeDtypeStruct((3, n, D), dt) for _, n, dt in groups),
        in_specs=in_specs, out_specs=(any_spec,) * (1 + ng),
        scratch_shapes=[pltpu.VMEM((2, brows, D), F32), pltpu.VMEM((2, srows, D), F32)]
        + [pltpu.VMEM((2, n, D), dt) for _, n, dt in groups]
        + [pltpu.SemaphoreType.DMA((4,)), pltpu.SemaphoreType.DMA((4,)), pltpu.SemaphoreType.DMA((3 * ng,)),
           pltpu.SemaphoreType.DMA((3 * ng,)), pltpu.SemaphoreType.DMA((max(1, 8 * nx),))],
        compiler_params=_cparams(2, VMEM_LIMIT_LARGE),
    )(*args)
    t_own = outs[0]
    return [(t_own[r0:r0 + n], landed) for (r0, n, _), landed in zip(groups, outs[1:])]


def _inproj_fwd(x, g1, own_first, own_second):
    s = x.shape[0]
    tm = min(2 * TM, s // 2)
    nt = s // tm
    assert nt % 2 == 0
    rows1, rows2 = own_first.shape[0], own_second.shape[0]
    wrows = W_OFF["win"][1]
    cw = 2 * wrows

    def chip_col(b):
        px, py, _ = _my_pos()
        return (2 * px + py) ^ jnp.where(b == 0, 0, jnp.where(b == 1, 2, jnp.where(b == 2, 1, 3)))

    def body(x_ref, g1_ref, own1_ref, own2_ref, u_ref, z_ref, gw1_ref, gw2_ref, w_vmem, u_buf, stage1, stage2, sems,
             usem, send1, recv1, local1, send2, recv2, local2):
        b = pl.program_id(0)
        i = pl.program_id(1)
        ga = _Gather(own1_ref, gw1_ref, stage1, send1, recv1, local1)
        gb = _Gather(own2_ref, gw2_ref, stage2, send2, recv2, local2)
        c = ga.c

        def load_chip(px, py, own_too):
            copies = []
            for pc in range(2):
                dst = w_vmem.at[pl.ds(pc * wrows, wrows), :]
                copies.append(pltpu.make_async_copy(gw1_ref.at[4 * px + 2 * py + pc, pl.ds(0, wrows), :], dst,
                                                    sems.at[pc]))
            if own_too:
                mine_dst = w_vmem.at[pl.ds(pl.multiple_of(c * wrows, 16), wrows), :]
                copies[0] = pltpu.make_async_copy(own1_ref.at[pl.ds(0, wrows), :], mine_dst, sems.at[0])
                theirs_dst = w_vmem.at[pl.ds(pl.multiple_of((1 - c) * wrows, 16), wrows), :]
                copies[1] = pltpu.make_async_copy(gw1_ref.at[4 * px + 2 * py + 1 - c, pl.ds(0, wrows), :], theirs_dst,
                                                  sems.at[1])
            for cp in copies:
                cp.start()
            for cp in copies:
                cp.wait()

        @pl.when(jnp.logical_and(b == 0, i == 0))
        def _():
            ga.send_mine(far=False)
            ga.wait_sibling()
            load_chip(ga.me[0], ga.me[1], True)

        @pl.when(jnp.logical_and(b == 0, i == nt // 4))
        def _():
            ga.send_far()
            gb.send_mine()

        @pl.when(jnp.logical_and(b == 0, i == (3 * nt) // 4))
        def _():
            ga.pass_on((0, 1))

        @pl.when(jnp.logical_and(b == 1, i == (3 * nt) // 4))
        def _():
            ga.pass_on((2,))

        for j in range(3):
            @pl.when(jnp.logical_and(b == j + 1, i == 0))
            def _(j=j):
                ga.wait_passed((j,))
                load_chip(ga.chips[j][0], ga.chips[j][1], False)

        @pl.when(jnp.logical_and(b == 2, i == nt // 2))
        def _():
            gb.pass_on((0, 1))

        @pl.when(jnp.logical_and(b == 3, i == (3 * nt) // 4))
        def _():
            gb.pass_on((2,))

        slot = i % 2

        def u_write(t, sl):
            return pltpu.make_async_copy(u_buf.at[sl], u_ref.at[pl.ds(pl.multiple_of(t * tm, tm), tm), :], usem.at[sl])

        def u_read(t, sl):
            return pltpu.make_async_copy(u_ref.at[pl.ds(pl.multiple_of(t * tm, tm), tm), :], u_buf.at[sl], usem.at[sl])

        @pl.when(b == 0)
        def _():
            @pl.when(i >= 2)
            def _():
                u_write(i - 2, slot).wait()

            xv = x_ref[...]
            inv = lax.rsqrt(jnp.mean(xv * xv, axis=-1, keepdims=True) + EPS)
            u_buf[slot] = (xv * inv * g1_ref[...]).astype(_MXU)
            u_write(i, slot).start()

            @pl.when(i == nt - 1)
            def _():
                u_write(i - 1, 1 - slot).wait()
                u_write(i, slot).wait()
                u_read(0, 0).start()

        @pl.when(b > 0)
        def _():
            u_read(i, slot).wait()

            @pl.when(jnp.logical_or(b < 3, i < nt - 1))
            def _():
                u_read((i + 1) % nt, 1 - slot).start()

        z_ref[...] = _nt(u_buf[slot], w_vmem[...])

        @pl.when(jnp.logical_and(b == 3, i == nt - 1))
        def _():
            ga.finish_sends()
            gb.wait_sibling()
            gb.wait_passed((0, 1, 2))
            gb.finish_sends()

    any_spec = pl.BlockSpec(memory_space=pl.ANY)
    dma7 = pltpu.SemaphoreType.DMA((7,))
    return pl.pallas_call(
        body, name="inproj_fwd", grid=(4, nt),
        out_shape=(jax.ShapeDtypeStruct((s, D), _MXU), jax.ShapeDtypeStruct((s, NIN), F32),
                   jax.ShapeDtypeStruct((NDEV, rows1, D), own_first.dtype),
                   jax.ShapeDtypeStruct((NDEV, rows2, D), own_second.dtype)),
        in_specs=[pl.BlockSpec((tm, D), lambda b, i: (jnp.where(b == 0, i, nt - 1), 0)),
                  pl.BlockSpec((1, D), lambda b, i: (0, 0)), any_spec, any_spec],
        out_specs=(any_spec, pl.BlockSpec((tm, cw), lambda b, i: (i, chip_col(b))), any_spec, any_spec),
        scratch_shapes=[pltpu.VMEM((cw, D), _MXU), pltpu.VMEM((2, tm, D), _MXU), pltpu.VMEM((rows1, D), own_first.dtype),
                        pltpu.VMEM((rows2, D), own_second.dtype), pltpu.SemaphoreType.DMA((2,)),
                        pltpu.SemaphoreType.DMA((2,)),
                        dma7, dma7, pltpu.SemaphoreType.DMA, dma7, dma7, pltpu.SemaphoreType.DMA],
        compiler_params=_cparams(2),
    )(x, g1, own_first, own_second)


def _pool_tile(pbuf, t0, tm, pw_ref, scale_ref):
    t = t0 + lax.broadcasted_iota(jnp.int32, (tm, GD), 0)
    pooled, mixed_pre = [], []
    for g, w in enumerate(WINDOWS):
        cs = pl.ds(g * GD, GD)
        cur = pbuf[pl.ds(HALO, tm), cs]
        acc = cur
        for d in range(1, w):
            acc = acc + pbuf[pl.ds(HALO - d, tm), cs]
        cnt = jnp.minimum(t + 1, w).astype(F32)
        pg = acc / cnt - cur
        pooled.append(pg)
        mixed_pre.append(_nn(pg.astype(_MXU), pw_ref[g]))
    return pooled, mixed_pre


def _lru_gates_head(hh, lbuf, start, tm, cw_ref, cb_ref, wrg_ref, brg_ref, wig_ref, big_ref, sp):
    cs = pl.ds(hh * HD, HD)
    xc = cb_ref[:, cs] + cw_ref[pl.ds(CONV - 1, 1), cs] * lbuf[pl.ds(HALO, tm), cs]
    for k in range(CONV - 1):
        xc = xc + cw_ref[pl.ds(k, 1), cs] * lbuf[pl.ds(HALO - (CONV - 1) + k, tm), cs]
    xcm = xc.astype(_MXU)
    r = _sigmoid(_nn(xcm, wrg_ref[hh]) + brg_ref[pl.ds(hh, 1), :])
    ig = _sigmoid(_nn(xcm, wig_ref[hh]) + big_ref[pl.ds(hh, 1), :])
    a = jnp.exp(-LRU_C * r * sp[:, hh * HD:(hh + 1) * HD])
    one_m = 1.0 - a * a
    live = jnp.logical_and(one_m > 0.0, jnp.logical_not(start))
    inv_mult = lax.rsqrt(jnp.where(live, one_m, 1.0))
    mult = jnp.where(live, one_m * inv_mult, jnp.where(start, 1.0, 0.0))
    return xc, r, ig, a, live, inv_mult, mult


def _seg_layout(tm):
    seg = tm // 8
    return seg, seg + 8


def _to_segments(dst_ref, hh, val, tm):
    seg, pitch = _seg_layout(tm)
    for s in range(8):
        dst_ref[hh, pl.ds(s * pitch, seg), :] = val[s * seg:(s + 1) * seg, :]


def _from_segments(src_ref, hh, tm):
    seg, pitch = _seg_layout(tm)
    return jnp.concatenate([src_ref[hh, pl.ds(s * pitch, seg), :] for s in range(8)], axis=0)


def _segment_scan(a_ref, b_ref, out_ref, hk, pk, carry_ref, tm, reverse):
    seg, pitch = _seg_layout(tm)
    row = lax.broadcasted_iota(jnp.int32, (8, HD), 0)
    order = range(seg - 1, -1, -1) if reverse else range(seg)
    for hh in range(HEADS):
        cs = pl.ds(hh * HD, HD)
        if reverse:
            a0 = a_ref[hh, pl.ds(0, 8, stride=pitch), :]
            a_wrap = jnp.where(row <= 6, pltpu.roll(a0, 7, 0), 1.0)
        hv = jnp.zeros((8, HD), F32)
        pv = jnp.ones((8, HD), F32)
        for k in order:
            if not reverse:
                av = a_ref[hh, pl.ds(k, 8, stride=pitch), :]
            elif k + 1 < seg:
                av = a_ref[hh, pl.ds(k + 1, 8, stride=pitch), :]
            else:
                av = a_wrap
            hv = av * hv + b_ref[hh, pl.ds(k, 8, stride=pitch), :]
            pv = av * pv
            hk[hh, pl.ds(8 * k, 8), :] = hv
            pk[hh, pl.ds(8 * k, 8), :] = pv
        for d in (1, 2, 4):
            if reverse:
                keep, sh = row < 8 - d, 8 - d
            else:
                keep, sh = row >= d, d
            hv = hv + pv * jnp.where(keep, pltpu.roll(hv, sh, 0), 0.0)
            pv = pv * jnp.where(keep, pltpu.roll(pv, sh, 0), 1.0)
        cin = carry_ref[:, cs]
        ends = hv + pv * cin
        if reverse:
            enter = jnp.where(row <= 6, pltpu.roll(ends, 7, 0), cin)
            carry_ref[:, cs] = jnp.broadcast_to((a0 * ends)[0:1, :], (8, HD))
        else:
            enter = jnp.where(row >= 1, pltpu.roll(ends, 1, 0), cin)
            carry_ref[:, cs] = jnp.broadcast_to(ends[7:8, :], (8, HD))
        for k in range(seg):
            out_ref[hh, pl.ds(k, 8, stride=pitch), :] = hk[hh, pl.ds(8 * k, 8), :] + pk[hh, pl.ds(8 * k, 8), :] * enter


def _mixer_fwd(z, x, gw, small, own_third):
    s = x.shape[0]
    tm = min(TM_SEQ, s)
    nt = s // tm
    rows3 = own_third.shape[0]
    (pool_w, pool_scale, conv_w, conv_b, w_rg, b_rg, w_ig, b_ig, lam, b_gate) = small

    def body(z_ref, x_ref, gw_ref, own3_ref, pw_ref, ps_ref, cw_ref, cb_ref, wrg_ref, brg_ref, wig_ref, big_ref, lam_ref,
             bg_ref, h_ref, yl_ref, mg_ref, yp_ref, yr_ref, h1_ref, a_ref, r_ref, ig_ref, xc_ref, gw3_ref,
             pprojT, lru_w, wout_w, pbuf, lbuf, a_s, b_s, h_s, hk, pk, hcar, sems, stage3, send3, recv3, local3):
        i = pl.program_id(0)
        t0 = i * tm
        gc = _Gather(own3_ref, gw3_ref, stage3, send3, recv3, local3)

        @pl.when(i == 0)
        def _():
            gc.send_mine()
            _load_weights(gw_ref, [("pproj", pprojT, PW), ("lru", lru_w, D), ("wout", wout_w, D)], sems)
            pbuf[pl.ds(0, HALO), :] = jnp.zeros((HALO, PW), F32)
            lbuf[pl.ds(0, HALO), :] = jnp.zeros((HALO, D), F32)
            hcar[...] = jnp.zeros_like(hcar)

        @pl.when(i == nt // 2)
        def _():
            gc.pass_on((0, 1))

        @pl.when(i == (3 * nt) // 4)
        def _():
            gc.pass_on((2,))

        pbuf[pl.ds(HALO, tm), :] = z_ref[:, pl.ds(0, PW)]
        _, mixed_pre = _pool_tile(pbuf, t0, tm, pw_ref, ps_ref)
        mixed = jnp.concatenate(mixed_pre, axis=1) * ps_ref[...]
        y_pool = _nt(mixed.astype(_MXU), pprojT[...])
        pbuf[pl.ds(0, HALO), :] = pbuf[pl.ds(tm, HALO), :]

        lbuf[pl.ds(HALO, tm), :] = z_ref[:, pl.ds(PW, D)]
        sp, _ = _softplus_neg(lam_ref[...])
        start = (t0 + lax.broadcasted_iota(jnp.int32, (tm, HD), 0)) == 0
        for hh in range(HEADS):
            xc, r, ig, a, _, _, mult = _lru_gates_head(hh, lbuf, start, tm, cw_ref, cb_ref, wrg_ref, brg_ref,
                                                       wig_ref, big_ref, sp)
            _to_segments(a_s, hh, a, tm)
            _to_segments(b_s, hh, mult * ig * xc, tm)
            cs = pl.ds(hh * HD, HD)
            a_ref[:, cs] = a
            r_ref[:, cs] = r.astype(_MXU)
            ig_ref[:, cs] = ig.astype(_MXU)
            xc_ref[:, cs] = xc.astype(_MXU)
        lbuf[pl.ds(0, HALO), :] = lbuf[pl.ds(tm, HALO), :]
        _segment_scan(a_s, b_s, h_s, hk, pk, hcar, tm, reverse=False)
        for hh in range(HEADS):
            h_ref[:, pl.ds(hh * HD, HD)] = _from_segments(h_s, hh, tm)
        gel, _ = _gelu_and_grad(z_ref[:, pl.ds(PW + D, D)])
        yl = (h_ref[...] * gel).astype(_MXU)
        yl_ref[...] = yl
        y_lru = _nn(yl, lru_w[...])

        g0 = _sigmoid(z_ref[:, pl.ds(PW + 2 * D, D)] + bg_ref[pl.ds(0, 1), :])
        g1 = _sigmoid(z_ref[:, pl.ds(PW + 3 * D, D)] + bg_ref[pl.ds(1, 1), :])
        merged = (g0 * y_pool + g1 * y_lru).astype(_MXU)
        mg_ref[...] = merged
        yp_ref[...] = y_pool.astype(_MXU)
        yr_ref[...] = y_lru.astype(_MXU)
        h1_ref[...] = x_ref[...] + _nn(merged, wout_w[...])

        @pl.when(i == nt - 1)
        def _():
            gc.wait_sibling()
            gc.wait_passed((0, 1, 2))
            gc.finish_sends()

    tok = lambda w, dt: jax.ShapeDtypeStruct((s, w), dt)
    tspec = lambda w: pl.BlockSpec((tm, w), lambda i: (i, 0))
    full = lambda a: pl.BlockSpec(a.shape, lambda i: (0,) * a.ndim)
    any_spec = pl.BlockSpec(memory_space=pl.ANY)
    seg_buf = pltpu.VMEM((HEADS, 8 * _seg_layout(tm)[1], HD), F32)
    dma7 = pltpu.SemaphoreType.DMA((7,))
    return pl.pallas_call(
        body, name="mixer_fwd", grid=(nt,),
        out_shape=(tok(D, F32), tok(D, _MXU), tok(D, _MXU), tok(D, _MXU), tok(D, _MXU), tok(D, F32),
                   tok(D, F32), tok(D, _MXU), tok(D, _MXU), tok(D, _MXU),
                   jax.ShapeDtypeStruct((NDEV, rows3, D), own_third.dtype)),
        in_specs=[tspec(NIN), tspec(D), any_spec, any_spec] + [full(a) for a in small],
        out_specs=(tspec(D),) * 10 + (any_spec,),
        scratch_shapes=[pltpu.VMEM((D, PW), _MXU), pltpu.VMEM((D, D), _MXU), pltpu.VMEM((D, D), _MXU),
                        pltpu.VMEM((tm + HALO, PW), F32), pltpu.VMEM((tm + HALO, D), F32),
                        seg_buf, seg_buf, seg_buf, pltpu.VMEM((HEADS, tm, HD), F32), pltpu.VMEM((HEADS, tm, HD), F32),
                        pltpu.VMEM((8, D), F32), pltpu.SemaphoreType.DMA((3 * NDEV,)),
                        pltpu.VMEM((rows3, D), own_third.dtype), dma7, dma7, pltpu.SemaphoreType.DMA],
        compiler_params=_cparams(1),
    )(z, x, gw, own_third, *small)


def _ffn_fwd(h1, g2, gw):
    s = h1.shape[0]
    tm = min(TM, s)

    def body(h1_ref, g2_ref, gw_ref, v_ref, gf_ref, uf_ref, h2_ref, wffnT, wffo, sems):
        @pl.when(pl.program_id(0) == 0)
        def _():
            _load_weights(gw_ref, [("wffn", wffnT, D), ("wffo", wffo, D)], sems)

        hv = h1_ref[...]
        inv = lax.rsqrt(jnp.mean(hv * hv, axis=-1, keepdims=True) + EPS)
        v = (hv * inv * g2_ref[...]).astype(_MXU)
        v_ref[...] = v
        acc = hv
        for c0, cn in FF_CHUNKS:
            cs = pl.ds(c0, cn)
            gf = _nt(v, wffnT[cs, :]).astype(_MXU)
            uf = _nt(v, wffnT[pl.ds(FF + c0, cn), :]).astype(_MXU)
            gf_ref[:, cs] = gf
            uf_ref[:, cs] = uf
            gf32 = gf.astype(F32)
            act = (gf32 * _sigmoid(gf32) * uf.astype(F32)).astype(_MXU)
            acc = acc + _nn(act, wffo[cs, :])
        h2_ref[...] = acc

    tspec = lambda w: pl.BlockSpec((tm, w), lambda i: (i, 0))
    return pl.pallas_call(
        body, name="ffn_fwd", grid=(s // tm,),
        out_shape=(jax.ShapeDtypeStruct((s, D), _MXU), jax.ShapeDtypeStruct((s, FF), _MXU),
                   jax.ShapeDtypeStruct((s, FF), _MXU), jax.ShapeDtypeStruct((s, D), F32)),
        in_specs=[tspec(D), pl.BlockSpec((1, D), lambda i: (0, 0)), pl.BlockSpec(memory_space=pl.ANY)],
        out_specs=(tspec(D), tspec(FF), tspec(FF), tspec(D)),
        scratch_shapes=[pltpu.VMEM((2 * FF, D), _MXU), pltpu.VMEM((FF, D), _MXU), pltpu.SemaphoreType.DMA((2 * NDEV,))],
        compiler_params=_cparams(1),
    )(h1, g2, gw)


def _rms_bwd(dy, xn, inv, g):
    dg = jnp.sum(dy * xn, axis=0, keepdims=True)
    dxn = dy * g
    dx = inv * (dxn - xn * jnp.mean(dxn * xn, axis=-1, keepdims=True))
    return dx, dg


def _ple_loss_fwd_bwd(h2, p, target, g3, gfin, gw):
    s = h2.shape[0]
    tm = min(TM, s)

    def body(h2_ref, p_ref, t_ref, g3_ref, gf_ref, gw_ref,
             dh2_ref, loss_ref, dg3_ref, dgf_ref, gwpg_ref, gple_ref, wpg, pleT, sems):
        i = pl.program_id(0)

        @pl.when(i == 0)
        def _():
            _load_weights(gw_ref, [("wpg", wpg, D), ("ple", pleT, PLE)], sems)
            for ref in (loss_ref, dg3_ref, dgf_ref, gwpg_ref, gple_ref):
                ref[...] = jnp.zeros_like(ref)

        hv = h2_ref[...]
        inv3 = lax.rsqrt(jnp.mean(hv * hv, axis=-1, keepdims=True) + EPS)
        xn3 = hv * inv3
        n3 = (xn3 * g3_ref[...]).astype(_MXU)
        pg = _sigmoid(_nn(n3, wpg[...]))
        pm = p_ref[...].astype(_MXU)
        e = _nt(pm, pleT[...])
        h3 = hv + pg * e
        invf = lax.rsqrt(jnp.mean(h3 * h3, axis=-1, keepdims=True) + EPS)
        xf = h3 * invf
        diff = xf * gf_ref[...] - t_ref[...]
        loss_ref[...] += jnp.sum(diff * diff) * (0.5 / D)
        dh3, dgf = _rms_bwd(diff * (1.0 / D), xf, invf, gf_ref[...])
        dgf_ref[...] += dgf
        gple_ref[:, pl.ds(0, PLE)] += _tn((dh3 * pg).astype(_MXU), pm)
        dpg = (dh3 * e * pg * (1.0 - pg)).astype(_MXU)
        gwpg_ref[...] += _tn(n3, dpg)
        dn3 = _nt(dpg, wpg[...])
        dx3, dg3 = _rms_bwd(dn3, xn3, inv3, g3_ref[...])
        dg3_ref[...] += dg3
        dh2_ref[...] = dh3 + dx3

    tspec = lambda w: pl.BlockSpec((tm, w), lambda i: (i, 0))
    vec = pl.BlockSpec((1, D), lambda i: (0, 0))
    mat = pl.BlockSpec((D, D), lambda i: (0, 0))
    return pl.pallas_call(
        body, name="ple_loss", grid=(s // tm,),
        out_shape=(jax.ShapeDtypeStruct((s, D), F32), jax.ShapeDtypeStruct((8, 128), F32),
                   jax.ShapeDtypeStruct((1, D), F32), jax.ShapeDtypeStruct((1, D), F32),
                   jax.ShapeDtypeStruct((D, D), F32), jax.ShapeDtypeStruct((D, D), F32)),
        in_specs=[tspec(D), tspec(PLE), tspec(D), vec, vec, pl.BlockSpec(memory_space=pl.ANY)],
        out_specs=(tspec(D), pl.BlockSpec((8, 128), lambda i: (0, 0)), vec, vec, mat, mat),
        scratch_shapes=[pltpu.VMEM((D, D), _MXU), pltpu.VMEM((D, PLE), _MXU), pltpu.SemaphoreType.DMA((2 * NDEV,))],
        compiler_params=_cparams(1),
    )(h2, p, target, g3, gfin, gw)


def _ffn_bwd_hidden(dh2, gf, uf, gw):
    s = dh2.shape[0]
    tm = min(TM, s)
    nt = s // tm

    def body(dh2_ref, gf_ref, uf_ref, gw_ref, dff_ref, gwo_ref, wffo, gacc, sems):
        i = pl.program_id(0)

        @pl.when(i == 0)
        def _():
            _load_weights(gw_ref, [("wffo", wffo, D)], sems)
            gacc[...] = jnp.zeros_like(gacc)

        dm = dh2_ref[...].astype(_MXU)
        for c0, cn in FF_CHUNKS:
            cs = pl.ds(c0, cn)
            dact = _nt(dm, wffo[cs, :])
            gfv = gf_ref[:, cs].astype(F32)
            ufv = uf_ref[:, cs].astype(F32)
            sg = _sigmoid(gfv)
            silu = gfv * sg
            gacc[cs, :] += _tn((silu * ufv).astype(_MXU), dm)
            dff_ref[:, cs] = (dact * ufv * (sg * (1.0 + gfv * (1.0 - sg)))).astype(_MXU)
            dff_ref[:, pl.ds(FF + c0, cn)] = (dact * silu).astype(_MXU)

        @pl.when(i == nt - 1)
        def _():
            pltpu.sync_copy(gacc, gwo_ref)

    tspec = lambda w: pl.BlockSpec((tm, w), lambda i: (i, 0))
    return pl.pallas_call(
        body, name="ffn_bwd_hidden", grid=(nt,),
        out_shape=(jax.ShapeDtypeStruct((s, 2 * FF), _MXU), jax.ShapeDtypeStruct((FF, D), F32)),
        in_specs=[tspec(D), tspec(FF), tspec(FF), pl.BlockSpec(memory_space=pl.ANY)],
        out_specs=(tspec(2 * FF), pl.BlockSpec(memory_space=pl.ANY)),
        scratch_shapes=[pltpu.VMEM((FF, D), _MXU), pltpu.VMEM((FF, D), F32), pltpu.SemaphoreType.DMA((NDEV,))],
        compiler_params=_cparams(1),
    )(dh2, gf, uf, gw)


def _proj_norm_bwd(dy, x, dres, g, gw, slab, width, name, lhs=None):
    s = x.shape[0]
    tm = min(TM, s)
    nl = 0 if lhs is None else 1

    def body(*refs):
        dy_ref, x_ref, dr_ref, g_ref = refs[:4]
        l_refs = refs[4:4 + nl]
        gw_ref, dx_ref, dg_ref = refs[4 + nl:7 + nl]
        gl_refs = refs[7 + nl:7 + 2 * nl]
        wT, sems = refs[7 + 2 * nl:]

        @pl.when(pl.program_id(0) == 0)
        def _():
            _load_weights(gw_ref, [(slab, wT, D)], sems)
            dg_ref[...] = jnp.zeros_like(dg_ref)
            for ref in gl_refs:
                ref[...] = jnp.zeros_like(ref)

        dv = _nn(dy_ref[...], wT[...])
        xv = x_ref[...]
        inv = lax.rsqrt(jnp.mean(xv * xv, axis=-1, keepdims=True) + EPS)
        dx, dg = _rms_bwd(dv, xv * inv, inv, g_ref[...])
        dg_ref[...] += dg
        dr = dr_ref[...]
        dx_ref[...] = dr + dx
        for l_ref, gl_ref in zip(l_refs, gl_refs):
            gl_ref[...] += _tn(l_ref[...], dr.astype(_MXU))

    tspec = lambda w: pl.BlockSpec((tm, w), lambda i: (i, 0))
    vec = pl.BlockSpec((1, D), lambda i: (0, 0))
    mat = pl.BlockSpec((D, D), lambda i: (0, 0))
    return pl.pallas_call(
        body, name=name, grid=(s // tm,),
        out_shape=(jax.ShapeDtypeStruct((s, D), F32), jax.ShapeDtypeStruct((1, D), F32))
        + (jax.ShapeDtypeStruct((D, D), F32),) * nl,
        in_specs=[tspec(width), tspec(D), tspec(D), vec] + [tspec(D)] * nl + [pl.BlockSpec(memory_space=pl.ANY)],
        out_specs=(tspec(D), vec) + (mat,) * nl,
        scratch_shapes=[pltpu.VMEM((width, D), _MXU), pltpu.SemaphoreType.DMA((NDEV,))],
        compiler_params=_cparams(1),
    )(dy, x, dres, g, *([] if lhs is None else [lhs]), gw)


def _mixer_bwd(dh1, z, h, y_pool, y_lru, saved, gw, small):
    s = dh1.shape[0]
    tm = min(TM_SEQ, s)
    nt = s // tm
    (pool_w, pool_scale, conv_w, conv_b, w_rg, b_rg, w_ig, b_ig, lam, b_gate) = small

    def body(dh1_ref, z_ref, zp_ref, h_ref, hp_ref, yp_ref, yr_ref, a_ref, r_ref, ig_ref, xc_ref, gw_ref,
             pw_ref, ps_ref, cw_ref, cb_ref, wrg_ref, brg_ref, wig_ref, big_ref, lam_ref, bg_ref,
             dz_ref, dyr_ref, dyp_ref, mx_ref,
             gbg_ref, glam_ref, gbrg_ref, gbig_ref, gcb_ref, gcw_ref, gps_ref, gpw_ref, gwrg_ref, gwig_ref,
             pprojT, lru_w, wout_w, pbuf, lbuf, hbuf, qbuf, xbuf, a_s, g_s, dh_s, hk, pk, dcar, sems):
        step = pl.program_id(0)
        i = nt - 1 - step
        t0 = i * tm

        @pl.when(step == 0)
        def _():
            _load_weights(gw_ref, [("pproj", pprojT, PW), ("lru", lru_w, D), ("wout", wout_w, D)], sems)
            for ref in (gbg_ref, glam_ref, gbrg_ref, gbig_ref, gcb_ref, gcw_ref, gps_ref, gpw_ref, gwrg_ref, gwig_ref):
                ref[...] = jnp.zeros_like(ref)
            qbuf[pl.ds(tm, HALO), :] = jnp.zeros((HALO, PW), F32)
            xbuf[pl.ds(tm, 8), :] = jnp.zeros((8, D), F32)
            dcar[...] = jnp.zeros_like(dcar)

        first = i == 0
        zprev = jnp.where(first, 0.0, zp_ref[...])
        hprev = jnp.where(first, 0.0, hp_ref[...])

        d_merged = _nt(dh1_ref[...].astype(_MXU), wout_w[...])

        g0 = _sigmoid(z_ref[:, pl.ds(PW + 2 * D, D)] + bg_ref[pl.ds(0, 1), :])
        g1 = _sigmoid(z_ref[:, pl.ds(PW + 3 * D, D)] + bg_ref[pl.ds(1, 1), :])
        dz0 = d_merged * yp_ref[...].astype(F32) * g0 * (1.0 - g0)
        dz1 = d_merged * yr_ref[...].astype(F32) * g1 * (1.0 - g1)
        dz_ref[:, pl.ds(PW + 2 * D, D)] = dz0.astype(_MXU)
        dz_ref[:, pl.ds(PW + 3 * D, D)] = dz1.astype(_MXU)
        gbg_ref[pl.ds(0, 1), :] += jnp.sum(dz0, axis=0, keepdims=True)
        gbg_ref[pl.ds(1, 1), :] += jnp.sum(dz1, axis=0, keepdims=True)
        d_ypool = (d_merged * g0).astype(_MXU)
        d_ylru = (d_merged * g1).astype(_MXU)
        dyp_ref[...] = d_ypool
        dyr_ref[...] = d_ylru

        d_yl = _nt(d_ylru, lru_w[...])
        gel, dgel = _gelu_and_grad(z_ref[:, pl.ds(PW + D, D)])
        dz_ref[:, pl.ds(PW + D, D)] = (d_yl * h_ref[...] * dgel).astype(_MXU)
        g_full = d_yl * gel
        lbuf[pl.ds(0, HALO), :] = zprev[:, PW:PW + D]
        lbuf[pl.ds(HALO, tm), :] = z_ref[:, pl.ds(PW, D)]
        hbuf[pl.ds(0, 8), :] = hprev
        hbuf[pl.ds(8, tm), :] = h_ref[...]
        sp, sneg = _softplus_neg(lam_ref[...])
        start = (t0 + lax.broadcasted_iota(jnp.int32, (tm, HD), 0)) == 0
        for hh in range(HEADS):
            cs = pl.ds(hh * HD, HD)
            _to_segments(a_s, hh, a_ref[:, cs], tm)
            _to_segments(g_s, hh, g_full[:, hh * HD:(hh + 1) * HD], tm)
        _segment_scan(a_s, g_s, dh_s, hk, pk, dcar, tm, reverse=True)
        for hh in range(HEADS):
            cs = pl.ds(hh * HD, HD)
            a = a_ref[:, cs]
            r = r_ref[:, cs].astype(F32)
            ig = ig_ref[:, cs].astype(F32)
            xc = xc_ref[:, cs].astype(F32)
            a2 = a * a
            one_m = 1.0 - a2
            live = jnp.logical_and(one_m > 0.0, jnp.logical_not(start))
            inv_mult = lax.rsqrt(jnp.where(live, one_m, 1.0))
            mult = jnp.where(live, one_m * inv_mult, jnp.where(start, 1.0, 0.0))
            dh = _from_segments(dh_s, hh, tm)
            d_mult = dh * ig * xc
            d_loga = dh * hbuf[pl.ds(7, tm), cs] * a - jnp.where(live, d_mult * a2 * inv_mult, 0.0)
            glam_ref[:, cs] += jnp.sum(d_loga * (LRU_C * r) * sneg[:, hh * HD:(hh + 1) * HD], axis=0, keepdims=True)
            d_rpre = d_loga * (-LRU_C * sp[:, hh * HD:(hh + 1) * HD]) * r * (1.0 - r)
            d_igpre = dh * mult * xc * ig * (1.0 - ig)
            gbrg_ref[pl.ds(hh, 1), :] += jnp.sum(d_rpre, axis=0, keepdims=True)
            gbig_ref[pl.ds(hh, 1), :] += jnp.sum(d_igpre, axis=0, keepdims=True)
            drm = d_rpre.astype(_MXU)
            dim = d_igpre.astype(_MXU)
            xcm = xc.astype(_MXU)
            gwrg_ref[hh] += _tn(xcm, drm)
            gwig_ref[hh] += _tn(xcm, dim)
            d_xc = dh * mult * ig + _nt(drm, wrg_ref[hh]) + _nt(dim, wig_ref[hh])
            gcb_ref[:, cs] += jnp.sum(d_xc, axis=0, keepdims=True)
            for k in range(CONV):
                gcw_ref[pl.ds(k, 1), cs] += jnp.sum(d_xc * lbuf[pl.ds(HALO - (CONV - 1) + k, tm), cs], axis=0,
                                                    keepdims=True)
            xbuf[pl.ds(0, tm), cs] = d_xc
        dzl = cw_ref[pl.ds(CONV - 1, 1), :] * xbuf[pl.ds(0, tm), :]
        for k in range(CONV - 1):
            dzl = dzl + cw_ref[pl.ds(k, 1), :] * xbuf[pl.ds(CONV - 1 - k, tm), :]
        dz_ref[:, pl.ds(PW, D)] = dzl.astype(_MXU)
        xbuf[pl.ds(tm, 8), :] = xbuf[pl.ds(0, 8), :]

        d_mixed = _nn(d_ypool, pprojT[...])
        pbuf[pl.ds(0, HALO), :] = zprev[:, 0:PW]
        pbuf[pl.ds(HALO, tm), :] = z_ref[:, pl.ds(0, PW)]
        pooled, mixed_pre = _pool_tile(pbuf, t0, tm, pw_ref, ps_ref)
        mp = jnp.concatenate(mixed_pre, axis=1)
        mx_ref[...] = (mp * ps_ref[...]).astype(_MXU)
        gps_ref[...] += jnp.sum(d_mixed * mp, axis=0, keepdims=True)
        d_mp = (d_mixed * ps_ref[...]).astype(_MXU)
        t = t0 + lax.broadcasted_iota(jnp.int32, (tm, GD), 0)
        d_pooled = []
        for g, w in enumerate(WINDOWS):
            dmg = d_mp[:, g * GD:(g + 1) * GD]
            gpw_ref[g] += _tn(pooled[g].astype(_MXU), dmg)
            dp = _nt(dmg, pw_ref[g])
            d_pooled.append(dp)
            qbuf[pl.ds(0, tm), pl.ds(g * GD, GD)] = dp / jnp.minimum(t + 1, w).astype(F32)
        for g, w in enumerate(WINDOWS):
            cs = pl.ds(g * GD, GD)
            acc = qbuf[pl.ds(0, tm), cs]
            for d in range(1, w):
                acc = acc + qbuf[pl.ds(d, tm), cs]
            dz_ref[:, cs] = (acc - d_pooled[g]).astype(_MXU)
        qbuf[pl.ds(tm, HALO), :] = qbuf[pl.ds(0, HALO), :]

    rev = lambda w: pl.BlockSpec((tm, w), lambda g: (nt - 1 - g, 0))
    prev = lambda rows, w: pl.BlockSpec((rows, w), lambda g: (jnp.maximum((nt - 1 - g) * (tm // rows) - 1, 0), 0))
    full = lambda a: pl.BlockSpec(a.shape, lambda g: (0,) * a.ndim)
    tok = lambda w, dt: jax.ShapeDtypeStruct((s, w), dt)
    acc_shapes = [(2, D), (1, D), (HEADS, HD), (HEADS, HD), (1, D), (CONV, D), (1, PW), (GROUPS, GD, GD),
                  (HEADS, HD, HD), (HEADS, HD, HD)]
    acc_specs = tuple(pl.BlockSpec(sh, lambda g, n=len(sh): (0,) * n) for sh in acc_shapes)
    seg_buf = pltpu.VMEM((HEADS, 8 * _seg_layout(tm)[1], HD), F32)
    a_in, r_in, ig_in, xc_in = saved
    return pl.pallas_call(
        body, name="mixer_bwd", grid=(nt,),
        out_shape=(tok(NIN, _MXU), tok(D, _MXU), tok(D, _MXU), tok(PW, _MXU))
        + tuple(jax.ShapeDtypeStruct(sh, F32) for sh in acc_shapes),
        in_specs=[rev(D), rev(NIN), prev(HALO, NIN), rev(D), prev(8, D), rev(D), rev(D), rev(D), rev(D), rev(D), rev(D),
                  pl.BlockSpec(memory_space=pl.ANY)] + [full(a) for a in small],
        out_specs=(rev(NIN), rev(D), rev(D), rev(PW)) + acc_specs,
        scratch_shapes=[pltpu.VMEM((D, PW), _MXU), pltpu.VMEM((D, D), _MXU), pltpu.VMEM((D, D), _MXU),
                        pltpu.VMEM((tm + HALO, PW), F32), pltpu.VMEM((tm + HALO, D), F32),
                        pltpu.VMEM((tm + 8, D), F32), pltpu.VMEM((tm + HALO, PW), F32), pltpu.VMEM((tm + 8, D), F32),
                        seg_buf, seg_buf, seg_buf, pltpu.VMEM((HEADS, tm, HD), F32), pltpu.VMEM((HEADS, tm, HD), F32),
                        pltpu.VMEM((8, D), F32), pltpu.SemaphoreType.DMA((3 * NDEV,))],
        compiler_params=_cparams(1),
    )(dh1, z, z, h, h, y_pool, y_lru, a_in, r_in, ig_in, xc_in, gw, *small)


def _split3(a):
    hi = a.astype(jnp.bfloat16).astype(F32)
    mid = (a - hi).astype(jnp.bfloat16).astype(F32)
    lo = (a - hi - mid).astype(jnp.bfloat16).astype(F32)
    return jnp.stack([hi, mid, lo])


def _small_pack(parts):
    flat = jnp.concatenate([a.reshape(-1) for a in parts])
    return jnp.pad(flat, (0, NDEV * SMALL_ROWS * D - flat.shape[0])).reshape(NDEV * SMALL_ROWS, D)


def _small_unpack(packed, shapes):
    flat = packed.reshape(-1)
    out, o = [], 0
    for sh in shapes:
        n = math.prod(sh)
        out.append(flat[o:o + n].reshape(sh))
        o += n
    return out


def kernel(x, p, norm1_g, w_in, b_gate, pool_w, pool_scale, pool_proj, conv_w, conv_b, w_rg, b_rg, w_ig, b_ig, lru_lambda, lru_proj, w_out, norm2_g, w_ffn_in, w_ffn_out, ple_norm_g, w_ple_gate, w_ple_proj, final_g, loss_target, m_norm1_g, m_w_in, m_b_gate, m_pool_w, m_pool_scale, m_pool_proj, m_conv_w, m_conv_b, m_w_rg, m_b_rg, m_w_ig, m_b_ig, m_lru_lambda, m_lru_proj, m_w_out, m_norm2_g, m_w_ffn_in, m_w_ffn_out, m_ple_norm_g, m_w_ple_gate, m_w_ple_proj, m_final_g, v_norm1_g, v_w_in, v_b_gate, v_pool_w, v_pool_scale, v_pool_proj, v_conv_w, v_conv_b, v_w_rg, v_b_rg, v_w_ig, v_b_ig, v_lru_lambda, v_lru_proj, v_w_out, v_norm2_g, v_w_ffn_in, v_w_ffn_out, v_ple_norm_g, v_w_ple_gate, v_w_ple_proj, v_final_g):
    axes = ("x", "y", "c")
    me = 4 * lax.axis_index("x") + 2 * lax.axis_index("y") + lax.axis_index("c")
    x2 = x[0]
    p2 = p[0, 0]
    tgt = loss_target[0]

    n_small = (CONV + 2) * 128
    small_terms = _split3(jnp.concatenate([conv_w[0].reshape(-1), b_gate[0].reshape(-1)]))
    small_rows = jnp.pad(small_terms, ((0, 16 - 3), (0, D - n_small)))
    own_first = jnp.concatenate([w_in[0].T.astype(_MXU), small_rows.astype(_MXU)], axis=0)
    own_second = jnp.concatenate([
        jnp.pad(pool_proj[0].T, ((0, 0), (0, D - PW))).astype(_MXU), lru_proj[0].astype(_MXU), w_out[0].astype(_MXU),
    ], axis=0)
    own_third = jnp.concatenate([
        w_ffn_in[0].T.astype(_MXU), jnp.pad(w_ple_proj[0].T, ((0, 0), (0, D - PLE))).astype(_MXU),
        w_ffn_out[0].astype(_MXU), w_ple_gate[0].astype(_MXU),
    ], axis=0)
    u, z, gw_first, gw = _inproj_fwd(x2, norm1_g, own_first, own_second)
    off = W_OFF["f32s"][0]
    st = gw_first[:, off:off + 3, :n_small].astype(F32)
    sf = st[:, 0] + st[:, 1] + st[:, 2]
    conv_w_full = sf[:, :CONV * 128].reshape(NDEV, CONV, 128).transpose(1, 0, 2).reshape(CONV, D)
    b_gate_full = sf[:, CONV * 128:].reshape(NDEV, 2, 128).transpose(1, 0, 2).reshape(2, D)

    small = (pool_w[0].astype(_MXU), pool_scale, conv_w_full, conv_b, w_rg[0].astype(_MXU), b_rg[0],
             w_ig[0].astype(_MXU), b_ig[0], lru_lambda, b_gate_full)

    h, yl, merged, y_pool, y_lru, h1, *saved, gw_third = _mixer_fwd(z, x2, gw, small, own_third)
    v, gf, uf, h2 = _ffn_fwd(h1, norm2_g, gw_third)

    dh2, loss_blk, g_ple_norm, g_final, part_wpg, part_ple = _ple_loss_fwd_bwd(h2, p2, tgt, ple_norm_g,
                                                                               final_g.reshape(1, D), gw_third)
    dff, part_wffo = _ffn_bwd_hidden(dh2, gf, uf, gw_third)
    dh1, g_norm2 = _proj_norm_bwd(dff, h1, dh2, norm2_g, gw_third, "wffn", 2 * FF, "ffn_bwd_in")
    (dz, d_ylru, d_ypool, mixed, g_bgate, g_lam, g_brg, g_big, g_convb, g_convw, g_pscale, g_poolw, g_wrg,
     g_wig) = _mixer_bwd(dh1, z, h, y_pool, y_lru, saved, gw, small)
    grad_x, g_norm1, part_wout = _proj_norm_bwd(dz, x2, dh1, norm1_g, gw_first, "win", NIN, "inproj_bwd", lhs=merged)

    small_shapes = [(1, D), (GROUPS, GD, GD), (1, PW), (1, D), (HEADS, HD, HD), (HEADS, HD), (HEADS, HD, HD),
                    (HEADS, HD), (1, D), (1, D), (1, D), (1, D), (2, D), (CONV, D), (1, 1)]
    small_part = _small_pack([g_norm1, g_poolw, g_pscale, g_convb, g_wrg, g_brg, g_wig, g_big, g_lam, g_norm2,
                              g_ple_norm, g_final, g_bgate, g_convw, loss_blk[0:1, 0:1]])
    riders = [_grad_matmul(yl, d_ylru, "grad_lru_proj"), part_wout, _grad_matmul(d_ypool, mixed, "grad_pool_proj")]
    rs_wffn = _grad_matmul_rs(dff, v, "grad_w_ffn_in", 704, extras=[part_wffo, part_wpg, part_ple], narrow=_MXU)
    rs_win = _grad_matmul_rs(dz, u, "grad_w_in", 576, extras=riders + [small_part], narrow=_MXU, tail=SMALL_ROWS)

    def reduced(parts, name):
        return [_sum_arrays([t_own, landed[0], landed[1], landed[2]], "rs_sum_" + name + str(n))
                for n, (t_own, landed) in enumerate(parts)]

    red_wffn, = reduced(rs_wffn, "wffn")
    red_win, red_small = reduced(rs_win, "win")
    g_w_in = red_win[:576].T
    g_w_ffn_in = red_wffn[:704].T
    g_w_ffn_out = red_wffn[704:1056]
    g_w_ple_gate = red_wffn[1056:1184]
    g_w_ple_proj = red_wffn[1184:1312, :PLE].T
    g_lru_proj, g_w_out = red_win[576:704], red_win[704:832]
    g_pool_proj = red_win[832:960, :PW].T
    small_red = _all_gather_small(red_small)
    (gs_norm1, gs_poolw, gs_pscale, gs_convb, gs_wrg, gs_brg, gs_wig, gs_big, gs_lam, gs_norm2, gs_ple_norm,
     gs_final, gs_bgate, gs_convw, loss_sum) = _small_unpack(small_red, small_shapes)
    loss = loss_sum[0, 0]
    g_b_gate = lax.dynamic_slice_in_dim(gs_bgate, me * 128, 128, axis=1)
    g_conv_w = lax.dynamic_slice_in_dim(gs_convw, me * 128, 128, axis=1)

    grads = {
        "norm1_g": gs_norm1, "w_in": g_w_in[None], "b_gate": g_b_gate[None], "pool_w": gs_poolw[None],
        "pool_scale": gs_pscale, "pool_proj": g_pool_proj[None], "conv_w": g_conv_w[None], "conv_b": gs_convb,
        "w_rg": gs_wrg[None], "b_rg": gs_brg[None], "w_ig": gs_wig[None], "b_ig": gs_big[None], "lru_lambda": gs_lam,
        "lru_proj": g_lru_proj[None], "w_out": g_w_out[None], "norm2_g": gs_norm2, "w_ffn_in": g_w_ffn_in[None],
        "w_ffn_out": g_w_ffn_out[None], "ple_norm_g": gs_ple_norm, "w_ple_gate": g_w_ple_gate[None],
        "w_ple_proj": g_w_ple_proj[None], "final_g": gs_final.reshape(D),
    }
    weights = dict(norm1_g=norm1_g, w_in=w_in, b_gate=b_gate, pool_w=pool_w, pool_scale=pool_scale, pool_proj=pool_proj,
                   conv_w=conv_w, conv_b=conv_b, w_rg=w_rg, b_rg=b_rg, w_ig=w_ig, b_ig=b_ig, lru_lambda=lru_lambda,
                   lru_proj=lru_proj, w_out=w_out, norm2_g=norm2_g, w_ffn_in=w_ffn_in, w_ffn_out=w_ffn_out,
                   ple_norm_g=ple_norm_g, w_ple_gate=w_ple_gate, w_ple_proj=w_ple_proj, final_g=final_g)
    moments_m = dict(norm1_g=m_norm1_g, w_in=m_w_in, b_gate=m_b_gate, pool_w=m_pool_w, pool_scale=m_pool_scale,
                     pool_proj=m_pool_proj, conv_w=m_conv_w, conv_b=m_conv_b, w_rg=m_w_rg, b_rg=m_b_rg, w_ig=m_w_ig,
                     b_ig=m_b_ig, lru_lambda=m_lru_lambda, lru_proj=m_lru_proj, w_out=m_w_out, norm2_g=m_norm2_g,
                     w_ffn_in=m_w_ffn_in, w_ffn_out=m_w_ffn_out, ple_norm_g=m_ple_norm_g, w_ple_gate=m_w_ple_gate,
                     w_ple_proj=m_w_ple_proj, final_g=m_final_g)
    moments_v = dict(norm1_g=v_norm1_g, w_in=v_w_in, b_gate=v_b_gate, pool_w=v_pool_w, pool_scale=v_pool_scale,
                     pool_proj=v_pool_proj, conv_w=v_conv_w, conv_b=v_conv_b, w_rg=v_w_rg, b_rg=v_b_rg, w_ig=v_w_ig,
                     b_ig=v_b_ig, lru_lambda=v_lru_lambda, lru_proj=v_lru_proj, w_out=v_w_out, norm2_g=v_norm2_g,
                     w_ffn_in=v_w_ffn_in, w_ffn_out=v_w_ffn_out, ple_norm_g=v_ple_norm_g, w_ple_gate=v_w_ple_gate,
                     w_ple_proj=v_w_ple_proj, final_g=v_final_g)
    names = list(weights)
    big = ("w_in", "w_ffn_in", "w_ffn_out", "lru_proj", "w_out", "w_ple_gate", "pool_proj", "w_ple_proj")
    slab_space = {"w_in": red_win[:576], "w_ffn_in": red_wffn[:704]}
    delta, new_m, new_v = {}, {}, {}
    for n in big:
        sh = weights[n].shape
        if n in slab_space:
            as2d = lambda a: a[0].T
            back = lambda a: a.T[None]
            g2d = slab_space[n]
        else:
            as2d = lambda a: a.reshape(sh[-2], sh[-1])
            back = lambda a: a.reshape(sh)
            g2d = as2d(grads[n])
        d_, m_, v_ = _adamw(as2d(weights[n]), g2d, as2d(moments_m[n]), as2d(moments_v[n]), "adamw_" + n)
        delta[n], new_m[n], new_v[n] = back(d_), back(m_), back(v_)
    rest = [n for n in names if n not in big]
    rest_shapes = [weights[n].shape for n in rest]
    packed = [_small_pack([src[n] for n in rest]) for src in (weights, grads, moments_m, moments_v)]
    d_, m_, v_ = _adamw(*packed, "adamw_small")
    for n, a, b_, c_ in zip(rest, _small_unpack(d_, rest_shapes), _small_unpack(m_, rest_shapes),
                            _small_unpack(v_, rest_shapes)):
        delta[n], new_m[n], new_v[n] = a, b_, c_

    return (loss, grad_x[None], *[grads[n] for n in names], *[delta[n] for n in names],
            *[new_m[n] for n in names], *[new_v[n] for n in names])
```

```python
import functools
import math

import jax
import jax.numpy as jnp
from jax import lax
from jax.experimental import pallas as pl
from jax.experimental.pallas import tpu as pltpu

F32 = jnp.float32
D = 1024
NIN = 4608
PW = 512
FF = 2816
MXU_DIM = 256
FF_CHUNKS = tuple((c0, min(4 * MXU_DIM, FF - c0)) for c0 in range(0, FF, 4 * MXU_DIM))
PLE = 256
HEADS, HD = 8, 128
GROUPS, GD = 4, 128
WINDOWS = (2, 4, 8, 16)
HALO = 16
CONV = 4
EPS = 1e-6
LRU_C = 8.0
NDEV = 8
MESH = pl.DeviceIdType.MESH

ADAM_LR, ADAM_B1, ADAM_B2, ADAM_EPS, ADAM_WD, ADAM_STEP = 0.001, 0.9, 0.999, 1e-08, 0.01, 10

_MXU = jnp.bfloat16
TM = 512
TM_SEQ = 256
VMEM_LIMIT = 56 * 1024 * 1024
VMEM_LIMIT_LARGE = 60 * 1024 * 1024
W_FIRST = (("win", 576), ("f32s", 16))
W_SECOND = (("pproj", 128), ("lru", 128), ("wout", 128))
W_THIRD = (("wffn", 704), ("ple", 128), ("wffo", 352), ("wpg", 128))
W_OFF = {}
for _slabs in (W_FIRST, W_SECOND, W_THIRD):
    _o = 0
    for _n, _r in _slabs:
        W_OFF[_n] = (_o, _r)
        _o += _r
SMALL_ROWS = 48


def _cparams(n_axes=1, vmem=VMEM_LIMIT):
    return pltpu.CompilerParams(dimension_semantics=("arbitrary",) * n_axes, vmem_limit_bytes=vmem)


def _my_pos():
    return lax.axis_index("x"), lax.axis_index("y"), lax.axis_index("c")


def _nt(a, b):
    return lax.dot_general(a, b, (((1,), (1,)), ((), ())), preferred_element_type=F32)


def _nn(a, b):
    return lax.dot_general(a, b, (((1,), (0,)), ((), ())), preferred_element_type=F32)


def _tn(a, b):
    return lax.dot_general(a, b, (((0,), (0,)), ((), ())), preferred_element_type=F32)


def _sigmoid(x):
    return 0.5 * jnp.tanh(0.5 * x) + 0.5


_GELU_K = math.sqrt(2.0 / math.pi)


def _gelu_and_grad(x):
    x2 = x * x
    inner = _GELU_K * (x + 0.044715 * x2 * x)
    t = jnp.tanh(inner)
    g = 0.5 * x * (1.0 + t)
    dg = 0.5 * (1.0 + t) + 0.5 * x * (1.0 - t * t) * _GELU_K * (1.0 + 3.0 * 0.044715 * x2)
    return g, dg


def _softplus_neg(lam):
    x = -lam
    t = jnp.exp(-jnp.abs(x))
    u = 1.0 + t
    l1p = jnp.where(u == 1.0, t, jnp.log(u) * t / (u - 1.0))
    return jnp.maximum(x, 0.0) + l1p, _sigmoid(x)


def _start_slab_loads(g_ref, name, dst_ref, sems, base, width=D):
    off, rows = W_OFF[name]
    copies = []
    for k in range(NDEV):
        if width == D:
            src = g_ref.at[k, pl.ds(off, rows), :]
        else:
            src = g_ref.at[k, pl.ds(off, rows), pl.ds(0, width)]
        cp = pltpu.make_async_copy(src, dst_ref.at[pl.ds(k * rows, rows), :], sems.at[base + k])
        cp.start()
        copies.append(cp)
    return copies


def _load_weights(g_ref, items, sems):
    copies = []
    for n, (name, dst, width) in enumerate(items):
        copies += _start_slab_loads(g_ref, name, dst, sems, n * NDEV, width)
    for cp in copies:
        cp.wait()


class _Gather:
    def __init__(self, own_ref, out_ref, stage, send_sems, recv_sems, local_sem):
        x, y, c = _my_pos()
        self.c = c
        self.me, self.sibling = (x, y, c), (x, y, 1 - c)
        self.chips = [(1 - x, y), (x, 1 - y), (1 - x, 1 - y)]
        self.own_ref, self.out_ref, self.stage = own_ref, out_ref, stage
        self.send_sems, self.recv_sems = send_sems, recv_sems
        self.mine = pltpu.make_async_copy(stage, self.slab(*self.me), local_sem)
        self.first = [self.copy(0, self.me, self.sibling, src=stage)] + [
            self.copy(1 + j, self.me, (*chip, c), src=stage) for j, chip in enumerate(self.chips)]
        self.passed = [self.copy(4 + j, (*chip, c), self.sibling) for j, chip in enumerate(self.chips)]

    def slab(self, px, py, pc):
        return self.out_ref.at[4 * px + 2 * py + pc]

    def copy(self, k, block, to, src=None):
        return pltpu.make_async_remote_copy(
            src_ref=self.slab(*block) if src is None else src, dst_ref=self.slab(*block),
            send_sem=self.send_sems.at[k], recv_sem=self.recv_sems.at[k], device_id=to, device_id_type=MESH)

    def send_mine(self, far=True):
        pltpu.sync_copy(self.own_ref, self.stage)
        self.mine.start()
        for cp in self.first[:3]:
            cp.start()
        if far:
            self.send_far()

    def send_far(self):
        self.first[3].start()

    def pass_on(self, js):
        for j in js:
            self.copy(1 + j, (*self.chips[j], self.c), self.me).wait_recv()
            self.passed[j].start()

    def wait_sibling(self):
        self.copy(0, self.sibling, self.me).wait_recv()

    def wait_passed(self, js):
        for j in js:
            self.copy(4 + j, (*self.chips[j], 1 - self.c), self.me).wait_recv()

    def finish_sends(self):
        for cp in self.first + self.passed:
            cp.wait_send()
        self.mine.wait()


def _all_gather_small(piece):
    rows = piece.shape[0]

    def body(p_ref, out_ref, send_sems, recv_sems, local_sem):
        x, y, c = _my_pos()
        me = 4 * x + 2 * y + c
        mine = pltpu.make_async_copy(p_ref, out_ref.at[pl.ds(pl.multiple_of(me * rows, 8), rows), :], local_sem)
        mine.start()
        sends = []
        peers = []
        for r in range(1, NDEV):
            px = 1 - x if (r >> 2) & 1 else x
            py = 1 - y if (r >> 1) & 1 else y
            pc = 1 - c if r & 1 else c
            peers.append((px, py, pc))
            cp = pltpu.make_async_remote_copy(
                src_ref=p_ref, dst_ref=out_ref.at[pl.ds(pl.multiple_of(me * rows, 8), rows), :],
                send_sem=send_sems.at[r - 1], recv_sem=recv_sems.at[r - 1], device_id=(px, py, pc),
                device_id_type=MESH)
            cp.start()
            sends.append(cp)
        for r, (px, py, pc) in enumerate(peers):
            them = 4 * px + 2 * py + pc
            pltpu.make_async_remote_copy(
                src_ref=p_ref, dst_ref=out_ref.at[pl.ds(pl.multiple_of(them * rows, 8), rows), :],
                send_sem=send_sems.at[r], recv_sem=recv_sems.at[r], device_id=(px, py, pc),
                device_id_type=MESH).wait_recv()
        for cp in sends:
            cp.wait_send()
        mine.wait()

    return pl.pallas_call(
        body, name="ag_small",
        out_shape=jax.ShapeDtypeStruct((NDEV * rows, piece.shape[1]), piece.dtype),
        in_specs=[pl.BlockSpec(memory_space=pltpu.VMEM)],
        out_specs=pl.BlockSpec(memory_space=pl.ANY),
        scratch_shapes=[pltpu.SemaphoreType.DMA((7,)), pltpu.SemaphoreType.DMA((7,)), pltpu.SemaphoreType.DMA],
    )(piece)


def _row_block(rows, target=512, mult=8):
    b = min(rows, target) // mult * mult
    while rows % b:
        b -= mult
    return b


def _sum_arrays(arrs, name, narrow=None, target=704):
    rows, cols = arrs[0].shape
    br = _row_block(rows, target, 16)
    n = len(arrs)

    def body(*refs):
        acc = refs[0][...].astype(F32)
        for r in refs[1:n]:
            acc = acc + r[...].astype(F32)
        refs[n][...] = acc
        if narrow is not None:
            refs[n + 1][...] = acc.astype(narrow)

    spec = pl.BlockSpec((br, cols), lambda i: (i, 0))
    shape = jax.ShapeDtypeStruct((rows, cols), F32)
    if narrow is None:
        out_shape, out_specs = shape, spec
    else:
        out_shape, out_specs = (shape, jax.ShapeDtypeStruct((rows, cols), narrow)), (spec, spec)
    return pl.pallas_call(
        body, name=name, grid=(rows // br,), out_shape=out_shape,
        in_specs=[spec] * n, out_specs=out_specs, compiler_params=_cparams(1),
    )(*arrs)


def _adamw(w, g, m, v, name):
    rows, cols = w.shape
    br = _row_block(rows, 256)

    def body(w_ref, g_ref, m_ref, v_ref, d_ref, nm_ref, nv_ref):
        g_ = g_ref[...]
        m_ = ADAM_B1 * m_ref[...] + (1.0 - ADAM_B1) * g_
        v_ = ADAM_B2 * v_ref[...] + (1.0 - ADAM_B2) * (g_ * g_)
        m_hat = m_ / (1.0 - ADAM_B1 ** ADAM_STEP)
        v_hat = v_ / (1.0 - ADAM_B2 ** ADAM_STEP)
        d_ref[...] = -ADAM_LR * (m_hat / (jnp.sqrt(v_hat) + ADAM_EPS) + ADAM_WD * w_ref[...])
        nm_ref[...] = m_
        nv_ref[...] = v_

    spec = pl.BlockSpec((br, cols), lambda i: (i, 0))
    shape = jax.ShapeDtypeStruct((rows, cols), F32)
    return pl.pallas_call(
        body, name=name, grid=(rows // br,), out_shape=(shape, shape, shape),
        in_specs=[spec] * 4, out_specs=(spec, spec, spec), compiler_params=_cparams(1),
    )(w, g, m, v)


_CHIP_FLIPS = (2, 3, 1, 0)


def _grad_matmul(lhs, rhs, name):
    s, r = lhs.shape
    k = rhs.shape[1]
    tm = min(4 * TM, s)

    def body(l_ref, r_ref, o_ref):
        @pl.when(pl.program_id(0) == 0)
        def _():
            o_ref[...] = jnp.zeros_like(o_ref)

        o_ref[:, pl.ds(0, k)] += _tn(l_ref[...].astype(_MXU), r_ref[...].astype(_MXU))

    return pl.pallas_call(
        body, name=name, grid=(s // tm,),
        out_shape=jax.ShapeDtypeStruct((r, D), F32),
        in_specs=[pl.BlockSpec((tm, r), lambda i: (i, 0)), pl.BlockSpec((tm, k), lambda i: (i, 0))],
        out_specs=pl.BlockSpec((r, D), lambda i: (0, 0)),
        compiler_params=_cparams(1),
    )(lhs, rhs)


def _grad_matmul_rs(lhs, rhs, name, rows, extras=(), narrow=None, tail=0):
    s, r8 = lhs.shape
    k = rhs.shape[1]
    cpb = 1
    nblk = 4 // cpb
    nx = len(extras)
    ers = [e.shape[0] // NDEV for e in extras]
    er = sum(ers)
    srows = rows + er
    brows = 2 * cpb * srows
    groups = [(0, srows - tail, F32 if narrow is None else narrow)] + ([(srows - tail, tail, F32)] if tail else [])
    ng = len(groups)
    resident = ((2 * brows + 2 * srows) * D * 4 + sum(2 * n * D * jnp.dtype(dt).itemsize for _, n, dt in groups)
                + 2 * cpb * rows * k * 4)
    per_token = 2 * (2 * cpb * rows * lhs.dtype.itemsize + k * rhs.dtype.itemsize)
    fitting = [t for t in (4 * TM, 2 * TM, TM)
               if s % t == 0 and 2 * t <= s and resident + t * per_token <= VMEM_LIMIT_LARGE * 9 // 10]
    tm = fitting[0] if fitting else min(TM, s)
    nt = s // tm
    mid = min(nt - 1, max(1, nt // 6))

    def flip_of(p):
        return jnp.where(p == 0, 2, jnp.where(p == 1, 3, jnp.where(p == 2, 1, 0)))

    def block_col(b):
        x, y, _ = _my_pos()
        return (2 * x + y) ^ flip_of(b)

    def body(*refs):
        l_ref, r_ref = refs[:2]
        x_refs = refs[2:2 + nx]
        rest = refs[2 + nx:]
        town_ref = rest[0]
        lici_refs = rest[1:1 + ng]
        acc, stage = rest[1 + ng:3 + ng]
        send_bufs = rest[3 + ng:3 + 2 * ng]
        dsend, drecv, isend, irecv, xsem = rest[3 + 2 * ng:]
        b = pl.program_id(0)
        i = pl.program_id(1)
        x, y, c = _my_pos()
        mine = 2 * x + y
        sibling = (x, y, 1 - c)

        def chip_at(p):
            return mine ^ _CHIP_FLIPS[p]

        def slab_rows(p, parity):
            within = 0 if cpb == 1 else (chip_at(p) & 1) * 2
            return pl.ds(pl.multiple_of((within + parity) * srows, 8), srows)

        def push(p, slot):
            return pltpu.make_async_remote_copy(
                src_ref=acc.at[slot, slab_rows(p, 1 - c), :], dst_ref=stage.at[p % 2],
                send_sem=dsend.at[p], recv_sem=drecv.at[p], device_id=sibling, device_id_type=MESH)

        def ici(p):
            ch = chip_at(p)
            return [pltpu.make_async_remote_copy(
                src_ref=send_bufs[g].at[p % 2], dst_ref=lici_refs[g].at[p], send_sem=isend.at[3 * g + p],
                recv_sem=irecv.at[3 * g + p], device_id=(ch >> 1, ch & 1, c), device_id_type=MESH) for g in range(ng)]

        def extra_loads(p, slot):
            copies = []
            within = 0 if cpb == 1 else (chip_at(p) & 1) * 2
            for parity in range(2):
                off = rows
                for n, (x_ref, e) in enumerate(zip(x_refs, ers)):
                    src = x_ref.at[pl.ds(pl.multiple_of((2 * chip_at(p) + parity) * e, 8), e), :]
                    dst = acc.at[slot, pl.ds(pl.multiple_of((within + parity) * srows + off, 8), e), :]
                    copies.append(pltpu.make_async_copy(src, dst, xsem.at[(p * 2 + parity) * nx + n]))
                    off += e
            return copies

        def combine(p, slot):
            push(p, slot).wait_recv()
            total = acc[slot, slab_rows(p, c), :] + stage[p % 2]
            if p == 3:
                stage[p % 2] = total
                pltpu.sync_copy(stage.at[p % 2], town_ref)
            else:
                if p == 2:
                    for cp in ici(0):
                        cp.wait_send()
                for g, (r0, n, dt) in enumerate(groups):
                    send_bufs[g][p % 2] = total[r0:r0 + n, :].astype(dt)
                for cp in ici(p):
                    cp.start()

        for bb in range(nblk):
            slot = bb % 2
            positions = list(range(bb * cpb, (bb + 1) * cpb))

            @pl.when(jnp.logical_and(b == bb, i == 0))
            def _(bb=bb, slot=slot, positions=positions):
                if bb >= 2:
                    for p in range((bb - 2) * cpb, (bb - 1) * cpb):
                        push(p, slot).wait_send()
                for q in range(2 * cpb):
                    acc[slot, pl.ds(q * srows, rows), :] = jnp.zeros((rows, D), F32)
                for p in positions:
                    for cp in extra_loads(p, slot):
                        cp.start()

            if bb >= 1:
                @pl.when(jnp.logical_and(b == bb, i == mid))
                def _(bb=bb):
                    for p in range((bb - 1) * cpb, bb * cpb):
                        combine(p, (bb - 1) % 2)

        res = _tn(l_ref[...].astype(_MXU), r_ref[...].astype(_MXU))
        slot_now = b % 2
        for q in range(2 * cpb):
            acc[slot_now, pl.ds(q * srows, rows), pl.ds(0, k)] += res[q * rows:(q + 1) * rows, :]

        for bb in range(nblk):
            slot = bb % 2
            positions = list(range(bb * cpb, (bb + 1) * cpb))

            @pl.when(jnp.logical_and(b == bb, i == nt - 1))
            def _(bb=bb, slot=slot, positions=positions):
                for p in positions:
                    for cp in extra_loads(p, slot):
                        cp.wait()
                for p in positions:
                    push(p, slot).start()
                if bb == nblk - 1:
                    for p in positions:
                        combine(p, slot)
                    for p in range(max(0, (nblk - 2) * cpb), 4):
                        push(p, slot).wait_send()
                    for p in range(1, 3):
                        for cp in ici(p):
                            cp.wait_send()
                    for p in range(3):
                        for cp in ici(p):
                            cp.wait_recv()

    in_specs = [pl.BlockSpec((tm, 2 * cpb * rows), lambda b, i: (i, block_col(b))),
                pl.BlockSpec((tm, k), lambda b, i: (i, 0))]
    any_spec = pl.BlockSpec(memory_space=pl.ANY)
    in_specs += [any_spec] * nx
    args = [lhs, rhs, *extras]
    outs = pl.pallas_call(
        body, name=name, grid=(nblk, nt),
        out_shape=(jax.ShapeDtypeStruct((srows, D), F32),)
        + tuple(jax.ShapeDtypeStruct((3, n, D), dt) for _, n, dt in groups),
        in_specs=in_specs, out_specs=(any_spec,) * (1 + ng),
        scratch_shapes=[pltpu.VMEM((2, brows, D), F32), pltpu.VMEM((2, srows, D), F32)]
        + [pltpu.VMEM((2, n, D), dt) for _, n, dt in groups]
        + [pltpu.SemaphoreType.DMA((4,)), pltpu.SemaphoreType.DMA((4,)), pltpu.SemaphoreType.DMA((3 * ng,)),
           pltpu.SemaphoreType.DMA((3 * ng,)), pltpu.SemaphoreType.DMA((max(1, 8 * nx),))],
        compiler_params=_cparams(2, VMEM_LIMIT_LARGE),
    )(*args)
    t_own = outs[0]
    return [(t_own[r0:r0 + n], landed) for (r0, n, _), landed in zip(groups, outs[1:])]


def _inproj_fwd(x, g1, own_first, own_second):
    s = x.shape[0]
    tm = min(2 * TM, s // 2)
    nt = s // tm
    assert nt % 2 == 0
    rows1, rows2 = own_first.shape[0], own_second.shape[0]
    wrows = W_OFF["win"][1]
    cw = 2 * wrows

    def chip_col(b):
        px, py, _ = _my_pos()
        return (2 * px + py) ^ jnp.where(b == 0, 0, jnp.where(b == 1, 2, jnp.where(b == 2, 1, 3)))

    def body(x_ref, g1_ref, own1_ref, own2_ref, u_ref, z_ref, gw1_ref, gw2_ref, w_vmem, u_buf, stage1, stage2, sems,
             usem, send1, recv1, local1, send2, recv2, local2):
        b = pl.program_id(0)
        i = pl.program_id(1)
        ga = _Gather(own1_ref, gw1_ref, stage1, send1, recv1, local1)
        gb = _Gather(own2_ref, gw2_ref, stage2, send2, recv2, local2)
        c = ga.c

        def load_chip(px, py, own_too):
            copies = []
            for pc in range(2):
                dst = w_vmem.at[pl.ds(pc * wrows, wrows), :]
                copies.append(pltpu.make_async_copy(gw1_ref.at[4 * px + 2 * py + pc, pl.ds(0, wrows), :], dst,
                                                    sems.at[pc]))
            if own_too:
                mine_dst = w_vmem.at[pl.ds(pl.multiple_of(c * wrows, 16), wrows), :]
                copies[0] = pltpu.make_async_copy(own1_ref.at[pl.ds(0, wrows), :], mine_dst, sems.at[0])
                theirs_dst = w_vmem.at[pl.ds(pl.multiple_of((1 - c) * wrows, 16), wrows), :]
                copies[1] = pltpu.make_async_copy(gw1_ref.at[4 * px + 2 * py + 1 - c, pl.ds(0, wrows), :], theirs_dst,
                                                  sems.at[1])
            for cp in copies:
                cp.start()
            for cp in copies:
                cp.wait()

        @pl.when(jnp.logical_and(b == 0, i == 0))
        def _():
            ga.send_mine(far=False)
            ga.wait_sibling()
            load_chip(ga.me[0], ga.me[1], True)

        @pl.when(jnp.logical_and(b == 0, i == nt // 4))
        def _():
            ga.send_far()
            gb.send_mine()

        @pl.when(jnp.logical_and(b == 0, i == (3 * nt) // 4))
        def _():
            ga.pass_on((0, 1))

        @pl.when(jnp.logical_and(b == 1, i == (3 * nt) // 4))
        def _():
            ga.pass_on((2,))

        for j in range(3):
            @pl.when(jnp.logical_and(b == j + 1, i == 0))
            def _(j=j):
                ga.wait_passed((j,))
                load_chip(ga.chips[j][0], ga.chips[j][1], False)

        @pl.when(jnp.logical_and(b == 2, i == nt // 2))
        def _():
            gb.pass_on((0, 1))

        @pl.when(jnp.logical_and(b == 3, i == (3 * nt) // 4))
        def _():
            gb.pass_on((2,))

        slot = i % 2

        def u_write(t, sl):
            return pltpu.make_async_copy(u_buf.at[sl], u_ref.at[pl.ds(pl.multiple_of(t * tm, tm), tm), :], usem.at[sl])

        def u_read(t, sl):
            return pltpu.make_async_copy(u_ref.at[pl.ds(pl.multiple_of(t * tm, tm), tm), :], u_buf.at[sl], usem.at[sl])

        @pl.when(b == 0)
        def _():
            @pl.when(i >= 2)
            def _():
                u_write(i - 2, slot).wait()

            xv = x_ref[...]
            inv = lax.rsqrt(jnp.mean(xv * xv, axis=-1, keepdims=True) + EPS)
            u_buf[slot] = (xv * inv * g1_ref[...]).astype(_MXU)
            u_write(i, slot).start()

            @pl.when(i == nt - 1)
            def _():
                u_write(i - 1, 1 - slot).wait()
                u_write(i, slot).wait()
                u_read(0, 0).start()

        @pl.when(b > 0)
        def _():
            u_read(i, slot).wait()

            @pl.when(jnp.logical_or(b < 3, i < nt - 1))
            def _():
                u_read((i + 1) % nt, 1 - slot).start()

        z_ref[...] = _nt(u_buf[slot], w_vmem[...])

        @pl.when(jnp.logical_and(b == 3, i == nt - 1))
        def _():
            ga.finish_sends()
            gb.wait_sibling()
            gb.wait_passed((0, 1, 2))
            gb.finish_sends()

    any_spec = pl.BlockSpec(memory_space=pl.ANY)
    dma7 = pltpu.SemaphoreType.DMA((7,))
    return pl.pallas_call(
        body, name="inproj_fwd", grid=(4, nt),
        out_shape=(jax.ShapeDtypeStruct((s, D), _MXU), jax.ShapeDtypeStruct((s, NIN), F32),
                   jax.ShapeDtypeStruct((NDEV, rows1, D), own_first.dtype),
                   jax.ShapeDtypeStruct((NDEV, rows2, D), own_second.dtype)),
        in_specs=[pl.BlockSpec((tm, D), lambda b, i: (jnp.where(b == 0, i, nt - 1), 0)),
                  pl.BlockSpec((1, D), lambda b, i: (0, 0)), any_spec, any_spec],
        out_specs=(any_spec, pl.BlockSpec((tm, cw), lambda b, i: (i, chip_col(b))), any_spec, any_spec),
        scratch_shapes=[pltpu.VMEM((cw, D), _MXU), pltpu.VMEM((2, tm, D), _MXU), pltpu.VMEM((rows1, D), own_first.dtype),
                        pltpu.VMEM((rows2, D), own_second.dtype), pltpu.SemaphoreType.DMA((2,)),
                        pltpu.SemaphoreType.DMA((2,)),
                        dma7, dma7, pltpu.SemaphoreType.DMA, dma7, dma7, pltpu.SemaphoreType.DMA],
        compiler_params=_cparams(2),
    )(x, g1, own_first, own_second)


def _pool_tile(pbuf, t0, tm, pw_ref, scale_ref):
    t = t0 + lax.broadcasted_iota(jnp.int32, (tm, GD), 0)
    pooled, mixed_pre = [], []
    for g, w in enumerate(WINDOWS):
        cs = pl.ds(g * GD, GD)
        cur = pbuf[pl.ds(HALO, tm), cs]
        acc = cur
        for d in range(1, w):
            acc = acc + pbuf[pl.ds(HALO - d, tm), cs]
        cnt = jnp.minimum(t + 1, w).astype(F32)
        pg = acc / cnt - cur
        pooled.append(pg)
        mixed_pre.append(_nn(pg.astype(_MXU), pw_ref[g]))
    return pooled, mixed_pre


def _lru_gates_head(hh, lbuf, start, tm, cw_ref, cb_ref, wrg_ref, brg_ref, wig_ref, big_ref, sp):
    cs = pl.ds(hh * HD, HD)
    xc = cb_ref[:, cs] + cw_ref[pl.ds(CONV - 1, 1), cs] * lbuf[pl.ds(HALO, tm), cs]
    for k in range(CONV - 1):
        xc = xc + cw_ref[pl.ds(k, 1), cs] * lbuf[pl.ds(HALO - (CONV - 1) + k, tm), cs]
    xcm = xc.astype(_MXU)
    r = _sigmoid(_nn(xcm, wrg_ref[hh]) + brg_ref[pl.ds(hh, 1), :])
    ig = _sigmoid(_nn(xcm, wig_ref[hh]) + big_ref[pl.ds(hh, 1), :])
    a = jnp.exp(-LRU_C * r * sp[:, hh * HD:(hh + 1) * HD])
    one_m = 1.0 - a * a
    live = jnp.logical_and(one_m > 0.0, jnp.logical_not(start))
    inv_mult = lax.rsqrt(jnp.where(live, one_m, 1.0))
    mult = jnp.where(live, one_m * inv_mult, jnp.where(start, 1.0, 0.0))
    return xc, r, ig, a, live, inv_mult, mult


def _seg_layout(tm):
    seg = tm // 8
    return seg, seg + 8


def _to_segments(dst_ref, hh, val, tm):
    seg, pitch = _seg_layout(tm)
    for s in range(8):
        dst_ref[hh, pl.ds(s * pitch, seg), :] = val[s * seg:(s + 1) * seg, :]


def _from_segments(src_ref, hh, tm):
    seg, pitch = _seg_layout(tm)
    return jnp.concatenate([src_ref[hh, pl.ds(s * pitch, seg), :] for s in range(8)], axis=0)


def _segment_scan(a_ref, b_ref, out_ref, hk, pk, carry_ref, tm, reverse):
    seg, pitch = _seg_layout(tm)
    row = lax.broadcasted_iota(jnp.int32, (8, HD), 0)
    order = range(seg - 1, -1, -1) if reverse else range(seg)
    for hh in range(HEADS):
        cs = pl.ds(hh * HD, HD)
        if reverse:
            a0 = a_ref[hh, pl.ds(0, 8, stride=pitch), :]
            a_wrap = jnp.where(row <= 6, pltpu.roll(a0, 7, 0), 1.0)
        hv = jnp.zeros((8, HD), F32)
        pv = jnp.ones((8, HD), F32)
        for k in order:
            if not reverse:
                av = a_ref[hh, pl.ds(k, 8, stride=pitch), :]
            elif k + 1 < seg:
                av = a_ref[hh, pl.ds(k + 1, 8, stride=pitch), :]
            else:
                av = a_wrap
            hv = av * hv + b_ref[hh, pl.ds(k, 8, stride=pitch), :]
            pv = av * pv
            hk[hh, pl.ds(8 * k, 8), :] = hv
            pk[hh, pl.ds(8 * k, 8), :] = pv
        for d in (1, 2, 4):
            if reverse:
                keep, sh = row < 8 - d, 8 - d
            else:
                keep, sh = row >= d, d
            hv = hv + pv * jnp.where(keep, pltpu.roll(hv, sh, 0), 0.0)
            pv = pv * jnp.where(keep, pltpu.roll(pv, sh, 0), 1.0)
        cin = carry_ref[:, cs]
        ends = hv + pv * cin
        if reverse:
            enter = jnp.where(row <= 6, pltpu.roll(ends, 7, 0), cin)
            carry_ref[:, cs] = jnp.broadcast_to((a0 * ends)[0:1, :], (8, HD))
        else:
            enter = jnp.where(row >= 1, pltpu.roll(ends, 1, 0), cin)
            carry_ref[:, cs] = jnp.broadcast_to(ends[7:8, :], (8, HD))
        for k in range(seg):
            out_ref[hh, pl.ds(k, 8, stride=pitch), :] = hk[hh, pl.ds(8 * k, 8), :] + pk[hh, pl.ds(8 * k, 8), :] * enter


def _mixer_fwd(z, x, gw, small, own_third):
    s = x.shape[0]
    tm = min(TM_SEQ, s)
    nt = s // tm
    rows3 = own_third.shape[0]
    (pool_w, pool_scale, conv_w, conv_b, w_rg, b_rg, w_ig, b_ig, lam, b_gate) = small

    def body(z_ref, x_ref, gw_ref, own3_ref, pw_ref, ps_ref, cw_ref, cb_ref, wrg_ref, brg_ref, wig_ref, big_ref, lam_ref,
             bg_ref, h_ref, yl_ref, mg_ref, yp_ref, yr_ref, h1_ref, a_ref, r_ref, ig_ref, xc_ref, gw3_ref,
             pprojT, lru_w, wout_w, pbuf, lbuf, a_s, b_s, h_s, hk, pk, hcar, sems, stage3, send3, recv3, local3):
        i = pl.program_id(0)
        t0 = i * tm
        gc = _Gather(own3_ref, gw3_ref, stage3, send3, recv3, local3)

        @pl.when(i == 0)
        def _():
            gc.send_mine()
            _load_weights(gw_ref, [("pproj", pprojT, PW), ("lru", lru_w, D), ("wout", wout_w, D)], sems)
            pbuf[pl.ds(0, HALO), :] = jnp.zeros((HALO, PW), F32)
            lbuf[pl.ds(0, HALO), :] = jnp.zeros((HALO, D), F32)
            hcar[...] = jnp.zeros_like(hcar)

        @pl.when(i == nt // 2)
        def _():
            gc.pass_on((0, 1))

        @pl.when(i == (3 * nt) // 4)
        def _():
            gc.pass_on((2,))

        pbuf[pl.ds(HALO, tm), :] = z_ref[:, pl.ds(0, PW)]
        _, mixed_pre = _pool_tile(pbuf, t0, tm, pw_ref, ps_ref)
        mixed = jnp.concatenate(mixed_pre, axis=1) * ps_ref[...]
        y_pool = _nt(mixed.astype(_MXU), pprojT[...])
        pbuf[pl.ds(0, HALO), :] = pbuf[pl.ds(tm, HALO), :]

        lbuf[pl.ds(HALO, tm), :] = z_ref[:, pl.ds(PW, D)]
        sp, _ = _softplus_neg(lam_ref[...])
        start = (t0 + lax.broadcasted_iota(jnp.int32, (tm, HD), 0)) == 0
        for hh in range(HEADS):
            xc, r, ig, a, _, _, mult = _lru_gates_head(hh, lbuf, start, tm, cw_ref, cb_ref, wrg_ref, brg_ref,
                                                       wig_ref, big_ref, sp)
            _to_segments(a_s, hh, a, tm)
            _to_segments(b_s, hh, mult * ig * xc, tm)
            cs = pl.ds(hh * HD, HD)
            a_ref[:, cs] = a
            r_ref[:, cs] = r.astype(_MXU)
            ig_ref[:, cs] = ig.astype(_MXU)
            xc_ref[:, cs] = xc.astype(_MXU)
        lbuf[pl.ds(0, HALO), :] = lbuf[pl.ds(tm, HALO), :]
        _segment_scan(a_s, b_s, h_s, hk, pk, hcar, tm, reverse=False)
        for hh in range(HEADS):
            h_ref[:, pl.ds(hh * HD, HD)] = _from_segments(h_s, hh, tm)
        gel, _ = _gelu_and_grad(z_ref[:, pl.ds(PW + D, D)])
        yl = (h_ref[...] * gel).astype(_MXU)
        yl_ref[...] = yl
        y_lru = _nn(yl, lru_w[...])

        g0 = _sigmoid(z_ref[:, pl.ds(PW + 2 * D, D)] + bg_ref[pl.ds(0, 1), :])
        g1 = _sigmoid(z_ref[:, pl.ds(PW + 3 * D, D)] + bg_ref[pl.ds(1, 1), :])
        merged = (g0 * y_pool + g1 * y_lru).astype(_MXU)
        mg_ref[...] = merged
        yp_ref[...] = y_pool.astype(_MXU)
        yr_ref[...] = y_lru.astype(_MXU)
        h1_ref[...] = x_ref[...] + _nn(merged, wout_w[...])

        @pl.when(i == nt - 1)
        def _():
            gc.wait_sibling()
            gc.wait_passed((0, 1, 2))
            gc.finish_sends()

    tok = lambda w, dt: jax.ShapeDtypeStruct((s, w), dt)
    tspec = lambda w: pl.BlockSpec((tm, w), lambda i: (i, 0))
    full = lambda a: pl.BlockSpec(a.shape, lambda i: (0,) * a.ndim)
    any_spec = pl.BlockSpec(memory_space=pl.ANY)
    seg_buf = pltpu.VMEM((HEADS, 8 * _seg_layout(tm)[1], HD), F32)
    dma7 = pltpu.SemaphoreType.DMA((7,))
    return pl.pallas_call(
        body, name="mixer_fwd", grid=(nt,),
        out_shape=(tok(D, F32), tok(D, _MXU), tok(D, _MXU), tok(D, _MXU), tok(D, _MXU), tok(D, F32),
                   tok(D, F32), tok(D, _MXU), tok(D, _MXU), tok(D, _MXU),
                   jax.ShapeDtypeStruct((NDEV, rows3, D), own_third.dtype)),
        in_specs=[tspec(NIN), tspec(D), any_spec, any_spec] + [full(a) for a in small],
        out_specs=(tspec(D),) * 10 + (any_spec,),
        scratch_shapes=[pltpu.VMEM((D, PW), _MXU), pltpu.VMEM((D, D), _MXU), pltpu.VMEM((D, D), _MXU),
                        pltpu.VMEM((tm + HALO, PW), F32), pltpu.VMEM((tm + HALO, D), F32),
                        seg_buf, seg_buf, seg_buf, pltpu.VMEM((HEADS, tm, HD), F32), pltpu.VMEM((HEADS, tm, HD), F32),
                        pltpu.VMEM((8, D), F32), pltpu.SemaphoreType.DMA((3 * NDEV,)),
                        pltpu.VMEM((rows3, D), own_third.dtype), dma7, dma7, pltpu.SemaphoreType.DMA],
        compiler_params=_cparams(1),
    )(z, x, gw, own_third, *small)


def _ffn_fwd(h1, g2, gw):
    s = h1.shape[0]
    tm = min(TM, s)

    def body(h1_ref, g2_ref, gw_ref, v_ref, gf_ref, uf_ref, h2_ref, wffnT, wffo, sems):
        @pl.when(pl.program_id(0) == 0)
        def _():
            _load_weights(gw_ref, [("wffn", wffnT, D), ("wffo", wffo, D)], sems)

        hv = h1_ref[...]
        inv = lax.rsqrt(jnp.mean(hv * hv, axis=-1, keepdims=True) + EPS)
        v = (hv * inv * g2_ref[...]).astype(_MXU)
        v_ref[...] = v
        acc = hv
        for c0, cn in FF_CHUNKS:
            cs = pl.ds(c0, cn)
            gf = _nt(v, wffnT[cs, :]).astype(_MXU)
            uf = _nt(v, wffnT[pl.ds(FF + c0, cn), :]).astype(_MXU)
            gf_ref[:, cs] = gf
            uf_ref[:, cs] = uf
            gf32 = gf.astype(F32)
            act = (gf32 * _sigmoid(gf32) * uf.astype(F32)).astype(_MXU)
            acc = acc + _nn(act, wffo[cs, :])
        h2_ref[...] = acc

    tspec = lambda w: pl.BlockSpec((tm, w), lambda i: (i, 0))
    return pl.pallas_call(
        body, name="ffn_fwd", grid=(s // tm,),
        out_shape=(jax.ShapeDtypeStruct((s, D), _MXU), jax.ShapeDtypeStruct((s, FF), _MXU),
                   jax.ShapeDtypeStruct((s, FF), _MXU), jax.ShapeDtypeStruct((s, D), F32)),
        in_specs=[tspec(D), pl.BlockSpec((1, D), lambda i: (0, 0)), pl.BlockSpec(memory_space=pl.ANY)],
        out_specs=(tspec(D), tspec(FF), tspec(FF), tspec(D)),
        scratch_shapes=[pltpu.VMEM((2 * FF, D), _MXU), pltpu.VMEM((FF, D), _MXU), pltpu.SemaphoreType.DMA((2 * NDEV,))],
        compiler_params=_cparams(1),
    )(h1, g2, gw)


def _rms_bwd(dy, xn, inv, g):
    dg = jnp.sum(dy * xn, axis=0, keepdims=True)
    dxn = dy * g
    dx = inv * (dxn - xn * jnp.mean(dxn * xn, axis=-1, keepdims=True))
    return dx, dg


def _ple_loss_fwd_bwd(h2, p, target, g3, gfin, gw):
    s = h2.shape[0]
    tm = min(TM, s)

    def body(h2_ref, p_ref, t_ref, g3_ref, gf_ref, gw_ref,
             dh2_ref, loss_ref, dg3_ref, dgf_ref, gwpg_ref, gple_ref, wpg, pleT, sems):
        i = pl.program_id(0)

        @pl.when(i == 0)
        def _():
            _load_weights(gw_ref, [("wpg", wpg, D), ("ple", pleT, PLE)], sems)
            for ref in (loss_ref, dg3_ref, dgf_ref, gwpg_ref, gple_ref):
                ref[...] = jnp.zeros_like(ref)

        hv = h2_ref[...]
        inv3 = lax.rsqrt(jnp.mean(hv * hv, axis=-1, keepdims=True) + EPS)
        xn3 = hv * inv3
        n3 = (xn3 * g3_ref[...]).astype(_MXU)
        pg = _sigmoid(_nn(n3, wpg[...]))
        pm = p_ref[...].astype(_MXU)
        e = _nt(pm, pleT[...])
        h3 = hv + pg * e
        invf = lax.rsqrt(jnp.mean(h3 * h3, axis=-1, keepdims=True) + EPS)
        xf = h3 * invf
        diff = xf * gf_ref[...] - t_ref[...]
        loss_ref[...] += jnp.sum(diff * diff) * (0.5 / D)
        dh3, dgf = _rms_bwd(diff * (1.0 / D), xf, invf, gf_ref[...])
        dgf_ref[...] += dgf
        gple_ref[:, pl.ds(0, PLE)] += _tn((dh3 * pg).astype(_MXU), pm)
        dpg = (dh3 * e * pg * (1.0 - pg)).astype(_MXU)
        gwpg_ref[...] += _tn(n3, dpg)
        dn3 = _nt(dpg, wpg[...])
        dx3, dg3 = _rms_bwd(dn3, xn3, inv3, g3_ref[...])
        dg3_ref[...] += dg3
        dh2_ref[...] = dh3 + dx3

    tspec = lambda w: pl.BlockSpec((tm, w), lambda i: (i, 0))
    vec = pl.BlockSpec((1, D), lambda i: (0, 0))
    mat = pl.BlockSpec((D, D), lambda i: (0, 0))
    return pl.pallas_call(
        body, name="ple_loss", grid=(s // tm,),
        out_shape=(jax.ShapeDtypeStruct((s, D), F32), jax.ShapeDtypeStruct((8, 128), F32),
                   jax.ShapeDtypeStruct((1, D), F32), jax.ShapeDtypeStruct((1, D), F32),
                   jax.ShapeDtypeStruct((D, D), F32), jax.ShapeDtypeStruct((D, D), F32)),
        in_specs=[tspec(D), tspec(PLE), tspec(D), vec, vec, pl.BlockSpec(memory_space=pl.ANY)],
        out_specs=(tspec(D), pl.BlockSpec((8, 128), lambda i: (0, 0)), vec, vec, mat, mat),
        scratch_shapes=[pltpu.VMEM((D, D), _MXU), pltpu.VMEM((D, PLE), _MXU), pltpu.SemaphoreType.DMA((2 * NDEV,))],
        compiler_params=_cparams(1),
    )(h2, p, target, g3, gfin, gw)


def _ffn_bwd_hidden(dh2, gf, uf, gw):
    s = dh2.shape[0]
    tm = min(TM, s)
    nt = s // tm

    def body(dh2_ref, gf_ref, uf_ref, gw_ref, dff_ref, gwo_ref, wffo, gacc, sems):
        i = pl.program_id(0)

        @pl.when(i == 0)
        def _():
            _load_weights(gw_ref, [("wffo", wffo, D)], sems)
            gacc[...] = jnp.zeros_like(gacc)

        dm = dh2_ref[...].astype(_MXU)
        dacts = [_nt(dm, wffo[pl.ds(c0, cn), :]) for c0, cn in FF_CHUNKS]
        for (c0, cn), dact in zip(FF_CHUNKS, dacts):
            cs = pl.ds(c0, cn)
            gfv = gf_ref[:, cs].astype(F32)
            ufv = uf_ref[:, cs].astype(F32)
            sg = _sigmoid(gfv)
            silu = gfv * sg
            gacc[cs, :] += _tn((silu * ufv).astype(_MXU), dm)
            dff_ref[:, cs] = (dact * ufv * (sg * (1.0 + gfv * (1.0 - sg)))).astype(_MXU)
            dff_ref[:, pl.ds(FF + c0, cn)] = (dact * silu).astype(_MXU)

        @pl.when(i == nt - 1)
        def _():
            pltpu.sync_copy(gacc, gwo_ref)

    tspec = lambda w: pl.BlockSpec((tm, w), lambda i: (i, 0))
    return pl.pallas_call(
        body, name="ffn_bwd_hidden", grid=(nt,),
        out_shape=(jax.ShapeDtypeStruct((s, 2 * FF), _MXU), jax.ShapeDtypeStruct((FF, D), F32)),
        in_specs=[tspec(D), tspec(FF), tspec(FF), pl.BlockSpec(memory_space=pl.ANY)],
        out_specs=(tspec(2 * FF), pl.BlockSpec(memory_space=pl.ANY)),
        scratch_shapes=[pltpu.VMEM((FF, D), _MXU), pltpu.VMEM((FF, D), F32), pltpu.SemaphoreType.DMA((NDEV,))],
        compiler_params=_cparams(1),
    )(dh2, gf, uf, gw)


def _proj_norm_bwd(dy, x, dres, g, gw, slab, width, name, lhs=None):
    s = x.shape[0]
    tm = min(TM, s)
    nl = 0 if lhs is None else 1

    def body(*refs):
        dy_ref, x_ref, dr_ref, g_ref = refs[:4]
        l_refs = refs[4:4 + nl]
        gw_ref, dx_ref, dg_ref = refs[4 + nl:7 + nl]
        gl_refs = refs[7 + nl:7 + 2 * nl]
        wT, sems = refs[7 + 2 * nl:]

        @pl.when(pl.program_id(0) == 0)
        def _():
            _load_weights(gw_ref, [(slab, wT, D)], sems)
            dg_ref[...] = jnp.zeros_like(dg_ref)
            for ref in gl_refs:
                ref[...] = jnp.zeros_like(ref)

        dv = _nn(dy_ref[...], wT[...])
        xv = x_ref[...]
        inv = lax.rsqrt(jnp.mean(xv * xv, axis=-1, keepdims=True) + EPS)
        dx, dg = _rms_bwd(dv, xv * inv, inv, g_ref[...])
        dg_ref[...] += dg
        dr = dr_ref[...]
        dx_ref[...] = dr + dx
        for l_ref, gl_ref in zip(l_refs, gl_refs):
            gl_ref[...] += _tn(l_ref[...], dr.astype(_MXU))

    tspec = lambda w: pl.BlockSpec((tm, w), lambda i: (i, 0))
    vec = pl.BlockSpec((1, D), lambda i: (0, 0))
    mat = pl.BlockSpec((D, D), lambda i: (0, 0))
    return pl.pallas_call(
        body, name=name, grid=(s // tm,),
        out_shape=(jax.ShapeDtypeStruct((s, D), F32), jax.ShapeDtypeStruct((1, D), F32))
        + (jax.ShapeDtypeStruct((D, D), F32),) * nl,
        in_specs=[tspec(width), tspec(D), tspec(D), vec] + [tspec(D)] * nl + [pl.BlockSpec(memory_space=pl.ANY)],
        out_specs=(tspec(D), vec) + (mat,) * nl,
        scratch_shapes=[pltpu.VMEM((width, D), _MXU), pltpu.SemaphoreType.DMA((NDEV,))],
        compiler_params=_cparams(1),
    )(dy, x, dres, g, *([] if lhs is None else [lhs]), gw)


def _mixer_bwd(dh1, z, h, y_pool, y_lru, saved, gw, small):
    s = dh1.shape[0]
    tm = min(TM_SEQ, s)
    nt = s // tm
    (pool_w, pool_scale, conv_w, conv_b, w_rg, b_rg, w_ig, b_ig, lam, b_gate) = small

    def body(dh1_ref, z_ref, zp_ref, h_ref, hp_ref, yp_ref, yr_ref, a_ref, r_ref, ig_ref, xc_ref, gw_ref,
             pw_ref, ps_ref, cw_ref, cb_ref, wrg_ref, brg_ref, wig_ref, big_ref, lam_ref, bg_ref,
             dz_ref, dyr_ref, dyp_ref, mx_ref,
             gbg_ref, glam_ref, gbrg_ref, gbig_ref, gcb_ref, gcw_ref, gps_ref, gpw_ref, gwrg_ref, gwig_ref,
             pprojT, lru_w, wout_w, pbuf, lbuf, hbuf, qbuf, xbuf, a_s, g_s, dh_s, hk, pk, dcar, sems):
        step = pl.program_id(0)
        i = nt - 1 - step
        t0 = i * tm

        @pl.when(step == 0)
        def _():
            _load_weights(gw_ref, [("pproj", pprojT, PW), ("lru", lru_w, D), ("wout", wout_w, D)], sems)
            for ref in (gbg_ref, glam_ref, gbrg_ref, gbig_ref, gcb_ref, gcw_ref, gps_ref, gpw_ref, gwrg_ref, gwig_ref):
                ref[...] = jnp.zeros_like(ref)
            qbuf[pl.ds(tm, HALO), :] = jnp.zeros((HALO, PW), F32)
            xbuf[pl.ds(tm, 8), :] = jnp.zeros((8, D), F32)
            dcar[...] = jnp.zeros_like(dcar)

        first = i == 0
        zprev = jnp.where(first, 0.0, zp_ref[...])
        hprev = jnp.where(first, 0.0, hp_ref[...])

        d_merged = _nt(dh1_ref[...].astype(_MXU), wout_w[...])

        g0 = _sigmoid(z_ref[:, pl.ds(PW + 2 * D, D)] + bg_ref[pl.ds(0, 1), :])
        g1 = _sigmoid(z_ref[:, pl.ds(PW + 3 * D, D)] + bg_ref[pl.ds(1, 1), :])
        dz0 = d_merged * yp_ref[...].astype(F32) * g0 * (1.0 - g0)
        dz1 = d_merged * yr_ref[...].astype(F32) * g1 * (1.0 - g1)
        dz_ref[:, pl.ds(PW + 2 * D, D)] = dz0.astype(_MXU)
        dz_ref[:, pl.ds(PW + 3 * D, D)] = dz1.astype(_MXU)
        gbg_ref[pl.ds(0, 1), :] += jnp.sum(dz0, axis=0, keepdims=True)
        gbg_ref[pl.ds(1, 1), :] += jnp.sum(dz1, axis=0, keepdims=True)
        d_ypool = (d_merged * g0).astype(_MXU)
        d_ylru = (d_merged * g1).astype(_MXU)
        dyp_ref[...] = d_ypool
        dyr_ref[...] = d_ylru

        d_yl = _nt(d_ylru, lru_w[...])
        gel, dgel = _gelu_and_grad(z_ref[:, pl.ds(PW + D, D)])
        dz_ref[:, pl.ds(PW + D, D)] = (d_yl * h_ref[...] * dgel).astype(_MXU)
        g_full = d_yl * gel
        lbuf[pl.ds(0, HALO), :] = zprev[:, PW:PW + D]
        lbuf[pl.ds(HALO, tm), :] = z_ref[:, pl.ds(PW, D)]
        hbuf[pl.ds(0, 8), :] = hprev
        hbuf[pl.ds(8, tm), :] = h_ref[...]
        sp, sneg = _softplus_neg(lam_ref[...])
        start = (t0 + lax.broadcasted_iota(jnp.int32, (tm, HD), 0)) == 0
        for hh in range(HEADS):
            cs = pl.ds(hh * HD, HD)
            _to_segments(a_s, hh, a_ref[:, cs], tm)
            _to_segments(g_s, hh, g_full[:, hh * HD:(hh + 1) * HD], tm)
        _segment_scan(a_s, g_s, dh_s, hk, pk, dcar, tm, reverse=True)
        for hh in range(HEADS):
            cs = pl.ds(hh * HD, HD)
            a = a_ref[:, cs]
            r = r_ref[:, cs].astype(F32)
            ig = ig_ref[:, cs].astype(F32)
            xc = xc_ref[:, cs].astype(F32)
            a2 = a * a
            one_m = 1.0 - a2
            live = jnp.logical_and(one_m > 0.0, jnp.logical_not(start))
            inv_mult = lax.rsqrt(jnp.where(live, one_m, 1.0))
            mult = jnp.where(live, one_m * inv_mult, jnp.where(start, 1.0, 0.0))
            dh = _from_segments(dh_s, hh, tm)
            d_mult = dh * ig * xc
            d_loga = dh * hbuf[pl.ds(7, tm), cs] * a - jnp.where(live, d_mult * a2 * inv_mult, 0.0)
            glam_ref[:, cs] += jnp.sum(d_loga * (LRU_C * r) * sneg[:, hh * HD:(hh + 1) * HD], axis=0, keepdims=True)
            d_rpre = d_loga * (-LRU_C * sp[:, hh * HD:(hh + 1) * HD]) * r * (1.0 - r)
            d_igpre = dh * mult * xc * ig * (1.0 - ig)
            gbrg_ref[pl.ds(hh, 1), :] += jnp.sum(d_rpre, axis=0, keepdims=True)
            gbig_ref[pl.ds(hh, 1), :] += jnp.sum(d_igpre, axis=0, keepdims=True)
            drm = d_rpre.astype(_MXU)
            dim = d_igpre.astype(_MXU)
            xcm = xc.astype(_MXU)
            gwrg_ref[hh] += _tn(xcm, drm)
            gwig_ref[hh] += _tn(xcm, dim)
            d_xc = dh * mult * ig + _nt(drm, wrg_ref[hh]) + _nt(dim, wig_ref[hh])
            gcb_ref[:, cs] += jnp.sum(d_xc, axis=0, keepdims=True)
            for k in range(CONV):
                gcw_ref[pl.ds(k, 1), cs] += jnp.sum(d_xc * lbuf[pl.ds(HALO - (CONV - 1) + k, tm), cs], axis=0,
                                                    keepdims=True)
            xbuf[pl.ds(0, tm), cs] = d_xc
        dzl = cw_ref[pl.ds(CONV - 1, 1), :] * xbuf[pl.ds(0, tm), :]
        for k in range(CONV - 1):
            dzl = dzl + cw_ref[pl.ds(k, 1), :] * xbuf[pl.ds(CONV - 1 - k, tm), :]
        dz_ref[:, pl.ds(PW, D)] = dzl.astype(_MXU)
        xbuf[pl.ds(tm, 8), :] = xbuf[pl.ds(0, 8), :]

        d_mixed = _nn(d_ypool, pprojT[...])
        pbuf[pl.ds(0, HALO), :] = zprev[:, 0:PW]
        pbuf[pl.ds(HALO, tm), :] = z_ref[:, pl.ds(0, PW)]
        pooled, mixed_pre = _pool_tile(pbuf, t0, tm, pw_ref, ps_ref)
        mp = jnp.concatenate(mixed_pre, axis=1)
        mx_ref[...] = (mp * ps_ref[...]).astype(_MXU)
        gps_ref[...] += jnp.sum(d_mixed * mp, axis=0, keepdims=True)
        d_mp = (d_mixed * ps_ref[...]).astype(_MXU)
        t = t0 + lax.broadcasted_iota(jnp.int32, (tm, GD), 0)
        d_pooled = []
        for g, w in enumerate(WINDOWS):
            dmg = d_mp[:, g * GD:(g + 1) * GD]
            gpw_ref[g] += _tn(pooled[g].astype(_MXU), dmg)
            dp = _nt(dmg, pw_ref[g])
            d_pooled.append(dp)
            qbuf[pl.ds(0, tm), pl.ds(g * GD, GD)] = dp / jnp.minimum(t + 1, w).astype(F32)
        for g, w in enumerate(WINDOWS):
            cs = pl.ds(g * GD, GD)
            acc = qbuf[pl.ds(0, tm), cs]
            for d in range(1, w):
                acc = acc + qbuf[pl.ds(d, tm), cs]
            dz_ref[:, cs] = (acc - d_pooled[g]).astype(_MXU)
        qbuf[pl.ds(tm, HALO), :] = qbuf[pl.ds(0, HALO), :]

    rev = lambda w: pl.BlockSpec((tm, w), lambda g: (nt - 1 - g, 0))
    prev = lambda rows, w: pl.BlockSpec((rows, w), lambda g: (jnp.maximum((nt - 1 - g) * (tm // rows) - 1, 0), 0))
    full = lambda a: pl.BlockSpec(a.shape, lambda g: (0,) * a.ndim)
    tok = lambda w, dt: jax.ShapeDtypeStruct((s, w), dt)
    acc_shapes = [(2, D), (1, D), (HEADS, HD), (HEADS, HD), (1, D), (CONV, D), (1, PW), (GROUPS, GD, GD),
                  (HEADS, HD, HD), (HEADS, HD, HD)]
    acc_specs = tuple(pl.BlockSpec(sh, lambda g, n=len(sh): (0,) * n) for sh in acc_shapes)
    seg_buf = pltpu.VMEM((HEADS, 8 * _seg_layout(tm)[1], HD), F32)
    a_in, r_in, ig_in, xc_in = saved
    return pl.pallas_call(
        body, name="mixer_bwd", grid=(nt,),
        out_shape=(tok(NIN, _MXU), tok(D, _MXU), tok(D, _MXU), tok(PW, _MXU))
        + tuple(jax.ShapeDtypeStruct(sh, F32) for sh in acc_shapes),
        in_specs=[rev(D), rev(NIN), prev(HALO, NIN), rev(D), prev(8, D), rev(D), rev(D), rev(D), rev(D), rev(D), rev(D),
                  pl.BlockSpec(memory_space=pl.ANY)] + [full(a) for a in small],
        out_specs=(rev(NIN), rev(D), rev(D), rev(PW)) + acc_specs,
        scratch_shapes=[pltpu.VMEM((D, PW), _MXU), pltpu.VMEM((D, D), _MXU), pltpu.VMEM((D, D), _MXU),
                        pltpu.VMEM((tm + HALO, PW), F32), pltpu.VMEM((tm + HALO, D), F32),
                        pltpu.VMEM((tm + 8, D), F32), pltpu.VMEM((tm + HALO, PW), F32), pltpu.VMEM((tm + 8, D), F32),
                        seg_buf, seg_buf, seg_buf, pltpu.VMEM((HEADS, tm, HD), F32), pltpu.VMEM((HEADS, tm, HD), F32),
                        pltpu.VMEM((8, D), F32), pltpu.SemaphoreType.DMA((3 * NDEV,))],
        compiler_params=_cparams(1),
    )(dh1, z, z, h, h, y_pool, y_lru, a_in, r_in, ig_in, xc_in, gw, *small)


def _split3(a):
    hi = a.astype(jnp.bfloat16).astype(F32)
    mid = (a - hi).astype(jnp.bfloat16).astype(F32)
    lo = (a - hi - mid).astype(jnp.bfloat16).astype(F32)
    return jnp.stack([hi, mid, lo])


def _small_pack(parts):
    flat = jnp.concatenate([a.reshape(-1) for a in parts])
    return jnp.pad(flat, (0, NDEV * SMALL_ROWS * D - flat.shape[0])).reshape(NDEV * SMALL_ROWS, D)


def _small_unpack(packed, shapes):
    flat = packed.reshape(-1)
    out, o = [], 0
    for sh in shapes:
        n = math.prod(sh)
        out.append(flat[o:o + n].reshape(sh))
        o += n
    return out


def kernel(x, p, norm1_g, w_in, b_gate, pool_w, pool_scale, pool_proj, conv_w, conv_b, w_rg, b_rg, w_ig, b_ig, lru_lambda, lru_proj, w_out, norm2_g, w_ffn_in, w_ffn_out, ple_norm_g, w_ple_gate, w_ple_proj, final_g, loss_target, m_norm1_g, m_w_in, m_b_gate, m_pool_w, m_pool_scale, m_pool_proj, m_conv_w, m_conv_b, m_w_rg, m_b_rg, m_w_ig, m_b_ig, m_lru_lambda, m_lru_proj, m_w_out, m_norm2_g, m_w_ffn_in, m_w_ffn_out, m_ple_norm_g, m_w_ple_gate, m_w_ple_proj, m_final_g, v_norm1_g, v_w_in, v_b_gate, v_pool_w, v_pool_scale, v_pool_proj, v_conv_w, v_conv_b, v_w_rg, v_b_rg, v_w_ig, v_b_ig, v_lru_lambda, v_lru_proj, v_w_out, v_norm2_g, v_w_ffn_in, v_w_ffn_out, v_ple_norm_g, v_w_ple_gate, v_w_ple_proj, v_final_g):
    axes = ("x", "y", "c")
    me = 4 * lax.axis_index("x") + 2 * lax.axis_index("y") + lax.axis_index("c")
    x2 = x[0]
    p2 = p[0, 0]
    tgt = loss_target[0]

    n_small = (CONV + 2) * 128
    small_terms = _split3(jnp.concatenate([conv_w[0].reshape(-1), b_gate[0].reshape(-1)]))
    small_rows = jnp.pad(small_terms, ((0, 16 - 3), (0, D - n_small)))
    own_first = jnp.concatenate([w_in[0].T.astype(_MXU), small_rows.astype(_MXU)], axis=0)
    own_second = jnp.concatenate([
        jnp.pad(pool_proj[0].T, ((0, 0), (0, D - PW))).astype(_MXU), lru_proj[0].astype(_MXU), w_out[0].astype(_MXU),
    ], axis=0)
    own_third = jnp.concatenate([
        w_ffn_in[0].T.astype(_MXU), jnp.pad(w_ple_proj[0].T, ((0, 0), (0, D - PLE))).astype(_MXU),
        w_ffn_out[0].astype(_MXU), w_ple_gate[0].astype(_MXU),
    ], axis=0)
    u, z, gw_first, gw = _inproj_fwd(x2, norm1_g, own_first, own_second)
    off = W_OFF["f32s"][0]
    st = gw_first[:, off:off + 3, :n_small].astype(F32)
    sf = st[:, 0] + st[:, 1] + st[:, 2]
    conv_w_full = sf[:, :CONV * 128].reshape(NDEV, CONV, 128).transpose(1, 0, 2).reshape(CONV, D)
    b_gate_full = sf[:, CONV * 128:].reshape(NDEV, 2, 128).transpose(1, 0, 2).reshape(2, D)

    small = (pool_w[0].astype(_MXU), pool_scale, conv_w_full, conv_b, w_rg[0].astype(_MXU), b_rg[0],
             w_ig[0].astype(_MXU), b_ig[0], lru_lambda, b_gate_full)

    h, yl, merged, y_pool, y_lru, h1, *saved, gw_third = _mixer_fwd(z, x2, gw, small, own_third)
    v, gf, uf, h2 = _ffn_fwd(h1, norm2_g, gw_third)

    dh2, loss_blk, g_ple_norm, g_final, part_wpg, part_ple = _ple_loss_fwd_bwd(h2, p2, tgt, ple_norm_g,
                                                                               final_g.reshape(1, D), gw_third)
    dff, part_wffo = _ffn_bwd_hidden(dh2, gf, uf, gw_third)
    dh1, g_norm2 = _proj_norm_bwd(dff, h1, dh2, norm2_g, gw_third, "wffn", 2 * FF, "ffn_bwd_in")
    (dz, d_ylru, d_ypool, mixed, g_bgate, g_lam, g_brg, g_big, g_convb, g_convw, g_pscale, g_poolw, g_wrg,
     g_wig) = _mixer_bwd(dh1, z, h, y_pool, y_lru, saved, gw, small)
    grad_x, g_norm1, part_wout = _proj_norm_bwd(dz, x2, dh1, norm1_g, gw_first, "win", NIN, "inproj_bwd", lhs=merged)

    small_shapes = [(1, D), (GROUPS, GD, GD), (1, PW), (1, D), (HEADS, HD, HD), (HEADS, HD), (HEADS, HD, HD),
                    (HEADS, HD), (1, D), (1, D), (1, D), (1, D), (2, D), (CONV, D), (1, 1)]
    small_part = _small_pack([g_norm1, g_poolw, g_pscale, g_convb, g_wrg, g_brg, g_wig, g_big, g_lam, g_norm2,
                              g_ple_norm, g_final, g_bgate, g_convw, loss_blk[0:1, 0:1]])
    riders = [_grad_matmul(yl, d_ylru, "grad_lru_proj"), part_wout, _grad_matmul(d_ypool, mixed, "grad_pool_proj")]
    rs_wffn = _grad_matmul_rs(dff, v, "grad_w_ffn_in", 704, extras=[part_wffo, part_wpg, part_ple], narrow=_MXU)
    rs_win = _grad_matmul_rs(dz, u, "grad_w_in", 576, extras=riders + [small_part], narrow=_MXU, tail=SMALL_ROWS)

    def reduced(parts, name):
        return [_sum_arrays([t_own, landed[0], landed[1], landed[2]], "rs_sum_" + name + str(n))
                for n, (t_own, landed) in enumerate(parts)]

    red_wffn, = reduced(rs_wffn, "wffn")
    red_win, red_small = reduced(rs_win, "win")
    g_w_in = red_win[:576].T
    g_w_ffn_in = red_wffn[:704].T
    g_w_ffn_out = red_wffn[704:1056]
    g_w_ple_gate = red_wffn[1056:1184]
    g_w_ple_proj = red_wffn[1184:1312, :PLE].T
    g_lru_proj, g_w_out = red_win[576:704], red_win[704:832]
    g_pool_proj = red_win[832:960, :PW].T
    small_red = _all_gather_small(red_small)
    (gs_norm1, gs_poolw, gs_pscale, gs_convb, gs_wrg, gs_brg, gs_wig, gs_big, gs_lam, gs_norm2, gs_ple_norm,
     gs_final, gs_bgate, gs_convw, loss_sum) = _small_unpack(small_red, small_shapes)
    loss = loss_sum[0, 0]
    g_b_gate = lax.dynamic_slice_in_dim(gs_bgate, me * 128, 128, axis=1)
    g_conv_w = lax.dynamic_slice_in_dim(gs_convw, me * 128, 128, axis=1)

    grads = {
        "norm1_g": gs_norm1, "w_in": g_w_in[None], "b_gate": g_b_gate[None], "pool_w": gs_poolw[None],
        "pool_scale": gs_pscale, "pool_proj": g_pool_proj[None], "conv_w": g_conv_w[None], "conv_b": gs_convb,
        "w_rg": gs_wrg[None], "b_rg": gs_brg[None], "w_ig": gs_wig[None], "b_ig": gs_big[None], "lru_lambda": gs_lam,
        "lru_proj": g_lru_proj[None], "w_out": g_w_out[None], "norm2_g": gs_norm2, "w_ffn_in": g_w_ffn_in[None],
        "w_ffn_out": g_w_ffn_out[None], "ple_norm_g": gs_ple_norm, "w_ple_gate": g_w_ple_gate[None],
        "w_ple_proj": g_w_ple_proj[None], "final_g": gs_final.reshape(D),
    }
    weights = dict(norm1_g=norm1_g, w_in=w_in, b_gate=b_gate, pool_w=pool_w, pool_scale=pool_scale, pool_proj=pool_proj,
                   conv_w=conv_w, conv_b=conv_b, w_rg=w_rg, b_rg=b_rg, w_ig=w_ig, b_ig=b_ig, lru_lambda=lru_lambda,
                   lru_proj=lru_proj, w_out=w_out, norm2_g=norm2_g, w_ffn_in=w_ffn_in, w_ffn_out=w_ffn_out,
                   ple_norm_g=ple_norm_g, w_ple_gate=w_ple_gate, w_ple_proj=w_ple_proj, final_g=final_g)
    moments_m = dict(norm1_g=m_norm1_g, w_in=m_w_in, b_gate=m_b_gate, pool_w=m_pool_w, pool_scale=m_pool_scale,
                     pool_proj=m_pool_proj, conv_w=m_conv_w, conv_b=m_conv_b, w_rg=m_w_rg, b_rg=m_b_rg, w_ig=m_w_ig,
                     b_ig=m_b_ig, lru_lambda=m_lru_lambda, lru_proj=m_lru_proj, w_out=m_w_out, norm2_g=m_norm2_g,
                     w_ffn_in=m_w_ffn_in, w_ffn_out=m_w_ffn_out, ple_norm_g=m_ple_norm_g, w_ple_gate=m_w_ple_gate,
                     w_ple_proj=m_w_ple_proj, final_g=m_final_g)
    moments_v = dict(norm1_g=v_norm1_g, w_in=v_w_in, b_gate=v_b_gate, pool_w=v_pool_w, pool_scale=v_pool_scale,
                     pool_proj=v_pool_proj, conv_w=v_conv_w, conv_b=v_conv_b, w_rg=v_w_rg, b_rg=v_b_rg, w_ig=v_w_ig,
                     b_ig=v_b_ig, lru_lambda=v_lru_lambda, lru_proj=v_lru_proj, w_out=v_w_out, norm2_g=v_norm2_g,
                     w_ffn_in=v_w_ffn_in, w_ffn_out=v_w_ffn_out, ple_norm_g=v_ple_norm_g, w_ple_gate=v_w_ple_gate,
                     w_ple_proj=v_w_ple_proj, final_g=v_final_g)
    names = list(weights)
    big = ("w_in", "w_ffn_in", "w_ffn_out", "lru_proj", "w_out", "w_ple_gate", "pool_proj", "w_ple_proj")
    slab_space = {"w_in": red_win[:576], "w_ffn_in": red_wffn[:704]}
    delta, new_m, new_v = {}, {}, {}
    for n in big:
        sh = weights[n].shape
        if n in slab_space:
            as2d = lambda a: a[0].T
            back = lambda a: a.T[None]
            g2d = slab_space[n]
        else:
            as2d = lambda a: a.reshape(sh[-2], sh[-1])
            back = lambda a: a.reshape(sh)
            g2d = as2d(grads[n])
        d_, m_, v_ = _adamw(as2d(weights[n]), g2d, as2d(moments_m[n]), as2d(moments_v[n]), "adamw_" + n)
        delta[n], new_m[n], new_v[n] = back(d_), back(m_), back(v_)
    rest = [n for n in names if n not in big]
    rest_shapes = [weights[n].shape for n in rest]
    packed = [_small_pack([src[n] for n in rest]) for src in (weights, grads, moments_m, moments_v)]
    d_, m_, v_ = _adamw(*packed, "adamw_small")
    for n, a, b_, c_ in zip(rest, _small_unpack(d_, rest_shapes), _small_unpack(m_, rest_shapes),
                            _small_unpack(v_, rest_shapes)):
        delta[n], new_m[n], new_v[n] = a, b_, c_

    return (loss, grad_x[None], *[grads[n] for n in names], *[delta[n] for n in names],
            *[new_m[n] for n in names], *[new_v[n] for n in names])
```

```python
import functools
import math

import jax
import jax.numpy as jnp
from jax import lax
from jax.experimental import pallas as pl
from jax.experimental.pallas import tpu as pltpu

F32 = jnp.float32
D = 1024
NIN = 4608
PW = 512
FF = 2816
MXU_DIM = 256
FF_CHUNKS = tuple((c0, min(4 * MXU_DIM, FF - c0)) for c0 in range(0, FF, 4 * MXU_DIM))
PLE = 256
HEADS, HD = 8, 128
GROUPS, GD = 4, 128
WINDOWS = (2, 4, 8, 16)
HALO = 16
CONV = 4
EPS = 1e-6
LRU_C = 8.0
NDEV = 8
MESH = pl.DeviceIdType.MESH

ADAM_LR, ADAM_B1, ADAM_B2, ADAM_EPS, ADAM_WD, ADAM_STEP = 0.001, 0.9, 0.999, 1e-08, 0.01, 10

_MXU = jnp.bfloat16
TM = 512
TM_SEQ = 256
VMEM_LIMIT = 56 * 1024 * 1024
VMEM_LIMIT_LARGE = 60 * 1024 * 1024
W_FIRST = (("win", 576), ("f32s", 16))
W_SECOND = (("pproj", 128), ("lru", 128), ("wout", 128))
W_THIRD = (("wffn", 704), ("ple", 128), ("wffo", 352), ("wpg", 128))
W_OFF = {}
for _slabs in (W_FIRST, W_SECOND, W_THIRD):
    _o = 0
    for _n, _r in _slabs:
        W_OFF[_n] = (_o, _r)
        _o += _r
SMALL_ROWS = 48


def _cparams(n_axes=1, vmem=VMEM_LIMIT):
    return pltpu.CompilerParams(dimension_semantics=("arbitrary",) * n_axes, vmem_limit_bytes=vmem)


def _my_pos():
    return lax.axis_index("x"), lax.axis_index("y"), lax.axis_index("c")


def _nt(a, b):
    return lax.dot_general(a, b, (((1,), (1,)), ((), ())), preferred_element_type=F32)


def _nn(a, b):
    return lax.dot_general(a, b, (((1,), (0,)), ((), ())), preferred_element_type=F32)


def _tn(a, b):
    return lax.dot_general(a, b, (((0,), (0,)), ((), ())), preferred_element_type=F32)


def _sigmoid(x):
    return 0.5 * jnp.tanh(0.5 * x) + 0.5


_GELU_K = math.sqrt(2.0 / math.pi)


def _gelu_and_grad(x):
    x2 = x * x
    inner = _GELU_K * (x + 0.044715 * x2 * x)
    t = jnp.tanh(inner)
    g = 0.5 * x * (1.0 + t)
    dg = 0.5 * (1.0 + t) + 0.5 * x * (1.0 - t * t) * _GELU_K * (1.0 + 3.0 * 0.044715 * x2)
    return g, dg


def _softplus_neg(lam):
    x = -lam
    t = jnp.exp(-jnp.abs(x))
    u = 1.0 + t
    l1p = jnp.where(u == 1.0, t, jnp.log(u) * t / (u - 1.0))
    return jnp.maximum(x, 0.0) + l1p, _sigmoid(x)


def _start_slab_loads(g_ref, name, dst_ref, sems, base, width=D):
    off, rows = W_OFF[name]
    copies = []
    for k in range(NDEV):
        if width == D:
            src = g_ref.at[k, pl.ds(off, rows), :]
        else:
            src = g_ref.at[k, pl.ds(off, rows), pl.ds(0, width)]
        cp = pltpu.make_async_copy(src, dst_ref.at[pl.ds(k * rows, rows), :], sems.at[base + k])
        cp.start()
        copies.append(cp)
    return copies


def _load_weights(g_ref, items, sems):
    copies = []
    for n, (name, dst, width) in enumerate(items):
        copies += _start_slab_loads(g_ref, name, dst, sems, n * NDEV, width)
    for cp in copies:
        cp.wait()


class _Gather:
    def __init__(self, own_ref, out_ref, stage, send_sems, recv_sems, local_sem):
        x, y, c = _my_pos()
        self.c = c
        self.me, self.sibling = (x, y, c), (x, y, 1 - c)
        self.chips = [(1 - x, y), (x, 1 - y), (1 - x, 1 - y)]
        self.own_ref, self.out_ref, self.stage = own_ref, out_ref, stage
        self.send_sems, self.recv_sems = send_sems, recv_sems
        self.mine = pltpu.make_async_copy(stage, self.slab(*self.me), local_sem)
        self.first = [self.copy(0, self.me, self.sibling, src=stage)] + [
            self.copy(1 + j, self.me, (*chip, c), src=stage) for j, chip in enumerate(self.chips)]
        self.passed = [self.copy(4 + j, (*chip, c), self.sibling) for j, chip in enumerate(self.chips)]

    def slab(self, px, py, pc):
        return self.out_ref.at[4 * px + 2 * py + pc]

    def copy(self, k, block, to, src=None):
        return pltpu.make_async_remote_copy(
            src_ref=self.slab(*block) if src is None else src, dst_ref=self.slab(*block),
            send_sem=self.send_sems.at[k], recv_sem=self.recv_sems.at[k], device_id=to, device_id_type=MESH)

    def send_mine(self, far=True):
        pltpu.sync_copy(self.own_ref, self.stage)
        self.mine.start()
        for cp in self.first[:3]:
            cp.start()
        if far:
            self.send_far()

    def send_far(self):
        self.first[3].start()

    def pass_on(self, js):
        for j in js:
            self.copy(1 + j, (*self.chips[j], self.c), self.me).wait_recv()
            self.passed[j].start()

    def wait_sibling(self):
        self.copy(0, self.sibling, self.me).wait_recv()

    def wait_passed(self, js):
        for j in js:
            self.copy(4 + j, (*self.chips[j], 1 - self.c), self.me).wait_recv()

    def finish_sends(self):
        for cp in self.first + self.passed:
            cp.wait_send()
        self.mine.wait()


def _all_gather_small(piece):
    rows = piece.shape[0]

    def body(p_ref, out_ref, send_sems, recv_sems, local_sem):
        x, y, c = _my_pos()
        me = 4 * x + 2 * y + c
        mine = pltpu.make_async_copy(p_ref, out_ref.at[pl.ds(pl.multiple_of(me * rows, 8), rows), :], local_sem)
        mine.start()
        sends = []
        peers = []
        for r in range(1, NDEV):
            px = 1 - x if (r >> 2) & 1 else x
            py = 1 - y if (r >> 1) & 1 else y
            pc = 1 - c if r & 1 else c
            peers.append((px, py, pc))
            cp = pltpu.make_async_remote_copy(
                src_ref=p_ref, dst_ref=out_ref.at[pl.ds(pl.multiple_of(me * rows, 8), rows), :],
                send_sem=send_sems.at[r - 1], recv_sem=recv_sems.at[r - 1], device_id=(px, py, pc),
                device_id_type=MESH)
            cp.start()
            sends.append(cp)
        for r, (px, py, pc) in enumerate(peers):
            them = 4 * px + 2 * py + pc
            pltpu.make_async_remote_copy(
                src_ref=p_ref, dst_ref=out_ref.at[pl.ds(pl.multiple_of(them * rows, 8), rows), :],
                send_sem=send_sems.at[r], recv_sem=recv_sems.at[r], device_id=(px, py, pc),
                device_id_type=MESH).wait_recv()
        for cp in sends:
            cp.wait_send()
        mine.wait()

    return pl.pallas_call(
        body, name="ag_small",
        out_shape=jax.ShapeDtypeStruct((NDEV * rows, piece.shape[1]), piece.dtype),
        in_specs=[pl.BlockSpec(memory_space=pltpu.VMEM)],
        out_specs=pl.BlockSpec(memory_space=pl.ANY),
        scratch_shapes=[pltpu.SemaphoreType.DMA((7,)), pltpu.SemaphoreType.DMA((7,)), pltpu.SemaphoreType.DMA],
    )(piece)


def _row_block(rows, target=512, mult=8):
    b = min(rows, target) // mult * mult
    while rows % b:
        b -= mult
    return b


def _sum_arrays(arrs, name, narrow=None, target=704):
    rows, cols = arrs[0].shape
    br = _row_block(rows, target, 16)
    n = len(arrs)

    def body(*refs):
        acc = refs[0][...].astype(F32)
        for r in refs[1:n]:
            acc = acc + r[...].astype(F32)
        refs[n][...] = acc
        if narrow is not None:
            refs[n + 1][...] = acc.astype(narrow)

    spec = pl.BlockSpec((br, cols), lambda i: (i, 0))
    shape = jax.ShapeDtypeStruct((rows, cols), F32)
    if narrow is None:
        out_shape, out_specs = shape, spec
    else:
        out_shape, out_specs = (shape, jax.ShapeDtypeStruct((rows, cols), narrow)), (spec, spec)
    return pl.pallas_call(
        body, name=name, grid=(rows // br,), out_shape=out_shape,
        in_specs=[spec] * n, out_specs=out_specs, compiler_params=_cparams(1),
    )(*arrs)


def _adamw(w, g, m, v, name):
    rows, cols = w.shape
    br = _row_block(rows, 256)

    def body(w_ref, g_ref, m_ref, v_ref, d_ref, nm_ref, nv_ref):
        g_ = g_ref[...]
        m_ = ADAM_B1 * m_ref[...] + (1.0 - ADAM_B1) * g_
        v_ = ADAM_B2 * v_ref[...] + (1.0 - ADAM_B2) * (g_ * g_)
        m_hat = m_ / (1.0 - ADAM_B1 ** ADAM_STEP)
        v_hat = v_ / (1.0 - ADAM_B2 ** ADAM_STEP)
        d_ref[...] = -ADAM_LR * (m_hat / (jnp.sqrt(v_hat) + ADAM_EPS) + ADAM_WD * w_ref[...])
        nm_ref[...] = m_
        nv_ref[...] = v_

    spec = pl.BlockSpec((br, cols), lambda i: (i, 0))
    shape = jax.ShapeDtypeStruct((rows, cols), F32)
    return pl.pallas_call(
        body, name=name, grid=(rows // br,), out_shape=(shape, shape, shape),
        in_specs=[spec] * 4, out_specs=(spec, spec, spec), compiler_params=_cparams(1),
    )(w, g, m, v)


_CHIP_FLIPS = (2, 3, 1, 0)


def _grad_matmul(lhs, rhs, name):
    s, r = lhs.shape
    k = rhs.shape[1]
    tm = min(8 * TM, s)

    def body(l_ref, r_ref, o_ref):
        @pl.when(pl.program_id(0) == 0)
        def _():
            o_ref[...] = jnp.zeros_like(o_ref)

        o_ref[:, pl.ds(0, k)] += _tn(l_ref[...].astype(_MXU), r_ref[...].astype(_MXU))

    return pl.pallas_call(
        body, name=name, grid=(s // tm,),
        out_shape=jax.ShapeDtypeStruct((r, D), F32),
        in_specs=[pl.BlockSpec((tm, r), lambda i: (i, 0)), pl.BlockSpec((tm, k), lambda i: (i, 0))],
        out_specs=pl.BlockSpec((r, D), lambda i: (0, 0)),
        compiler_params=_cparams(1),
    )(lhs, rhs)


def _grad_matmul_rs(lhs, rhs, name, rows, extras=(), narrow=None, tail=0):
    s, r8 = lhs.shape
    k = rhs.shape[1]
    cpb = 1
    nblk = 4 // cpb
    nx = len(extras)
    ers = [e.shape[0] // NDEV for e in extras]
    er = sum(ers)
    srows = rows + er
    brows = 2 * cpb * srows
    groups = [(0, srows - tail, F32 if narrow is None else narrow)] + ([(srows - tail, tail, F32)] if tail else [])
    ng = len(groups)
    resident = ((2 * brows + 2 * srows) * D * 4 + sum(2 * n * D * jnp.dtype(dt).itemsize for _, n, dt in groups)
                + 2 * cpb * rows * k * 4)
    per_token = 2 * (2 * cpb * rows * lhs.dtype.itemsize + k * rhs.dtype.itemsize)
    fitting = [t for t in (4 * TM, 2 * TM, TM)
               if s % t == 0 and 2 * t <= s and resident + t * per_token <= VMEM_LIMIT_LARGE * 9 // 10]
    tm = fitting[0] if fitting else min(TM, s)
    nt = s // tm
    mid = min(nt - 1, max(1, nt // 6))

    def flip_of(p):
        return jnp.where(p == 0, 2, jnp.where(p == 1, 3, jnp.where(p == 2, 1, 0)))

    def block_col(b):
        x, y, _ = _my_pos()
        return (2 * x + y) ^ flip_of(b)

    def body(*refs):
        l_ref, r_ref = refs[:2]
        x_refs = refs[2:2 + nx]
        rest = refs[2 + nx:]
        town_ref = rest[0]
        lici_refs = rest[1:1 + ng]
        acc, stage = rest[1 + ng:3 + ng]
        send_bufs = rest[3 + ng:3 + 2 * ng]
        dsend, drecv, isend, irecv, xsem = rest[3 + 2 * ng:]
        b = pl.program_id(0)
        i = pl.program_id(1)
        x, y, c = _my_pos()
        mine = 2 * x + y
        sibling = (x, y, 1 - c)

        def chip_at(p):
            return mine ^ _CHIP_FLIPS[p]

        def slab_rows(p, parity):
            within = 0 if cpb == 1 else (chip_at(p) & 1) * 2
            return pl.ds(pl.multiple_of((within + parity) * srows, 8), srows)

        def push(p, slot):
            return pltpu.make_async_remote_copy(
                src_ref=acc.at[slot, slab_rows(p, 1 - c), :], dst_ref=stage.at[p % 2],
                send_sem=dsend.at[p], recv_sem=drecv.at[p], device_id=sibling, device_id_type=MESH)

        def ici(p):
            ch = chip_at(p)
            return [pltpu.make_async_remote_copy(
                src_ref=send_bufs[g].at[p % 2], dst_ref=lici_refs[g].at[p], send_sem=isend.at[3 * g + p],
                recv_sem=irecv.at[3 * g + p], device_id=(ch >> 1, ch & 1, c), device_id_type=MESH) for g in range(ng)]

        def extra_loads(p, slot):
            copies = []
            within = 0 if cpb == 1 else (chip_at(p) & 1) * 2
            for parity in range(2):
                off = rows
                for n, (x_ref, e) in enumerate(zip(x_refs, ers)):
                    src = x_ref.at[pl.ds(pl.multiple_of((2 * chip_at(p) + parity) * e, 8), e), :]
                    dst = acc.at[slot, pl.ds(pl.multiple_of((within + parity) * srows + off, 8), e), :]
                    copies.append(pltpu.make_async_copy(src, dst, xsem.at[(p * 2 + parity) * nx + n]))
                    off += e
            return copies

        def combine(p, slot):
            push(p, slot).wait_recv()
            total = acc[slot, slab_rows(p, c), :] + stage[p % 2]
            if p == 3:
                stage[p % 2] = total
                pltpu.sync_copy(stage.at[p % 2], town_ref)
            else:
                if p == 2:
                    for cp in ici(0):
                        cp.wait_send()
                for g, (r0, n, dt) in enumerate(groups):
                    send_bufs[g][p % 2] = total[r0:r0 + n, :].astype(dt)
                for cp in ici(p):
                    cp.start()

        for bb in range(nblk):
            slot = bb % 2
            positions = list(range(bb * cpb, (bb + 1) * cpb))

            @pl.when(jnp.logical_and(b == bb, i == 0))
            def _(bb=bb, slot=slot, positions=positions):
                if bb >= 2:
                    for p in range((bb - 2) * cpb, (bb - 1) * cpb):
                        push(p, slot).wait_send()
                for q in range(2 * cpb):
                    acc[slot, pl.ds(q * srows, rows), :] = jnp.zeros((rows, D), F32)
                for p in positions:
                    for cp in extra_loads(p, slot):
                        cp.start()

            if bb >= 1:
                @pl.when(jnp.logical_and(b == bb, i == mid))
                def _(bb=bb):
                    for p in range((bb - 1) * cpb, bb * cpb):
                        combine(p, (bb - 1) % 2)

        res = _tn(l_ref[...].astype(_MXU), r_ref[...].astype(_MXU))
        slot_now = b % 2
        for q in range(2 * cpb):
            acc[slot_now, pl.ds(q * srows, rows), pl.ds(0, k)] += res[q * rows:(q + 1) * rows, :]

        for bb in range(nblk):
            slot = bb % 2
            positions = list(range(bb * cpb, (bb + 1) * cpb))

            @pl.when(jnp.logical_and(b == bb, i == nt - 1))
            def _(bb=bb, slot=slot, positions=positions):
                for p in positions:
                    for cp in extra_loads(p, slot):
                        cp.wait()
                for p in positions:
                    push(p, slot).start()
                if bb == nblk - 1:
                    for p in positions:
                        combine(p, slot)
                    for p in range(max(0, (nblk - 2) * cpb), 4):
                        push(p, slot).wait_send()
                    for p in range(1, 3):
                        for cp in ici(p):
                            cp.wait_send()
                    for p in range(3):
                        for cp in ici(p):
                            cp.wait_recv()

    in_specs = [pl.BlockSpec((tm, 2 * cpb * rows), lambda b, i: (i, block_col(b))),
                pl.BlockSpec((tm, k), lambda b, i: (i, 0))]
    any_spec = pl.BlockSpec(memory_space=pl.ANY)
    in_specs += [any_spec] * nx
    args = [lhs, rhs, *extras]
    outs = pl.pallas_call(
        body, name=name, grid=(nblk, nt),
        out_shape=(jax.ShapeDtypeStruct((srows, D), F32),)
        + tuple(jax.ShapeDtypeStruct((3, n, D), dt) for _, n, dt in groups),
        in_specs=in_specs, out_specs=(any_spec,) * (1 + ng),
        scratch_shapes=[pltpu.VMEM((2, brows, D), F32), pltpu.VMEM((2, srows, D), F32)]
        + [pltpu.VMEM((2, n, D), dt) for _, n, dt in groups]
        + [pltpu.SemaphoreType.DMA((4,)), pltpu.SemaphoreType.DMA((4,)), pltpu.SemaphoreType.DMA((3 * ng,)),
           pltpu.SemaphoreType.DMA((3 * ng,)), pltpu.SemaphoreType.DMA((max(1, 8 * nx),))],
        compiler_params=_cparams(2, VMEM_LIMIT_LARGE),
    )(*args)
    t_own = outs[0]
    return [(t_own[r0:r0 + n], landed) for (r0, n, _), landed in zip(groups, outs[1:])]


def _inproj_fwd(x, g1, own_first, own_second):
    s = x.shape[0]
    tm = min(4 * TM, s // 2)
    nt = s // tm
    assert nt % 2 == 0
    rows1, rows2 = own_first.shape[0], own_second.shape[0]
    wrows = W_OFF["win"][1]
    cw = 2 * wrows

    def chip_col(b):
        px, py, _ = _my_pos()
        return (2 * px + py) ^ jnp.where(b == 0, 0, jnp.where(b == 1, 2, jnp.where(b == 2, 1, 3)))

    def body(x_ref, g1_ref, own1_ref, own2_ref, u_ref, z_ref, gw1_ref, gw2_ref, w_vmem, u_buf, stage1, stage2, sems,
             usem, send1, recv1, local1, send2, recv2, local2):
        b = pl.program_id(0)
        i = pl.program_id(1)
        ga = _Gather(own1_ref, gw1_ref, stage1, send1, recv1, local1)
        gb = _Gather(own2_ref, gw2_ref, stage2, send2, recv2, local2)
        c = ga.c

        def load_chip(px, py, own_too):
            copies = []
            for pc in range(2):
                dst = w_vmem.at[pl.ds(pc * wrows, wrows), :]
                copies.append(pltpu.make_async_copy(gw1_ref.at[4 * px + 2 * py + pc, pl.ds(0, wrows), :], dst,
                                                    sems.at[pc]))
            if own_too:
                mine_dst = w_vmem.at[pl.ds(pl.multiple_of(c * wrows, 16), wrows), :]
                copies[0] = pltpu.make_async_copy(own1_ref.at[pl.ds(0, wrows), :], mine_dst, sems.at[0])
                theirs_dst = w_vmem.at[pl.ds(pl.multiple_of((1 - c) * wrows, 16), wrows), :]
                copies[1] = pltpu.make_async_copy(gw1_ref.at[4 * px + 2 * py + 1 - c, pl.ds(0, wrows), :], theirs_dst,
                                                  sems.at[1])
            for cp in copies:
                cp.start()
            for cp in copies:
                cp.wait()

        @pl.when(jnp.logical_and(b == 0, i == 0))
        def _():
            ga.send_mine(far=False)
            ga.wait_sibling()
            load_chip(ga.me[0], ga.me[1], True)

        @pl.when(jnp.logical_and(b == 0, i == nt // 4))
        def _():
            ga.send_far()
            gb.send_mine()

        @pl.when(jnp.logical_and(b == 0, i == (3 * nt) // 4))
        def _():
            ga.pass_on((0, 1))

        @pl.when(jnp.logical_and(b == 1, i == (3 * nt) // 4))
        def _():
            ga.pass_on((2,))

        for j in range(3):
            @pl.when(jnp.logical_and(b == j + 1, i == 0))
            def _(j=j):
                ga.wait_passed((j,))
                load_chip(ga.chips[j][0], ga.chips[j][1], False)

        @pl.when(jnp.logical_and(b == 2, i == nt // 2))
        def _():
            gb.pass_on((0, 1))

        @pl.when(jnp.logical_and(b == 3, i == (3 * nt) // 4))
        def _():
            gb.pass_on((2,))

        slot = i % 2

        def u_write(t, sl):
            return pltpu.make_async_copy(u_buf.at[sl], u_ref.at[pl.ds(pl.multiple_of(t * tm, tm), tm), :], usem.at[sl])

        def u_read(t, sl):
            return pltpu.make_async_copy(u_ref.at[pl.ds(pl.multiple_of(t * tm, tm), tm), :], u_buf.at[sl], usem.at[sl])

        @pl.when(b == 0)
        def _():
            @pl.when(i >= 2)
            def _():
                u_write(i - 2, slot).wait()

            xv = x_ref[...]
            inv = lax.rsqrt(jnp.mean(xv * xv, axis=-1, keepdims=True) + EPS)
            u_buf[slot] = (xv * inv * g1_ref[...]).astype(_MXU)
            u_write(i, slot).start()

            @pl.when(i == nt - 1)
            def _():
                u_write(i - 1, 1 - slot).wait()
                u_write(i, slot).wait()
                u_read(0, 0).start()

        @pl.when(b > 0)
        def _():
            u_read(i, slot).wait()

            @pl.when(jnp.logical_or(b < 3, i < nt - 1))
            def _():
                u_read((i + 1) % nt, 1 - slot).start()

        z_ref[...] = _nt(u_buf[slot], w_vmem[...])

        @pl.when(jnp.logical_and(b == 3, i == nt - 1))
        def _():
            ga.finish_sends()
            gb.wait_sibling()
            gb.wait_passed((0, 1, 2))
            gb.finish_sends()

    any_spec = pl.BlockSpec(memory_space=pl.ANY)
    dma7 = pltpu.SemaphoreType.DMA((7,))
    return pl.pallas_call(
        body, name="inproj_fwd", grid=(4, nt),
        out_shape=(jax.ShapeDtypeStruct((s, D), _MXU), jax.ShapeDtypeStruct((s, NIN), F32),
                   jax.ShapeDtypeStruct((NDEV, rows1, D), own_first.dtype),
                   jax.ShapeDtypeStruct((NDEV, rows2, D), own_second.dtype)),
        in_specs=[pl.BlockSpec((tm, D), lambda b, i: (jnp.where(b == 0, i, nt - 1), 0)),
                  pl.BlockSpec((1, D), lambda b, i: (0, 0)), any_spec, any_spec],
        out_specs=(any_spec, pl.BlockSpec((tm, cw), lambda b, i: (i, chip_col(b))), any_spec, any_spec),
        scratch_shapes=[pltpu.VMEM((cw, D), _MXU), pltpu.VMEM((2, tm, D), _MXU), pltpu.VMEM((rows1, D), own_first.dtype),
                        pltpu.VMEM((rows2, D), own_second.dtype), pltpu.SemaphoreType.DMA((2,)),
                        pltpu.SemaphoreType.DMA((2,)),
                        dma7, dma7, pltpu.SemaphoreType.DMA, dma7, dma7, pltpu.SemaphoreType.DMA],
        compiler_params=_cparams(2),
    )(x, g1, own_first, own_second)


def _pool_tile(pbuf, t0, tm, pw_ref, scale_ref):
    t = t0 + lax.broadcasted_iota(jnp.int32, (tm, GD), 0)
    pooled, mixed_pre = [], []
    for g, w in enumerate(WINDOWS):
        cs = pl.ds(g * GD, GD)
        cur = pbuf[pl.ds(HALO, tm), cs]
        acc = cur
        for d in range(1, w):
            acc = acc + pbuf[pl.ds(HALO - d, tm), cs]
        cnt = jnp.minimum(t + 1, w).astype(F32)
        pg = acc / cnt - cur
        pooled.append(pg)
        mixed_pre.append(_nn(pg.astype(_MXU), pw_ref[g]))
    return pooled, mixed_pre


def _lru_gates_head(hh, lbuf, start, tm, cw_ref, cb_ref, wrg_ref, brg_ref, wig_ref, big_ref, sp):
    cs = pl.ds(hh * HD, HD)
    xc = cb_ref[:, cs] + cw_ref[pl.ds(CONV - 1, 1), cs] * lbuf[pl.ds(HALO, tm), cs]
    for k in range(CONV - 1):
        xc = xc + cw_ref[pl.ds(k, 1), cs] * lbuf[pl.ds(HALO - (CONV - 1) + k, tm), cs]
    xcm = xc.astype(_MXU)
    r = _sigmoid(_nn(xcm, wrg_ref[hh]) + brg_ref[pl.ds(hh, 1), :])
    ig = _sigmoid(_nn(xcm, wig_ref[hh]) + big_ref[pl.ds(hh, 1), :])
    a = jnp.exp(-LRU_C * r * sp[:, hh * HD:(hh + 1) * HD])
    one_m = 1.0 - a * a
    live = jnp.logical_and(one_m > 0.0, jnp.logical_not(start))
    inv_mult = lax.rsqrt(jnp.where(live, one_m, 1.0))
    mult = jnp.where(live, one_m * inv_mult, jnp.where(start, 1.0, 0.0))
    return xc, r, ig, a, live, inv_mult, mult


def _seg_layout(tm):
    seg = tm // 8
    return seg, seg + 8


def _to_segments(dst_ref, hh, val, tm):
    seg, pitch = _seg_layout(tm)
    for s in range(8):
        dst_ref[hh, pl.ds(s * pitch, seg), :] = val[s * seg:(s + 1) * seg, :]


def _from_segments(src_ref, hh, tm):
    seg, pitch = _seg_layout(tm)
    return jnp.concatenate([src_ref[hh, pl.ds(s * pitch, seg), :] for s in range(8)], axis=0)


def _segment_scan(a_ref, b_ref, out_ref, hk, pk, carry_ref, tm, reverse):
    seg, pitch = _seg_layout(tm)
    row = lax.broadcasted_iota(jnp.int32, (8, HD), 0)
    order = range(seg - 1, -1, -1) if reverse else range(seg)
    for hh in range(HEADS):
        cs = pl.ds(hh * HD, HD)
        if reverse:
            a0 = a_ref[hh, pl.ds(0, 8, stride=pitch), :]
            a_wrap = jnp.where(row <= 6, pltpu.roll(a0, 7, 0), 1.0)
        hv = jnp.zeros((8, HD), F32)
        pv = jnp.ones((8, HD), F32)
        for k in order:
            if not reverse:
                av = a_ref[hh, pl.ds(k, 8, stride=pitch), :]
            elif k + 1 < seg:
                av = a_ref[hh, pl.ds(k + 1, 8, stride=pitch), :]
            else:
                av = a_wrap
            hv = av * hv + b_ref[hh, pl.ds(k, 8, stride=pitch), :]
            pv = av * pv
            hk[hh, pl.ds(8 * k, 8), :] = hv
            pk[hh, pl.ds(8 * k, 8), :] = pv
        for d in (1, 2, 4):
            if reverse:
                keep, sh = row < 8 - d, 8 - d
            else:
                keep, sh = row >= d, d
            hv = hv + pv * jnp.where(keep, pltpu.roll(hv, sh, 0), 0.0)
            pv = pv * jnp.where(keep, pltpu.roll(pv, sh, 0), 1.0)
        cin = carry_ref[:, cs]
        ends = hv + pv * cin
        if reverse:
            enter = jnp.where(row <= 6, pltpu.roll(ends, 7, 0), cin)
            carry_ref[:, cs] = jnp.broadcast_to((a0 * ends)[0:1, :], (8, HD))
        else:
            enter = jnp.where(row >= 1, pltpu.roll(ends, 1, 0), cin)
            carry_ref[:, cs] = jnp.broadcast_to(ends[7:8, :], (8, HD))
        for k in range(seg):
            out_ref[hh, pl.ds(k, 8, stride=pitch), :] = hk[hh, pl.ds(8 * k, 8), :] + pk[hh, pl.ds(8 * k, 8), :] * enter


def _mixer_fwd(z, x, gw, small, own_third):
    s = x.shape[0]
    tm = min(TM_SEQ, s)
    nt = s // tm
    rows3 = own_third.shape[0]
    (pool_w, pool_scale, conv_w, conv_b, w_rg, b_rg, w_ig, b_ig, lam, b_gate) = small

    def body(z_ref, x_ref, gw_ref, own3_ref, pw_ref, ps_ref, cw_ref, cb_ref, wrg_ref, brg_ref, wig_ref, big_ref, lam_ref,
             bg_ref, h_ref, yl_ref, mg_ref, yp_ref, yr_ref, h1_ref, a_ref, r_ref, ig_ref, xc_ref, gw3_ref,
             pprojT, lru_w, wout_w, pbuf, lbuf, a_s, b_s, h_s, hk, pk, hcar, sems, stage3, send3, recv3, local3):
        i = pl.program_id(0)
        t0 = i * tm
        gc = _Gather(own3_ref, gw3_ref, stage3, send3, recv3, local3)

        @pl.when(i == 0)
        def _():
            gc.send_mine()
            _load_weights(gw_ref, [("pproj", pprojT, PW), ("lru", lru_w, D), ("wout", wout_w, D)], sems)
            pbuf[pl.ds(0, HALO), :] = jnp.zeros((HALO, PW), F32)
            lbuf[pl.ds(0, HALO), :] = jnp.zeros((HALO, D), F32)
            hcar[...] = jnp.zeros_like(hcar)

        @pl.when(i == nt // 2)
        def _():
            gc.pass_on((0, 1))

        @pl.when(i == (3 * nt) // 4)
        def _():
            gc.pass_on((2,))

        pbuf[pl.ds(HALO, tm), :] = z_ref[:, pl.ds(0, PW)]
        _, mixed_pre = _pool_tile(pbuf, t0, tm, pw_ref, ps_ref)
        mixed = jnp.concatenate(mixed_pre, axis=1) * ps_ref[...]
        y_pool = _nt(mixed.astype(_MXU), pprojT[...])
        pbuf[pl.ds(0, HALO), :] = pbuf[pl.ds(tm, HALO), :]

        lbuf[pl.ds(HALO, tm), :] = z_ref[:, pl.ds(PW, D)]
        sp, _ = _softplus_neg(lam_ref[...])
        start = (t0 + lax.broadcasted_iota(jnp.int32, (tm, HD), 0)) == 0
        for hh in range(HEADS):
            xc, r, ig, a, _, _, mult = _lru_gates_head(hh, lbuf, start, tm, cw_ref, cb_ref, wrg_ref, brg_ref,
                                                       wig_ref, big_ref, sp)
            _to_segments(a_s, hh, a, tm)
            _to_segments(b_s, hh, mult * ig * xc, tm)
            cs = pl.ds(hh * HD, HD)
            a_ref[:, cs] = a
            r_ref[:, cs] = r.astype(_MXU)
            ig_ref[:, cs] = ig.astype(_MXU)
            xc_ref[:, cs] = xc.astype(_MXU)
        lbuf[pl.ds(0, HALO), :] = lbuf[pl.ds(tm, HALO), :]
        _segment_scan(a_s, b_s, h_s, hk, pk, hcar, tm, reverse=False)
        for hh in range(HEADS):
            h_ref[:, pl.ds(hh * HD, HD)] = _from_segments(h_s, hh, tm)
        gel, _ = _gelu_and_grad(z_ref[:, pl.ds(PW + D, D)])
        yl = (h_ref[...] * gel).astype(_MXU)
        yl_ref[...] = yl
        y_lru = _nn(yl, lru_w[...])

        g0 = _sigmoid(z_ref[:, pl.ds(PW + 2 * D, D)] + bg_ref[pl.ds(0, 1), :])
        g1 = _sigmoid(z_ref[:, pl.ds(PW + 3 * D, D)] + bg_ref[pl.ds(1, 1), :])
        merged = (g0 * y_pool + g1 * y_lru).astype(_MXU)
        mg_ref[...] = merged
        yp_ref[...] = y_pool.astype(_MXU)
        yr_ref[...] = y_lru.astype(_MXU)
        h1_ref[...] = x_ref[...] + _nn(merged, wout_w[...])

        @pl.when(i == nt - 1)
        def _():
            gc.wait_sibling()
            gc.wait_passed((0, 1, 2))
            gc.finish_sends()

    tok = lambda w, dt: jax.ShapeDtypeStruct((s, w), dt)
    tspec = lambda w: pl.BlockSpec((tm, w), lambda i: (i, 0))
    full = lambda a: pl.BlockSpec(a.shape, lambda i: (0,) * a.ndim)
    any_spec = pl.BlockSpec(memory_space=pl.ANY)
    seg_buf = pltpu.VMEM((HEADS, 8 * _seg_layout(tm)[1], HD), F32)
    dma7 = pltpu.SemaphoreType.DMA((7,))
    return pl.pallas_call(
        body, name="mixer_fwd", grid=(nt,),
        out_shape=(tok(D, F32), tok(D, _MXU), tok(D, _MXU), tok(D, _MXU), tok(D, _MXU), tok(D, F32),
                   tok(D, F32), tok(D, _MXU), tok(D, _MXU), tok(D, _MXU),
                   jax.ShapeDtypeStruct((NDEV, rows3, D), own_third.dtype)),
        in_specs=[tspec(NIN), tspec(D), any_spec, any_spec] + [full(a) for a in small],
        out_specs=(tspec(D),) * 10 + (any_spec,),
        scratch_shapes=[pltpu.VMEM((D, PW), _MXU), pltpu.VMEM((D, D), _MXU), pltpu.VMEM((D, D), _MXU),
                        pltpu.VMEM((tm + HALO, PW), F32), pltpu.VMEM((tm + HALO, D), F32),
                        seg_buf, seg_buf, seg_buf, pltpu.VMEM((HEADS, tm, HD), F32), pltpu.VMEM((HEADS, tm, HD), F32),
                        pltpu.VMEM((8, D), F32), pltpu.SemaphoreType.DMA((3 * NDEV,)),
                        pltpu.VMEM((rows3, D), own_third.dtype), dma7, dma7, pltpu.SemaphoreType.DMA],
        compiler_params=_cparams(1),
    )(z, x, gw, own_third, *small)


def _ffn_fwd(h1, g2, gw):
    s = h1.shape[0]
    tm = min(TM, s)

    def body(h1_ref, g2_ref, gw_ref, v_ref, gf_ref, uf_ref, h2_ref, wffnT, wffo, sems):
        @pl.when(pl.program_id(0) == 0)
        def _():
            _load_weights(gw_ref, [("wffn", wffnT, D), ("wffo", wffo, D)], sems)

        hv = h1_ref[...]
        inv = lax.rsqrt(jnp.mean(hv * hv, axis=-1, keepdims=True) + EPS)
        v = (hv * inv * g2_ref[...]).astype(_MXU)
        v_ref[...] = v
        acc = hv
        for c0, cn in FF_CHUNKS:
            cs = pl.ds(c0, cn)
            gf = _nt(v, wffnT[cs, :]).astype(_MXU)
            uf = _nt(v, wffnT[pl.ds(FF + c0, cn), :]).astype(_MXU)
            gf_ref[:, cs] = gf
            uf_ref[:, cs] = uf
            gf32 = gf.astype(F32)
            act = (gf32 * _sigmoid(gf32) * uf.astype(F32)).astype(_MXU)
            acc = acc + _nn(act, wffo[cs, :])
        h2_ref[...] = acc

    tspec = lambda w: pl.BlockSpec((tm, w), lambda i: (i, 0))
    return pl.pallas_call(
        body, name="ffn_fwd", grid=(s // tm,),
        out_shape=(jax.ShapeDtypeStruct((s, D), _MXU), jax.ShapeDtypeStruct((s, FF), _MXU),
                   jax.ShapeDtypeStruct((s, FF), _MXU), jax.ShapeDtypeStruct((s, D), F32)),
        in_specs=[tspec(D), pl.BlockSpec((1, D), lambda i: (0, 0)), pl.BlockSpec(memory_space=pl.ANY)],
        out_specs=(tspec(D), tspec(FF), tspec(FF), tspec(D)),
        scratch_shapes=[pltpu.VMEM((2 * FF, D), _MXU), pltpu.VMEM((FF, D), _MXU), pltpu.SemaphoreType.DMA((2 * NDEV,))],
        compiler_params=_cparams(1),
    )(h1, g2, gw)


def _rms_bwd(dy, xn, inv, g):
    dg = jnp.sum(dy * xn, axis=0, keepdims=True)
    dxn = dy * g
    dx = inv * (dxn - xn * jnp.mean(dxn * xn, axis=-1, keepdims=True))
    return dx, dg


def _ple_loss_fwd_bwd(h2, p, target, g3, gfin, gw):
    s = h2.shape[0]
    tm = min(TM, s)

    def body(h2_ref, p_ref, t_ref, g3_ref, gf_ref, gw_ref,
             dh2_ref, loss_ref, dg3_ref, dgf_ref, gwpg_ref, gple_ref, wpg, pleT, sems):
        i = pl.program_id(0)

        @pl.when(i == 0)
        def _():
            _load_weights(gw_ref, [("wpg", wpg, D), ("ple", pleT, PLE)], sems)
            for ref in (loss_ref, dg3_ref, dgf_ref, gwpg_ref, gple_ref):
                ref[...] = jnp.zeros_like(ref)

        hv = h2_ref[...]
        inv3 = lax.rsqrt(jnp.mean(hv * hv, axis=-1, keepdims=True) + EPS)
        xn3 = hv * inv3
        n3 = (xn3 * g3_ref[...]).astype(_MXU)
        pg = _sigmoid(_nn(n3, wpg[...]))
        pm = p_ref[...].astype(_MXU)
        e = _nt(pm, pleT[...])
        h3 = hv + pg * e
        invf = lax.rsqrt(jnp.mean(h3 * h3, axis=-1, keepdims=True) + EPS)
        xf = h3 * invf
        diff = xf * gf_ref[...] - t_ref[...]
        loss_ref[...] += jnp.sum(diff * diff) * (0.5 / D)
        dh3, dgf = _rms_bwd(diff * (1.0 / D), xf, invf, gf_ref[...])
        dgf_ref[...] += dgf
        gple_ref[:, pl.ds(0, PLE)] += _tn((dh3 * pg).astype(_MXU), pm)
        dpg = (dh3 * e * pg * (1.0 - pg)).astype(_MXU)
        gwpg_ref[...] += _tn(n3, dpg)
        dn3 = _nt(dpg, wpg[...])
        dx3, dg3 = _rms_bwd(dn3, xn3, inv3, g3_ref[...])
        dg3_ref[...] += dg3
        dh2_ref[...] = dh3 + dx3

    tspec = lambda w: pl.BlockSpec((tm, w), lambda i: (i, 0))
    vec = pl.BlockSpec((1, D), lambda i: (0, 0))
    mat = pl.BlockSpec((D, D), lambda i: (0, 0))
    return pl.pallas_call(
        body, name="ple_loss", grid=(s // tm,),
        out_shape=(jax.ShapeDtypeStruct((s, D), F32), jax.ShapeDtypeStruct((8, 128), F32),
                   jax.ShapeDtypeStruct((1, D), F32), jax.ShapeDtypeStruct((1, D), F32),
                   jax.ShapeDtypeStruct((D, D), F32), jax.ShapeDtypeStruct((D, D), F32)),
        in_specs=[tspec(D), tspec(PLE), tspec(D), vec, vec, pl.BlockSpec(memory_space=pl.ANY)],
        out_specs=(tspec(D), pl.BlockSpec((8, 128), lambda i: (0, 0)), vec, vec, mat, mat),
        scratch_shapes=[pltpu.VMEM((D, D), _MXU), pltpu.VMEM((D, PLE), _MXU), pltpu.SemaphoreType.DMA((2 * NDEV,))],
        compiler_params=_cparams(1),
    )(h2, p, target, g3, gfin, gw)


def _ffn_bwd_hidden(dh2, gf, uf, gw):
    s = dh2.shape[0]
    tm = min(TM, s)
    nt = s // tm

    def body(dh2_ref, gf_ref, uf_ref, gw_ref, dff_ref, gwo_ref, wffo, gacc, sems):
        i = pl.program_id(0)

        @pl.when(i == 0)
        def _():
            _load_weights(gw_ref, [("wffo", wffo, D)], sems)
            gacc[...] = jnp.zeros_like(gacc)

        dm = dh2_ref[...].astype(_MXU)
        dacts = [_nt(dm, wffo[pl.ds(c0, cn), :]) for c0, cn in FF_CHUNKS]
        for (c0, cn), dact in zip(FF_CHUNKS, dacts):
            cs = pl.ds(c0, cn)
            gfv = gf_ref[:, cs].astype(F32)
            ufv = uf_ref[:, cs].astype(F32)
            sg = _sigmoid(gfv)
            silu = gfv * sg
            gacc[cs, :] += _tn((silu * ufv).astype(_MXU), dm)
            dff_ref[:, cs] = (dact * ufv * (sg * (1.0 + gfv * (1.0 - sg)))).astype(_MXU)
            dff_ref[:, pl.ds(FF + c0, cn)] = (dact * silu).astype(_MXU)

        @pl.when(i == nt - 1)
        def _():
            pltpu.sync_copy(gacc, gwo_ref)

    tspec = lambda w: pl.BlockSpec((tm, w), lambda i: (i, 0))
    return pl.pallas_call(
        body, name="ffn_bwd_hidden", grid=(nt,),
        out_shape=(jax.ShapeDtypeStruct((s, 2 * FF), _MXU), jax.ShapeDtypeStruct((FF, D), F32)),
        in_specs=[tspec(D), tspec(FF), tspec(FF), pl.BlockSpec(memory_space=pl.ANY)],
        out_specs=(tspec(2 * FF), pl.BlockSpec(memory_space=pl.ANY)),
        scratch_shapes=[pltpu.VMEM((FF, D), _MXU), pltpu.VMEM((FF, D), F32), pltpu.SemaphoreType.DMA((NDEV,))],
        compiler_params=_cparams(1),
    )(dh2, gf, uf, gw)


def _proj_norm_bwd(dy, x, dres, g, gw, slab, width, name, lhs=None):
    s = x.shape[0]
    tm = min(TM, s)
    nl = 0 if lhs is None else 1

    def body(*refs):
        dy_ref, x_ref, dr_ref, g_ref = refs[:4]
        l_refs = refs[4:4 + nl]
        gw_ref, dx_ref, dg_ref = refs[4 + nl:7 + nl]
        gl_refs = refs[7 + nl:7 + 2 * nl]
        wT, sems = refs[7 + 2 * nl:]

        @pl.when(pl.program_id(0) == 0)
        def _():
            _load_weights(gw_ref, [(slab, wT, D)], sems)
            dg_ref[...] = jnp.zeros_like(dg_ref)
            for ref in gl_refs:
                ref[...] = jnp.zeros_like(ref)

        dv = _nn(dy_ref[...], wT[...])
        xv = x_ref[...]
        inv = lax.rsqrt(jnp.mean(xv * xv, axis=-1, keepdims=True) + EPS)
        dx, dg = _rms_bwd(dv, xv * inv, inv, g_ref[...])
        dg_ref[...] += dg
        dr = dr_ref[...]
        dx_ref[...] = dr + dx
        for l_ref, gl_ref in zip(l_refs, gl_refs):
            gl_ref[...] += _tn(l_ref[...], dr.astype(_MXU))

    tspec = lambda w: pl.BlockSpec((tm, w), lambda i: (i, 0))
    vec = pl.BlockSpec((1, D), lambda i: (0, 0))
    mat = pl.BlockSpec((D, D), lambda i: (0, 0))
    return pl.pallas_call(
        body, name=name, grid=(s // tm,),
        out_shape=(jax.ShapeDtypeStruct((s, D), F32), jax.ShapeDtypeStruct((1, D), F32))
        + (jax.ShapeDtypeStruct((D, D), F32),) * nl,
        in_specs=[tspec(width), tspec(D), tspec(D), vec] + [tspec(D)] * nl + [pl.BlockSpec(memory_space=pl.ANY)],
        out_specs=(tspec(D), vec) + (mat,) * nl,
        scratch_shapes=[pltpu.VMEM((width, D), _MXU), pltpu.SemaphoreType.DMA((NDEV,))],
        compiler_params=_cparams(1),
    )(dy, x, dres, g, *([] if lhs is None else [lhs]), gw)


def _mixer_bwd(dh1, z, h, y_pool, y_lru, saved, gw, small):
    s = dh1.shape[0]
    tm = min(TM_SEQ, s)
    nt = s // tm
    (pool_w, pool_scale, conv_w, conv_b, w_rg, b_rg, w_ig, b_ig, lam, b_gate) = small

    def body(dh1_ref, z_ref, zp_ref, h_ref, hp_ref, yp_ref, yr_ref, a_ref, r_ref, ig_ref, xc_ref, gw_ref,
             pw_ref, ps_ref, cw_ref, cb_ref, wrg_ref, brg_ref, wig_ref, big_ref, lam_ref, bg_ref,
             dz_ref, dyr_ref, dyp_ref, mx_ref,
             gbg_ref, glam_ref, gbrg_ref, gbig_ref, gcb_ref, gcw_ref, gps_ref, gpw_ref, gwrg_ref, gwig_ref,
             pprojT, lru_w, wout_w, pbuf, lbuf, hbuf, qbuf, xbuf, a_s, g_s, dh_s, hk, pk, dcar, sems):
        step = pl.program_id(0)
        i = nt - 1 - step
        t0 = i * tm

        @pl.when(step == 0)
        def _():
            _load_weights(gw_ref, [("pproj", pprojT, PW), ("lru", lru_w, D), ("wout", wout_w, D)], sems)
            for ref in (gbg_ref, glam_ref, gbrg_ref, gbig_ref, gcb_ref, gcw_ref, gps_ref, gpw_ref, gwrg_ref, gwig_ref):
                ref[...] = jnp.zeros_like(ref)
            qbuf[pl.ds(tm, HALO), :] = jnp.zeros((HALO, PW), F32)
            xbuf[pl.ds(tm, 8), :] = jnp.zeros((8, D), F32)
            dcar[...] = jnp.zeros_like(dcar)

        first = i == 0
        zprev = jnp.where(first, 0.0, zp_ref[...])
        hprev = jnp.where(first, 0.0, hp_ref[...])

        d_merged = _nt(dh1_ref[...].astype(_MXU), wout_w[...])

        g0 = _sigmoid(z_ref[:, pl.ds(PW + 2 * D, D)] + bg_ref[pl.ds(0, 1), :])
        g1 = _sigmoid(z_ref[:, pl.ds(PW + 3 * D, D)] + bg_ref[pl.ds(1, 1), :])
        dz0 = d_merged * yp_ref[...].astype(F32) * g0 * (1.0 - g0)
        dz1 = d_merged * yr_ref[...].astype(F32) * g1 * (1.0 - g1)
        dz_ref[:, pl.ds(PW + 2 * D, D)] = dz0.astype(_MXU)
        dz_ref[:, pl.ds(PW + 3 * D, D)] = dz1.astype(_MXU)
        gbg_ref[pl.ds(0, 1), :] += jnp.sum(dz0, axis=0, keepdims=True)
        gbg_ref[pl.ds(1, 1), :] += jnp.sum(dz1, axis=0, keepdims=True)
        d_ypool = (d_merged * g0).astype(_MXU)
        d_ylru = (d_merged * g1).astype(_MXU)
        dyp_ref[...] = d_ypool
        dyr_ref[...] = d_ylru

        d_yl = _nt(d_ylru, lru_w[...])
        gel, dgel = _gelu_and_grad(z_ref[:, pl.ds(PW + D, D)])
        dz_ref[:, pl.ds(PW + D, D)] = (d_yl * h_ref[...] * dgel).astype(_MXU)
        g_full = d_yl * gel
        lbuf[pl.ds(0, HALO), :] = zprev[:, PW:PW + D]
        lbuf[pl.ds(HALO, tm), :] = z_ref[:, pl.ds(PW, D)]
        hbuf[pl.ds(0, 8), :] = hprev
        hbuf[pl.ds(8, tm), :] = h_ref[...]
        sp, sneg = _softplus_neg(lam_ref[...])
        start = (t0 + lax.broadcasted_iota(jnp.int32, (tm, HD), 0)) == 0
        for hh in range(HEADS):
            cs = pl.ds(hh * HD, HD)
            _to_segments(a_s, hh, a_ref[:, cs], tm)
            _to_segments(g_s, hh, g_full[:, hh * HD:(hh + 1) * HD], tm)
        _segment_scan(a_s, g_s, dh_s, hk, pk, dcar, tm, reverse=True)
        for hh in range(HEADS):
            cs = pl.ds(hh * HD, HD)
            a = a_ref[:, cs]
            r = r_ref[:, cs].astype(F32)
            ig = ig_ref[:, cs].astype(F32)
            xc = xc_ref[:, cs].astype(F32)
            a2 = a * a
            one_m = 1.0 - a2
            live = jnp.logical_and(one_m > 0.0, jnp.logical_not(start))
            inv_mult = lax.rsqrt(jnp.where(live, one_m, 1.0))
            mult = jnp.where(live, one_m * inv_mult, jnp.where(start, 1.0, 0.0))
            dh = _from_segments(dh_s, hh, tm)
            d_mult = dh * ig * xc
            d_loga = dh * hbuf[pl.ds(7, tm), cs] * a - jnp.where(live, d_mult * a2 * inv_mult, 0.0)
            glam_ref[:, cs] += jnp.sum(d_loga * (LRU_C * r) * sneg[:, hh * HD:(hh + 1) * HD], axis=0, keepdims=True)
            d_rpre = d_loga * (-LRU_C * sp[:, hh * HD:(hh + 1) * HD]) * r * (1.0 - r)
            d_igpre = dh * mult * xc * ig * (1.0 - ig)
            gbrg_ref[pl.ds(hh, 1), :] += jnp.sum(d_rpre, axis=0, keepdims=True)
            gbig_ref[pl.ds(hh, 1), :] += jnp.sum(d_igpre, axis=0, keepdims=True)
            drm = d_rpre.astype(_MXU)
            dim = d_igpre.astype(_MXU)
            xcm = xc.astype(_MXU)
            gwrg_ref[hh] += _tn(xcm, drm)
            gwig_ref[hh] += _tn(xcm, dim)
            d_xc = dh * mult * ig + _nt(drm, wrg_ref[hh]) + _nt(dim, wig_ref[hh])
            gcb_ref[:, cs] += jnp.sum(d_xc, axis=0, keepdims=True)
            for k in range(CONV):
                gcw_ref[pl.ds(k, 1), cs] += jnp.sum(d_xc * lbuf[pl.ds(HALO - (CONV - 1) + k, tm), cs], axis=0,
                                                    keepdims=True)
            xbuf[pl.ds(0, tm), cs] = d_xc
        dzl = cw_ref[pl.ds(CONV - 1, 1), :] * xbuf[pl.ds(0, tm), :]
        for k in range(CONV - 1):
            dzl = dzl + cw_ref[pl.ds(k, 1), :] * xbuf[pl.ds(CONV - 1 - k, tm), :]
        dz_ref[:, pl.ds(PW, D)] = dzl.astype(_MXU)
        xbuf[pl.ds(tm, 8), :] = xbuf[pl.ds(0, 8), :]

        d_mixed = _nn(d_ypool, pprojT[...])
        pbuf[pl.ds(0, HALO), :] = zprev[:, 0:PW]
        pbuf[pl.ds(HALO, tm), :] = z_ref[:, pl.ds(0, PW)]
        pooled, mixed_pre = _pool_tile(pbuf, t0, tm, pw_ref, ps_ref)
        mp = jnp.concatenate(mixed_pre, axis=1)
        mx_ref[...] = (mp * ps_ref[...]).astype(_MXU)
        gps_ref[...] += jnp.sum(d_mixed * mp, axis=0, keepdims=True)
        d_mp = (d_mixed * ps_ref[...]).astype(_MXU)
        t = t0 + lax.broadcasted_iota(jnp.int32, (tm, GD), 0)
        d_pooled = []
        for g, w in enumerate(WINDOWS):
            dmg = d_mp[:, g * GD:(g + 1) * GD]
            gpw_ref[g] += _tn(pooled[g].astype(_MXU), dmg)
            dp = _nt(dmg, pw_ref[g])
            d_pooled.append(dp)
            qbuf[pl.ds(0, tm), pl.ds(g * GD, GD)] = dp / jnp.minimum(t + 1, w).astype(F32)
        for g, w in enumerate(WINDOWS):
            cs = pl.ds(g * GD, GD)
            acc = qbuf[pl.ds(0, tm), cs]
            for d in range(1, w):
                acc = acc + qbuf[pl.ds(d, tm), cs]
            dz_ref[:, cs] = (acc - d_pooled[g]).astype(_MXU)
        qbuf[pl.ds(tm, HALO), :] = qbuf[pl.ds(0, HALO), :]

    rev = lambda w: pl.BlockSpec((tm, w), lambda g: (nt - 1 - g, 0))
    prev = lambda rows, w: pl.BlockSpec((rows, w), lambda g: (jnp.maximum((nt - 1 - g) * (tm // rows) - 1, 0), 0))
    full = lambda a: pl.BlockSpec(a.shape, lambda g: (0,) * a.ndim)
    tok = lambda w, dt: jax.ShapeDtypeStruct((s, w), dt)
    acc_shapes = [(2, D), (1, D), (HEADS, HD), (HEADS, HD), (1, D), (CONV, D), (1, PW), (GROUPS, GD, GD),
                  (HEADS, HD, HD), (HEADS, HD, HD)]
    acc_specs = tuple(pl.BlockSpec(sh, lambda g, n=len(sh): (0,) * n) for sh in acc_shapes)
    seg_buf = pltpu.VMEM((HEADS, 8 * _seg_layout(tm)[1], HD), F32)
    a_in, r_in, ig_in, xc_in = saved
    return pl.pallas_call(
        body, name="mixer_bwd", grid=(nt,),
        out_shape=(tok(NIN, _MXU), tok(D, _MXU), tok(D, _MXU), tok(PW, _MXU))
        + tuple(jax.ShapeDtypeStruct(sh, F32) for sh in acc_shapes),
        in_specs=[rev(D), rev(NIN), prev(HALO, NIN), rev(D), prev(8, D), rev(D), rev(D), rev(D), rev(D), rev(D), rev(D),
                  pl.BlockSpec(memory_space=pl.ANY)] + [full(a) for a in small],
        out_specs=(rev(NIN), rev(D), rev(D), rev(PW)) + acc_specs,
        scratch_shapes=[pltpu.VMEM((D, PW), _MXU), pltpu.VMEM((D, D), _MXU), pltpu.VMEM((D, D), _MXU),
                        pltpu.VMEM((tm + HALO, PW), F32), pltpu.VMEM((tm + HALO, D), F32),
                        pltpu.VMEM((tm + 8, D), F32), pltpu.VMEM((tm + HALO, PW), F32), pltpu.VMEM((tm + 8, D), F32),
                        seg_buf, seg_buf, seg_buf, pltpu.VMEM((HEADS, tm, HD), F32), pltpu.VMEM((HEADS, tm, HD), F32),
                        pltpu.VMEM((8, D), F32), pltpu.SemaphoreType.DMA((3 * NDEV,))],
        compiler_params=_cparams(1),
    )(dh1, z, z, h, h, y_pool, y_lru, a_in, r_in, ig_in, xc_in, gw, *small)


def _split3(a):
    hi = a.astype(jnp.bfloat16).astype(F32)
    mid = (a - hi).astype(jnp.bfloat16).astype(F32)
    lo = (a - hi - mid).astype(jnp.bfloat16).astype(F32)
    return jnp.stack([hi, mid, lo])


def _small_pack(parts):
    flat = jnp.concatenate([a.reshape(-1) for a in parts])
    return jnp.pad(flat, (0, NDEV * SMALL_ROWS * D - flat.shape[0])).reshape(NDEV * SMALL_ROWS, D)


def _small_unpack(packed, shapes):
    flat = packed.reshape(-1)
    out, o = [], 0
    for sh in shapes:
        n = math.prod(sh)
        out.append(flat[o:o + n].reshape(sh))
        o += n
    return out


def kernel(x, p, norm1_g, w_in, b_gate, pool_w, pool_scale, pool_proj, conv_w, conv_b, w_rg, b_rg, w_ig, b_ig, lru_lambda, lru_proj, w_out, norm2_g, w_ffn_in, w_ffn_out, ple_norm_g, w_ple_gate, w_ple_proj, final_g, loss_target, m_norm1_g, m_w_in, m_b_gate, m_pool_w, m_pool_scale, m_pool_proj, m_conv_w, m_conv_b, m_w_rg, m_b_rg, m_w_ig, m_b_ig, m_lru_lambda, m_lru_proj, m_w_out, m_norm2_g, m_w_ffn_in, m_w_ffn_out, m_ple_norm_g, m_w_ple_gate, m_w_ple_proj, m_final_g, v_norm1_g, v_w_in, v_b_gate, v_pool_w, v_pool_scale, v_pool_proj, v_conv_w, v_conv_b, v_w_rg, v_b_rg, v_w_ig, v_b_ig, v_lru_lambda, v_lru_proj, v_w_out, v_norm2_g, v_w_ffn_in, v_w_ffn_out, v_ple_norm_g, v_w_ple_gate, v_w_ple_proj, v_final_g):
    axes = ("x", "y", "c")
    me = 4 * lax.axis_index("x") + 2 * lax.axis_index("y") + lax.axis_index("c")
    x2 = x[0]
    p2 = p[0, 0]
    tgt = loss_target[0]

    n_small = (CONV + 2) * 128
    small_terms = _split3(jnp.concatenate([conv_w[0].reshape(-1), b_gate[0].reshape(-1)]))
    small_rows = jnp.pad(small_terms, ((0, 16 - 3), (0, D - n_small)))
    own_first = jnp.concatenate([w_in[0].T.astype(_MXU), small_rows.astype(_MXU)], axis=0)
    own_second = jnp.concatenate([
        jnp.pad(pool_proj[0].T, ((0, 0), (0, D - PW))).astype(_MXU), lru_proj[0].astype(_MXU), w_out[0].astype(_MXU),
    ], axis=0)
    own_third = jnp.concatenate([
        w_ffn_in[0].T.astype(_MXU), jnp.pad(w_ple_proj[0].T, ((0, 0), (0, D - PLE))).astype(_MXU),
        w_ffn_out[0].astype(_MXU), w_ple_gate[0].astype(_MXU),
    ], axis=0)
    u, z, gw_first, gw = _inproj_fwd(x2, norm1_g, own_first, own_second)
    off = W_OFF["f32s"][0]
    st = gw_first[:, off:off + 3, :n_small].astype(F32)
    sf = st[:, 0] + st[:, 1] + st[:, 2]
    conv_w_full = sf[:, :CONV * 128].reshape(NDEV, CONV, 128).transpose(1, 0, 2).reshape(CONV, D)
    b_gate_full = sf[:, CONV * 128:].reshape(NDEV, 2, 128).transpose(1, 0, 2).reshape(2, D)

    small = (pool_w[0].astype(_MXU), pool_scale, conv_w_full, conv_b, w_rg[0].astype(_MXU), b_rg[0],
             w_ig[0].astype(_MXU), b_ig[0], lru_lambda, b_gate_full)

    h, yl, merged, y_pool, y_lru, h1, *saved, gw_third = _mixer_fwd(z, x2, gw, small, own_third)
    v, gf, uf, h2 = _ffn_fwd(h1, norm2_g, gw_third)

    dh2, loss_blk, g_ple_norm, g_final, part_wpg, part_ple = _ple_loss_fwd_bwd(h2, p2, tgt, ple_norm_g,
                                                                               final_g.reshape(1, D), gw_third)
    dff, part_wffo = _ffn_bwd_hidden(dh2, gf, uf, gw_third)
    dh1, g_norm2 = _proj_norm_bwd(dff, h1, dh2, norm2_g, gw_third, "wffn", 2 * FF, "ffn_bwd_in")
    (dz, d_ylru, d_ypool, mixed, g_bgate, g_lam, g_brg, g_big, g_convb, g_convw, g_pscale, g_poolw, g_wrg,
     g_wig) = _mixer_bwd(dh1, z, h, y_pool, y_lru, saved, gw, small)
    grad_x, g_norm1, part_wout = _proj_norm_bwd(dz, x2, dh1, norm1_g, gw_first, "win", NIN, "inproj_bwd", lhs=merged)

    small_shapes = [(1, D), (GROUPS, GD, GD), (1, PW), (1, D), (HEADS, HD, HD), (HEADS, HD), (HEADS, HD, HD),
                    (HEADS, HD), (1, D), (1, D), (1, D), (1, D), (2, D), (CONV, D), (1, 1)]
    small_part = _small_pack([g_norm1, g_poolw, g_pscale, g_convb, g_wrg, g_brg, g_wig, g_big, g_lam, g_norm2,
                              g_ple_norm, g_final, g_bgate, g_convw, loss_blk[0:1, 0:1]])
    riders = [_grad_matmul(yl, d_ylru, "grad_lru_proj"), part_wout, _grad_matmul(d_ypool, mixed, "grad_pool_proj")]
    rs_wffn = _grad_matmul_rs(dff, v, "grad_w_ffn_in", 704, extras=[part_wffo, part_wpg, part_ple], narrow=_MXU)
    rs_win = _grad_matmul_rs(dz, u, "grad_w_in", 576, extras=riders + [small_part], narrow=_MXU, tail=SMALL_ROWS)

    def reduced(parts, name):
        return [_sum_arrays([t_own, landed[0], landed[1], landed[2]], "rs_sum_" + name + str(n))
                for n, (t_own, landed) in enumerate(parts)]

    red_wffn, = reduced(rs_wffn, "wffn")
    red_win, red_small = reduced(rs_win, "win")
    g_w_in = red_win[:576].T
    g_w_ffn_in = red_wffn[:704].T
    g_w_ffn_out = red_wffn[704:1056]
    g_w_ple_gate = red_wffn[1056:1184]
    g_w_ple_proj = red_wffn[1184:1312, :PLE].T
    g_lru_proj, g_w_out = red_win[576:704], red_win[704:832]
    g_pool_proj = red_win[832:960, :PW].T
    small_red = _all_gather_small(red_small)
    (gs_norm1, gs_poolw, gs_pscale, gs_convb, gs_wrg, gs_brg, gs_wig, gs_big, gs_lam, gs_norm2, gs_ple_norm,
     gs_final, gs_bgate, gs_convw, loss_sum) = _small_unpack(small_red, small_shapes)
    loss = loss_sum[0, 0]
    g_b_gate = lax.dynamic_slice_in_dim(gs_bgate, me * 128, 128, axis=1)
    g_conv_w = lax.dynamic_slice_in_dim(gs_convw, me * 128, 128, axis=1)

    grads = {
        "norm1_g": gs_norm1, "w_in": g_w_in[None], "b_gate": g_b_gate[None], "pool_w": gs_poolw[None],
        "pool_scale": gs_pscale, "pool_proj": g_pool_proj[None], "conv_w": g_conv_w[None], "conv_b": gs_convb,
        "w_rg": gs_wrg[None], "b_rg": gs_brg[None], "w_ig": gs_wig[None], "b_ig": gs_big[None], "lru_lambda": gs_lam,
        "lru_proj": g_lru_proj[None], "w_out": g_w_out[None], "norm2_g": gs_norm2, "w_ffn_in": g_w_ffn_in[None],
        "w_ffn_out": g_w_ffn_out[None], "ple_norm_g": gs_ple_norm, "w_ple_gate": g_w_ple_gate[None],
        "w_ple_proj": g_w_ple_proj[None], "final_g": gs_final.reshape(D),
    }
    weights = dict(norm1_g=norm1_g, w_in=w_in, b_gate=b_gate, pool_w=pool_w, pool_scale=pool_scale, pool_proj=pool_proj,
                   conv_w=conv_w, conv_b=conv_b, w_rg=w_rg, b_rg=b_rg, w_ig=w_ig, b_ig=b_ig, lru_lambda=lru_lambda,
                   lru_proj=lru_proj, w_out=w_out, norm2_g=norm2_g, w_ffn_in=w_ffn_in, w_ffn_out=w_ffn_out,
                   ple_norm_g=ple_norm_g, w_ple_gate=w_ple_gate, w_ple_proj=w_ple_proj, final_g=final_g)
    moments_m = dict(norm1_g=m_norm1_g, w_in=m_w_in, b_gate=m_b_gate, pool_w=m_pool_w, pool_scale=m_pool_scale,
                     pool_proj=m_pool_proj, conv_w=m_conv_w, conv_b=m_conv_b, w_rg=m_w_rg, b_rg=m_b_rg, w_ig=m_w_ig,
                     b_ig=m_b_ig, lru_lambda=m_lru_lambda, lru_proj=m_lru_proj, w_out=m_w_out, norm2_g=m_norm2_g,
                     w_ffn_in=m_w_ffn_in, w_ffn_out=m_w_ffn_out, ple_norm_g=m_ple_norm_g, w_ple_gate=m_w_ple_gate,
                     w_ple_proj=m_w_ple_proj, final_g=m_final_g)
    moments_v = dict(norm1_g=v_norm1_g, w_in=v_w_in, b_gate=v_b_gate, pool_w=v_pool_w, pool_scale=v_pool_scale,
                     pool_proj=v_pool_proj, conv_w=v_conv_w, conv_b=v_conv_b, w_rg=v_w_rg, b_rg=v_b_rg, w_ig=v_w_ig,
                     b_ig=v_b_ig, lru_lambda=v_lru_lambda, lru_proj=v_lru_proj, w_out=v_w_out, norm2_g=v_norm2_g,
                     w_ffn_in=v_w_ffn_in, w_ffn_out=v_w_ffn_out, ple_norm_g=v_ple_norm_g, w_ple_gate=v_w_ple_gate,
                     w_ple_proj=v_w_ple_proj, final_g=v_final_g)
    names = list(weights)
    big = ("w_in", "w_ffn_in", "w_ffn_out", "lru_proj", "w_out", "w_ple_gate", "pool_proj", "w_ple_proj")
    slab_space = {"w_in": red_win[:576], "w_ffn_in": red_wffn[:704]}
    delta, new_m, new_v = {}, {}, {}
    for n in big:
        sh = weights[n].shape
        if n in slab_space:
            as2d = lambda a: a[0].T
            back = lambda a: a.T[None]
            g2d = slab_space[n]
        else:
            as2d = lambda a: a.reshape(sh[-2], sh[-1])
            back = lambda a: a.reshape(sh)
            g2d = as2d(grads[n])
        d_, m_, v_ = _adamw(as2d(weights[n]), g2d, as2d(moments_m[n]), as2d(moments_v[n]), "adamw_" + n)
        delta[n], new_m[n], new_v[n] = back(d_), back(m_), back(v_)
    rest = [n for n in names if n not in big]
    rest_shapes = [weights[n].shape for n in rest]
    packed = [_small_pack([src[n] for n in rest]) for src in (weights, grads, moments_m, moments_v)]
    d_, m_, v_ = _adamw(*packed, "adamw_small")
    for n, a, b_, c_ in zip(rest, _small_unpack(d_, rest_shapes), _small_unpack(m_, rest_shapes),
                            _small_unpack(v_, rest_shapes)):
        delta[n], new_m[n], new_v[n] = a, b_, c_

    return (loss, grad_x[None], *[grads[n] for n in names], *[delta[n] for n in names],
            *[new_m[n] for n in names], *[new_v[n] for n in names])
```

```python
import functools
import math

import jax
import jax.numpy as jnp
from jax import lax
from jax.experimental import pallas as pl
from jax.experimental.pallas import tpu as pltpu

F32 = jnp.float32
D = 1024
NIN = 4608
PW = 512
FF = 2816
MXU_DIM = 256
FF_CHUNKS = tuple((c0, min(3 * MXU_DIM, FF - c0)) for c0 in range(0, FF, 3 * MXU_DIM))
PLE = 256
HEADS, HD = 8, 128
GROUPS, GD = 4, 128
WINDOWS = (2, 4, 8, 16)
HALO = 16
CONV = 4
EPS = 1e-6
LRU_C = 8.0
NDEV = 8
MESH = pl.DeviceIdType.MESH

ADAM_LR, ADAM_B1, ADAM_B2, ADAM_EPS, ADAM_WD, ADAM_STEP = 0.001, 0.9, 0.999, 1e-08, 0.01, 10

_MXU = jnp.bfloat16
TM = 512
TM_SEQ = 256
VMEM_LIMIT = 56 * 1024 * 1024
VMEM_LIMIT_LARGE = 60 * 1024 * 1024
W_FIRST = (("win", 576), ("f32s", 16))
W_SECOND = (("pproj", 128), ("lru", 128), ("wout", 128))
W_THIRD = (("wffn", 704), ("ple", 128), ("wffo", 352), ("wpg", 128))
W_OFF = {}
for _slabs in (W_FIRST, W_SECOND, W_THIRD):
    _o = 0
    for _n, _r in _slabs:
        W_OFF[_n] = (_o, _r)
        _o += _r
SMALL_ROWS = 48


def _cparams(n_axes=1, vmem=VMEM_LIMIT):
    return pltpu.CompilerParams(dimension_semantics=("arbitrary",) * n_axes, vmem_limit_bytes=vmem)


def _my_pos():
    return lax.axis_index("x"), lax.axis_index("y"), lax.axis_index("c")


def _nt(a, b):
    return lax.dot_general(a, b, (((1,), (1,)), ((), ())), preferred_element_type=F32)


def _nn(a, b):
    return lax.dot_general(a, b, (((1,), (0,)), ((), ())), preferred_element_type=F32)


def _tn(a, b):
    return lax.dot_general(a, b, (((0,), (0,)), ((), ())), preferred_element_type=F32)


def _sigmoid(x):
    return 0.5 * jnp.tanh(0.5 * x) + 0.5


_GELU_K = math.sqrt(2.0 / math.pi)


def _gelu_and_grad(x):
    x2 = x * x
    inner = _GELU_K * (x + 0.044715 * x2 * x)
    t = jnp.tanh(inner)
    g = 0.5 * x * (1.0 + t)
    dg = 0.5 * (1.0 + t) + 0.5 * x * (1.0 - t * t) * _GELU_K * (1.0 + 3.0 * 0.044715 * x2)
    return g, dg


def _softplus_neg(lam):
    x = -lam
    t = jnp.exp(-jnp.abs(x))
    u = 1.0 + t
    l1p = jnp.where(u == 1.0, t, jnp.log(u) * t / (u - 1.0))
    return jnp.maximum(x, 0.0) + l1p, _sigmoid(x)


def _start_slab_loads(g_ref, name, dst_ref, sems, base, width=D):
    off, rows = W_OFF[name]
    copies = []
    for k in range(NDEV):
        if width == D:
            src = g_ref.at[k, pl.ds(off, rows), :]
        else:
            src = g_ref.at[k, pl.ds(off, rows), pl.ds(0, width)]
        cp = pltpu.make_async_copy(src, dst_ref.at[pl.ds(k * rows, rows), :], sems.at[base + k])
        cp.start()
        copies.append(cp)
    return copies


def _load_weights(g_ref, items, sems):
    copies = []
    for n, (name, dst, width) in enumerate(items):
        copies += _start_slab_loads(g_ref, name, dst, sems, n * NDEV, width)
    for cp in copies:
        cp.wait()


class _Gather:
    def __init__(self, own_ref, out_ref, stage, send_sems, recv_sems, local_sem):
        x, y, c = _my_pos()
        self.c = c
        self.me, self.sibling = (x, y, c), (x, y, 1 - c)
        self.chips = [(1 - x, y), (x, 1 - y), (1 - x, 1 - y)]
        self.own_ref, self.out_ref, self.stage = own_ref, out_ref, stage
        self.send_sems, self.recv_sems = send_sems, recv_sems
        self.mine = pltpu.make_async_copy(stage, self.slab(*self.me), local_sem)
        self.first = [self.copy(0, self.me, self.sibling, src=stage)] + [
            self.copy(1 + j, self.me, (*chip, c), src=stage) for j, chip in enumerate(self.chips)]
        self.passed = [self.copy(4 + j, (*chip, c), self.sibling) for j, chip in enumerate(self.chips)]

    def slab(self, px, py, pc):
        return self.out_ref.at[4 * px + 2 * py + pc]

    def copy(self, k, block, to, src=None):
        return pltpu.make_async_remote_copy(
            src_ref=self.slab(*block) if src is None else src, dst_ref=self.slab(*block),
            send_sem=self.send_sems.at[k], recv_sem=self.recv_sems.at[k], device_id=to, device_id_type=MESH)

    def send_mine(self, far=True):
        pltpu.sync_copy(self.own_ref, self.stage)
        self.mine.start()
        for cp in self.first[:3]:
            cp.start()
        if far:
            self.send_far()

    def send_far(self):
        self.first[3].start()

    def pass_on(self, js):
        for j in js:
            self.copy(1 + j, (*self.chips[j], self.c), self.me).wait_recv()
            self.passed[j].start()

    def wait_sibling(self):
        self.copy(0, self.sibling, self.me).wait_recv()

    def wait_passed(self, js):
        for j in js:
            self.copy(4 + j, (*self.chips[j], 1 - self.c), self.me).wait_recv()

    def finish_sends(self):
        for cp in self.first + self.passed:
            cp.wait_send()
        self.mine.wait()


def _all_gather_small(piece):
    rows = piece.shape[0]

    def body(p_ref, out_ref, send_sems, recv_sems, local_sem):
        x, y, c = _my_pos()
        me = 4 * x + 2 * y + c
        mine = pltpu.make_async_copy(p_ref, out_ref.at[pl.ds(pl.multiple_of(me * rows, 8), rows), :], local_sem)
        mine.start()
        sends = []
        peers = []
        for r in range(1, NDEV):
            px = 1 - x if (r >> 2) & 1 else x
            py = 1 - y if (r >> 1) & 1 else y
            pc = 1 - c if r & 1 else c
            peers.append((px, py, pc))
            cp = pltpu.make_async_remote_copy(
                src_ref=p_ref, dst_ref=out_ref.at[pl.ds(pl.multiple_of(me * rows, 8), rows), :],
                send_sem=send_sems.at[r - 1], recv_sem=recv_sems.at[r - 1], device_id=(px, py, pc),
                device_id_type=MESH)
            cp.start()
            sends.append(cp)
        for r, (px, py, pc) in enumerate(peers):
            them = 4 * px + 2 * py + pc
            pltpu.make_async_remote_copy(
                src_ref=p_ref, dst_ref=out_ref.at[pl.ds(pl.multiple_of(them * rows, 8), rows), :],
                send_sem=send_sems.at[r], recv_sem=recv_sems.at[r], device_id=(px, py, pc),
                device_id_type=MESH).wait_recv()
        for cp in sends:
            cp.wait_send()
        mine.wait()

    return pl.pallas_call(
        body, name="ag_small",
        out_shape=jax.ShapeDtypeStruct((NDEV * rows, piece.shape[1]), piece.dtype),
        in_specs=[pl.BlockSpec(memory_space=pltpu.VMEM)],
        out_specs=pl.BlockSpec(memory_space=pl.ANY),
        scratch_shapes=[pltpu.SemaphoreType.DMA((7,)), pltpu.SemaphoreType.DMA((7,)), pltpu.SemaphoreType.DMA],
    )(piece)


def _row_block(rows, target=512, mult=8):
    b = min(rows, target) // mult * mult
    while rows % b:
        b -= mult
    return b


def _sum_arrays(arrs, name, narrow=None, target=704):
    rows, cols = arrs[0].shape
    br = _row_block(rows, target, 16)
    n = len(arrs)

    def body(*refs):
        acc = refs[0][...].astype(F32)
        for r in refs[1:n]:
            acc = acc + r[...].astype(F32)
        refs[n][...] = acc
        if narrow is not None:
            refs[n + 1][...] = acc.astype(narrow)

    spec = pl.BlockSpec((br, cols), lambda i: (i, 0))
    shape = jax.ShapeDtypeStruct((rows, cols), F32)
    if narrow is None:
        out_shape, out_specs = shape, spec
    else:
        out_shape, out_specs = (shape, jax.ShapeDtypeStruct((rows, cols), narrow)), (spec, spec)
    return pl.pallas_call(
        body, name=name, grid=(rows // br,), out_shape=out_shape,
        in_specs=[spec] * n, out_specs=out_specs, compiler_params=_cparams(1),
    )(*arrs)


def _adamw(w, g, m, v, name):
    rows, cols = w.shape
    br = _row_block(rows, 256)

    def body(w_ref, g_ref, m_ref, v_ref, d_ref, nm_ref, nv_ref):
        g_ = g_ref[...]
        m_ = ADAM_B1 * m_ref[...] + (1.0 - ADAM_B1) * g_
        v_ = ADAM_B2 * v_ref[...] + (1.0 - ADAM_B2) * (g_ * g_)
        m_hat = m_ / (1.0 - ADAM_B1 ** ADAM_STEP)
        v_hat = v_ / (1.0 - ADAM_B2 ** ADAM_STEP)
        d_ref[...] = -ADAM_LR * (m_hat / (jnp.sqrt(v_hat) + ADAM_EPS) + ADAM_WD * w_ref[...])
        nm_ref[...] = m_
        nv_ref[...] = v_

    spec = pl.BlockSpec((br, cols), lambda i: (i, 0))
    shape = jax.ShapeDtypeStruct((rows, cols), F32)
    return pl.pallas_call(
        body, name=name, grid=(rows // br,), out_shape=(shape, shape, shape),
        in_specs=[spec] * 4, out_specs=(spec, spec, spec), compiler_params=_cparams(1),
    )(w, g, m, v)


_CHIP_FLIPS = (2, 3, 1, 0)


def _grad_matmul(lhs, rhs, name):
    s, r = lhs.shape
    k = rhs.shape[1]
    tm = min(4 * TM, s)

    def body(l_ref, r_ref, o_ref):
        @pl.when(pl.program_id(0) == 0)
        def _():
            o_ref[...] = jnp.zeros_like(o_ref)

        o_ref[:, pl.ds(0, k)] += _tn(l_ref[...].astype(_MXU), r_ref[...].astype(_MXU))

    return pl.pallas_call(
        body, name=name, grid=(s // tm,),
        out_shape=jax.ShapeDtypeStruct((r, D), F32),
        in_specs=[pl.BlockSpec((tm, r), lambda i: (i, 0)), pl.BlockSpec((tm, k), lambda i: (i, 0))],
        out_specs=pl.BlockSpec((r, D), lambda i: (0, 0)),
        compiler_params=_cparams(1),
    )(lhs, rhs)


def _grad_matmul_rs(lhs, rhs, name, rows, extras=(), narrow=None, tail=0):
    s, r8 = lhs.shape
    k = rhs.shape[1]
    cpb = 1
    nblk = 4 // cpb
    nx = len(extras)
    ers = [e.shape[0] // NDEV for e in extras]
    er = sum(ers)
    srows = rows + er
    brows = 2 * cpb * srows
    groups = [(0, srows - tail, F32 if narrow is None else narrow)] + ([(srows - tail, tail, F32)] if tail else [])
    ng = len(groups)
    resident = ((2 * brows + 2 * srows) * D * 4 + sum(2 * n * D * jnp.dtype(dt).itemsize for _, n, dt in groups)
                + 2 * cpb * rows * k * 4)
    per_token = 2 * (2 * cpb * rows * lhs.dtype.itemsize + k * rhs.dtype.itemsize)
    fitting = [t for t in (4 * TM, 2 * TM, TM)
               if s % t == 0 and 2 * t <= s and resident + t * per_token <= VMEM_LIMIT_LARGE * 9 // 10]
    tm = fitting[0] if fitting else min(TM, s)
    nt = s // tm
    mid = min(nt - 1, max(1, nt // 6))

    def flip_of(p):
        return jnp.where(p == 0, 2, jnp.where(p == 1, 3, jnp.where(p == 2, 1, 0)))

    def block_col(b):
        x, y, _ = _my_pos()
        return (2 * x + y) ^ flip_of(b)

    def body(*refs):
        l_ref, r_ref = refs[:2]
        x_refs = refs[2:2 + nx]
        rest = refs[2 + nx:]
        town_ref = rest[0]
        lici_refs = rest[1:1 + ng]
        acc, stage = rest[1 + ng:3 + ng]
        send_bufs = rest[3 + ng:3 + 2 * ng]
        dsend, drecv, isend, irecv, xsem = rest[3 + 2 * ng:]
        b = pl.program_id(0)
        i = pl.program_id(1)
        x, y, c = _my_pos()
        mine = 2 * x + y
        sibling = (x, y, 1 - c)

        def chip_at(p):
            return mine ^ _CHIP_FLIPS[p]

        def slab_rows(p, parity):
            within = 0 if cpb == 1 else (chip_at(p) & 1) * 2
            return pl.ds(pl.multiple_of((within + parity) * srows, 8), srows)

        def push(p, slot):
            return pltpu.make_async_remote_copy(
                src_ref=acc.at[slot, slab_rows(p, 1 - c), :], dst_ref=stage.at[p % 2],
                send_sem=dsend.at[p], recv_sem=drecv.at[p], device_id=sibling, device_id_type=MESH)

        def ici(p):
            ch = chip_at(p)
            return [pltpu.make_async_remote_copy(
                src_ref=send_bufs[g].at[p % 2], dst_ref=lici_refs[g].at[p], send_sem=isend.at[3 * g + p],
                recv_sem=irecv.at[3 * g + p], device_id=(ch >> 1, ch & 1, c), device_id_type=MESH) for g in range(ng)]

        def extra_loads(p, slot):
            copies = []
            within = 0 if cpb == 1 else (chip_at(p) & 1) * 2
            for parity in range(2):
                off = rows
                for n, (x_ref, e) in enumerate(zip(x_refs, ers)):
                    src = x_ref.at[pl.ds(pl.multiple_of((2 * chip_at(p) + parity) * e, 8), e), :]
                    dst = acc.at[slot, pl.ds(pl.multiple_of((within + parity) * srows + off, 8), e), :]
                    copies.append(pltpu.make_async_copy(src, dst, xsem.at[(p * 2 + parity) * nx + n]))
                    off += e
            return copies

        def combine(p, slot):
            push(p, slot).wait_recv()
            total = acc[slot, slab_rows(p, c), :] + stage[p % 2]
            if p == 3:
                stage[p % 2] = total
                pltpu.sync_copy(stage.at[p % 2], town_ref)
            else:
                if p == 2:
                    for cp in ici(0):
                        cp.wait_send()
                for g, (r0, n, dt) in enumerate(groups):
                    send_bufs[g][p % 2] = total[r0:r0 + n, :].astype(dt)
                for cp in ici(p):
                    cp.start()

        for bb in range(nblk):
            slot = bb % 2
            positions = list(range(bb * cpb, (bb + 1) * cpb))

            @pl.when(jnp.logical_and(b == bb, i == 0))
            def _(bb=bb, slot=slot, positions=positions):
                if bb >= 2:
                    for p in range((bb - 2) * cpb, (bb - 1) * cpb):
                        push(p, slot).wait_send()
                for q in range(2 * cpb):
                    acc[slot, pl.ds(q * srows, rows), :] = jnp.zeros((rows, D), F32)
                for p in positions:
                    for cp in extra_loads(p, slot):
                        cp.start()

            if bb >= 1:
                @pl.when(jnp.logical_and(b == bb, i == mid))
                def _(bb=bb):
                    for p in range((bb - 1) * cpb, bb * cpb):
                        combine(p, (bb - 1) % 2)

        res = _tn(l_ref[...].astype(_MXU), r_ref[...].astype(_MXU))
        slot_now = b % 2
        for q in range(2 * cpb):
            acc[slot_now, pl.ds(q * srows, rows), pl.ds(0, k)] += res[q * rows:(q + 1) * rows, :]

        for bb in range(nblk):
            slot = bb % 2
            positions = list(range(bb * cpb, (bb + 1) * cpb))

            @pl.when(jnp.logical_and(b == bb, i == nt - 1))
            def _(bb=bb, slot=slot, positions=positions):
                for p in positions:
                    for cp in extra_loads(p, slot):
                        cp.wait()
                for p in positions:
                    push(p, slot).start()
                if bb == nblk - 1:
                    for p in positions:
                        combine(p, slot)
                    for p in range(max(0, (nblk - 2) * cpb), 4):
                        push(p, slot).wait_send()
                    for p in range(1, 3):
                        for cp in ici(p):
                            cp.wait_send()
                    for p in range(3):
                        for cp in ici(p):
                            cp.wait_recv()

    in_specs = [pl.BlockSpec((tm, 2 * cpb * rows), lambda b, i: (i, block_col(b))),
                pl.BlockSpec((tm, k), lambda b, i: (i, 0))]
    any_spec = pl.BlockSpec(memory_space=pl.ANY)
    in_specs += [any_spec] * nx
    args = [lhs, rhs, *extras]
    outs = pl.pallas_call(
        body, name=name, grid=(nblk, nt),
        out_shape=(jax.ShapeDtypeStruct((srows, D), F32),)
        + tuple(jax.ShapeDtypeStruct((3, n, D), dt) for _, n, dt in groups),
        in_specs=in_specs, out_specs=(any_spec,) * (1 + ng),
        scratch_shapes=[pltpu.VMEM((2, brows, D), F32), pltpu.VMEM((2, srows, D), F32)]
        + [pltpu.VMEM((2, n, D), dt) for _, n, dt in groups]
        + [pltpu.SemaphoreType.DMA((4,)), pltpu.SemaphoreType.DMA((4,)), pltpu.SemaphoreType.DMA((3 * ng,)),
           pltpu.SemaphoreType.DMA((3 * ng,)), pltpu.SemaphoreType.DMA((max(1, 8 * nx),))],
        compiler_params=_cparams(2, VMEM_LIMIT_LARGE),
    )(*args)
    t_own = outs[0]
    return [(t_own[r0:r0 + n], landed) for (r0, n, _), landed in zip(groups, outs[1:])]


def _inproj_fwd(x, g1, own_first, own_second):
    s = x.shape[0]
    tm = min(2 * TM, s // 2)
    nt = s // tm
    assert nt % 2 == 0
    rows1, rows2 = own_first.shape[0], own_second.shape[0]
    wrows = W_OFF["win"][1]
    cw = 2 * wrows

    def chip_col(b):
        px, py, _ = _my_pos()
        return (2 * px + py) ^ jnp.where(b == 0, 0, jnp.where(b == 1, 2, jnp.where(b == 2, 1, 3)))

    def body(x_ref, g1_ref, own1_ref, own2_ref, u_ref, z_ref, gw1_ref, gw2_ref, w_vmem, u_buf, stage1, stage2, sems,
             usem, send1, recv1, local1, send2, recv2, local2):
        b = pl.program_id(0)
        i = pl.program_id(1)
        ga = _Gather(own1_ref, gw1_ref, stage1, send1, recv1, local1)
        gb = _Gather(own2_ref, gw2_ref, stage2, send2, recv2, local2)
        c = ga.c

        def load_chip(px, py, own_too):
            copies = []
            for pc in range(2):
                dst = w_vmem.at[pl.ds(pc * wrows, wrows), :]
                copies.append(pltpu.make_async_copy(gw1_ref.at[4 * px + 2 * py + pc, pl.ds(0, wrows), :], dst,
                                                    sems.at[pc]))
            if own_too:
                mine_dst = w_vmem.at[pl.ds(pl.multiple_of(c * wrows, 16), wrows), :]
                copies[0] = pltpu.make_async_copy(own1_ref.at[pl.ds(0, wrows), :], mine_dst, sems.at[0])
                theirs_dst = w_vmem.at[pl.ds(pl.multiple_of((1 - c) * wrows, 16), wrows), :]
                copies[1] = pltpu.make_async_copy(gw1_ref.at[4 * px + 2 * py + 1 - c, pl.ds(0, wrows), :], theirs_dst,
                                                  sems.at[1])
            for cp in copies:
                cp.start()
            for cp in copies:
                cp.wait()

        @pl.when(jnp.logical_and(b == 0, i == 0))
        def _():
            ga.send_mine(far=False)
            ga.wait_sibling()
            load_chip(ga.me[0], ga.me[1], True)

        @pl.when(jnp.logical_and(b == 0, i == nt // 4))
        def _():
            ga.send_far()
            gb.send_mine()

        @pl.when(jnp.logical_and(b == 0, i == (3 * nt) // 4))
        def _():
            ga.pass_on((0, 1))

        @pl.when(jnp.logical_and(b == 1, i == (3 * nt) // 4))
        def _():
            ga.pass_on((2,))

        for j in range(3):
            @pl.when(jnp.logical_and(b == j + 1, i == 0))
            def _(j=j):
                ga.wait_passed((j,))
                load_chip(ga.chips[j][0], ga.chips[j][1], False)

        @pl.when(jnp.logical_and(b == 2, i == nt // 2))
        def _():
            gb.pass_on((0, 1))

        @pl.when(jnp.logical_and(b == 3, i == (3 * nt) // 4))
        def _():
            gb.pass_on((2,))

        slot = i % 2

        def u_write(t, sl):
            return pltpu.make_async_copy(u_buf.at[sl], u_ref.at[pl.ds(pl.multiple_of(t * tm, tm), tm), :], usem.at[sl])

        def u_read(t, sl):
            return pltpu.make_async_copy(u_ref.at[pl.ds(pl.multiple_of(t * tm, tm), tm), :], u_buf.at[sl], usem.at[sl])

        @pl.when(b == 0)
        def _():
            @pl.when(i >= 2)
            def _():
                u_write(i - 2, slot).wait()

            xv = x_ref[...]
            inv = lax.rsqrt(jnp.mean(xv * xv, axis=-1, keepdims=True) + EPS)
            u_buf[slot] = (xv * inv * g1_ref[...]).astype(_MXU)
            u_write(i, slot).start()

            @pl.when(i == nt - 1)
            def _():
                u_write(i - 1, 1 - slot).wait()
                u_write(i, slot).wait()
                u_read(0, 0).start()

        @pl.when(b > 0)
        def _():
            u_read(i, slot).wait()

            @pl.when(jnp.logical_or(b < 3, i < nt - 1))
            def _():
                u_read((i + 1) % nt, 1 - slot).start()

        z_ref[...] = _nt(u_buf[slot], w_vmem[...])

        @pl.when(jnp.logical_and(b == 3, i == nt - 1))
        def _():
            ga.finish_sends()
            gb.wait_sibling()
            gb.wait_passed((0, 1, 2))
            gb.finish_sends()

    any_spec = pl.BlockSpec(memory_space=pl.ANY)
    dma7 = pltpu.SemaphoreType.DMA((7,))
    return pl.pallas_call(
        body, name="inproj_fwd", grid=(4, nt),
        out_shape=(jax.ShapeDtypeStruct((s, D), _MXU), jax.ShapeDtypeStruct((s, NIN), F32),
                   jax.ShapeDtypeStruct((NDEV, rows1, D), own_first.dtype),
                   jax.ShapeDtypeStruct((NDEV, rows2, D), own_second.dtype)),
        in_specs=[pl.BlockSpec((tm, D), lambda b, i: (jnp.where(b == 0, i, nt - 1), 0)),
                  pl.BlockSpec((1, D), lambda b, i: (0, 0)), any_spec, any_spec],
        out_specs=(any_spec, pl.BlockSpec((tm, cw), lambda b, i: (i, chip_col(b))), any_spec, any_spec),
        scratch_shapes=[pltpu.VMEM((cw, D), _MXU), pltpu.VMEM((2, tm, D), _MXU), pltpu.VMEM((rows1, D), own_first.dtype),
                        pltpu.VMEM((rows2, D), own_second.dtype), pltpu.SemaphoreType.DMA((2,)),
                        pltpu.SemaphoreType.DMA((2,)),
                        dma7, dma7, pltpu.SemaphoreType.DMA, dma7, dma7, pltpu.SemaphoreType.DMA],
        compiler_params=_cparams(2),
    )(x, g1, own_first, own_second)


def _pool_tile(pbuf, t0, tm, pw_ref, scale_ref):
    t = t0 + lax.broadcasted_iota(jnp.int32, (tm, GD), 0)
    pooled, mixed_pre = [], []
    for g, w in enumerate(WINDOWS):
        cs = pl.ds(g * GD, GD)
        cur = pbuf[pl.ds(HALO, tm), cs]
        acc = cur
        for d in range(1, w):
            acc = acc + pbuf[pl.ds(HALO - d, tm), cs]
        cnt = jnp.minimum(t + 1, w).astype(F32)
        pg = acc / cnt - cur
        pooled.append(pg)
        mixed_pre.append(_nn(pg.astype(_MXU), pw_ref[g]))
    return pooled, mixed_pre


def _lru_gates_head(hh, lbuf, start, tm, cw_ref, cb_ref, wrg_ref, brg_ref, wig_ref, big_ref, sp):
    cs = pl.ds(hh * HD, HD)
    xc = cb_ref[:, cs] + cw_ref[pl.ds(CONV - 1, 1), cs] * lbuf[pl.ds(HALO, tm), cs]
    for k in range(CONV - 1):
        xc = xc + cw_ref[pl.ds(k, 1), cs] * lbuf[pl.ds(HALO - (CONV - 1) + k, tm), cs]
    xcm = xc.astype(_MXU)
    r = _sigmoid(_nn(xcm, wrg_ref[hh]) + brg_ref[pl.ds(hh, 1), :])
    ig = _sigmoid(_nn(xcm, wig_ref[hh]) + big_ref[pl.ds(hh, 1), :])
    a = jnp.exp(-LRU_C * r * sp[:, hh * HD:(hh + 1) * HD])
    one_m = 1.0 - a * a
    live = jnp.logical_and(one_m > 0.0, jnp.logical_not(start))
    inv_mult = lax.rsqrt(jnp.where(live, one_m, 1.0))
    mult = jnp.where(live, one_m * inv_mult, jnp.where(start, 1.0, 0.0))
    return xc, r, ig, a, live, inv_mult, mult


def _seg_layout(tm):
    seg = tm // 8
    return seg, seg + 8


def _to_segments(dst_ref, hh, val, tm):
    seg, pitch = _seg_layout(tm)
    for s in range(8):
        dst_ref[hh, pl.ds(s * pitch, seg), :] = val[s * seg:(s + 1) * seg, :]


def _from_segments(src_ref, hh, tm):
    seg, pitch = _seg_layout(tm)
    return jnp.concatenate([src_ref[hh, pl.ds(s * pitch, seg), :] for s in range(8)], axis=0)


def _segment_scan(a_ref, b_ref, out_ref, hk, pk, carry_ref, tm, reverse):
    seg, pitch = _seg_layout(tm)
    row = lax.broadcasted_iota(jnp.int32, (8, HD), 0)
    order = range(seg - 1, -1, -1) if reverse else range(seg)
    for hh in range(HEADS):
        cs = pl.ds(hh * HD, HD)
        if reverse:
            a0 = a_ref[hh, pl.ds(0, 8, stride=pitch), :]
            a_wrap = jnp.where(row <= 6, pltpu.roll(a0, 7, 0), 1.0)
        hv = jnp.zeros((8, HD), F32)
        pv = jnp.ones((8, HD), F32)
        for k in order:
            if not reverse:
                av = a_ref[hh, pl.ds(k, 8, stride=pitch), :]
            elif k + 1 < seg:
                av = a_ref[hh, pl.ds(k + 1, 8, stride=pitch), :]
            else:
                av = a_wrap
            hv = av * hv + b_ref[hh, pl.ds(k, 8, stride=pitch), :]
            pv = av * pv
            hk[hh, pl.ds(8 * k, 8), :] = hv
            pk[hh, pl.ds(8 * k, 8), :] = pv
        for d in (1, 2, 4):
            if reverse:
                keep, sh = row < 8 - d, 8 - d
            else:
                keep, sh = row >= d, d
            hv = hv + pv * jnp.where(keep, pltpu.roll(hv, sh, 0), 0.0)
            pv = pv * jnp.where(keep, pltpu.roll(pv, sh, 0), 1.0)
        cin = carry_ref[:, cs]
        ends = hv + pv * cin
        if reverse:
            enter = jnp.where(row <= 6, pltpu.roll(ends, 7, 0), cin)
            carry_ref[:, cs] = jnp.broadcast_to((a0 * ends)[0:1, :], (8, HD))
        else:
            enter = jnp.where(row >= 1, pltpu.roll(ends, 1, 0), cin)
            carry_ref[:, cs] = jnp.broadcast_to(ends[7:8, :], (8, HD))
        for k in range(seg):
            out_ref[hh, pl.ds(k, 8, stride=pitch), :] = hk[hh, pl.ds(8 * k, 8), :] + pk[hh, pl.ds(8 * k, 8), :] * enter


def _mixer_fwd(z, x, gw, small, own_third):
    s = x.shape[0]
    tm = min(TM_SEQ, s)
    nt = s // tm
    rows3 = own_third.shape[0]
    (pool_w, pool_scale, conv_w, conv_b, w_rg, b_rg, w_ig, b_ig, lam, b_gate) = small

    def body(z_ref, x_ref, gw_ref, own3_ref, pw_ref, ps_ref, cw_ref, cb_ref, wrg_ref, brg_ref, wig_ref, big_ref, lam_ref,
             bg_ref, h_ref, yl_ref, mg_ref, yp_ref, yr_ref, h1_ref, a_ref, r_ref, ig_ref, xc_ref, gw3_ref,
             pprojT, lru_w, wout_w, pbuf, lbuf, a_s, b_s, h_s, hk, pk, hcar, sems, stage3, send3, recv3, local3):
        i = pl.program_id(0)
        t0 = i * tm
        gc = _Gather(own3_ref, gw3_ref, stage3, send3, recv3, local3)

        @pl.when(i == 0)
        def _():
            gc.send_mine()
            _load_weights(gw_ref, [("pproj", pprojT, PW), ("lru", lru_w, D), ("wout", wout_w, D)], sems)
            pbuf[pl.ds(0, HALO), :] = jnp.zeros((HALO, PW), F32)
            lbuf[pl.ds(0, HALO), :] = jnp.zeros((HALO, D), F32)
            hcar[...] = jnp.zeros_like(hcar)

        @pl.when(i == nt // 2)
        def _():
            gc.pass_on((0, 1))

        @pl.when(i == (3 * nt) // 4)
        def _():
            gc.pass_on((2,))

        pbuf[pl.ds(HALO, tm), :] = z_ref[:, pl.ds(0, PW)]
        _, mixed_pre = _pool_tile(pbuf, t0, tm, pw_ref, ps_ref)
        mixed = jnp.concatenate(mixed_pre, axis=1) * ps_ref[...]
        y_pool = _nt(mixed.astype(_MXU), pprojT[...])
        pbuf[pl.ds(0, HALO), :] = pbuf[pl.ds(tm, HALO), :]

        lbuf[pl.ds(HALO, tm), :] = z_ref[:, pl.ds(PW, D)]
        sp, _ = _softplus_neg(lam_ref[...])
        start = (t0 + lax.broadcasted_iota(jnp.int32, (tm, HD), 0)) == 0
        for hh in range(HEADS):
            xc, r, ig, a, _, _, mult = _lru_gates_head(hh, lbuf, start, tm, cw_ref, cb_ref, wrg_ref, brg_ref,
                                                       wig_ref, big_ref, sp)
            _to_segments(a_s, hh, a, tm)
            _to_segments(b_s, hh, mult * ig * xc, tm)
            cs = pl.ds(hh * HD, HD)
            a_ref[:, cs] = a
            r_ref[:, cs] = r.astype(_MXU)
            ig_ref[:, cs] = ig.astype(_MXU)
            xc_ref[:, cs] = xc.astype(_MXU)
        lbuf[pl.ds(0, HALO), :] = lbuf[pl.ds(tm, HALO), :]
        _segment_scan(a_s, b_s, h_s, hk, pk, hcar, tm, reverse=False)
        for hh in range(HEADS):
            h_ref[:, pl.ds(hh * HD, HD)] = _from_segments(h_s, hh, tm)
        gel, _ = _gelu_and_grad(z_ref[:, pl.ds(PW + D, D)])
        yl = (h_ref[...] * gel).astype(_MXU)
        yl_ref[...] = yl
        y_lru = _nn(yl, lru_w[...])

        g0 = _sigmoid(z_ref[:, pl.ds(PW + 2 * D, D)] + bg_ref[pl.ds(0, 1), :])
        g1 = _sigmoid(z_ref[:, pl.ds(PW + 3 * D, D)] + bg_ref[pl.ds(1, 1), :])
        merged = (g0 * y_pool + g1 * y_lru).astype(_MXU)
        mg_ref[...] = merged
        yp_ref[...] = y_pool.astype(_MXU)
        yr_ref[...] = y_lru.astype(_MXU)
        h1_ref[...] = x_ref[...] + _nn(merged, wout_w[...])

        @pl.when(i == nt - 1)
        def _():
            gc.wait_sibling()
            gc.wait_passed((0, 1, 2))
            gc.finish_sends()

    tok = lambda w, dt: jax.ShapeDtypeStruct((s, w), dt)
    tspec = lambda w: pl.BlockSpec((tm, w), lambda i: (i, 0))
    full = lambda a: pl.BlockSpec(a.shape, lambda i: (0,) * a.ndim)
    any_spec = pl.BlockSpec(memory_space=pl.ANY)
    seg_buf = pltpu.VMEM((HEADS, 8 * _seg_layout(tm)[1], HD), F32)
    dma7 = pltpu.SemaphoreType.DMA((7,))
    return pl.pallas_call(
        body, name="mixer_fwd", grid=(nt,),
        out_shape=(tok(D, F32), tok(D, _MXU), tok(D, _MXU), tok(D, _MXU), tok(D, _MXU), tok(D, F32),
                   tok(D, F32), tok(D, _MXU), tok(D, _MXU), tok(D, _MXU),
                   jax.ShapeDtypeStruct((NDEV, rows3, D), own_third.dtype)),
        in_specs=[tspec(NIN), tspec(D), any_spec, any_spec] + [full(a) for a in small],
        out_specs=(tspec(D),) * 10 + (any_spec,),
        scratch_shapes=[pltpu.VMEM((D, PW), _MXU), pltpu.VMEM((D, D), _MXU), pltpu.VMEM((D, D), _MXU),
                        pltpu.VMEM((tm + HALO, PW), F32), pltpu.VMEM((tm + HALO, D), F32),
                        seg_buf, seg_buf, seg_buf, pltpu.VMEM((HEADS, tm, HD), F32), pltpu.VMEM((HEADS, tm, HD), F32),
                        pltpu.VMEM((8, D), F32), pltpu.SemaphoreType.DMA((3 * NDEV,)),
                        pltpu.VMEM((rows3, D), own_third.dtype), dma7, dma7, pltpu.SemaphoreType.DMA],
        compiler_params=_cparams(1),
    )(z, x, gw, own_third, *small)


def _ffn_fwd(h1, g2, gw):
    s = h1.shape[0]
    tm = min(TM, s)

    def body(h1_ref, g2_ref, gw_ref, v_ref, gf_ref, uf_ref, h2_ref, wffnT, wffo, sems):
        @pl.when(pl.program_id(0) == 0)
        def _():
            _load_weights(gw_ref, [("wffn", wffnT, D), ("wffo", wffo, D)], sems)

        hv = h1_ref[...]
        inv = lax.rsqrt(jnp.mean(hv * hv, axis=-1, keepdims=True) + EPS)
        v = (hv * inv * g2_ref[...]).astype(_MXU)
        v_ref[...] = v
        acc = hv
        for c0, cn in FF_CHUNKS:
            cs = pl.ds(c0, cn)
            gf = _nt(v, wffnT[cs, :]).astype(_MXU)
            uf = _nt(v, wffnT[pl.ds(FF + c0, cn), :]).astype(_MXU)
            gf_ref[:, cs] = gf
            uf_ref[:, cs] = uf
            gf32 = gf.astype(F32)
            act = (gf32 * _sigmoid(gf32) * uf.astype(F32)).astype(_MXU)
            acc = acc + _nn(act, wffo[cs, :])
        h2_ref[...] = acc

    tspec = lambda w: pl.BlockSpec((tm, w), lambda i: (i, 0))
    return pl.pallas_call(
        body, name="ffn_fwd", grid=(s // tm,),
        out_shape=(jax.ShapeDtypeStruct((s, D), _MXU), jax.ShapeDtypeStruct((s, FF), _MXU),
                   jax.ShapeDtypeStruct((s, FF), _MXU), jax.ShapeDtypeStruct((s, D), F32)),
        in_specs=[tspec(D), pl.BlockSpec((1, D), lambda i: (0, 0)), pl.BlockSpec(memory_space=pl.ANY)],
        out_specs=(tspec(D), tspec(FF), tspec(FF), tspec(D)),
        scratch_shapes=[pltpu.VMEM((2 * FF, D), _MXU), pltpu.VMEM((FF, D), _MXU), pltpu.SemaphoreType.DMA((2 * NDEV,))],
        compiler_params=_cparams(1),
    )(h1, g2, gw)


def _rms_bwd(dy, xn, inv, g):
    dg = jnp.sum(dy * xn, axis=0, keepdims=True)
    dxn = dy * g
    dx = inv * (dxn - xn * jnp.mean(dxn * xn, axis=-1, keepdims=True))
    return dx, dg


def _ple_loss_fwd_bwd(h2, p, target, g3, gfin, gw):
    s = h2.shape[0]
    tm = min(TM, s)

    def body(h2_ref, p_ref, t_ref, g3_ref, gf_ref, gw_ref,
             dh2_ref, loss_ref, dg3_ref, dgf_ref, gwpg_ref, gple_ref, wpg, pleT, sems):
        i = pl.program_id(0)

        @pl.when(i == 0)
        def _():
            _load_weights(gw_ref, [("wpg", wpg, D), ("ple", pleT, PLE)], sems)
            for ref in (loss_ref, dg3_ref, dgf_ref, gwpg_ref, gple_ref):
                ref[...] = jnp.zeros_like(ref)

        hv = h2_ref[...]
        inv3 = lax.rsqrt(jnp.mean(hv * hv, axis=-1, keepdims=True) + EPS)
        xn3 = hv * inv3
        n3 = (xn3 * g3_ref[...]).astype(_MXU)
        pg = _sigmoid(_nn(n3, wpg[...]))
        pm = p_ref[...].astype(_MXU)
        e = _nt(pm, pleT[...])
        h3 = hv + pg * e
        invf = lax.rsqrt(jnp.mean(h3 * h3, axis=-1, keepdims=True) + EPS)
        xf = h3 * invf
        diff = xf * gf_ref[...] - t_ref[...]
        loss_ref[...] += jnp.sum(diff * diff) * (0.5 / D)
        dh3, dgf = _rms_bwd(diff * (1.0 / D), xf, invf, gf_ref[...])
        dgf_ref[...] += dgf
        gple_ref[:, pl.ds(0, PLE)] += _tn((dh3 * pg).astype(_MXU), pm)
        dpg = (dh3 * e * pg * (1.0 - pg)).astype(_MXU)
        gwpg_ref[...] += _tn(n3, dpg)
        dn3 = _nt(dpg, wpg[...])
        dx3, dg3 = _rms_bwd(dn3, xn3, inv3, g3_ref[...])
        dg3_ref[...] += dg3
        dh2_ref[...] = dh3 + dx3

    tspec = lambda w: pl.BlockSpec((tm, w), lambda i: (i, 0))
    vec = pl.BlockSpec((1, D), lambda i: (0, 0))
    mat = pl.BlockSpec((D, D), lambda i: (0, 0))
    return pl.pallas_call(
        body, name="ple_loss", grid=(s // tm,),
        out_shape=(jax.ShapeDtypeStruct((s, D), F32), jax.ShapeDtypeStruct((8, 128), F32),
                   jax.ShapeDtypeStruct((1, D), F32), jax.ShapeDtypeStruct((1, D), F32),
                   jax.ShapeDtypeStruct((D, D), F32), jax.ShapeDtypeStruct((D, D), F32)),
        in_specs=[tspec(D), tspec(PLE), tspec(D), vec, vec, pl.BlockSpec(memory_space=pl.ANY)],
        out_specs=(tspec(D), pl.BlockSpec((8, 128), lambda i: (0, 0)), vec, vec, mat, mat),
        scratch_shapes=[pltpu.VMEM((D, D), _MXU), pltpu.VMEM((D, PLE), _MXU), pltpu.SemaphoreType.DMA((2 * NDEV,))],
        compiler_params=_cparams(1),
    )(h2, p, target, g3, gfin, gw)


def _ffn_bwd_hidden(dh2, gf, uf, gw):
    s = dh2.shape[0]
    tm = min(TM, s)
    nt = s // tm

    def body(dh2_ref, gf_ref, uf_ref, gw_ref, dff_ref, gwo_ref, wffo, gacc, sems):
        i = pl.program_id(0)

        @pl.when(i == 0)
        def _():
            _load_weights(gw_ref, [("wffo", wffo, D)], sems)
            gacc[...] = jnp.zeros_like(gacc)

        dm = dh2_ref[...].astype(_MXU)
        dacts = [_nt(dm, wffo[pl.ds(c0, cn), :]) for c0, cn in FF_CHUNKS]
        for (c0, cn), dact in zip(FF_CHUNKS, dacts):
            cs = pl.ds(c0, cn)
            gfv = gf_ref[:, cs].astype(F32)
            ufv = uf_ref[:, cs].astype(F32)
            sg = _sigmoid(gfv)
            silu = gfv * sg
            gacc[cs, :] += _tn((silu * ufv).astype(_MXU), dm)
            dff_ref[:, cs] = (dact * ufv * (sg * (1.0 + gfv * (1.0 - sg)))).astype(_MXU)
            dff_ref[:, pl.ds(FF + c0, cn)] = (dact * silu).astype(_MXU)

        @pl.when(i == nt - 1)
        def _():
            pltpu.sync_copy(gacc, gwo_ref)

    tspec = lambda w: pl.BlockSpec((tm, w), lambda i: (i, 0))
    return pl.pallas_call(
        body, name="ffn_bwd_hidden", grid=(nt,),
        out_shape=(jax.ShapeDtypeStruct((s, 2 * FF), _MXU), jax.ShapeDtypeStruct((FF, D), F32)),
        in_specs=[tspec(D), tspec(FF), tspec(FF), pl.BlockSpec(memory_space=pl.ANY)],
        out_specs=(tspec(2 * FF), pl.BlockSpec(memory_space=pl.ANY)),
        scratch_shapes=[pltpu.VMEM((FF, D), _MXU), pltpu.VMEM((FF, D), F32), pltpu.SemaphoreType.DMA((NDEV,))],
        compiler_params=_cparams(1),
    )(dh2, gf, uf, gw)


def _proj_norm_bwd(dy, x, dres, g, gw, slab, width, name, lhs=None):
    s = x.shape[0]
    tm = min(TM, s)
    nl = 0 if lhs is None else 1

    def body(*refs):
        dy_ref, x_ref, dr_ref, g_ref = refs[:4]
        l_refs = refs[4:4 + nl]
        gw_ref, dx_ref, dg_ref = refs[4 + nl:7 + nl]
        gl_refs = refs[7 + nl:7 + 2 * nl]
        wT, sems = refs[7 + 2 * nl:]

        @pl.when(pl.program_id(0) == 0)
        def _():
            _load_weights(gw_ref, [(slab, wT, D)], sems)
            dg_ref[...] = jnp.zeros_like(dg_ref)
            for ref in gl_refs:
                ref[...] = jnp.zeros_like(ref)

        dv = _nn(dy_ref[...], wT[...])
        xv = x_ref[...]
        inv = lax.rsqrt(jnp.mean(xv * xv, axis=-1, keepdims=True) + EPS)
        dx, dg = _rms_bwd(dv, xv * inv, inv, g_ref[...])
        dg_ref[...] += dg
        dr = dr_ref[...]
        dx_ref[...] = dr + dx
        for l_ref, gl_ref in zip(l_refs, gl_refs):
            gl_ref[...] += _tn(l_ref[...], dr.astype(_MXU))

    tspec = lambda w: pl.BlockSpec((tm, w), lambda i: (i, 0))
    vec = pl.BlockSpec((1, D), lambda i: (0, 0))
    mat = pl.BlockSpec((D, D), lambda i: (0, 0))
    return pl.pallas_call(
        body, name=name, grid=(s // tm,),
        out_shape=(jax.ShapeDtypeStruct((s, D), F32), jax.ShapeDtypeStruct((1, D), F32))
        + (jax.ShapeDtypeStruct((D, D), F32),) * nl,
        in_specs=[tspec(width), tspec(D), tspec(D), vec] + [tspec(D)] * nl + [pl.BlockSpec(memory_space=pl.ANY)],
        out_specs=(tspec(D), vec) + (mat,) * nl,
        scratch_shapes=[pltpu.VMEM((width, D), _MXU), pltpu.SemaphoreType.DMA((NDEV,))],
        compiler_params=_cparams(1),
    )(dy, x, dres, g, *([] if lhs is None else [lhs]), gw)


def _mixer_bwd(dh1, z, h, y_pool, y_lru, saved, gw, small):
    s = dh1.shape[0]
    tm = min(TM_SEQ, s)
    nt = s // tm
    (pool_w, pool_scale, conv_w, conv_b, w_rg, b_rg, w_ig, b_ig, lam, b_gate) = small

    def body(dh1_ref, z_ref, zp_ref, h_ref, hp_ref, yp_ref, yr_ref, a_ref, r_ref, ig_ref, xc_ref, gw_ref,
             pw_ref, ps_ref, cw_ref, cb_ref, wrg_ref, brg_ref, wig_ref, big_ref, lam_ref, bg_ref,
             dz_ref, dyr_ref, dyp_ref, mx_ref,
             gbg_ref, glam_ref, gbrg_ref, gbig_ref, gcb_ref, gcw_ref, gps_ref, gpw_ref, gwrg_ref, gwig_ref,
             pprojT, lru_w, wout_w, pbuf, lbuf, hbuf, qbuf, xbuf, a_s, g_s, dh_s, hk, pk, dcar, sems):
        step = pl.program_id(0)
        i = nt - 1 - step
        t0 = i * tm

        @pl.when(step == 0)
        def _():
            _load_weights(gw_ref, [("pproj", pprojT, PW), ("lru", lru_w, D), ("wout", wout_w, D)], sems)
            for ref in (gbg_ref, glam_ref, gbrg_ref, gbig_ref, gcb_ref, gcw_ref, gps_ref, gpw_ref, gwrg_ref, gwig_ref):
                ref[...] = jnp.zeros_like(ref)
            qbuf[pl.ds(tm, HALO), :] = jnp.zeros((HALO, PW), F32)
            xbuf[pl.ds(tm, 8), :] = jnp.zeros((8, D), F32)
            dcar[...] = jnp.zeros_like(dcar)

        first = i == 0
        zprev = jnp.where(first, 0.0, zp_ref[...])
        hprev = jnp.where(first, 0.0, hp_ref[...])

        d_merged = _nt(dh1_ref[...].astype(_MXU), wout_w[...])

        g0 = _sigmoid(z_ref[:, pl.ds(PW + 2 * D, D)] + bg_ref[pl.ds(0, 1), :])
        g1 = _sigmoid(z_ref[:, pl.ds(PW + 3 * D, D)] + bg_ref[pl.ds(1, 1), :])
        dz0 = d_merged * yp_ref[...].astype(F32) * g0 * (1.0 - g0)
        dz1 = d_merged * yr_ref[...].astype(F32) * g1 * (1.0 - g1)
        dz_ref[:, pl.ds(PW + 2 * D, D)] = dz0.astype(_MXU)
        dz_ref[:, pl.ds(PW + 3 * D, D)] = dz1.astype(_MXU)
        gbg_ref[pl.ds(0, 1), :] += jnp.sum(dz0, axis=0, keepdims=True)
        gbg_ref[pl.ds(1, 1), :] += jnp.sum(dz1, axis=0, keepdims=True)
        d_ypool = (d_merged * g0).astype(_MXU)
        d_ylru = (d_merged * g1).astype(_MXU)
        dyp_ref[...] = d_ypool
        dyr_ref[...] = d_ylru

        d_yl = _nt(d_ylru, lru_w[...])
        gel, dgel = _gelu_and_grad(z_ref[:, pl.ds(PW + D, D)])
        dz_ref[:, pl.ds(PW + D, D)] = (d_yl * h_ref[...] * dgel).astype(_MXU)
        g_full = d_yl * gel
        lbuf[pl.ds(0, HALO), :] = zprev[:, PW:PW + D]
        lbuf[pl.ds(HALO, tm), :] = z_ref[:, pl.ds(PW, D)]
        hbuf[pl.ds(0, 8), :] = hprev
        hbuf[pl.ds(8, tm), :] = h_ref[...]
        sp, sneg = _softplus_neg(lam_ref[...])
        start = (t0 + lax.broadcasted_iota(jnp.int32, (tm, HD), 0)) == 0
        for hh in range(HEADS):
            cs = pl.ds(hh * HD, HD)
            _to_segments(a_s, hh, a_ref[:, cs], tm)
            _to_segments(g_s, hh, g_full[:, hh * HD:(hh + 1) * HD], tm)
        _segment_scan(a_s, g_s, dh_s, hk, pk, dcar, tm, reverse=True)
        for hh in range(HEADS):
            cs = pl.ds(hh * HD, HD)
            a = a_ref[:, cs]
            r = r_ref[:, cs].astype(F32)
            ig = ig_ref[:, cs].astype(F32)
            xc = xc_ref[:, cs].astype(F32)
            a2 = a * a
            one_m = 1.0 - a2
            live = jnp.logical_and(one_m > 0.0, jnp.logical_not(start))
            inv_mult = lax.rsqrt(jnp.where(live, one_m, 1.0))
            mult = jnp.where(live, one_m * inv_mult, jnp.where(start, 1.0, 0.0))
            dh = _from_segments(dh_s, hh, tm)
            d_mult = dh * ig * xc
            d_loga = dh * hbuf[pl.ds(7, tm), cs] * a - jnp.where(live, d_mult * a2 * inv_mult, 0.0)
            glam_ref[:, cs] += jnp.sum(d_loga * (LRU_C * r) * sneg[:, hh * HD:(hh + 1) * HD], axis=0, keepdims=True)
            d_rpre = d_loga * (-LRU_C * sp[:, hh * HD:(hh + 1) * HD]) * r * (1.0 - r)
            d_igpre = dh * mult * xc * ig * (1.0 - ig)
            gbrg_ref[pl.ds(hh, 1), :] += jnp.sum(d_rpre, axis=0, keepdims=True)
            gbig_ref[pl.ds(hh, 1), :] += jnp.sum(d_igpre, axis=0, keepdims=True)
            drm = d_rpre.astype(_MXU)
            dim = d_igpre.astype(_MXU)
            xcm = xc.astype(_MXU)
            gwrg_ref[hh] += _tn(xcm, drm)
            gwig_ref[hh] += _tn(xcm, dim)
            d_xc = dh * mult * ig + _nt(drm, wrg_ref[hh]) + _nt(dim, wig_ref[hh])
            gcb_ref[:, cs] += jnp.sum(d_xc, axis=0, keepdims=True)
            for k in range(CONV):
                gcw_ref[pl.ds(k, 1), cs] += jnp.sum(d_xc * lbuf[pl.ds(HALO - (CONV - 1) + k, tm), cs], axis=0,
                                                    keepdims=True)
            xbuf[pl.ds(0, tm), cs] = d_xc
        dzl = cw_ref[pl.ds(CONV - 1, 1), :] * xbuf[pl.ds(0, tm), :]
        for k in range(CONV - 1):
            dzl = dzl + cw_ref[pl.ds(k, 1), :] * xbuf[pl.ds(CONV - 1 - k, tm), :]
        dz_ref[:, pl.ds(PW, D)] = dzl.astype(_MXU)
        xbuf[pl.ds(tm, 8), :] = xbuf[pl.ds(0, 8), :]

        d_mixed = _nn(d_ypool, pprojT[...])
        pbuf[pl.ds(0, HALO), :] = zprev[:, 0:PW]
        pbuf[pl.ds(HALO, tm), :] = z_ref[:, pl.ds(0, PW)]
        pooled, mixed_pre = _pool_tile(pbuf, t0, tm, pw_ref, ps_ref)
        mp = jnp.concatenate(mixed_pre, axis=1)
        mx_ref[...] = (mp * ps_ref[...]).astype(_MXU)
        gps_ref[...] += jnp.sum(d_mixed * mp, axis=0, keepdims=True)
        d_mp = (d_mixed * ps_ref[...]).astype(_MXU)
        t = t0 + lax.broadcasted_iota(jnp.int32, (tm, GD), 0)
        d_pooled = []
        for g, w in enumerate(WINDOWS):
            dmg = d_mp[:, g * GD:(g + 1) * GD]
            gpw_ref[g] += _tn(pooled[g].astype(_MXU), dmg)
            dp = _nt(dmg, pw_ref[g])
            d_pooled.append(dp)
            qbuf[pl.ds(0, tm), pl.ds(g * GD, GD)] = dp / jnp.minimum(t + 1, w).astype(F32)
        for g, w in enumerate(WINDOWS):
            cs = pl.ds(g * GD, GD)
            acc = qbuf[pl.ds(0, tm), cs]
            for d in range(1, w):
                acc = acc + qbuf[pl.ds(d, tm), cs]
            dz_ref[:, cs] = (acc - d_pooled[g]).astype(_MXU)
        qbuf[pl.ds(tm, HALO), :] = qbuf[pl.ds(0, HALO), :]

    rev = lambda w: pl.BlockSpec((tm, w), lambda g: (nt - 1 - g, 0))
    prev = lambda rows, w: pl.BlockSpec((rows, w), lambda g: (jnp.maximum((nt - 1 - g) * (tm // rows) - 1, 0), 0))
    full = lambda a: pl.BlockSpec(a.shape, lambda g: (0,) * a.ndim)
    tok = lambda w, dt: jax.ShapeDtypeStruct((s, w), dt)
    acc_shapes = [(2, D), (1, D), (HEADS, HD), (HEADS, HD), (1, D), (CONV, D), (1, PW), (GROUPS, GD, GD),
                  (HEADS, HD, HD), (HEADS, HD, HD)]
    acc_specs = tuple(pl.BlockSpec(sh, lambda g, n=len(sh): (0,) * n) for sh in acc_shapes)
    seg_buf = pltpu.VMEM((HEADS, 8 * _seg_layout(tm)[1], HD), F32)
    a_in, r_in, ig_in, xc_in = saved
    return pl.pallas_call(
        body, name="mixer_bwd", grid=(nt,),
        out_shape=(tok(NIN, _MXU), tok(D, _MXU), tok(D, _MXU), tok(PW, _MXU))
        + tuple(jax.ShapeDtypeStruct(sh, F32) for sh in acc_shapes),
        in_specs=[rev(D), rev(NIN), prev(HALO, NIN), rev(D), prev(8, D), rev(D), rev(D), rev(D), rev(D), rev(D), rev(D),
                  pl.BlockSpec(memory_space=pl.ANY)] + [full(a) for a in small],
        out_specs=(rev(NIN), rev(D), rev(D), rev(PW)) + acc_specs,
        scratch_shapes=[pltpu.VMEM((D, PW), _MXU), pltpu.VMEM((D, D), _MXU), pltpu.VMEM((D, D), _MXU),
                        pltpu.VMEM((tm + HALO, PW), F32), pltpu.VMEM((tm + HALO, D), F32),
                        pltpu.VMEM((tm + 8, D), F32), pltpu.VMEM((tm + HALO, PW), F32), pltpu.VMEM((tm + 8, D), F32),
                        seg_buf, seg_buf, seg_buf, pltpu.VMEM((HEADS, tm, HD), F32), pltpu.VMEM((HEADS, tm, HD), F32),
                        pltpu.VMEM((8, D), F32), pltpu.SemaphoreType.DMA((3 * NDEV,))],
        compiler_params=_cparams(1),
    )(dh1, z, z, h, h, y_pool, y_lru, a_in, r_in, ig_in, xc_in, gw, *small)


def _split3(a):
    hi = a.astype(jnp.bfloat16).astype(F32)
    mid = (a - hi).astype(jnp.bfloat16).astype(F32)
    lo = (a - hi - mid).astype(jnp.bfloat16).astype(F32)
    return jnp.stack([hi, mid, lo])


def _small_pack(parts):
    flat = jnp.concatenate([a.reshape(-1) for a in parts])
    return jnp.pad(flat, (0, NDEV * SMALL_ROWS * D - flat.shape[0])).reshape(NDEV * SMALL_ROWS, D)


def _small_unpack(packed, shapes):
    flat = packed.reshape(-1)
    out, o = [], 0
    for sh in shapes:
        n = math.prod(sh)
        out.append(flat[o:o + n].reshape(sh))
        o += n
    return out


def kernel(x, p, norm1_g, w_in, b_gate, pool_w, pool_scale, pool_proj, conv_w, conv_b, w_rg, b_rg, w_ig, b_ig, lru_lambda, lru_proj, w_out, norm2_g, w_ffn_in, w_ffn_out, ple_norm_g, w_ple_gate, w_ple_proj, final_g, loss_target, m_norm1_g, m_w_in, m_b_gate, m_pool_w, m_pool_scale, m_pool_proj, m_conv_w, m_conv_b, m_w_rg, m_b_rg, m_w_ig, m_b_ig, m_lru_lambda, m_lru_proj, m_w_out, m_norm2_g, m_w_ffn_in, m_w_ffn_out, m_ple_norm_g, m_w_ple_gate, m_w_ple_proj, m_final_g, v_norm1_g, v_w_in, v_b_gate, v_pool_w, v_pool_scale, v_pool_proj, v_conv_w, v_conv_b, v_w_rg, v_b_rg, v_w_ig, v_b_ig, v_lru_lambda, v_lru_proj, v_w_out, v_norm2_g, v_w_ffn_in, v_w_ffn_out, v_ple_norm_g, v_w_ple_gate, v_w_ple_proj, v_final_g):
    axes = ("x", "y", "c")
    me = 4 * lax.axis_index("x") + 2 * lax.axis_index("y") + lax.axis_index("c")
    x2 = x[0]
    p2 = p[0, 0]
    tgt = loss_target[0]

    n_small = (CONV + 2) * 128
    small_terms = _split3(jnp.concatenate([conv_w[0].reshape(-1), b_gate[0].reshape(-1)]))
    small_rows = jnp.pad(small_terms, ((0, 16 - 3), (0, D - n_small)))
    own_first = jnp.concatenate([w_in[0].T.astype(_MXU), small_rows.astype(_MXU)], axis=0)
    own_second = jnp.concatenate([
        jnp.pad(pool_proj[0].T, ((0, 0), (0, D - PW))).astype(_MXU), lru_proj[0].astype(_MXU), w_out[0].astype(_MXU),
    ], axis=0)
    own_third = jnp.concatenate([
        w_ffn_in[0].T.astype(_MXU), jnp.pad(w_ple_proj[0].T, ((0, 0), (0, D - PLE))).astype(_MXU),
        w_ffn_out[0].astype(_MXU), w_ple_gate[0].astype(_MXU),
    ], axis=0)
    u, z, gw_first, gw = _inproj_fwd(x2, norm1_g, own_first, own_second)
    off = W_OFF["f32s"][0]
    st = gw_first[:, off:off + 3, :n_small].astype(F32)
    sf = st[:, 0] + st[:, 1] + st[:, 2]
    conv_w_full = sf[:, :CONV * 128].reshape(NDEV, CONV, 128).transpose(1, 0, 2).reshape(CONV, D)
    b_gate_full = sf[:, CONV * 128:].reshape(NDEV, 2, 128).transpose(1, 0, 2).reshape(2, D)

    small = (pool_w[0].astype(_MXU), pool_scale, conv_w_full, conv_b, w_rg[0].astype(_MXU), b_rg[0],
             w_ig[0].astype(_MXU), b_ig[0], lru_lambda, b_gate_full)

    h, yl, merged, y_pool, y_lru, h1, *saved, gw_third = _mixer_fwd(z, x2, gw, small, own_third)
    v, gf, uf, h2 = _ffn_fwd(h1, norm2_g, gw_third)

    dh2, loss_blk, g_ple_norm, g_final, part_wpg, part_ple = _ple_loss_fwd_bwd(h2, p2, tgt, ple_norm_g,
                                                                               final_g.reshape(1, D), gw_third)
    dff, part_wffo = _ffn_bwd_hidden(dh2, gf, uf, gw_third)
    dh1, g_norm2 = _proj_norm_bwd(dff, h1, dh2, norm2_g, gw_third, "wffn", 2 * FF, "ffn_bwd_in")
    (dz, d_ylru, d_ypool, mixed, g_bgate, g_lam, g_brg, g_big, g_convb, g_convw, g_pscale, g_poolw, g_wrg,
     g_wig) = _mixer_bwd(dh1, z, h, y_pool, y_lru, saved, gw, small)
    grad_x, g_norm1, part_wout = _proj_norm_bwd(dz, x2, dh1, norm1_g, gw_first, "win", NIN, "inproj_bwd", lhs=merged)

    small_shapes = [(1, D), (GROUPS, GD, GD), (1, PW), (1, D), (HEADS, HD, HD), (HEADS, HD), (HEADS, HD, HD),
                    (HEADS, HD), (1, D), (1, D), (1, D), (1, D), (2, D), (CONV, D), (1, 1)]
    small_part = _small_pack([g_norm1, g_poolw, g_pscale, g_convb, g_wrg, g_brg, g_wig, g_big, g_lam, g_norm2,
                              g_ple_norm, g_final, g_bgate, g_convw, loss_blk[0:1, 0:1]])
    riders = [_grad_matmul(yl, d_ylru, "grad_lru_proj"), part_wout, _grad_matmul(d_ypool, mixed, "grad_pool_proj")]
    rs_wffn = _grad_matmul_rs(dff, v, "grad_w_ffn_in", 704, extras=[part_wffo, part_wpg, part_ple], narrow=_MXU)
    rs_win = _grad_matmul_rs(dz, u, "grad_w_in", 576, extras=riders + [small_part], narrow=_MXU, tail=SMALL_ROWS)

    def reduced(parts, name):
        return [_sum_arrays([t_own, landed[0], landed[1], landed[2]], "rs_sum_" + name + str(n))
                for n, (t_own, landed) in enumerate(parts)]

    red_wffn, = reduced(rs_wffn, "wffn")
    red_win, red_small = reduced(rs_win, "win")
    g_w_in = red_win[:576].T
    g_w_ffn_in = red_wffn[:704].T
    g_w_ffn_out = red_wffn[704:1056]
    g_w_ple_gate = red_wffn[1056:1184]
    g_w_ple_proj = red_wffn[1184:1312, :PLE].T
    g_lru_proj, g_w_out = red_win[576:704], red_win[704:832]
    g_pool_proj = red_win[832:960, :PW].T
    small_red = _all_gather_small(red_small)
    (gs_norm1, gs_poolw, gs_pscale, gs_convb, gs_wrg, gs_brg, gs_wig, gs_big, gs_lam, gs_norm2, gs_ple_norm,
     gs_final, gs_bgate, gs_convw, loss_sum) = _small_unpack(small_red, small_shapes)
    loss = loss_sum[0, 0]
    g_b_gate = lax.dynamic_slice_in_dim(gs_bgate, me * 128, 128, axis=1)
    g_conv_w = lax.dynamic_slice_in_dim(gs_convw, me * 128, 128, axis=1)

    grads = {
        "norm1_g": gs_norm1, "w_in": g_w_in[None], "b_gate": g_b_gate[None], "pool_w": gs_poolw[None],
        "pool_scale": gs_pscale, "pool_proj": g_pool_proj[None], "conv_w": g_conv_w[None], "conv_b": gs_convb,
        "w_rg": gs_wrg[None], "b_rg": gs_brg[None], "w_ig": gs_wig[None], "b_ig": gs_big[None], "lru_lambda": gs_lam,
        "lru_proj": g_lru_proj[None], "w_out": g_w_out[None], "norm2_g": gs_norm2, "w_ffn_in": g_w_ffn_in[None],
        "w_ffn_out": g_w_ffn_out[None], "ple_norm_g": gs_ple_norm, "w_ple_gate": g_w_ple_gate[None],
        "w_ple_proj": g_w_ple_proj[None], "final_g": gs_final.reshape(D),
    }
    weights = dict(norm1_g=norm1_g, w_in=w_in, b_gate=b_gate, pool_w=pool_w, pool_scale=pool_scale, pool_proj=pool_proj,
                   conv_w=conv_w, conv_b=conv_b, w_rg=w_rg, b_rg=b_rg, w_ig=w_ig, b_ig=b_ig, lru_lambda=lru_lambda,
                   lru_proj=lru_proj, w_out=w_out, norm2_g=norm2_g, w_ffn_in=w_ffn_in, w_ffn_out=w_ffn_out,
                   ple_norm_g=ple_norm_g, w_ple_gate=w_ple_gate, w_ple_proj=w_ple_proj, final_g=final_g)
    moments_m = dict(norm1_g=m_norm1_g, w_in=m_w_in, b_gate=m_b_gate, pool_w=m_pool_w, pool_scale=m_pool_scale,
                     pool_proj=m_pool_proj, conv_w=m_conv_w, conv_b=m_conv_b, w_rg=m_w_rg, b_rg=m_b_rg, w_ig=m_w_ig,
                     b_ig=m_b_ig, lru_lambda=m_lru_lambda, lru_proj=m_lru_proj, w_out=m_w_out, norm2_g=m_norm2_g,
                     w_ffn_in=m_w_ffn_in, w_ffn_out=m_w_ffn_out, ple_norm_g=m_ple_norm_g, w_ple_gate=m_w_ple_gate,
                     w_ple_proj=m_w_ple_proj, final_g=m_final_g)
    moments_v = dict(norm1_g=v_norm1_g, w_in=v_w_in, b_gate=v_b_gate, pool_w=v_pool_w, pool_scale=v_pool_scale,
                     pool_proj=v_pool_proj, conv_w=v_conv_w, conv_b=v_conv_b, w_rg=v_w_rg, b_rg=v_b_rg, w_ig=v_w_ig,
                     b_ig=v_b_ig, lru_lambda=v_lru_lambda, lru_proj=v_lru_proj, w_out=v_w_out, norm2_g=v_norm2_g,
                     w_ffn_in=v_w_ffn_in, w_ffn_out=v_w_ffn_out, ple_norm_g=v_ple_norm_g, w_ple_gate=v_w_ple_gate,
                     w_ple_proj=v_w_ple_proj, final_g=v_final_g)
    names = list(weights)
    big = ("w_in", "w_ffn_in", "w_ffn_out", "lru_proj", "w_out", "w_ple_gate", "pool_proj", "w_ple_proj")
    slab_space = {"w_in": red_win[:576], "w_ffn_in": red_wffn[:704]}
    delta, new_m, new_v = {}, {}, {}
    for n in big:
        sh = weights[n].shape
        if n in slab_space:
            as2d = lambda a: a[0].T
            back = lambda a: a.T[None]
            g2d = slab_space[n]
        else:
            as2d = lambda a: a.reshape(sh[-2], sh[-1])
            back = lambda a: a.reshape(sh)
            g2d = as2d(grads[n])
        d_, m_, v_ = _adamw(as2d(weights[n]), g2d, as2d(moments_m[n]), as2d(moments_v[n]), "adamw_" + n)
        delta[n], new_m[n], new_v[n] = back(d_), back(m_), back(v_)
    rest = [n for n in names if n not in big]
    rest_shapes = [weights[n].shape for n in rest]
    packed = [_small_pack([src[n] for n in rest]) for src in (weights, grads, moments_m, moments_v)]
    d_, m_, v_ = _adamw(*packed, "adamw_small")
    for n, a, b_, c_ in zip(rest, _small_unpack(d_, rest_shapes), _small_unpack(m_, rest_shapes),
                            _small_unpack(v_, rest_shapes)):
        delta[n], new_m[n], new_v[n] = a, b_, c_

    return (loss, grad_x[None], *[grads[n] for n in names], *[delta[n] for n in names],
            *[new_m[n] for n in names], *[new_v[n] for n in names])
```

```python
import functools
import math

import jax
import jax.numpy as jnp
from jax import lax
from jax.experimental import pallas as pl
from jax.experimental.pallas import tpu as pltpu

F32 = jnp.float32
D = 1024
NIN = 4608
PW = 512
FF = 2816
MXU_DIM = 256
FF_CHUNKS = tuple((c0, min(3 * MXU_DIM, FF - c0)) for c0 in range(0, FF, 3 * MXU_DIM))
PLE = 256
HEADS, HD = 8, 128
GROUPS, GD = 4, 128
WINDOWS = (2, 4, 8, 16)
HALO = 16
CONV = 4
EPS = 1e-6
LRU_C = 8.0
NDEV = 8
MESH = pl.DeviceIdType.MESH

ADAM_LR, ADAM_B1, ADAM_B2, ADAM_EPS, ADAM_WD, ADAM_STEP = 0.001, 0.9, 0.999, 1e-08, 0.01, 10

_MXU = jnp.bfloat16
TM = 512
TM_SEQ = 256
VMEM_LIMIT = 56 * 1024 * 1024
VMEM_LIMIT_LARGE = 60 * 1024 * 1024
W_FIRST = (("win", 576), ("f32s", 16))
W_SECOND = (("pproj", 128), ("lru", 128), ("wout", 128))
W_THIRD = (("wffn", 704), ("ple", 128), ("wffo", 352), ("wpg", 128))
W_OFF = {}
for _slabs in (W_FIRST, W_SECOND, W_THIRD):
    _o = 0
    for _n, _r in _slabs:
        W_OFF[_n] = (_o, _r)
        _o += _r
SMALL_ROWS = 48


def _cparams(n_axes=1, vmem=VMEM_LIMIT):
    return pltpu.CompilerParams(dimension_semantics=("arbitrary",) * n_axes, vmem_limit_bytes=vmem)


def _my_pos():
    return lax.axis_index("x"), lax.axis_index("y"), lax.axis_index("c")


def _nt(a, b):
    return lax.dot_general(a, b, (((1,), (1,)), ((), ())), preferred_element_type=F32)


def _nn(a, b):
    return lax.dot_general(a, b, (((1,), (0,)), ((), ())), preferred_element_type=F32)


def _tn(a, b):
    return lax.dot_general(a, b, (((0,), (0,)), ((), ())), preferred_element_type=F32)


def _sigmoid(x):
    return 0.5 * jnp.tanh(0.5 * x) + 0.5


_GELU_K = math.sqrt(2.0 / math.pi)


def _gelu_and_grad(x):
    x2 = x * x
    inner = _GELU_K * (x + 0.044715 * x2 * x)
    t = jnp.tanh(inner)
    g = 0.5 * x * (1.0 + t)
    dg = 0.5 * (1.0 + t) + 0.5 * x * (1.0 - t * t) * _GELU_K * (1.0 + 3.0 * 0.044715 * x2)
    return g, dg


def _softplus_neg(lam):
    x = -lam
    t = jnp.exp(-jnp.abs(x))
    u = 1.0 + t
    l1p = jnp.where(u == 1.0, t, jnp.log(u) * t / (u - 1.0))
    return jnp.maximum(x, 0.0) + l1p, _sigmoid(x)


def _start_slab_loads(g_ref, name, dst_ref, sems, base, width=D):
    off, rows = W_OFF[name]
    copies = []
    for k in range(NDEV):
        if width == D:
            src = g_ref.at[k, pl.ds(off, rows), :]
        else:
            src = g_ref.at[k, pl.ds(off, rows), pl.ds(0, width)]
        cp = pltpu.make_async_copy(src, dst_ref.at[pl.ds(k * rows, rows), :], sems.at[base + k])
        cp.start()
        copies.append(cp)
    return copies


def _load_weights(g_ref, items, sems):
    copies = []
    for n, (name, dst, width) in enumerate(items):
        copies += _start_slab_loads(g_ref, name, dst, sems, n * NDEV, width)
    for cp in copies:
        cp.wait()


class _Gather:
    def __init__(self, own_ref, out_ref, stage, send_sems, recv_sems, local_sem):
        x, y, c = _my_pos()
        self.c = c
        self.me, self.sibling = (x, y, c), (x, y, 1 - c)
        self.chips = [(1 - x, y), (x, 1 - y), (1 - x, 1 - y)]
        self.own_ref, self.out_ref, self.stage = own_ref, out_ref, stage
        self.send_sems, self.recv_sems = send_sems, recv_sems
        self.mine = pltpu.make_async_copy(stage, self.slab(*self.me), local_sem)
        self.first = [self.copy(0, self.me, self.sibling, src=stage)] + [
            self.copy(1 + j, self.me, (*chip, c), src=stage) for j, chip in enumerate(self.chips)]
        self.passed = [self.copy(4 + j, (*chip, c), self.sibling) for j, chip in enumerate(self.chips)]

    def slab(self, px, py, pc):
        return self.out_ref.at[4 * px + 2 * py + pc]

    def copy(self, k, block, to, src=None):
        return pltpu.make_async_remote_copy(
            src_ref=self.slab(*block) if src is None else src, dst_ref=self.slab(*block),
            send_sem=self.send_sems.at[k], recv_sem=self.recv_sems.at[k], device_id=to, device_id_type=MESH)

    def send_mine(self, far=True):
        pltpu.sync_copy(self.own_ref, self.stage)
        self.mine.start()
        for cp in self.first[:3]:
            cp.start()
        if far:
            self.send_far()

    def send_far(self):
        self.first[3].start()

    def pass_on(self, js):
        for j in js:
            self.copy(1 + j, (*self.chips[j], self.c), self.me).wait_recv()
            self.passed[j].start()

    def wait_sibling(self):
        self.copy(0, self.sibling, self.me).wait_recv()

    def wait_passed(self, js):
        for j in js:
            self.copy(4 + j, (*self.chips[j], 1 - self.c), self.me).wait_recv()

    def finish_sends(self):
        for cp in self.first + self.passed:
            cp.wait_send()
        self.mine.wait()


def _all_gather_small(piece):
    rows = piece.shape[0]

    def body(p_ref, out_ref, send_sems, recv_sems, local_sem):
        x, y, c = _my_pos()
        me = 4 * x + 2 * y + c
        mine = pltpu.make_async_copy(p_ref, out_ref.at[pl.ds(pl.multiple_of(me * rows, 8), rows), :], local_sem)
        mine.start()
        sends = []
        peers = []
        for r in range(1, NDEV):
            px = 1 - x if (r >> 2) & 1 else x
            py = 1 - y if (r >> 1) & 1 else y
            pc = 1 - c if r & 1 else c
            peers.append((px, py, pc))
            cp = pltpu.make_async_remote_copy(
                src_ref=p_ref, dst_ref=out_ref.at[pl.ds(pl.multiple_of(me * rows, 8), rows), :],
                send_sem=send_sems.at[r - 1], recv_sem=recv_sems.at[r - 1], device_id=(px, py, pc),
                device_id_type=MESH)
            cp.start()
            sends.append(cp)
        for r, (px, py, pc) in enumerate(peers):
            them = 4 * px + 2 * py + pc
            pltpu.make_async_remote_copy(
                src_ref=p_ref, dst_ref=out_ref.at[pl.ds(pl.multiple_of(them * rows, 8), rows), :],
                send_sem=send_sems.at[r], recv_sem=recv_sems.at[r], device_id=(px, py, pc),
                device_id_type=MESH).wait_recv()
        for cp in sends:
            cp.wait_send()
        mine.wait()

    return pl.pallas_call(
        body, name="ag_small",
        out_shape=jax.ShapeDtypeStruct((NDEV * rows, piece.shape[1]), piece.dtype),
        in_specs=[pl.BlockSpec(memory_space=pltpu.VMEM)],
        out_specs=pl.BlockSpec(memory_space=pl.ANY),
        scratch_shapes=[pltpu.SemaphoreType.DMA((7,)), pltpu.SemaphoreType.DMA((7,)), pltpu.SemaphoreType.DMA],
    )(piece)


def _row_block(rows, target=512, mult=8):
    b = min(rows, target) // mult * mult
    while rows % b:
        b -= mult
    return b


def _sum_arrays(arrs, name, narrow=None, target=704):
    rows, cols = arrs[0].shape
    br = _row_block(rows, target, 16)
    n = len(arrs)

    def body(*refs):
        acc = refs[0][...].astype(F32)
        for r in refs[1:n]:
            acc = acc + r[...].astype(F32)
        refs[n][...] = acc
        if narrow is not None:
            refs[n + 1][...] = acc.astype(narrow)

    spec = pl.BlockSpec((br, cols), lambda i: (i, 0))
    shape = jax.ShapeDtypeStruct((rows, cols), F32)
    if narrow is None:
        out_shape, out_specs = shape, spec
    else:
        out_shape, out_specs = (shape, jax.ShapeDtypeStruct((rows, cols), narrow)), (spec, spec)
    return pl.pallas_call(
        body, name=name, grid=(rows // br,), out_shape=out_shape,
        in_specs=[spec] * n, out_specs=out_specs, compiler_params=_cparams(1),
    )(*arrs)


def _adamw(w, g, m, v, name):
    rows, cols = w.shape
    br = _row_block(rows, 256)

    def body(w_ref, g_ref, m_ref, v_ref, d_ref, nm_ref, nv_ref):
        g_ = g_ref[...]
        m_ = ADAM_B1 * m_ref[...] + (1.0 - ADAM_B1) * g_
        v_ = ADAM_B2 * v_ref[...] + (1.0 - ADAM_B2) * (g_ * g_)
        m_hat = m_ / (1.0 - ADAM_B1 ** ADAM_STEP)
        v_hat = v_ / (1.0 - ADAM_B2 ** ADAM_STEP)
        d_ref[...] = -ADAM_LR * (m_hat / (jnp.sqrt(v_hat) + ADAM_EPS) + ADAM_WD * w_ref[...])
        nm_ref[...] = m_
        nv_ref[...] = v_

    spec = pl.BlockSpec((br, cols), lambda i: (i, 0))
    shape = jax.ShapeDtypeStruct((rows, cols), F32)
    return pl.pallas_call(
        body, name=name, grid=(rows // br,), out_shape=(shape, shape, shape),
        in_specs=[spec] * 4, out_specs=(spec, spec, spec), compiler_params=_cparams(1),
    )(w, g, m, v)


_CHIP_FLIPS = (2, 3, 1, 0)


def _grad_matmul(lhs, rhs, name):
    s, r = lhs.shape
    k = rhs.shape[1]
    tm = min(4 * TM, s)

    def body(l_ref, r_ref, o_ref):
        @pl.when(pl.program_id(0) == 0)
        def _():
            o_ref[...] = jnp.zeros_like(o_ref)

        o_ref[:, pl.ds(0, k)] += _tn(l_ref[...].astype(_MXU), r_ref[...].astype(_MXU))

    return pl.pallas_call(
        body, name=name, grid=(s // tm,),
        out_shape=jax.ShapeDtypeStruct((r, D), F32),
        in_specs=[pl.BlockSpec((tm, r), lambda i: (i, 0)), pl.BlockSpec((tm, k), lambda i: (i, 0))],
        out_specs=pl.BlockSpec((r, D), lambda i: (0, 0)),
        compiler_params=_cparams(1),
    )(lhs, rhs)


def _grad_matmul_rs(lhs, rhs, name, rows, extras=(), narrow=None, tail=0):
    s, r8 = lhs.shape
    k = rhs.shape[1]
    cpb = 1
    nblk = 4 // cpb
    nx = len(extras)
    ers = [e.shape[0] // NDEV for e in extras]
    er = sum(ers)
    srows = rows + er
    brows = 2 * cpb * srows
    groups = [(0, srows - tail, F32 if narrow is None else narrow)] + ([(srows - tail, tail, F32)] if tail else [])
    ng = len(groups)
    resident = ((2 * brows + 2 * srows) * D * 4 + sum(2 * n * D * jnp.dtype(dt).itemsize for _, n, dt in groups)
                + 2 * cpb * rows * k * 4)
    per_token = 2 * (2 * cpb * rows * lhs.dtype.itemsize + k * rhs.dtype.itemsize)
    fitting = [t for t in (4 * TM, 2 * TM, TM)
               if s % t == 0 and 2 * t <= s and resident + t * per_token <= VMEM_LIMIT_LARGE * 9 // 10]
    tm = fitting[0] if fitting else min(TM, s)
    nt = s // tm
    mid = min(nt - 1, max(1, nt // 6))

    def flip_of(p):
        return jnp.where(p == 0, 2, jnp.where(p == 1, 3, jnp.where(p == 2, 1, 0)))

    def block_col(b):
        x, y, _ = _my_pos()
        return (2 * x + y) ^ flip_of(b)

    def body(*refs):
        l_ref, r_ref = refs[:2]
        x_refs = refs[2:2 + nx]
        rest = refs[2 + nx:]
        town_ref = rest[0]
        lici_refs = rest[1:1 + ng]
        acc, stage = rest[1 + ng:3 + ng]
        send_bufs = rest[3 + ng:3 + 2 * ng]
        dsend, drecv, isend, irecv, xsem = rest[3 + 2 * ng:]
        b = pl.program_id(0)
        i = pl.program_id(1)
        x, y, c = _my_pos()
        mine = 2 * x + y
        sibling = (x, y, 1 - c)

        def chip_at(p):
            return mine ^ _CHIP_FLIPS[p]

        def slab_rows(p, parity):
            within = 0 if cpb == 1 else (chip_at(p) & 1) * 2
            return pl.ds(pl.multiple_of((within + parity) * srows, 8), srows)

        def push(p, slot):
            return pltpu.make_async_remote_copy(
                src_ref=acc.at[slot, slab_rows(p, 1 - c), :], dst_ref=stage.at[p % 2],
                send_sem=dsend.at[p], recv_sem=drecv.at[p], device_id=sibling, device_id_type=MESH)

        def ici(p):
            ch = chip_at(p)
            return [pltpu.make_async_remote_copy(
                src_ref=send_bufs[g].at[p % 2], dst_ref=lici_refs[g].at[p], send_sem=isend.at[3 * g + p],
                recv_sem=irecv.at[3 * g + p], device_id=(ch >> 1, ch & 1, c), device_id_type=MESH) for g in range(ng)]

        def extra_loads(p, slot):
            copies = []
            within = 0 if cpb == 1 else (chip_at(p) & 1) * 2
            for parity in range(2):
                off = rows
                for n, (x_ref, e) in enumerate(zip(x_refs, ers)):
                    src = x_ref.at[pl.ds(pl.multiple_of((2 * chip_at(p) + parity) * e, 8), e), :]
                    dst = acc.at[slot, pl.ds(pl.multiple_of((within + parity) * srows + off, 8), e), :]
                    copies.append(pltpu.make_async_copy(src, dst, xsem.at[(p * 2 + parity) * nx + n]))
                    off += e
            return copies

        def combine(p, slot):
            push(p, slot).wait_recv()
            total = acc[slot, slab_rows(p, c), :] + stage[p % 2]
            if p == 3:
                stage[p % 2] = total
                pltpu.sync_copy(stage.at[p % 2], town_ref)
            else:
                if p == 2:
                    for cp in ici(0):
                        cp.wait_send()
                for g, (r0, n, dt) in enumerate(groups):
                    send_bufs[g][p % 2] = total[r0:r0 + n, :].astype(dt)
                for cp in ici(p):
                    cp.start()

        for bb in range(nblk):
            slot = bb % 2
            positions = list(range(bb * cpb, (bb + 1) * cpb))

            @pl.when(jnp.logical_and(b == bb, i == 0))
            def _(bb=bb, slot=slot, positions=positions):
                if bb >= 2:
                    for p in range((bb - 2) * cpb, (bb - 1) * cpb):
                        push(p, slot).wait_send()
                for q in range(2 * cpb):
                    acc[slot, pl.ds(q * srows, rows), :] = jnp.zeros((rows, D), F32)
                for p in positions:
                    for cp in extra_loads(p, slot):
                        cp.start()

            if bb >= 1:
                @pl.when(jnp.logical_and(b == bb, i == mid))
                def _(bb=bb):
                    for p in range((bb - 1) * cpb, bb * cpb):
                        combine(p, (bb - 1) % 2)

        res = _tn(l_ref[...].astype(_MXU), r_ref[...].astype(_MXU))
        slot_now = b % 2
        for q in range(2 * cpb):
            acc[slot_now, pl.ds(q * srows, rows), pl.ds(0, k)] += res[q * rows:(q + 1) * rows, :]

        for bb in range(nblk):
            slot = bb % 2
            positions = list(range(bb * cpb, (bb + 1) * cpb))

            @pl.when(jnp.logical_and(b == bb, i == nt - 1))
            def _(bb=bb, slot=slot, positions=positions):
                for p in positions:
                    for cp in extra_loads(p, slot):
                        cp.wait()
                for p in positions:
                    push(p, slot).start()
                if bb == nblk - 1:
                    for p in positions:
                        combine(p, slot)
                    for p in range(max(0, (nblk - 2) * cpb), 4):
                        push(p, slot).wait_send()
                    for p in range(1, 3):
                        for cp in ici(p):
                            cp.wait_send()
                    for p in range(3):
                        for cp in ici(p):
                            cp.wait_recv()

    in_specs = [pl.BlockSpec((tm, 2 * cpb * rows), lambda b, i: (i, block_col(b))),
                pl.BlockSpec((tm, k), lambda b, i: (i, 0))]
    any_spec = pl.BlockSpec(memory_space=pl.ANY)
    in_specs += [any_spec] * nx
    args = [lhs, rhs, *extras]
    outs = pl.pallas_call(
        body, name=name, grid=(nblk, nt),
        out_shape=(jax.ShapeDtypeStruct((srows, D), F32),)
        + tuple(jax.ShapeDtypeStruct((3, n, D), dt) for _, n, dt in groups),
        in_specs=in_specs, out_specs=(any_spec,) * (1 + ng),
        scratch_shapes=[pltpu.VMEM((2, brows, D), F32), pltpu.VMEM((2, srows, D), F32)]
        + [pltpu.VMEM((2, n, D), dt) for _, n, dt in groups]
        + [pltpu.SemaphoreType.DMA((4,)), pltpu.SemaphoreType.DMA((4,)), pltpu.SemaphoreType.DMA((3 * ng,)),
           pltpu.SemaphoreType.DMA((3 * ng,)), pltpu.SemaphoreType.DMA((max(1, 8 * nx),))],
        compiler_params=_cparams(2, VMEM_LIMIT_LARGE),
    )(*args)
    t_own = outs[0]
    return [(t_own[r0:r0 + n], landed) for (r0, n, _), landed in zip(groups, outs[1:])]


def _inproj_fwd(x, g1, own_first, own_second):
    s = x.shape[0]
    tm = min(2 * TM, s // 2)
    nt = s // tm
    assert nt % 2 == 0
    rows1, rows2 = own_first.shape[0], own_second.shape[0]
    wrows = W_OFF["win"][1]
    cw = 2 * wrows

    def chip_col(b):
        px, py, _ = _my_pos()
        return (2 * px + py) ^ jnp.where(b == 0, 0, jnp.where(b == 1, 2, jnp.where(b == 2, 1, 3)))

    def body(x_ref, g1_ref, own1_ref, own2_ref, u_ref, z_ref, gw1_ref, gw2_ref, w_vmem, u_buf, stage1, stage2, sems,
             usem, send1, recv1, local1, send2, recv2, local2):
        b = pl.program_id(0)
        i = pl.program_id(1)
        ga = _Gather(own1_ref, gw1_ref, stage1, send1, recv1, local1)
        gb = _Gather(own2_ref, gw2_ref, stage2, send2, recv2, local2)
        c = ga.c

        def load_chip(px, py, own_too):
            copies = []
            for pc in range(2):
                dst = w_vmem.at[pl.ds(pc * wrows, wrows), :]
                copies.append(pltpu.make_async_copy(gw1_ref.at[4 * px + 2 * py + pc, pl.ds(0, wrows), :], dst,
                                                    sems.at[pc]))
            if own_too:
                mine_dst = w_vmem.at[pl.ds(pl.multiple_of(c * wrows, 16), wrows), :]
                copies[0] = pltpu.make_async_copy(own1_ref.at[pl.ds(0, wrows), :], mine_dst, sems.at[0])
                theirs_dst = w_vmem.at[pl.ds(pl.multiple_of((1 - c) * wrows, 16), wrows), :]
                copies[1] = pltpu.make_async_copy(gw1_ref.at[4 * px + 2 * py + 1 - c, pl.ds(0, wrows), :], theirs_dst,
                                                  sems.at[1])
            for cp in copies:
                cp.start()
            for cp in copies:
                cp.wait()

        @pl.when(jnp.logical_and(b == 0, i == 0))
        def _():
            ga.send_mine(far=False)
            ga.wait_sibling()
            load_chip(ga.me[0], ga.me[1], True)

        @pl.when(jnp.logical_and(b == 0, i == nt // 4))
        def _():
            ga.send_far()
            gb.send_mine()

        @pl.when(jnp.logical_and(b == 0, i == (3 * nt) // 4))
        def _():
            ga.pass_on((0, 1))

        @pl.when(jnp.logical_and(b == 1, i == (3 * nt) // 4))
        def _():
            ga.pass_on((2,))

        for j in range(3):
            @pl.when(jnp.logical_and(b == j + 1, i == 0))
            def _(j=j):
                ga.wait_passed((j,))
                load_chip(ga.chips[j][0], ga.chips[j][1], False)

        @pl.when(jnp.logical_and(b == 2, i == nt // 2))
        def _():
            gb.pass_on((0, 1))

        @pl.when(jnp.logical_and(b == 3, i == (3 * nt) // 4))
        def _():
            gb.pass_on((2,))

        slot = i % 2

        def u_write(t, sl):
            return pltpu.make_async_copy(u_buf.at[sl], u_ref.at[pl.ds(pl.multiple_of(t * tm, tm), tm), :], usem.at[sl])

        def u_read(t, sl):
            return pltpu.make_async_copy(u_ref.at[pl.ds(pl.multiple_of(t * tm, tm), tm), :], u_buf.at[sl], usem.at[sl])

        @pl.when(b == 0)
        def _():
            @pl.when(i >= 2)
            def _():
                u_write(i - 2, slot).wait()

            xv = x_ref[...]
            inv = lax.rsqrt(jnp.mean(xv * xv, axis=-1, keepdims=True) + EPS)
            u_buf[slot] = (xv * inv * g1_ref[...]).astype(_MXU)
            u_write(i, slot).start()

            @pl.when(i == nt - 1)
            def _():
                u_write(i - 1, 1 - slot).wait()
                u_write(i, slot).wait()
                u_read(0, 0).start()

        @pl.when(b > 0)
        def _():
            u_read(i, slot).wait()

            @pl.when(jnp.logical_or(b < 3, i < nt - 1))
            def _():
                u_read((i + 1) % nt, 1 - slot).start()

        z_ref[...] = _nt(u_buf[slot], w_vmem[...])

        @pl.when(jnp.logical_and(b == 3, i == nt - 1))
        def _():
            ga.finish_sends()
            gb.wait_sibling()
            gb.wait_passed((0, 1, 2))
            gb.finish_sends()

    any_spec = pl.BlockSpec(memory_space=pl.ANY)
    dma7 = pltpu.SemaphoreType.DMA((7,))
    return pl.pallas_call(
        body, name="inproj_fwd", grid=(4, nt),
        out_shape=(jax.ShapeDtypeStruct((s, D), _MXU), jax.ShapeDtypeStruct((s, NIN), F32),
                   jax.ShapeDtypeStruct((NDEV, rows1, D), own_first.dtype),
                   jax.ShapeDtypeStruct((NDEV, rows2, D), own_second.dtype)),
        in_specs=[pl.BlockSpec((tm, D), lambda b, i: (jnp.where(b == 0, i, nt - 1), 0)),
                  pl.BlockSpec((1, D), lambda b, i: (0, 0)), any_spec, any_spec],
        out_specs=(any_spec, pl.BlockSpec((tm, cw), lambda b, i: (i, chip_col(b))), any_spec, any_spec),
        scratch_shapes=[pltpu.VMEM((cw, D), _MXU), pltpu.VMEM((2, tm, D), _MXU), pltpu.VMEM((rows1, D), own_first.dtype),
                        pltpu.VMEM((rows2, D), own_second.dtype), pltpu.SemaphoreType.DMA((2,)),
                        pltpu.SemaphoreType.DMA((2,)),
                        dma7, dma7, pltpu.SemaphoreType.DMA, dma7, dma7, pltpu.SemaphoreType.DMA],
        compiler_params=_cparams(2),
    )(x, g1, own_first, own_second)


def _pool_tile(pbuf, t0, tm, pw_ref, scale_ref):
    t = t0 + lax.broadcasted_iota(jnp.int32, (tm, GD), 0)
    pooled, mixed_pre = [], []
    for g, w in enumerate(WINDOWS):
        cs = pl.ds(g * GD, GD)
        cur = pbuf[pl.ds(HALO, tm), cs]
        acc = cur
        for d in range(1, w):
            acc = acc + pbuf[pl.ds(HALO - d, tm), cs]
        cnt = jnp.minimum(t + 1, w).astype(F32)
        pg = acc / cnt - cur
        pooled.append(pg)
        mixed_pre.append(_nn(pg.astype(_MXU), pw_ref[g]))
    return pooled, mixed_pre


def _lru_gates_head(hh, lbuf, start, tm, cw_ref, cb_ref, wrg_ref, brg_ref, wig_ref, big_ref, sp):
    cs = pl.ds(hh * HD, HD)
    xc = cb_ref[:, cs] + cw_ref[pl.ds(CONV - 1, 1), cs] * lbuf[pl.ds(HALO, tm), cs]
    for k in range(CONV - 1):
        xc = xc + cw_ref[pl.ds(k, 1), cs] * lbuf[pl.ds(HALO - (CONV - 1) + k, tm), cs]
    xcm = xc.astype(_MXU)
    r = _sigmoid(_nn(xcm, wrg_ref[hh]) + brg_ref[pl.ds(hh, 1), :])
    ig = _sigmoid(_nn(xcm, wig_ref[hh]) + big_ref[pl.ds(hh, 1), :])
    a = jnp.exp(-LRU_C * r * sp[:, hh * HD:(hh + 1) * HD])
    one_m = 1.0 - a * a
    live = jnp.logical_and(one_m > 0.0, jnp.logical_not(start))
    inv_mult = lax.rsqrt(jnp.where(live, one_m, 1.0))
    mult = jnp.where(live, one_m * inv_mult, jnp.where(start, 1.0, 0.0))
    return xc, r, ig, a, live, inv_mult, mult


def _seg_layout(tm):
    seg = tm // 8
    return seg, seg + 8


def _to_segments(dst_ref, hh, val, tm):
    seg, pitch = _seg_layout(tm)
    for s in range(8):
        dst_ref[hh, pl.ds(s * pitch, seg), :] = val[s * seg:(s + 1) * seg, :]


def _from_segments(src_ref, hh, tm):
    seg, pitch = _seg_layout(tm)
    return jnp.concatenate([src_ref[hh, pl.ds(s * pitch, seg), :] for s in range(8)], axis=0)


def _segment_scan(a_ref, b_ref, out_ref, hk, pk, carry_ref, tm, reverse):
    seg, pitch = _seg_layout(tm)
    row = lax.broadcasted_iota(jnp.int32, (8, HD), 0)
    order = range(seg - 1, -1, -1) if reverse else range(seg)
    for hh in range(HEADS):
        cs = pl.ds(hh * HD, HD)
        if reverse:
            a0 = a_ref[hh, pl.ds(0, 8, stride=pitch), :]
            a_wrap = jnp.where(row <= 6, pltpu.roll(a0, 7, 0), 1.0)
        hv = jnp.zeros((8, HD), F32)
        pv = jnp.ones((8, HD), F32)
        for k in order:
            if not reverse:
                av = a_ref[hh, pl.ds(k, 8, stride=pitch), :]
            elif k + 1 < seg:
                av = a_ref[hh, pl.ds(k + 1, 8, stride=pitch), :]
            else:
                av = a_wrap
            hv = av * hv + b_ref[hh, pl.ds(k, 8, stride=pitch), :]
            pv = av * pv
            hk[hh, pl.ds(8 * k, 8), :] = hv
            pk[hh, pl.ds(8 * k, 8), :] = pv
        for d in (1, 2, 4):
            if reverse:
                keep, sh = row < 8 - d, 8 - d
            else:
                keep, sh = row >= d, d
            hv = hv + pv * jnp.where(keep, pltpu.roll(hv, sh, 0), 0.0)
            pv = pv * jnp.where(keep, pltpu.roll(pv, sh, 0), 1.0)
        cin = carry_ref[:, cs]
        ends = hv + pv * cin
        if reverse:
            enter = jnp.where(row <= 6, pltpu.roll(ends, 7, 0), cin)
            carry_ref[:, cs] = jnp.broadcast_to((a0 * ends)[0:1, :], (8, HD))
        else:
            enter = jnp.where(row >= 1, pltpu.roll(ends, 1, 0), cin)
            carry_ref[:, cs] = jnp.broadcast_to(ends[7:8, :], (8, HD))
        for k in range(seg):
            out_ref[hh, pl.ds(k, 8, stride=pitch), :] = hk[hh, pl.ds(8 * k, 8), :] + pk[hh, pl.ds(8 * k, 8), :] * enter


def _mixer_fwd(z, x, gw, small, own_third):
    s = x.shape[0]
    tm = min(TM_SEQ, s)
    nt = s // tm
    rows3 = own_third.shape[0]
    (pool_w, pool_scale, conv_w, conv_b, w_rg, b_rg, w_ig, b_ig, lam, b_gate) = small

    def body(z_hbm, x_ref, gw_ref, own3_ref, pw_ref, ps_ref, cw_ref, cb_ref, wrg_ref, brg_ref, wig_ref, big_ref, lam_ref,
             bg_ref, h_ref, yl_ref, mg_ref, yp_ref, yr_ref, h1_ref, a_ref, r_ref, ig_ref, xc_ref, gw3_ref,
             pprojT, lru_w, wout_w, pbuf, lbuf, a_s, b_s, h_s, hk, pk, hcar, sems, stage3, send3, recv3, local3,
             zbuf, zsem):
        i = pl.program_id(0)
        t0 = i * tm
        gc = _Gather(own3_ref, gw3_ref, stage3, send3, recv3, local3)
        slot = i % 3

        def z_fetch(t, sl):
            r0 = t * tm if isinstance(t, int) else pl.multiple_of(t * tm, tm)
            return pltpu.make_async_copy(z_hbm.at[pl.ds(r0, tm), :], zbuf.at[sl], zsem.at[sl])

        @pl.when(i == 0)
        def _():
            for t in range(min(3, nt)):
                z_fetch(t, t).start()

        z_fetch(i, slot).wait()
        z_ref = zbuf.at[slot]

        @pl.when(i == 0)
        def _():
            gc.send_mine()
            _load_weights(gw_ref, [("pproj", pprojT, PW), ("lru", lru_w, D), ("wout", wout_w, D)], sems)
            pbuf[pl.ds(0, HALO), :] = jnp.zeros((HALO, PW), F32)
            lbuf[pl.ds(0, HALO), :] = jnp.zeros((HALO, D), F32)
            hcar[...] = jnp.zeros_like(hcar)

        @pl.when(i == nt // 2)
        def _():
            gc.pass_on((0, 1))

        @pl.when(i == (3 * nt) // 4)
        def _():
            gc.pass_on((2,))

        pbuf[pl.ds(HALO, tm), :] = z_ref[:, pl.ds(0, PW)]
        _, mixed_pre = _pool_tile(pbuf, t0, tm, pw_ref, ps_ref)
        mixed = jnp.concatenate(mixed_pre, axis=1) * ps_ref[...]
        y_pool = _nt(mixed.astype(_MXU), pprojT[...])
        pbuf[pl.ds(0, HALO), :] = pbuf[pl.ds(tm, HALO), :]

        lbuf[pl.ds(HALO, tm), :] = z_ref[:, pl.ds(PW, D)]
        sp, _ = _softplus_neg(lam_ref[...])
        start = (t0 + lax.broadcasted_iota(jnp.int32, (tm, HD), 0)) == 0
        for hh in range(HEADS):
            xc, r, ig, a, _, _, mult = _lru_gates_head(hh, lbuf, start, tm, cw_ref, cb_ref, wrg_ref, brg_ref,
                                                       wig_ref, big_ref, sp)
            _to_segments(a_s, hh, a, tm)
            _to_segments(b_s, hh, mult * ig * xc, tm)
            cs = pl.ds(hh * HD, HD)
            a_ref[:, cs] = a
            r_ref[:, cs] = r.astype(_MXU)
            ig_ref[:, cs] = ig.astype(_MXU)
            xc_ref[:, cs] = xc.astype(_MXU)
        lbuf[pl.ds(0, HALO), :] = lbuf[pl.ds(tm, HALO), :]
        _segment_scan(a_s, b_s, h_s, hk, pk, hcar, tm, reverse=False)
        for hh in range(HEADS):
            h_ref[:, pl.ds(hh * HD, HD)] = _from_segments(h_s, hh, tm)
        gel, _ = _gelu_and_grad(z_ref[:, pl.ds(PW + D, D)])
        yl = (h_ref[...] * gel).astype(_MXU)
        yl_ref[...] = yl
        y_lru = _nn(yl, lru_w[...])

        g0 = _sigmoid(z_ref[:, pl.ds(PW + 2 * D, D)] + bg_ref[pl.ds(0, 1), :])
        g1 = _sigmoid(z_ref[:, pl.ds(PW + 3 * D, D)] + bg_ref[pl.ds(1, 1), :])
        merged = (g0 * y_pool + g1 * y_lru).astype(_MXU)
        mg_ref[...] = merged
        yp_ref[...] = y_pool.astype(_MXU)
        yr_ref[...] = y_lru.astype(_MXU)
        h1_ref[...] = x_ref[...] + _nn(merged, wout_w[...])

        @pl.when(i + 3 < nt)
        def _():
            z_fetch(i + 3, slot).start()

        @pl.when(i == nt - 1)
        def _():
            gc.wait_sibling()
            gc.wait_passed((0, 1, 2))
            gc.finish_sends()

    tok = lambda w, dt: jax.ShapeDtypeStruct((s, w), dt)
    tspec = lambda w: pl.BlockSpec((tm, w), lambda i: (i, 0))
    full = lambda a: pl.BlockSpec(a.shape, lambda i: (0,) * a.ndim)
    any_spec = pl.BlockSpec(memory_space=pl.ANY)
    seg_buf = pltpu.VMEM((HEADS, 8 * _seg_layout(tm)[1], HD), F32)
    dma7 = pltpu.SemaphoreType.DMA((7,))
    return pl.pallas_call(
        body, name="mixer_fwd", grid=(nt,),
        out_shape=(tok(D, F32), tok(D, _MXU), tok(D, _MXU), tok(D, _MXU), tok(D, _MXU), tok(D, F32),
                   tok(D, F32), tok(D, _MXU), tok(D, _MXU), tok(D, _MXU),
                   jax.ShapeDtypeStruct((NDEV, rows3, D), own_third.dtype)),
        in_specs=[any_spec, tspec(D), any_spec, any_spec] + [full(a) for a in small],
        out_specs=(tspec(D),) * 10 + (any_spec,),
        scratch_shapes=[pltpu.VMEM((D, PW), _MXU), pltpu.VMEM((D, D), _MXU), pltpu.VMEM((D, D), _MXU),
                        pltpu.VMEM((tm + HALO, PW), F32), pltpu.VMEM((tm + HALO, D), F32),
                        seg_buf, seg_buf, seg_buf, pltpu.VMEM((HEADS, tm, HD), F32), pltpu.VMEM((HEADS, tm, HD), F32),
                        pltpu.VMEM((8, D), F32), pltpu.SemaphoreType.DMA((3 * NDEV,)),
                        pltpu.VMEM((rows3, D), own_third.dtype), dma7, dma7, pltpu.SemaphoreType.DMA,
                        pltpu.VMEM((3, tm, NIN), F32), pltpu.SemaphoreType.DMA((3,))],
        compiler_params=_cparams(1),
    )(z, x, gw, own_third, *small)


def _ffn_fwd(h1, g2, gw):
    s = h1.shape[0]
    tm = min(TM, s)

    def body(h1_ref, g2_ref, gw_ref, v_ref, gf_ref, uf_ref, h2_ref, wffnT, wffo, sems):
        @pl.when(pl.program_id(0) == 0)
        def _():
            _load_weights(gw_ref, [("wffn", wffnT, D), ("wffo", wffo, D)], sems)

        hv = h1_ref[...]
        inv = lax.rsqrt(jnp.mean(hv * hv, axis=-1, keepdims=True) + EPS)
        v = (hv * inv * g2_ref[...]).astype(_MXU)
        v_ref[...] = v
        acc = hv
        for c0, cn in FF_CHUNKS:
            cs = pl.ds(c0, cn)
            gf = _nt(v, wffnT[cs, :]).astype(_MXU)
            uf = _nt(v, wffnT[pl.ds(FF + c0, cn), :]).astype(_MXU)
            gf_ref[:, cs] = gf
            uf_ref[:, cs] = uf
            gf32 = gf.astype(F32)
            act = (gf32 * _sigmoid(gf32) * uf.astype(F32)).astype(_MXU)
            acc = acc + _nn(act, wffo[cs, :])
        h2_ref[...] = acc

    tspec = lambda w: pl.BlockSpec((tm, w), lambda i: (i, 0))
    return pl.pallas_call(
        body, name="ffn_fwd", grid=(s // tm,),
        out_shape=(jax.ShapeDtypeStruct((s, D), _MXU), jax.ShapeDtypeStruct((s, FF), _MXU),
                   jax.ShapeDtypeStruct((s, FF), _MXU), jax.ShapeDtypeStruct((s, D), F32)),
        in_specs=[tspec(D), pl.BlockSpec((1, D), lambda i: (0, 0)), pl.BlockSpec(memory_space=pl.ANY)],
        out_specs=(tspec(D), tspec(FF), tspec(FF), tspec(D)),
        scratch_shapes=[pltpu.VMEM((2 * FF, D), _MXU), pltpu.VMEM((FF, D), _MXU), pltpu.SemaphoreType.DMA((2 * NDEV,))],
        compiler_params=_cparams(1),
    )(h1, g2, gw)


def _rms_bwd(dy, xn, inv, g):
    dg = jnp.sum(dy * xn, axis=0, keepdims=True)
    dxn = dy * g
    dx = inv * (dxn - xn * jnp.mean(dxn * xn, axis=-1, keepdims=True))
    return dx, dg


def _ple_loss_fwd_bwd(h2, p, target, g3, gfin, gw):
    s = h2.shape[0]
    tm = min(TM, s)

    def body(h2_ref, p_ref, t_ref, g3_ref, gf_ref, gw_ref,
             dh2_ref, loss_ref, dg3_ref, dgf_ref, gwpg_ref, gple_ref, wpg, pleT, sems):
        i = pl.program_id(0)

        @pl.when(i == 0)
        def _():
            _load_weights(gw_ref, [("wpg", wpg, D), ("ple", pleT, PLE)], sems)
            for ref in (loss_ref, dg3_ref, dgf_ref, gwpg_ref, gple_ref):
                ref[...] = jnp.zeros_like(ref)

        hv = h2_ref[...]
        inv3 = lax.rsqrt(jnp.mean(hv * hv, axis=-1, keepdims=True) + EPS)
        xn3 = hv * inv3
        n3 = (xn3 * g3_ref[...]).astype(_MXU)
        pg = _sigmoid(_nn(n3, wpg[...]))
        pm = p_ref[...].astype(_MXU)
        e = _nt(pm, pleT[...])
        h3 = hv + pg * e
        invf = lax.rsqrt(jnp.mean(h3 * h3, axis=-1, keepdims=True) + EPS)
        xf = h3 * invf
        diff = xf * gf_ref[...] - t_ref[...]
        loss_ref[...] += jnp.sum(diff * diff) * (0.5 / D)
        dh3, dgf = _rms_bwd(diff * (1.0 / D), xf, invf, gf_ref[...])
        dgf_ref[...] += dgf
        gple_ref[:, pl.ds(0, PLE)] += _tn((dh3 * pg).astype(_MXU), pm)
        dpg = (dh3 * e * pg * (1.0 - pg)).astype(_MXU)
        gwpg_ref[...] += _tn(n3, dpg)
        dn3 = _nt(dpg, wpg[...])
        dx3, dg3 = _rms_bwd(dn3, xn3, inv3, g3_ref[...])
        dg3_ref[...] += dg3
        dh2_ref[...] = dh3 + dx3

    tspec = lambda w: pl.BlockSpec((tm, w), lambda i: (i, 0))
    vec = pl.BlockSpec((1, D), lambda i: (0, 0))
    mat = pl.BlockSpec((D, D), lambda i: (0, 0))
    return pl.pallas_call(
        body, name="ple_loss", grid=(s // tm,),
        out_shape=(jax.ShapeDtypeStruct((s, D), F32), jax.ShapeDtypeStruct((8, 128), F32),
                   jax.ShapeDtypeStruct((1, D), F32), jax.ShapeDtypeStruct((1, D), F32),
                   jax.ShapeDtypeStruct((D, D), F32), jax.ShapeDtypeStruct((D, D), F32)),
        in_specs=[tspec(D), tspec(PLE), tspec(D), vec, vec, pl.BlockSpec(memory_space=pl.ANY)],
        out_specs=(tspec(D), pl.BlockSpec((8, 128), lambda i: (0, 0)), vec, vec, mat, mat),
        scratch_shapes=[pltpu.VMEM((D, D), _MXU), pltpu.VMEM((D, PLE), _MXU), pltpu.SemaphoreType.DMA((2 * NDEV,))],
        compiler_params=_cparams(1),
    )(h2, p, target, g3, gfin, gw)


def _ffn_bwd_hidden(dh2, gf, uf, gw):
    s = dh2.shape[0]
    tm = min(TM, s)
    nt = s // tm

    def body(dh2_ref, gf_ref, uf_ref, gw_ref, dff_ref, gwo_ref, wffo, gacc, sems):
        i = pl.program_id(0)

        @pl.when(i == 0)
        def _():
            _load_weights(gw_ref, [("wffo", wffo, D)], sems)
            gacc[...] = jnp.zeros_like(gacc)

        dm = dh2_ref[...].astype(_MXU)
        dacts = [_nt(dm, wffo[pl.ds(c0, cn), :]) for c0, cn in FF_CHUNKS]
        for (c0, cn), dact in zip(FF_CHUNKS, dacts):
            cs = pl.ds(c0, cn)
            gfv = gf_ref[:, cs].astype(F32)
            ufv = uf_ref[:, cs].astype(F32)
            sg = _sigmoid(gfv)
            silu = gfv * sg
            gacc[cs, :] += _tn((silu * ufv).astype(_MXU), dm)
            dff_ref[:, cs] = (dact * ufv * (sg * (1.0 + gfv * (1.0 - sg)))).astype(_MXU)
            dff_ref[:, pl.ds(FF + c0, cn)] = (dact * silu).astype(_MXU)

        @pl.when(i == nt - 1)
        def _():
            pltpu.sync_copy(gacc, gwo_ref)

    tspec = lambda w: pl.BlockSpec((tm, w), lambda i: (i, 0))
    return pl.pallas_call(
        body, name="ffn_bwd_hidden", grid=(nt,),
        out_shape=(jax.ShapeDtypeStruct((s, 2 * FF), _MXU), jax.ShapeDtypeStruct((FF, D), F32)),
        in_specs=[tspec(D), tspec(FF), tspec(FF), pl.BlockSpec(memory_space=pl.ANY)],
        out_specs=(tspec(2 * FF), pl.BlockSpec(memory_space=pl.ANY)),
        scratch_shapes=[pltpu.VMEM((FF, D), _MXU), pltpu.VMEM((FF, D), F32), pltpu.SemaphoreType.DMA((NDEV,))],
        compiler_params=_cparams(1),
    )(dh2, gf, uf, gw)


def _proj_norm_bwd(dy, x, dres, g, gw, slab, width, name, lhs=None):
    s = x.shape[0]
    tm = min(TM, s)
    nl = 0 if lhs is None else 1

    def body(*refs):
        dy_ref, x_ref, dr_ref, g_ref = refs[:4]
        l_refs = refs[4:4 + nl]
        gw_ref, dx_ref, dg_ref = refs[4 + nl:7 + nl]
        gl_refs = refs[7 + nl:7 + 2 * nl]
        wT, sems = refs[7 + 2 * nl:]

        @pl.when(pl.program_id(0) == 0)
        def _():
            _load_weights(gw_ref, [(slab, wT, D)], sems)
            dg_ref[...] = jnp.zeros_like(dg_ref)
            for ref in gl_refs:
                ref[...] = jnp.zeros_like(ref)

        dv = _nn(dy_ref[...], wT[...])
        xv = x_ref[...]
        inv = lax.rsqrt(jnp.mean(xv * xv, axis=-1, keepdims=True) + EPS)
        dx, dg = _rms_bwd(dv, xv * inv, inv, g_ref[...])
        dg_ref[...] += dg
        dr = dr_ref[...]
        dx_ref[...] = dr + dx
        for l_ref, gl_ref in zip(l_refs, gl_refs):
            gl_ref[...] += _tn(l_ref[...], dr.astype(_MXU))

    tspec = lambda w: pl.BlockSpec((tm, w), lambda i: (i, 0))
    vec = pl.BlockSpec((1, D), lambda i: (0, 0))
    mat = pl.BlockSpec((D, D), lambda i: (0, 0))
    return pl.pallas_call(
        body, name=name, grid=(s // tm,),
        out_shape=(jax.ShapeDtypeStruct((s, D), F32), jax.ShapeDtypeStruct((1, D), F32))
        + (jax.ShapeDtypeStruct((D, D), F32),) * nl,
        in_specs=[tspec(width), tspec(D), tspec(D), vec] + [tspec(D)] * nl + [pl.BlockSpec(memory_space=pl.ANY)],
        out_specs=(tspec(D), vec) + (mat,) * nl,
        scratch_shapes=[pltpu.VMEM((width, D), _MXU), pltpu.SemaphoreType.DMA((NDEV,))],
        compiler_params=_cparams(1),
    )(dy, x, dres, g, *([] if lhs is None else [lhs]), gw)


def _mixer_bwd(dh1, z, h, y_pool, y_lru, saved, gw, small):
    s = dh1.shape[0]
    tm = min(TM_SEQ, s)
    nt = s // tm
    (pool_w, pool_scale, conv_w, conv_b, w_rg, b_rg, w_ig, b_ig, lam, b_gate) = small

    def body(dh1_ref, z_ref, zp_ref, h_ref, hp_ref, yp_ref, yr_ref, a_ref, r_ref, ig_ref, xc_ref, gw_ref,
             pw_ref, ps_ref, cw_ref, cb_ref, wrg_ref, brg_ref, wig_ref, big_ref, lam_ref, bg_ref,
             dz_ref, dyr_ref, dyp_ref, mx_ref,
             gbg_ref, glam_ref, gbrg_ref, gbig_ref, gcb_ref, gcw_ref, gps_ref, gpw_ref, gwrg_ref, gwig_ref,
             pprojT, lru_w, wout_w, pbuf, lbuf, hbuf, qbuf, xbuf, a_s, g_s, dh_s, hk, pk, dcar, sems):
        step = pl.program_id(0)
        i = nt - 1 - step
        t0 = i * tm

        @pl.when(step == 0)
        def _():
            _load_weights(gw_ref, [("pproj", pprojT, PW), ("lru", lru_w, D), ("wout", wout_w, D)], sems)
            for ref in (gbg_ref, glam_ref, gbrg_ref, gbig_ref, gcb_ref, gcw_ref, gps_ref, gpw_ref, gwrg_ref, gwig_ref):
                ref[...] = jnp.zeros_like(ref)
            qbuf[pl.ds(tm, HALO), :] = jnp.zeros((HALO, PW), F32)
            xbuf[pl.ds(tm, 8), :] = jnp.zeros((8, D), F32)
            dcar[...] = jnp.zeros_like(dcar)

        first = i == 0
        zprev = jnp.where(first, 0.0, zp_ref[...])
        hprev = jnp.where(first, 0.0, hp_ref[...])

        d_merged = _nt(dh1_ref[...].astype(_MXU), wout_w[...])

        g0 = _sigmoid(z_ref[:, pl.ds(PW + 2 * D, D)] + bg_ref[pl.ds(0, 1), :])
        g1 = _sigmoid(z_ref[:, pl.ds(PW + 3 * D, D)] + bg_ref[pl.ds(1, 1), :])
        dz0 = d_merged * yp_ref[...].astype(F32) * g0 * (1.0 - g0)
        dz1 = d_merged * yr_ref[...].astype(F32) * g1 * (1.0 - g1)
        dz_ref[:, pl.ds(PW + 2 * D, D)] = dz0.astype(_MXU)
        dz_ref[:, pl.ds(PW + 3 * D, D)] = dz1.astype(_MXU)
        gbg_ref[pl.ds(0, 1), :] += jnp.sum(dz0, axis=0, keepdims=True)
        gbg_ref[pl.ds(1, 1), :] += jnp.sum(dz1, axis=0, keepdims=True)
        d_ypool = (d_merged * g0).astype(_MXU)
        d_ylru = (d_merged * g1).astype(_MXU)
        dyp_ref[...] = d_ypool
        dyr_ref[...] = d_ylru

        d_yl = _nt(d_ylru, lru_w[...])
        gel, dgel = _gelu_and_grad(z_ref[:, pl.ds(PW + D, D)])
        dz_ref[:, pl.ds(PW + D, D)] = (d_yl * h_ref[...] * dgel).astype(_MXU)
        g_full = d_yl * gel
        lbuf[pl.ds(0, HALO), :] = zprev[:, PW:PW + D]
        lbuf[pl.ds(HALO, tm), :] = z_ref[:, pl.ds(PW, D)]
        hbuf[pl.ds(0, 8), :] = hprev
        hbuf[pl.ds(8, tm), :] = h_ref[...]
        sp, sneg = _softplus_neg(lam_ref[...])
        start = (t0 + lax.broadcasted_iota(jnp.int32, (tm, HD), 0)) == 0
        for hh in range(HEADS):
            cs = pl.ds(hh * HD, HD)
            _to_segments(a_s, hh, a_ref[:, cs], tm)
            _to_segments(g_s, hh, g_full[:, hh * HD:(hh + 1) * HD], tm)
        _segment_scan(a_s, g_s, dh_s, hk, pk, dcar, tm, reverse=True)
        for hh in range(HEADS):
            cs = pl.ds(hh * HD, HD)
            a = a_ref[:, cs]
            r = r_ref[:, cs].astype(F32)
            ig = ig_ref[:, cs].astype(F32)
            xc = xc_ref[:, cs].astype(F32)
            a2 = a * a
            one_m = 1.0 - a2
            live = jnp.logical_and(one_m > 0.0, jnp.logical_not(start))
            inv_mult = lax.rsqrt(jnp.where(live, one_m, 1.0))
            mult = jnp.where(live, one_m * inv_mult, jnp.where(start, 1.0, 0.0))
            dh = _from_segments(dh_s, hh, tm)
            d_mult = dh * ig * xc
            d_loga = dh * hbuf[pl.ds(7, tm), cs] * a - jnp.where(live, d_mult * a2 * inv_mult, 0.0)
            glam_ref[:, cs] += jnp.sum(d_loga * (LRU_C * r) * sneg[:, hh * HD:(hh + 1) * HD], axis=0, keepdims=True)
            d_rpre = d_loga * (-LRU_C * sp[:, hh * HD:(hh + 1) * HD]) * r * (1.0 - r)
            d_igpre = dh * mult * xc * ig * (1.0 - ig)
            gbrg_ref[pl.ds(hh, 1), :] += jnp.sum(d_rpre, axis=0, keepdims=True)
            gbig_ref[pl.ds(hh, 1), :] += jnp.sum(d_igpre, axis=0, keepdims=True)
            drm = d_rpre.astype(_MXU)
            dim = d_igpre.astype(_MXU)
            xcm = xc.astype(_MXU)
            gwrg_ref[hh] += _tn(xcm, drm)
            gwig_ref[hh] += _tn(xcm, dim)
            d_xc = dh * mult * ig + _nt(drm, wrg_ref[hh]) + _nt(dim, wig_ref[hh])
            gcb_ref[:, cs] += jnp.sum(d_xc, axis=0, keepdims=True)
            for k in range(CONV):
                gcw_ref[pl.ds(k, 1), cs] += jnp.sum(d_xc * lbuf[pl.ds(HALO - (CONV - 1) + k, tm), cs], axis=0,
                                                    keepdims=True)
            xbuf[pl.ds(0, tm), cs] = d_xc
        dzl = cw_ref[pl.ds(CONV - 1, 1), :] * xbuf[pl.ds(0, tm), :]
        for k in range(CONV - 1):
            dzl = dzl + cw_ref[pl.ds(k, 1), :] * xbuf[pl.ds(CONV - 1 - k, tm), :]
        dz_ref[:, pl.ds(PW, D)] = dzl.astype(_MXU)
        xbuf[pl.ds(tm, 8), :] = xbuf[pl.ds(0, 8), :]

        d_mixed = _nn(d_ypool, pprojT[...])
        pbuf[pl.ds(0, HALO), :] = zprev[:, 0:PW]
        pbuf[pl.ds(HALO, tm), :] = z_ref[:, pl.ds(0, PW)]
        pooled, mixed_pre = _pool_tile(pbuf, t0, tm, pw_ref, ps_ref)
        mp = jnp.concatenate(mixed_pre, axis=1)
        mx_ref[...] = (mp * ps_ref[...]).astype(_MXU)
        gps_ref[...] += jnp.sum(d_mixed * mp, axis=0, keepdims=True)
        d_mp = (d_mixed * ps_ref[...]).astype(_MXU)
        t = t0 + lax.broadcasted_iota(jnp.int32, (tm, GD), 0)
        d_pooled = []
        for g, w in enumerate(WINDOWS):
            dmg = d_mp[:, g * GD:(g + 1) * GD]
            gpw_ref[g] += _tn(pooled[g].astype(_MXU), dmg)
            dp = _nt(dmg, pw_ref[g])
            d_pooled.append(dp)
            qbuf[pl.ds(0, tm), pl.ds(g * GD, GD)] = dp / jnp.minimum(t + 1, w).astype(F32)
        for g, w in enumerate(WINDOWS):
            cs = pl.ds(g * GD, GD)
            acc = qbuf[pl.ds(0, tm), cs]
            for d in range(1, w):
                acc = acc + qbuf[pl.ds(d, tm), cs]
            dz_ref[:, cs] = (acc - d_pooled[g]).astype(_MXU)
        qbuf[pl.ds(tm, HALO), :] = qbuf[pl.ds(0, HALO), :]

    rev = lambda w: pl.BlockSpec((tm, w), lambda g: (nt - 1 - g, 0))
    prev = lambda rows, w: pl.BlockSpec((rows, w), lambda g: (jnp.maximum((nt - 1 - g) * (tm // rows) - 1, 0), 0))
    full = lambda a: pl.BlockSpec(a.shape, lambda g: (0,) * a.ndim)
    tok = lambda w, dt: jax.ShapeDtypeStruct((s, w), dt)
    acc_shapes = [(2, D), (1, D), (HEADS, HD), (HEADS, HD), (1, D), (CONV, D), (1, PW), (GROUPS, GD, GD),
                  (HEADS, HD, HD), (HEADS, HD, HD)]
    acc_specs = tuple(pl.BlockSpec(sh, lambda g, n=len(sh): (0,) * n) for sh in acc_shapes)
    seg_buf = pltpu.VMEM((HEADS, 8 * _seg_layout(tm)[1], HD), F32)
    a_in, r_in, ig_in, xc_in = saved
    return pl.pallas_call(
        body, name="mixer_bwd", grid=(nt,),
        out_shape=(tok(NIN, _MXU), tok(D, _MXU), tok(D, _MXU), tok(PW, _MXU))
        + tuple(jax.ShapeDtypeStruct(sh, F32) for sh in acc_shapes),
        in_specs=[rev(D), rev(NIN), prev(HALO, NIN), rev(D), prev(8, D), rev(D), rev(D), rev(D), rev(D), rev(D), rev(D),
                  pl.BlockSpec(memory_space=pl.ANY)] + [full(a) for a in small],
        out_specs=(rev(NIN), rev(D), rev(D), rev(PW)) + acc_specs,
        scratch_shapes=[pltpu.VMEM((D, PW), _MXU), pltpu.VMEM((D, D), _MXU), pltpu.VMEM((D, D), _MXU),
                        pltpu.VMEM((tm + HALO, PW), F32), pltpu.VMEM((tm + HALO, D), F32),
                        pltpu.VMEM((tm + 8, D), F32), pltpu.VMEM((tm + HALO, PW), F32), pltpu.VMEM((tm + 8, D), F32),
                        seg_buf, seg_buf, seg_buf, pltpu.VMEM((HEADS, tm, HD), F32), pltpu.VMEM((HEADS, tm, HD), F32),
                        pltpu.VMEM((8, D), F32), pltpu.SemaphoreType.DMA((3 * NDEV,))],
        compiler_params=_cparams(1),
    )(dh1, z, z, h, h, y_pool, y_lru, a_in, r_in, ig_in, xc_in, gw, *small)


def _split3(a):
    hi = a.astype(jnp.bfloat16).astype(F32)
    mid = (a - hi).astype(jnp.bfloat16).astype(F32)
    lo = (a - hi - mid).astype(jnp.bfloat16).astype(F32)
    return jnp.stack([hi, mid, lo])


def _small_pack(parts):
    flat = jnp.concatenate([a.reshape(-1) for a in parts])
    return jnp.pad(flat, (0, NDEV * SMALL_ROWS * D - flat.shape[0])).reshape(NDEV * SMALL_ROWS, D)


def _small_unpack(packed, shapes):
    flat = packed.reshape(-1)
    out, o = [], 0
    for sh in shapes:
        n = math.prod(sh)
        out.append(flat[o:o + n].reshape(sh))
        o += n
    return out


def kernel(x, p, norm1_g, w_in, b_gate, pool_w, pool_scale, pool_proj, conv_w, conv_b, w_rg, b_rg, w_ig, b_ig, lru_lambda, lru_proj, w_out, norm2_g, w_ffn_in, w_ffn_out, ple_norm_g, w_ple_gate, w_ple_proj, final_g, loss_target, m_norm1_g, m_w_in, m_b_gate, m_pool_w, m_pool_scale, m_pool_proj, m_conv_w, m_conv_b, m_w_rg, m_b_rg, m_w_ig, m_b_ig, m_lru_lambda, m_lru_proj, m_w_out, m_norm2_g, m_w_ffn_in, m_w_ffn_out, m_ple_norm_g, m_w_ple_gate, m_w_ple_proj, m_final_g, v_norm1_g, v_w_in, v_b_gate, v_pool_w, v_pool_scale, v_pool_proj, v_conv_w, v_conv_b, v_w_rg, v_b_rg, v_w_ig, v_b_ig, v_lru_lambda, v_lru_proj, v_w_out, v_norm2_g, v_w_ffn_in, v_w_ffn_out, v_ple_norm_g, v_w_ple_gate, v_w_ple_proj, v_final_g):
    axes = ("x", "y", "c")
    me = 4 * lax.axis_index("x") + 2 * lax.axis_index("y") + lax.axis_index("c")
    x2 = x[0]
    p2 = p[0, 0]
    tgt = loss_target[0]

    n_small = (CONV + 2) * 128
    small_terms = _split3(jnp.concatenate([conv_w[0].reshape(-1), b_gate[0].reshape(-1)]))
    small_rows = jnp.pad(small_terms, ((0, 16 - 3), (0, D - n_small)))
    own_first = jnp.concatenate([w_in[0].T.astype(_MXU), small_rows.astype(_MXU)], axis=0)
    own_second = jnp.concatenate([
        jnp.pad(pool_proj[0].T, ((0, 0), (0, D - PW))).astype(_MXU), lru_proj[0].astype(_MXU), w_out[0].astype(_MXU),
    ], axis=0)
    own_third = jnp.concatenate([
        w_ffn_in[0].T.astype(_MXU), jnp.pad(w_ple_proj[0].T, ((0, 0), (0, D - PLE))).astype(_MXU),
        w_ffn_out[0].astype(_MXU), w_ple_gate[0].astype(_MXU),
    ], axis=0)
    u, z, gw_first, gw = _inproj_fwd(x2, norm1_g, own_first, own_second)
    off = W_OFF["f32s"][0]
    st = gw_first[:, off:off + 3, :n_small].astype(F32)
    sf = st[:, 0] + st[:, 1] + st[:, 2]
    conv_w_full = sf[:, :CONV * 128].reshape(NDEV, CONV, 128).transpose(1, 0, 2).reshape(CONV, D)
    b_gate_full = sf[:, CONV * 128:].reshape(NDEV, 2, 128).transpose(1, 0, 2).reshape(2, D)

    small = (pool_w[0].astype(_MXU), pool_scale, conv_w_full, conv_b, w_rg[0].astype(_MXU), b_rg[0],
             w_ig[0].astype(_MXU), b_ig[0], lru_lambda, b_gate_full)

    h, yl, merged, y_pool, y_lru, h1, *saved, gw_third = _mixer_fwd(z, x2, gw, small, own_third)
    v, gf, uf, h2 = _ffn_fwd(h1, norm2_g, gw_third)

    dh2, loss_blk, g_ple_norm, g_final, part_wpg, part_ple = _ple_loss_fwd_bwd(h2, p2, tgt, ple_norm_g,
                                                                               final_g.reshape(1, D), gw_third)
    dff, part_wffo = _ffn_bwd_hidden(dh2, gf, uf, gw_third)
    dh1, g_norm2 = _proj_norm_bwd(dff, h1, dh2, norm2_g, gw_third, "wffn", 2 * FF, "ffn_bwd_in")
    (dz, d_ylru, d_ypool, mixed, g_bgate, g_lam, g_brg, g_big, g_convb, g_convw, g_pscale, g_poolw, g_wrg,
     g_wig) = _mixer_bwd(dh1, z, h, y_pool, y_lru, saved, gw, small)
    grad_x, g_norm1, part_wout = _proj_norm_bwd(dz, x2, dh1, norm1_g, gw_first, "win", NIN, "inproj_bwd", lhs=merged)

    small_shapes = [(1, D), (GROUPS, GD, GD), (1, PW), (1, D), (HEADS, HD, HD), (HEADS, HD), (HEADS, HD, HD),
                    (HEADS, HD), (1, D), (1, D), (1, D), (1, D), (2, D), (CONV, D), (1, 1)]
    small_part = _small_pack([g_norm1, g_poolw, g_pscale, g_convb, g_wrg, g_brg, g_wig, g_big, g_lam, g_norm2,
                              g_ple_norm, g_final, g_bgate, g_convw, loss_blk[0:1, 0:1]])
    riders = [_grad_matmul(yl, d_ylru, "grad_lru_proj"), part_wout, _grad_matmul(d_ypool, mixed, "grad_pool_proj")]
    rs_wffn = _grad_matmul_rs(dff, v, "grad_w_ffn_in", 704, extras=[part_wffo, part_wpg, part_ple], narrow=_MXU)
    rs_win = _grad_matmul_rs(dz, u, "grad_w_in", 576, extras=riders + [small_part], narrow=_MXU, tail=SMALL_ROWS)

    def reduced(parts, name):
        return [_sum_arrays([t_own, landed[0], landed[1], landed[2]], "rs_sum_" + name + str(n))
                for n, (t_own, landed) in enumerate(parts)]

    red_wffn, = reduced(rs_wffn, "wffn")
    red_win, red_small = reduced(rs_win, "win")
    g_w_in = red_win[:576].T
    g_w_ffn_in = red_wffn[:704].T
    g_w_ffn_out = red_wffn[704:1056]
    g_w_ple_gate = red_wffn[1056:1184]
    g_w_ple_proj = red_wffn[1184:1312, :PLE].T
    g_lru_proj, g_w_out = red_win[576:704], red_win[704:832]
    g_pool_proj = red_win[832:960, :PW].T
    small_red = _all_gather_small(red_small)
    (gs_norm1, gs_poolw, gs_pscale, gs_convb, gs_wrg, gs_brg, gs_wig, gs_big, gs_lam, gs_norm2, gs_ple_norm,
     gs_final, gs_bgate, gs_convw, loss_sum) = _small_unpack(small_red, small_shapes)
    loss = loss_sum[0, 0]
    g_b_gate = lax.dynamic_slice_in_dim(gs_bgate, me * 128, 128, axis=1)
    g_conv_w = lax.dynamic_slice_in_dim(gs_convw, me * 128, 128, axis=1)

    grads = {
        "norm1_g": gs_norm1, "w_in": g_w_in[None], "b_gate": g_b_gate[None], "pool_w": gs_poolw[None],
        "pool_scale": gs_pscale, "pool_proj": g_pool_proj[None], "conv_w": g_conv_w[None], "conv_b": gs_convb,
        "w_rg": gs_wrg[None], "b_rg": gs_brg[None], "w_ig": gs_wig[None], "b_ig": gs_big[None], "lru_lambda": gs_lam,
        "lru_proj": g_lru_proj[None], "w_out": g_w_out[None], "norm2_g": gs_norm2, "w_ffn_in": g_w_ffn_in[None],
        "w_ffn_out": g_w_ffn_out[None], "ple_norm_g": gs_ple_norm, "w_ple_gate": g_w_ple_gate[None],
        "w_ple_proj": g_w_ple_proj[None], "final_g": gs_final.reshape(D),
    }
    weights = dict(norm1_g=norm1_g, w_in=w_in, b_gate=b_gate, pool_w=pool_w, pool_scale=pool_scale, pool_proj=pool_proj,
                   conv_w=conv_w, conv_b=conv_b, w_rg=w_rg, b_rg=b_rg, w_ig=w_ig, b_ig=b_ig, lru_lambda=lru_lambda,
                   lru_proj=lru_proj, w_out=w_out, norm2_g=norm2_g, w_ffn_in=w_ffn_in, w_ffn_out=w_ffn_out,
                   ple_norm_g=ple_norm_g, w_ple_gate=w_ple_gate, w_ple_proj=w_ple_proj, final_g=final_g)
    moments_m = dict(norm1_g=m_norm1_g, w_in=m_w_in, b_gate=m_b_gate, pool_w=m_pool_w, pool_scale=m_pool_scale,
                     pool_proj=m_pool_proj, conv_w=m_conv_w, conv_b=m_conv_b, w_rg=m_w_rg, b_rg=m_b_rg, w_ig=m_w_ig,
                     b_ig=m_b_ig, lru_lambda=m_lru_lambda, lru_proj=m_lru_proj, w_out=m_w_out, norm2_g=m_norm2_g,
                     w_ffn_in=m_w_ffn_in, w_ffn_out=m_w_ffn_out, ple_norm_g=m_ple_norm_g, w_ple_gate=m_w_ple_gate,
                     w_ple_proj=m_w_ple_proj, final_g=m_final_g)
    moments_v = dict(norm1_g=v_norm1_g, w_in=v_w_in, b_gate=v_b_gate, pool_w=v_pool_w, pool_scale=v_pool_scale,
                     pool_proj=v_pool_proj, conv_w=v_conv_w, conv_b=v_conv_b, w_rg=v_w_rg, b_rg=v_b_rg, w_ig=v_w_ig,
                     b_ig=v_b_ig, lru_lambda=v_lru_lambda, lru_proj=v_lru_proj, w_out=v_w_out, norm2_g=v_norm2_g,
                     w_ffn_in=v_w_ffn_in, w_ffn_out=v_w_ffn_out, ple_norm_g=v_ple_norm_g, w_ple_gate=v_w_ple_gate,
                     w_ple_proj=v_w_ple_proj, final_g=v_final_g)
    names = list(weights)
    big = ("w_in", "w_ffn_in", "w_ffn_out", "lru_proj", "w_out", "w_ple_gate", "pool_proj", "w_ple_proj")
    slab_space = {"w_in": red_win[:576], "w_ffn_in": red_wffn[:704]}
    delta, new_m, new_v = {}, {}, {}
    for n in big:
        sh = weights[n].shape
        if n in slab_space:
            as2d = lambda a: a[0].T
            back = lambda a: a.T[None]
            g2d = slab_space[n]
        else:
            as2d = lambda a: a.reshape(sh[-2], sh[-1])
            back = lambda a: a.reshape(sh)
            g2d = as2d(grads[n])
        d_, m_, v_ = _adamw(as2d(weights[n]), g2d, as2d(moments_m[n]), as2d(moments_v[n]), "adamw_" + n)
        delta[n], new_m[n], new_v[n] = back(d_), back(m_), back(v_)
    rest = [n for n in names if n not in big]
    rest_shapes = [weights[n].shape for n in rest]
    packed = [_small_pack([src[n] for n in rest]) for src in (weights, grads, moments_m, moments_v)]
    d_, m_, v_ = _adamw(*packed, "adamw_small")
    for n, a, b_, c_ in zip(rest, _small_unpack(d_, rest_shapes), _small_unpack(m_, rest_shapes),
                            _small_unpack(v_, rest_shapes)):
        delta[n], new_m[n], new_v[n] = a, b_, c_

    return (loss, grad_x[None], *[grads[n] for n in names], *[delta[n] for n in names],
            *[new_m[n] for n in names], *[new_v[n] for n in names])
```
